```python
import jax, jax.numpy as jnp
from jax import lax
import numpy as np

D_MODEL = 1024
BATCH = 4
SEQ = 4096
DEPTH = 1

PLE_DIM = 256
RW_HEADS = 8
RW_HEAD = 64
RW_DIM = RW_HEADS * RW_HEAD
RW_DECAY_LORA = 64
RW_A_LORA = 64
RW_GATE_LORA = 128
RW_GN_EPS = 64e-5
HG_HEADS = 4
HG_EXPAND = 128
HG_HEAD_V = 128
HG_KDIM = HG_HEADS * HG_EXPAND
HG_VDIM = HG_HEADS * HG_HEAD_V
HG_CHUNK = 64
N_BRANCH = 2
C_R = 0
C_K = C_R + RW_DIM
C_V = C_K + RW_DIM
C_WD = C_V + RW_DIM
C_AD = C_WD + RW_DECAY_LORA
C_GD = C_AD + RW_A_LORA
C_RW_END = C_GD + RW_GATE_LORA
C_Q = C_RW_END
C_F = C_Q + HG_KDIM
C_I = C_F + HG_KDIM
C_OG = C_I + HG_VDIM
C_GATE = C_OG + HG_VDIM
N_IN = C_GATE + N_BRANCH * D_MODEL
RW_COLS = C_RW_END
N_GROUPS = 4
EXPERTS_PER_GROUP = 8
N_EXPERTS = N_GROUPS * EXPERTS_PER_GROUP
TOP_K = 2
D_EXPERT = 512
MOE_BLOCK = 128
LN_EPS = 1e-5
RMS_EPS = 1e-6
DEEPNORM_ALPHA = (2 * DEPTH) ** 0.25
DEEPNORM_BETA = (8 * DEPTH) ** -0.25

kernel_name = 'hybrid_rwkv7_hgrn2_hmoe_deepnorm'


def layer_norm(x, g, b):
    xf = x.astype(jnp.float32)
    mu = jnp.mean(xf, -1, keepdims=True)
    var = jnp.mean(jnp.square(xf - mu), -1, keepdims=True)
    return ((xf - mu) * lax.rsqrt(var + LN_EPS) * g + b).astype(x.dtype)


def token_shift(u):
    return jnp.pad(u, ((0, 0), (1, 0), (0, 0)))[:, :-1]


def rwkv7_scan(r, w, k, v, a_vec, b_vec):
    B, T, H, N = r.shape

    def step(S, inp):
        r_t, w_t, k_t, v_t, a_t, b_t = inp
        sa = jnp.einsum('bhij,bhj->bhi', S, a_t)
        S = S * w_t[:, :, None, :] + sa[..., None] * b_t[:, :, None, :] + v_t[..., None] * k_t[:, :, None, :]
        return S, jnp.einsum('bhij,bhj->bhi', S, r_t)

    seqs = tuple(jnp.moveaxis(t, 1, 0) for t in (r, w, k, v, a_vec, b_vec))
    _, y = lax.scan(step, jnp.zeros((B, H, N, N), jnp.float32), seqs)
    return jnp.moveaxis(y, 0, 1)


def rwkv7_branch(u, mu, w0, w_up, a0, a_up, g_up, k_k, k_a, r_k, gn_w, gn_b):
    B, T, _ = u.shape
    dt = u.dtype
    u = u + (token_shift(u) - u) * mu
    r = u[..., C_R:C_K]
    k = u[..., C_K:C_V]
    v = u[..., C_V:C_WD]
    xw = u[..., C_WD:C_AD]
    xa = u[..., C_AD:C_GD]
    xg = u[..., C_GD:C_RW_END]
    w_log = -jax.nn.softplus(-(w0 + jnp.tanh(xw) @ w_up)) - 0.5
    decay = jnp.exp(-jnp.exp(w_log.astype(jnp.float32)))
    a = jax.nn.sigmoid(a0 + xa @ a_up)
    g = jax.nn.sigmoid(xg) @ g_up
    hs = lambda t: t.astype(jnp.float32).reshape(B, T, RW_HEADS, RW_HEAD)
    kk = hs(k * k_k)
    kk = kk / jnp.maximum(jnp.sqrt(jnp.sum(kk * kk, -1, keepdims=True)), 1e-12)
    k = k * (1.0 + (a - 1.0) * k_a)
    rh, kh, vh, ah = hs(r), hs(k), hs(v), hs(a)
    y = rwkv7_scan(rh, decay.reshape(B, T, RW_HEADS, RW_HEAD), kh, vh, -kk, kk * ah)
    m = jnp.mean(y, -1, keepdims=True)
    var = jnp.mean(jnp.square(y - m), -1, keepdims=True)
    y = ((y - m) * lax.rsqrt(var + RW_GN_EPS)).reshape(B, T, RW_DIM) * gn_w + gn_b
    bonus = jnp.sum(rh * kh * r_k, -1, keepdims=True) * vh
    y = (y + bonus.reshape(B, T, RW_DIM)) * g
    return y.astype(dt)


def hgrn2_branch(u, lb, norm_w):
    B, T, _ = u.shape
    dt = u.dtype
    z = u[..., HG_KDIM:2 * HG_KDIM].astype(jnp.float32)
    f = lb + (1.0 - lb) * jax.nn.sigmoid(z)
    k = (1.0 - lb) * jax.nn.sigmoid(-z)
    logf = jnp.log(f)
    q = jax.nn.silu(u[..., :HG_KDIM].astype(jnp.float32))
    vin = u[..., 2 * HG_KDIM:2 * HG_KDIM + HG_VDIM].astype(jnp.float32)
    og = u[..., 2 * HG_KDIM + HG_VDIM:].astype(jnp.float32)
    nc = T // HG_CHUNK

    def chunks(t, d):
        return t.reshape(B, nc, HG_CHUNK, HG_HEADS, d).transpose(1, 0, 3, 2, 4)

    qc, kc = chunks(q, HG_EXPAND), chunks(k, HG_EXPAND)
    bc = jnp.cumsum(chunks(logf, HG_EXPAND), axis=3)
    vc = chunks(vin, HG_HEAD_V)
    mask = jnp.tril(jnp.ones((HG_CHUNK, HG_CHUNK), bool))[:, :, None]

    def step(S, inp):
        q_c, k_c, b_c, v_c = inp
        diff = b_c[:, :, :, None, :] - b_c[:, :, None, :, :]
        dec = jnp.exp(jnp.where(mask, diff, -jnp.inf))
        att = jnp.einsum('bhte,bhse,bhtse->bhts', q_c, k_c, dec)
        o = att @ v_c + jnp.einsum('bhte,bhev->bhtv', q_c * jnp.exp(b_c), S)
        b_last = b_c[:, :, -1:, :]
        S = jnp.exp(b_last[:, :, 0, :])[..., None] * S + jnp.einsum('bhse,bhsv->bhev', k_c * jnp.exp(b_last - b_c), v_c)
        return S, o

    S0 = jnp.zeros((B, HG_HEADS, HG_EXPAND, HG_HEAD_V), jnp.float32)
    _, o = lax.scan(step, S0, (qc, kc, bc, vc))
    o = o.transpose(1, 0, 3, 2, 4).reshape(B, T, HG_HEADS, HG_HEAD_V)
    o = o * lax.rsqrt(jnp.mean(o * o, -1, keepdims=True) + RMS_EPS)
    o = o.reshape(B, T, HG_VDIM) * norm_w * jax.nn.sigmoid(og)
    return o.astype(dt)


def hier_route(xf, wg, bg, we, be):
    pg = jax.nn.softmax((xf @ wg).astype(jnp.float32) + bg, axis=-1)
    pg_sel, gidx = lax.top_k(pg, 1)
    le = ((xf @ we).astype(jnp.float32) + be).reshape(-1, N_GROUPS, EXPERTS_PER_GROUP)
    le_sel = jnp.take_along_axis(le, gidx[:, :, None], axis=1)[:, 0]
    pe = jax.nn.softmax(le_sel, axis=-1)
    pv, eidx = lax.top_k(pe, TOP_K)
    pv = pv / jnp.sum(pv, -1, keepdims=True)
    return gidx * EXPERTS_PER_GROUP + eidx, pg_sel * pv


def moe_experts(xf, eid, ew, w1, w3, w2):
    N, D = xf.shape
    NA = N * TOP_K
    eid_f = eid.reshape(-1)
    ew_f = ew.reshape(-1)
    tok = jnp.repeat(jnp.arange(N, dtype=jnp.int32), TOP_K)
    order = jnp.argsort(eid_f)
    se = eid_f[order]
    counts = jnp.zeros((N_EXPERTS,), jnp.int32).at[eid_f].add(1)
    start = jnp.cumsum(counts) - counts
    padded = (counts + MOE_BLOCK - 1) // MOE_BLOCK * MOE_BLOCK
    pad_end = jnp.cumsum(padded)
    pad_start = pad_end - padded
    dest = pad_start[se] + (jnp.arange(NA, dtype=jnp.int32) - start[se])
    P = (NA + MOE_BLOCK - 1) // MOE_BLOCK * MOE_BLOCK + N_EXPERTS * MOE_BLOCK
    nblk = P // MOE_BLOCK
    tokbuf = jnp.full((P,), N, jnp.int32).at[dest].set(tok[order])
    wbuf = jnp.zeros((P,), jnp.float32).at[dest].set(ew_f[order])
    blk_e = jnp.minimum(jnp.searchsorted(pad_end, jnp.arange(nblk, dtype=jnp.int32) * MOE_BLOCK, side='right'), N_EXPERTS - 1)
    xpad = jnp.concatenate([xf, jnp.zeros((1, D), xf.dtype)], axis=0)
    xbuf = xpad[tokbuf].reshape(nblk, MOE_BLOCK, D)

    def block(args):
        xb, e = args
        h = jax.nn.silu(xb @ w1[e]) * (xb @ w3[e])
        return h @ w2[e]

    ybuf = lax.map(block, (xbuf, blk_e)).reshape(P, D) * wbuf[:, None].astype(xf.dtype)
    return jnp.zeros((N + 1, D), xf.dtype).at[tokbuf].add(ybuf)[:N]


def setup_inputs(seed: int = 0) -> dict:
    key = jax.random.key(seed)
    ks = jax.random.split(key, 40)
    f32 = jnp.float32
    nrm = lambda k, shape, s: jax.random.normal(k, shape, f32) * s
    D = D_MODEL
    return {
        'x': nrm(ks[0], (BATCH, SEQ, D), 1.0),
        'p': nrm(ks[1], (DEPTH, BATCH, SEQ, PLE_DIM), 1.0),
        'w_in': nrm(ks[2], (DEPTH, D, N_IN), D ** -0.5),
        'rw_mu': jax.random.uniform(ks[3], (DEPTH, RW_COLS), f32, 0.1, 0.9),
        'rw_w0': jax.random.uniform(ks[4], (DEPTH, RW_DIM), f32, -6.5, -1.5),
        'rw_w_up': nrm(ks[5], (DEPTH, RW_DECAY_LORA, RW_DIM), 0.1),
        'rw_a0': nrm(ks[6], (DEPTH, RW_DIM), 0.1),
        'rw_a_up': nrm(ks[7], (DEPTH, RW_A_LORA, RW_DIM), RW_A_LORA ** -0.5),
        'rw_g_up': nrm(ks[8], (DEPTH, RW_GATE_LORA, RW_DIM), RW_GATE_LORA ** -0.5),
        'rw_k_k': 0.85 + nrm(ks[9], (DEPTH, RW_DIM), 0.02),
        'rw_k_a': 1.0 + nrm(ks[10], (DEPTH, RW_DIM), 0.02),
        'rw_r_k': nrm(ks[11], (DEPTH, RW_HEADS, RW_HEAD), 0.1),
        'rw_gn_w': 1.0 + nrm(ks[12], (DEPTH, RW_DIM), 0.02),
        'rw_gn_b': nrm(ks[13], (DEPTH, RW_DIM), 0.02),
        'w_a_out': nrm(ks[14], (DEPTH, RW_DIM, D), RW_DIM ** -0.5),
        'hg_lb_logits': nrm(ks[15], (DEPTH + 1, HG_KDIM), 0.1),
        'hg_norm_w': 1.0 + nrm(ks[16], (DEPTH, HG_VDIM), 0.02),
        'w_b_out': nrm(ks[17], (DEPTH, HG_VDIM, D), HG_VDIM ** -0.5),
        'w_o': nrm(ks[18], (DEPTH, D, D), D ** -0.5 * DEEPNORM_BETA),
        'ln1_g': 1.0 + nrm(ks[19], (DEPTH, D), 0.02),
        'ln1_b': nrm(ks[20], (DEPTH, D), 0.02),
        'router_g_w': nrm(ks[21], (DEPTH, D, N_GROUPS), D ** -0.5),
        'router_g_b': nrm(ks[22], (DEPTH, N_GROUPS), 0.01),
        'router_e_w': nrm(ks[23], (DEPTH, D, N_EXPERTS), D ** -0.5),
        'router_e_b': nrm(ks[24], (DEPTH, N_EXPERTS), 0.01),
        'w1': nrm(ks[25], (DEPTH, N_EXPERTS, D, D_EXPERT), D ** -0.5),
        'w3': nrm(ks[26], (DEPTH, N_EXPERTS, D, D_EXPERT), D ** -0.5),
        'w2': nrm(ks[27], (DEPTH, N_EXPERTS, D_EXPERT, D), D_EXPERT ** -0.5 * DEEPNORM_BETA),
        'ln2_g': 1.0 + nrm(ks[28], (DEPTH, D), 0.02),
        'ln2_b': nrm(ks[29], (DEPTH, D), 0.02),
        'w_pe': nrm(ks[30], (DEPTH, PLE_DIM, D), PLE_DIM ** -0.5),
        'w_pg': nrm(ks[31], (DEPTH, D, D), D ** -0.5),
    }


def reference(x, p, w_in, rw_mu, rw_w0, rw_w_up, rw_a0, rw_a_up, rw_g_up, rw_k_k, rw_k_a, rw_r_k,
              rw_gn_w, rw_gn_b, w_a_out, hg_lb_logits, hg_norm_w, w_b_out, w_o, ln1_g, ln1_b,
              router_g_w, router_g_b, router_e_w, router_e_b, w1, w3, w2, ln2_g, ln2_b, w_pe, w_pg):
    B, T, D = x.shape
    lb_all = jnp.cumsum(jax.nn.softmax(hg_lb_logits.astype(jnp.float32), axis=0), axis=0)
    for i in range(DEPTH):
        proj = x @ w_in[i]
        y_a = rwkv7_branch(proj[..., :C_RW_END], rw_mu[i], rw_w0[i], rw_w_up[i], rw_a0[i], rw_a_up[i],
                           rw_g_up[i], rw_k_k[i], rw_k_a[i], rw_r_k[i], rw_gn_w[i], rw_gn_b[i])
        y_b = hgrn2_branch(proj[..., C_Q:C_GATE], lb_all[i].astype(x.dtype) if False else lb_all[i], hg_norm_w[i])
        gates = jax.nn.sigmoid(proj[..., C_GATE:])
        merged = gates[..., :D] * (y_a @ w_a_out[i]) + gates[..., D:] * (y_b @ w_b_out[i])
        x = layer_norm(DEEPNORM_ALPHA * x + merged @ w_o[i], ln1_g[i], ln1_b[i])
        xf = x.reshape(B * T, D)
        eid, ew = hier_route(xf, router_g_w[i], router_g_b[i], router_e_w[i], router_e_b[i])
        ffn = moe_experts(xf, eid, ew, w1[i], w3[i], w2[i]).reshape(B, T, D)
        x = layer_norm(DEEPNORM_ALPHA * x + ffn, ln2_g[i], ln2_b[i])
        x = x + jax.nn.sigmoid(x @ w_pg[i]) * (p[i] @ w_pe[i])
    return x
```

```python
import functools

import jax
import jax.numpy as jnp
from jax import lax
from jax.experimental import pallas as pl
from jax.experimental.pallas import tpu as pltpu

F32 = jnp.float32
BF16 = jnp.bfloat16
I32 = jnp.int32

NN = (((1,), (0,)), ((), ()))
NT = (((1,), (1,)), ((), ()))

RW_HEAD = 64
RW_DECAY_LORA = 64
RW_A_LORA = 64
RW_GATE_LORA = 128
RW_GN_EPS = 64e-5
HG_HEADS = 4
N_GROUPS = 4
EXPERTS_PER_GROUP = 8
LN_EPS = 1e-5
RMS_EPS = 1e-6

CHUNK = 64
SUB = 16
LANES = 128
VMEM_LIMIT = 56 * 1024 * 1024

TM_PROJ = 256
TB_REC = 256
TM_MERGE = 512
TB_ROUTE = 512
TD_DISPATCH = 512
TM_EXPERT = 256
TC_COMBINE = 256


def _dg(a, b, dn=NN):
    return lax.dot_general(a, b, dn, preferred_element_type=F32)


def _dot(a, b, dn=NN):
    return _dg(a.astype(BF16), b.astype(BF16), dn)


def _split(a):
    hi = a.astype(BF16)
    lo = (a - hi.astype(F32)).astype(BF16)
    return hi, lo


def _dot_hl(a, b_exact, dn=NN):
    hi, lo = _split(a)
    return _dg(hi, b_exact, dn) + _dg(lo, b_exact, dn)


def _dot3(a, b, dn=NN):
    ah, al = _split(a)
    bh, bl = _split(b)
    return _dg(ah, bh, dn) + (_dg(ah, bl, dn) + _dg(al, bh, dn))


def _cumsum_chunks(x, tri):
    h = x.astype(BF16)
    r1 = x - h.astype(F32)
    m = r1.astype(BF16)
    l = (r1 - m.astype(F32)).astype(BF16)
    return _dg(tri, h) + (_dg(tri, m) + _dg(tri, l))


def _layer_norm(h, g, b):
    mu = jnp.mean(h, axis=-1, keepdims=True)
    d = h - mu
    var = jnp.mean(d * d, axis=-1, keepdims=True)
    return d * lax.rsqrt(var + LN_EPS) * g + b


def _params(sem):
    return pltpu.CompilerParams(dimension_semantics=sem, vmem_limit_bytes=VMEM_LIMIT)


def _proj_kernel(x_ref, wr_ref, wh_ref, wg_ref, pr_ref, ph_ref, pg_ref):
    xb = x_ref[...].astype(BF16)
    pr_ref[...] = _dg(xb, wr_ref[...])
    ph_ref[...] = _dg(xb, wh_ref[...])
    pg_ref[...] = _dg(xb, wg_ref[...])


def _project(xf, w_rw, w_hg, w_gt):
    n, d = xf.shape
    tm = TM_PROJ
    full = lambda w: pl.BlockSpec(w.shape, lambda i: (0, 0))
    tile = lambda c: pl.BlockSpec((tm, c), lambda i: (i, 0))
    return pl.pallas_call(
        _proj_kernel,
        grid=(n // tm,),
        in_specs=[tile(d), full(w_rw), full(w_hg), full(w_gt)],
        out_specs=[tile(w_rw.shape[1]), tile(w_hg.shape[1]), tile(w_gt.shape[1])],
        out_shape=[jax.ShapeDtypeStruct((n, w.shape[1]), F32) for w in (w_rw, w_hg, w_gt)],
        compiler_params=_params(("parallel",)),
    )(xf, w_rw, w_hg, w_gt)


def _tri_inverse(a_bd, lvl_ref):
    n = a_bd.shape[0]
    r = lax.broadcasted_iota(I32, (n, n), 0)
    c = lax.broadcasted_iota(I32, (n, n), 1)
    t = jnp.where(r == c, 1.0, 0.0).astype(F32) + a_bd * lvl_ref[0]
    for lv in range(1, lvl_ref.shape[0]):
        am = a_bd * lvl_ref[lv]
        t = t + _dot(_dot(t, am), t)
    return t


def _rwkv_pair_chunk(r, k, v, av, bv, lw, lc, s, lvl_ref, bd):
    c = CHUNK
    lane = lax.broadcasted_iota(I32, (c, LANES), 1)
    row = lax.broadcasted_iota(I32, (c, LANES), 0)
    m1 = lane < RW_HEAD
    two = lambda x: jnp.concatenate([jnp.where(m1, x, 0.0), jnp.where(m1, 0.0, x)], axis=0)

    lx = lc - lw
    l_end = lc[c - 1:c]
    e_r = jnp.exp(l_end - lc)
    e_n = jnp.exp(-lc)
    at = av * jnp.exp(lx)
    rt = r * jnp.exp(lc)
    bt = bv * e_n
    kt = k * e_n
    bh = bv * e_r
    kh = k * e_r

    lhs = jnp.concatenate([at, rt], axis=0)
    rhs = jnp.concatenate([two(bt), two(kt)], axis=0)
    p = _dot(lhs, rhs, NT)
    scol = jnp.bitwise_and(lane, RW_HEAD - 1)
    strict = row > scol
    incl = row >= scol
    sab = jnp.where(strict, p[:c, :LANES], 0.0)
    sak = jnp.where(strict, p[:c, LANES:], 0.0)
    srb = jnp.where(incl, p[c:, :LANES], 0.0)
    srk = jnp.where(incl, p[c:, LANES:], 0.0)

    t = _tri_inverse(two(sab), lvl_ref)
    sv = _dot(jnp.concatenate([sak, srk], axis=0), two(v))
    akv = sv[:c]
    rkv = sv[c:]
    x = _dot(t, jnp.concatenate([two(at), two(akv)], axis=1))
    uk = x[:c, :LANES] + x[c:, :LANES]
    w = x[:c, LANES:] + x[c:, LANES:]
    g1 = _dot(jnp.concatenate([uk, rt], axis=0), s, NT)
    u = g1[:c] + w
    y = g1[c:] + rkv + _dot(srb, two(u))
    uv = jnp.concatenate([u, v], axis=0)
    bk = jnp.concatenate([bh, kh], axis=0)
    s_new = s * jnp.exp(l_end) + bd * _dot(uv.T, bk)
    return y, s_new


def _rwkv_kernel(u_ref, mu_ref, w0_ref, wup_ref, a0_ref, aup_ref, gup_ref, kk_ref, ka_ref, rk_ref,
                 gnw_ref, gnb_ref, tri_ref, gsum_ref, lvl_ref, bd_ref, ya_ref,
                 s_ref, prev_ref, r_s, k_s, v_s, a_s, b_s, lw_s, lc_s, g_s, y_s):
    tb = pl.program_id(1)

    @pl.when(tb == 0)
    def _():
        s_ref[...] = jnp.zeros_like(s_ref)
        prev_ref[...] = jnp.zeros_like(prev_ref)

    u = u_ref[0]
    nt = u.shape[0]
    dim = r_s.shape[1]
    rowid = lax.broadcasted_iota(I32, u.shape, 0)
    shifted = jnp.where(rowid == 0, prev_ref[...], pltpu.roll(u, 1, axis=0))
    prev_ref[...] = u[nt - 1:nt, :]
    um = u + (shifted - u) * mu_ref[...]

    r = um[:, 0:dim]
    k = um[:, dim:2 * dim]
    v = um[:, 2 * dim:3 * dim]
    xwa = um[:, 3 * dim:3 * dim + LANES]
    xg = um[:, 3 * dim + LANES:3 * dim + 2 * LANES]

    wpre = w0_ref[...] + _dot(jnp.tanh(xwa), wup_ref[...])
    z = -wpre
    softplus = jnp.maximum(z, 0.0) + jnp.log(1.0 + jnp.exp(-jnp.abs(z)))
    lw = -jnp.exp(-softplus - 0.5)
    a = jax.nn.sigmoid(a0_ref[...] + _dot(xwa, aup_ref[...]))
    g_s[...] = _dot(jax.nn.sigmoid(xg), gup_ref[...])
    kk = k * kk_ref[...]
    ss = _dot_hl(kk * kk, gsum_ref[...])
    kk = kk / jnp.maximum(jnp.sqrt(ss), 1e-12)
    r_s[...] = r
    k_s[...] = k * (1.0 + (a - 1.0) * ka_ref[...])
    v_s[...] = v
    a_s[...] = -kk
    b_s[...] = kk * a
    lw_s[...] = lw
    lc_s[...] = _cumsum_chunks(lw, tri_ref[...])
    bd = bd_ref[...]

    def chunk_body(ci, carry):
        off = pl.multiple_of(ci * CHUNK, CHUNK)
        rows = pl.ds(off, CHUNK)
        for p in range(dim // LANES):
            ls = slice(p * LANES, (p + 1) * LANES)
            y, s_new = _rwkv_pair_chunk(r_s[rows, ls], k_s[rows, ls], v_s[rows, ls], a_s[rows, ls],
                                        b_s[rows, ls], lw_s[rows, ls], lc_s[rows, ls], s_ref[p], lvl_ref, bd)
            s_ref[p] = s_new
            y_s[rows, ls] = y
        return carry

    lax.fori_loop(0, nt // CHUNK, chunk_body, 0)

    y = y_s[...]
    gsum = gsum_ref[...]
    inv_n = 1.0 / RW_HEAD
    m = _dot_hl(y, gsum) * inv_n
    d = y - m
    var = _dot_hl(d * d, gsum) * inv_n
    yn = d * lax.rsqrt(var + RW_GN_EPS) * gnw_ref[...] + gnb_ref[...]
    bonus = _dot_hl(r_s[...] * k_s[...] * rk_ref[...], gsum) * v_s[...]
    ya_ref[0] = ((yn + bonus) * g_s[...]).astype(BF16)


def _rwkv_branch(proj_rw, mu, w0, wup, a0, aup, gup, k_k, k_a, r_k, gn_w, gn_b):
    b, t, cols = proj_rw.shape
    dim = w0.shape[1]
    tb = TB_REC
    ii = jnp.arange(tb)
    tri = ((ii[:, None] // CHUNK == ii[None, :] // CHUNK) & (ii[:, None] >= ii[None, :])).astype(BF16)
    jj = jnp.arange(dim)
    gsum = (jj[:, None] // RW_HEAD == jj[None, :] // RW_HEAD).astype(BF16)
    rr = jnp.arange(LANES)[:, None]
    cc = jnp.arange(LANES)[None, :]
    lvls = []
    s = 1
    while s < CHUNK:
        lvls.append(((rr // (2 * s) == cc // (2 * s)) & ((rr // s) % 2 == 1) & ((cc // s) % 2 == 0)).astype(F32))
        s *= 2
    lvl = jnp.stack(lvls)
    bd = (rr // RW_HEAD == cc // RW_HEAD).astype(F32)
    zpad = lambda rows: jnp.zeros((rows, dim), F32)
    wup_p = jnp.concatenate([wup, zpad(LANES - wup.shape[0])], axis=0).astype(BF16)
    aup_p = jnp.concatenate([zpad(LANES - aup.shape[0]), aup], axis=0).astype(BF16)
    row2 = lambda a_: a_.reshape(1, -1)
    const = lambda a_: pl.BlockSpec(a_.shape, lambda bi, ti: (0,) * a_.ndim)
    args = [row2(mu), row2(w0), wup_p, row2(a0), aup_p, gup.astype(BF16), row2(k_k), row2(k_a), row2(r_k),
            row2(gn_w), row2(gn_b), tri, gsum, lvl, bd]
    sc = lambda: pltpu.VMEM((tb, dim), F32)
    return pl.pallas_call(
        _rwkv_kernel,
        grid=(b, t // tb),
        in_specs=[pl.BlockSpec((1, tb, cols), lambda bi, ti: (bi, ti, 0))] + [const(a_) for a_ in args],
        out_specs=pl.BlockSpec((1, tb, dim), lambda bi, ti: (bi, ti, 0)),
        out_shape=jax.ShapeDtypeStruct((b, t, dim), BF16),
        scratch_shapes=[pltpu.VMEM((dim // LANES, LANES, LANES), F32), pltpu.VMEM((1, cols), F32)]
                       + [sc() for _ in range(9)],
        compiler_params=_params(("arbitrary", "arbitrary")),
    )(proj_rw, *args)


def _hgrn_head_chunk(q, k, v, lf, bc, st):
    c = CHUNK
    outs = []
    for i in range(c // SUB):
        lo, hi = SUB * i, SUB * (i + 1)
        m = bc[lo:lo + 1] - lf[lo:lo + 1]
        qi = q[lo:hi] * jnp.exp(bc[lo:hi] - m)
        kj = k[:hi] * jnp.exp(m - bc[:hi])
        att = _dot(qi, kj, NT)
        tt = lax.broadcasted_iota(I32, (SUB, hi), 0) + lo
        s_ = lax.broadcasted_iota(I32, (SUB, hi), 1)
        att = jnp.where(s_ <= tt, att, 0.0)
        outs.append(_dot(att, v[:hi]))
    o = jnp.concatenate(outs, axis=0)
    o = o + _dot(q * jnp.exp(bc), st, NT)
    b_end = bc[c - 1:c]
    st_new = st * jnp.exp(b_end) + _dot(v.T, k * jnp.exp(b_end - bc))
    return o, st_new


def _hgrn_kernel(layer, u_ref, lbl_ref, nw_ref, tri_ref, gsum_ref, yb_ref,
                 st_ref, q_s, k_s, v_s, lf_s, bc_s, o_s):
    tb = pl.program_id(1)

    @pl.when(tb == 0)
    def _():
        st_ref[...] = jnp.zeros_like(st_ref)

    u = u_ref[0]
    nt = u.shape[0]
    dim = q_s.shape[1]
    lbl = lbl_ref[...]
    e = jnp.exp(lbl - jnp.max(lbl, axis=0, keepdims=True))
    lb = jnp.sum(e[0:layer + 1], axis=0, keepdims=True) / jnp.sum(e, axis=0, keepdims=True)
    zf = u[:, dim:2 * dim]
    f = lb + (1.0 - lb) * jax.nn.sigmoid(zf)
    qin = u[:, 0:dim]
    q_s[...] = qin * jax.nn.sigmoid(qin)
    k_s[...] = (1.0 - lb) * jax.nn.sigmoid(-zf)
    v_s[...] = u[:, 2 * dim:3 * dim]
    lf = jnp.log(f)
    lf_s[...] = lf
    bc_s[...] = _cumsum_chunks(lf, tri_ref[...])

    def chunk_body(ci, carry):
        off = pl.multiple_of(ci * CHUNK, CHUNK)
        rows = pl.ds(off, CHUNK)
        for h in range(dim // LANES):
            ls = slice(h * LANES, (h + 1) * LANES)
            o, st_new = _hgrn_head_chunk(q_s[rows, ls], k_s[rows, ls], v_s[rows, ls], lf_s[rows, ls],
                                         bc_s[rows, ls], st_ref[h])
            st_ref[h] = st_new
            o_s[rows, ls] = o
        return carry

    lax.fori_loop(0, nt // CHUNK, chunk_body, 0)

    o = o_s[...]
    ms = _dot_hl(o * o, gsum_ref[...]) * (1.0 / LANES)
    og = u[:, 3 * dim:4 * dim]
    yb_ref[0] = (o * lax.rsqrt(ms + RMS_EPS) * nw_ref[...] * jax.nn.sigmoid(og)).astype(BF16)


def _hgrn_branch(proj_hg, lb_logits, norm_w, layer):
    b, t, cols = proj_hg.shape
    dim = cols // 4
    tb = TB_REC
    ii = jnp.arange(tb)
    tri = ((ii[:, None] // CHUNK == ii[None, :] // CHUNK) & (ii[:, None] >= ii[None, :])).astype(BF16)
    jj = jnp.arange(dim)
    gsum = (jj[:, None] // LANES == jj[None, :] // LANES).astype(BF16)
    const = lambda a_: pl.BlockSpec(a_.shape, lambda bi, ti: (0,) * a_.ndim)
    args = [lb_logits, norm_w.reshape(1, -1), tri, gsum]
    sc = lambda: pltpu.VMEM((tb, dim), F32)
    return pl.pallas_call(
        functools.partial(_hgrn_kernel, layer),
        grid=(b, t // tb),
        in_specs=[pl.BlockSpec((1, tb, cols), lambda bi, ti: (bi, ti, 0))] + [const(a_) for a_ in args],
        out_specs=pl.BlockSpec((1, tb, dim), lambda bi, ti: (bi, ti, 0)),
        out_shape=jax.ShapeDtypeStruct((b, t, dim), BF16),
        scratch_shapes=[pltpu.VMEM((dim // LANES, LANES, LANES), F32)] + [sc() for _ in range(6)],
        compiler_params=_params(("arbitrary", "arbitrary")),
    )(proj_hg, *args)


def _merge_kernel(alpha, x_ref, ya_ref, yb_ref, pg_ref, wa_ref, wb_ref, wo_ref, g_ref, b_ref, wr_ref,
                  x1_ref, lt_ref):
    d = x_ref.shape[1]
    gates = jax.nn.sigmoid(pg_ref[...])
    merged = gates[:, :d] * _dg(ya_ref[...], wa_ref[...]) + gates[:, d:] * _dg(yb_ref[...], wb_ref[...])
    h = alpha * x_ref[...] + _dot(merged, wo_ref[...])
    x1 = _layer_norm(h, g_ref[...], b_ref[...])
    x1_ref[...] = x1
    lt_ref[...] = _dot3(wr_ref[...], x1, NT)


def _merge(xf, ya, yb, pgate, wa, wb, wo, g, bta, wr, alpha):
    n, d = xf.shape
    tm = TM_MERGE
    tile = lambda a_: pl.BlockSpec((tm, a_.shape[1]), lambda i: (i, 0))
    const = lambda a_: pl.BlockSpec(a_.shape, lambda i: (0, 0))
    return pl.pallas_call(
        functools.partial(_merge_kernel, alpha),
        grid=(n // tm,),
        in_specs=[tile(xf), tile(ya), tile(yb), tile(pgate), const(wa), const(wb), const(wo), const(g),
                  const(bta), const(wr)],
        out_specs=[pl.BlockSpec((tm, d), lambda i: (i, 0)), pl.BlockSpec((LANES, tm), lambda i: (0, i))],
        out_shape=[jax.ShapeDtypeStruct((n, d), F32), jax.ShapeDtypeStruct((LANES, n), F32)],
        compiler_params=_params(("parallel",)),
    )(xf, ya, yb, pgate, wa, wb, wo, g, bta, wr)


ROUTER_EXPERT_ROW = 8


def _route_kernel(lt_ref, bias_ref, upper_ref, route_ref, cnt_ref, carry_ref):
    @pl.when(pl.program_id(0) == 0)
    def _():
        carry_ref[...] = jnp.zeros_like(carry_ref)

    ne = N_GROUPS * EXPERTS_PER_GROUP
    lt = lt_ref[...] + bias_ref[...]
    nb = lt.shape[1]
    neg = -jnp.inf
    lg = lt[0:8]
    rg = lax.broadcasted_iota(I32, (8, nb), 0).astype(F32)
    lg = jnp.where(rg < N_GROUPS, lg, neg)
    mg = jnp.max(lg, axis=0, keepdims=True)
    gidx = jnp.min(jnp.where(lg == mg, rg, 1e9), axis=0, keepdims=True)
    pg_sel = 1.0 / jnp.sum(jnp.exp(lg - mg), axis=0, keepdims=True)

    le = lt[ROUTER_EXPERT_ROW:ROUTER_EXPERT_ROW + ne]
    re = lax.broadcasted_iota(I32, (ne, nb), 0).astype(F32)
    in_group = jnp.floor(re * (1.0 / EXPERTS_PER_GROUP)) == gidx
    l1 = jnp.where(in_group, le, neg)
    m1 = jnp.max(l1, axis=0, keepdims=True)
    i1 = jnp.min(jnp.where(l1 == m1, re, 1e9), axis=0, keepdims=True)
    l2 = jnp.where(re == i1, neg, l1)
    m2 = jnp.max(l2, axis=0, keepdims=True)
    i2 = jnp.min(jnp.where(l2 == m2, re, 1e9), axis=0, keepdims=True)
    e2 = jnp.exp(m2 - m1)
    w1 = pg_sel / (1.0 + e2)
    w2 = pg_sel * e2 / (1.0 + e2)

    sel1 = re == i1
    sel2 = re == i2
    onehot = jnp.where(sel1 | sel2, 1.0, 0.0)
    before = _dg(onehot.astype(BF16), upper_ref[...])
    tot = carry_ref[...] + before
    rank1 = jnp.sum(jnp.where(sel1, tot, 0.0), axis=0, keepdims=True)
    rank2 = jnp.sum(jnp.where(sel2, tot, 0.0), axis=0, keepdims=True)
    carry = carry_ref[...] + jnp.sum(onehot, axis=1, keepdims=True)
    carry_ref[...] = carry
    cnt_ref[...] = jnp.broadcast_to(carry, cnt_ref.shape)
    zero = jnp.zeros_like(w1)
    route_ref[...] = jnp.concatenate([i1, i2, rank1, rank2, w1, w2, zero, zero], axis=0)


def _route(lt, bias_col):
    n = lt.shape[1]
    tb = TB_ROUTE
    ne = N_GROUPS * EXPERTS_PER_GROUP
    ii = jnp.arange(tb)
    upper = (ii[:, None] < ii[None, :]).astype(BF16)
    return pl.pallas_call(
        _route_kernel,
        grid=(n // tb,),
        in_specs=[pl.BlockSpec((LANES, tb), lambda i: (0, i)),
                  pl.BlockSpec((LANES, 1), lambda i: (0, 0)),
                  pl.BlockSpec((tb, tb), lambda i: (0, 0))],
        out_specs=[pl.BlockSpec((8, tb), lambda i: (0, i)), pl.BlockSpec((ne, LANES), lambda i: (0, 0))],
        out_shape=[jax.ShapeDtypeStruct((8, n), F32), jax.ShapeDtypeStruct((ne, LANES), F32)],
        scratch_shapes=[pltpu.VMEM((ne, 1), F32)],
        compiler_params=_params(("arbitrary",)),
    )(lt, bias_col, upper)


def _dispatch_kernel(pos_ref, x_ref, init_ref, xbuf_ref, sem):
    del init_ref
    i = pl.program_id(0)
    td = x_ref.shape[0]
    n = pl.num_programs(0) * td

    def row_copy(j, slot):
        dst = pos_ref[slot * n + i * td + j]
        return pltpu.make_async_copy(x_ref.at[pl.ds(j, 1), :], xbuf_ref.at[pl.ds(dst, 1), :], sem)

    def issue(j, carry):
        row_copy(j, 0).start()
        row_copy(j, 1).start()
        return carry

    lax.fori_loop(0, td, issue, 0)
    for slot in range(2):
        pltpu.make_async_copy(x_ref, xbuf_ref.at[pl.ds(0, td), :], sem).wait()


def _dispatch(pos_flat, x1, rows):
    n, d = x1.shape
    td = TD_DISPATCH
    init = jnp.zeros((rows, d), F32)
    return pl.pallas_call(
        _dispatch_kernel,
        grid_spec=pltpu.PrefetchScalarGridSpec(
            num_scalar_prefetch=1,
            grid=(n // td,),
            in_specs=[pl.BlockSpec((td, d), lambda i, pos: (i, 0)), pl.BlockSpec(memory_space=pl.ANY)],
            out_specs=pl.BlockSpec(memory_space=pl.ANY),
            scratch_shapes=[pltpu.SemaphoreType.DMA(())],
        ),
        out_shape=jax.ShapeDtypeStruct((rows, d), F32),
        input_output_aliases={2: 0},
        compiler_params=_params(("arbitrary",)),
    )(pos_flat, x1, init)


def _expert_kernel(be_ref, nu_ref, x_ref, w1_ref, w3_ref, w2_ref, y_ref, w1b, w3b, w2b):
    i = pl.program_id(0)
    prev = be_ref[jnp.maximum(i - 1, 0)]

    @pl.when((i == 0) | (be_ref[i] != prev))
    def _():
        w1b[...] = w1_ref[0].astype(BF16)
        w3b[...] = w3_ref[0].astype(BF16)
        w2b[...] = w2_ref[0].astype(BF16)

    @pl.when(i < nu_ref[0])
    def _():
        xb = x_ref[...].astype(BF16)
        h1 = _dg(xb, w1b[...])
        h3 = _dg(xb, w3b[...])
        h = (h1 * jax.nn.sigmoid(h1)) * h3
        y_ref[...] = _dot(h, w2b[...])

    @pl.when(i >= nu_ref[0])
    def _():
        y_ref[...] = jnp.zeros_like(y_ref)


def _experts(blk_e, n_used, xbuf, w1, w3, w2):
    rows, d = xbuf.shape
    de = w1.shape[2]
    tm = TM_EXPERT
    row_map = lambda i, be, nu: (jnp.minimum(i, nu[0] - 1), 0)
    return pl.pallas_call(
        _expert_kernel,
        grid_spec=pltpu.PrefetchScalarGridSpec(
            num_scalar_prefetch=2,
            grid=(rows // tm,),
            in_specs=[pl.BlockSpec((tm, d), row_map),
                      pl.BlockSpec((1, d, de), lambda i, be, nu: (be[i], 0, 0)),
                      pl.BlockSpec((1, d, de), lambda i, be, nu: (be[i], 0, 0)),
                      pl.BlockSpec((1, de, d), lambda i, be, nu: (be[i], 0, 0))],
            out_specs=pl.BlockSpec((tm, d), lambda i, be, nu: (i, 0)),
            scratch_shapes=[pltpu.VMEM((d, de), BF16), pltpu.VMEM((d, de), BF16), pltpu.VMEM((de, d), BF16)],
        ),
        out_shape=jax.ShapeDtypeStruct((rows, d), F32),
        compiler_params=_params(("arbitrary",)),
    )(blk_e, n_used, xbuf, w1, w3, w2)


def _combine_kernel(alpha, pos_ref, x1_ref, ew_ref, p_ref, wpe_ref, wpg_ref, g_ref, b_ref, ybuf_ref,
                    out_ref, gbuf, sem):
    i = pl.program_id(0)
    nsteps = pl.num_programs(0)
    tc = x1_ref.shape[0]
    n = nsteps * tc

    def gather(step, buf):
        def body(j, carry):
            for slot in range(2):
                src = pos_ref[slot * n + step * tc + j]
                pltpu.make_async_copy(ybuf_ref.at[pl.ds(src, 1), :], gbuf.at[buf, slot, pl.ds(j, 1), :],
                                      sem.at[buf]).start()
            return carry
        lax.fori_loop(0, tc, body, 0)

    @pl.when(i == 0)
    def _():
        gather(0, 0)

    @pl.when(i + 1 < nsteps)
    def _():
        gather(i + 1, (i + 1) % 2)

    cur = i % 2
    for slot in range(2):
        pltpu.make_async_copy(ybuf_ref.at[pl.ds(0, tc), :], gbuf.at[cur, slot], sem.at[cur]).wait()

    ew = ew_ref[...]
    ffn = gbuf[cur, 0] * ew[:, 0:1] + gbuf[cur, 1] * ew[:, 1:2]
    x2 = _layer_norm(alpha * x1_ref[...] + ffn, g_ref[...], b_ref[...])
    gate = jax.nn.sigmoid(_dot(x2, wpg_ref[...]))
    out_ref[...] = x2 + gate * _dot(p_ref[...], wpe_ref[...])


def _combine(pos_flat, x1, ew_t, pf, wpe, wpg, g, bta, ybuf, alpha):
    n, d = x1.shape
    tc = TC_COMBINE
    tile = lambda a_: pl.BlockSpec((tc, a_.shape[1]), lambda i, pos: (i, 0))
    const = lambda a_: pl.BlockSpec(a_.shape, lambda i, pos: (0, 0))
    return pl.pallas_call(
        functools.partial(_combine_kernel, alpha),
        grid_spec=pltpu.PrefetchScalarGridSpec(
            num_scalar_prefetch=1,
            grid=(n // tc,),
            in_specs=[tile(x1), tile(ew_t), tile(pf), const(wpe), const(wpg), const(g), const(bta),
                      pl.BlockSpec(memory_space=pl.ANY)],
            out_specs=pl.BlockSpec((tc, d), lambda i, pos: (i, 0)),
            scratch_shapes=[pltpu.VMEM((2, 2, tc, d), F32), pltpu.SemaphoreType.DMA((2,))],
        ),
        out_shape=jax.ShapeDtypeStruct((n, d), F32),
        compiler_params=_params(("arbitrary",)),
    )(pos_flat, x1, ew_t, pf, wpe, wpg, g, bta, ybuf)


def _layer(x, p_i, w_in, rw_mu, rw_w0, rw_w_up, rw_a0, rw_a_up, rw_g_up, rw_k_k, rw_k_a, rw_r_k, rw_gn_w,
           rw_gn_b, w_a_out, hg_lb_logits, hg_norm_w, w_b_out, w_o, ln1_g, ln1_b, router_g_w, router_g_b,
           router_e_w, router_e_b, w1, w3, w2, ln2_g, ln2_b, w_pe, w_pg, alpha, layer):
    b, t, d = x.shape
    n = b * t
    rw_dim = rw_w0.shape[0]
    rw_cols = 3 * rw_dim + RW_DECAY_LORA + RW_A_LORA + RW_GATE_LORA
    hg_cols = 4 * hg_norm_w.shape[0]
    ne = N_GROUPS * EXPERTS_PER_GROUP
    row2 = lambda a_: a_.reshape(1, -1)
    xf = x.reshape(n, d)

    wb = w_in.astype(BF16)
    proj_rw, proj_hg, proj_gt = _project(xf, wb[:, :rw_cols], wb[:, rw_cols:rw_cols + hg_cols],
                                         wb[:, rw_cols + hg_cols:])
    ya = _rwkv_branch(proj_rw.reshape(b, t, rw_cols), rw_mu, row2(rw_w0), rw_w_up, rw_a0, rw_a_up, rw_g_up,
                      rw_k_k, rw_k_a, rw_r_k, rw_gn_w, rw_gn_b)
    yb = _hgrn_branch(proj_hg.reshape(b, t, hg_cols), hg_lb_logits, hg_norm_w, layer)

    wr = jnp.zeros((LANES, d), F32)
    wr = wr.at[:N_GROUPS].set(router_g_w.T).at[ROUTER_EXPERT_ROW:ROUTER_EXPERT_ROW + ne].set(router_e_w.T)
    bias = jnp.zeros((LANES,), F32)
    bias = bias.at[:N_GROUPS].set(router_g_b).at[ROUTER_EXPERT_ROW:ROUTER_EXPERT_ROW + ne].set(router_e_b)
    x1, lt = _merge(xf, ya.reshape(n, -1), yb.reshape(n, -1), proj_gt, w_a_out.astype(BF16),
                    w_b_out.astype(BF16), w_o.astype(BF16), row2(ln1_g), row2(ln1_b), wr, alpha)
    route, cnt = _route(lt, bias.reshape(LANES, 1))

    tm = TM_EXPERT
    nblk = (n * 2) // tm + ne
    counts = cnt[:, 0].astype(I32)
    nb = (counts + tm - 1) // tm
    bend = jnp.cumsum(nb)
    n_used = bend[-1:]
    blk = jnp.minimum(jnp.arange(nblk, dtype=I32), n_used[0] - 1)
    blk_e = jnp.minimum(jnp.searchsorted(bend, blk, side='right'), ne - 1).astype(I32)
    pad_start = (bend - nb) * tm
    eid = route[0:2].astype(I32)
    pos = (pad_start[eid] + route[2:4].astype(I32)).reshape(-1)

    xbuf = _dispatch(pos, x1, nblk * tm)
    ybuf = _experts(blk_e, n_used.astype(I32), xbuf, w1, w3, w2)
    out = _combine(pos, x1, route[4:6].T, p_i.reshape(n, -1), w_pe.astype(BF16), w_pg.astype(BF16),
                   row2(ln2_g), row2(ln2_b), ybuf, alpha)
    return out.reshape(b, t, d)


def kernel(x, p, w_in, rw_mu, rw_w0, rw_w_up, rw_a0, rw_a_up, rw_g_up, rw_k_k, rw_k_a, rw_r_k, rw_gn_w, rw_gn_b,
           w_a_out, hg_lb_logits, hg_norm_w, w_b_out, w_o, ln1_g, ln1_b, router_g_w, router_g_b, router_e_w,
           router_e_b, w1, w3, w2, ln2_g, ln2_b, w_pe, w_pg):
    depth = w_in.shape[0]
    alpha = (2 * depth) ** 0.25
    for i in range(depth):
        x = _layer(x, p[i], w_in[i], rw_mu[i], rw_w0[i], rw_w_up[i], rw_a0[i], rw_a_up[i], rw_g_up[i], rw_k_k[i],
                   rw_k_a[i], rw_r_k[i].reshape(-1), rw_gn_w[i], rw_gn_b[i], w_a_out[i], hg_lb_logits,
                   hg_norm_w[i], w_b_out[i], w_o[i], ln1_g[i], ln1_b[i],
                   router_g_w[i], router_g_b[i], router_e_w[i], router_e_b[i], w1[i], w3[i], w2[i], ln2_g[i],
                   ln2_b[i], w_pe[i], w_pg[i], alpha, i)
    return x
```

```python
import functools

import jax
import jax.numpy as jnp
from jax import lax
from jax.experimental import pallas as pl
from jax.experimental.pallas import tpu as pltpu

F32 = jnp.float32
BF16 = jnp.bfloat16
I32 = jnp.int32

NN = (((1,), (0,)), ((), ()))
NT = (((1,), (1,)), ((), ()))

RW_HEAD = 64
RW_DECAY_LORA = 64
RW_A_LORA = 64
RW_GATE_LORA = 128
RW_GN_EPS = 64e-5
HG_HEADS = 4
N_GROUPS = 4
EXPERTS_PER_GROUP = 8
LN_EPS = 1e-5
RMS_EPS = 1e-6

CHUNK = 64
SUB = 16
LANES = 128
VMEM_LIMIT = 56 * 1024 * 1024

TM_PROJ = 256
TB_REC = 256
TM_MERGE = 512
TB_ROUTE = 512
TD_DISPATCH = 512
TM_EXPERT = 256
TC_COMBINE = 256


def _dg(a, b, dn=NN):
    return lax.dot_general(a, b, dn, preferred_element_type=F32)


def _dot(a, b, dn=NN):
    return _dg(a.astype(BF16), b.astype(BF16), dn)


def _split(a):
    hi = a.astype(BF16)
    lo = (a - hi.astype(F32)).astype(BF16)
    return hi, lo


def _dot_hl(a, b_exact, dn=NN):
    hi, lo = _split(a)
    return _dg(hi, b_exact, dn) + _dg(lo, b_exact, dn)


def _dot3(a, b, dn=NN):
    ah, al = _split(a)
    bh, bl = _split(b)
    return _dg(ah, bh, dn) + (_dg(ah, bl, dn) + _dg(al, bh, dn))


def _cumsum_chunks(x, tri):
    h = x.astype(BF16)
    r1 = x - h.astype(F32)
    m = r1.astype(BF16)
    l = (r1 - m.astype(F32)).astype(BF16)
    return _dg(tri, h) + (_dg(tri, m) + _dg(tri, l))


def _layer_norm(h, g, b):
    mu = jnp.mean(h, axis=-1, keepdims=True)
    d = h - mu
    var = jnp.mean(d * d, axis=-1, keepdims=True)
    return d * lax.rsqrt(var + LN_EPS) * g + b


def _params(sem):
    return pltpu.CompilerParams(dimension_semantics=sem, vmem_limit_bytes=VMEM_LIMIT)


def _proj_kernel(x_ref, wr_ref, wh_ref, wg_ref, pr_ref, ph_ref, pg_ref):
    xb = x_ref[...].astype(BF16)
    pr_ref[...] = _dg(xb, wr_ref[...])
    ph_ref[...] = _dg(xb, wh_ref[...])
    pg_ref[...] = _dg(xb, wg_ref[...])


def _project(xf, w_rw, w_hg, w_gt):
    n, d = xf.shape
    tm = TM_PROJ
    full = lambda w: pl.BlockSpec(w.shape, lambda i: (0, 0))
    tile = lambda c: pl.BlockSpec((tm, c), lambda i: (i, 0))
    return pl.pallas_call(
        _proj_kernel,
        grid=(n // tm,),
        in_specs=[tile(d), full(w_rw), full(w_hg), full(w_gt)],
        out_specs=[tile(w_rw.shape[1]), tile(w_hg.shape[1]), tile(w_gt.shape[1])],
        out_shape=[jax.ShapeDtypeStruct((n, w.shape[1]), F32) for w in (w_rw, w_hg, w_gt)],
        compiler_params=_params(("parallel",)),
    )(xf, w_rw, w_hg, w_gt)


def _rwkv_chunk(ins, lvl_ref, bd):
    c = CHUNK
    lane = lax.broadcasted_iota(I32, (c, LANES), 1)
    row = lax.broadcasted_iota(I32, (c, LANES), 0)
    m1 = lane < RW_HEAD
    two = lambda x: jnp.concatenate([jnp.where(m1, x, 0.0), jnp.where(m1, 0.0, x)], axis=0)
    scol = jnp.bitwise_and(lane, RW_HEAD - 1)
    strict = row > scol
    incl = row >= scol
    r2 = lax.broadcasted_iota(I32, (LANES, LANES), 0)
    c2 = lax.broadcasted_iota(I32, (LANES, LANES), 1)
    eye = jnp.where(r2 == c2, 1.0, 0.0).astype(F32)
    each = lambda f, *ls: [f(*xs) for xs in zip(*ls)]

    def prep(r, k, v, av, bv, lw, lc, s):
        l_end = lc[c - 1:c]
        e_r = jnp.exp(l_end - lc)
        e_n = jnp.exp(-lc)
        return dict(at=av * jnp.exp(lc - lw), rt=r * jnp.exp(lc), bt=bv * e_n, kt=k * e_n, bh=bv * e_r,
                    kh=k * e_r, pc=jnp.exp(l_end), v=v, s=s)

    q = [prep(*xs) for xs in ins]
    p = [_dot(jnp.concatenate([d["at"], d["rt"]], axis=0),
              jnp.concatenate([two(d["bt"]), two(d["kt"])], axis=0), NT) for d in q]
    sab = [jnp.where(strict, x[:c, :LANES], 0.0) for x in p]
    sak = [jnp.where(strict, x[:c, LANES:], 0.0) for x in p]
    srb = [jnp.where(incl, x[c:, :LANES], 0.0) for x in p]
    srk = [jnp.where(incl, x[c:, LANES:], 0.0) for x in p]
    sv = each(lambda ak, rk, d: _dot(jnp.concatenate([ak, rk], axis=0), two(d["v"])), sak, srk, q)

    a_bd = [two(x) for x in sab]
    t = [eye + a * lvl_ref[0] for a in a_bd]
    for lv in range(1, lvl_ref.shape[0]):
        ta = each(lambda t_, a: _dot(t_, a * lvl_ref[lv]), t, a_bd)
        t = each(lambda t_, ta_: t_ + _dot(ta_, t_), t, ta)

    x = each(lambda t_, d, sv_: _dot(t_, jnp.concatenate([two(d["at"]), two(sv_[:c])], axis=1)), t, q, sv)
    uk = [x_[:c, :LANES] + x_[c:, :LANES] for x_ in x]
    w = [x_[:c, LANES:] + x_[c:, LANES:] for x_ in x]
    g1 = each(lambda uk_, d: _dot(jnp.concatenate([uk_, d["rt"]], axis=0), d["s"], NT), uk, q)
    u = each(lambda g, w_: g[:c] + w_, g1, w)
    y = each(lambda g, sv_, srb_, u_: g[c:] + sv_[c:] + _dot(srb_, two(u_)), g1, sv, srb, u)
    upd = each(lambda u_, d: _dot(jnp.concatenate([u_, d["v"]], axis=0).T,
                                  jnp.concatenate([d["bh"], d["kh"]], axis=0)), u, q)
    s_new = each(lambda d, up: d["s"] * d["pc"] + bd * up, q, upd)
    return list(zip(y, s_new))


def _rwkv_kernel(u_ref, mu_ref, w0_ref, wup_ref, a0_ref, aup_ref, gup_ref, kk_ref, ka_ref, rk_ref,
                 gnw_ref, gnb_ref, tri_ref, gsum_ref, lvl_ref, bd_ref, ya_ref,
                 s_ref, prev_ref, r_s, k_s, v_s, a_s, b_s, lw_s, lc_s, g_s, y_s):
    tb = pl.program_id(1)

    @pl.when(tb == 0)
    def _():
        s_ref[...] = jnp.zeros_like(s_ref)
        prev_ref[...] = jnp.zeros_like(prev_ref)

    u = u_ref[0]
    nt = u.shape[0]
    dim = r_s.shape[1]
    rowid = lax.broadcasted_iota(I32, u.shape, 0)
    shifted = jnp.where(rowid == 0, prev_ref[...], pltpu.roll(u, 1, axis=0))
    prev_ref[...] = u[nt - 1:nt, :]
    um = u + (shifted - u) * mu_ref[...]

    r = um[:, 0:dim]
    k = um[:, dim:2 * dim]
    v = um[:, 2 * dim:3 * dim]
    xwa = um[:, 3 * dim:3 * dim + LANES]
    xg = um[:, 3 * dim + LANES:3 * dim + 2 * LANES]

    wpre = w0_ref[...] + _dot(jnp.tanh(xwa), wup_ref[...])
    z = -wpre
    softplus = jnp.maximum(z, 0.0) + jnp.log(1.0 + jnp.exp(-jnp.abs(z)))
    lw = -jnp.exp(-softplus - 0.5)
    a = jax.nn.sigmoid(a0_ref[...] + _dot(xwa, aup_ref[...]))
    g_s[...] = _dot(jax.nn.sigmoid(xg), gup_ref[...])
    kk = k * kk_ref[...]
    ss = _dot_hl(kk * kk, gsum_ref[...])
    kk = kk / jnp.maximum(jnp.sqrt(ss), 1e-12)
    r_s[...] = r
    k_s[...] = k * (1.0 + (a - 1.0) * ka_ref[...])
    v_s[...] = v
    a_s[...] = -kk
    b_s[...] = kk * a
    lw_s[...] = lw
    lc_s[...] = _cumsum_chunks(lw, tri_ref[...])
    bd = bd_ref[...]

    def chunk_body(ci, carry):
        off = pl.multiple_of(ci * CHUNK, CHUNK)
        rows = pl.ds(off, CHUNK)
        npair = dim // LANES
        lanes = [slice(p * LANES, (p + 1) * LANES) for p in range(npair)]
        ins = [(r_s[rows, ls], k_s[rows, ls], v_s[rows, ls], a_s[rows, ls], b_s[rows, ls], lw_s[rows, ls],
                lc_s[rows, ls], s_ref[p]) for p, ls in enumerate(lanes)]
        outs = _rwkv_chunk(ins, lvl_ref, bd)
        for p, ls in enumerate(lanes):
            y, s_new = outs[p]
            s_ref[p] = s_new
            y_s[rows, ls] = y
        return carry

    lax.fori_loop(0, nt // CHUNK, chunk_body, 0)

    y = y_s[...]
    gsum = gsum_ref[...]
    inv_n = 1.0 / RW_HEAD
    m = _dot_hl(y, gsum) * inv_n
    d = y - m
    var = _dot_hl(d * d, gsum) * inv_n
    yn = d * lax.rsqrt(var + RW_GN_EPS) * gnw_ref[...] + gnb_ref[...]
    bonus = _dot_hl(r_s[...] * k_s[...] * rk_ref[...], gsum) * v_s[...]
    ya_ref[0] = ((yn + bonus) * g_s[...]).astype(BF16)


def _rwkv_branch(proj_rw, mu, w0, wup, a0, aup, gup, k_k, k_a, r_k, gn_w, gn_b):
    b, t, cols = proj_rw.shape
    dim = w0.shape[1]
    tb = TB_REC
    ii = jnp.arange(tb)
    tri = ((ii[:, None] // CHUNK == ii[None, :] // CHUNK) & (ii[:, None] >= ii[None, :])).astype(BF16)
    jj = jnp.arange(dim)
    gsum = (jj[:, None] // RW_HEAD == jj[None, :] // RW_HEAD).astype(BF16)
    rr = jnp.arange(LANES)[:, None]
    cc = jnp.arange(LANES)[None, :]
    lvls = []
    s = 1
    while s < CHUNK:
        lvls.append(((rr // (2 * s) == cc // (2 * s)) & ((rr // s) % 2 == 1) & ((cc // s) % 2 == 0)).astype(F32))
        s *= 2
    lvl = jnp.stack(lvls)
    bd = (rr // RW_HEAD == cc // RW_HEAD).astype(F32)
    zpad = lambda rows: jnp.zeros((rows, dim), F32)
    wup_p = jnp.concatenate([wup, zpad(LANES - wup.shape[0])], axis=0).astype(BF16)
    aup_p = jnp.concatenate([zpad(LANES - aup.shape[0]), aup], axis=0).astype(BF16)
    row2 = lambda a_: a_.reshape(1, -1)
    const = lambda a_: pl.BlockSpec(a_.shape, lambda bi, ti: (0,) * a_.ndim)
    args = [row2(mu), row2(w0), wup_p, row2(a0), aup_p, gup.astype(BF16), row2(k_k), row2(k_a), row2(r_k),
            row2(gn_w), row2(gn_b), tri, gsum, lvl, bd]
    sc = lambda: pltpu.VMEM((tb, dim), F32)
    return pl.pallas_call(
        _rwkv_kernel,
        grid=(b, t // tb),
        in_specs=[pl.BlockSpec((1, tb, cols), lambda bi, ti: (bi, ti, 0))] + [const(a_) for a_ in args],
        out_specs=pl.BlockSpec((1, tb, dim), lambda bi, ti: (bi, ti, 0)),
        out_shape=jax.ShapeDtypeStruct((b, t, dim), BF16),
        scratch_shapes=[pltpu.VMEM((dim // LANES, LANES, LANES), F32), pltpu.VMEM((1, cols), F32)]
                       + [sc() for _ in range(9)],
        compiler_params=_params(("arbitrary", "arbitrary")),
    )(proj_rw, *args)


def _hgrn_chunk(ins):
    c = CHUNK
    subs = [(SUB * i, SUB * (i + 1)) for i in range(c // SUB)]

    def scores(q, k, lf, bc, lo, hi):
        m = bc[lo:lo + 1] - lf[lo:lo + 1]
        att = _dot(q[lo:hi] * jnp.exp(bc[lo:hi] - m), k[:hi] * jnp.exp(m - bc[:hi]), NT)
        tt = lax.broadcasted_iota(I32, (SUB, hi), 0) + lo
        s_ = lax.broadcasted_iota(I32, (SUB, hi), 1)
        return jnp.where(s_ <= tt, att, 0.0)

    att = [[scores(q, k, lf, bc, lo, hi) for lo, hi in subs] for q, k, v, lf, bc, st in ins]
    inter = [_dot(q * jnp.exp(bc), st, NT) for q, k, v, lf, bc, st in ins]
    upd = [_dot(v.T, k * jnp.exp(bc[c - 1:c] - bc)) for q, k, v, lf, bc, st in ins]
    intra = [[_dot(a, x[2][:hi]) for a, (lo, hi) in zip(arow, subs)] for arow, x in zip(att, ins)]
    o = [jnp.concatenate(rows, axis=0) + it for rows, it in zip(intra, inter)]
    st_new = [x[5] * jnp.exp(x[4][c - 1:c]) + up for x, up in zip(ins, upd)]
    return list(zip(o, st_new))


def _hgrn_kernel(layer, u_ref, lbl_ref, nw_ref, tri_ref, gsum_ref, yb_ref,
                 st_ref, q_s, k_s, v_s, lf_s, bc_s, o_s):
    tb = pl.program_id(1)

    @pl.when(tb == 0)
    def _():
        st_ref[...] = jnp.zeros_like(st_ref)

    u = u_ref[0]
    nt = u.shape[0]
    dim = q_s.shape[1]
    lbl = lbl_ref[...]
    e = jnp.exp(lbl - jnp.max(lbl, axis=0, keepdims=True))
    lb = jnp.sum(e[0:layer + 1], axis=0, keepdims=True) / jnp.sum(e, axis=0, keepdims=True)
    zf = u[:, dim:2 * dim]
    sig = jax.nn.sigmoid(zf)
    f = lb + (1.0 - lb) * sig
    qin = u[:, 0:dim]
    q_s[...] = qin * jax.nn.sigmoid(qin)
    k_s[...] = (1.0 - lb) * (1.0 - sig)
    v_s[...] = u[:, 2 * dim:3 * dim]
    lf = jnp.log(f)
    lf_s[...] = lf
    bc_s[...] = _cumsum_chunks(lf, tri_ref[...])

    def chunk_body(ci, carry):
        off = pl.multiple_of(ci * CHUNK, CHUNK)
        rows = pl.ds(off, CHUNK)
        lanes = [slice(h * LANES, (h + 1) * LANES) for h in range(dim // LANES)]
        ins = [(q_s[rows, ls], k_s[rows, ls], v_s[rows, ls], lf_s[rows, ls], bc_s[rows, ls], st_ref[h])
               for h, ls in enumerate(lanes)]
        outs = _hgrn_chunk(ins)
        for h, ls in enumerate(lanes):
            o, st_new = outs[h]
            st_ref[h] = st_new
            o_s[rows, ls] = o
        return carry

    lax.fori_loop(0, nt // CHUNK, chunk_body, 0)

    o = o_s[...]
    ms = _dot_hl(o * o, gsum_ref[...]) * (1.0 / LANES)
    og = u[:, 3 * dim:4 * dim]
    yb_ref[0] = (o * lax.rsqrt(ms + RMS_EPS) * nw_ref[...] * jax.nn.sigmoid(og)).astype(BF16)


def _hgrn_branch(proj_hg, lb_logits, norm_w, layer):
    b, t, cols = proj_hg.shape
    dim = cols // 4
    tb = TB_REC
    ii = jnp.arange(tb)
    tri = ((ii[:, None] // CHUNK == ii[None, :] // CHUNK) & (ii[:, None] >= ii[None, :])).astype(BF16)
    jj = jnp.arange(dim)
    gsum = (jj[:, None] // LANES == jj[None, :] // LANES).astype(BF16)
    const = lambda a_: pl.BlockSpec(a_.shape, lambda bi, ti: (0,) * a_.ndim)
    args = [lb_logits, norm_w.reshape(1, -1), tri, gsum]
    sc = lambda: pltpu.VMEM((tb, dim), F32)
    return pl.pallas_call(
        functools.partial(_hgrn_kernel, layer),
        grid=(b, t // tb),
        in_specs=[pl.BlockSpec((1, tb, cols), lambda bi, ti: (bi, ti, 0))] + [const(a_) for a_ in args],
        out_specs=pl.BlockSpec((1, tb, dim), lambda bi, ti: (bi, ti, 0)),
        out_shape=jax.ShapeDtypeStruct((b, t, dim), BF16),
        scratch_shapes=[pltpu.VMEM((dim // LANES, LANES, LANES), F32)] + [sc() for _ in range(6)],
        compiler_params=_params(("arbitrary", "arbitrary")),
    )(proj_hg, *args)


def _merge_kernel(alpha, x_ref, ya_ref, yb_ref, pg_ref, wa_ref, wb_ref, wo_ref, g_ref, b_ref, wr_ref,
                  x1_ref, lt_ref):
    d = x_ref.shape[1]
    gates = jax.nn.sigmoid(pg_ref[...])
    merged = gates[:, :d] * _dg(ya_ref[...], wa_ref[...]) + gates[:, d:] * _dg(yb_ref[...], wb_ref[...])
    h = alpha * x_ref[...] + _dot(merged, wo_ref[...])
    x1 = _layer_norm(h, g_ref[...], b_ref[...])
    x1_ref[...] = x1
    lt_ref[...] = _dot3(wr_ref[...], x1, NT)


def _merge(xf, ya, yb, pgate, wa, wb, wo, g, bta, wr, alpha):
    n, d = xf.shape
    tm = TM_MERGE
    tile = lambda a_: pl.BlockSpec((tm, a_.shape[1]), lambda i: (i, 0))
    const = lambda a_: pl.BlockSpec(a_.shape, lambda i: (0, 0))
    return pl.pallas_call(
        functools.partial(_merge_kernel, alpha),
        grid=(n // tm,),
        in_specs=[tile(xf), tile(ya), tile(yb), tile(pgate), const(wa), const(wb), const(wo), const(g),
                  const(bta), const(wr)],
        out_specs=[pl.BlockSpec((tm, d), lambda i: (i, 0)), pl.BlockSpec((LANES, tm), lambda i: (0, i))],
        out_shape=[jax.ShapeDtypeStruct((n, d), F32), jax.ShapeDtypeStruct((LANES, n), F32)],
        compiler_params=_params(("parallel",)),
    )(xf, ya, yb, pgate, wa, wb, wo, g, bta, wr)


ROUTER_EXPERT_ROW = 8


def _route_kernel(lt_ref, bias_ref, upper_ref, route_ref, cnt_ref, carry_ref):
    @pl.when(pl.program_id(0) == 0)
    def _():
        carry_ref[...] = jnp.zeros_like(carry_ref)

    ne = N_GROUPS * EXPERTS_PER_GROUP
    lt = lt_ref[...] + bias_ref[...]
    nb = lt.shape[1]
    neg = -jnp.inf
    lg = lt[0:8]
    rg = lax.broadcasted_iota(I32, (8, nb), 0).astype(F32)
    lg = jnp.where(rg < N_GROUPS, lg, neg)
    mg = jnp.max(lg, axis=0, keepdims=True)
    gidx = jnp.min(jnp.where(lg == mg, rg, 1e9), axis=0, keepdims=True)
    pg_sel = 1.0 / jnp.sum(jnp.exp(lg - mg), axis=0, keepdims=True)

    le = lt[ROUTER_EXPERT_ROW:ROUTER_EXPERT_ROW + ne]
    re = lax.broadcasted_iota(I32, (ne, nb), 0).astype(F32)
    in_group = jnp.floor(re * (1.0 / EXPERTS_PER_GROUP)) == gidx
    l1 = jnp.where(in_group, le, neg)
    m1 = jnp.max(l1, axis=0, keepdims=True)
    i1 = jnp.min(jnp.where(l1 == m1, re, 1e9), axis=0, keepdims=True)
    l2 = jnp.where(re == i1, neg, l1)
    m2 = jnp.max(l2, axis=0, keepdims=True)
    i2 = jnp.min(jnp.where(l2 == m2, re, 1e9), axis=0, keepdims=True)
    e2 = jnp.exp(m2 - m1)
    w1 = pg_sel / (1.0 + e2)
    w2 = pg_sel * e2 / (1.0 + e2)

    sel1 = re == i1
    sel2 = re == i2
    onehot = jnp.where(sel1 | sel2, 1.0, 0.0)
    before = _dg(onehot.astype(BF16), upper_ref[...])
    tot = carry_ref[...] + before
    rank1 = jnp.sum(jnp.where(sel1, tot, 0.0), axis=0, keepdims=True)
    rank2 = jnp.sum(jnp.where(sel2, tot, 0.0), axis=0, keepdims=True)
    carry = carry_ref[...] + jnp.sum(onehot, axis=1, keepdims=True)
    carry_ref[...] = carry
    cnt_ref[...] = jnp.broadcast_to(carry, cnt_ref.shape)
    zero = jnp.zeros_like(w1)
    route_ref[...] = jnp.concatenate([i1, i2, rank1, rank2, w1, w2, zero, zero], axis=0)


def _route(lt, bias_col):
    n = lt.shape[1]
    tb = TB_ROUTE
    ne = N_GROUPS * EXPERTS_PER_GROUP
    ii = jnp.arange(tb)
    upper = (ii[:, None] < ii[None, :]).astype(BF16)
    return pl.pallas_call(
        _route_kernel,
        grid=(n // tb,),
        in_specs=[pl.BlockSpec((LANES, tb), lambda i: (0, i)),
                  pl.BlockSpec((LANES, 1), lambda i: (0, 0)),
                  pl.BlockSpec((tb, tb), lambda i: (0, 0))],
        out_specs=[pl.BlockSpec((8, tb), lambda i: (0, i)), pl.BlockSpec((ne, LANES), lambda i: (0, 0))],
        out_shape=[jax.ShapeDtypeStruct((8, n), F32), jax.ShapeDtypeStruct((ne, LANES), F32)],
        scratch_shapes=[pltpu.VMEM((ne, 1), F32)],
        compiler_params=_params(("arbitrary",)),
    )(lt, bias_col, upper)


TAB_LANES = 2 * LANES


def _finalize_kernel(tm, route_ref, cnt_ref, lower_ref, pos_ref, tab_ref):
    ne = cnt_ref.shape[0]
    cnt = cnt_ref[...]
    nb = jnp.floor((cnt + (tm - 1)) * (1.0 / tm))
    bstart = _dg(lower_ref[...], nb.astype(BF16))
    bend = bstart + nb
    pad_start = bstart[:, 0:1] * tm
    route = route_ref[...]
    nt = route.shape[1]
    re = lax.broadcasted_iota(I32, (ne, nt), 0).astype(F32)
    rows = []
    for slot in range(2):
        start = jnp.sum(jnp.where(re == route[slot:slot + 1], pad_start, 0.0), axis=0, keepdims=True)
        rows.append(start + route[2 + slot:3 + slot])
    pos_ref[...] = jnp.concatenate(rows, axis=0).astype(I32)

    nl = tab_ref.shape[1]
    n_used = jnp.max(bend[:, 0:1], axis=0, keepdims=True)
    blk = jnp.minimum(lax.broadcasted_iota(I32, (ne, nl), 1).astype(F32), n_used - 1.0)
    blk_e = jnp.sum(jnp.where(bend[:, 0:1] <= blk, 1.0, 0.0), axis=0, keepdims=True)
    blk_e = jnp.minimum(blk_e, ne - 1.0)
    diag = (lax.broadcasted_iota(I32, (ne, nl), 0) == lax.broadcasted_iota(I32, (ne, nl), 1))
    to_lanes = lambda col: jnp.sum(jnp.where(diag, col, 0.0), axis=0, keepdims=True)
    pad_lo = to_lanes(pad_start + cnt[:, 0:1])
    pad_hi = to_lanes(bend[:, 0:1] * tm)
    zero = jnp.zeros((1, nl), F32)
    tab_ref[...] = jnp.concatenate([blk_e, jnp.broadcast_to(n_used, (1, nl)), pad_lo, pad_hi,
                                    zero, zero, zero, zero], axis=0).astype(I32)


def _finalize(route, cnt, tm):
    n = route.shape[1]
    ne = cnt.shape[0]
    tb = TB_ROUTE
    ii = jnp.arange(ne)
    lower = (ii[:, None] > ii[None, :]).astype(BF16)
    return pl.pallas_call(
        functools.partial(_finalize_kernel, tm),
        grid=(n // tb,),
        in_specs=[pl.BlockSpec((8, tb), lambda i: (0, i)), pl.BlockSpec(cnt.shape, lambda i: (0, 0)),
                  pl.BlockSpec((ne, ne), lambda i: (0, 0))],
        out_specs=[pl.BlockSpec((2, tb), lambda i: (0, i)), pl.BlockSpec((8, TAB_LANES), lambda i: (0, 0))],
        out_shape=[jax.ShapeDtypeStruct((2, n), I32), jax.ShapeDtypeStruct((8, TAB_LANES), I32)],
        compiler_params=_params(("arbitrary",)),
    )(route, cnt, lower)


def _dispatch_kernel(tm, pos_ref, plo_ref, phi_ref, nu_ref, x_ref, xbuf_ref, zblk, sem, zsem):
    i = pl.program_id(0)
    td = x_ref.shape[0]
    n = pl.num_programs(0) * td
    ne = plo_ref.shape[0]
    nblk = xbuf_ref.shape[0] // tm

    def row_copy(j, slot):
        dst = pos_ref[slot * n + i * td + j]
        return pltpu.make_async_copy(x_ref.at[pl.ds(j, 1), :], xbuf_ref.at[pl.ds(dst, 1), :], sem)

    def issue(j, carry):
        row_copy(j, 0).start()
        row_copy(j, 1).start()
        return carry

    lax.fori_loop(0, td, issue, 0)

    zero_row = lambda j: pltpu.make_async_copy(zblk.at[pl.ds(0, 1), :], xbuf_ref.at[pl.ds(j, 1), :], zsem)
    zero_blk = lambda b: pltpu.make_async_copy(
        zblk, xbuf_ref.at[pl.ds(pl.multiple_of(b * tm, tm), tm), :], zsem)

    def for_pad_rows(fn):
        def per_expert(e, carry):
            def body(j, c2):
                fn(zero_row(j))
                return c2
            return lax.fori_loop(plo_ref[e], phi_ref[e], body, carry)
        lax.fori_loop(0, ne, per_expert, 0)

        def per_blk(b, carry):
            fn(zero_blk(b))
            return carry
        lax.fori_loop(nu_ref[0], nblk, per_blk, 0)

    @pl.when(i == 0)
    def _():
        zblk[...] = jnp.zeros_like(zblk)
        for_pad_rows(lambda cp: cp.start())
        for_pad_rows(lambda cp: cp.wait())

    for slot in range(2):
        pltpu.make_async_copy(x_ref, xbuf_ref.at[pl.ds(0, td), :], sem).wait()


def _dispatch(pos_flat, pad_lo, pad_hi, n_used, x1, rows, tm):
    n, d = x1.shape
    td = TD_DISPATCH
    return pl.pallas_call(
        functools.partial(_dispatch_kernel, tm),
        grid_spec=pltpu.PrefetchScalarGridSpec(
            num_scalar_prefetch=4,
            grid=(n // td,),
            in_specs=[pl.BlockSpec((td, d), lambda i, *_: (i, 0))],
            out_specs=pl.BlockSpec(memory_space=pl.ANY),
            scratch_shapes=[pltpu.VMEM((tm, d), F32), pltpu.SemaphoreType.DMA(()), pltpu.SemaphoreType.DMA(())],
        ),
        out_shape=jax.ShapeDtypeStruct((rows, d), F32),
        compiler_params=_params(("arbitrary",)),
    )(pos_flat, pad_lo, pad_hi, n_used, x1)


def _expert_kernel(be_ref, nu_ref, x_ref, w1_ref, w3_ref, w2_ref, y_ref, w1b, w3b, w2b):
    i = pl.program_id(0)
    prev = be_ref[jnp.maximum(i - 1, 0)]

    @pl.when((i == 0) | (be_ref[i] != prev))
    def _():
        w1b[...] = w1_ref[0].astype(BF16)
        w3b[...] = w3_ref[0].astype(BF16)
        w2b[...] = w2_ref[0].astype(BF16)

    @pl.when(i < nu_ref[0])
    def _():
        xb = x_ref[...].astype(BF16)
        h1 = _dg(xb, w1b[...])
        h3 = _dg(xb, w3b[...])
        h = (h1 * jax.nn.sigmoid(h1)) * h3
        y_ref[...] = _dot(h, w2b[...])

    @pl.when(i >= nu_ref[0])
    def _():
        y_ref[...] = jnp.zeros_like(y_ref)


def _experts(blk_e, n_used, xbuf, w1, w3, w2):
    rows, d = xbuf.shape
    de = w1.shape[2]
    tm = TM_EXPERT
    row_map = lambda i, be, nu: (jnp.minimum(i, nu[0] - 1), 0)
    return pl.pallas_call(
        _expert_kernel,
        grid_spec=pltpu.PrefetchScalarGridSpec(
            num_scalar_prefetch=2,
            grid=(rows // tm,),
            in_specs=[pl.BlockSpec((tm, d), row_map),
                      pl.BlockSpec((1, d, de), lambda i, be, nu: (be[i], 0, 0)),
                      pl.BlockSpec((1, d, de), lambda i, be, nu: (be[i], 0, 0)),
                      pl.BlockSpec((1, de, d), lambda i, be, nu: (be[i], 0, 0))],
            out_specs=pl.BlockSpec((tm, d), lambda i, be, nu: (i, 0)),
            scratch_shapes=[pltpu.VMEM((d, de), BF16), pltpu.VMEM((d, de), BF16), pltpu.VMEM((de, d), BF16)],
        ),
        out_shape=jax.ShapeDtypeStruct((rows, d), F32),
        compiler_params=_params(("arbitrary",)),
    )(blk_e, n_used, xbuf, w1, w3, w2)


def _combine_kernel(alpha, pos_ref, x1_ref, ew_ref, p_ref, wpe_ref, wpg_ref, g_ref, b_ref, ybuf_ref,
                    out_ref, gbuf, sem):
    i = pl.program_id(0)
    nsteps = pl.num_programs(0)
    tc = x1_ref.shape[0]
    n = nsteps * tc

    def gather(step, buf):
        def body(j, carry):
            for slot in range(2):
                src = pos_ref[slot * n + step * tc + j]
                pltpu.make_async_copy(ybuf_ref.at[pl.ds(src, 1), :], gbuf.at[buf, slot, pl.ds(j, 1), :],
                                      sem.at[buf]).start()
            return carry
        lax.fori_loop(0, tc, body, 0)

    @pl.when(i == 0)
    def _():
        gather(0, 0)

    @pl.when(i + 1 < nsteps)
    def _():
        gather(i + 1, (i + 1) % 2)

    cur = i % 2
    for slot in range(2):
        pltpu.make_async_copy(ybuf_ref.at[pl.ds(0, tc), :], gbuf.at[cur, slot], sem.at[cur]).wait()

    ew = ew_ref[...]
    ffn = gbuf[cur, 0] * ew[:, 0:1] + gbuf[cur, 1] * ew[:, 1:2]
    x2 = _layer_norm(alpha * x1_ref[...] + ffn, g_ref[...], b_ref[...])
    gate = jax.nn.sigmoid(_dot(x2, wpg_ref[...]))
    out_ref[...] = x2 + gate * _dot(p_ref[...], wpe_ref[...])


def _combine(pos_flat, x1, ew_t, pf, wpe, wpg, g, bta, ybuf, alpha):
    n, d = x1.shape
    tc = TC_COMBINE
    tile = lambda a_: pl.BlockSpec((tc, a_.shape[1]), lambda i, pos: (i, 0))
    const = lambda a_: pl.BlockSpec(a_.shape, lambda i, pos: (0, 0))
    return pl.pallas_call(
        functools.partial(_combine_kernel, alpha),
        grid_spec=pltpu.PrefetchScalarGridSpec(
            num_scalar_prefetch=1,
            grid=(n // tc,),
            in_specs=[tile(x1), tile(ew_t), tile(pf), const(wpe), const(wpg), const(g), const(bta),
                      pl.BlockSpec(memory_space=pl.ANY)],
            out_specs=pl.BlockSpec((tc, d), lambda i, pos: (i, 0)),
            scratch_shapes=[pltpu.VMEM((2, 2, tc, d), F32), pltpu.SemaphoreType.DMA((2,))],
        ),
        out_shape=jax.ShapeDtypeStruct((n, d), F32),
        compiler_params=_params(("arbitrary",)),
    )(pos_flat, x1, ew_t, pf, wpe, wpg, g, bta, ybuf)


def _layer(x, p_i, w_in, rw_mu, rw_w0, rw_w_up, rw_a0, rw_a_up, rw_g_up, rw_k_k, rw_k_a, rw_r_k, rw_gn_w,
           rw_gn_b, w_a_out, hg_lb_logits, hg_norm_w, w_b_out, w_o, ln1_g, ln1_b, router_g_w, router_g_b,
           router_e_w, router_e_b, w1, w3, w2, ln2_g, ln2_b, w_pe, w_pg, alpha, layer):
    b, t, d = x.shape
    n = b * t
    rw_dim = rw_w0.shape[0]
    rw_cols = 3 * rw_dim + RW_DECAY_LORA + RW_A_LORA + RW_GATE_LORA
    hg_cols = 4 * hg_norm_w.shape[0]
    ne = N_GROUPS * EXPERTS_PER_GROUP
    row2 = lambda a_: a_.reshape(1, -1)
    xf = x.reshape(n, d)

    wb = w_in.astype(BF16)
    proj_rw, proj_hg, proj_gt = _project(xf, wb[:, :rw_cols], wb[:, rw_cols:rw_cols + hg_cols],
                                         wb[:, rw_cols + hg_cols:])
    ya = _rwkv_branch(proj_rw.reshape(b, t, rw_cols), rw_mu, row2(rw_w0), rw_w_up, rw_a0, rw_a_up, rw_g_up,
                      rw_k_k, rw_k_a, rw_r_k, rw_gn_w, rw_gn_b)
    yb = _hgrn_branch(proj_hg.reshape(b, t, hg_cols), hg_lb_logits, hg_norm_w, layer)

    wr = jnp.zeros((LANES, d), F32)
    wr = wr.at[:N_GROUPS].set(router_g_w.T).at[ROUTER_EXPERT_ROW:ROUTER_EXPERT_ROW + ne].set(router_e_w.T)
    bias = jnp.zeros((LANES,), F32)
    bias = bias.at[:N_GROUPS].set(router_g_b).at[ROUTER_EXPERT_ROW:ROUTER_EXPERT_ROW + ne].set(router_e_b)
    x1, lt = _merge(xf, ya.reshape(n, -1), yb.reshape(n, -1), proj_gt, w_a_out.astype(BF16),
                    w_b_out.astype(BF16), w_o.astype(BF16), row2(ln1_g), row2(ln1_b), wr, alpha)
    route, cnt = _route(lt, bias.reshape(LANES, 1))

    tm = TM_EXPERT
    nblk = (n * 2) // tm + ne
    assert nblk <= TAB_LANES
    pos2, tab = _finalize(route, cnt, tm)
    pos = pos2.reshape(-1)
    blk_e, n_used = tab[0, :nblk], tab[1, :1]

    xbuf = _dispatch(pos, tab[2, :ne], tab[3, :ne], n_used, x1, nblk * tm, tm)
    ybuf = _experts(blk_e, n_used, xbuf, w1, w3, w2)
    out = _combine(pos, x1, route[4:6].T, p_i.reshape(n, -1), w_pe.astype(BF16), w_pg.astype(BF16),
                   row2(ln2_g), row2(ln2_b), ybuf, alpha)
    return out.reshape(b, t, d)


def kernel(x, p, w_in, rw_mu, rw_w0, rw_w_up, rw_a0, rw_a_up, rw_g_up, rw_k_k, rw_k_a, rw_r_k, rw_gn_w, rw_gn_b,
           w_a_out, hg_lb_logits, hg_norm_w, w_b_out, w_o, ln1_g, ln1_b, router_g_w, router_g_b, router_e_w,
           router_e_b, w1, w3, w2, ln2_g, ln2_b, w_pe, w_pg):
    depth = w_in.shape[0]
    alpha = (2 * depth) ** 0.25
    for i in range(depth):
        x = _layer(x, p[i], w_in[i], rw_mu[i], rw_w0[i], rw_w_up[i], rw_a0[i], rw_a_up[i], rw_g_up[i], rw_k_k[i],
                   rw_k_a[i], rw_r_k[i].reshape(-1), rw_gn_w[i], rw_gn_b[i], w_a_out[i], hg_lb_logits,
                   hg_norm_w[i], w_b_out[i], w_o[i], ln1_g[i], ln1_b[i],
                   router_g_w[i], router_g_b[i], router_e_w[i], router_e_b[i], w1[i], w3[i], w2[i], ln2_g[i],
                   ln2_b[i], w_pe[i], w_pg[i], alpha, i)
    return x
```

```python
import functools

import jax
import jax.numpy as jnp
from jax import lax
from jax.experimental import pallas as pl
from jax.experimental.pallas import tpu as pltpu

F32 = jnp.float32
BF16 = jnp.bfloat16
I32 = jnp.int32

NN = (((1,), (0,)), ((), ()))
NT = (((1,), (1,)), ((), ()))

RW_HEAD = 64
RW_DECAY_LORA = 64
RW_A_LORA = 64
RW_GATE_LORA = 128
RW_GN_EPS = 64e-5
RW_DECAY_SCALE = 0.6065306597126334
HG_HEADS = 4
N_GROUPS = 4
EXPERTS_PER_GROUP = 8
LN_EPS = 1e-5
RMS_EPS = 1e-6

CHUNK = 64
SUB = 16
LANES = 128
VMEM_LIMIT = 56 * 1024 * 1024

TM_PROJ = 256
TB_REC = 256
TM_MERGE = 512
TB_ROUTE = 512
TD_DISPATCH = 512
TM_EXPERT = 256
TC_COMBINE = 256
RW_WIDE = 4
DMA_UNROLL = 8


def _dg(a, b, dn=NN):
    return lax.dot_general(a, b, dn, preferred_element_type=F32)


def _dot(a, b, dn=NN):
    return _dg(a.astype(BF16), b.astype(BF16), dn)


def _split(a):
    hi = a.astype(BF16)
    lo = (a - hi.astype(F32)).astype(BF16)
    return hi, lo


def _dot_hl(a, b_exact, dn=NN):
    hi, lo = _split(a)
    return _dg(hi, b_exact, dn) + _dg(lo, b_exact, dn)


def _dot3(a, b, dn=NN):
    ah, al = _split(a)
    bh, bl = _split(b)
    return _dg(ah, bh, dn) + (_dg(ah, bl, dn) + _dg(al, bh, dn))


def _cumsum_chunks(x, tri):
    h = x.astype(BF16)
    r1 = x - h.astype(F32)
    m = r1.astype(BF16)
    l = (r1 - m.astype(F32)).astype(BF16)
    return _dg(tri, h) + (_dg(tri, m) + _dg(tri, l))


def _layer_norm(h, g, b):
    mu = jnp.mean(h, axis=-1, keepdims=True)
    d = h - mu
    var = jnp.mean(d * d, axis=-1, keepdims=True)
    return d * lax.rsqrt(var + LN_EPS) * g + b


def _params(sem):
    return pltpu.CompilerParams(dimension_semantics=sem, vmem_limit_bytes=VMEM_LIMIT)


def _proj_kernel(x_ref, wr_ref, wh_ref, wg_ref, pr_ref, ph_ref, pg_ref):
    xb = x_ref[...].astype(BF16)
    pr_ref[...] = _dg(xb, wr_ref[...])
    ph_ref[...] = _dg(xb, wh_ref[...])
    pg_ref[...] = _dg(xb, wg_ref[...])


def _project(xf, w_rw, w_hg, w_gt):
    n, d = xf.shape
    tm = TM_PROJ
    full = lambda w: pl.BlockSpec(w.shape, lambda i: (0, 0))
    tile = lambda c: pl.BlockSpec((tm, c), lambda i: (i, 0))
    return pl.pallas_call(
        _proj_kernel,
        grid=(n // tm,),
        in_specs=[tile(d), full(w_rw), full(w_hg), full(w_gt)],
        out_specs=[tile(w_rw.shape[1]), tile(w_hg.shape[1]), tile(w_gt.shape[1])],
        out_shape=[jax.ShapeDtypeStruct((n, w.shape[1]), F32) for w in (w_rw, w_hg, w_gt)],
        compiler_params=_params(("parallel",)),
    )(xf, w_rw, w_hg, w_gt)


def _each(f, *ls):
    return [f(*xs) for xs in zip(*ls)]


def _two(x):
    m1 = lax.broadcasted_iota(I32, x.shape, 1) < RW_HEAD
    return jnp.concatenate([jnp.where(m1, x, 0.0), jnp.where(m1, 0.0, x)], axis=0)


def _rwkv_chunk_prepare(ins, lvl_ref):
    c = CHUNK
    lane = lax.broadcasted_iota(I32, (c, LANES), 1)
    row = lax.broadcasted_iota(I32, (c, LANES), 0)
    scol = jnp.bitwise_and(lane, RW_HEAD - 1)
    strict = row > scol
    incl = row >= scol
    r2 = lax.broadcasted_iota(I32, (LANES, LANES), 0)
    c2 = lax.broadcasted_iota(I32, (LANES, LANES), 1)
    eye = jnp.where(r2 == c2, 1.0, 0.0).astype(F32)

    def prep(r, k, v, av, bv, lw, lc):
        l_end = lc[c - 1:c]
        e_r = jnp.exp(l_end - lc)
        e_n = jnp.exp(-lc)
        return dict(at=av * jnp.exp(lc - lw), rt=r * jnp.exp(lc), bt=bv * e_n, kt=k * e_n,
                    bk=jnp.concatenate([bv * e_r, k * e_r], axis=0), pc=jnp.exp(l_end), v=v)

    q = [prep(*xs) for xs in ins]
    p = [_dot(jnp.concatenate([d["at"], d["rt"]], axis=0),
              jnp.concatenate([_two(d["bt"]), _two(d["kt"])], axis=0), NT) for d in q]
    sab = [jnp.where(strict, x[:c, :LANES], 0.0) for x in p]
    sak = [jnp.where(strict, x[:c, LANES:], 0.0) for x in p]
    srb = [jnp.where(incl, x[c:, :LANES], 0.0) for x in p]
    srk = [jnp.where(incl, x[c:, LANES:], 0.0) for x in p]
    sv = _each(lambda ak, rk, d: _dot(jnp.concatenate([ak, rk], axis=0), _two(d["v"])), sak, srk, q)

    a_bd = [_two(x) for x in sab]
    t = [eye + a * lvl_ref[0] for a in a_bd]
    for lv in range(1, lvl_ref.shape[0]):
        ta = _each(lambda t_, a: _dot(t_, a * lvl_ref[lv]), t, a_bd)
        t = _each(lambda t_, ta_: t_ + _dot(ta_, t_), t, ta)

    x = _each(lambda t_, d, sv_: _dot(t_, jnp.concatenate([_two(d["at"]), _two(sv_[:c])], axis=1)), t, q, sv)
    return [dict(uk=x_[:c, :LANES] + x_[c:, :LANES],
                 w=x_[:c, LANES:] + x_[c:, LANES:],
                 rt=d["rt"], rkv=sv_[c:], srb=srb_, bk=d["bk"], v=d["v"], pc=d["pc"])
            for x_, d, sv_, srb_ in zip(x, q, sv, srb)]


def _rwkv_chunk_apply(prep, states, bd):
    c = CHUNK
    g1 = _each(lambda d, s: _dot(jnp.concatenate([d["uk"], d["rt"]], axis=0), s, NT), prep, states)
    u = _each(lambda g, d: g[:c] + d["w"], g1, prep)
    y = _each(lambda g, d, u_: g[c:] + d["rkv"] + _dot(d["srb"], _two(u_)), g1, prep, u)
    upd = _each(lambda u_, d: _dot(jnp.concatenate([u_, d["v"]], axis=0).T, d["bk"]), u, prep)
    s_new = _each(lambda d, s, up: s * d["pc"] + bd * up, prep, states, upd)
    return list(zip(y, s_new))


def _rwkv_kernel(u_ref, mu_ref, w0_ref, wup_ref, a0_ref, aup_ref, gup_ref, kk_ref, ka_ref, rk_ref,
                 gnw_ref, gnb_ref, tri_ref, gsum_ref, lvl_ref, bd_ref, ya_ref,
                 s_ref, prev_ref, r_s, k_s, v_s, a_s, b_s, lw_s, lc_s, g_s, y_s):
    tb = pl.program_id(1)

    @pl.when(tb == 0)
    def _():
        s_ref[...] = jnp.zeros_like(s_ref)
        prev_ref[...] = jnp.zeros_like(prev_ref)

    u = u_ref[0]
    nt = u.shape[0]
    dim = r_s.shape[1]
    rowid = lax.broadcasted_iota(I32, u.shape, 0)
    shifted = jnp.where(rowid == 0, prev_ref[...], pltpu.roll(u, 1, axis=0))
    prev_ref[...] = u[nt - 1:nt, :]
    um = u + (shifted - u) * mu_ref[...]

    r = um[:, 0:dim]
    k = um[:, dim:2 * dim]
    v = um[:, 2 * dim:3 * dim]
    xwa = um[:, 3 * dim:3 * dim + LANES]
    xg = um[:, 3 * dim + LANES:3 * dim + 2 * LANES]

    wpre = w0_ref[...] + _dot(jnp.tanh(xwa), wup_ref[...])
    lw = -RW_DECAY_SCALE * jax.nn.sigmoid(wpre)
    a = jax.nn.sigmoid(a0_ref[...] + _dot(xwa, aup_ref[...]))
    g_s[...] = _dot(jax.nn.sigmoid(xg), gup_ref[...])
    kk = k * kk_ref[...]
    ss = _dot(kk * kk, gsum_ref[...])
    kk = kk * lax.rsqrt(jnp.maximum(ss, 1e-24))
    r_s[...] = r
    k_s[...] = k * (1.0 + (a - 1.0) * ka_ref[...])
    v_s[...] = v
    a_s[...] = -kk
    b_s[...] = kk * a
    lw_s[...] = lw
    lc_s[...] = _cumsum_chunks(lw, tri_ref[...])
    bd = bd_ref[...]

    npair = dim // LANES
    lanes = [slice(p * LANES, (p + 1) * LANES) for p in range(npair)]

    def group_body(gi, carry):
        rows = [pl.ds(pl.multiple_of((gi * RW_WIDE + ci) * CHUNK, CHUNK), CHUNK) for ci in range(RW_WIDE)]
        ins = [(r_s[rw, ls], k_s[rw, ls], v_s[rw, ls], a_s[rw, ls], b_s[rw, ls], lw_s[rw, ls], lc_s[rw, ls])
               for rw in rows for ls in lanes]
        prep = _rwkv_chunk_prepare(ins, lvl_ref)
        states = [s_ref[p] for p in range(npair)]
        for ci, rw in enumerate(rows):
            outs = _rwkv_chunk_apply(prep[ci * npair:(ci + 1) * npair], states, bd)
            states = [s_new for _, s_new in outs]
            for (y, _), ls in zip(outs, lanes):
                y_s[rw, ls] = y
        for p in range(npair):
            s_ref[p] = states[p]
        return carry

    lax.fori_loop(0, nt // (CHUNK * RW_WIDE), group_body, 0)

    y = y_s[...]
    gsum = gsum_ref[...]
    inv_n = 1.0 / RW_HEAD
    m = _dot_hl(y, gsum) * inv_n
    d = y - m
    var = _dot(d * d, gsum) * inv_n
    yn = d * lax.rsqrt(var + RW_GN_EPS) * gnw_ref[...] + gnb_ref[...]
    bonus = _dot(r_s[...] * k_s[...] * rk_ref[...], gsum) * v_s[...]
    ya_ref[0] = ((yn + bonus) * g_s[...]).astype(BF16)


def _rwkv_branch(proj_rw, mu, w0, wup, a0, aup, gup, k_k, k_a, r_k, gn_w, gn_b):
    b, t, cols = proj_rw.shape
    dim = w0.shape[1]
    tb = TB_REC
    ii = jnp.arange(tb)
    tri = ((ii[:, None] // CHUNK == ii[None, :] // CHUNK) & (ii[:, None] >= ii[None, :])).astype(BF16)
    jj = jnp.arange(dim)
    gsum = (jj[:, None] // RW_HEAD == jj[None, :] // RW_HEAD).astype(BF16)
    rr = jnp.arange(LANES)[:, None]
    cc = jnp.arange(LANES)[None, :]
    lvls = []
    s = 1
    while s < CHUNK:
        lvls.append(((rr // (2 * s) == cc // (2 * s)) & ((rr // s) % 2 == 1) & ((cc // s) % 2 == 0)).astype(F32))
        s *= 2
    lvl = jnp.stack(lvls)
    bd = (rr // RW_HEAD == cc // RW_HEAD).astype(F32)
    zpad = lambda rows: jnp.zeros((rows, dim), F32)
    wup_p = jnp.concatenate([wup, zpad(LANES - wup.shape[0])], axis=0).astype(BF16)
    aup_p = jnp.concatenate([zpad(LANES - aup.shape[0]), aup], axis=0).astype(BF16)
    row2 = lambda a_: a_.reshape(1, -1)
    const = lambda a_: pl.BlockSpec(a_.shape, lambda bi, ti: (0,) * a_.ndim)
    args = [row2(mu), row2(w0), wup_p, row2(a0), aup_p, gup.astype(BF16), row2(k_k), row2(k_a), row2(r_k),
            row2(gn_w), row2(gn_b), tri, gsum, lvl, bd]
    sc = lambda: pltpu.VMEM((tb, dim), F32)
    return pl.pallas_call(
        _rwkv_kernel,
        grid=(b, t // tb),
        in_specs=[pl.BlockSpec((1, tb, cols), lambda bi, ti: (bi, ti, 0))] + [const(a_) for a_ in args],
        out_specs=pl.BlockSpec((1, tb, dim), lambda bi, ti: (bi, ti, 0)),
        out_shape=jax.ShapeDtypeStruct((b, t, dim), BF16),
        scratch_shapes=[pltpu.VMEM((dim // LANES, LANES, LANES), F32), pltpu.VMEM((1, cols), F32)]
                       + [sc() for _ in range(9)],
        compiler_params=_params(("arbitrary", "arbitrary")),
    )(proj_rw, *args)


def _hgrn_chunk(ins):
    c = CHUNK
    subs = [(SUB * i, SUB * (i + 1)) for i in range(c // SUB)]

    def scores(q, k, lf, bc, lo, hi):
        m = bc[lo:lo + 1] - lf[lo:lo + 1]
        att = _dot(q[lo:hi] * jnp.exp(bc[lo:hi] - m), k[:hi] * jnp.exp(m - bc[:hi]), NT)
        tt = lax.broadcasted_iota(I32, (SUB, hi), 0) + lo
        s_ = lax.broadcasted_iota(I32, (SUB, hi), 1)
        return jnp.where(s_ <= tt, att, 0.0)

    att = [[scores(q, k, lf, bc, lo, hi) for lo, hi in subs] for q, k, v, lf, bc, st in ins]
    inter = [_dot(q * jnp.exp(bc), st, NT) for q, k, v, lf, bc, st in ins]
    upd = [_dot(v.T, k * jnp.exp(bc[c - 1:c] - bc)) for q, k, v, lf, bc, st in ins]
    intra = [[_dot(a, x[2][:hi]) for a, (lo, hi) in zip(arow, subs)] for arow, x in zip(att, ins)]
    o = [jnp.concatenate(rows, axis=0) + it for rows, it in zip(intra, inter)]
    st_new = [x[5] * jnp.exp(x[4][c - 1:c]) + up for x, up in zip(ins, upd)]
    return list(zip(o, st_new))


def _hgrn_kernel(layer, u_ref, lbl_ref, nw_ref, tri_ref, gsum_ref, yb_ref,
                 st_ref, q_s, k_s, v_s, lf_s, bc_s, o_s):
    tb = pl.program_id(1)

    @pl.when(tb == 0)
    def _():
        st_ref[...] = jnp.zeros_like(st_ref)

    u = u_ref[0]
    nt = u.shape[0]
    dim = q_s.shape[1]
    lbl = lbl_ref[...]
    e = jnp.exp(lbl - jnp.max(lbl, axis=0, keepdims=True))
    lb = jnp.sum(e[0:layer + 1], axis=0, keepdims=True) / jnp.sum(e, axis=0, keepdims=True)
    zf = u[:, dim:2 * dim]
    sig = jax.nn.sigmoid(zf)
    f = lb + (1.0 - lb) * sig
    qin = u[:, 0:dim]
    q_s[...] = qin * jax.nn.sigmoid(qin)
    k_s[...] = (1.0 - lb) * (1.0 - sig)
    v_s[...] = u[:, 2 * dim:3 * dim]
    lf = jnp.log(f)
    lf_s[...] = lf
    bc_s[...] = _cumsum_chunks(lf, tri_ref[...])

    def chunk_body(ci, carry):
        off = pl.multiple_of(ci * CHUNK, CHUNK)
        rows = pl.ds(off, CHUNK)
        lanes = [slice(h * LANES, (h + 1) * LANES) for h in range(dim // LANES)]
        ins = [(q_s[rows, ls], k_s[rows, ls], v_s[rows, ls], lf_s[rows, ls], bc_s[rows, ls], st_ref[h])
               for h, ls in enumerate(lanes)]
        outs = _hgrn_chunk(ins)
        for h, ls in enumerate(lanes):
            o, st_new = outs[h]
            st_ref[h] = st_new
            o_s[rows, ls] = o
        return carry

    lax.fori_loop(0, nt // CHUNK, chunk_body, 0)

    o = o_s[...]
    ms = _dot(o * o, gsum_ref[...]) * (1.0 / LANES)
    og = u[:, 3 * dim:4 * dim]
    yb_ref[0] = (o * lax.rsqrt(ms + RMS_EPS) * nw_ref[...] * jax.nn.sigmoid(og)).astype(BF16)


def _hgrn_branch(proj_hg, lb_logits, norm_w, layer):
    b, t, cols = proj_hg.shape
    dim = cols // 4
    tb = TB_REC
    ii = jnp.arange(tb)
    tri = ((ii[:, None] // CHUNK == ii[None, :] // CHUNK) & (ii[:, None] >= ii[None, :])).astype(BF16)
    jj = jnp.arange(dim)
    gsum = (jj[:, None] // LANES == jj[None, :] // LANES).astype(BF16)
    const = lambda a_: pl.BlockSpec(a_.shape, lambda bi, ti: (0,) * a_.ndim)
    args = [lb_logits, norm_w.reshape(1, -1), tri, gsum]
    sc = lambda: pltpu.VMEM((tb, dim), F32)
    return pl.pallas_call(
        functools.partial(_hgrn_kernel, layer),
        grid=(b, t // tb),
        in_specs=[pl.BlockSpec((1, tb, cols), lambda bi, ti: (bi, ti, 0))] + [const(a_) for a_ in args],
        out_specs=pl.BlockSpec((1, tb, dim), lambda bi, ti: (bi, ti, 0)),
        out_shape=jax.ShapeDtypeStruct((b, t, dim), BF16),
        scratch_shapes=[pltpu.VMEM((dim // LANES, LANES, LANES), F32)] + [sc() for _ in range(6)],
        compiler_params=_params(("arbitrary", "arbitrary")),
    )(proj_hg, *args)


def _merge_kernel(alpha, x_ref, ya_ref, yb_ref, pg_ref, wa_ref, wb_ref, wo_ref, g_ref, b_ref, wr_ref,
                  x1_ref, lt_ref):
    d = x_ref.shape[1]
    gates = jax.nn.sigmoid(pg_ref[...])
    merged = gates[:, :d] * _dg(ya_ref[...], wa_ref[...]) + gates[:, d:] * _dg(yb_ref[...], wb_ref[...])
    h = alpha * x_ref[...] + _dot(merged, wo_ref[...])
    x1 = _layer_norm(h, g_ref[...], b_ref[...])
    x1_ref[...] = x1
    lt_ref[...] = _dot3(wr_ref[...], x1, NT)


def _merge(xf, ya, yb, pgate, wa, wb, wo, g, bta, wr, alpha):
    n, d = xf.shape
    tm = TM_MERGE
    tile = lambda a_: pl.BlockSpec((tm, a_.shape[1]), lambda i: (i, 0))
    const = lambda a_: pl.BlockSpec(a_.shape, lambda i: (0, 0))
    return pl.pallas_call(
        functools.partial(_merge_kernel, alpha),
        grid=(n // tm,),
        in_specs=[tile(xf), tile(ya), tile(yb), tile(pgate), const(wa), const(wb), const(wo), const(g),
                  const(bta), const(wr)],
        out_specs=[pl.BlockSpec((tm, d), lambda i: (i, 0)), pl.BlockSpec((LANES, tm), lambda i: (0, i))],
        out_shape=[jax.ShapeDtypeStruct((n, d), F32), jax.ShapeDtypeStruct((LANES, n), F32)],
        compiler_params=_params(("parallel",)),
    )(xf, ya, yb, pgate, wa, wb, wo, g, bta, wr)


ROUTER_EXPERT_ROW = 8


def _route_kernel(lt_ref, bias_ref, upper_ref, route_ref, cnt_ref, carry_ref):
    @pl.when(pl.program_id(0) == 0)
    def _():
        carry_ref[...] = jnp.zeros_like(carry_ref)

    ne = N_GROUPS * EXPERTS_PER_GROUP
    lt = lt_ref[...] + bias_ref[...]
    nb = lt.shape[1]
    neg = -jnp.inf
    lg = lt[0:8]
    rg = lax.broadcasted_iota(I32, (8, nb), 0).astype(F32)
    lg = jnp.where(rg < N_GROUPS, lg, neg)
    mg = jnp.max(lg, axis=0, keepdims=True)
    gidx = jnp.min(jnp.where(lg == mg, rg, 1e9), axis=0, keepdims=True)
    pg_sel = 1.0 / jnp.sum(jnp.exp(lg - mg), axis=0, keepdims=True)

    le = lt[ROUTER_EXPERT_ROW:ROUTER_EXPERT_ROW + ne]
    re = lax.broadcasted_iota(I32, (ne, nb), 0).astype(F32)
    in_group = jnp.floor(re * (1.0 / EXPERTS_PER_GROUP)) == gidx
    l1 = jnp.where(in_group, le, neg)
    m1 = jnp.max(l1, axis=0, keepdims=True)
    i1 = jnp.min(jnp.where(l1 == m1, re, 1e9), axis=0, keepdims=True)
    l2 = jnp.where(re == i1, neg, l1)
    m2 = jnp.max(l2, axis=0, keepdims=True)
    i2 = jnp.min(jnp.where(l2 == m2, re, 1e9), axis=0, keepdims=True)
    e2 = jnp.exp(m2 - m1)
    w1 = pg_sel / (1.0 + e2)
    w2 = pg_sel * e2 / (1.0 + e2)

    sel1 = re == i1
    sel2 = re == i2
    onehot = jnp.where(sel1 | sel2, 1.0, 0.0)
    before = _dg(onehot.astype(BF16), upper_ref[...])
    tot = carry_ref[...] + before
    rank1 = jnp.sum(jnp.where(sel1, tot, 0.0), axis=0, keepdims=True)
    rank2 = jnp.sum(jnp.where(sel2, tot, 0.0), axis=0, keepdims=True)
    carry = carry_ref[...] + jnp.sum(onehot, axis=1, keepdims=True)
    carry_ref[...] = carry
    cnt_ref[...] = jnp.broadcast_to(carry, cnt_ref.shape)
    zero = jnp.zeros_like(w1)
    route_ref[...] = jnp.concatenate([i1, i2, rank1, rank2, w1, w2, zero, zero], axis=0)


def _route(lt, bias_col):
    n = lt.shape[1]
    tb = TB_ROUTE
    ne = N_GROUPS * EXPERTS_PER_GROUP
    ii = jnp.arange(tb)
    upper = (ii[:, None] < ii[None, :]).astype(BF16)
    return pl.pallas_call(
        _route_kernel,
        grid=(n // tb,),
        in_specs=[pl.BlockSpec((LANES, tb), lambda i: (0, i)),
                  pl.BlockSpec((LANES, 1), lambda i: (0, 0)),
                  pl.BlockSpec((tb, tb), lambda i: (0, 0))],
        out_specs=[pl.BlockSpec((8, tb), lambda i: (0, i)), pl.BlockSpec((ne, LANES), lambda i: (0, 0))],
        out_shape=[jax.ShapeDtypeStruct((8, n), F32), jax.ShapeDtypeStruct((ne, LANES), F32)],
        scratch_shapes=[pltpu.VMEM((ne, 1), F32)],
        compiler_params=_params(("arbitrary",)),
    )(lt, bias_col, upper)


TAB_LANES = 2 * LANES


def _finalize_kernel(tm, route_ref, cnt_ref, lower_ref, pos_ref, tab_ref):
    ne = cnt_ref.shape[0]
    cnt = cnt_ref[...]
    nb = jnp.floor((cnt + (tm - 1)) * (1.0 / tm))
    bstart = _dg(lower_ref[...], nb.astype(BF16))
    bend = bstart + nb
    pad_start = bstart[:, 0:1] * tm
    route = route_ref[...]
    nt = route.shape[1]
    re = lax.broadcasted_iota(I32, (ne, nt), 0).astype(F32)
    rows = []
    for slot in range(2):
        start = jnp.sum(jnp.where(re == route[slot:slot + 1], pad_start, 0.0), axis=0, keepdims=True)
        rows.append(start + route[2 + slot:3 + slot])
    pos_ref[...] = jnp.concatenate(rows, axis=0).astype(I32)

    nl = tab_ref.shape[1]
    n_used = jnp.max(bend[:, 0:1], axis=0, keepdims=True)
    blk = jnp.minimum(lax.broadcasted_iota(I32, (ne, nl), 1).astype(F32), n_used - 1.0)
    blk_e = jnp.sum(jnp.where(bend[:, 0:1] <= blk, 1.0, 0.0), axis=0, keepdims=True)
    blk_e = jnp.minimum(blk_e, ne - 1.0)
    diag = (lax.broadcasted_iota(I32, (ne, nl), 0) == lax.broadcasted_iota(I32, (ne, nl), 1))
    to_lanes = lambda col: jnp.sum(jnp.where(diag, col, 0.0), axis=0, keepdims=True)
    pad_lo = to_lanes(pad_start + cnt[:, 0:1])
    pad_hi = to_lanes(bend[:, 0:1] * tm)
    zero = jnp.zeros((1, nl), F32)
    tab_ref[...] = jnp.concatenate([blk_e, jnp.broadcast_to(n_used, (1, nl)), pad_lo, pad_hi,
                                    zero, zero, zero, zero], axis=0).astype(I32)


def _finalize(route, cnt, tm):
    n = route.shape[1]
    ne = cnt.shape[0]
    tb = TB_ROUTE
    ii = jnp.arange(ne)
    lower = (ii[:, None] > ii[None, :]).astype(BF16)
    return pl.pallas_call(
        functools.partial(_finalize_kernel, tm),
        grid=(n // tb,),
        in_specs=[pl.BlockSpec((8, tb), lambda i: (0, i)), pl.BlockSpec(cnt.shape, lambda i: (0, 0)),
                  pl.BlockSpec((ne, ne), lambda i: (0, 0))],
        out_specs=[pl.BlockSpec((2, tb), lambda i: (0, i)), pl.BlockSpec((8, TAB_LANES), lambda i: (0, 0))],
        out_shape=[jax.ShapeDtypeStruct((2, n), I32), jax.ShapeDtypeStruct((8, TAB_LANES), I32)],
        compiler_params=_params(("arbitrary",)),
    )(route, cnt, lower)


def _dispatch_kernel(tm, pos_ref, plo_ref, phi_ref, nu_ref, x_ref, xbuf_ref, zblk, sem, zsem):
    i = pl.program_id(0)
    td = x_ref.shape[0]
    n = pl.num_programs(0) * td
    ne = plo_ref.shape[0]
    nblk = xbuf_ref.shape[0] // tm

    def row_copy(j, slot):
        dst = pos_ref[slot * n + i * td + j]
        return pltpu.make_async_copy(x_ref.at[pl.ds(j, 1), :], xbuf_ref.at[pl.ds(dst, 1), :], sem)

    def issue(j, carry):
        row_copy(j, 0).start()
        row_copy(j, 1).start()
        return carry

    lax.fori_loop(0, td, issue, 0, unroll=DMA_UNROLL)

    zero_row = lambda j: pltpu.make_async_copy(zblk.at[pl.ds(0, 1), :], xbuf_ref.at[pl.ds(j, 1), :], zsem)
    zero_blk = lambda b: pltpu.make_async_copy(
        zblk, xbuf_ref.at[pl.ds(pl.multiple_of(b * tm, tm), tm), :], zsem)

    def for_pad_rows(fn):
        def per_expert(e, carry):
            def body(j, c2):
                fn(zero_row(j))
                return c2
            return lax.fori_loop(plo_ref[e], phi_ref[e], body, carry)
        lax.fori_loop(0, ne, per_expert, 0)

        def per_blk(b, carry):
            fn(zero_blk(b))
            return carry
        lax.fori_loop(nu_ref[0], nblk, per_blk, 0)

    @pl.when(i == 0)
    def _():
        zblk[...] = jnp.zeros_like(zblk)
        for_pad_rows(lambda cp: cp.start())
        for_pad_rows(lambda cp: cp.wait())

    for slot in range(2):
        pltpu.make_async_copy(x_ref, xbuf_ref.at[pl.ds(0, td), :], sem).wait()


def _dispatch(pos_flat, pad_lo, pad_hi, n_used, x1, rows, tm):
    n, d = x1.shape
    td = TD_DISPATCH
    return pl.pallas_call(
        functools.partial(_dispatch_kernel, tm),
        grid_spec=pltpu.PrefetchScalarGridSpec(
            num_scalar_prefetch=4,
            grid=(n // td,),
            in_specs=[pl.BlockSpec((td, d), lambda i, *_: (i, 0))],
            out_specs=pl.BlockSpec(memory_space=pl.ANY),
            scratch_shapes=[pltpu.VMEM((tm, d), F32), pltpu.SemaphoreType.DMA(()), pltpu.SemaphoreType.DMA(())],
        ),
        out_shape=jax.ShapeDtypeStruct((rows, d), F32),
        compiler_params=_params(("arbitrary",)),
    )(pos_flat, pad_lo, pad_hi, n_used, x1)


def _expert_kernel(be_ref, nu_ref, x_ref, w1_ref, w3_ref, w2_ref, y_ref, w1b, w3b, w2b):
    i = pl.program_id(0)
    prev = be_ref[jnp.maximum(i - 1, 0)]

    @pl.when((i == 0) | (be_ref[i] != prev))
    def _():
        w1b[...] = w1_ref[0].astype(BF16)
        w3b[...] = w3_ref[0].astype(BF16)
        w2b[...] = w2_ref[0].astype(BF16)

    @pl.when(i < nu_ref[0])
    def _():
        xb = x_ref[...].astype(BF16)
        h1 = _dg(xb, w1b[...])
        h3 = _dg(xb, w3b[...])
        h = (h1 * jax.nn.sigmoid(h1)) * h3
        y_ref[...] = _dot(h, w2b[...])

    @pl.when(i >= nu_ref[0])
    def _():
        y_ref[...] = jnp.zeros_like(y_ref)


def _experts(blk_e, n_used, xbuf, w1, w3, w2):
    rows, d = xbuf.shape
    de = w1.shape[2]
    tm = TM_EXPERT
    row_map = lambda i, be, nu: (jnp.minimum(i, nu[0] - 1), 0)
    return pl.pallas_call(
        _expert_kernel,
        grid_spec=pltpu.PrefetchScalarGridSpec(
            num_scalar_prefetch=2,
            grid=(rows // tm,),
            in_specs=[pl.BlockSpec((tm, d), row_map),
                      pl.BlockSpec((1, d, de), lambda i, be, nu: (be[i], 0, 0)),
                      pl.BlockSpec((1, d, de), lambda i, be, nu: (be[i], 0, 0)),
                      pl.BlockSpec((1, de, d), lambda i, be, nu: (be[i], 0, 0))],
            out_specs=pl.BlockSpec((tm, d), lambda i, be, nu: (i, 0)),
            scratch_shapes=[pltpu.VMEM((d, de), BF16), pltpu.VMEM((d, de), BF16), pltpu.VMEM((de, d), BF16)],
        ),
        out_shape=jax.ShapeDtypeStruct((rows, d), F32),
        compiler_params=_params(("arbitrary",)),
    )(blk_e, n_used, xbuf, w1, w3, w2)


def _combine_kernel(alpha, pos_ref, x1_ref, ew_ref, p_ref, wpe_ref, wpg_ref, g_ref, b_ref, ybuf_ref,
                    out_ref, gbuf, sem):
    i = pl.program_id(0)
    nsteps = pl.num_programs(0)
    tc = x1_ref.shape[0]
    n = nsteps * tc

    def gather(step, buf):
        def body(j, carry):
            for slot in range(2):
                src = pos_ref[slot * n + step * tc + j]
                pltpu.make_async_copy(ybuf_ref.at[pl.ds(src, 1), :], gbuf.at[buf, slot, pl.ds(j, 1), :],
                                      sem.at[buf]).start()
            return carry
        lax.fori_loop(0, tc, body, 0, unroll=DMA_UNROLL)

    @pl.when(i == 0)
    def _():
        gather(0, 0)

    @pl.when(i + 1 < nsteps)
    def _():
        gather(i + 1, (i + 1) % 2)

    cur = i % 2
    for slot in range(2):
        pltpu.make_async_copy(ybuf_ref.at[pl.ds(0, tc), :], gbuf.at[cur, slot], sem.at[cur]).wait()

    ew = ew_ref[...]
    ffn = gbuf[cur, 0] * ew[:, 0:1] + gbuf[cur, 1] * ew[:, 1:2]
    x2 = _layer_norm(alpha * x1_ref[...] + ffn, g_ref[...], b_ref[...])
    gate = jax.nn.sigmoid(_dot(x2, wpg_ref[...]))
    out_ref[...] = x2 + gate * _dot(p_ref[...], wpe_ref[...])


def _combine(pos_flat, x1, ew_t, pf, wpe, wpg, g, bta, ybuf, alpha):
    n, d = x1.shape
    tc = TC_COMBINE
    tile = lambda a_: pl.BlockSpec((tc, a_.shape[1]), lambda i, pos: (i, 0))
    const = lambda a_: pl.BlockSpec(a_.shape, lambda i, pos: (0, 0))
    return pl.pallas_call(
        functools.partial(_combine_kernel, alpha),
        grid_spec=pltpu.PrefetchScalarGridSpec(
            num_scalar_prefetch=1,
            grid=(n // tc,),
            in_specs=[tile(x1), tile(ew_t), tile(pf), const(wpe), const(wpg), const(g), const(bta),
                      pl.BlockSpec(memory_space=pl.ANY)],
            out_specs=pl.BlockSpec((tc, d), lambda i, pos: (i, 0)),
            scratch_shapes=[pltpu.VMEM((2, 2, tc, d), F32), pltpu.SemaphoreType.DMA((2,))],
        ),
        out_shape=jax.ShapeDtypeStruct((n, d), F32),
        compiler_params=_params(("arbitrary",)),
    )(pos_flat, x1, ew_t, pf, wpe, wpg, g, bta, ybuf)


def _layer(x, p_i, w_in, rw_mu, rw_w0, rw_w_up, rw_a0, rw_a_up, rw_g_up, rw_k_k, rw_k_a, rw_r_k, rw_gn_w,
           rw_gn_b, w_a_out, hg_lb_logits, hg_norm_w, w_b_out, w_o, ln1_g, ln1_b, router_g_w, router_g_b,
           router_e_w, router_e_b, w1, w3, w2, ln2_g, ln2_b, w_pe, w_pg, alpha, layer):
    b, t, d = x.shape
    n = b * t
    rw_dim = rw_w0.shape[0]
    rw_cols = 3 * rw_dim + RW_DECAY_LORA + RW_A_LORA + RW_GATE_LORA
    hg_cols = 4 * hg_norm_w.shape[0]
    ne = N_GROUPS * EXPERTS_PER_GROUP
    row2 = lambda a_: a_.reshape(1, -1)
    xf = x.reshape(n, d)

    wb = w_in.astype(BF16)
    proj_rw, proj_hg, proj_gt = _project(xf, wb[:, :rw_cols], wb[:, rw_cols:rw_cols + hg_cols],
                                         wb[:, rw_cols + hg_cols:])
    ya = _rwkv_branch(proj_rw.reshape(b, t, rw_cols), rw_mu, row2(rw_w0), rw_w_up, rw_a0, rw_a_up, rw_g_up,
                      rw_k_k, rw_k_a, rw_r_k, rw_gn_w, rw_gn_b)
    yb = _hgrn_branch(proj_hg.reshape(b, t, hg_cols), hg_lb_logits, hg_norm_w, layer)

    wr = jnp.zeros((LANES, d), F32)
    wr = wr.at[:N_GROUPS].set(router_g_w.T).at[ROUTER_EXPERT_ROW:ROUTER_EXPERT_ROW + ne].set(router_e_w.T)
    bias = jnp.zeros((LANES,), F32)
    bias = bias.at[:N_GROUPS].set(router_g_b).at[ROUTER_EXPERT_ROW:ROUTER_EXPERT_ROW + ne].set(router_e_b)
    x1, lt = _merge(xf, ya.reshape(n, -1), yb.reshape(n, -1), proj_gt, w_a_out.astype(BF16),
                    w_b_out.astype(BF16), w_o.astype(BF16), row2(ln1_g), row2(ln1_b), wr, alpha)
    route, cnt = _route(lt, bias.reshape(LANES, 1))

    tm = TM_EXPERT
    nblk = (n * 2) // tm + ne
    assert nblk <= TAB_LANES
    pos2, tab = _finalize(route, cnt, tm)
    pos = pos2.reshape(-1)
    blk_e, n_used = tab[0, :nblk], tab[1, :1]

    xbuf = _dispatch(pos, tab[2, :ne], tab[3, :ne], n_used, x1, nblk * tm, tm)
    ybuf = _experts(blk_e, n_used, xbuf, w1, w3, w2)
    out = _combine(pos, x1, route[4:6].T, p_i.reshape(n, -1), w_pe.astype(BF16), w_pg.astype(BF16),
                   row2(ln2_g), row2(ln2_b), ybuf, alpha)
    return out.reshape(b, t, d)


def kernel(x, p, w_in, rw_mu, rw_w0, rw_w_up, rw_a0, rw_a_up, rw_g_up, rw_k_k, rw_k_a, rw_r_k, rw_gn_w, rw_gn_b,
           w_a_out, hg_lb_logits, hg_norm_w, w_b_out, w_o, ln1_g, ln1_b, router_g_w, router_g_b, router_e_w,
           router_e_b, w1, w3, w2, ln2_g, ln2_b, w_pe, w_pg):
    depth = w_in.shape[0]
    alpha = (2 * depth) ** 0.25
    for i in range(depth):
        x = _layer(x, p[i], w_in[i], rw_mu[i], rw_w0[i], rw_w_up[i], rw_a0[i], rw_a_up[i], rw_g_up[i], rw_k_k[i],
                   rw_k_a[i], rw_r_k[i].reshape(-1), rw_gn_w[i], rw_gn_b[i], w_a_out[i], hg_lb_logits,
                   hg_norm_w[i], w_b_out[i], w_o[i], ln1_g[i], ln1_b[i],
                   router_g_w[i], router_g_b[i], router_e_w[i], router_e_b[i], w1[i], w3[i], w2[i], ln2_g[i],
                   ln2_b[i], w_pe[i], w_pg[i], alpha, i)
    return x
```

```python
import functools

import jax
import jax.numpy as jnp
from jax import lax
from jax.experimental import pallas as pl
from jax.experimental.pallas import tpu as pltpu

F32 = jnp.float32
BF16 = jnp.bfloat16
I32 = jnp.int32

NN = (((1,), (0,)), ((), ()))
NT = (((1,), (1,)), ((), ()))

RW_HEAD = 64
RW_DECAY_LORA = 64
RW_A_LORA = 64
RW_GATE_LORA = 128
RW_GN_EPS = 64e-5
RW_DECAY_SCALE = 0.6065306597126334
HG_HEADS = 4
N_GROUPS = 4
EXPERTS_PER_GROUP = 8
LN_EPS = 1e-5
RMS_EPS = 1e-6

CHUNK = 64
SUB = 16
LANES = 128
VMEM_LIMIT = 56 * 1024 * 1024

TM_PROJ = 256
TB_REC = 256
TM_MERGE = 512
TB_ROUTE = 512
TD_DISPATCH = 512
TM_EXPERT = 256
TC_COMBINE = 256
RW_WIDE = 4
HG_WIDE = 4
DMA_UNROLL = 8


def _dg(a, b, dn=NN):
    return lax.dot_general(a, b, dn, preferred_element_type=F32)


def _dot(a, b, dn=NN):
    return _dg(a.astype(BF16), b.astype(BF16), dn)


def _split(a):
    hi = a.astype(BF16)
    lo = (a - hi.astype(F32)).astype(BF16)
    return hi, lo


def _dot_hl(a, b_exact, dn=NN):
    hi, lo = _split(a)
    return _dg(hi, b_exact, dn) + _dg(lo, b_exact, dn)


def _dot3(a, b, dn=NN):
    ah, al = _split(a)
    bh, bl = _split(b)
    return _dg(ah, bh, dn) + (_dg(ah, bl, dn) + _dg(al, bh, dn))


def _cumsum_chunks(x, tri):
    h = x.astype(BF16)
    r1 = x - h.astype(F32)
    m = r1.astype(BF16)
    l = (r1 - m.astype(F32)).astype(BF16)
    return _dg(tri, h) + (_dg(tri, m) + _dg(tri, l))


def _sigmoid(x):
    return 0.5 * jnp.tanh(0.5 * x) + 0.5


def _layer_norm(h, g, b):
    mu = jnp.mean(h, axis=-1, keepdims=True)
    d = h - mu
    var = jnp.mean(d * d, axis=-1, keepdims=True)
    return d * lax.rsqrt(var + LN_EPS) * g + b


def _params(sem):
    return pltpu.CompilerParams(dimension_semantics=sem, vmem_limit_bytes=VMEM_LIMIT)


def _proj_kernel(x_ref, wr_ref, wh_ref, wg_ref, pr_ref, ph_ref, pg_ref):
    xb = x_ref[...].astype(BF16)
    pr_ref[...] = _dg(xb, wr_ref[...])
    ph_ref[...] = _dg(xb, wh_ref[...])
    pg_ref[...] = _dg(xb, wg_ref[...])


def _project(xf, w_rw, w_hg, w_gt):
    n, d = xf.shape
    tm = TM_PROJ
    full = lambda w: pl.BlockSpec(w.shape, lambda i: (0, 0))
    tile = lambda c: pl.BlockSpec((tm, c), lambda i: (i, 0))
    return pl.pallas_call(
        _proj_kernel,
        grid=(n // tm,),
        in_specs=[tile(d), full(w_rw), full(w_hg), full(w_gt)],
        out_specs=[tile(w_rw.shape[1]), tile(w_hg.shape[1]), tile(w_gt.shape[1])],
        out_shape=[jax.ShapeDtypeStruct((n, w.shape[1]), F32) for w in (w_rw, w_hg, w_gt)],
        compiler_params=_params(("parallel",)),
    )(xf, w_rw, w_hg, w_gt)


def _each(f, *ls):
    return [f(*xs) for xs in zip(*ls)]


def _two(x):
    m1 = lax.broadcasted_iota(I32, x.shape, 1) < RW_HEAD
    return jnp.concatenate([jnp.where(m1, x, 0.0), jnp.where(m1, 0.0, x)], axis=0)


def _rwkv_chunk_prepare(ins, lvl_ref):
    c = CHUNK
    lane = lax.broadcasted_iota(I32, (c, LANES), 1)
    row = lax.broadcasted_iota(I32, (c, LANES), 0)
    scol = jnp.bitwise_and(lane, RW_HEAD - 1)
    strict = row > scol
    incl = row >= scol
    r2 = lax.broadcasted_iota(I32, (LANES, LANES), 0)
    c2 = lax.broadcasted_iota(I32, (LANES, LANES), 1)
    eye = jnp.where(r2 == c2, 1.0, 0.0).astype(F32)

    def prep(r, k, v, av, bv, lw, lc):
        l_end = lc[c - 1:c]
        e_r = jnp.exp(l_end - lc)
        e_n = jnp.exp(-lc)
        return dict(at=av * jnp.exp(lc - lw), rt=r * jnp.exp(lc), bt=bv * e_n, kt=k * e_n,
                    bk=jnp.concatenate([bv * e_r, k * e_r], axis=0), pc=jnp.exp(l_end), v=v)

    q = [prep(*xs) for xs in ins]
    p = [_dot(jnp.concatenate([d["at"], d["rt"]], axis=0),
              jnp.concatenate([_two(d["bt"]), _two(d["kt"])], axis=0), NT) for d in q]
    sab = [jnp.where(strict, x[:c, :LANES], 0.0) for x in p]
    sak = [jnp.where(strict, x[:c, LANES:], 0.0) for x in p]
    srb = [jnp.where(incl, x[c:, :LANES], 0.0) for x in p]
    srk = [jnp.where(incl, x[c:, LANES:], 0.0) for x in p]
    sv = _each(lambda ak, rk, d: _dot(jnp.concatenate([ak, rk], axis=0), _two(d["v"])), sak, srk, q)

    a_bd = [_two(x) for x in sab]
    t = [eye + a * lvl_ref[0] for a in a_bd]
    for lv in range(1, lvl_ref.shape[0]):
        ta = _each(lambda t_, a: _dot(t_, a * lvl_ref[lv]), t, a_bd)
        t = _each(lambda t_, ta_: t_ + _dot(ta_, t_), t, ta)

    x = _each(lambda t_, d, sv_: _dot(t_, jnp.concatenate([_two(d["at"]), _two(sv_[:c])], axis=1)), t, q, sv)
    return [dict(uk=x_[:c, :LANES] + x_[c:, :LANES],
                 w=x_[:c, LANES:] + x_[c:, LANES:],
                 rt=d["rt"], rkv=sv_[c:], srb=srb_, bk=d["bk"], v=d["v"], pc=d["pc"])
            for x_, d, sv_, srb_ in zip(x, q, sv, srb)]


def _rwkv_chunk_apply(prep, states, bd):
    c = CHUNK
    g1 = _each(lambda d, s: _dot(jnp.concatenate([d["uk"], d["rt"]], axis=0), s, NT), prep, states)
    u = _each(lambda g, d: g[:c] + d["w"], g1, prep)
    y = _each(lambda g, d, u_: g[c:] + d["rkv"] + _dot(d["srb"], _two(u_)), g1, prep, u)
    upd = _each(lambda u_, d: _dot(jnp.concatenate([u_, d["v"]], axis=0).T, d["bk"]), u, prep)
    s_new = _each(lambda d, s, up: s * d["pc"] + bd * up, prep, states, upd)
    return list(zip(y, s_new))


def _rwkv_kernel(u_ref, mu_ref, w0_ref, wup_ref, a0_ref, aup_ref, gup_ref, kk_ref, ka_ref, rk_ref,
                 gnw_ref, gnb_ref, tri_ref, gsum_ref, lvl_ref, bd_ref, ya_ref,
                 s_ref, prev_ref, r_s, k_s, v_s, a_s, b_s, lw_s, lc_s, g_s, y_s):
    tb = pl.program_id(1)

    @pl.when(tb == 0)
    def _():
        s_ref[...] = jnp.zeros_like(s_ref)
        prev_ref[...] = jnp.zeros_like(prev_ref)

    u = u_ref[0]
    nt = u.shape[0]
    dim = r_s.shape[1]
    rowid = lax.broadcasted_iota(I32, u.shape, 0)
    shifted = jnp.where(rowid == 0, prev_ref[...], pltpu.roll(u, 1, axis=0))
    prev_ref[...] = u[nt - 1:nt, :]
    um = u + (shifted - u) * mu_ref[...]

    r = um[:, 0:dim]
    k = um[:, dim:2 * dim]
    v = um[:, 2 * dim:3 * dim]
    xwa = um[:, 3 * dim:3 * dim + LANES]
    xg = um[:, 3 * dim + LANES:3 * dim + 2 * LANES]

    wpre = w0_ref[...] + _dot(jnp.tanh(xwa), wup_ref[...])
    lw = -RW_DECAY_SCALE * _sigmoid(wpre)
    a = _sigmoid(a0_ref[...] + _dot(xwa, aup_ref[...]))
    g_s[...] = _dot(_sigmoid(xg), gup_ref[...])
    kk = k * kk_ref[...]
    ss = _dot(kk * kk, gsum_ref[...])
    kk = kk * lax.rsqrt(jnp.maximum(ss, 1e-24))
    r_s[...] = r
    k_s[...] = k * (1.0 + (a - 1.0) * ka_ref[...])
    v_s[...] = v
    a_s[...] = -kk
    b_s[...] = kk * a
    lw_s[...] = lw
    lc_s[...] = _cumsum_chunks(lw, tri_ref[...])
    bd = bd_ref[...]

    npair = dim // LANES
    lanes = [slice(p * LANES, (p + 1) * LANES) for p in range(npair)]

    def group_body(gi, carry):
        rows = [pl.ds(pl.multiple_of((gi * RW_WIDE + ci) * CHUNK, CHUNK), CHUNK) for ci in range(RW_WIDE)]
        ins = [(r_s[rw, ls], k_s[rw, ls], v_s[rw, ls], a_s[rw, ls], b_s[rw, ls], lw_s[rw, ls], lc_s[rw, ls])
               for rw in rows for ls in lanes]
        prep = _rwkv_chunk_prepare(ins, lvl_ref)
        states = [s_ref[p] for p in range(npair)]
        for ci, rw in enumerate(rows):
            outs = _rwkv_chunk_apply(prep[ci * npair:(ci + 1) * npair], states, bd)
            states = [s_new for _, s_new in outs]
            for (y, _), ls in zip(outs, lanes):
                y_s[rw, ls] = y
        for p in range(npair):
            s_ref[p] = states[p]
        return carry

    lax.fori_loop(0, nt // (CHUNK * RW_WIDE), group_body, 0)

    y = y_s[...]
    gsum = gsum_ref[...]
    inv_n = 1.0 / RW_HEAD
    m = _dot_hl(y, gsum) * inv_n
    d = y - m
    var = _dot(d * d, gsum) * inv_n
    yn = d * lax.rsqrt(var + RW_GN_EPS) * gnw_ref[...] + gnb_ref[...]
    bonus = _dot(r_s[...] * k_s[...] * rk_ref[...], gsum) * v_s[...]
    ya_ref[0] = ((yn + bonus) * g_s[...]).astype(BF16)


def _rwkv_branch(proj_rw, mu, w0, wup, a0, aup, gup, k_k, k_a, r_k, gn_w, gn_b):
    b, t, cols = proj_rw.shape
    dim = w0.shape[1]
    tb = TB_REC
    ii = jnp.arange(tb)
    tri = ((ii[:, None] // CHUNK == ii[None, :] // CHUNK) & (ii[:, None] >= ii[None, :])).astype(BF16)
    jj = jnp.arange(dim)
    gsum = (jj[:, None] // RW_HEAD == jj[None, :] // RW_HEAD).astype(BF16)
    rr = jnp.arange(LANES)[:, None]
    cc = jnp.arange(LANES)[None, :]
    lvls = []
    s = 1
    while s < CHUNK:
        lvls.append(((rr // (2 * s) == cc // (2 * s)) & ((rr // s) % 2 == 1) & ((cc // s) % 2 == 0)).astype(F32))
        s *= 2
    lvl = jnp.stack(lvls)
    bd = (rr // RW_HEAD == cc // RW_HEAD).astype(F32)
    zpad = lambda rows: jnp.zeros((rows, dim), F32)
    wup_p = jnp.concatenate([wup, zpad(LANES - wup.shape[0])], axis=0).astype(BF16)
    aup_p = jnp.concatenate([zpad(LANES - aup.shape[0]), aup], axis=0).astype(BF16)
    row2 = lambda a_: a_.reshape(1, -1)
    const = lambda a_: pl.BlockSpec(a_.shape, lambda bi, ti: (0,) * a_.ndim)
    args = [row2(mu), row2(w0), wup_p, row2(a0), aup_p, gup.astype(BF16), row2(k_k), row2(k_a), row2(r_k),
            row2(gn_w), row2(gn_b), tri, gsum, lvl, bd]
    sc = lambda: pltpu.VMEM((tb, dim), F32)
    return pl.pallas_call(
        _rwkv_kernel,
        grid=(b, t // tb),
        in_specs=[pl.BlockSpec((1, tb, cols), lambda bi, ti: (bi, ti, 0))] + [const(a_) for a_ in args],
        out_specs=pl.BlockSpec((1, tb, dim), lambda bi, ti: (bi, ti, 0)),
        out_shape=jax.ShapeDtypeStruct((b, t, dim), BF16),
        scratch_shapes=[pltpu.VMEM((dim // LANES, LANES, LANES), F32), pltpu.VMEM((1, cols), F32)]
                       + [sc() for _ in range(9)],
        compiler_params=_params(("arbitrary", "arbitrary")),
    )(proj_rw, *args)


def _hgrn_chunk_prepare(ins):
    c = CHUNK
    subs = [(SUB * i, SUB * (i + 1)) for i in range(c // SUB)]

    def scores(q, k, lf, bc, lo, hi):
        m = bc[lo:lo + 1] - lf[lo:lo + 1]
        att = _dot(q[lo:hi] * jnp.exp(bc[lo:hi] - m), k[:hi] * jnp.exp(m - bc[:hi]), NT)
        tt = lax.broadcasted_iota(I32, (SUB, hi), 0) + lo
        s_ = lax.broadcasted_iota(I32, (SUB, hi), 1)
        return jnp.where(s_ <= tt, att, 0.0)

    att = [[scores(q, k, lf, bc, lo, hi) for lo, hi in subs] for q, k, v, lf, bc in ins]
    upd = [_dot(v.T, k * jnp.exp(bc[c - 1:c] - bc)) for q, k, v, lf, bc in ins]
    intra = [[_dot(a, x[2][:hi]) for a, (lo, hi) in zip(arow, subs)] for arow, x in zip(att, ins)]
    return [dict(intra=jnp.concatenate(rows, axis=0), qe=x[0] * jnp.exp(x[4]), upd=up, pc=jnp.exp(x[4][c - 1:c]))
            for rows, x, up in zip(intra, ins, upd)]


def _hgrn_chunk_apply(prep, states):
    o = _each(lambda d, st: d["intra"] + _dot(d["qe"], st, NT), prep, states)
    st_new = _each(lambda d, st: st * d["pc"] + d["upd"], prep, states)
    return list(zip(o, st_new))


def _hgrn_kernel(layer, u_ref, lbl_ref, nw_ref, tri_ref, gsum_ref, yb_ref,
                 st_ref, q_s, k_s, v_s, lf_s, bc_s, o_s):
    tb = pl.program_id(1)

    @pl.when(tb == 0)
    def _():
        st_ref[...] = jnp.zeros_like(st_ref)

    u = u_ref[0]
    nt = u.shape[0]
    dim = q_s.shape[1]
    lbl = lbl_ref[...]
    e = jnp.exp(lbl - jnp.max(lbl, axis=0, keepdims=True))
    lb = jnp.sum(e[0:layer + 1], axis=0, keepdims=True) / jnp.sum(e, axis=0, keepdims=True)
    zf = u[:, dim:2 * dim]
    sig = _sigmoid(zf)
    f = lb + (1.0 - lb) * sig
    qin = u[:, 0:dim]
    q_s[...] = qin * _sigmoid(qin)
    k_s[...] = (1.0 - lb) * (1.0 - sig)
    v_s[...] = u[:, 2 * dim:3 * dim]
    lf = jnp.log(f)
    lf_s[...] = lf
    bc_s[...] = _cumsum_chunks(lf, tri_ref[...])

    nhead = dim // LANES
    lanes = [slice(h * LANES, (h + 1) * LANES) for h in range(nhead)]

    def group_body(gi, carry):
        rows = [pl.ds(pl.multiple_of((gi * HG_WIDE + ci) * CHUNK, CHUNK), CHUNK) for ci in range(HG_WIDE)]
        ins = [(q_s[rw, ls], k_s[rw, ls], v_s[rw, ls], lf_s[rw, ls], bc_s[rw, ls]) for rw in rows for ls in lanes]
        prep = _hgrn_chunk_prepare(ins)
        states = [st_ref[h] for h in range(nhead)]
        for ci, rw in enumerate(rows):
            outs = _hgrn_chunk_apply(prep[ci * nhead:(ci + 1) * nhead], states)
            states = [st_new for _, st_new in outs]
            for (o, _), ls in zip(outs, lanes):
                o_s[rw, ls] = o
        for h in range(nhead):
            st_ref[h] = states[h]
        return carry

    lax.fori_loop(0, nt // (CHUNK * HG_WIDE), group_body, 0)

    o = o_s[...]
    ms = _dot(o * o, gsum_ref[...]) * (1.0 / LANES)
    og = u[:, 3 * dim:4 * dim]
    yb_ref[0] = (o * lax.rsqrt(ms + RMS_EPS) * nw_ref[...] * _sigmoid(og)).astype(BF16)


def _hgrn_branch(proj_hg, lb_logits, norm_w, layer):
    b, t, cols = proj_hg.shape
    dim = cols // 4
    tb = TB_REC
    ii = jnp.arange(tb)
    tri = ((ii[:, None] // CHUNK == ii[None, :] // CHUNK) & (ii[:, None] >= ii[None, :])).astype(BF16)
    jj = jnp.arange(dim)
    gsum = (jj[:, None] // LANES == jj[None, :] // LANES).astype(BF16)
    const = lambda a_: pl.BlockSpec(a_.shape, lambda bi, ti: (0,) * a_.ndim)
    args = [lb_logits, norm_w.reshape(1, -1), tri, gsum]
    sc = lambda: pltpu.VMEM((tb, dim), F32)
    return pl.pallas_call(
        functools.partial(_hgrn_kernel, layer),
        grid=(b, t // tb),
        in_specs=[pl.BlockSpec((1, tb, cols), lambda bi, ti: (bi, ti, 0))] + [const(a_) for a_ in args],
        out_specs=pl.BlockSpec((1, tb, dim), lambda bi, ti: (bi, ti, 0)),
        out_shape=jax.ShapeDtypeStruct((b, t, dim), BF16),
        scratch_shapes=[pltpu.VMEM((dim // LANES, LANES, LANES), F32)] + [sc() for _ in range(6)],
        compiler_params=_params(("arbitrary", "arbitrary")),
    )(proj_hg, *args)


def _merge_kernel(alpha, x_ref, ya_ref, yb_ref, pg_ref, wa_ref, wb_ref, wo_ref, g_ref, b_ref, wr_ref,
                  x1_ref, lt_ref):
    d = x_ref.shape[1]
    gates = _sigmoid(pg_ref[...])
    merged = gates[:, :d] * _dg(ya_ref[...], wa_ref[...]) + gates[:, d:] * _dg(yb_ref[...], wb_ref[...])
    h = alpha * x_ref[...] + _dot(merged, wo_ref[...])
    x1 = _layer_norm(h, g_ref[...], b_ref[...])
    x1_ref[...] = x1
    lt_ref[...] = _dot3(wr_ref[...], x1, NT)


def _merge(xf, ya, yb, pgate, wa, wb, wo, g, bta, wr, alpha):
    n, d = xf.shape
    tm = TM_MERGE
    tile = lambda a_: pl.BlockSpec((tm, a_.shape[1]), lambda i: (i, 0))
    const = lambda a_: pl.BlockSpec(a_.shape, lambda i: (0, 0))
    return pl.pallas_call(
        functools.partial(_merge_kernel, alpha),
        grid=(n // tm,),
        in_specs=[tile(xf), tile(ya), tile(yb), tile(pgate), const(wa), const(wb), const(wo), const(g),
                  const(bta), const(wr)],
        out_specs=[pl.BlockSpec((tm, d), lambda i: (i, 0)), pl.BlockSpec((LANES, tm), lambda i: (0, i))],
        out_shape=[jax.ShapeDtypeStruct((n, d), F32), jax.ShapeDtypeStruct((LANES, n), F32)],
        compiler_params=_params(("parallel",)),
    )(xf, ya, yb, pgate, wa, wb, wo, g, bta, wr)


ROUTER_EXPERT_ROW = 8


def _route_kernel(lt_ref, bias_ref, upper_ref, route_ref, cnt_ref, carry_ref):
    @pl.when(pl.program_id(0) == 0)
    def _():
        carry_ref[...] = jnp.zeros_like(carry_ref)

    ne = N_GROUPS * EXPERTS_PER_GROUP
    lt = lt_ref[...] + bias_ref[...]
    nb = lt.shape[1]
    neg = -jnp.inf
    lg = lt[0:8]
    rg = lax.broadcasted_iota(I32, (8, nb), 0).astype(F32)
    lg = jnp.where(rg < N_GROUPS, lg, neg)
    mg = jnp.max(lg, axis=0, keepdims=True)
    gidx = jnp.min(jnp.where(lg == mg, rg, 1e9), axis=0, keepdims=True)
    pg_sel = 1.0 / jnp.sum(jnp.exp(lg - mg), axis=0, keepdims=True)

    le = lt[ROUTER_EXPERT_ROW:ROUTER_EXPERT_ROW + ne]
    re = lax.broadcasted_iota(I32, (ne, nb), 0).astype(F32)
    in_group = jnp.floor(re * (1.0 / EXPERTS_PER_GROUP)) == gidx
    l1 = jnp.where(in_group, le, neg)
    m1 = jnp.max(l1, axis=0, keepdims=True)
    i1 = jnp.min(jnp.where(l1 == m1, re, 1e9), axis=0, keepdims=True)
    l2 = jnp.where(re == i1, neg, l1)
    m2 = jnp.max(l2, axis=0, keepdims=True)
    i2 = jnp.min(jnp.where(l2 == m2, re, 1e9), axis=0, keepdims=True)
    e2 = jnp.exp(m2 - m1)
    w1 = pg_sel / (1.0 + e2)
    w2 = pg_sel * e2 / (1.0 + e2)

    sel1 = re == i1
    sel2 = re == i2
    onehot = jnp.where(sel1 | sel2, 1.0, 0.0)
    before = _dg(onehot.astype(BF16), upper_ref[...])
    tot = carry_ref[...] + before
    rank1 = jnp.sum(jnp.where(sel1, tot, 0.0), axis=0, keepdims=True)
    rank2 = jnp.sum(jnp.where(sel2, tot, 0.0), axis=0, keepdims=True)
    carry = carry_ref[...] + jnp.sum(onehot, axis=1, keepdims=True)
    carry_ref[...] = carry
    cnt_ref[...] = jnp.broadcast_to(carry, cnt_ref.shape)
    zero = jnp.zeros_like(w1)
    route_ref[...] = jnp.concatenate([i1, i2, rank1, rank2, w1, w2, zero, zero], axis=0)


def _route(lt, bias_col):
    n = lt.shape[1]
    tb = TB_ROUTE
    ne = N_GROUPS * EXPERTS_PER_GROUP
    ii = jnp.arange(tb)
    upper = (ii[:, None] < ii[None, :]).astype(BF16)
    return pl.pallas_call(
        _route_kernel,
        grid=(n // tb,),
        in_specs=[pl.BlockSpec((LANES, tb), lambda i: (0, i)),
                  pl.BlockSpec((LANES, 1), lambda i: (0, 0)),
                  pl.BlockSpec((tb, tb), lambda i: (0, 0))],
        out_specs=[pl.BlockSpec((8, tb), lambda i: (0, i)), pl.BlockSpec((ne, LANES), lambda i: (0, 0))],
        out_shape=[jax.ShapeDtypeStruct((8, n), F32), jax.ShapeDtypeStruct((ne, LANES), F32)],
        scratch_shapes=[pltpu.VMEM((ne, 1), F32)],
        compiler_params=_params(("arbitrary",)),
    )(lt, bias_col, upper)


TAB_LANES = 2 * LANES


def _finalize_kernel(tm, route_ref, cnt_ref, lower_ref, pos_ref, tab_ref):
    ne = cnt_ref.shape[0]
    cnt = cnt_ref[...]
    nb = jnp.floor((cnt + (tm - 1)) * (1.0 / tm))
    bstart = _dg(lower_ref[...], nb.astype(BF16))
    bend = bstart + nb
    pad_start = bstart[:, 0:1] * tm
    route = route_ref[...]
    nt = route.shape[1]
    re = lax.broadcasted_iota(I32, (ne, nt), 0).astype(F32)
    rows = []
    for slot in range(2):
        start = jnp.sum(jnp.where(re == route[slot:slot + 1], pad_start, 0.0), axis=0, keepdims=True)
        rows.append(start + route[2 + slot:3 + slot])
    pos_ref[...] = jnp.concatenate(rows, axis=0).astype(I32)

    nl = tab_ref.shape[1]
    n_used = jnp.max(bend[:, 0:1], axis=0, keepdims=True)
    blk = jnp.minimum(lax.broadcasted_iota(I32, (ne, nl), 1).astype(F32), n_used - 1.0)
    blk_e = jnp.sum(jnp.where(bend[:, 0:1] <= blk, 1.0, 0.0), axis=0, keepdims=True)
    blk_e = jnp.minimum(blk_e, ne - 1.0)
    diag = (lax.broadcasted_iota(I32, (ne, nl), 0) == lax.broadcasted_iota(I32, (ne, nl), 1))
    to_lanes = lambda col: jnp.sum(jnp.where(diag, col, 0.0), axis=0, keepdims=True)
    pad_lo = to_lanes(pad_start + cnt[:, 0:1])
    pad_hi = to_lanes(bend[:, 0:1] * tm)
    zero = jnp.zeros((1, nl), F32)
    tab_ref[...] = jnp.concatenate([blk_e, jnp.broadcast_to(n_used, (1, nl)), pad_lo, pad_hi,
                                    zero, zero, zero, zero], axis=0).astype(I32)


def _finalize(route, cnt, tm):
    n = route.shape[1]
    ne = cnt.shape[0]
    tb = TB_ROUTE
    ii = jnp.arange(ne)
    lower = (ii[:, None] > ii[None, :]).astype(BF16)
    return pl.pallas_call(
        functools.partial(_finalize_kernel, tm),
        grid=(n // tb,),
        in_specs=[pl.BlockSpec((8, tb), lambda i: (0, i)), pl.BlockSpec(cnt.shape, lambda i: (0, 0)),
                  pl.BlockSpec((ne, ne), lambda i: (0, 0))],
        out_specs=[pl.BlockSpec((2, tb), lambda i: (0, i)), pl.BlockSpec((8, TAB_LANES), lambda i: (0, 0))],
        out_shape=[jax.ShapeDtypeStruct((2, n), I32), jax.ShapeDtypeStruct((8, TAB_LANES), I32)],
        compiler_params=_params(("arbitrary",)),
    )(route, cnt, lower)


def _dispatch_kernel(tm, pos_ref, plo_ref, phi_ref, nu_ref, x_ref, xbuf_ref, zblk, sem, zsem):
    i = pl.program_id(0)
    td = x_ref.shape[0]
    n = pl.num_programs(0) * td
    ne = plo_ref.shape[0]
    nblk = xbuf_ref.shape[0] // tm

    def row_copy(j, slot):
        dst = pos_ref[slot * n + i * td + j]
        return pltpu.make_async_copy(x_ref.at[pl.ds(j, 1), :], xbuf_ref.at[pl.ds(dst, 1), :], sem)

    def issue(j, carry):
        row_copy(j, 0).start()
        row_copy(j, 1).start()
        return carry

    lax.fori_loop(0, td, issue, 0, unroll=DMA_UNROLL)

    zero_row = lambda j: pltpu.make_async_copy(zblk.at[pl.ds(0, 1), :], xbuf_ref.at[pl.ds(j, 1), :], zsem)
    zero_blk = lambda b: pltpu.make_async_copy(
        zblk, xbuf_ref.at[pl.ds(pl.multiple_of(b * tm, tm), tm), :], zsem)

    def for_pad_rows(fn):
        def per_expert(e, carry):
            def body(j, c2):
                fn(zero_row(j))
                return c2
            return lax.fori_loop(plo_ref[e], phi_ref[e], body, carry)
        lax.fori_loop(0, ne, per_expert, 0)

        def per_blk(b, carry):
            fn(zero_blk(b))
            return carry
        lax.fori_loop(nu_ref[0], nblk, per_blk, 0)

    @pl.when(i == 0)
    def _():
        zblk[...] = jnp.zeros_like(zblk)
        for_pad_rows(lambda cp: cp.start())
        for_pad_rows(lambda cp: cp.wait())

    for slot in range(2):
        pltpu.make_async_copy(x_ref, xbuf_ref.at[pl.ds(0, td), :], sem).wait()


def _dispatch(pos_flat, pad_lo, pad_hi, n_used, x1, rows, tm):
    n, d = x1.shape
    td = TD_DISPATCH
    return pl.pallas_call(
        functools.partial(_dispatch_kernel, tm),
        grid_spec=pltpu.PrefetchScalarGridSpec(
            num_scalar_prefetch=4,
            grid=(n // td,),
            in_specs=[pl.BlockSpec((td, d), lambda i, *_: (i, 0))],
            out_specs=pl.BlockSpec(memory_space=pl.ANY),
            scratch_shapes=[pltpu.VMEM((tm, d), F32), pltpu.SemaphoreType.DMA(()), pltpu.SemaphoreType.DMA(())],
        ),
        out_shape=jax.ShapeDtypeStruct((rows, d), F32),
        compiler_params=_params(("arbitrary",)),
    )(pos_flat, pad_lo, pad_hi, n_used, x1)


def _expert_kernel(be_ref, nu_ref, x_ref, w1_ref, w3_ref, w2_ref, y_ref, w1b, w3b, w2b):
    i = pl.program_id(0)
    prev = be_ref[jnp.maximum(i - 1, 0)]

    @pl.when((i == 0) | (be_ref[i] != prev))
    def _():
        w1b[...] = w1_ref[0].astype(BF16)
        w3b[...] = w3_ref[0].astype(BF16)
        w2b[...] = w2_ref[0].astype(BF16)

    @pl.when(i < nu_ref[0])
    def _():
        xb = x_ref[...].astype(BF16)
        h1 = _dg(xb, w1b[...])
        h3 = _dg(xb, w3b[...])
        h = (h1 * _sigmoid(h1)) * h3
        y_ref[...] = _dot(h, w2b[...])

    @pl.when(i >= nu_ref[0])
    def _():
        y_ref[...] = jnp.zeros_like(y_ref)


def _experts(blk_e, n_used, xbuf, w1, w3, w2):
    rows, d = xbuf.shape
    de = w1.shape[2]
    tm = TM_EXPERT
    row_map = lambda i, be, nu: (jnp.minimum(i, nu[0] - 1), 0)
    return pl.pallas_call(
        _expert_kernel,
        grid_spec=pltpu.PrefetchScalarGridSpec(
            num_scalar_prefetch=2,
            grid=(rows // tm,),
            in_specs=[pl.BlockSpec((tm, d), row_map),
                      pl.BlockSpec((1, d, de), lambda i, be, nu: (be[i], 0, 0)),
                      pl.BlockSpec((1, d, de), lambda i, be, nu: (be[i], 0, 0)),
                      pl.BlockSpec((1, de, d), lambda i, be, nu: (be[i], 0, 0))],
            out_specs=pl.BlockSpec((tm, d), lambda i, be, nu: (i, 0)),
            scratch_shapes=[pltpu.VMEM((d, de), BF16), pltpu.VMEM((d, de), BF16), pltpu.VMEM((de, d), BF16)],
        ),
        out_shape=jax.ShapeDtypeStruct((rows, d), F32),
        compiler_params=_params(("arbitrary",)),
    )(blk_e, n_used, xbuf, w1, w3, w2)


def _combine_kernel(alpha, pos_ref, x1_ref, ew_ref, p_ref, wpe_ref, wpg_ref, g_ref, b_ref, ybuf_ref,
                    out_ref, gbuf, sem):
    i = pl.program_id(0)
    nsteps = pl.num_programs(0)
    tc = x1_ref.shape[0]
    n = nsteps * tc

    def gather(step, buf):
        def body(j, carry):
            for slot in range(2):
                src = pos_ref[slot * n + step * tc + j]
                pltpu.make_async_copy(ybuf_ref.at[pl.ds(src, 1), :], gbuf.at[buf, slot, pl.ds(j, 1), :],
                                      sem.at[buf]).start()
            return carry
        lax.fori_loop(0, tc, body, 0, unroll=DMA_UNROLL)

    @pl.when(i == 0)
    def _():
        gather(0, 0)

    @pl.when(i + 1 < nsteps)
    def _():
        gather(i + 1, (i + 1) % 2)

    cur = i % 2
    for slot in range(2):
        pltpu.make_async_copy(ybuf_ref.at[pl.ds(0, tc), :], gbuf.at[cur, slot], sem.at[cur]).wait()

    ew = ew_ref[...]
    ffn = gbuf[cur, 0] * ew[:, 0:1] + gbuf[cur, 1] * ew[:, 1:2]
    x2 = _layer_norm(alpha * x1_ref[...] + ffn, g_ref[...], b_ref[...])
    gate = _sigmoid(_dot(x2, wpg_ref[...]))
    out_ref[...] = x2 + gate * _dot(p_ref[...], wpe_ref[...])


def _combine(pos_flat, x1, ew_t, pf, wpe, wpg, g, bta, ybuf, alpha):
    n, d = x1.shape
    tc = TC_COMBINE
    tile = lambda a_: pl.BlockSpec((tc, a_.shape[1]), lambda i, pos: (i, 0))
    const = lambda a_: pl.BlockSpec(a_.shape, lambda i, pos: (0, 0))
    return pl.pallas_call(
        functools.partial(_combine_kernel, alpha),
        grid_spec=pltpu.PrefetchScalarGridSpec(
            num_scalar_prefetch=1,
            grid=(n // tc,),
            in_specs=[tile(x1), tile(ew_t), tile(pf), const(wpe), const(wpg), const(g), const(bta),
                      pl.BlockSpec(memory_space=pl.ANY)],
            out_specs=pl.BlockSpec((tc, d), lambda i, pos: (i, 0)),
            scratch_shapes=[pltpu.VMEM((2, 2, tc, d), F32), pltpu.SemaphoreType.DMA((2,))],
        ),
        out_shape=jax.ShapeDtypeStruct((n, d), F32),
        compiler_params=_params(("arbitrary",)),
    )(pos_flat, x1, ew_t, pf, wpe, wpg, g, bta, ybuf)


def _layer(x, p_i, w_in, rw_mu, rw_w0, rw_w_up, rw_a0, rw_a_up, rw_g_up, rw_k_k, rw_k_a, rw_r_k, rw_gn_w,
           rw_gn_b, w_a_out, hg_lb_logits, hg_norm_w, w_b_out, w_o, ln1_g, ln1_b, router_g_w, router_g_b,
           router_e_w, router_e_b, w1, w3, w2, ln2_g, ln2_b, w_pe, w_pg, alpha, layer):
    b, t, d = x.shape
    n = b * t
    rw_dim = rw_w0.shape[0]
    rw_cols = 3 * rw_dim + RW_DECAY_LORA + RW_A_LORA + RW_GATE_LORA
    hg_cols = 4 * hg_norm_w.shape[0]
    ne = N_GROUPS * EXPERTS_PER_GROUP
    row2 = lambda a_: a_.reshape(1, -1)
    xf = x.reshape(n, d)

    wb = w_in.astype(BF16)
    proj_rw, proj_hg, proj_gt = _project(xf, wb[:, :rw_cols], wb[:, rw_cols:rw_cols + hg_cols],
                                         wb[:, rw_cols + hg_cols:])
    ya = _rwkv_branch(proj_rw.reshape(b, t, rw_cols), rw_mu, row2(rw_w0), rw_w_up, rw_a0, rw_a_up, rw_g_up,
                      rw_k_k, rw_k_a, rw_r_k, rw_gn_w, rw_gn_b)
    yb = _hgrn_branch(proj_hg.reshape(b, t, hg_cols), hg_lb_logits, hg_norm_w, layer)

    wr = jnp.zeros((LANES, d), F32)
    wr = wr.at[:N_GROUPS].set(router_g_w.T).at[ROUTER_EXPERT_ROW:ROUTER_EXPERT_ROW + ne].set(router_e_w.T)
    bias = jnp.zeros((LANES,), F32)
    bias = bias.at[:N_GROUPS].set(router_g_b).at[ROUTER_EXPERT_ROW:ROUTER_EXPERT_ROW + ne].set(router_e_b)
    x1, lt = _merge(xf, ya.reshape(n, -1), yb.reshape(n, -1), proj_gt, w_a_out.astype(BF16),
                    w_b_out.astype(BF16), w_o.astype(BF16), row2(ln1_g), row2(ln1_b), wr, alpha)
    route, cnt = _route(lt, bias.reshape(LANES, 1))

    tm = TM_EXPERT
    nblk = (n * 2) // tm + ne
    assert nblk <= TAB_LANES
    pos2, tab = _finalize(route, cnt, tm)
    pos = pos2.reshape(-1)
    blk_e, n_used = tab[0, :nblk], tab[1, :1]

    xbuf = _dispatch(pos, tab[2, :ne], tab[3, :ne], n_used, x1, nblk * tm, tm)
    ybuf = _experts(blk_e, n_used, xbuf, w1, w3, w2)
    out = _combine(pos, x1, route[4:6].T, p_i.reshape(n, -1), w_pe.astype(BF16), w_pg.astype(BF16),
                   row2(ln2_g), row2(ln2_b), ybuf, alpha)
    return out.reshape(b, t, d)


def kernel(x, p, w_in, rw_mu, rw_w0, rw_w_up, rw_a0, rw_a_up, rw_g_up, rw_k_k, rw_k_a, rw_r_k, rw_gn_w, rw_gn_b,
           w_a_out, hg_lb_logits, hg_norm_w, w_b_out, w_o, ln1_g, ln1_b, router_g_w, router_g_b, router_e_w,
           router_e_b, w1, w3, w2, ln2_g, ln2_b, w_pe, w_pg):
    depth = w_in.shape[0]
    alpha = (2 * depth) ** 0.25
    for i in range(depth):
        x = _layer(x, p[i], w_in[i], rw_mu[i], rw_w0[i], rw_w_up[i], rw_a0[i], rw_a_up[i], rw_g_up[i], rw_k_k[i],
                   rw_k_a[i], rw_r_k[i].reshape(-1), rw_gn_w[i], rw_gn_b[i], w_a_out[i], hg_lb_logits,
                   hg_norm_w[i], w_b_out[i], w_o[i], ln1_g[i], ln1_b[i],
                   router_g_w[i], router_g_b[i], router_e_w[i], router_e_b[i], w1[i], w3[i], w2[i], ln2_g[i],
                   ln2_b[i], w_pe[i], w_pg[i], alpha, i)
    return x
```

```python
import functools

import jax
import jax.numpy as jnp
from jax import lax
from jax.experimental import pallas as pl
from jax.experimental.pallas import tpu as pltpu

F32 = jnp.float32
BF16 = jnp.bfloat16
I32 = jnp.int32

NN = (((1,), (0,)), ((), ()))
NT = (((1,), (1,)), ((), ()))

RW_HEAD = 64
RW_DECAY_LORA = 64
RW_A_LORA = 64
RW_GATE_LORA = 128
RW_GN_EPS = 64e-5
RW_DECAY_SCALE = 0.6065306597126334
HG_HEADS = 4
N_GROUPS = 4
EXPERTS_PER_GROUP = 8
LN_EPS = 1e-5
RMS_EPS = 1e-6

CHUNK = 64
SUB = 16
LANES = 128
VMEM_LIMIT = 56 * 1024 * 1024

TM_PROJ = 256
TB_REC = 256
TM_MERGE = 512
TB_ROUTE = 512
TD_DISPATCH = 512
SEG_ALIGN = 8
TM_EXPERT = 256
RW_WIDE = 4
HG_WIDE = 4


def _dg(a, b, dn=NN):
    return lax.dot_general(a, b, dn, preferred_element_type=F32)


def _dot(a, b, dn=NN):
    return _dg(a.astype(BF16), b.astype(BF16), dn)


def _split(a):
    hi = a.astype(BF16)
    lo = (a - hi.astype(F32)).astype(BF16)
    return hi, lo


def _dot_hl(a, b_exact, dn=NN):
    hi, lo = _split(a)
    return _dg(hi, b_exact, dn) + _dg(lo, b_exact, dn)


def _dot3(a, b, dn=NN):
    ah, al = _split(a)
    bh, bl = _split(b)
    return _dg(ah, bh, dn) + (_dg(ah, bl, dn) + _dg(al, bh, dn))


def _cumsum_chunks(x, tri):
    h = x.astype(BF16)
    r1 = x - h.astype(F32)
    m = r1.astype(BF16)
    l = (r1 - m.astype(F32)).astype(BF16)
    return _dg(tri, h) + (_dg(tri, m) + _dg(tri, l))


def _sigmoid(x):
    return 0.5 * jnp.tanh(0.5 * x) + 0.5


def _layer_norm(h, g, b):
    mu = jnp.mean(h, axis=-1, keepdims=True)
    d = h - mu
    var = jnp.mean(d * d, axis=-1, keepdims=True)
    return d * lax.rsqrt(var + LN_EPS) * g + b


def _params(sem):
    return pltpu.CompilerParams(dimension_semantics=sem, vmem_limit_bytes=VMEM_LIMIT)


def _proj_kernel(x_ref, wr_ref, wh_ref, wg_ref, pr_ref, ph_ref, pg_ref):
    xb = x_ref[...].astype(BF16)
    pr_ref[...] = _dg(xb, wr_ref[...])
    ph_ref[...] = _dg(xb, wh_ref[...])
    pg_ref[...] = _dg(xb, wg_ref[...])


def _project(xf, w_rw, w_hg, w_gt):
    n, d = xf.shape
    tm = TM_PROJ
    full = lambda w: pl.BlockSpec(w.shape, lambda i: (0, 0))
    tile = lambda c: pl.BlockSpec((tm, c), lambda i: (i, 0))
    return pl.pallas_call(
        _proj_kernel,
        grid=(n // tm,),
        in_specs=[tile(d), full(w_rw), full(w_hg), full(w_gt)],
        out_specs=[tile(w_rw.shape[1]), tile(w_hg.shape[1]), tile(w_gt.shape[1])],
        out_shape=[jax.ShapeDtypeStruct((n, w.shape[1]), F32) for w in (w_rw, w_hg, w_gt)],
        compiler_params=_params(("parallel",)),
    )(xf, w_rw, w_hg, w_gt)


def _each(f, *ls):
    return [f(*xs) for xs in zip(*ls)]


def _two(x):
    m1 = lax.broadcasted_iota(I32, x.shape, 1) < RW_HEAD
    return jnp.concatenate([jnp.where(m1, x, 0.0), jnp.where(m1, 0.0, x)], axis=0)


def _rwkv_chunk_prepare(ins, lvl_ref):
    c = CHUNK
    lane = lax.broadcasted_iota(I32, (c, LANES), 1)
    row = lax.broadcasted_iota(I32, (c, LANES), 0)
    scol = jnp.bitwise_and(lane, RW_HEAD - 1)
    strict = row > scol
    incl = row >= scol
    r2 = lax.broadcasted_iota(I32, (LANES, LANES), 0)
    c2 = lax.broadcasted_iota(I32, (LANES, LANES), 1)
    eye = jnp.where(r2 == c2, 1.0, 0.0).astype(F32)

    def prep(r, k, v, av, bv, lw, lc):
        l_end = lc[c - 1:c]
        e_r = jnp.exp(l_end - lc)
        e_n = jnp.exp(-lc)
        return dict(at=av * jnp.exp(lc - lw), rt=r * jnp.exp(lc), bt=bv * e_n, kt=k * e_n,
                    bk=jnp.concatenate([bv * e_r, k * e_r], axis=0), pc=jnp.exp(l_end), v=v)

    q = [prep(*xs) for xs in ins]
    p = [_dot(jnp.concatenate([d["at"], d["rt"]], axis=0),
              jnp.concatenate([_two(d["bt"]), _two(d["kt"])], axis=0), NT) for d in q]
    sab = [jnp.where(strict, x[:c, :LANES], 0.0) for x in p]
    sak = [jnp.where(strict, x[:c, LANES:], 0.0) for x in p]
    srb = [jnp.where(incl, x[c:, :LANES], 0.0) for x in p]
    srk = [jnp.where(incl, x[c:, LANES:], 0.0) for x in p]
    sv = _each(lambda ak, rk, d: _dot(jnp.concatenate([ak, rk], axis=0), _two(d["v"])), sak, srk, q)

    a_bd = [_two(x) for x in sab]
    t = [eye + a * lvl_ref[0] for a in a_bd]
    for lv in range(1, lvl_ref.shape[0]):
        ta = _each(lambda t_, a: _dot(t_, a * lvl_ref[lv]), t, a_bd)
        t = _each(lambda t_, ta_: t_ + _dot(ta_, t_), t, ta)

    x = _each(lambda t_, d, sv_: _dot(t_, jnp.concatenate([_two(d["at"]), _two(sv_[:c])], axis=1)), t, q, sv)
    return [dict(uk=x_[:c, :LANES] + x_[c:, :LANES],
                 w=x_[:c, LANES:] + x_[c:, LANES:],
                 rt=d["rt"], rkv=sv_[c:], srb=srb_, bk=d["bk"], v=d["v"], pc=d["pc"])
            for x_, d, sv_, srb_ in zip(x, q, sv, srb)]


def _rwkv_chunk_apply(prep, states, bd):
    c = CHUNK
    g1 = _each(lambda d, s: _dot(jnp.concatenate([d["uk"], d["rt"]], axis=0), s, NT), prep, states)
    u = _each(lambda g, d: g[:c] + d["w"], g1, prep)
    y = _each(lambda g, d, u_: g[c:] + d["rkv"] + _dot(d["srb"], _two(u_)), g1, prep, u)
    upd = _each(lambda u_, d: _dot(jnp.concatenate([u_, d["v"]], axis=0).T, d["bk"]), u, prep)
    s_new = _each(lambda d, s, up: s * d["pc"] + bd * up, prep, states, upd)
    return list(zip(y, s_new))


def _rwkv_kernel(u_ref, mu_ref, w0_ref, wup_ref, a0_ref, aup_ref, gup_ref, kk_ref, ka_ref, rk_ref,
                 gnw_ref, gnb_ref, tri_ref, gsum_ref, lvl_ref, bd_ref, ya_ref,
                 s_ref, prev_ref, r_s, k_s, v_s, a_s, b_s, lw_s, lc_s, g_s, y_s):
    tb = pl.program_id(1)

    @pl.when(tb == 0)
    def _():
        s_ref[...] = jnp.zeros_like(s_ref)
        prev_ref[...] = jnp.zeros_like(prev_ref)

    u = u_ref[0]
    nt = u.shape[0]
    dim = r_s.shape[1]
    rowid = lax.broadcasted_iota(I32, u.shape, 0)
    shifted = jnp.where(rowid == 0, prev_ref[...], pltpu.roll(u, 1, axis=0))
    prev_ref[...] = u[nt - 1:nt, :]
    um = u + (shifted - u) * mu_ref[...]

    r = um[:, 0:dim]
    k = um[:, dim:2 * dim]
    v = um[:, 2 * dim:3 * dim]
    xwa = um[:, 3 * dim:3 * dim + LANES]
    xg = um[:, 3 * dim + LANES:3 * dim + 2 * LANES]

    wpre = w0_ref[...] + _dot(jnp.tanh(xwa), wup_ref[...])
    lw = -RW_DECAY_SCALE * _sigmoid(wpre)
    a = _sigmoid(a0_ref[...] + _dot(xwa, aup_ref[...]))
    g_s[...] = _dot(_sigmoid(xg), gup_ref[...])
    kk = k * kk_ref[...]
    ss = _dot(kk * kk, gsum_ref[...])
    kk = kk * lax.rsqrt(jnp.maximum(ss, 1e-24))
    r_s[...] = r
    k_s[...] = k * (1.0 + (a - 1.0) * ka_ref[...])
    v_s[...] = v
    a_s[...] = -kk
    b_s[...] = kk * a
    lw_s[...] = lw
    lc_s[...] = _cumsum_chunks(lw, tri_ref[...])
    bd = bd_ref[...]

    npair = dim // LANES
    lanes = [slice(p * LANES, (p + 1) * LANES) for p in range(npair)]

    def group_body(gi, carry):
        rows = [pl.ds(pl.multiple_of((gi * RW_WIDE + ci) * CHUNK, CHUNK), CHUNK) for ci in range(RW_WIDE)]
        ins = [(r_s[rw, ls], k_s[rw, ls], v_s[rw, ls], a_s[rw, ls], b_s[rw, ls], lw_s[rw, ls], lc_s[rw, ls])
               for rw in rows for ls in lanes]
        prep = _rwkv_chunk_prepare(ins, lvl_ref)
        states = [s_ref[p] for p in range(npair)]
        for ci, rw in enumerate(rows):
            outs = _rwkv_chunk_apply(prep[ci * npair:(ci + 1) * npair], states, bd)
            states = [s_new for _, s_new in outs]
            for (y, _), ls in zip(outs, lanes):
                y_s[rw, ls] = y
        for p in range(npair):
            s_ref[p] = states[p]
        return carry

    lax.fori_loop(0, nt // (CHUNK * RW_WIDE), group_body, 0)

    y = y_s[...]
    gsum = gsum_ref[...]
    inv_n = 1.0 / RW_HEAD
    m = _dot_hl(y, gsum) * inv_n
    d = y - m
    var = _dot(d * d, gsum) * inv_n
    yn = d * lax.rsqrt(var + RW_GN_EPS) * gnw_ref[...] + gnb_ref[...]
    bonus = _dot(r_s[...] * k_s[...] * rk_ref[...], gsum) * v_s[...]
    ya_ref[0] = ((yn + bonus) * g_s[...]).astype(BF16)


def _rwkv_branch(proj_rw, mu, w0, wup, a0, aup, gup, k_k, k_a, r_k, gn_w, gn_b):
    b, t, cols = proj_rw.shape
    dim = w0.shape[1]
    tb = TB_REC
    ii = jnp.arange(tb)
    tri = ((ii[:, None] // CHUNK == ii[None, :] // CHUNK) & (ii[:, None] >= ii[None, :])).astype(BF16)
    jj = jnp.arange(dim)
    gsum = (jj[:, None] // RW_HEAD == jj[None, :] // RW_HEAD).astype(BF16)
    rr = jnp.arange(LANES)[:, None]
    cc = jnp.arange(LANES)[None, :]
    lvls = []
    s = 1
    while s < CHUNK:
        lvls.append(((rr // (2 * s) == cc // (2 * s)) & ((rr // s) % 2 == 1) & ((cc // s) % 2 == 0)).astype(F32))
        s *= 2
    lvl = jnp.stack(lvls)
    bd = (rr // RW_HEAD == cc // RW_HEAD).astype(F32)
    zpad = lambda rows: jnp.zeros((rows, dim), F32)
    wup_p = jnp.concatenate([wup, zpad(LANES - wup.shape[0])], axis=0).astype(BF16)
    aup_p = jnp.concatenate([zpad(LANES - aup.shape[0]), aup], axis=0).astype(BF16)
    row2 = lambda a_: a_.reshape(1, -1)
    const = lambda a_: pl.BlockSpec(a_.shape, lambda bi, ti: (0,) * a_.ndim)
    args = [row2(mu), row2(w0), wup_p, row2(a0), aup_p, gup.astype(BF16), row2(k_k), row2(k_a), row2(r_k),
            row2(gn_w), row2(gn_b), tri, gsum, lvl, bd]
    sc = lambda: pltpu.VMEM((tb, dim), F32)
    return pl.pallas_call(
        _rwkv_kernel,
        grid=(b, t // tb),
        in_specs=[pl.BlockSpec((1, tb, cols), lambda bi, ti: (bi, ti, 0))] + [const(a_) for a_ in args],
        out_specs=pl.BlockSpec((1, tb, dim), lambda bi, ti: (bi, ti, 0)),
        out_shape=jax.ShapeDtypeStruct((b, t, dim), BF16),
        scratch_shapes=[pltpu.VMEM((dim // LANES, LANES, LANES), F32), pltpu.VMEM((1, cols), F32)]
                       + [sc() for _ in range(9)],
        compiler_params=_params(("arbitrary", "arbitrary")),
    )(proj_rw, *args)


def _hgrn_chunk_prepare(ins):
    c = CHUNK
    subs = [(SUB * i, SUB * (i + 1)) for i in range(c // SUB)]

    def scores(q, k, lf, bc, lo, hi):
        m = bc[lo:lo + 1] - lf[lo:lo + 1]
        att = _dot(q[lo:hi] * jnp.exp(bc[lo:hi] - m), k[:hi] * jnp.exp(m - bc[:hi]), NT)
        tt = lax.broadcasted_iota(I32, (SUB, hi), 0) + lo
        s_ = lax.broadcasted_iota(I32, (SUB, hi), 1)
        return jnp.where(s_ <= tt, att, 0.0)

    att = [[scores(q, k, lf, bc, lo, hi) for lo, hi in subs] for q, k, v, lf, bc in ins]
    upd = [_dot(v.T, k * jnp.exp(bc[c - 1:c] - bc)) for q, k, v, lf, bc in ins]
    intra = [[_dot(a, x[2][:hi]) for a, (lo, hi) in zip(arow, subs)] for arow, x in zip(att, ins)]
    return [dict(intra=jnp.concatenate(rows, axis=0), qe=x[0] * jnp.exp(x[4]), upd=up, pc=jnp.exp(x[4][c - 1:c]))
            for rows, x, up in zip(intra, ins, upd)]


def _hgrn_chunk_apply(prep, states):
    o = _each(lambda d, st: d["intra"] + _dot(d["qe"], st, NT), prep, states)
    st_new = _each(lambda d, st: st * d["pc"] + d["upd"], prep, states)
    return list(zip(o, st_new))


def _hgrn_kernel(layer, u_ref, lbl_ref, nw_ref, tri_ref, gsum_ref, yb_ref,
                 st_ref, q_s, k_s, v_s, lf_s, bc_s, o_s):
    tb = pl.program_id(1)

    @pl.when(tb == 0)
    def _():
        st_ref[...] = jnp.zeros_like(st_ref)

    u = u_ref[0]
    nt = u.shape[0]
    dim = q_s.shape[1]
    lbl = lbl_ref[...]
    e = jnp.exp(lbl - jnp.max(lbl, axis=0, keepdims=True))
    lb = jnp.sum(e[0:layer + 1], axis=0, keepdims=True) / jnp.sum(e, axis=0, keepdims=True)
    zf = u[:, dim:2 * dim]
    sig = _sigmoid(zf)
    f = lb + (1.0 - lb) * sig
    qin = u[:, 0:dim]
    q_s[...] = qin * _sigmoid(qin)
    k_s[...] = (1.0 - lb) * (1.0 - sig)
    v_s[...] = u[:, 2 * dim:3 * dim]
    lf = jnp.log(f)
    lf_s[...] = lf
    bc_s[...] = _cumsum_chunks(lf, tri_ref[...])

    nhead = dim // LANES
    lanes = [slice(h * LANES, (h + 1) * LANES) for h in range(nhead)]

    def group_body(gi, carry):
        rows = [pl.ds(pl.multiple_of((gi * HG_WIDE + ci) * CHUNK, CHUNK), CHUNK) for ci in range(HG_WIDE)]
        ins = [(q_s[rw, ls], k_s[rw, ls], v_s[rw, ls], lf_s[rw, ls], bc_s[rw, ls]) for rw in rows for ls in lanes]
        prep = _hgrn_chunk_prepare(ins)
        states = [st_ref[h] for h in range(nhead)]
        for ci, rw in enumerate(rows):
            outs = _hgrn_chunk_apply(prep[ci * nhead:(ci + 1) * nhead], states)
            states = [st_new for _, st_new in outs]
            for (o, _), ls in zip(outs, lanes):
                o_s[rw, ls] = o
        for h in range(nhead):
            st_ref[h] = states[h]
        return carry

    lax.fori_loop(0, nt // (CHUNK * HG_WIDE), group_body, 0)

    o = o_s[...]
    ms = _dot(o * o, gsum_ref[...]) * (1.0 / LANES)
    og = u[:, 3 * dim:4 * dim]
    yb_ref[0] = (o * lax.rsqrt(ms + RMS_EPS) * nw_ref[...] * _sigmoid(og)).astype(BF16)


def _hgrn_branch(proj_hg, lb_logits, norm_w, layer):
    b, t, cols = proj_hg.shape
    dim = cols // 4
    tb = TB_REC
    ii = jnp.arange(tb)
    tri = ((ii[:, None] // CHUNK == ii[None, :] // CHUNK) & (ii[:, None] >= ii[None, :])).astype(BF16)
    jj = jnp.arange(dim)
    gsum = (jj[:, None] // LANES == jj[None, :] // LANES).astype(BF16)
    const = lambda a_: pl.BlockSpec(a_.shape, lambda bi, ti: (0,) * a_.ndim)
    args = [lb_logits, norm_w.reshape(1, -1), tri, gsum]
    sc = lambda: pltpu.VMEM((tb, dim), F32)
    return pl.pallas_call(
        functools.partial(_hgrn_kernel, layer),
        grid=(b, t // tb),
        in_specs=[pl.BlockSpec((1, tb, cols), lambda bi, ti: (bi, ti, 0))] + [const(a_) for a_ in args],
        out_specs=pl.BlockSpec((1, tb, dim), lambda bi, ti: (bi, ti, 0)),
        out_shape=jax.ShapeDtypeStruct((b, t, dim), BF16),
        scratch_shapes=[pltpu.VMEM((dim // LANES, LANES, LANES), F32)] + [sc() for _ in range(6)],
        compiler_params=_params(("arbitrary", "arbitrary")),
    )(proj_hg, *args)


def _merge_kernel(alpha, x_ref, ya_ref, yb_ref, pg_ref, wa_ref, wb_ref, wo_ref, g_ref, b_ref, wr_ref,
                  x1_ref, lt_ref):
    d = x_ref.shape[1]
    gates = _sigmoid(pg_ref[...])
    merged = gates[:, :d] * _dg(ya_ref[...], wa_ref[...]) + gates[:, d:] * _dg(yb_ref[...], wb_ref[...])
    h = alpha * x_ref[...] + _dot(merged, wo_ref[...])
    x1 = _layer_norm(h, g_ref[...], b_ref[...])
    x1_ref[...] = x1
    lt_ref[...] = _dot3(wr_ref[...], x1, NT)


def _merge(xf, ya, yb, pgate, wa, wb, wo, g, bta, wr, alpha):
    n, d = xf.shape
    tm = TM_MERGE
    tile = lambda a_: pl.BlockSpec((tm, a_.shape[1]), lambda i: (i, 0))
    const = lambda a_: pl.BlockSpec(a_.shape, lambda i: (0, 0))
    return pl.pallas_call(
        functools.partial(_merge_kernel, alpha),
        grid=(n // tm,),
        in_specs=[tile(xf), tile(ya), tile(yb), tile(pgate), const(wa), const(wb), const(wo), const(g),
                  const(bta), const(wr)],
        out_specs=[pl.BlockSpec((tm, d), lambda i: (i, 0)), pl.BlockSpec((LANES, tm), lambda i: (0, i))],
        out_shape=[jax.ShapeDtypeStruct((n, d), F32), jax.ShapeDtypeStruct((LANES, n), F32)],
        compiler_params=_params(("parallel",)),
    )(xf, ya, yb, pgate, wa, wb, wo, g, bta, wr)


ROUTER_EXPERT_ROW = 8


def _to_lanes(col, nl):
    ne = col.shape[0]
    diag = lax.broadcasted_iota(I32, (ne, nl), 0) == lax.broadcasted_iota(I32, (ne, nl), 1)
    return jnp.sum(jnp.where(diag, col, 0.0), axis=0, keepdims=True)


def _route_kernel(lt_ref, bias_ref, upper_ref, lower_ref, route_ref, seg_ref, cnt_ref, carry_ref):
    @pl.when(pl.program_id(0) == 0)
    def _():
        carry_ref[...] = jnp.zeros_like(carry_ref)

    ne = N_GROUPS * EXPERTS_PER_GROUP
    lt = lt_ref[...] + bias_ref[...]
    nb = lt.shape[1]
    neg = -jnp.inf
    lg = lt[0:8]
    rg = lax.broadcasted_iota(I32, (8, nb), 0).astype(F32)
    lg = jnp.where(rg < N_GROUPS, lg, neg)
    mg = jnp.max(lg, axis=0, keepdims=True)
    gidx = jnp.min(jnp.where(lg == mg, rg, 1e9), axis=0, keepdims=True)
    pg_sel = 1.0 / jnp.sum(jnp.exp(lg - mg), axis=0, keepdims=True)

    le = lt[ROUTER_EXPERT_ROW:ROUTER_EXPERT_ROW + ne]
    re = lax.broadcasted_iota(I32, (ne, nb), 0).astype(F32)
    in_group = jnp.floor(re * (1.0 / EXPERTS_PER_GROUP)) == gidx
    l1 = jnp.where(in_group, le, neg)
    m1 = jnp.max(l1, axis=0, keepdims=True)
    i1 = jnp.min(jnp.where(l1 == m1, re, 1e9), axis=0, keepdims=True)
    l2 = jnp.where(re == i1, neg, l1)
    m2 = jnp.max(l2, axis=0, keepdims=True)
    i2 = jnp.min(jnp.where(l2 == m2, re, 1e9), axis=0, keepdims=True)
    e2 = jnp.exp(m2 - m1)
    w1 = pg_sel / (1.0 + e2)
    w2 = pg_sel * e2 / (1.0 + e2)

    sel1 = re == i1
    sel2 = re == i2
    onehot = jnp.where(sel1 | sel2, 1.0, 0.0)
    before = _dg(onehot.astype(BF16), upper_ref[...])
    cnt_t = jnp.sum(onehot, axis=1, keepdims=True)
    seg = jnp.floor((cnt_t + (SEG_ALIGN - 1)) * (1.0 / SEG_ALIGN)) * SEG_ALIGN
    lstart = _dg(lower_ref[...], jnp.broadcast_to(seg, (ne, LANES)).astype(BF16))[:, 0:1]
    tot = lstart + before
    lpos1 = jnp.sum(jnp.where(sel1, tot, 0.0), axis=0, keepdims=True)
    lpos2 = jnp.sum(jnp.where(sel2, tot, 0.0), axis=0, keepdims=True)
    grel = carry_ref[...]
    carry = grel + seg
    carry_ref[...] = carry
    cnt_ref[...] = jnp.broadcast_to(carry, cnt_ref.shape)
    zero = jnp.zeros_like(w1)
    route_ref[...] = jnp.concatenate([i1, i2, lpos1, lpos2, w1, w2, zero, zero], axis=0)
    nl = seg_ref.shape[2]
    zl = jnp.zeros((1, nl), F32)
    ltot = jnp.broadcast_to(jnp.sum(seg, axis=0, keepdims=True), (1, nl))
    seg_ref[0] = jnp.concatenate([_to_lanes(seg, nl), _to_lanes(lstart, nl), _to_lanes(grel, nl), ltot,
                                  zl, zl, zl, zl], axis=0)


def _route(lt, bias_col):
    n = lt.shape[1]
    tb = TB_ROUTE
    ne = N_GROUPS * EXPERTS_PER_GROUP
    ii = jnp.arange(tb)
    upper = (ii[:, None] < ii[None, :]).astype(BF16)
    ee = jnp.arange(ne)
    lower = (ee[:, None] > ee[None, :]).astype(BF16)
    return pl.pallas_call(
        _route_kernel,
        grid=(n // tb,),
        in_specs=[pl.BlockSpec((LANES, tb), lambda i: (0, i)),
                  pl.BlockSpec((LANES, 1), lambda i: (0, 0)),
                  pl.BlockSpec((tb, tb), lambda i: (0, 0)),
                  pl.BlockSpec((ne, ne), lambda i: (0, 0))],
        out_specs=[pl.BlockSpec((8, tb), lambda i: (0, i)), pl.BlockSpec((1, 8, LANES), lambda i: (i, 0, 0)),
                   pl.BlockSpec((ne, LANES), lambda i: (0, 0))],
        out_shape=[jax.ShapeDtypeStruct((8, n), F32), jax.ShapeDtypeStruct((n // tb, 8, LANES), F32),
                   jax.ShapeDtypeStruct((ne, LANES), F32)],
        scratch_shapes=[pltpu.VMEM((ne, 1), F32)],
        compiler_params=_params(("arbitrary",)),
    )(lt, bias_col, upper, lower)


TAB_LANES = 2 * LANES


def _finalize_kernel(tm, seg_ref, cnt_ref, lower_ref, segtab_ref, tab_ref):
    ne = cnt_ref.shape[0]
    cnt = cnt_ref[...]
    nb = jnp.floor((cnt + (tm - 1)) * (1.0 / tm))
    bstart = _dg(lower_ref[...], nb.astype(BF16))
    bend = bstart + nb
    pad_start = bstart[:, 0:1] * tm
    seg = seg_ref[0]
    gstart = seg[2:3] + _to_lanes(pad_start, seg.shape[1])
    segtab_ref[0] = jnp.concatenate([seg[0:2], gstart, seg[3:8]], axis=0).astype(I32)

    nl = tab_ref.shape[1]
    n_used = jnp.max(bend[:, 0:1], axis=0, keepdims=True)
    blk = jnp.minimum(lax.broadcasted_iota(I32, (ne, nl), 1).astype(F32), n_used - 1.0)
    blk_e = jnp.sum(jnp.where(bend[:, 0:1] <= blk, 1.0, 0.0), axis=0, keepdims=True)
    blk_e = jnp.minimum(blk_e, ne - 1.0)
    pad_lo = _to_lanes(pad_start + cnt[:, 0:1], nl)
    pad_hi = _to_lanes(bend[:, 0:1] * tm, nl)
    zero = jnp.zeros((1, nl), F32)
    tab_ref[...] = jnp.concatenate([blk_e, jnp.broadcast_to(n_used, (1, nl)), pad_lo, pad_hi,
                                    zero, zero, zero, zero], axis=0).astype(I32)


def _finalize(seg, cnt, tm):
    ntile = seg.shape[0]
    ne = cnt.shape[0]
    ii = jnp.arange(ne)
    lower = (ii[:, None] > ii[None, :]).astype(BF16)
    return pl.pallas_call(
        functools.partial(_finalize_kernel, tm),
        grid=(ntile,),
        in_specs=[pl.BlockSpec((1, 8, LANES), lambda i: (i, 0, 0)), pl.BlockSpec(cnt.shape, lambda i: (0, 0)),
                  pl.BlockSpec((ne, ne), lambda i: (0, 0))],
        out_specs=[pl.BlockSpec((1, 8, LANES), lambda i: (i, 0, 0)), pl.BlockSpec((8, TAB_LANES), lambda i: (0, 0))],
        out_shape=[jax.ShapeDtypeStruct((ntile, 8, LANES), I32), jax.ShapeDtypeStruct((8, TAB_LANES), I32)],
        compiler_params=_params(("arbitrary",)),
    )(seg, cnt, lower)


def _for_each_piece(length, max_len, fn):
    size = SEG_ALIGN
    sizes = []
    while size <= max_len:
        sizes.append(size)
        size *= 2
    for size in reversed(sizes):
        @pl.when(jnp.bitwise_and(length, size) != 0)
        def _(size=size):
            fn(pl.multiple_of(jnp.bitwise_and(length, -2 * size), SEG_ALIGN), size)


def _sorted_rows(td):
    return 2 * td + N_GROUPS * EXPERTS_PER_GROUP * SEG_ALIGN


def _dispatch_kernel(tm, seglen_ref, lstart_ref, gstart_ref, ltot_ref, plo_ref, phi_ref, nu_ref,
                     x_ref, route_ref, xbuf_ref, sorted_ref, zblk, sem, zsem):
    i = pl.program_id(0)
    nsteps = pl.num_programs(0)
    td = x_ref.shape[0]
    ne = plo_ref.shape[0]
    nblk = xbuf_ref.shape[0] // tm
    nrow = sorted_ref.shape[1]
    buf = i % 2

    lpos = route_ref[2:4, :]
    r = lax.broadcasted_iota(I32, (nrow, td), 0).astype(F32)
    onehot = jnp.where((r == lpos[0:1]) | (r == lpos[1:2]), 1.0, 0.0).astype(BF16)
    sorted_ref[buf] = _dg(onehot, x_ref[...].astype(BF16))

    def wait_tile(step, b):
        _for_each_piece(ltot_ref[step], nrow, lambda off, size: pltpu.make_async_copy(
            sorted_ref.at[b, pl.ds(0, size), :], xbuf_ref.at[pl.ds(0, size), :], sem.at[b]).wait())

    for e in range(ne):
        idx = i * ne + e
        ls = pl.multiple_of(lstart_ref[idx], SEG_ALIGN)
        gs = pl.multiple_of(gstart_ref[idx], SEG_ALIGN)
        _for_each_piece(seglen_ref[idx], td, lambda off, size: pltpu.make_async_copy(
            sorted_ref.at[buf, pl.ds(ls + off, size), :], xbuf_ref.at[pl.ds(gs + off, size), :],
            sem.at[buf]).start())

    @pl.when(i > 0)
    def _():
        wait_tile(i - 1, 1 - buf)

    @pl.when(i == nsteps - 1)
    def _():
        wait_tile(i, buf)

    def pad_fill(fn):
        for e in range(ne):
            lo = pl.multiple_of(plo_ref[e], SEG_ALIGN)
            _for_each_piece(phi_ref[e] - lo, tm // 2, lambda off, size: fn(pltpu.make_async_copy(
                zblk.at[pl.ds(0, size), :], xbuf_ref.at[pl.ds(lo + off, size), :], zsem)))

        def per_blk(b, carry):
            fn(pltpu.make_async_copy(zblk, xbuf_ref.at[pl.ds(pl.multiple_of(b * tm, tm), tm), :], zsem))
            return carry
        lax.fori_loop(nu_ref[0], nblk, per_blk, 0)

    @pl.when(i == 0)
    def _():
        zblk[...] = jnp.zeros_like(zblk)
        pad_fill(lambda cp: cp.start())
        pad_fill(lambda cp: cp.wait())


def _dispatch(seglen, lstart, gstart, ltot, pad_lo, pad_hi, n_used, x1, route, rows, tm):
    n, d = x1.shape
    td = TD_DISPATCH
    return pl.pallas_call(
        functools.partial(_dispatch_kernel, tm),
        grid_spec=pltpu.PrefetchScalarGridSpec(
            num_scalar_prefetch=7,
            grid=(n // td,),
            in_specs=[pl.BlockSpec((td, d), lambda i, *_: (i, 0)), pl.BlockSpec((8, td), lambda i, *_: (0, i))],
            out_specs=pl.BlockSpec(memory_space=pl.ANY),
            scratch_shapes=[pltpu.VMEM((2, _sorted_rows(td), d), F32), pltpu.VMEM((tm, d), F32),
                            pltpu.SemaphoreType.DMA((2,)), pltpu.SemaphoreType.DMA(())],
        ),
        out_shape=jax.ShapeDtypeStruct((rows, d), F32),
        compiler_params=_params(("arbitrary",)),
    )(seglen, lstart, gstart, ltot, pad_lo, pad_hi, n_used, x1, route)


def _expert_kernel(be_ref, nu_ref, x_ref, w1_ref, w3_ref, w2_ref, y_ref, w1b, w3b, w2b):
    i = pl.program_id(0)
    prev = be_ref[jnp.maximum(i - 1, 0)]

    @pl.when((i == 0) | (be_ref[i] != prev))
    def _():
        w1b[...] = w1_ref[0].astype(BF16)
        w3b[...] = w3_ref[0].astype(BF16)
        w2b[...] = w2_ref[0].astype(BF16)

    @pl.when(i < nu_ref[0])
    def _():
        xb = x_ref[...].astype(BF16)
        h1 = _dg(xb, w1b[...])
        h3 = _dg(xb, w3b[...])
        h = (h1 * _sigmoid(h1)) * h3
        y_ref[...] = _dot(h, w2b[...])

    @pl.when(i >= nu_ref[0])
    def _():
        y_ref[...] = jnp.zeros_like(y_ref)


def _experts(blk_e, n_used, xbuf, w1, w3, w2):
    rows, d = xbuf.shape
    de = w1.shape[2]
    tm = TM_EXPERT
    row_map = lambda i, be, nu: (jnp.minimum(i, nu[0] - 1), 0)
    return pl.pallas_call(
        _expert_kernel,
        grid_spec=pltpu.PrefetchScalarGridSpec(
            num_scalar_prefetch=2,
            grid=(rows // tm,),
            in_specs=[pl.BlockSpec((tm, d), row_map),
                      pl.BlockSpec((1, d, de), lambda i, be, nu: (be[i], 0, 0)),
                      pl.BlockSpec((1, d, de), lambda i, be, nu: (be[i], 0, 0)),
                      pl.BlockSpec((1, de, d), lambda i, be, nu: (be[i], 0, 0))],
            out_specs=pl.BlockSpec((tm, d), lambda i, be, nu: (i, 0)),
            scratch_shapes=[pltpu.VMEM((d, de), BF16), pltpu.VMEM((d, de), BF16), pltpu.VMEM((de, d), BF16)],
        ),
        out_shape=jax.ShapeDtypeStruct((rows, d), F32),
        compiler_params=_params(("arbitrary",)),
    )(blk_e, n_used, xbuf, w1, w3, w2)


def _combine_kernel(alpha, seglen_ref, lstart_ref, gstart_ref, ltot_ref, x1_ref, rt_ref, p_ref, wpe_ref, wpg_ref,
                    g_ref, b_ref, ybuf_ref, out_ref, sorted_ref, sem):
    i = pl.program_id(0)
    nsteps = pl.num_programs(0)
    tc = x1_ref.shape[0]
    ne = N_GROUPS * EXPERTS_PER_GROUP
    nrow = sorted_ref.shape[1]

    def fetch(step, buf):
        for e in range(ne):
            idx = step * ne + e
            ls = pl.multiple_of(lstart_ref[idx], SEG_ALIGN)
            gs = pl.multiple_of(gstart_ref[idx], SEG_ALIGN)
            _for_each_piece(seglen_ref[idx], tc, lambda off, size: pltpu.make_async_copy(
                ybuf_ref.at[pl.ds(gs + off, size), :], sorted_ref.at[buf, pl.ds(ls + off, size), :],
                sem.at[buf]).start())

    @pl.when(i == 0)
    def _():
        sorted_ref[...] = jnp.zeros_like(sorted_ref)
        fetch(0, 0)

    @pl.when(i + 1 < nsteps)
    def _():
        fetch(i + 1, (i + 1) % 2)

    cur = i % 2
    _for_each_piece(ltot_ref[i], nrow, lambda off, size: pltpu.make_async_copy(
        ybuf_ref.at[pl.ds(0, size), :], sorted_ref.at[cur, pl.ds(0, size), :], sem.at[cur]).wait())

    rt = rt_ref[...]
    r = lax.broadcasted_iota(I32, (tc, nrow), 1).astype(F32)
    unsort = jnp.where(r == rt[:, 0:1], rt[:, 2:3], 0.0) + jnp.where(r == rt[:, 1:2], rt[:, 3:4], 0.0)
    ffn = _dot(unsort, sorted_ref[cur])
    x2 = _layer_norm(alpha * x1_ref[...] + ffn, g_ref[...], b_ref[...])
    gate = _sigmoid(_dot(x2, wpg_ref[...]))
    out_ref[...] = x2 + gate * _dot(p_ref[...], wpe_ref[...])


def _combine(seglen, lstart, gstart, ltot, x1, route_t, pf, wpe, wpg, g, bta, ybuf, alpha):
    n, d = x1.shape
    tc = TD_DISPATCH
    tile = lambda a_: pl.BlockSpec((tc, a_.shape[1]), lambda i, *_: (i, 0))
    const = lambda a_: pl.BlockSpec(a_.shape, lambda i, *_: (0, 0))
    return pl.pallas_call(
        functools.partial(_combine_kernel, alpha),
        grid_spec=pltpu.PrefetchScalarGridSpec(
            num_scalar_prefetch=4,
            grid=(n // tc,),
            in_specs=[tile(x1), tile(route_t), tile(pf), const(wpe), const(wpg), const(g), const(bta),
                      pl.BlockSpec(memory_space=pl.ANY)],
            out_specs=pl.BlockSpec((tc, d), lambda i, *_: (i, 0)),
            scratch_shapes=[pltpu.VMEM((2, _sorted_rows(tc), d), F32), pltpu.SemaphoreType.DMA((2,))],
        ),
        out_shape=jax.ShapeDtypeStruct((n, d), F32),
        compiler_params=_params(("arbitrary",)),
    )(seglen, lstart, gstart, ltot, x1, route_t, pf, wpe, wpg, g, bta, ybuf)


def _layer(x, p_i, w_in, rw_mu, rw_w0, rw_w_up, rw_a0, rw_a_up, rw_g_up, rw_k_k, rw_k_a, rw_r_k, rw_gn_w,
           rw_gn_b, w_a_out, hg_lb_logits, hg_norm_w, w_b_out, w_o, ln1_g, ln1_b, router_g_w, router_g_b,
           router_e_w, router_e_b, w1, w3, w2, ln2_g, ln2_b, w_pe, w_pg, alpha, layer):
    b, t, d = x.shape
    n = b * t
    rw_dim = rw_w0.shape[0]
    rw_cols = 3 * rw_dim + RW_DECAY_LORA + RW_A_LORA + RW_GATE_LORA
    hg_cols = 4 * hg_norm_w.shape[0]
    ne = N_GROUPS * EXPERTS_PER_GROUP
    row2 = lambda a_: a_.reshape(1, -1)
    xf = x.reshape(n, d)

    wb = w_in.astype(BF16)
    proj_rw, proj_hg, proj_gt = _project(xf, wb[:, :rw_cols], wb[:, rw_cols:rw_cols + hg_cols],
                                         wb[:, rw_cols + hg_cols:])
    ya = _rwkv_branch(proj_rw.reshape(b, t, rw_cols), rw_mu, row2(rw_w0), rw_w_up, rw_a0, rw_a_up, rw_g_up,
                      rw_k_k, rw_k_a, rw_r_k, rw_gn_w, rw_gn_b)
    yb = _hgrn_branch(proj_hg.reshape(b, t, hg_cols), hg_lb_logits, hg_norm_w, layer)

    wr = jnp.zeros((LANES, d), F32)
    wr = wr.at[:N_GROUPS].set(router_g_w.T).at[ROUTER_EXPERT_ROW:ROUTER_EXPERT_ROW + ne].set(router_e_w.T)
    bias = jnp.zeros((LANES,), F32)
    bias = bias.at[:N_GROUPS].set(router_g_b).at[ROUTER_EXPERT_ROW:ROUTER_EXPERT_ROW + ne].set(router_e_b)
    x1, lt = _merge(xf, ya.reshape(n, -1), yb.reshape(n, -1), proj_gt, w_a_out.astype(BF16),
                    w_b_out.astype(BF16), w_o.astype(BF16), row2(ln1_g), row2(ln1_b), wr, alpha)
    route, seg, cnt = _route(lt, bias.reshape(LANES, 1))

    tm = TM_EXPERT
    ntile = n // TD_DISPATCH
    nblk = -(-(2 * n + (SEG_ALIGN - 1) * ne * ntile) // tm) + ne
    assert nblk <= TAB_LANES and TB_ROUTE == TD_DISPATCH
    segtab, tab = _finalize(seg, cnt, tm)
    per_seg = lambda row: segtab[:, row, :ne].reshape(-1)
    seglen, lstart, gstart, ltot = per_seg(0), per_seg(1), per_seg(2), segtab[:, 3, 0]
    blk_e, n_used = tab[0, :nblk], tab[1, :1]

    xbuf = _dispatch(seglen, lstart, gstart, ltot, tab[2, :ne], tab[3, :ne], n_used, x1, route, nblk * tm, tm)
    ybuf = _experts(blk_e, n_used, xbuf, w1, w3, w2)
    out = _combine(seglen, lstart, gstart, ltot, x1, route[2:6].T, p_i.reshape(n, -1), w_pe.astype(BF16),
                   w_pg.astype(BF16), row2(ln2_g), row2(ln2_b), ybuf, alpha)
    return out.reshape(b, t, d)


def kernel(x, p, w_in, rw_mu, rw_w0, rw_w_up, rw_a0, rw_a_up, rw_g_up, rw_k_k, rw_k_a, rw_r_k, rw_gn_w, rw_gn_b,
           w_a_out, hg_lb_logits, hg_norm_w, w_b_out, w_o, ln1_g, ln1_b, router_g_w, router_g_b, router_e_w,
           router_e_b, w1, w3, w2, ln2_g, ln2_b, w_pe, w_pg):
    depth = w_in.shape[0]
    alpha = (2 * depth) ** 0.25
    for i in range(depth):
        x = _layer(x, p[i], w_in[i], rw_mu[i], rw_w0[i], rw_w_up[i], rw_a0[i], rw_a_up[i], rw_g_up[i], rw_k_k[i],
                   rw_k_a[i], rw_r_k[i].reshape(-1), rw_gn_w[i], rw_gn_b[i], w_a_out[i], hg_lb_logits,
                   hg_norm_w[i], w_b_out[i], w_o[i], ln1_g[i], ln1_b[i],
                   router_g_w[i], router_g_b[i], router_e_w[i], router_e_b[i], w1[i], w3[i], w2[i], ln2_g[i],
                   ln2_b[i], w_pe[i], w_pg[i], alpha, i)
    return x
```

```python
import functools

import jax
import jax.numpy as jnp
from jax import lax
from jax.experimental import pallas as pl
from jax.experimental.pallas import tpu as pltpu

F32 = jnp.float32
BF16 = jnp.bfloat16
I32 = jnp.int32

NN = (((1,), (0,)), ((), ()))
NT = (((1,), (1,)), ((), ()))

RW_HEAD = 64
RW_DECAY_LORA = 64
RW_A_LORA = 64
RW_GATE_LORA = 128
RW_GN_EPS = 64e-5
RW_DECAY_SCALE = 0.6065306597126334
HG_HEADS = 4
N_GROUPS = 4
EXPERTS_PER_GROUP = 8
LN_EPS = 1e-5
RMS_EPS = 1e-6

CHUNK = 64
SUB = 16
LANES = 128
VMEM_LIMIT = 56 * 1024 * 1024

TM_PROJ = 256
TB_REC = 256
TM_MERGE = 512
TB_ROUTE = 512
TD_DISPATCH = 512
SEG_ALIGN = 16
TM_EXPERT = 256
RW_WIDE = 4
HG_WIDE = 4


def _dg(a, b, dn=NN):
    return lax.dot_general(a, b, dn, preferred_element_type=F32)


def _dot(a, b, dn=NN):
    return _dg(a.astype(BF16), b.astype(BF16), dn)


def _split(a):
    hi = a.astype(BF16)
    lo = (a - hi.astype(F32)).astype(BF16)
    return hi, lo


def _dot_hl(a, b_exact, dn=NN):
    hi, lo = _split(a)
    return _dg(hi, b_exact, dn) + _dg(lo, b_exact, dn)


def _dot3(a, b, dn=NN):
    ah, al = _split(a)
    bh, bl = _split(b)
    return _dg(ah, bh, dn) + (_dg(ah, bl, dn) + _dg(al, bh, dn))


def _cumsum_chunks(x, tri):
    h = x.astype(BF16)
    r1 = x - h.astype(F32)
    m = r1.astype(BF16)
    l = (r1 - m.astype(F32)).astype(BF16)
    return _dg(tri, h) + (_dg(tri, m) + _dg(tri, l))


def _sigmoid(x):
    return 0.5 * jnp.tanh(0.5 * x) + 0.5


def _layer_norm(h, g, b):
    mu = jnp.mean(h, axis=-1, keepdims=True)
    d = h - mu
    var = jnp.mean(d * d, axis=-1, keepdims=True)
    return d * lax.rsqrt(var + LN_EPS) * g + b


def _params(sem):
    return pltpu.CompilerParams(dimension_semantics=sem, vmem_limit_bytes=VMEM_LIMIT)


def _proj_kernel(x_ref, wr_ref, wh_ref, wg_ref, pr_ref, ph_ref, pg_ref):
    xb = x_ref[...].astype(BF16)
    pr_ref[...] = _dg(xb, wr_ref[...])
    ph_ref[...] = _dg(xb, wh_ref[...])
    pg_ref[...] = _dg(xb, wg_ref[...])


def _project(xf, w_rw, w_hg, w_gt):
    n, d = xf.shape
    tm = TM_PROJ
    full = lambda w: pl.BlockSpec(w.shape, lambda i: (0, 0))
    tile = lambda c: pl.BlockSpec((tm, c), lambda i: (i, 0))
    return pl.pallas_call(
        _proj_kernel,
        grid=(n // tm,),
        in_specs=[tile(d), full(w_rw), full(w_hg), full(w_gt)],
        out_specs=[tile(w_rw.shape[1]), tile(w_hg.shape[1]), tile(w_gt.shape[1])],
        out_shape=[jax.ShapeDtypeStruct((n, w.shape[1]), F32) for w in (w_rw, w_hg, w_gt)],
        compiler_params=_params(("parallel",)),
    )(xf, w_rw, w_hg, w_gt)


def _each(f, *ls):
    return [f(*xs) for xs in zip(*ls)]


def _two(x):
    m1 = lax.broadcasted_iota(I32, x.shape, 1) < RW_HEAD
    return jnp.concatenate([jnp.where(m1, x, 0.0), jnp.where(m1, 0.0, x)], axis=0)


def _rwkv_chunk_prepare(ins, lvl_ref):
    c = CHUNK
    lane = lax.broadcasted_iota(I32, (c, LANES), 1)
    row = lax.broadcasted_iota(I32, (c, LANES), 0)
    scol = jnp.bitwise_and(lane, RW_HEAD - 1)
    strict = row > scol
    incl = row >= scol
    r2 = lax.broadcasted_iota(I32, (LANES, LANES), 0)
    c2 = lax.broadcasted_iota(I32, (LANES, LANES), 1)
    eye = jnp.where(r2 == c2, 1.0, 0.0).astype(F32)

    def prep(r, k, v, av, bv, lw, lc):
        l_end = lc[c - 1:c]
        e_r = jnp.exp(l_end - lc)
        e_n = jnp.exp(-lc)
        return dict(at=av * jnp.exp(lc - lw), rt=r * jnp.exp(lc), bt=bv * e_n, kt=k * e_n,
                    bk=jnp.concatenate([bv * e_r, k * e_r], axis=0), pc=jnp.exp(l_end), v=v)

    q = [prep(*xs) for xs in ins]
    p = [_dot(jnp.concatenate([d["at"], d["rt"]], axis=0),
              jnp.concatenate([_two(d["bt"]), _two(d["kt"])], axis=0), NT) for d in q]
    sab = [jnp.where(strict, x[:c, :LANES], 0.0) for x in p]
    sak = [jnp.where(strict, x[:c, LANES:], 0.0) for x in p]
    srb = [jnp.where(incl, x[c:, :LANES], 0.0) for x in p]
    srk = [jnp.where(incl, x[c:, LANES:], 0.0) for x in p]
    sv = _each(lambda ak, rk, d: _dot(jnp.concatenate([ak, rk], axis=0), _two(d["v"])), sak, srk, q)

    a_bd = [_two(x) for x in sab]
    t = [eye + a * lvl_ref[0] for a in a_bd]
    for lv in range(1, lvl_ref.shape[0]):
        ta = _each(lambda t_, a: _dot(t_, a * lvl_ref[lv]), t, a_bd)
        t = _each(lambda t_, ta_: t_ + _dot(ta_, t_), t, ta)

    x = _each(lambda t_, d, sv_: _dot(t_, jnp.concatenate([_two(d["at"]), _two(sv_[:c])], axis=1)), t, q, sv)
    return [dict(uk=x_[:c, :LANES] + x_[c:, :LANES],
                 w=x_[:c, LANES:] + x_[c:, LANES:],
                 rt=d["rt"], rkv=sv_[c:], srb=srb_, bk=d["bk"], v=d["v"], pc=d["pc"])
            for x_, d, sv_, srb_ in zip(x, q, sv, srb)]


def _rwkv_chunk_apply(prep, states, bd):
    c = CHUNK
    g1 = _each(lambda d, s: _dot(jnp.concatenate([d["uk"], d["rt"]], axis=0), s, NT), prep, states)
    u = _each(lambda g, d: g[:c] + d["w"], g1, prep)
    y = _each(lambda g, d, u_: g[c:] + d["rkv"] + _dot(d["srb"], _two(u_)), g1, prep, u)
    upd = _each(lambda u_, d: _dot(jnp.concatenate([u_, d["v"]], axis=0).T, d["bk"]), u, prep)
    s_new = _each(lambda d, s, up: s * d["pc"] + bd * up, prep, states, upd)
    return list(zip(y, s_new))


def _rwkv_kernel(u_ref, mu_ref, w0_ref, wup_ref, a0_ref, aup_ref, gup_ref, kk_ref, ka_ref, rk_ref,
                 gnw_ref, gnb_ref, tri_ref, gsum_ref, lvl_ref, bd_ref, ya_ref,
                 s_ref, prev_ref, r_s, k_s, v_s, a_s, b_s, lw_s, lc_s, g_s, y_s):
    tb = pl.program_id(1)

    @pl.when(tb == 0)
    def _():
        s_ref[...] = jnp.zeros_like(s_ref)
        prev_ref[...] = jnp.zeros_like(prev_ref)

    u = u_ref[0]
    nt = u.shape[0]
    dim = r_s.shape[1]
    rowid = lax.broadcasted_iota(I32, u.shape, 0)
    shifted = jnp.where(rowid == 0, prev_ref[...], pltpu.roll(u, 1, axis=0))
    prev_ref[...] = u[nt - 1:nt, :]
    um = u + (shifted - u) * mu_ref[...]

    r = um[:, 0:dim]
    k = um[:, dim:2 * dim]
    v = um[:, 2 * dim:3 * dim]
    xwa = um[:, 3 * dim:3 * dim + LANES]
    xg = um[:, 3 * dim + LANES:3 * dim + 2 * LANES]

    wpre = w0_ref[...] + _dot(jnp.tanh(xwa), wup_ref[...])
    lw = -RW_DECAY_SCALE * _sigmoid(wpre)
    a = _sigmoid(a0_ref[...] + _dot(xwa, aup_ref[...]))
    g_s[...] = _dot(_sigmoid(xg), gup_ref[...])
    kk = k * kk_ref[...]
    ss = _dot(kk * kk, gsum_ref[...])
    kk = kk * lax.rsqrt(jnp.maximum(ss, 1e-24))
    r_s[...] = r
    k_s[...] = k * (1.0 + (a - 1.0) * ka_ref[...])
    v_s[...] = v
    a_s[...] = -kk
    b_s[...] = kk * a
    lw_s[...] = lw
    lc_s[...] = _cumsum_chunks(lw, tri_ref[...])
    bd = bd_ref[...]

    npair = dim // LANES
    lanes = [slice(p * LANES, (p + 1) * LANES) for p in range(npair)]

    def group_body(gi, carry):
        rows = [pl.ds(pl.multiple_of((gi * RW_WIDE + ci) * CHUNK, CHUNK), CHUNK) for ci in range(RW_WIDE)]
        ins = [(r_s[rw, ls], k_s[rw, ls], v_s[rw, ls], a_s[rw, ls], b_s[rw, ls], lw_s[rw, ls], lc_s[rw, ls])
               for rw in rows for ls in lanes]
        prep = _rwkv_chunk_prepare(ins, lvl_ref)
        states = [s_ref[p] for p in range(npair)]
        for ci, rw in enumerate(rows):
            outs = _rwkv_chunk_apply(prep[ci * npair:(ci + 1) * npair], states, bd)
            states = [s_new for _, s_new in outs]
            for (y, _), ls in zip(outs, lanes):
                y_s[rw, ls] = y
        for p in range(npair):
            s_ref[p] = states[p]
        return carry

    lax.fori_loop(0, nt // (CHUNK * RW_WIDE), group_body, 0)

    y = y_s[...]
    gsum = gsum_ref[...]
    inv_n = 1.0 / RW_HEAD
    m = _dot_hl(y, gsum) * inv_n
    d = y - m
    var = _dot(d * d, gsum) * inv_n
    yn = d * lax.rsqrt(var + RW_GN_EPS) * gnw_ref[...] + gnb_ref[...]
    bonus = _dot(r_s[...] * k_s[...] * rk_ref[...], gsum) * v_s[...]
    ya_ref[0] = ((yn + bonus) * g_s[...]).astype(BF16)


def _rwkv_branch(proj_rw, mu, w0, wup, a0, aup, gup, k_k, k_a, r_k, gn_w, gn_b):
    b, t, cols = proj_rw.shape
    dim = w0.shape[1]
    tb = TB_REC
    ii = jnp.arange(tb)
    tri = ((ii[:, None] // CHUNK == ii[None, :] // CHUNK) & (ii[:, None] >= ii[None, :])).astype(BF16)
    jj = jnp.arange(dim)
    gsum = (jj[:, None] // RW_HEAD == jj[None, :] // RW_HEAD).astype(BF16)
    rr = jnp.arange(LANES)[:, None]
    cc = jnp.arange(LANES)[None, :]
    lvls = []
    s = 1
    while s < CHUNK:
        lvls.append(((rr // (2 * s) == cc // (2 * s)) & ((rr // s) % 2 == 1) & ((cc // s) % 2 == 0)).astype(F32))
        s *= 2
    lvl = jnp.stack(lvls)
    bd = (rr // RW_HEAD == cc // RW_HEAD).astype(F32)
    zpad = lambda rows: jnp.zeros((rows, dim), F32)
    wup_p = jnp.concatenate([wup, zpad(LANES - wup.shape[0])], axis=0).astype(BF16)
    aup_p = jnp.concatenate([zpad(LANES - aup.shape[0]), aup], axis=0).astype(BF16)
    row2 = lambda a_: a_.reshape(1, -1)
    const = lambda a_: pl.BlockSpec(a_.shape, lambda bi, ti: (0,) * a_.ndim)
    args = [row2(mu), row2(w0), wup_p, row2(a0), aup_p, gup.astype(BF16), row2(k_k), row2(k_a), row2(r_k),
            row2(gn_w), row2(gn_b), tri, gsum, lvl, bd]
    sc = lambda: pltpu.VMEM((tb, dim), F32)
    return pl.pallas_call(
        _rwkv_kernel,
        grid=(b, t // tb),
        in_specs=[pl.BlockSpec((1, tb, cols), lambda bi, ti: (bi, ti, 0))] + [const(a_) for a_ in args],
        out_specs=pl.BlockSpec((1, tb, dim), lambda bi, ti: (bi, ti, 0)),
        out_shape=jax.ShapeDtypeStruct((b, t, dim), BF16),
        scratch_shapes=[pltpu.VMEM((dim // LANES, LANES, LANES), F32), pltpu.VMEM((1, cols), F32)]
                       + [sc() for _ in range(9)],
        compiler_params=_params(("arbitrary", "arbitrary")),
    )(proj_rw, *args)


def _hgrn_chunk_prepare(ins):
    c = CHUNK
    subs = [(SUB * i, SUB * (i + 1)) for i in range(c // SUB)]

    def scores(q, k, lf, bc, lo, hi):
        m = bc[lo:lo + 1] - lf[lo:lo + 1]
        att = _dot(q[lo:hi] * jnp.exp(bc[lo:hi] - m), k[:hi] * jnp.exp(m - bc[:hi]), NT)
        tt = lax.broadcasted_iota(I32, (SUB, hi), 0) + lo
        s_ = lax.broadcasted_iota(I32, (SUB, hi), 1)
        return jnp.where(s_ <= tt, att, 0.0)

    att = [[scores(q, k, lf, bc, lo, hi) for lo, hi in subs] for q, k, v, lf, bc in ins]
    upd = [_dot(v.T, k * jnp.exp(bc[c - 1:c] - bc)) for q, k, v, lf, bc in ins]
    intra = [[_dot(a, x[2][:hi]) for a, (lo, hi) in zip(arow, subs)] for arow, x in zip(att, ins)]
    return [dict(intra=jnp.concatenate(rows, axis=0), qe=x[0] * jnp.exp(x[4]), upd=up, pc=jnp.exp(x[4][c - 1:c]))
            for rows, x, up in zip(intra, ins, upd)]


def _hgrn_chunk_apply(prep, states):
    o = _each(lambda d, st: d["intra"] + _dot(d["qe"], st, NT), prep, states)
    st_new = _each(lambda d, st: st * d["pc"] + d["upd"], prep, states)
    return list(zip(o, st_new))


def _hgrn_kernel(layer, u_ref, lbl_ref, nw_ref, tri_ref, gsum_ref, yb_ref,
                 st_ref, q_s, k_s, v_s, lf_s, bc_s, o_s):
    tb = pl.program_id(1)

    @pl.when(tb == 0)
    def _():
        st_ref[...] = jnp.zeros_like(st_ref)

    u = u_ref[0]
    nt = u.shape[0]
    dim = q_s.shape[1]
    lbl = lbl_ref[...]
    e = jnp.exp(lbl - jnp.max(lbl, axis=0, keepdims=True))
    lb = jnp.sum(e[0:layer + 1], axis=0, keepdims=True) / jnp.sum(e, axis=0, keepdims=True)
    zf = u[:, dim:2 * dim]
    sig = _sigmoid(zf)
    f = lb + (1.0 - lb) * sig
    qin = u[:, 0:dim]
    q_s[...] = qin * _sigmoid(qin)
    k_s[...] = (1.0 - lb) * (1.0 - sig)
    v_s[...] = u[:, 2 * dim:3 * dim]
    lf = jnp.log(f)
    lf_s[...] = lf
    bc_s[...] = _cumsum_chunks(lf, tri_ref[...])

    nhead = dim // LANES
    lanes = [slice(h * LANES, (h + 1) * LANES) for h in range(nhead)]

    def group_body(gi, carry):
        rows = [pl.ds(pl.multiple_of((gi * HG_WIDE + ci) * CHUNK, CHUNK), CHUNK) for ci in range(HG_WIDE)]
        ins = [(q_s[rw, ls], k_s[rw, ls], v_s[rw, ls], lf_s[rw, ls], bc_s[rw, ls]) for rw in rows for ls in lanes]
        prep = _hgrn_chunk_prepare(ins)
        states = [st_ref[h] for h in range(nhead)]
        for ci, rw in enumerate(rows):
            outs = _hgrn_chunk_apply(prep[ci * nhead:(ci + 1) * nhead], states)
            states = [st_new for _, st_new in outs]
            for (o, _), ls in zip(outs, lanes):
                o_s[rw, ls] = o
        for h in range(nhead):
            st_ref[h] = states[h]
        return carry

    lax.fori_loop(0, nt // (CHUNK * HG_WIDE), group_body, 0)

    o = o_s[...]
    ms = _dot(o * o, gsum_ref[...]) * (1.0 / LANES)
    og = u[:, 3 * dim:4 * dim]
    yb_ref[0] = (o * lax.rsqrt(ms + RMS_EPS) * nw_ref[...] * _sigmoid(og)).astype(BF16)


def _hgrn_branch(proj_hg, lb_logits, norm_w, layer):
    b, t, cols = proj_hg.shape
    dim = cols // 4
    tb = TB_REC
    ii = jnp.arange(tb)
    tri = ((ii[:, None] // CHUNK == ii[None, :] // CHUNK) & (ii[:, None] >= ii[None, :])).astype(BF16)
    jj = jnp.arange(dim)
    gsum = (jj[:, None] // LANES == jj[None, :] // LANES).astype(BF16)
    const = lambda a_: pl.BlockSpec(a_.shape, lambda bi, ti: (0,) * a_.ndim)
    args = [lb_logits, norm_w.reshape(1, -1), tri, gsum]
    sc = lambda: pltpu.VMEM((tb, dim), F32)
    return pl.pallas_call(
        functools.partial(_hgrn_kernel, layer),
        grid=(b, t // tb),
        in_specs=[pl.BlockSpec((1, tb, cols), lambda bi, ti: (bi, ti, 0))] + [const(a_) for a_ in args],
        out_specs=pl.BlockSpec((1, tb, dim), lambda bi, ti: (bi, ti, 0)),
        out_shape=jax.ShapeDtypeStruct((b, t, dim), BF16),
        scratch_shapes=[pltpu.VMEM((dim // LANES, LANES, LANES), F32)] + [sc() for _ in range(6)],
        compiler_params=_params(("arbitrary", "arbitrary")),
    )(proj_hg, *args)


def _merge_kernel(alpha, x_ref, ya_ref, yb_ref, pg_ref, wa_ref, wb_ref, wo_ref, g_ref, b_ref, wr_ref,
                  x1_ref, lt_ref):
    d = x_ref.shape[1]
    gates = _sigmoid(pg_ref[...])
    merged = gates[:, :d] * _dg(ya_ref[...], wa_ref[...]) + gates[:, d:] * _dg(yb_ref[...], wb_ref[...])
    h = alpha * x_ref[...] + _dot(merged, wo_ref[...])
    x1 = _layer_norm(h, g_ref[...], b_ref[...])
    x1_ref[...] = x1
    lt_ref[...] = _dot3(wr_ref[...], x1, NT)


def _merge(xf, ya, yb, pgate, wa, wb, wo, g, bta, wr, alpha):
    n, d = xf.shape
    tm = TM_MERGE
    tile = lambda a_: pl.BlockSpec((tm, a_.shape[1]), lambda i: (i, 0))
    const = lambda a_: pl.BlockSpec(a_.shape, lambda i: (0, 0))
    return pl.pallas_call(
        functools.partial(_merge_kernel, alpha),
        grid=(n // tm,),
        in_specs=[tile(xf), tile(ya), tile(yb), tile(pgate), const(wa), const(wb), const(wo), const(g),
                  const(bta), const(wr)],
        out_specs=[pl.BlockSpec((tm, d), lambda i: (i, 0)), pl.BlockSpec((LANES, tm), lambda i: (0, i))],
        out_shape=[jax.ShapeDtypeStruct((n, d), F32), jax.ShapeDtypeStruct((LANES, n), F32)],
        compiler_params=_params(("parallel",)),
    )(xf, ya, yb, pgate, wa, wb, wo, g, bta, wr)


ROUTER_EXPERT_ROW = 8


def _to_lanes(col, nl):
    ne = col.shape[0]
    diag = lax.broadcasted_iota(I32, (ne, nl), 0) == lax.broadcasted_iota(I32, (ne, nl), 1)
    return jnp.sum(jnp.where(diag, col, 0.0), axis=0, keepdims=True)


def _route_kernel(lt_ref, bias_ref, upper_ref, lower_ref, route_ref, seg_ref, cnt_ref, carry_ref):
    @pl.when(pl.program_id(0) == 0)
    def _():
        carry_ref[...] = jnp.zeros_like(carry_ref)

    ne = N_GROUPS * EXPERTS_PER_GROUP
    lt = lt_ref[...] + bias_ref[...]
    nb = lt.shape[1]
    neg = -jnp.inf
    lg = lt[0:8]
    rg = lax.broadcasted_iota(I32, (8, nb), 0).astype(F32)
    lg = jnp.where(rg < N_GROUPS, lg, neg)
    mg = jnp.max(lg, axis=0, keepdims=True)
    gidx = jnp.min(jnp.where(lg == mg, rg, 1e9), axis=0, keepdims=True)
    pg_sel = 1.0 / jnp.sum(jnp.exp(lg - mg), axis=0, keepdims=True)

    le = lt[ROUTER_EXPERT_ROW:ROUTER_EXPERT_ROW + ne]
    re = lax.broadcasted_iota(I32, (ne, nb), 0).astype(F32)
    in_group = jnp.floor(re * (1.0 / EXPERTS_PER_GROUP)) == gidx
    l1 = jnp.where(in_group, le, neg)
    m1 = jnp.max(l1, axis=0, keepdims=True)
    i1 = jnp.min(jnp.where(l1 == m1, re, 1e9), axis=0, keepdims=True)
    l2 = jnp.where(re == i1, neg, l1)
    m2 = jnp.max(l2, axis=0, keepdims=True)
    i2 = jnp.min(jnp.where(l2 == m2, re, 1e9), axis=0, keepdims=True)
    e2 = jnp.exp(m2 - m1)
    w1 = pg_sel / (1.0 + e2)
    w2 = pg_sel * e2 / (1.0 + e2)

    sel1 = re == i1
    sel2 = re == i2
    onehot = jnp.where(sel1 | sel2, 1.0, 0.0)
    before = _dg(onehot.astype(BF16), upper_ref[...])
    cnt_t = jnp.sum(onehot, axis=1, keepdims=True)
    seg = jnp.floor((cnt_t + (SEG_ALIGN - 1)) * (1.0 / SEG_ALIGN)) * SEG_ALIGN
    lstart = _dg(lower_ref[...], jnp.broadcast_to(seg, (ne, LANES)).astype(BF16))[:, 0:1]
    tot = lstart + before
    lpos1 = jnp.sum(jnp.where(sel1, tot, 0.0), axis=0, keepdims=True)
    lpos2 = jnp.sum(jnp.where(sel2, tot, 0.0), axis=0, keepdims=True)
    grel = carry_ref[...]
    carry = grel + seg
    carry_ref[...] = carry
    cnt_ref[...] = jnp.broadcast_to(carry, cnt_ref.shape)
    zero = jnp.zeros_like(w1)
    route_ref[...] = jnp.concatenate([i1, i2, lpos1, lpos2, w1, w2, zero, zero], axis=0)
    nl = seg_ref.shape[2]
    zl = jnp.zeros((1, nl), F32)
    ltot = jnp.broadcast_to(jnp.sum(seg, axis=0, keepdims=True), (1, nl))
    seg_ref[0] = jnp.concatenate([_to_lanes(seg, nl), _to_lanes(lstart, nl), _to_lanes(grel, nl), ltot,
                                  zl, zl, zl, zl], axis=0)


def _route(lt, bias_col):
    n = lt.shape[1]
    tb = TB_ROUTE
    ne = N_GROUPS * EXPERTS_PER_GROUP
    ii = jnp.arange(tb)
    upper = (ii[:, None] < ii[None, :]).astype(BF16)
    ee = jnp.arange(ne)
    lower = (ee[:, None] > ee[None, :]).astype(BF16)
    return pl.pallas_call(
        _route_kernel,
        grid=(n // tb,),
        in_specs=[pl.BlockSpec((LANES, tb), lambda i: (0, i)),
                  pl.BlockSpec((LANES, 1), lambda i: (0, 0)),
                  pl.BlockSpec((tb, tb), lambda i: (0, 0)),
                  pl.BlockSpec((ne, ne), lambda i: (0, 0))],
        out_specs=[pl.BlockSpec((8, tb), lambda i: (0, i)), pl.BlockSpec((1, 8, LANES), lambda i: (i, 0, 0)),
                   pl.BlockSpec((ne, LANES), lambda i: (0, 0))],
        out_shape=[jax.ShapeDtypeStruct((8, n), F32), jax.ShapeDtypeStruct((n // tb, 8, LANES), F32),
                   jax.ShapeDtypeStruct((ne, LANES), F32)],
        scratch_shapes=[pltpu.VMEM((ne, 1), F32)],
        compiler_params=_params(("arbitrary",)),
    )(lt, bias_col, upper, lower)


TAB_LANES = 2 * LANES


def _finalize_kernel(tm, seg_ref, cnt_ref, lower_ref, segtab_ref, tab_ref):
    ne = cnt_ref.shape[0]
    cnt = cnt_ref[...]
    nb = jnp.floor((cnt + (tm - 1)) * (1.0 / tm))
    bstart = _dg(lower_ref[...], nb.astype(BF16))
    bend = bstart + nb
    pad_start = bstart[:, 0:1] * tm
    seg = seg_ref[0]
    gstart = seg[2:3] + _to_lanes(pad_start, seg.shape[1])
    segtab_ref[0] = jnp.concatenate([seg[0:2], gstart, seg[3:8]], axis=0).astype(I32)

    nl = tab_ref.shape[1]
    n_used = jnp.max(bend[:, 0:1], axis=0, keepdims=True)
    blk = jnp.minimum(lax.broadcasted_iota(I32, (ne, nl), 1).astype(F32), n_used - 1.0)
    blk_e = jnp.sum(jnp.where(bend[:, 0:1] <= blk, 1.0, 0.0), axis=0, keepdims=True)
    blk_e = jnp.minimum(blk_e, ne - 1.0)
    pad_lo = _to_lanes(pad_start + cnt[:, 0:1], nl)
    pad_hi = _to_lanes(bend[:, 0:1] * tm, nl)
    zero = jnp.zeros((1, nl), F32)
    tab_ref[...] = jnp.concatenate([blk_e, jnp.broadcast_to(n_used, (1, nl)), pad_lo, pad_hi,
                                    zero, zero, zero, zero], axis=0).astype(I32)


def _finalize(seg, cnt, tm):
    ntile = seg.shape[0]
    ne = cnt.shape[0]
    ii = jnp.arange(ne)
    lower = (ii[:, None] > ii[None, :]).astype(BF16)
    return pl.pallas_call(
        functools.partial(_finalize_kernel, tm),
        grid=(ntile,),
        in_specs=[pl.BlockSpec((1, 8, LANES), lambda i: (i, 0, 0)), pl.BlockSpec(cnt.shape, lambda i: (0, 0)),
                  pl.BlockSpec((ne, ne), lambda i: (0, 0))],
        out_specs=[pl.BlockSpec((1, 8, LANES), lambda i: (i, 0, 0)), pl.BlockSpec((8, TAB_LANES), lambda i: (0, 0))],
        out_shape=[jax.ShapeDtypeStruct((ntile, 8, LANES), I32), jax.ShapeDtypeStruct((8, TAB_LANES), I32)],
        compiler_params=_params(("arbitrary",)),
    )(seg, cnt, lower)


def _for_each_piece(length, max_len, fn):
    size = SEG_ALIGN
    sizes = []
    while size <= max_len:
        sizes.append(size)
        size *= 2
    for size in reversed(sizes):
        @pl.when(jnp.bitwise_and(length, size) != 0)
        def _(size=size):
            fn(pl.multiple_of(jnp.bitwise_and(length, -2 * size), SEG_ALIGN), size)


def _sorted_rows(td):
    return 2 * td + N_GROUPS * EXPERTS_PER_GROUP * SEG_ALIGN


def _dispatch_kernel(tm, seglen_ref, lstart_ref, gstart_ref, ltot_ref, plo_ref, phi_ref, nu_ref,
                     x_ref, route_ref, xbuf_ref, sorted_ref, zblk, sem, zsem):
    i = pl.program_id(0)
    nsteps = pl.num_programs(0)
    td = x_ref.shape[0]
    ne = plo_ref.shape[0]
    nblk = xbuf_ref.shape[0] // tm
    nrow = sorted_ref.shape[1]
    buf = i % 2

    lpos = route_ref[2:4, :]
    r = lax.broadcasted_iota(I32, (nrow, td), 0).astype(F32)
    onehot = jnp.where((r == lpos[0:1]) | (r == lpos[1:2]), 1.0, 0.0).astype(BF16)
    sorted_ref[buf] = _dg(onehot, x_ref[...].astype(BF16)).astype(BF16)

    def wait_tile(step, b):
        _for_each_piece(ltot_ref[step], nrow, lambda off, size: pltpu.make_async_copy(
            sorted_ref.at[b, pl.ds(0, size), :], xbuf_ref.at[pl.ds(0, size), :], sem.at[b]).wait())

    for e in range(ne):
        idx = i * ne + e
        ls = pl.multiple_of(lstart_ref[idx], SEG_ALIGN)
        gs = pl.multiple_of(gstart_ref[idx], SEG_ALIGN)
        _for_each_piece(seglen_ref[idx], td, lambda off, size: pltpu.make_async_copy(
            sorted_ref.at[buf, pl.ds(ls + off, size), :], xbuf_ref.at[pl.ds(gs + off, size), :],
            sem.at[buf]).start())

    @pl.when(i > 0)
    def _():
        wait_tile(i - 1, 1 - buf)

    @pl.when(i == nsteps - 1)
    def _():
        wait_tile(i, buf)

    def pad_fill(fn):
        for e in range(ne):
            lo = pl.multiple_of(plo_ref[e], SEG_ALIGN)
            _for_each_piece(phi_ref[e] - lo, tm // 2, lambda off, size: fn(pltpu.make_async_copy(
                zblk.at[pl.ds(0, size), :], xbuf_ref.at[pl.ds(lo + off, size), :], zsem)))

        def per_blk(b, carry):
            fn(pltpu.make_async_copy(zblk, xbuf_ref.at[pl.ds(pl.multiple_of(b * tm, tm), tm), :], zsem))
            return carry
        lax.fori_loop(nu_ref[0], nblk, per_blk, 0)

    @pl.when(i == 0)
    def _():
        zblk[...] = jnp.zeros_like(zblk)
        pad_fill(lambda cp: cp.start())
        pad_fill(lambda cp: cp.wait())


def _dispatch(seglen, lstart, gstart, ltot, pad_lo, pad_hi, n_used, x1, route, rows, tm):
    n, d = x1.shape
    td = TD_DISPATCH
    return pl.pallas_call(
        functools.partial(_dispatch_kernel, tm),
        grid_spec=pltpu.PrefetchScalarGridSpec(
            num_scalar_prefetch=7,
            grid=(n // td,),
            in_specs=[pl.BlockSpec((td, d), lambda i, *_: (i, 0)), pl.BlockSpec((8, td), lambda i, *_: (0, i))],
            out_specs=pl.BlockSpec(memory_space=pl.ANY),
            scratch_shapes=[pltpu.VMEM((2, _sorted_rows(td), d), BF16), pltpu.VMEM((tm, d), BF16),
                            pltpu.SemaphoreType.DMA((2,)), pltpu.SemaphoreType.DMA(())],
        ),
        out_shape=jax.ShapeDtypeStruct((rows, d), BF16),
        compiler_params=_params(("arbitrary",)),
    )(seglen, lstart, gstart, ltot, pad_lo, pad_hi, n_used, x1, route)


def _expert_kernel(be_ref, nu_ref, x_ref, w1_ref, w3_ref, w2_ref, y_ref, w1b, w3b, w2b):
    i = pl.program_id(0)
    prev = be_ref[jnp.maximum(i - 1, 0)]

    @pl.when((i == 0) | (be_ref[i] != prev))
    def _():
        w1b[...] = w1_ref[0].astype(BF16)
        w3b[...] = w3_ref[0].astype(BF16)
        w2b[...] = w2_ref[0].astype(BF16)

    @pl.when(i < nu_ref[0])
    def _():
        xb = x_ref[...]
        h1 = _dg(xb, w1b[...])
        h3 = _dg(xb, w3b[...])
        h = (h1 * _sigmoid(h1)) * h3
        y_ref[...] = _dot(h, w2b[...]).astype(BF16)

    @pl.when(i >= nu_ref[0])
    def _():
        y_ref[...] = jnp.zeros_like(y_ref)


def _experts(blk_e, n_used, xbuf, w1, w3, w2):
    rows, d = xbuf.shape
    de = w1.shape[2]
    tm = TM_EXPERT
    row_map = lambda i, be, nu: (jnp.minimum(i, nu[0] - 1), 0)
    return pl.pallas_call(
        _expert_kernel,
        grid_spec=pltpu.PrefetchScalarGridSpec(
            num_scalar_prefetch=2,
            grid=(rows // tm,),
            in_specs=[pl.BlockSpec((tm, d), row_map),
                      pl.BlockSpec((1, d, de), lambda i, be, nu: (be[i], 0, 0)),
                      pl.BlockSpec((1, d, de), lambda i, be, nu: (be[i], 0, 0)),
                      pl.BlockSpec((1, de, d), lambda i, be, nu: (be[i], 0, 0))],
            out_specs=pl.BlockSpec((tm, d), lambda i, be, nu: (i, 0)),
            scratch_shapes=[pltpu.VMEM((d, de), BF16), pltpu.VMEM((d, de), BF16), pltpu.VMEM((de, d), BF16)],
        ),
        out_shape=jax.ShapeDtypeStruct((rows, d), BF16),
        compiler_params=_params(("arbitrary",)),
    )(blk_e, n_used, xbuf, w1, w3, w2)


def _combine_kernel(alpha, seglen_ref, lstart_ref, gstart_ref, ltot_ref, x1_ref, rt_ref, p_ref, wpe_ref, wpg_ref,
                    g_ref, b_ref, ybuf_ref, out_ref, sorted_ref, sem):
    i = pl.program_id(0)
    nsteps = pl.num_programs(0)
    tc = x1_ref.shape[0]
    ne = N_GROUPS * EXPERTS_PER_GROUP
    nrow = sorted_ref.shape[1]

    def fetch(step, buf):
        for e in range(ne):
            idx = step * ne + e
            ls = pl.multiple_of(lstart_ref[idx], SEG_ALIGN)
            gs = pl.multiple_of(gstart_ref[idx], SEG_ALIGN)
            _for_each_piece(seglen_ref[idx], tc, lambda off, size: pltpu.make_async_copy(
                ybuf_ref.at[pl.ds(gs + off, size), :], sorted_ref.at[buf, pl.ds(ls + off, size), :],
                sem.at[buf]).start())

    @pl.when(i == 0)
    def _():
        sorted_ref[...] = jnp.zeros_like(sorted_ref)
        fetch(0, 0)

    @pl.when(i + 1 < nsteps)
    def _():
        fetch(i + 1, (i + 1) % 2)

    cur = i % 2
    _for_each_piece(ltot_ref[i], nrow, lambda off, size: pltpu.make_async_copy(
        ybuf_ref.at[pl.ds(0, size), :], sorted_ref.at[cur, pl.ds(0, size), :], sem.at[cur]).wait())

    rt = rt_ref[...]
    r = lax.broadcasted_iota(I32, (tc, nrow), 1).astype(F32)
    unsort = jnp.where(r == rt[:, 0:1], rt[:, 2:3], 0.0) + jnp.where(r == rt[:, 1:2], rt[:, 3:4], 0.0)
    ffn = _dg(unsort.astype(BF16), sorted_ref[cur])
    x2 = _layer_norm(alpha * x1_ref[...] + ffn, g_ref[...], b_ref[...])
    gate = _sigmoid(_dot(x2, wpg_ref[...]))
    out_ref[...] = x2 + gate * _dot(p_ref[...], wpe_ref[...])


def _combine(seglen, lstart, gstart, ltot, x1, route_t, pf, wpe, wpg, g, bta, ybuf, alpha):
    n, d = x1.shape
    tc = TD_DISPATCH
    tile = lambda a_: pl.BlockSpec((tc, a_.shape[1]), lambda i, *_: (i, 0))
    const = lambda a_: pl.BlockSpec(a_.shape, lambda i, *_: (0, 0))
    return pl.pallas_call(
        functools.partial(_combine_kernel, alpha),
        grid_spec=pltpu.PrefetchScalarGridSpec(
            num_scalar_prefetch=4,
            grid=(n // tc,),
            in_specs=[tile(x1), tile(route_t), tile(pf), const(wpe), const(wpg), const(g), const(bta),
                      pl.BlockSpec(memory_space=pl.ANY)],
            out_specs=pl.BlockSpec((tc, d), lambda i, *_: (i, 0)),
            scratch_shapes=[pltpu.VMEM((2, _sorted_rows(tc), d), BF16), pltpu.SemaphoreType.DMA((2,))],
        ),
        out_shape=jax.ShapeDtypeStruct((n, d), F32),
        compiler_params=_params(("arbitrary",)),
    )(seglen, lstart, gstart, ltot, x1, route_t, pf, wpe, wpg, g, bta, ybuf)


def _layer(x, p_i, w_in, rw_mu, rw_w0, rw_w_up, rw_a0, rw_a_up, rw_g_up, rw_k_k, rw_k_a, rw_r_k, rw_gn_w,
           rw_gn_b, w_a_out, hg_lb_logits, hg_norm_w, w_b_out, w_o, ln1_g, ln1_b, router_g_w, router_g_b,
           router_e_w, router_e_b, w1, w3, w2, ln2_g, ln2_b, w_pe, w_pg, alpha, layer):
    b, t, d = x.shape
    n = b * t
    rw_dim = rw_w0.shape[0]
    rw_cols = 3 * rw_dim + RW_DECAY_LORA + RW_A_LORA + RW_GATE_LORA
    hg_cols = 4 * hg_norm_w.shape[0]
    ne = N_GROUPS * EXPERTS_PER_GROUP
    row2 = lambda a_: a_.reshape(1, -1)
    xf = x.reshape(n, d)

    wb = w_in.astype(BF16)
    proj_rw, proj_hg, proj_gt = _project(xf, wb[:, :rw_cols], wb[:, rw_cols:rw_cols + hg_cols],
                                         wb[:, rw_cols + hg_cols:])
    ya = _rwkv_branch(proj_rw.reshape(b, t, rw_cols), rw_mu, row2(rw_w0), rw_w_up, rw_a0, rw_a_up, rw_g_up,
                      rw_k_k, rw_k_a, rw_r_k, rw_gn_w, rw_gn_b)
    yb = _hgrn_branch(proj_hg.reshape(b, t, hg_cols), hg_lb_logits, hg_norm_w, layer)

    wr = jnp.zeros((LANES, d), F32)
    wr = wr.at[:N_GROUPS].set(router_g_w.T).at[ROUTER_EXPERT_ROW:ROUTER_EXPERT_ROW + ne].set(router_e_w.T)
    bias = jnp.zeros((LANES,), F32)
    bias = bias.at[:N_GROUPS].set(router_g_b).at[ROUTER_EXPERT_ROW:ROUTER_EXPERT_ROW + ne].set(router_e_b)
    x1, lt = _merge(xf, ya.reshape(n, -1), yb.reshape(n, -1), proj_gt, w_a_out.astype(BF16),
                    w_b_out.astype(BF16), w_o.astype(BF16), row2(ln1_g), row2(ln1_b), wr, alpha)
    route, seg, cnt = _route(lt, bias.reshape(LANES, 1))

    tm = TM_EXPERT
    ntile = n // TD_DISPATCH
    nblk = -(-(2 * n + (SEG_ALIGN - 1) * ne * ntile) // tm) + ne
    assert nblk <= TAB_LANES and TB_ROUTE == TD_DISPATCH
    segtab, tab = _finalize(seg, cnt, tm)
    per_seg = lambda row: segtab[:, row, :ne].reshape(-1)
    seglen, lstart, gstart, ltot = per_seg(0), per_seg(1), per_seg(2), segtab[:, 3, 0]
    blk_e, n_used = tab[0, :nblk], tab[1, :1]

    xbuf = _dispatch(seglen, lstart, gstart, ltot, tab[2, :ne], tab[3, :ne], n_used, x1, route, nblk * tm, tm)
    ybuf = _experts(blk_e, n_used, xbuf, w1, w3, w2)
    out = _combine(seglen, lstart, gstart, ltot, x1, route[2:6].T, p_i.reshape(n, -1), w_pe.astype(BF16),
                   w_pg.astype(BF16), row2(ln2_g), row2(ln2_b), ybuf, alpha)
    return out.reshape(b, t, d)


def kernel(x, p, w_in, rw_mu, rw_w0, rw_w_up, rw_a0, rw_a_up, rw_g_up, rw_k_k, rw_k_a, rw_r_k, rw_gn_w, rw_gn_b,
           w_a_out, hg_lb_logits, hg_norm_w, w_b_out, w_o, ln1_g, ln1_b, router_g_w, router_g_b, router_e_w,
           router_e_b, w1, w3, w2, ln2_g, ln2_b, w_pe, w_pg):
    depth = w_in.shape[0]
    alpha = (2 * depth) ** 0.25
    for i in range(depth):
        x = _layer(x, p[i], w_in[i], rw_mu[i], rw_w0[i], rw_w_up[i], rw_a0[i], rw_a_up[i], rw_g_up[i], rw_k_k[i],
                   rw_k_a[i], rw_r_k[i].reshape(-1), rw_gn_w[i], rw_gn_b[i], w_a_out[i], hg_lb_logits,
                   hg_norm_w[i], w_b_out[i], w_o[i], ln1_g[i], ln1_b[i],
                   router_g_w[i], router_g_b[i], router_e_w[i], router_e_b[i], w1[i], w3[i], w2[i], ln2_g[i],
                   ln2_b[i], w_pe[i], w_pg[i], alpha, i)
    return x
```

```python
import functools

import jax
import jax.numpy as jnp
from jax import lax
from jax.experimental import pallas as pl
from jax.experimental.pallas import tpu as pltpu

F32 = jnp.float32
BF16 = jnp.bfloat16
I32 = jnp.int32

NN = (((1,), (0,)), ((), ()))
NT = (((1,), (1,)), ((), ()))

RW_HEAD = 64
RW_DECAY_LORA = 64
RW_A_LORA = 64
RW_GATE_LORA = 128
RW_GN_EPS = 64e-5
RW_DECAY_SCALE = 0.6065306597126334
HG_HEADS = 4
N_GROUPS = 4
EXPERTS_PER_GROUP = 8
LN_EPS = 1e-5
RMS_EPS = 1e-6

CHUNK = 64
SUB = 16
LANES = 128
VMEM_LIMIT = 56 * 1024 * 1024

TM_PROJ = 256
TB_REC = 256
TM_MERGE = 512
TB_ROUTE = 512
TD_DISPATCH = 512
SEG_ALIGN = 16
TM_EXPERT = 256
RW_WIDE = 4
HG_WIDE = 4


def _dg(a, b, dn=NN):
    return lax.dot_general(a, b, dn, preferred_element_type=F32)


def _dot(a, b, dn=NN):
    return _dg(a.astype(BF16), b.astype(BF16), dn)


def _split(a):
    hi = a.astype(BF16)
    lo = (a - hi.astype(F32)).astype(BF16)
    return hi, lo


def _dot_hl(a, b_exact, dn=NN):
    hi, lo = _split(a)
    return _dg(hi, b_exact, dn) + _dg(lo, b_exact, dn)


def _dot3(a, b, dn=NN):
    ah, al = _split(a)
    bh, bl = _split(b)
    return _dg(ah, bh, dn) + (_dg(ah, bl, dn) + _dg(al, bh, dn))


def _cumsum_chunks(x, tri):
    h = x.astype(BF16)
    r1 = x - h.astype(F32)
    m = r1.astype(BF16)
    l = (r1 - m.astype(F32)).astype(BF16)
    return _dg(tri, h) + (_dg(tri, m) + _dg(tri, l))


def _sigmoid(x):
    return 0.5 * jnp.tanh(0.5 * x) + 0.5


def _layer_norm(h, g, b):
    mu = jnp.mean(h, axis=-1, keepdims=True)
    d = h - mu
    var = jnp.mean(d * d, axis=-1, keepdims=True)
    return d * lax.rsqrt(var + LN_EPS) * g + b


def _params(sem):
    return pltpu.CompilerParams(dimension_semantics=sem, vmem_limit_bytes=VMEM_LIMIT)


def _proj_kernel(x_ref, wr_ref, wh_ref, wg_ref, pr_ref, ph_ref, pg_ref):
    xb = x_ref[...].astype(BF16)
    pr_ref[...] = _dg(xb, wr_ref[...])
    ph_ref[...] = _dg(xb, wh_ref[...])
    pg_ref[...] = _dg(xb, wg_ref[...])


def _project(xf, w_rw, w_hg, w_gt):
    n, d = xf.shape
    tm = TM_PROJ
    full = lambda w: pl.BlockSpec(w.shape, lambda i: (0, 0))
    tile = lambda c: pl.BlockSpec((tm, c), lambda i: (i, 0))
    return pl.pallas_call(
        _proj_kernel,
        grid=(n // tm,),
        in_specs=[tile(d), full(w_rw), full(w_hg), full(w_gt)],
        out_specs=[tile(w_rw.shape[1]), tile(w_hg.shape[1]), tile(w_gt.shape[1])],
        out_shape=[jax.ShapeDtypeStruct((n, w.shape[1]), F32) for w in (w_rw, w_hg, w_gt)],
        compiler_params=_params(("parallel",)),
    )(xf, w_rw, w_hg, w_gt)


def _each(f, *ls):
    return [f(*xs) for xs in zip(*ls)]


def _two(x):
    m1 = lax.broadcasted_iota(I32, x.shape, 1) < RW_HEAD
    return jnp.concatenate([jnp.where(m1, x, 0.0), jnp.where(m1, 0.0, x)], axis=0)


def _rwkv_chunk_prepare(ins, lvl_ref):
    c = CHUNK
    lane = lax.broadcasted_iota(I32, (c, LANES), 1)
    row = lax.broadcasted_iota(I32, (c, LANES), 0)
    scol = jnp.bitwise_and(lane, RW_HEAD - 1)
    strict = row > scol
    incl = row >= scol
    r2 = lax.broadcasted_iota(I32, (LANES, LANES), 0)
    c2 = lax.broadcasted_iota(I32, (LANES, LANES), 1)
    eye = jnp.where(r2 == c2, 1.0, 0.0).astype(F32)

    def prep(r, k, v, av, bv, lw, lc):
        l_end = lc[c - 1:c]
        e_r = jnp.exp(l_end - lc)
        e_n = jnp.exp(-lc)
        return dict(at=av * jnp.exp(lc - lw), rt=r * jnp.exp(lc), bt=bv * e_n, kt=k * e_n,
                    bk=jnp.concatenate([bv * e_r, k * e_r], axis=0), pc=jnp.exp(l_end), v=v)

    q = [prep(*xs) for xs in ins]
    p = [_dot(jnp.concatenate([d["at"], d["rt"]], axis=0),
              jnp.concatenate([_two(d["bt"]), _two(d["kt"])], axis=0), NT) for d in q]
    sab = [jnp.where(strict, x[:c, :LANES], 0.0) for x in p]
    sak = [jnp.where(strict, x[:c, LANES:], 0.0) for x in p]
    srb = [jnp.where(incl, x[c:, :LANES], 0.0) for x in p]
    srk = [jnp.where(incl, x[c:, LANES:], 0.0) for x in p]
    sv = _each(lambda ak, rk, d: _dot(jnp.concatenate([ak, rk], axis=0), _two(d["v"])), sak, srk, q)

    a_bd = [_two(x) for x in sab]
    t = [eye + a * lvl_ref[0] for a in a_bd]
    for lv in range(1, lvl_ref.shape[0]):
        ta = _each(lambda t_, a: _dot(t_, a * lvl_ref[lv]), t, a_bd)
        t = _each(lambda t_, ta_: t_ + _dot(ta_, t_), t, ta)

    x = _each(lambda t_, d, sv_: _dot(t_, jnp.concatenate([_two(d["at"]), _two(sv_[:c])], axis=1)), t, q, sv)
    return [dict(uk=x_[:c, :LANES] + x_[c:, :LANES],
                 w=x_[:c, LANES:] + x_[c:, LANES:],
                 rt=d["rt"], rkv=sv_[c:], srb=srb_, bk=d["bk"], v=d["v"], pc=d["pc"])
            for x_, d, sv_, srb_ in zip(x, q, sv, srb)]


def _rwkv_chunk_apply(prep, states, bd):
    c = CHUNK
    g1 = _each(lambda d, s: _dot(jnp.concatenate([d["uk"], d["rt"]], axis=0), s, NT), prep, states)
    u = _each(lambda g, d: g[:c] + d["w"], g1, prep)
    y = _each(lambda g, d, u_: g[c:] + d["rkv"] + _dot(d["srb"], _two(u_)), g1, prep, u)
    upd = _each(lambda u_, d: _dot(jnp.concatenate([u_, d["v"]], axis=0).T, d["bk"]), u, prep)
    s_new = _each(lambda d, s, up: s * d["pc"] + bd * up, prep, states, upd)
    return list(zip(y, s_new))


def _rwkv_kernel(u_ref, mu_ref, w0_ref, wup_ref, a0_ref, aup_ref, gup_ref, kk_ref, ka_ref, rk_ref,
                 gnw_ref, gnb_ref, tri_ref, gsum_ref, lvl_ref, bd_ref, ya_ref,
                 s_ref, prev_ref, r_s, k_s, v_s, a_s, b_s, lw_s, lc_s, g_s, y_s):
    tb = pl.program_id(1)

    @pl.when(tb == 0)
    def _():
        s_ref[...] = jnp.zeros_like(s_ref)
        prev_ref[...] = jnp.zeros_like(prev_ref)

    u = u_ref[0]
    nt = u.shape[0]
    dim = r_s.shape[1]
    rowid = lax.broadcasted_iota(I32, u.shape, 0)
    shifted = jnp.where(rowid == 0, prev_ref[...], pltpu.roll(u, 1, axis=0))
    prev_ref[...] = u[nt - 1:nt, :]
    um = u + (shifted - u) * mu_ref[...]

    r = um[:, 0:dim]
    k = um[:, dim:2 * dim]
    v = um[:, 2 * dim:3 * dim]
    xwa = um[:, 3 * dim:3 * dim + LANES]
    xg = um[:, 3 * dim + LANES:3 * dim + 2 * LANES]

    wpre = w0_ref[...] + _dot(jnp.tanh(xwa), wup_ref[...])
    lw = -RW_DECAY_SCALE * _sigmoid(wpre)
    a = _sigmoid(a0_ref[...] + _dot(xwa, aup_ref[...]))
    g_s[...] = _dot(_sigmoid(xg), gup_ref[...])
    kk = k * kk_ref[...]
    ss = _dot(kk * kk, gsum_ref[...])
    kk = kk * lax.rsqrt(jnp.maximum(ss, 1e-24))
    r_s[...] = r
    k_s[...] = k * (1.0 + (a - 1.0) * ka_ref[...])
    v_s[...] = v
    a_s[...] = -kk
    b_s[...] = kk * a
    lw_s[...] = lw
    lc_s[...] = _cumsum_chunks(lw, tri_ref[...])
    bd = bd_ref[...]

    npair = dim // LANES
    lanes = [slice(p * LANES, (p + 1) * LANES) for p in range(npair)]

    def group_body(gi, carry):
        rows = [pl.ds(pl.multiple_of((gi * RW_WIDE + ci) * CHUNK, CHUNK), CHUNK) for ci in range(RW_WIDE)]
        ins = [(r_s[rw, ls], k_s[rw, ls], v_s[rw, ls], a_s[rw, ls], b_s[rw, ls], lw_s[rw, ls], lc_s[rw, ls])
               for rw in rows for ls in lanes]
        prep = _rwkv_chunk_prepare(ins, lvl_ref)
        states = [s_ref[p] for p in range(npair)]
        for ci, rw in enumerate(rows):
            outs = _rwkv_chunk_apply(prep[ci * npair:(ci + 1) * npair], states, bd)
            states = [s_new for _, s_new in outs]
            for (y, _), ls in zip(outs, lanes):
                y_s[rw, ls] = y
        for p in range(npair):
            s_ref[p] = states[p]
        return carry

    lax.fori_loop(0, nt // (CHUNK * RW_WIDE), group_body, 0)

    y = y_s[...]
    gsum = gsum_ref[...]
    inv_n = 1.0 / RW_HEAD
    m = _dot_hl(y, gsum) * inv_n
    d = y - m
    var = _dot(d * d, gsum) * inv_n
    yn = d * lax.rsqrt(var + RW_GN_EPS) * gnw_ref[...] + gnb_ref[...]
    bonus = _dot(r_s[...] * k_s[...] * rk_ref[...], gsum) * v_s[...]
    ya_ref[0] = ((yn + bonus) * g_s[...]).astype(BF16)


def _rwkv_branch(proj_rw, mu, w0, wup, a0, aup, gup, k_k, k_a, r_k, gn_w, gn_b):
    b, t, cols = proj_rw.shape
    dim = w0.shape[1]
    tb = TB_REC
    ii = jnp.arange(tb)
    tri = ((ii[:, None] // CHUNK == ii[None, :] // CHUNK) & (ii[:, None] >= ii[None, :])).astype(BF16)
    jj = jnp.arange(dim)
    gsum = (jj[:, None] // RW_HEAD == jj[None, :] // RW_HEAD).astype(BF16)
    rr = jnp.arange(LANES)[:, None]
    cc = jnp.arange(LANES)[None, :]
    lvls = []
    s = 1
    while s < CHUNK:
        lvls.append(((rr // (2 * s) == cc // (2 * s)) & ((rr // s) % 2 == 1) & ((cc // s) % 2 == 0)).astype(F32))
        s *= 2
    lvl = jnp.stack(lvls)
    bd = (rr // RW_HEAD == cc // RW_HEAD).astype(F32)
    zpad = lambda rows: jnp.zeros((rows, dim), F32)
    wup_p = jnp.concatenate([wup, zpad(LANES - wup.shape[0])], axis=0).astype(BF16)
    aup_p = jnp.concatenate([zpad(LANES - aup.shape[0]), aup], axis=0).astype(BF16)
    row2 = lambda a_: a_.reshape(1, -1)
    const = lambda a_: pl.BlockSpec(a_.shape, lambda bi, ti: (0,) * a_.ndim)
    args = [row2(mu), row2(w0), wup_p, row2(a0), aup_p, gup.astype(BF16), row2(k_k), row2(k_a), row2(r_k),
            row2(gn_w), row2(gn_b), tri, gsum, lvl, bd]
    sc = lambda: pltpu.VMEM((tb, dim), F32)
    return pl.pallas_call(
        _rwkv_kernel,
        grid=(b, t // tb),
        in_specs=[pl.BlockSpec((1, tb, cols), lambda bi, ti: (bi, ti, 0))] + [const(a_) for a_ in args],
        out_specs=pl.BlockSpec((1, tb, dim), lambda bi, ti: (bi, ti, 0)),
        out_shape=jax.ShapeDtypeStruct((b, t, dim), BF16),
        scratch_shapes=[pltpu.VMEM((dim // LANES, LANES, LANES), F32), pltpu.VMEM((1, cols), F32)]
                       + [sc() for _ in range(9)],
        compiler_params=_params(("arbitrary", "arbitrary")),
    )(proj_rw, *args)


def _hgrn_chunk_prepare(ins):
    c = CHUNK
    subs = [(SUB * i, SUB * (i + 1)) for i in range(c // SUB)]

    def scores(q, k, lf, bc, lo, hi):
        m = bc[lo:lo + 1] - lf[lo:lo + 1]
        att = _dot(q[lo:hi] * jnp.exp(bc[lo:hi] - m), k[:hi] * jnp.exp(m - bc[:hi]), NT)
        tt = lax.broadcasted_iota(I32, (SUB, hi), 0) + lo
        s_ = lax.broadcasted_iota(I32, (SUB, hi), 1)
        return jnp.where(s_ <= tt, att, 0.0)

    att = [[scores(q, k, lf, bc, lo, hi) for lo, hi in subs] for q, k, v, lf, bc in ins]
    upd = [_dot(v.T, k * jnp.exp(bc[c - 1:c] - bc)) for q, k, v, lf, bc in ins]
    intra = [[_dot(a, x[2][:hi]) for a, (lo, hi) in zip(arow, subs)] for arow, x in zip(att, ins)]
    return [dict(intra=jnp.concatenate(rows, axis=0), qe=x[0] * jnp.exp(x[4]), upd=up, pc=jnp.exp(x[4][c - 1:c]))
            for rows, x, up in zip(intra, ins, upd)]


def _hgrn_chunk_apply(prep, states):
    o = _each(lambda d, st: d["intra"] + _dot(d["qe"], st, NT), prep, states)
    st_new = _each(lambda d, st: st * d["pc"] + d["upd"], prep, states)
    return list(zip(o, st_new))


def _hgrn_kernel(layer, u_ref, lbl_ref, nw_ref, tri_ref, gsum_ref, yb_ref,
                 st_ref, q_s, k_s, v_s, lf_s, bc_s, o_s):
    tb = pl.program_id(1)

    @pl.when(tb == 0)
    def _():
        st_ref[...] = jnp.zeros_like(st_ref)

    u = u_ref[0]
    nt = u.shape[0]
    dim = q_s.shape[1]
    lbl = lbl_ref[...]
    e = jnp.exp(lbl - jnp.max(lbl, axis=0, keepdims=True))
    lb = jnp.sum(e[0:layer + 1], axis=0, keepdims=True) / jnp.sum(e, axis=0, keepdims=True)
    zf = u[:, dim:2 * dim]
    sig = _sigmoid(zf)
    f = lb + (1.0 - lb) * sig
    qin = u[:, 0:dim]
    q_s[...] = qin * _sigmoid(qin)
    k_s[...] = (1.0 - lb) * (1.0 - sig)
    v_s[...] = u[:, 2 * dim:3 * dim]
    lf = jnp.log(f)
    lf_s[...] = lf
    bc_s[...] = _cumsum_chunks(lf, tri_ref[...])

    nhead = dim // LANES
    lanes = [slice(h * LANES, (h + 1) * LANES) for h in range(nhead)]

    def group_body(gi, carry):
        rows = [pl.ds(pl.multiple_of((gi * HG_WIDE + ci) * CHUNK, CHUNK), CHUNK) for ci in range(HG_WIDE)]
        ins = [(q_s[rw, ls], k_s[rw, ls], v_s[rw, ls], lf_s[rw, ls], bc_s[rw, ls]) for rw in rows for ls in lanes]
        prep = _hgrn_chunk_prepare(ins)
        states = [st_ref[h] for h in range(nhead)]
        for ci, rw in enumerate(rows):
            outs = _hgrn_chunk_apply(prep[ci * nhead:(ci + 1) * nhead], states)
            states = [st_new for _, st_new in outs]
            for (o, _), ls in zip(outs, lanes):
                o_s[rw, ls] = o
        for h in range(nhead):
            st_ref[h] = states[h]
        return carry

    lax.fori_loop(0, nt // (CHUNK * HG_WIDE), group_body, 0)

    o = o_s[...]
    ms = _dot(o * o, gsum_ref[...]) * (1.0 / LANES)
    og = u[:, 3 * dim:4 * dim]
    yb_ref[0] = (o * lax.rsqrt(ms + RMS_EPS) * nw_ref[...] * _sigmoid(og)).astype(BF16)


def _hgrn_branch(proj_hg, lb_logits, norm_w, layer):
    b, t, cols = proj_hg.shape
    dim = cols // 4
    tb = TB_REC
    ii = jnp.arange(tb)
    tri = ((ii[:, None] // CHUNK == ii[None, :] // CHUNK) & (ii[:, None] >= ii[None, :])).astype(BF16)
    jj = jnp.arange(dim)
    gsum = (jj[:, None] // LANES == jj[None, :] // LANES).astype(BF16)
    const = lambda a_: pl.BlockSpec(a_.shape, lambda bi, ti: (0,) * a_.ndim)
    args = [lb_logits, norm_w.reshape(1, -1), tri, gsum]
    sc = lambda: pltpu.VMEM((tb, dim), F32)
    return pl.pallas_call(
        functools.partial(_hgrn_kernel, layer),
        grid=(b, t // tb),
        in_specs=[pl.BlockSpec((1, tb, cols), lambda bi, ti: (bi, ti, 0))] + [const(a_) for a_ in args],
        out_specs=pl.BlockSpec((1, tb, dim), lambda bi, ti: (bi, ti, 0)),
        out_shape=jax.ShapeDtypeStruct((b, t, dim), BF16),
        scratch_shapes=[pltpu.VMEM((dim // LANES, LANES, LANES), F32)] + [sc() for _ in range(6)],
        compiler_params=_params(("arbitrary", "arbitrary")),
    )(proj_hg, *args)


def _merge_kernel(alpha, x_ref, ya_ref, yb_ref, pg_ref, wa_ref, wb_ref, wo_ref, g_ref, b_ref, wr_ref,
                  x1_ref, lt_ref):
    d = x_ref.shape[1]
    gates = _sigmoid(pg_ref[...])
    merged = gates[:, :d] * _dg(ya_ref[...], wa_ref[...]) + gates[:, d:] * _dg(yb_ref[...], wb_ref[...])
    h = alpha * x_ref[...] + _dot(merged, wo_ref[...])
    x1 = _layer_norm(h, g_ref[...], b_ref[...])
    x1_ref[...] = x1
    lt_ref[...] = _dot3(wr_ref[...], x1, NT)


def _merge(xf, ya, yb, pgate, wa, wb, wo, g, bta, wr, alpha):
    n, d = xf.shape
    tm = TM_MERGE
    tile = lambda a_: pl.BlockSpec((tm, a_.shape[1]), lambda i: (i, 0))
    const = lambda a_: pl.BlockSpec(a_.shape, lambda i: (0, 0))
    return pl.pallas_call(
        functools.partial(_merge_kernel, alpha),
        grid=(n // tm,),
        in_specs=[tile(xf), tile(ya), tile(yb), tile(pgate), const(wa), const(wb), const(wo), const(g),
                  const(bta), const(wr)],
        out_specs=[pl.BlockSpec((tm, d), lambda i: (i, 0)), pl.BlockSpec((LANES, tm), lambda i: (0, i))],
        out_shape=[jax.ShapeDtypeStruct((n, d), F32), jax.ShapeDtypeStruct((LANES, n), F32)],
        compiler_params=_params(("parallel",)),
    )(xf, ya, yb, pgate, wa, wb, wo, g, bta, wr)


ROUTER_EXPERT_ROW = 8


def _to_lanes(col, nl):
    ne = col.shape[0]
    diag = lax.broadcasted_iota(I32, (ne, nl), 0) == lax.broadcasted_iota(I32, (ne, nl), 1)
    return jnp.sum(jnp.where(diag, col, 0.0), axis=0, keepdims=True)


def _route_kernel(lt_ref, bias_ref, upper_ref, lower_ref, route_ref, seg_ref, cnt_ref, carry_ref):
    @pl.when(pl.program_id(0) == 0)
    def _():
        carry_ref[...] = jnp.zeros_like(carry_ref)

    ne = N_GROUPS * EXPERTS_PER_GROUP
    lt = lt_ref[...] + bias_ref[...]
    nb = lt.shape[1]
    neg = -jnp.inf
    lg = lt[0:8]
    rg = lax.broadcasted_iota(I32, (8, nb), 0).astype(F32)
    lg = jnp.where(rg < N_GROUPS, lg, neg)
    mg = jnp.max(lg, axis=0, keepdims=True)
    gidx = jnp.min(jnp.where(lg == mg, rg, 1e9), axis=0, keepdims=True)
    pg_sel = 1.0 / jnp.sum(jnp.exp(lg - mg), axis=0, keepdims=True)

    le = lt[ROUTER_EXPERT_ROW:ROUTER_EXPERT_ROW + ne]
    re = lax.broadcasted_iota(I32, (ne, nb), 0).astype(F32)
    in_group = jnp.floor(re * (1.0 / EXPERTS_PER_GROUP)) == gidx
    l1 = jnp.where(in_group, le, neg)
    m1 = jnp.max(l1, axis=0, keepdims=True)
    i1 = jnp.min(jnp.where(l1 == m1, re, 1e9), axis=0, keepdims=True)
    l2 = jnp.where(re == i1, neg, l1)
    m2 = jnp.max(l2, axis=0, keepdims=True)
    i2 = jnp.min(jnp.where(l2 == m2, re, 1e9), axis=0, keepdims=True)
    e2 = jnp.exp(m2 - m1)
    w1 = pg_sel / (1.0 + e2)
    w2 = pg_sel * e2 / (1.0 + e2)

    sel1 = re == i1
    sel2 = re == i2
    onehot = jnp.where(sel1 | sel2, 1.0, 0.0)
    before = _dg(onehot.astype(BF16), upper_ref[...])
    cnt_t = jnp.sum(onehot, axis=1, keepdims=True)
    seg = jnp.floor((cnt_t + (SEG_ALIGN - 1)) * (1.0 / SEG_ALIGN)) * SEG_ALIGN
    lstart = _dg(lower_ref[...], jnp.broadcast_to(seg, (ne, LANES)).astype(BF16))[:, 0:1]
    tot = lstart + before
    lpos1 = jnp.sum(jnp.where(sel1, tot, 0.0), axis=0, keepdims=True)
    lpos2 = jnp.sum(jnp.where(sel2, tot, 0.0), axis=0, keepdims=True)
    grel = carry_ref[...]
    carry = grel + seg
    carry_ref[...] = carry
    cnt_ref[...] = jnp.broadcast_to(carry, cnt_ref.shape)
    zero = jnp.zeros_like(w1)
    route_ref[...] = jnp.concatenate([i1, i2, lpos1, lpos2, w1, w2, zero, zero], axis=0)
    nl = seg_ref.shape[2]
    zl = jnp.zeros((1, nl), F32)
    ltot = jnp.broadcast_to(jnp.sum(seg, axis=0, keepdims=True), (1, nl))
    seg_ref[0] = jnp.concatenate([_to_lanes(seg, nl), _to_lanes(lstart, nl), _to_lanes(grel, nl), ltot,
                                  zl, zl, zl, zl], axis=0)


def _route(lt, bias_col):
    n = lt.shape[1]
    tb = TB_ROUTE
    ne = N_GROUPS * EXPERTS_PER_GROUP
    ii = jnp.arange(tb)
    upper = (ii[:, None] < ii[None, :]).astype(BF16)
    ee = jnp.arange(ne)
    lower = (ee[:, None] > ee[None, :]).astype(BF16)
    return pl.pallas_call(
        _route_kernel,
        grid=(n // tb,),
        in_specs=[pl.BlockSpec((LANES, tb), lambda i: (0, i)),
                  pl.BlockSpec((LANES, 1), lambda i: (0, 0)),
                  pl.BlockSpec((tb, tb), lambda i: (0, 0)),
                  pl.BlockSpec((ne, ne), lambda i: (0, 0))],
        out_specs=[pl.BlockSpec((8, tb), lambda i: (0, i)), pl.BlockSpec((1, 8, LANES), lambda i: (i, 0, 0)),
                   pl.BlockSpec((ne, LANES), lambda i: (0, 0))],
        out_shape=[jax.ShapeDtypeStruct((8, n), F32), jax.ShapeDtypeStruct((n // tb, 8, LANES), F32),
                   jax.ShapeDtypeStruct((ne, LANES), F32)],
        scratch_shapes=[pltpu.VMEM((ne, 1), F32)],
        compiler_params=_params(("arbitrary",)),
    )(lt, bias_col, upper, lower)


TAB_LANES = LANES


def _finalize_kernel(tm, seg_ref, cnt_ref, lower_ref, segtab_ref, tab_ref):
    ne = cnt_ref.shape[0]
    cnt = cnt_ref[...]
    nb = jnp.floor((cnt + (tm - 1)) * (1.0 / tm))
    bstart = _dg(lower_ref[...], nb.astype(BF16))
    bend = bstart + nb
    pad_start = bstart[:, 0:1] * tm
    seg = seg_ref[0]
    gstart = seg[2:3] + _to_lanes(pad_start, seg.shape[1])
    segtab_ref[0] = jnp.concatenate([seg[0:2], gstart, seg[3:8]], axis=0).astype(I32)

    nl = tab_ref.shape[1]
    n_used = jnp.max(bend[:, 0:1], axis=0, keepdims=True)
    pad_lo = _to_lanes(pad_start + cnt[:, 0:1], nl)
    pad_hi = _to_lanes(bend[:, 0:1] * tm, nl)
    zero = jnp.zeros((1, nl), F32)
    tab_ref[...] = jnp.concatenate([_to_lanes(bstart[:, 0:1], nl), jnp.broadcast_to(n_used, (1, nl)), pad_lo,
                                    pad_hi, _to_lanes(nb[:, 0:1], nl), zero, zero, zero], axis=0).astype(I32)


def _finalize(seg, cnt, tm):
    ntile = seg.shape[0]
    ne = cnt.shape[0]
    ii = jnp.arange(ne)
    lower = (ii[:, None] > ii[None, :]).astype(BF16)
    return pl.pallas_call(
        functools.partial(_finalize_kernel, tm),
        grid=(ntile,),
        in_specs=[pl.BlockSpec((1, 8, LANES), lambda i: (i, 0, 0)), pl.BlockSpec(cnt.shape, lambda i: (0, 0)),
                  pl.BlockSpec((ne, ne), lambda i: (0, 0))],
        out_specs=[pl.BlockSpec((1, 8, LANES), lambda i: (i, 0, 0)), pl.BlockSpec((8, TAB_LANES), lambda i: (0, 0))],
        out_shape=[jax.ShapeDtypeStruct((ntile, 8, LANES), I32), jax.ShapeDtypeStruct((8, TAB_LANES), I32)],
        compiler_params=_params(("arbitrary",)),
    )(seg, cnt, lower)


def _for_each_piece(length, max_len, fn):
    size = SEG_ALIGN
    sizes = []
    while size <= max_len:
        sizes.append(size)
        size *= 2
    for size in reversed(sizes):
        @pl.when(jnp.bitwise_and(length, size) != 0)
        def _(size=size):
            fn(pl.multiple_of(jnp.bitwise_and(length, -2 * size), SEG_ALIGN), size)


def _sorted_rows(td):
    return 2 * td + N_GROUPS * EXPERTS_PER_GROUP * SEG_ALIGN


def _dispatch_kernel(tm, seglen_ref, lstart_ref, gstart_ref, ltot_ref, plo_ref, phi_ref, nu_ref,
                     x_ref, route_ref, xbuf_ref, sorted_ref, zblk, sem, zsem):
    i = pl.program_id(0)
    nsteps = pl.num_programs(0)
    td = x_ref.shape[0]
    ne = plo_ref.shape[0]
    nblk = xbuf_ref.shape[0] // tm
    nrow = sorted_ref.shape[1]
    buf = i % 2

    lpos = route_ref[2:4, :]
    r = lax.broadcasted_iota(I32, (nrow, td), 0).astype(F32)
    onehot = jnp.where((r == lpos[0:1]) | (r == lpos[1:2]), 1.0, 0.0).astype(BF16)
    sorted_ref[buf] = _dg(onehot, x_ref[...].astype(BF16)).astype(BF16)

    def wait_tile(step, b):
        _for_each_piece(ltot_ref[step], nrow, lambda off, size: pltpu.make_async_copy(
            sorted_ref.at[b, pl.ds(0, size), :], xbuf_ref.at[pl.ds(0, size), :], sem.at[b]).wait())

    for e in range(ne):
        idx = i * ne + e
        ls = pl.multiple_of(lstart_ref[idx], SEG_ALIGN)
        gs = pl.multiple_of(gstart_ref[idx], SEG_ALIGN)
        _for_each_piece(seglen_ref[idx], td, lambda off, size: pltpu.make_async_copy(
            sorted_ref.at[buf, pl.ds(ls + off, size), :], xbuf_ref.at[pl.ds(gs + off, size), :],
            sem.at[buf]).start())

    @pl.when(i > 0)
    def _():
        wait_tile(i - 1, 1 - buf)

    @pl.when(i == nsteps - 1)
    def _():
        wait_tile(i, buf)

    def pad_fill(fn):
        for e in range(ne):
            lo = pl.multiple_of(plo_ref[e], SEG_ALIGN)
            _for_each_piece(phi_ref[e] - lo, tm // 2, lambda off, size: fn(pltpu.make_async_copy(
                zblk.at[pl.ds(0, size), :], xbuf_ref.at[pl.ds(lo + off, size), :], zsem)))

        def per_blk(b, carry):
            fn(pltpu.make_async_copy(zblk, xbuf_ref.at[pl.ds(pl.multiple_of(b * tm, tm), tm), :], zsem))
            return carry
        lax.fori_loop(nu_ref[0], nblk, per_blk, 0)

    @pl.when(i == 0)
    def _():
        zblk[...] = jnp.zeros_like(zblk)
        pad_fill(lambda cp: cp.start())
        pad_fill(lambda cp: cp.wait())


def _dispatch(seglen, lstart, gstart, ltot, pad_lo, pad_hi, n_used, x1, route, rows, tm):
    n, d = x1.shape
    td = TD_DISPATCH
    return pl.pallas_call(
        functools.partial(_dispatch_kernel, tm),
        grid_spec=pltpu.PrefetchScalarGridSpec(
            num_scalar_prefetch=7,
            grid=(n // td,),
            in_specs=[pl.BlockSpec((td, d), lambda i, *_: (i, 0)), pl.BlockSpec((8, td), lambda i, *_: (0, i))],
            out_specs=pl.BlockSpec(memory_space=pl.ANY),
            scratch_shapes=[pltpu.VMEM((2, _sorted_rows(td), d), BF16), pltpu.VMEM((tm, d), BF16),
                            pltpu.SemaphoreType.DMA((2,)), pltpu.SemaphoreType.DMA(())],
        ),
        out_shape=jax.ShapeDtypeStruct((rows, d), BF16),
        compiler_params=_params(("arbitrary",)),
    )(seglen, lstart, gstart, ltot, pad_lo, pad_hi, n_used, x1, route)


def _expert_kernel(tm, bstart_ref, nb_ref, nu_ref, w1_ref, w3_ref, w2_ref, xbuf_ref, ybuf_ref,
                   w1b, w3b, w2b, xb, yb, semx, semy):
    e = pl.program_id(0)
    nb = nb_ref[e]
    b0 = bstart_ref[e]
    nblk = ybuf_ref.shape[0] // tm
    rows = lambda blk: pl.ds(pl.multiple_of(blk * tm, tm), tm)
    x_copy = lambda j, slot: pltpu.make_async_copy(xbuf_ref.at[rows(b0 + j), :], xb.at[slot], semx.at[slot])
    y_copy = lambda blk, slot: pltpu.make_async_copy(yb.at[slot], ybuf_ref.at[rows(blk), :], semy.at[slot])

    @pl.when(nb > 0)
    def _():
        x_copy(0, 0).start()

    w1b[...] = w1_ref[0].astype(BF16)
    w3b[...] = w3_ref[0].astype(BF16)
    w2b[...] = w2_ref[0].astype(BF16)

    def body(j, carry):
        slot = j % 2
        x_copy(j, slot).wait()

        @pl.when(j + 1 < nb)
        def _():
            x_copy(j + 1, 1 - slot).start()

        @pl.when(j >= 2)
        def _():
            y_copy(b0 + j - 2, slot).wait()

        x = xb[slot]
        h1 = _dg(x, w1b[...])
        h3 = _dg(x, w3b[...])
        h = (h1 * _sigmoid(h1)) * h3
        yb[slot] = _dot(h, w2b[...]).astype(BF16)
        y_copy(b0 + j, slot).start()
        return carry

    lax.fori_loop(0, nb, body, 0)

    @pl.when(nb >= 2)
    def _():
        y_copy(b0 + nb - 2, nb % 2).wait()

    @pl.when(nb >= 1)
    def _():
        y_copy(b0 + nb - 1, (nb - 1) % 2).wait()

    @pl.when(e == pl.num_programs(0) - 1)
    def _():
        yb[0] = jnp.zeros(yb.shape[1:], yb.dtype)

        def fill(fn):
            def per_blk(blk, carry):
                fn(y_copy(blk, 0))
                return carry
            lax.fori_loop(nu_ref[0], nblk, per_blk, 0)

        fill(lambda cp: cp.start())
        fill(lambda cp: cp.wait())


def _experts(bstart, nb, n_used, xbuf, w1, w3, w2):
    rows, d = xbuf.shape
    ne, _, de = w1.shape
    tm = TM_EXPERT
    return pl.pallas_call(
        functools.partial(_expert_kernel, tm),
        grid_spec=pltpu.PrefetchScalarGridSpec(
            num_scalar_prefetch=3,
            grid=(ne,),
            in_specs=[pl.BlockSpec((1, d, de), lambda e, *_: (e, 0, 0)),
                      pl.BlockSpec((1, d, de), lambda e, *_: (e, 0, 0)),
                      pl.BlockSpec((1, de, d), lambda e, *_: (e, 0, 0)),
                      pl.BlockSpec(memory_space=pl.ANY)],
            out_specs=pl.BlockSpec(memory_space=pl.ANY),
            scratch_shapes=[pltpu.VMEM((d, de), BF16), pltpu.VMEM((d, de), BF16), pltpu.VMEM((de, d), BF16),
                            pltpu.VMEM((2, tm, d), BF16), pltpu.VMEM((2, tm, d), BF16),
                            pltpu.SemaphoreType.DMA((2,)), pltpu.SemaphoreType.DMA((2,))],
        ),
        out_shape=jax.ShapeDtypeStruct((rows, d), BF16),
        compiler_params=_params(("arbitrary",)),
    )(bstart, nb, n_used, w1, w3, w2, xbuf)


def _combine_kernel(alpha, seglen_ref, lstart_ref, gstart_ref, ltot_ref, x1_ref, rt_ref, p_ref, wpe_ref, wpg_ref,
                    g_ref, b_ref, ybuf_ref, out_ref, sorted_ref, sem):
    i = pl.program_id(0)
    nsteps = pl.num_programs(0)
    tc = x1_ref.shape[0]
    ne = N_GROUPS * EXPERTS_PER_GROUP
    nrow = sorted_ref.shape[1]

    def fetch(step, buf):
        for e in range(ne):
            idx = step * ne + e
            ls = pl.multiple_of(lstart_ref[idx], SEG_ALIGN)
            gs = pl.multiple_of(gstart_ref[idx], SEG_ALIGN)
            _for_each_piece(seglen_ref[idx], tc, lambda off, size: pltpu.make_async_copy(
                ybuf_ref.at[pl.ds(gs + off, size), :], sorted_ref.at[buf, pl.ds(ls + off, size), :],
                sem.at[buf]).start())

    @pl.when(i == 0)
    def _():
        sorted_ref[...] = jnp.zeros_like(sorted_ref)
        fetch(0, 0)

    @pl.when(i + 1 < nsteps)
    def _():
        fetch(i + 1, (i + 1) % 2)

    cur = i % 2
    _for_each_piece(ltot_ref[i], nrow, lambda off, size: pltpu.make_async_copy(
        ybuf_ref.at[pl.ds(0, size), :], sorted_ref.at[cur, pl.ds(0, size), :], sem.at[cur]).wait())

    rt = rt_ref[...]
    r = lax.broadcasted_iota(I32, (tc, nrow), 1).astype(F32)
    unsort = jnp.where(r == rt[:, 0:1], rt[:, 2:3], 0.0) + jnp.where(r == rt[:, 1:2], rt[:, 3:4], 0.0)
    ffn = _dg(unsort.astype(BF16), sorted_ref[cur])
    x2 = _layer_norm(alpha * x1_ref[...] + ffn, g_ref[...], b_ref[...])
    gate = _sigmoid(_dot(x2, wpg_ref[...]))
    out_ref[...] = x2 + gate * _dot(p_ref[...], wpe_ref[...])


def _combine(seglen, lstart, gstart, ltot, x1, route_t, pf, wpe, wpg, g, bta, ybuf, alpha):
    n, d = x1.shape
    tc = TD_DISPATCH
    tile = lambda a_: pl.BlockSpec((tc, a_.shape[1]), lambda i, *_: (i, 0))
    const = lambda a_: pl.BlockSpec(a_.shape, lambda i, *_: (0, 0))
    return pl.pallas_call(
        functools.partial(_combine_kernel, alpha),
        grid_spec=pltpu.PrefetchScalarGridSpec(
            num_scalar_prefetch=4,
            grid=(n // tc,),
            in_specs=[tile(x1), tile(route_t), tile(pf), const(wpe), const(wpg), const(g), const(bta),
                      pl.BlockSpec(memory_space=pl.ANY)],
            out_specs=pl.BlockSpec((tc, d), lambda i, *_: (i, 0)),
            scratch_shapes=[pltpu.VMEM((2, _sorted_rows(tc), d), BF16), pltpu.SemaphoreType.DMA((2,))],
        ),
        out_shape=jax.ShapeDtypeStruct((n, d), F32),
        compiler_params=_params(("arbitrary",)),
    )(seglen, lstart, gstart, ltot, x1, route_t, pf, wpe, wpg, g, bta, ybuf)


def _layer(x, p_i, w_in, rw_mu, rw_w0, rw_w_up, rw_a0, rw_a_up, rw_g_up, rw_k_k, rw_k_a, rw_r_k, rw_gn_w,
           rw_gn_b, w_a_out, hg_lb_logits, hg_norm_w, w_b_out, w_o, ln1_g, ln1_b, router_g_w, router_g_b,
           router_e_w, router_e_b, w1, w3, w2, ln2_g, ln2_b, w_pe, w_pg, alpha, layer):
    b, t, d = x.shape
    n = b * t
    rw_dim = rw_w0.shape[0]
    rw_cols = 3 * rw_dim + RW_DECAY_LORA + RW_A_LORA + RW_GATE_LORA
    hg_cols = 4 * hg_norm_w.shape[0]
    ne = N_GROUPS * EXPERTS_PER_GROUP
    row2 = lambda a_: a_.reshape(1, -1)
    xf = x.reshape(n, d)

    wb = w_in.astype(BF16)
    proj_rw, proj_hg, proj_gt = _project(xf, wb[:, :rw_cols], wb[:, rw_cols:rw_cols + hg_cols],
                                         wb[:, rw_cols + hg_cols:])
    ya = _rwkv_branch(proj_rw.reshape(b, t, rw_cols), rw_mu, row2(rw_w0), rw_w_up, rw_a0, rw_a_up, rw_g_up,
                      rw_k_k, rw_k_a, rw_r_k, rw_gn_w, rw_gn_b)
    yb = _hgrn_branch(proj_hg.reshape(b, t, hg_cols), hg_lb_logits, hg_norm_w, layer)

    wr = jnp.zeros((LANES, d), F32)
    wr = wr.at[:N_GROUPS].set(router_g_w.T).at[ROUTER_EXPERT_ROW:ROUTER_EXPERT_ROW + ne].set(router_e_w.T)
    bias = jnp.zeros((LANES,), F32)
    bias = bias.at[:N_GROUPS].set(router_g_b).at[ROUTER_EXPERT_ROW:ROUTER_EXPERT_ROW + ne].set(router_e_b)
    x1, lt = _merge(xf, ya.reshape(n, -1), yb.reshape(n, -1), proj_gt, w_a_out.astype(BF16),
                    w_b_out.astype(BF16), w_o.astype(BF16), row2(ln1_g), row2(ln1_b), wr, alpha)
    route, seg, cnt = _route(lt, bias.reshape(LANES, 1))

    tm = TM_EXPERT
    ntile = n // TD_DISPATCH
    nblk = -(-(2 * n + (SEG_ALIGN - 1) * ne * ntile) // tm) + ne
    assert TB_ROUTE == TD_DISPATCH
    segtab, tab = _finalize(seg, cnt, tm)
    per_seg = lambda row: segtab[:, row, :ne].reshape(-1)
    seglen, lstart, gstart, ltot = per_seg(0), per_seg(1), per_seg(2), segtab[:, 3, 0]
    n_used = tab[1, :1]

    xbuf = _dispatch(seglen, lstart, gstart, ltot, tab[2, :ne], tab[3, :ne], n_used, x1, route, nblk * tm, tm)
    ybuf = _experts(tab[0, :ne], tab[4, :ne], n_used, xbuf, w1, w3, w2)
    out = _combine(seglen, lstart, gstart, ltot, x1, route[2:6].T, p_i.reshape(n, -1), w_pe.astype(BF16),
                   w_pg.astype(BF16), row2(ln2_g), row2(ln2_b), ybuf, alpha)
    return out.reshape(b, t, d)


def kernel(x, p, w_in, rw_mu, rw_w0, rw_w_up, rw_a0, rw_a_up, rw_g_up, rw_k_k, rw_k_a, rw_r_k, rw_gn_w, rw_gn_b,
           w_a_out, hg_lb_logits, hg_norm_w, w_b_out, w_o, ln1_g, ln1_b, router_g_w, router_g_b, router_e_w,
           router_e_b, w1, w3, w2, ln2_g, ln2_b, w_pe, w_pg):
    depth = w_in.shape[0]
    alpha = (2 * depth) ** 0.25
    for i in range(depth):
        x = _layer(x, p[i], w_in[i], rw_mu[i], rw_w0[i], rw_w_up[i], rw_a0[i], rw_a_up[i], rw_g_up[i], rw_k_k[i],
                   rw_k_a[i], rw_r_k[i].reshape(-1), rw_gn_w[i], rw_gn_b[i], w_a_out[i], hg_lb_logits,
                   hg_norm_w[i], w_b_out[i], w_o[i], ln1_g[i], ln1_b[i],
                   router_g_w[i], router_g_b[i], router_e_w[i], router_e_b[i], w1[i], w3[i], w2[i], ln2_g[i],
                   ln2_b[i], w_pe[i], w_pg[i], alpha, i)
    return x
```

```python
import functools

import jax
import jax.numpy as jnp
from jax import lax
from jax.experimental import pallas as pl
from jax.experimental.pallas import tpu as pltpu

F32 = jnp.float32
BF16 = jnp.bfloat16
I32 = jnp.int32

NN = (((1,), (0,)), ((), ()))
NT = (((1,), (1,)), ((), ()))

RW_HEAD = 64
RW_DECAY_LORA = 64
RW_A_LORA = 64
RW_GATE_LORA = 128
RW_GN_EPS = 64e-5
RW_DECAY_SCALE = 0.6065306597126334
HG_HEADS = 4
N_GROUPS = 4
EXPERTS_PER_GROUP = 8
LN_EPS = 1e-5
RMS_EPS = 1e-6

CHUNK = 64
SUB = 16
LANES = 128
VMEM_LIMIT = 56 * 1024 * 1024

TM_PROJ = 256
TB_REC = 256
TM_MERGE = 512
TB_ROUTE = 512
TD_DISPATCH = 512
SEG_ALIGN = 16
TM_EXPERT = 512
RW_WIDE = 4
HG_WIDE = 4


def _dg(a, b, dn=NN):
    return lax.dot_general(a, b, dn, preferred_element_type=F32)


def _dot(a, b, dn=NN):
    return _dg(a.astype(BF16), b.astype(BF16), dn)


def _split(a):
    hi = a.astype(BF16)
    lo = (a - hi.astype(F32)).astype(BF16)
    return hi, lo


def _dot_hl(a, b_exact, dn=NN):
    hi, lo = _split(a)
    return _dg(hi, b_exact, dn) + _dg(lo, b_exact, dn)


def _dot3(a, b, dn=NN):
    ah, al = _split(a)
    bh, bl = _split(b)
    return _dg(ah, bh, dn) + (_dg(ah, bl, dn) + _dg(al, bh, dn))


def _cumsum_chunks(x, tri):
    h = x.astype(BF16)
    r1 = x - h.astype(F32)
    m = r1.astype(BF16)
    l = (r1 - m.astype(F32)).astype(BF16)
    return _dg(tri, h) + (_dg(tri, m) + _dg(tri, l))


def _sigmoid(x):
    return 0.5 * jnp.tanh(0.5 * x) + 0.5


def _layer_norm(h, g, b):
    mu = jnp.mean(h, axis=-1, keepdims=True)
    d = h - mu
    var = jnp.mean(d * d, axis=-1, keepdims=True)
    return d * lax.rsqrt(var + LN_EPS) * g + b


def _params(sem):
    return pltpu.CompilerParams(dimension_semantics=sem, vmem_limit_bytes=VMEM_LIMIT)


def _proj_kernel(x_ref, wr_ref, wh_ref, wg_ref, pr_ref, ph_ref, pg_ref):
    xb = x_ref[...].astype(BF16)
    pr_ref[...] = _dg(xb, wr_ref[...])
    ph_ref[...] = _dg(xb, wh_ref[...])
    pg_ref[...] = _dg(xb, wg_ref[...])


def _project(xf, w_rw, w_hg, w_gt):
    n, d = xf.shape
    tm = TM_PROJ
    full = lambda w: pl.BlockSpec(w.shape, lambda i: (0, 0))
    tile = lambda c: pl.BlockSpec((tm, c), lambda i: (i, 0))
    return pl.pallas_call(
        _proj_kernel,
        grid=(n // tm,),
        in_specs=[tile(d), full(w_rw), full(w_hg), full(w_gt)],
        out_specs=[tile(w_rw.shape[1]), tile(w_hg.shape[1]), tile(w_gt.shape[1])],
        out_shape=[jax.ShapeDtypeStruct((n, w.shape[1]), F32) for w in (w_rw, w_hg, w_gt)],
        compiler_params=_params(("parallel",)),
    )(xf, w_rw, w_hg, w_gt)


def _each(f, *ls):
    return [f(*xs) for xs in zip(*ls)]


def _two(x):
    m1 = lax.broadcasted_iota(I32, x.shape, 1) < RW_HEAD
    return jnp.concatenate([jnp.where(m1, x, 0.0), jnp.where(m1, 0.0, x)], axis=0)


def _rwkv_chunk_prepare(ins, lvl_ref):
    c = CHUNK
    lane = lax.broadcasted_iota(I32, (c, LANES), 1)
    row = lax.broadcasted_iota(I32, (c, LANES), 0)
    scol = jnp.bitwise_and(lane, RW_HEAD - 1)
    strict = row > scol
    incl = row >= scol
    r2 = lax.broadcasted_iota(I32, (LANES, LANES), 0)
    c2 = lax.broadcasted_iota(I32, (LANES, LANES), 1)
    eye = jnp.where(r2 == c2, 1.0, 0.0).astype(F32)

    def prep(r, k, v, av, bv, lw, lc):
        l_end = lc[c - 1:c]
        e_r = jnp.exp(l_end - lc)
        e_n = jnp.exp(-lc)
        return dict(at=av * jnp.exp(lc - lw), rt=r * jnp.exp(lc), bt=bv * e_n, kt=k * e_n,
                    bk=jnp.concatenate([bv * e_r, k * e_r], axis=0), pc=jnp.exp(l_end), v=v)

    q = [prep(*xs) for xs in ins]
    p = [_dot(jnp.concatenate([d["at"], d["rt"]], axis=0),
              jnp.concatenate([_two(d["bt"]), _two(d["kt"])], axis=0), NT) for d in q]
    sab = [jnp.where(strict, x[:c, :LANES], 0.0) for x in p]
    sak = [jnp.where(strict, x[:c, LANES:], 0.0) for x in p]
    srb = [jnp.where(incl, x[c:, :LANES], 0.0) for x in p]
    srk = [jnp.where(incl, x[c:, LANES:], 0.0) for x in p]
    sv = _each(lambda ak, rk, d: _dot(jnp.concatenate([ak, rk], axis=0), _two(d["v"])), sak, srk, q)

    a_bd = [_two(x) for x in sab]
    t = [eye + a * lvl_ref[0] for a in a_bd]
    for lv in range(1, lvl_ref.shape[0]):
        ta = _each(lambda t_, a: _dot(t_, a * lvl_ref[lv]), t, a_bd)
        t = _each(lambda t_, ta_: t_ + _dot(ta_, t_), t, ta)

    x = _each(lambda t_, d, sv_: _dot(t_, jnp.concatenate([_two(d["at"]), _two(sv_[:c])], axis=1)), t, q, sv)
    return [dict(uk=x_[:c, :LANES] + x_[c:, :LANES],
                 w=x_[:c, LANES:] + x_[c:, LANES:],
                 rt=d["rt"], rkv=sv_[c:], srb=srb_, bk=d["bk"], v=d["v"], pc=d["pc"])
            for x_, d, sv_, srb_ in zip(x, q, sv, srb)]


def _rwkv_chunk_apply(prep, states, bd):
    c = CHUNK
    g1 = _each(lambda d, s: _dot(jnp.concatenate([d["uk"], d["rt"]], axis=0), s, NT), prep, states)
    u = _each(lambda g, d: g[:c] + d["w"], g1, prep)
    y = _each(lambda g, d, u_: g[c:] + d["rkv"] + _dot(d["srb"], _two(u_)), g1, prep, u)
    upd = _each(lambda u_, d: _dot(jnp.concatenate([u_, d["v"]], axis=0).T, d["bk"]), u, prep)
    s_new = _each(lambda d, s, up: s * d["pc"] + bd * up, prep, states, upd)
    return list(zip(y, s_new))


def _rwkv_kernel(u_ref, mu_ref, w0_ref, wup_ref, a0_ref, aup_ref, gup_ref, kk_ref, ka_ref, rk_ref,
                 gnw_ref, gnb_ref, tri_ref, gsum_ref, lvl_ref, bd_ref, ya_ref,
                 s_ref, prev_ref, r_s, k_s, v_s, a_s, b_s, lw_s, lc_s, g_s, y_s):
    tb = pl.program_id(1)

    @pl.when(tb == 0)
    def _():
        s_ref[...] = jnp.zeros_like(s_ref)
        prev_ref[...] = jnp.zeros_like(prev_ref)

    u = u_ref[0]
    nt = u.shape[0]
    dim = r_s.shape[1]
    rowid = lax.broadcasted_iota(I32, u.shape, 0)
    shifted = jnp.where(rowid == 0, prev_ref[...], pltpu.roll(u, 1, axis=0))
    prev_ref[...] = u[nt - 1:nt, :]
    um = u + (shifted - u) * mu_ref[...]

    r = um[:, 0:dim]
    k = um[:, dim:2 * dim]
    v = um[:, 2 * dim:3 * dim]
    xwa = um[:, 3 * dim:3 * dim + LANES]
    xg = um[:, 3 * dim + LANES:3 * dim + 2 * LANES]

    wpre = w0_ref[...] + _dot(jnp.tanh(xwa), wup_ref[...])
    lw = -RW_DECAY_SCALE * _sigmoid(wpre)
    a = _sigmoid(a0_ref[...] + _dot(xwa, aup_ref[...]))
    g_s[...] = _dot(_sigmoid(xg), gup_ref[...])
    kk = k * kk_ref[...]
    ss = _dot(kk * kk, gsum_ref[...])
    kk = kk * lax.rsqrt(jnp.maximum(ss, 1e-24))
    r_s[...] = r
    k_s[...] = k * (1.0 + (a - 1.0) * ka_ref[...])
    v_s[...] = v
    a_s[...] = -kk
    b_s[...] = kk * a
    lw_s[...] = lw
    lc_s[...] = _cumsum_chunks(lw, tri_ref[...])
    bd = bd_ref[...]

    npair = dim // LANES
    lanes = [slice(p * LANES, (p + 1) * LANES) for p in range(npair)]

    def group_body(gi, carry):
        rows = [pl.ds(pl.multiple_of((gi * RW_WIDE + ci) * CHUNK, CHUNK), CHUNK) for ci in range(RW_WIDE)]
        ins = [(r_s[rw, ls], k_s[rw, ls], v_s[rw, ls], a_s[rw, ls], b_s[rw, ls], lw_s[rw, ls], lc_s[rw, ls])
               for rw in rows for ls in lanes]
        prep = _rwkv_chunk_prepare(ins, lvl_ref)
        states = [s_ref[p] for p in range(npair)]
        for ci, rw in enumerate(rows):
            outs = _rwkv_chunk_apply(prep[ci * npair:(ci + 1) * npair], states, bd)
            states = [s_new for _, s_new in outs]
            for (y, _), ls in zip(outs, lanes):
                y_s[rw, ls] = y
        for p in range(npair):
            s_ref[p] = states[p]
        return carry

    lax.fori_loop(0, nt // (CHUNK * RW_WIDE), group_body, 0)

    y = y_s[...]
    gsum = gsum_ref[...]
    inv_n = 1.0 / RW_HEAD
    m = _dot_hl(y, gsum) * inv_n
    d = y - m
    var = _dot(d * d, gsum) * inv_n
    yn = d * lax.rsqrt(var + RW_GN_EPS) * gnw_ref[...] + gnb_ref[...]
    bonus = _dot(r_s[...] * k_s[...] * rk_ref[...], gsum) * v_s[...]
    ya_ref[0] = ((yn + bonus) * g_s[...]).astype(BF16)


def _rwkv_branch(proj_rw, mu, w0, wup, a0, aup, gup, k_k, k_a, r_k, gn_w, gn_b):
    b, t, cols = proj_rw.shape
    dim = w0.shape[1]
    tb = TB_REC
    ii = jnp.arange(tb)
    tri = ((ii[:, None] // CHUNK == ii[None, :] // CHUNK) & (ii[:, None] >= ii[None, :])).astype(BF16)
    jj = jnp.arange(dim)
    gsum = (jj[:, None] // RW_HEAD == jj[None, :] // RW_HEAD).astype(BF16)
    rr = jnp.arange(LANES)[:, None]
    cc = jnp.arange(LANES)[None, :]
    lvls = []
    s = 1
    while s < CHUNK:
        lvls.append(((rr // (2 * s) == cc // (2 * s)) & ((rr // s) % 2 == 1) & ((cc // s) % 2 == 0)).astype(F32))
        s *= 2
    lvl = jnp.stack(lvls)
    bd = (rr // RW_HEAD == cc // RW_HEAD).astype(F32)
    zpad = lambda rows: jnp.zeros((rows, dim), F32)
    wup_p = jnp.concatenate([wup, zpad(LANES - wup.shape[0])], axis=0).astype(BF16)
    aup_p = jnp.concatenate([zpad(LANES - aup.shape[0]), aup], axis=0).astype(BF16)
    row2 = lambda a_: a_.reshape(1, -1)
    const = lambda a_: pl.BlockSpec(a_.shape, lambda bi, ti: (0,) * a_.ndim)
    args = [row2(mu), row2(w0), wup_p, row2(a0), aup_p, gup.astype(BF16), row2(k_k), row2(k_a), row2(r_k),
            row2(gn_w), row2(gn_b), tri, gsum, lvl, bd]
    sc = lambda: pltpu.VMEM((tb, dim), F32)
    return pl.pallas_call(
        _rwkv_kernel,
        grid=(b, t // tb),
        in_specs=[pl.BlockSpec((1, tb, cols), lambda bi, ti: (bi, ti, 0))] + [const(a_) for a_ in args],
        out_specs=pl.BlockSpec((1, tb, dim), lambda bi, ti: (bi, ti, 0)),
        out_shape=jax.ShapeDtypeStruct((b, t, dim), BF16),
        scratch_shapes=[pltpu.VMEM((dim // LANES, LANES, LANES), F32), pltpu.VMEM((1, cols), F32)]
                       + [sc() for _ in range(9)],
        compiler_params=_params(("arbitrary", "arbitrary")),
    )(proj_rw, *args)


def _hgrn_chunk_prepare(ins):
    c = CHUNK
    subs = [(SUB * i, SUB * (i + 1)) for i in range(c // SUB)]

    def scores(q, k, lf, bc, lo, hi):
        m = bc[lo:lo + 1] - lf[lo:lo + 1]
        att = _dot(q[lo:hi] * jnp.exp(bc[lo:hi] - m), k[:hi] * jnp.exp(m - bc[:hi]), NT)
        tt = lax.broadcasted_iota(I32, (SUB, hi), 0) + lo
        s_ = lax.broadcasted_iota(I32, (SUB, hi), 1)
        return jnp.where(s_ <= tt, att, 0.0)

    att = [[scores(q, k, lf, bc, lo, hi) for lo, hi in subs] for q, k, v, lf, bc in ins]
    upd = [_dot(v.T, k * jnp.exp(bc[c - 1:c] - bc)) for q, k, v, lf, bc in ins]
    intra = [[_dot(a, x[2][:hi]) for a, (lo, hi) in zip(arow, subs)] for arow, x in zip(att, ins)]
    return [dict(intra=jnp.concatenate(rows, axis=0), qe=x[0] * jnp.exp(x[4]), upd=up, pc=jnp.exp(x[4][c - 1:c]))
            for rows, x, up in zip(intra, ins, upd)]


def _hgrn_chunk_apply(prep, states):
    o = _each(lambda d, st: d["intra"] + _dot(d["qe"], st, NT), prep, states)
    st_new = _each(lambda d, st: st * d["pc"] + d["upd"], prep, states)
    return list(zip(o, st_new))


def _hgrn_kernel(layer, u_ref, lbl_ref, nw_ref, tri_ref, gsum_ref, yb_ref,
                 st_ref, q_s, k_s, v_s, lf_s, bc_s, o_s):
    tb = pl.program_id(1)

    @pl.when(tb == 0)
    def _():
        st_ref[...] = jnp.zeros_like(st_ref)

    u = u_ref[0]
    nt = u.shape[0]
    dim = q_s.shape[1]
    lbl = lbl_ref[...]
    e = jnp.exp(lbl - jnp.max(lbl, axis=0, keepdims=True))
    lb = jnp.sum(e[0:layer + 1], axis=0, keepdims=True) / jnp.sum(e, axis=0, keepdims=True)
    zf = u[:, dim:2 * dim]
    sig = _sigmoid(zf)
    f = lb + (1.0 - lb) * sig
    qin = u[:, 0:dim]
    q_s[...] = qin * _sigmoid(qin)
    k_s[...] = (1.0 - lb) * (1.0 - sig)
    v_s[...] = u[:, 2 * dim:3 * dim]
    lf = jnp.log(f)
    lf_s[...] = lf
    bc_s[...] = _cumsum_chunks(lf, tri_ref[...])

    nhead = dim // LANES
    lanes = [slice(h * LANES, (h + 1) * LANES) for h in range(nhead)]

    def group_body(gi, carry):
        rows = [pl.ds(pl.multiple_of((gi * HG_WIDE + ci) * CHUNK, CHUNK), CHUNK) for ci in range(HG_WIDE)]
        ins = [(q_s[rw, ls], k_s[rw, ls], v_s[rw, ls], lf_s[rw, ls], bc_s[rw, ls]) for rw in rows for ls in lanes]
        prep = _hgrn_chunk_prepare(ins)
        states = [st_ref[h] for h in range(nhead)]
        for ci, rw in enumerate(rows):
            outs = _hgrn_chunk_apply(prep[ci * nhead:(ci + 1) * nhead], states)
            states = [st_new for _, st_new in outs]
            for (o, _), ls in zip(outs, lanes):
                o_s[rw, ls] = o
        for h in range(nhead):
            st_ref[h] = states[h]
        return carry

    lax.fori_loop(0, nt // (CHUNK * HG_WIDE), group_body, 0)

    o = o_s[...]
    ms = _dot(o * o, gsum_ref[...]) * (1.0 / LANES)
    og = u[:, 3 * dim:4 * dim]
    yb_ref[0] = (o * lax.rsqrt(ms + RMS_EPS) * nw_ref[...] * _sigmoid(og)).astype(BF16)


def _hgrn_branch(proj_hg, lb_logits, norm_w, layer):
    b, t, cols = proj_hg.shape
    dim = cols // 4
    tb = TB_REC
    ii = jnp.arange(tb)
    tri = ((ii[:, None] // CHUNK == ii[None, :] // CHUNK) & (ii[:, None] >= ii[None, :])).astype(BF16)
    jj = jnp.arange(dim)
    gsum = (jj[:, None] // LANES == jj[None, :] // LANES).astype(BF16)
    const = lambda a_: pl.BlockSpec(a_.shape, lambda bi, ti: (0,) * a_.ndim)
    args = [lb_logits, norm_w.reshape(1, -1), tri, gsum]
    sc = lambda: pltpu.VMEM((tb, dim), F32)
    return pl.pallas_call(
        functools.partial(_hgrn_kernel, layer),
        grid=(b, t // tb),
        in_specs=[pl.BlockSpec((1, tb, cols), lambda bi, ti: (bi, ti, 0))] + [const(a_) for a_ in args],
        out_specs=pl.BlockSpec((1, tb, dim), lambda bi, ti: (bi, ti, 0)),
        out_shape=jax.ShapeDtypeStruct((b, t, dim), BF16),
        scratch_shapes=[pltpu.VMEM((dim // LANES, LANES, LANES), F32)] + [sc() for _ in range(6)],
        compiler_params=_params(("arbitrary", "arbitrary")),
    )(proj_hg, *args)


def _merge_kernel(alpha, x_ref, ya_ref, yb_ref, pg_ref, wa_ref, wb_ref, wo_ref, g_ref, b_ref, wr_ref,
                  x1_ref, lt_ref):
    d = x_ref.shape[1]
    gates = _sigmoid(pg_ref[...])
    merged = gates[:, :d] * _dg(ya_ref[...], wa_ref[...]) + gates[:, d:] * _dg(yb_ref[...], wb_ref[...])
    h = alpha * x_ref[...] + _dot(merged, wo_ref[...])
    x1 = _layer_norm(h, g_ref[...], b_ref[...])
    x1_ref[...] = x1
    lt_ref[...] = _dot3(wr_ref[...], x1, NT)


def _merge(xf, ya, yb, pgate, wa, wb, wo, g, bta, wr, alpha):
    n, d = xf.shape
    tm = TM_MERGE
    tile = lambda a_: pl.BlockSpec((tm, a_.shape[1]), lambda i: (i, 0))
    const = lambda a_: pl.BlockSpec(a_.shape, lambda i: (0, 0))
    return pl.pallas_call(
        functools.partial(_merge_kernel, alpha),
        grid=(n // tm,),
        in_specs=[tile(xf), tile(ya), tile(yb), tile(pgate), const(wa), const(wb), const(wo), const(g),
                  const(bta), const(wr)],
        out_specs=[pl.BlockSpec((tm, d), lambda i: (i, 0)), pl.BlockSpec((LANES, tm), lambda i: (0, i))],
        out_shape=[jax.ShapeDtypeStruct((n, d), F32), jax.ShapeDtypeStruct((LANES, n), F32)],
        compiler_params=_params(("parallel",)),
    )(xf, ya, yb, pgate, wa, wb, wo, g, bta, wr)


ROUTER_EXPERT_ROW = 8


def _to_lanes(col, nl):
    ne = col.shape[0]
    diag = lax.broadcasted_iota(I32, (ne, nl), 0) == lax.broadcasted_iota(I32, (ne, nl), 1)
    return jnp.sum(jnp.where(diag, col, 0.0), axis=0, keepdims=True)


def _route_kernel(lt_ref, bias_ref, upper_ref, lower_ref, route_ref, seg_ref, cnt_ref, carry_ref):
    @pl.when(pl.program_id(0) == 0)
    def _():
        carry_ref[...] = jnp.zeros_like(carry_ref)

    ne = N_GROUPS * EXPERTS_PER_GROUP
    lt = lt_ref[...] + bias_ref[...]
    nb = lt.shape[1]
    neg = -jnp.inf
    lg = lt[0:8]
    rg = lax.broadcasted_iota(I32, (8, nb), 0).astype(F32)
    lg = jnp.where(rg < N_GROUPS, lg, neg)
    mg = jnp.max(lg, axis=0, keepdims=True)
    gidx = jnp.min(jnp.where(lg == mg, rg, 1e9), axis=0, keepdims=True)
    pg_sel = 1.0 / jnp.sum(jnp.exp(lg - mg), axis=0, keepdims=True)

    le = lt[ROUTER_EXPERT_ROW:ROUTER_EXPERT_ROW + ne]
    re = lax.broadcasted_iota(I32, (ne, nb), 0).astype(F32)
    in_group = jnp.floor(re * (1.0 / EXPERTS_PER_GROUP)) == gidx
    l1 = jnp.where(in_group, le, neg)
    m1 = jnp.max(l1, axis=0, keepdims=True)
    i1 = jnp.min(jnp.where(l1 == m1, re, 1e9), axis=0, keepdims=True)
    l2 = jnp.where(re == i1, neg, l1)
    m2 = jnp.max(l2, axis=0, keepdims=True)
    i2 = jnp.min(jnp.where(l2 == m2, re, 1e9), axis=0, keepdims=True)
    e2 = jnp.exp(m2 - m1)
    w1 = pg_sel / (1.0 + e2)
    w2 = pg_sel * e2 / (1.0 + e2)

    sel1 = re == i1
    sel2 = re == i2
    onehot = jnp.where(sel1 | sel2, 1.0, 0.0)
    before = _dg(onehot.astype(BF16), upper_ref[...])
    cnt_t = jnp.sum(onehot, axis=1, keepdims=True)
    seg = jnp.floor((cnt_t + (SEG_ALIGN - 1)) * (1.0 / SEG_ALIGN)) * SEG_ALIGN
    lstart = _dg(lower_ref[...], jnp.broadcast_to(seg, (ne, LANES)).astype(BF16))[:, 0:1]
    tot = lstart + before
    lpos1 = jnp.sum(jnp.where(sel1, tot, 0.0), axis=0, keepdims=True)
    lpos2 = jnp.sum(jnp.where(sel2, tot, 0.0), axis=0, keepdims=True)
    grel = carry_ref[...]
    carry = grel + seg
    carry_ref[...] = carry
    cnt_ref[...] = jnp.broadcast_to(carry, cnt_ref.shape)
    zero = jnp.zeros_like(w1)
    route_ref[...] = jnp.concatenate([i1, i2, lpos1, lpos2, w1, w2, zero, zero], axis=0)
    nl = seg_ref.shape[2]
    zl = jnp.zeros((1, nl), F32)
    ltot = jnp.broadcast_to(jnp.sum(seg, axis=0, keepdims=True), (1, nl))
    seg_ref[0] = jnp.concatenate([_to_lanes(seg, nl), _to_lanes(lstart, nl), _to_lanes(grel, nl), ltot,
                                  zl, zl, zl, zl], axis=0)


def _route(lt, bias_col):
    n = lt.shape[1]
    tb = TB_ROUTE
    ne = N_GROUPS * EXPERTS_PER_GROUP
    ii = jnp.arange(tb)
    upper = (ii[:, None] < ii[None, :]).astype(BF16)
    ee = jnp.arange(ne)
    lower = (ee[:, None] > ee[None, :]).astype(BF16)
    return pl.pallas_call(
        _route_kernel,
        grid=(n // tb,),
        in_specs=[pl.BlockSpec((LANES, tb), lambda i: (0, i)),
                  pl.BlockSpec((LANES, 1), lambda i: (0, 0)),
                  pl.BlockSpec((tb, tb), lambda i: (0, 0)),
                  pl.BlockSpec((ne, ne), lambda i: (0, 0))],
        out_specs=[pl.BlockSpec((8, tb), lambda i: (0, i)), pl.BlockSpec((1, 8, LANES), lambda i: (i, 0, 0)),
                   pl.BlockSpec((ne, LANES), lambda i: (0, 0))],
        out_shape=[jax.ShapeDtypeStruct((8, n), F32), jax.ShapeDtypeStruct((n // tb, 8, LANES), F32),
                   jax.ShapeDtypeStruct((ne, LANES), F32)],
        scratch_shapes=[pltpu.VMEM((ne, 1), F32)],
        compiler_params=_params(("arbitrary",)),
    )(lt, bias_col, upper, lower)


TAB_LANES = LANES


def _finalize_kernel(tm, seg_ref, cnt_ref, lower_ref, segtab_ref, tab_ref):
    ne = cnt_ref.shape[0]
    cnt = cnt_ref[...]
    nb = jnp.floor((cnt + (tm - 1)) * (1.0 / tm))
    bstart = _dg(lower_ref[...], nb.astype(BF16))
    bend = bstart + nb
    pad_start = bstart[:, 0:1] * tm
    seg = seg_ref[0]
    gstart = seg[2:3] + _to_lanes(pad_start, seg.shape[1])
    segtab_ref[0] = jnp.concatenate([seg[0:2], gstart, seg[3:8]], axis=0).astype(I32)

    nl = tab_ref.shape[1]
    n_used = jnp.max(bend[:, 0:1], axis=0, keepdims=True)
    pad_lo = _to_lanes(pad_start + cnt[:, 0:1], nl)
    pad_hi = _to_lanes(bend[:, 0:1] * tm, nl)
    zero = jnp.zeros((1, nl), F32)
    tab_ref[...] = jnp.concatenate([_to_lanes(bstart[:, 0:1], nl), jnp.broadcast_to(n_used, (1, nl)), pad_lo,
                                    pad_hi, _to_lanes(nb[:, 0:1], nl), zero, zero, zero], axis=0).astype(I32)


def _finalize(seg, cnt, tm):
    ntile = seg.shape[0]
    ne = cnt.shape[0]
    ii = jnp.arange(ne)
    lower = (ii[:, None] > ii[None, :]).astype(BF16)
    return pl.pallas_call(
        functools.partial(_finalize_kernel, tm),
        grid=(ntile,),
        in_specs=[pl.BlockSpec((1, 8, LANES), lambda i: (i, 0, 0)), pl.BlockSpec(cnt.shape, lambda i: (0, 0)),
                  pl.BlockSpec((ne, ne), lambda i: (0, 0))],
        out_specs=[pl.BlockSpec((1, 8, LANES), lambda i: (i, 0, 0)), pl.BlockSpec((8, TAB_LANES), lambda i: (0, 0))],
        out_shape=[jax.ShapeDtypeStruct((ntile, 8, LANES), I32), jax.ShapeDtypeStruct((8, TAB_LANES), I32)],
        compiler_params=_params(("arbitrary",)),
    )(seg, cnt, lower)


def _for_each_piece(length, max_len, fn):
    size = SEG_ALIGN
    sizes = []
    while size <= max_len:
        sizes.append(size)
        size *= 2
    for size in reversed(sizes):
        @pl.when(jnp.bitwise_and(length, size) != 0)
        def _(size=size):
            fn(pl.multiple_of(jnp.bitwise_and(length, -2 * size), SEG_ALIGN), size)


def _sorted_rows(td):
    return 2 * td + N_GROUPS * EXPERTS_PER_GROUP * SEG_ALIGN


def _dispatch_kernel(tm, seglen_ref, lstart_ref, gstart_ref, ltot_ref, plo_ref, phi_ref, nu_ref,
                     x_ref, route_ref, xbuf_ref, sorted_ref, zblk, sem, zsem):
    i = pl.program_id(0)
    nsteps = pl.num_programs(0)
    td = x_ref.shape[0]
    ne = plo_ref.shape[0]
    nblk = xbuf_ref.shape[0] // tm
    nrow = sorted_ref.shape[1]
    buf = i % 2

    lpos = route_ref[2:4, :]
    r = lax.broadcasted_iota(I32, (nrow, td), 0).astype(F32)
    onehot = jnp.where((r == lpos[0:1]) | (r == lpos[1:2]), 1.0, 0.0).astype(BF16)
    sorted_ref[buf] = _dg(onehot, x_ref[...].astype(BF16)).astype(BF16)

    def wait_tile(step, b):
        _for_each_piece(ltot_ref[step], nrow, lambda off, size: pltpu.make_async_copy(
            sorted_ref.at[b, pl.ds(0, size), :], xbuf_ref.at[pl.ds(0, size), :], sem.at[b]).wait())

    for e in range(ne):
        idx = i * ne + e
        ls = pl.multiple_of(lstart_ref[idx], SEG_ALIGN)
        gs = pl.multiple_of(gstart_ref[idx], SEG_ALIGN)
        _for_each_piece(seglen_ref[idx], td, lambda off, size: pltpu.make_async_copy(
            sorted_ref.at[buf, pl.ds(ls + off, size), :], xbuf_ref.at[pl.ds(gs + off, size), :],
            sem.at[buf]).start())

    @pl.when(i > 0)
    def _():
        wait_tile(i - 1, 1 - buf)

    @pl.when(i == nsteps - 1)
    def _():
        wait_tile(i, buf)

    def pad_fill(fn):
        for e in range(ne):
            lo = pl.multiple_of(plo_ref[e], SEG_ALIGN)
            _for_each_piece(phi_ref[e] - lo, tm // 2, lambda off, size: fn(pltpu.make_async_copy(
                zblk.at[pl.ds(0, size), :], xbuf_ref.at[pl.ds(lo + off, size), :], zsem)))

        def per_blk(b, carry):
            fn(pltpu.make_async_copy(zblk, xbuf_ref.at[pl.ds(pl.multiple_of(b * tm, tm), tm), :], zsem))
            return carry
        lax.fori_loop(nu_ref[0], nblk, per_blk, 0)

    @pl.when(i == 0)
    def _():
        zblk[...] = jnp.zeros_like(zblk)
        pad_fill(lambda cp: cp.start())
        pad_fill(lambda cp: cp.wait())


def _dispatch(seglen, lstart, gstart, ltot, pad_lo, pad_hi, n_used, x1, route, rows, tm):
    n, d = x1.shape
    td = TD_DISPATCH
    return pl.pallas_call(
        functools.partial(_dispatch_kernel, tm),
        grid_spec=pltpu.PrefetchScalarGridSpec(
            num_scalar_prefetch=7,
            grid=(n // td,),
            in_specs=[pl.BlockSpec((td, d), lambda i, *_: (i, 0)), pl.BlockSpec((8, td), lambda i, *_: (0, i))],
            out_specs=pl.BlockSpec(memory_space=pl.ANY),
            scratch_shapes=[pltpu.VMEM((2, _sorted_rows(td), d), BF16), pltpu.VMEM((tm, d), BF16),
                            pltpu.SemaphoreType.DMA((2,)), pltpu.SemaphoreType.DMA(())],
        ),
        out_shape=jax.ShapeDtypeStruct((rows, d), BF16),
        compiler_params=_params(("arbitrary",)),
    )(seglen, lstart, gstart, ltot, pad_lo, pad_hi, n_used, x1, route)


def _expert_kernel(tm, bstart_ref, nb_ref, nu_ref, w1_ref, w3_ref, w2_ref, xbuf_ref, ybuf_ref,
                   w1b, w3b, w2b, xb, yb, semx, semy):
    e = pl.program_id(0)
    nb = nb_ref[e]
    b0 = bstart_ref[e]
    nblk = ybuf_ref.shape[0] // tm
    rows = lambda blk: pl.ds(pl.multiple_of(blk * tm, tm), tm)
    x_copy = lambda j, slot: pltpu.make_async_copy(xbuf_ref.at[rows(b0 + j), :], xb.at[slot], semx.at[slot])
    y_copy = lambda blk, slot: pltpu.make_async_copy(yb.at[slot], ybuf_ref.at[rows(blk), :], semy.at[slot])

    @pl.when(nb > 0)
    def _():
        x_copy(0, 0).start()

    w1b[...] = w1_ref[0].astype(BF16)
    w3b[...] = w3_ref[0].astype(BF16)
    w2b[...] = w2_ref[0].astype(BF16)

    def body(j, carry):
        slot = j % 2
        x_copy(j, slot).wait()

        @pl.when(j + 1 < nb)
        def _():
            x_copy(j + 1, 1 - slot).start()

        @pl.when(j >= 2)
        def _():
            y_copy(b0 + j - 2, slot).wait()

        x = xb[slot]
        h1 = _dg(x, w1b[...])
        h3 = _dg(x, w3b[...])
        h = (h1 * _sigmoid(h1)) * h3
        yb[slot] = _dot(h, w2b[...]).astype(BF16)
        y_copy(b0 + j, slot).start()
        return carry

    lax.fori_loop(0, nb, body, 0)

    @pl.when(nb >= 2)
    def _():
        y_copy(b0 + nb - 2, nb % 2).wait()

    @pl.when(nb >= 1)
    def _():
        y_copy(b0 + nb - 1, (nb - 1) % 2).wait()

    @pl.when(e == pl.num_programs(0) - 1)
    def _():
        yb[0] = jnp.zeros(yb.shape[1:], yb.dtype)

        def fill(fn):
            def per_blk(blk, carry):
                fn(y_copy(blk, 0))
                return carry
            lax.fori_loop(nu_ref[0], nblk, per_blk, 0)

        fill(lambda cp: cp.start())
        fill(lambda cp: cp.wait())


def _experts(bstart, nb, n_used, xbuf, w1, w3, w2):
    rows, d = xbuf.shape
    ne, _, de = w1.shape
    tm = TM_EXPERT
    return pl.pallas_call(
        functools.partial(_expert_kernel, tm),
        grid_spec=pltpu.PrefetchScalarGridSpec(
            num_scalar_prefetch=3,
            grid=(ne,),
            in_specs=[pl.BlockSpec((1, d, de), lambda e, *_: (e, 0, 0)),
                      pl.BlockSpec((1, d, de), lambda e, *_: (e, 0, 0)),
                      pl.BlockSpec((1, de, d), lambda e, *_: (e, 0, 0)),
                      pl.BlockSpec(memory_space=pl.ANY)],
            out_specs=pl.BlockSpec(memory_space=pl.ANY),
            scratch_shapes=[pltpu.VMEM((d, de), BF16), pltpu.VMEM((d, de), BF16), pltpu.VMEM((de, d), BF16),
                            pltpu.VMEM((2, tm, d), BF16), pltpu.VMEM((2, tm, d), BF16),
                            pltpu.SemaphoreType.DMA((2,)), pltpu.SemaphoreType.DMA((2,))],
        ),
        out_shape=jax.ShapeDtypeStruct((rows, d), BF16),
        compiler_params=_params(("arbitrary",)),
    )(bstart, nb, n_used, w1, w3, w2, xbuf)


def _combine_kernel(alpha, seglen_ref, lstart_ref, gstart_ref, ltot_ref, x1_ref, rt_ref, p_ref, wpe_ref, wpg_ref,
                    g_ref, b_ref, ybuf_ref, out_ref, sorted_ref, sem):
    i = pl.program_id(0)
    nsteps = pl.num_programs(0)
    tc = x1_ref.shape[0]
    ne = N_GROUPS * EXPERTS_PER_GROUP
    nrow = sorted_ref.shape[1]

    def fetch(step, buf):
        for e in range(ne):
            idx = step * ne + e
            ls = pl.multiple_of(lstart_ref[idx], SEG_ALIGN)
            gs = pl.multiple_of(gstart_ref[idx], SEG_ALIGN)
            _for_each_piece(seglen_ref[idx], tc, lambda off, size: pltpu.make_async_copy(
                ybuf_ref.at[pl.ds(gs + off, size), :], sorted_ref.at[buf, pl.ds(ls + off, size), :],
                sem.at[buf]).start())

    @pl.when(i == 0)
    def _():
        sorted_ref[...] = jnp.zeros_like(sorted_ref)
        fetch(0, 0)

    @pl.when(i + 1 < nsteps)
    def _():
        fetch(i + 1, (i + 1) % 2)

    cur = i % 2
    _for_each_piece(ltot_ref[i], nrow, lambda off, size: pltpu.make_async_copy(
        ybuf_ref.at[pl.ds(0, size), :], sorted_ref.at[cur, pl.ds(0, size), :], sem.at[cur]).wait())

    rt = rt_ref[...]
    r = lax.broadcasted_iota(I32, (tc, nrow), 1).astype(F32)
    unsort = jnp.where(r == rt[:, 0:1], rt[:, 2:3], 0.0) + jnp.where(r == rt[:, 1:2], rt[:, 3:4], 0.0)
    ffn = _dg(unsort.astype(BF16), sorted_ref[cur])
    x2 = _layer_norm(alpha * x1_ref[...] + ffn, g_ref[...], b_ref[...])
    gate = _sigmoid(_dot(x2, wpg_ref[...]))
    out_ref[...] = x2 + gate * _dot(p_ref[...], wpe_ref[...])


def _combine(seglen, lstart, gstart, ltot, x1, route_t, pf, wpe, wpg, g, bta, ybuf, alpha):
    n, d = x1.shape
    tc = TD_DISPATCH
    tile = lambda a_: pl.BlockSpec((tc, a_.shape[1]), lambda i, *_: (i, 0))
    const = lambda a_: pl.BlockSpec(a_.shape, lambda i, *_: (0, 0))
    return pl.pallas_call(
        functools.partial(_combine_kernel, alpha),
        grid_spec=pltpu.PrefetchScalarGridSpec(
            num_scalar_prefetch=4,
            grid=(n // tc,),
            in_specs=[tile(x1), tile(route_t), tile(pf), const(wpe), const(wpg), const(g), const(bta),
                      pl.BlockSpec(memory_space=pl.ANY)],
            out_specs=pl.BlockSpec((tc, d), lambda i, *_: (i, 0)),
            scratch_shapes=[pltpu.VMEM((2, _sorted_rows(tc), d), BF16), pltpu.SemaphoreType.DMA((2,))],
        ),
        out_shape=jax.ShapeDtypeStruct((n, d), F32),
        compiler_params=_params(("arbitrary",)),
    )(seglen, lstart, gstart, ltot, x1, route_t, pf, wpe, wpg, g, bta, ybuf)


def _layer(x, p_i, w_in, rw_mu, rw_w0, rw_w_up, rw_a0, rw_a_up, rw_g_up, rw_k_k, rw_k_a, rw_r_k, rw_gn_w,
           rw_gn_b, w_a_out, hg_lb_logits, hg_norm_w, w_b_out, w_o, ln1_g, ln1_b, router_g_w, router_g_b,
           router_e_w, router_e_b, w1, w3, w2, ln2_g, ln2_b, w_pe, w_pg, alpha, layer):
    b, t, d = x.shape
    n = b * t
    rw_dim = rw_w0.shape[0]
    rw_cols = 3 * rw_dim + RW_DECAY_LORA + RW_A_LORA + RW_GATE_LORA
    hg_cols = 4 * hg_norm_w.shape[0]
    ne = N_GROUPS * EXPERTS_PER_GROUP
    row2 = lambda a_: a_.reshape(1, -1)
    xf = x.reshape(n, d)

    wb = w_in.astype(BF16)
    proj_rw, proj_hg, proj_gt = _project(xf, wb[:, :rw_cols], wb[:, rw_cols:rw_cols + hg_cols],
                                         wb[:, rw_cols + hg_cols:])
    ya = _rwkv_branch(proj_rw.reshape(b, t, rw_cols), rw_mu, row2(rw_w0), rw_w_up, rw_a0, rw_a_up, rw_g_up,
                      rw_k_k, rw_k_a, rw_r_k, rw_gn_w, rw_gn_b)
    yb = _hgrn_branch(proj_hg.reshape(b, t, hg_cols), hg_lb_logits, hg_norm_w, layer)

    wr = jnp.zeros((LANES, d), F32)
    wr = wr.at[:N_GROUPS].set(router_g_w.T).at[ROUTER_EXPERT_ROW:ROUTER_EXPERT_ROW + ne].set(router_e_w.T)
    bias = jnp.zeros((LANES,), F32)
    bias = bias.at[:N_GROUPS].set(router_g_b).at[ROUTER_EXPERT_ROW:ROUTER_EXPERT_ROW + ne].set(router_e_b)
    x1, lt = _merge(xf, ya.reshape(n, -1), yb.reshape(n, -1), proj_gt, w_a_out.astype(BF16),
                    w_b_out.astype(BF16), w_o.astype(BF16), row2(ln1_g), row2(ln1_b), wr, alpha)
    route, seg, cnt = _route(lt, bias.reshape(LANES, 1))

    tm = TM_EXPERT
    ntile = n // TD_DISPATCH
    nblk = -(-(2 * n + (SEG_ALIGN - 1) * ne * ntile) // tm) + ne
    assert TB_ROUTE == TD_DISPATCH
    segtab, tab = _finalize(seg, cnt, tm)
    per_seg = lambda row: segtab[:, row, :ne].reshape(-1)
    seglen, lstart, gstart, ltot = per_seg(0), per_seg(1), per_seg(2), segtab[:, 3, 0]
    n_used = tab[1, :1]

    xbuf = _dispatch(seglen, lstart, gstart, ltot, tab[2, :ne], tab[3, :ne], n_used, x1, route, nblk * tm, tm)
    ybuf = _experts(tab[0, :ne], tab[4, :ne], n_used, xbuf, w1, w3, w2)
    out = _combine(seglen, lstart, gstart, ltot, x1, route[2:6].T, p_i.reshape(n, -1), w_pe.astype(BF16),
                   w_pg.astype(BF16), row2(ln2_g), row2(ln2_b), ybuf, alpha)
    return out.reshape(b, t, d)


def kernel(x, p, w_in, rw_mu, rw_w0, rw_w_up, rw_a0, rw_a_up, rw_g_up, rw_k_k, rw_k_a, rw_r_k, rw_gn_w, rw_gn_b,
           w_a_out, hg_lb_logits, hg_norm_w, w_b_out, w_o, ln1_g, ln1_b, router_g_w, router_g_b, router_e_w,
           router_e_b, w1, w3, w2, ln2_g, ln2_b, w_pe, w_pg):
    depth = w_in.shape[0]
    alpha = (2 * depth) ** 0.25
    for i in range(depth):
        x = _layer(x, p[i], w_in[i], rw_mu[i], rw_w0[i], rw_w_up[i], rw_a0[i], rw_a_up[i], rw_g_up[i], rw_k_k[i],
                   rw_k_a[i], rw_r_k[i].reshape(-1), rw_gn_w[i], rw_gn_b[i], w_a_out[i], hg_lb_logits,
                   hg_norm_w[i], w_b_out[i], w_o[i], ln1_g[i], ln1_b[i],
                   router_g_w[i], router_g_b[i], router_e_w[i], router_e_b[i], w1[i], w3[i], w2[i], ln2_g[i],
                   ln2_b[i], w_pe[i], w_pg[i], alpha, i)
    return x
```

```python
import functools

import jax
import jax.numpy as jnp
from jax import lax
from jax.experimental import pallas as pl
from jax.experimental.pallas import tpu as pltpu

F32 = jnp.float32
BF16 = jnp.bfloat16
I32 = jnp.int32

NN = (((1,), (0,)), ((), ()))
NT = (((1,), (1,)), ((), ()))

RW_HEAD = 64
RW_DECAY_LORA = 64
RW_A_LORA = 64
RW_GATE_LORA = 128
RW_GN_EPS = 64e-5
RW_DECAY_SCALE = 0.6065306597126334
HG_HEADS = 4
N_GROUPS = 4
EXPERTS_PER_GROUP = 8
LN_EPS = 1e-5
RMS_EPS = 1e-6

CHUNK = 64
SUB = 16
LANES = 128
VMEM_LIMIT = 56 * 1024 * 1024

TM_PROJ = 256
TB_REC = 256
TM_MERGE = 512
TB_ROUTE = 512
TD_DISPATCH = 512
SEG_ALIGN = 16
TM_EXPERT = 512
ROW_DMA_PRIORITY = 1
RW_WIDE = 4
HG_WIDE = 4


def _dg(a, b, dn=NN):
    return lax.dot_general(a, b, dn, preferred_element_type=F32)


def _dot(a, b, dn=NN):
    return _dg(a.astype(BF16), b.astype(BF16), dn)


def _split(a):
    hi = a.astype(BF16)
    lo = (a - hi.astype(F32)).astype(BF16)
    return hi, lo


def _dot_hl(a, b_exact, dn=NN):
    hi, lo = _split(a)
    return _dg(hi, b_exact, dn) + _dg(lo, b_exact, dn)


def _dot3(a, b, dn=NN):
    ah, al = _split(a)
    bh, bl = _split(b)
    return _dg(ah, bh, dn) + (_dg(ah, bl, dn) + _dg(al, bh, dn))


def _cumsum_chunks(x, tri):
    h = x.astype(BF16)
    r1 = x - h.astype(F32)
    m = r1.astype(BF16)
    l = (r1 - m.astype(F32)).astype(BF16)
    return _dg(tri, h) + (_dg(tri, m) + _dg(tri, l))


def _sigmoid(x):
    return 0.5 * jnp.tanh(0.5 * x) + 0.5


def _layer_norm(h, g, b):
    mu = jnp.mean(h, axis=-1, keepdims=True)
    d = h - mu
    var = jnp.mean(d * d, axis=-1, keepdims=True)
    return d * lax.rsqrt(var + LN_EPS) * g + b


def _params(sem):
    return pltpu.CompilerParams(dimension_semantics=sem, vmem_limit_bytes=VMEM_LIMIT)


def _proj_kernel(x_ref, wr_ref, wh_ref, wg_ref, pr_ref, ph_ref, pg_ref):
    xb = x_ref[...].astype(BF16)
    pr_ref[...] = _dg(xb, wr_ref[...])
    ph_ref[...] = _dg(xb, wh_ref[...])
    pg_ref[...] = _dg(xb, wg_ref[...])


def _project(xf, w_rw, w_hg, w_gt):
    n, d = xf.shape
    tm = TM_PROJ
    full = lambda w: pl.BlockSpec(w.shape, lambda i: (0, 0))
    tile = lambda c: pl.BlockSpec((tm, c), lambda i: (i, 0))
    return pl.pallas_call(
        _proj_kernel,
        grid=(n // tm,),
        in_specs=[tile(d), full(w_rw), full(w_hg), full(w_gt)],
        out_specs=[tile(w_rw.shape[1]), tile(w_hg.shape[1]), tile(w_gt.shape[1])],
        out_shape=[jax.ShapeDtypeStruct((n, w.shape[1]), F32) for w in (w_rw, w_hg, w_gt)],
        compiler_params=_params(("parallel",)),
    )(xf, w_rw, w_hg, w_gt)


def _each(f, *ls):
    return [f(*xs) for xs in zip(*ls)]


def _two(x):
    m1 = lax.broadcasted_iota(I32, x.shape, 1) < RW_HEAD
    return jnp.concatenate([jnp.where(m1, x, 0.0), jnp.where(m1, 0.0, x)], axis=0)


def _rwkv_chunk_prepare(ins, lvl_ref):
    c = CHUNK
    lane = lax.broadcasted_iota(I32, (c, LANES), 1)
    row = lax.broadcasted_iota(I32, (c, LANES), 0)
    scol = jnp.bitwise_and(lane, RW_HEAD - 1)
    strict = row > scol
    incl = row >= scol
    r2 = lax.broadcasted_iota(I32, (LANES, LANES), 0)
    c2 = lax.broadcasted_iota(I32, (LANES, LANES), 1)
    eye = jnp.where(r2 == c2, 1.0, 0.0).astype(F32)

    def prep(r, k, v, av, bv, lw, lc):
        l_end = lc[c - 1:c]
        e_r = jnp.exp(l_end - lc)
        e_n = jnp.exp(-lc)
        return dict(at=av * jnp.exp(lc - lw), rt=r * jnp.exp(lc), bt=bv * e_n, kt=k * e_n,
                    bk=jnp.concatenate([bv * e_r, k * e_r], axis=0), pc=jnp.exp(l_end), v=v)

    q = [prep(*xs) for xs in ins]
    p = [_dot(jnp.concatenate([d["at"], d["rt"]], axis=0),
              jnp.concatenate([_two(d["bt"]), _two(d["kt"])], axis=0), NT) for d in q]
    sab = [jnp.where(strict, x[:c, :LANES], 0.0) for x in p]
    sak = [jnp.where(strict, x[:c, LANES:], 0.0) for x in p]
    srb = [jnp.where(incl, x[c:, :LANES], 0.0) for x in p]
    srk = [jnp.where(incl, x[c:, LANES:], 0.0) for x in p]
    sv = _each(lambda ak, rk, d: _dot(jnp.concatenate([ak, rk], axis=0), _two(d["v"])), sak, srk, q)

    a_bd = [_two(x) for x in sab]
    t = [eye + a * lvl_ref[0] for a in a_bd]
    for lv in range(1, lvl_ref.shape[0]):
        ta = _each(lambda t_, a: _dot(t_, a * lvl_ref[lv]), t, a_bd)
        t = _each(lambda t_, ta_: t_ + _dot(ta_, t_), t, ta)

    x = _each(lambda t_, d, sv_: _dot(t_, jnp.concatenate([_two(d["at"]), _two(sv_[:c])], axis=1)), t, q, sv)
    return [dict(uk=x_[:c, :LANES] + x_[c:, :LANES],
                 w=x_[:c, LANES:] + x_[c:, LANES:],
                 rt=d["rt"], rkv=sv_[c:], srb=srb_, bk=d["bk"], v=d["v"], pc=d["pc"])
            for x_, d, sv_, srb_ in zip(x, q, sv, srb)]


def _rwkv_chunk_apply(prep, states, bd):
    c = CHUNK
    g1 = _each(lambda d, s: _dot(jnp.concatenate([d["uk"], d["rt"]], axis=0), s, NT), prep, states)
    u = _each(lambda g, d: g[:c] + d["w"], g1, prep)
    y = _each(lambda g, d, u_: g[c:] + d["rkv"] + _dot(d["srb"], _two(u_)), g1, prep, u)
    upd = _each(lambda u_, d: _dot(jnp.concatenate([u_, d["v"]], axis=0).T, d["bk"]), u, prep)
    s_new = _each(lambda d, s, up: s * d["pc"] + bd * up, prep, states, upd)
    return list(zip(y, s_new))


def _rwkv_kernel(u_ref, mu_ref, w0_ref, wup_ref, a0_ref, aup_ref, gup_ref, kk_ref, ka_ref, rk_ref,
                 gnw_ref, gnb_ref, tri_ref, gsum_ref, lvl_ref, bd_ref, ya_ref,
                 s_ref, prev_ref, r_s, k_s, v_s, a_s, b_s, lw_s, lc_s, g_s, y_s):
    tb = pl.program_id(1)

    @pl.when(tb == 0)
    def _():
        s_ref[...] = jnp.zeros_like(s_ref)
        prev_ref[...] = jnp.zeros_like(prev_ref)

    u = u_ref[0]
    nt = u.shape[0]
    dim = r_s.shape[1]
    rowid = lax.broadcasted_iota(I32, u.shape, 0)
    shifted = jnp.where(rowid == 0, prev_ref[...], pltpu.roll(u, 1, axis=0))
    prev_ref[...] = u[nt - 1:nt, :]
    um = u + (shifted - u) * mu_ref[...]

    r = um[:, 0:dim]
    k = um[:, dim:2 * dim]
    v = um[:, 2 * dim:3 * dim]
    xwa = um[:, 3 * dim:3 * dim + LANES]
    xg = um[:, 3 * dim + LANES:3 * dim + 2 * LANES]

    wpre = w0_ref[...] + _dot(jnp.tanh(xwa), wup_ref[...])
    lw = -RW_DECAY_SCALE * _sigmoid(wpre)
    a = _sigmoid(a0_ref[...] + _dot(xwa, aup_ref[...]))
    g_s[...] = _dot(_sigmoid(xg), gup_ref[...])
    kk = k * kk_ref[...]
    ss = _dot(kk * kk, gsum_ref[...])
    kk = kk * lax.rsqrt(jnp.maximum(ss, 1e-24))
    r_s[...] = r
    k_s[...] = k * (1.0 + (a - 1.0) * ka_ref[...])
    v_s[...] = v
    a_s[...] = -kk
    b_s[...] = kk * a
    lw_s[...] = lw
    lc_s[...] = _cumsum_chunks(lw, tri_ref[...])
    bd = bd_ref[...]

    npair = dim // LANES
    lanes = [slice(p * LANES, (p + 1) * LANES) for p in range(npair)]

    def group_body(gi, carry):
        rows = [pl.ds(pl.multiple_of((gi * RW_WIDE + ci) * CHUNK, CHUNK), CHUNK) for ci in range(RW_WIDE)]
        ins = [(r_s[rw, ls], k_s[rw, ls], v_s[rw, ls], a_s[rw, ls], b_s[rw, ls], lw_s[rw, ls], lc_s[rw, ls])
               for rw in rows for ls in lanes]
        prep = _rwkv_chunk_prepare(ins, lvl_ref)
        states = [s_ref[p] for p in range(npair)]
        for ci, rw in enumerate(rows):
            outs = _rwkv_chunk_apply(prep[ci * npair:(ci + 1) * npair], states, bd)
            states = [s_new for _, s_new in outs]
            for (y, _), ls in zip(outs, lanes):
                y_s[rw, ls] = y
        for p in range(npair):
            s_ref[p] = states[p]
        return carry

    lax.fori_loop(0, nt // (CHUNK * RW_WIDE), group_body, 0)

    y = y_s[...]
    gsum = gsum_ref[...]
    inv_n = 1.0 / RW_HEAD
    m = _dot_hl(y, gsum) * inv_n
    d = y - m
    var = _dot(d * d, gsum) * inv_n
    yn = d * lax.rsqrt(var + RW_GN_EPS) * gnw_ref[...] + gnb_ref[...]
    bonus = _dot(r_s[...] * k_s[...] * rk_ref[...], gsum) * v_s[...]
    ya_ref[0] = ((yn + bonus) * g_s[...]).astype(BF16)


def _rwkv_branch(proj_rw, mu, w0, wup, a0, aup, gup, k_k, k_a, r_k, gn_w, gn_b):
    b, t, cols = proj_rw.shape
    dim = w0.shape[1]
    tb = TB_REC
    ii = jnp.arange(tb)
    tri = ((ii[:, None] // CHUNK == ii[None, :] // CHUNK) & (ii[:, None] >= ii[None, :])).astype(BF16)
    jj = jnp.arange(dim)
    gsum = (jj[:, None] // RW_HEAD == jj[None, :] // RW_HEAD).astype(BF16)
    rr = jnp.arange(LANES)[:, None]
    cc = jnp.arange(LANES)[None, :]
    lvls = []
    s = 1
    while s < CHUNK:
        lvls.append(((rr // (2 * s) == cc // (2 * s)) & ((rr // s) % 2 == 1) & ((cc // s) % 2 == 0)).astype(F32))
        s *= 2
    lvl = jnp.stack(lvls)
    bd = (rr // RW_HEAD == cc // RW_HEAD).astype(F32)
    zpad = lambda rows: jnp.zeros((rows, dim), F32)
    wup_p = jnp.concatenate([wup, zpad(LANES - wup.shape[0])], axis=0).astype(BF16)
    aup_p = jnp.concatenate([zpad(LANES - aup.shape[0]), aup], axis=0).astype(BF16)
    row2 = lambda a_: a_.reshape(1, -1)
    const = lambda a_: pl.BlockSpec(a_.shape, lambda bi, ti: (0,) * a_.ndim)
    args = [row2(mu), row2(w0), wup_p, row2(a0), aup_p, gup.astype(BF16), row2(k_k), row2(k_a), row2(r_k),
            row2(gn_w), row2(gn_b), tri, gsum, lvl, bd]
    sc = lambda: pltpu.VMEM((tb, dim), F32)
    return pl.pallas_call(
        _rwkv_kernel,
        grid=(b, t // tb),
        in_specs=[pl.BlockSpec((1, tb, cols), lambda bi, ti: (bi, ti, 0))] + [const(a_) for a_ in args],
        out_specs=pl.BlockSpec((1, tb, dim), lambda bi, ti: (bi, ti, 0)),
        out_shape=jax.ShapeDtypeStruct((b, t, dim), BF16),
        scratch_shapes=[pltpu.VMEM((dim // LANES, LANES, LANES), F32), pltpu.VMEM((1, cols), F32)]
                       + [sc() for _ in range(9)],
        compiler_params=_params(("arbitrary", "arbitrary")),
    )(proj_rw, *args)


def _hgrn_chunk_prepare(ins):
    c = CHUNK
    subs = [(SUB * i, SUB * (i + 1)) for i in range(c // SUB)]

    def scores(q, k, lf, bc, lo, hi):
        m = bc[lo:lo + 1] - lf[lo:lo + 1]
        att = _dot(q[lo:hi] * jnp.exp(bc[lo:hi] - m), k[:hi] * jnp.exp(m - bc[:hi]), NT)
        tt = lax.broadcasted_iota(I32, (SUB, hi), 0) + lo
        s_ = lax.broadcasted_iota(I32, (SUB, hi), 1)
        return jnp.where(s_ <= tt, att, 0.0)

    att = [[scores(q, k, lf, bc, lo, hi) for lo, hi in subs] for q, k, v, lf, bc in ins]
    upd = [_dot(v.T, k * jnp.exp(bc[c - 1:c] - bc)) for q, k, v, lf, bc in ins]
    intra = [[_dot(a, x[2][:hi]) for a, (lo, hi) in zip(arow, subs)] for arow, x in zip(att, ins)]
    return [dict(intra=jnp.concatenate(rows, axis=0), qe=x[0] * jnp.exp(x[4]), upd=up, pc=jnp.exp(x[4][c - 1:c]))
            for rows, x, up in zip(intra, ins, upd)]


def _hgrn_chunk_apply(prep, states):
    o = _each(lambda d, st: d["intra"] + _dot(d["qe"], st, NT), prep, states)
    st_new = _each(lambda d, st: st * d["pc"] + d["upd"], prep, states)
    return list(zip(o, st_new))


def _hgrn_kernel(layer, u_ref, lbl_ref, nw_ref, tri_ref, gsum_ref, yb_ref,
                 st_ref, q_s, k_s, v_s, lf_s, bc_s, o_s):
    tb = pl.program_id(1)

    @pl.when(tb == 0)
    def _():
        st_ref[...] = jnp.zeros_like(st_ref)

    u = u_ref[0]
    nt = u.shape[0]
    dim = q_s.shape[1]
    lbl = lbl_ref[...]
    e = jnp.exp(lbl - jnp.max(lbl, axis=0, keepdims=True))
    lb = jnp.sum(e[0:layer + 1], axis=0, keepdims=True) / jnp.sum(e, axis=0, keepdims=True)
    zf = u[:, dim:2 * dim]
    sig = _sigmoid(zf)
    f = lb + (1.0 - lb) * sig
    qin = u[:, 0:dim]
    q_s[...] = qin * _sigmoid(qin)
    k_s[...] = (1.0 - lb) * (1.0 - sig)
    v_s[...] = u[:, 2 * dim:3 * dim]
    lf = jnp.log(f)
    lf_s[...] = lf
    bc_s[...] = _cumsum_chunks(lf, tri_ref[...])

    nhead = dim // LANES
    lanes = [slice(h * LANES, (h + 1) * LANES) for h in range(nhead)]

    def group_body(gi, carry):
        rows = [pl.ds(pl.multiple_of((gi * HG_WIDE + ci) * CHUNK, CHUNK), CHUNK) for ci in range(HG_WIDE)]
        ins = [(q_s[rw, ls], k_s[rw, ls], v_s[rw, ls], lf_s[rw, ls], bc_s[rw, ls]) for rw in rows for ls in lanes]
        prep = _hgrn_chunk_prepare(ins)
        states = [st_ref[h] for h in range(nhead)]
        for ci, rw in enumerate(rows):
            outs = _hgrn_chunk_apply(prep[ci * nhead:(ci + 1) * nhead], states)
            states = [st_new for _, st_new in outs]
            for (o, _), ls in zip(outs, lanes):
                o_s[rw, ls] = o
        for h in range(nhead):
            st_ref[h] = states[h]
        return carry

    lax.fori_loop(0, nt // (CHUNK * HG_WIDE), group_body, 0)

    o = o_s[...]
    ms = _dot(o * o, gsum_ref[...]) * (1.0 / LANES)
    og = u[:, 3 * dim:4 * dim]
    yb_ref[0] = (o * lax.rsqrt(ms + RMS_EPS) * nw_ref[...] * _sigmoid(og)).astype(BF16)


def _hgrn_branch(proj_hg, lb_logits, norm_w, layer):
    b, t, cols = proj_hg.shape
    dim = cols // 4
    tb = TB_REC
    ii = jnp.arange(tb)
    tri = ((ii[:, None] // CHUNK == ii[None, :] // CHUNK) & (ii[:, None] >= ii[None, :])).astype(BF16)
    jj = jnp.arange(dim)
    gsum = (jj[:, None] // LANES == jj[None, :] // LANES).astype(BF16)
    const = lambda a_: pl.BlockSpec(a_.shape, lambda bi, ti: (0,) * a_.ndim)
    args = [lb_logits, norm_w.reshape(1, -1), tri, gsum]
    sc = lambda: pltpu.VMEM((tb, dim), F32)
    return pl.pallas_call(
        functools.partial(_hgrn_kernel, layer),
        grid=(b, t // tb),
        in_specs=[pl.BlockSpec((1, tb, cols), lambda bi, ti: (bi, ti, 0))] + [const(a_) for a_ in args],
        out_specs=pl.BlockSpec((1, tb, dim), lambda bi, ti: (bi, ti, 0)),
        out_shape=jax.ShapeDtypeStruct((b, t, dim), BF16),
        scratch_shapes=[pltpu.VMEM((dim // LANES, LANES, LANES), F32)] + [sc() for _ in range(6)],
        compiler_params=_params(("arbitrary", "arbitrary")),
    )(proj_hg, *args)


def _merge_kernel(alpha, x_ref, ya_ref, yb_ref, pg_ref, wa_ref, wb_ref, wo_ref, g_ref, b_ref, wr_ref,
                  x1_ref, lt_ref):
    d = x_ref.shape[1]
    gates = _sigmoid(pg_ref[...])
    merged = gates[:, :d] * _dg(ya_ref[...], wa_ref[...]) + gates[:, d:] * _dg(yb_ref[...], wb_ref[...])
    h = alpha * x_ref[...] + _dot(merged, wo_ref[...])
    x1 = _layer_norm(h, g_ref[...], b_ref[...])
    x1_ref[...] = x1
    lt_ref[...] = _dot3(wr_ref[...], x1, NT)


def _merge(xf, ya, yb, pgate, wa, wb, wo, g, bta, wr, alpha):
    n, d = xf.shape
    tm = TM_MERGE
    tile = lambda a_: pl.BlockSpec((tm, a_.shape[1]), lambda i: (i, 0))
    const = lambda a_: pl.BlockSpec(a_.shape, lambda i: (0, 0))
    return pl.pallas_call(
        functools.partial(_merge_kernel, alpha),
        grid=(n // tm,),
        in_specs=[tile(xf), tile(ya), tile(yb), tile(pgate), const(wa), const(wb), const(wo), const(g),
                  const(bta), const(wr)],
        out_specs=[pl.BlockSpec((tm, d), lambda i: (i, 0)), pl.BlockSpec((LANES, tm), lambda i: (0, i))],
        out_shape=[jax.ShapeDtypeStruct((n, d), F32), jax.ShapeDtypeStruct((LANES, n), F32)],
        compiler_params=_params(("parallel",)),
    )(xf, ya, yb, pgate, wa, wb, wo, g, bta, wr)


ROUTER_EXPERT_ROW = 8


def _to_lanes(col, nl):
    ne = col.shape[0]
    diag = lax.broadcasted_iota(I32, (ne, nl), 0) == lax.broadcasted_iota(I32, (ne, nl), 1)
    return jnp.sum(jnp.where(diag, col, 0.0), axis=0, keepdims=True)


def _route_kernel(lt_ref, bias_ref, upper_ref, lower_ref, route_ref, seg_ref, cnt_ref, carry_ref):
    @pl.when(pl.program_id(0) == 0)
    def _():
        carry_ref[...] = jnp.zeros_like(carry_ref)

    ne = N_GROUPS * EXPERTS_PER_GROUP
    lt = lt_ref[...] + bias_ref[...]
    nb = lt.shape[1]
    neg = -jnp.inf
    lg = lt[0:8]
    rg = lax.broadcasted_iota(I32, (8, nb), 0).astype(F32)
    lg = jnp.where(rg < N_GROUPS, lg, neg)
    mg = jnp.max(lg, axis=0, keepdims=True)
    gidx = jnp.min(jnp.where(lg == mg, rg, 1e9), axis=0, keepdims=True)
    pg_sel = 1.0 / jnp.sum(jnp.exp(lg - mg), axis=0, keepdims=True)

    le = lt[ROUTER_EXPERT_ROW:ROUTER_EXPERT_ROW + ne]
    re = lax.broadcasted_iota(I32, (ne, nb), 0).astype(F32)
    in_group = jnp.floor(re * (1.0 / EXPERTS_PER_GROUP)) == gidx
    l1 = jnp.where(in_group, le, neg)
    m1 = jnp.max(l1, axis=0, keepdims=True)
    i1 = jnp.min(jnp.where(l1 == m1, re, 1e9), axis=0, keepdims=True)
    l2 = jnp.where(re == i1, neg, l1)
    m2 = jnp.max(l2, axis=0, keepdims=True)
    i2 = jnp.min(jnp.where(l2 == m2, re, 1e9), axis=0, keepdims=True)
    e2 = jnp.exp(m2 - m1)
    w1 = pg_sel / (1.0 + e2)
    w2 = pg_sel * e2 / (1.0 + e2)

    sel1 = re == i1
    sel2 = re == i2
    onehot = jnp.where(sel1 | sel2, 1.0, 0.0)
    before = _dg(onehot.astype(BF16), upper_ref[...])
    cnt_t = jnp.sum(onehot, axis=1, keepdims=True)
    seg = jnp.floor((cnt_t + (SEG_ALIGN - 1)) * (1.0 / SEG_ALIGN)) * SEG_ALIGN
    lstart = _dg(lower_ref[...], jnp.broadcast_to(seg, (ne, LANES)).astype(BF16))[:, 0:1]
    tot = lstart + before
    lpos1 = jnp.sum(jnp.where(sel1, tot, 0.0), axis=0, keepdims=True)
    lpos2 = jnp.sum(jnp.where(sel2, tot, 0.0), axis=0, keepdims=True)
    grel = carry_ref[...]
    carry = grel + seg
    carry_ref[...] = carry
    cnt_ref[...] = jnp.broadcast_to(carry, cnt_ref.shape)
    zero = jnp.zeros_like(w1)
    route_ref[...] = jnp.concatenate([i1, i2, lpos1, lpos2, w1, w2, zero, zero], axis=0)
    nl = seg_ref.shape[2]
    zl = jnp.zeros((1, nl), F32)
    ltot = jnp.broadcast_to(jnp.sum(seg, axis=0, keepdims=True), (1, nl))
    seg_ref[0] = jnp.concatenate([_to_lanes(seg, nl), _to_lanes(lstart, nl), _to_lanes(grel, nl), ltot,
                                  zl, zl, zl, zl], axis=0)


def _route(lt, bias_col):
    n = lt.shape[1]
    tb = TB_ROUTE
    ne = N_GROUPS * EXPERTS_PER_GROUP
    ii = jnp.arange(tb)
    upper = (ii[:, None] < ii[None, :]).astype(BF16)
    ee = jnp.arange(ne)
    lower = (ee[:, None] > ee[None, :]).astype(BF16)
    return pl.pallas_call(
        _route_kernel,
        grid=(n // tb,),
        in_specs=[pl.BlockSpec((LANES, tb), lambda i: (0, i)),
                  pl.BlockSpec((LANES, 1), lambda i: (0, 0)),
                  pl.BlockSpec((tb, tb), lambda i: (0, 0)),
                  pl.BlockSpec((ne, ne), lambda i: (0, 0))],
        out_specs=[pl.BlockSpec((8, tb), lambda i: (0, i)), pl.BlockSpec((1, 8, LANES), lambda i: (i, 0, 0)),
                   pl.BlockSpec((ne, LANES), lambda i: (0, 0))],
        out_shape=[jax.ShapeDtypeStruct((8, n), F32), jax.ShapeDtypeStruct((n // tb, 8, LANES), F32),
                   jax.ShapeDtypeStruct((ne, LANES), F32)],
        scratch_shapes=[pltpu.VMEM((ne, 1), F32)],
        compiler_params=_params(("arbitrary",)),
    )(lt, bias_col, upper, lower)


TAB_LANES = LANES


def _finalize_kernel(tm, seg_ref, cnt_ref, lower_ref, segtab_ref, tab_ref):
    ne = cnt_ref.shape[0]
    cnt = cnt_ref[...]
    nb = jnp.floor((cnt + (tm - 1)) * (1.0 / tm))
    bstart = _dg(lower_ref[...], nb.astype(BF16))
    bend = bstart + nb
    pad_start = bstart[:, 0:1] * tm
    seg = seg_ref[0]
    gstart = seg[2:3] + _to_lanes(pad_start, seg.shape[1])
    segtab_ref[0] = jnp.concatenate([seg[0:2], gstart, seg[3:8]], axis=0).astype(I32)

    nl = tab_ref.shape[1]
    n_used = jnp.max(bend[:, 0:1], axis=0, keepdims=True)
    pad_lo = _to_lanes(pad_start + cnt[:, 0:1], nl)
    pad_hi = _to_lanes(bend[:, 0:1] * tm, nl)
    zero = jnp.zeros((1, nl), F32)
    tab_ref[...] = jnp.concatenate([_to_lanes(bstart[:, 0:1], nl), jnp.broadcast_to(n_used, (1, nl)), pad_lo,
                                    pad_hi, _to_lanes(nb[:, 0:1], nl), zero, zero, zero], axis=0).astype(I32)


def _finalize(seg, cnt, tm):
    ntile = seg.shape[0]
    ne = cnt.shape[0]
    ii = jnp.arange(ne)
    lower = (ii[:, None] > ii[None, :]).astype(BF16)
    return pl.pallas_call(
        functools.partial(_finalize_kernel, tm),
        grid=(ntile,),
        in_specs=[pl.BlockSpec((1, 8, LANES), lambda i: (i, 0, 0)), pl.BlockSpec(cnt.shape, lambda i: (0, 0)),
                  pl.BlockSpec((ne, ne), lambda i: (0, 0))],
        out_specs=[pl.BlockSpec((1, 8, LANES), lambda i: (i, 0, 0)), pl.BlockSpec((8, TAB_LANES), lambda i: (0, 0))],
        out_shape=[jax.ShapeDtypeStruct((ntile, 8, LANES), I32), jax.ShapeDtypeStruct((8, TAB_LANES), I32)],
        compiler_params=_params(("arbitrary",)),
    )(seg, cnt, lower)


def _for_each_piece(length, max_len, fn):
    size = SEG_ALIGN
    sizes = []
    while size <= max_len:
        sizes.append(size)
        size *= 2
    for size in reversed(sizes):
        @pl.when(jnp.bitwise_and(length, size) != 0)
        def _(size=size):
            fn(pl.multiple_of(jnp.bitwise_and(length, -2 * size), SEG_ALIGN), size)


def _sorted_rows(td):
    return 2 * td + N_GROUPS * EXPERTS_PER_GROUP * SEG_ALIGN


def _dispatch_kernel(tm, seglen_ref, lstart_ref, gstart_ref, ltot_ref, plo_ref, phi_ref, nu_ref,
                     x_ref, route_ref, xbuf_ref, sorted_ref, zblk, sem, zsem):
    i = pl.program_id(0)
    nsteps = pl.num_programs(0)
    td = x_ref.shape[0]
    ne = plo_ref.shape[0]
    nblk = xbuf_ref.shape[0] // tm
    nrow = sorted_ref.shape[1]
    buf = i % 2

    lpos = route_ref[2:4, :]
    r = lax.broadcasted_iota(I32, (nrow, td), 0).astype(F32)
    onehot = jnp.where((r == lpos[0:1]) | (r == lpos[1:2]), 1.0, 0.0).astype(BF16)
    sorted_ref[buf] = _dg(onehot, x_ref[...].astype(BF16)).astype(BF16)

    def wait_tile(step, b):
        _for_each_piece(ltot_ref[step], nrow, lambda off, size: pltpu.make_async_copy(
            sorted_ref.at[b, pl.ds(0, size), :], xbuf_ref.at[pl.ds(0, size), :], sem.at[b]).wait())

    for e in range(ne):
        idx = i * ne + e
        ls = pl.multiple_of(lstart_ref[idx], SEG_ALIGN)
        gs = pl.multiple_of(gstart_ref[idx], SEG_ALIGN)
        _for_each_piece(seglen_ref[idx], td, lambda off, size: pltpu.make_async_copy(
            sorted_ref.at[buf, pl.ds(ls + off, size), :], xbuf_ref.at[pl.ds(gs + off, size), :],
            sem.at[buf]).start())

    @pl.when(i > 0)
    def _():
        wait_tile(i - 1, 1 - buf)

    @pl.when(i == nsteps - 1)
    def _():
        wait_tile(i, buf)

    def pad_fill(fn):
        for e in range(ne):
            lo = pl.multiple_of(plo_ref[e], SEG_ALIGN)
            _for_each_piece(phi_ref[e] - lo, tm // 2, lambda off, size: fn(pltpu.make_async_copy(
                zblk.at[pl.ds(0, size), :], xbuf_ref.at[pl.ds(lo + off, size), :], zsem)))

        def per_blk(b, carry):
            fn(pltpu.make_async_copy(zblk, xbuf_ref.at[pl.ds(pl.multiple_of(b * tm, tm), tm), :], zsem))
            return carry
        lax.fori_loop(nu_ref[0], nblk, per_blk, 0)

    @pl.when(i == 0)
    def _():
        zblk[...] = jnp.zeros_like(zblk)
        pad_fill(lambda cp: cp.start())
        pad_fill(lambda cp: cp.wait())


def _dispatch(seglen, lstart, gstart, ltot, pad_lo, pad_hi, n_used, x1, route, rows, tm):
    n, d = x1.shape
    td = TD_DISPATCH
    return pl.pallas_call(
        functools.partial(_dispatch_kernel, tm),
        grid_spec=pltpu.PrefetchScalarGridSpec(
            num_scalar_prefetch=7,
            grid=(n // td,),
            in_specs=[pl.BlockSpec((td, d), lambda i, *_: (i, 0)), pl.BlockSpec((8, td), lambda i, *_: (0, i))],
            out_specs=pl.BlockSpec(memory_space=pl.ANY),
            scratch_shapes=[pltpu.VMEM((2, _sorted_rows(td), d), BF16), pltpu.VMEM((tm, d), BF16),
                            pltpu.SemaphoreType.DMA((2,)), pltpu.SemaphoreType.DMA(())],
        ),
        out_shape=jax.ShapeDtypeStruct((rows, d), BF16),
        compiler_params=_params(("arbitrary",)),
    )(seglen, lstart, gstart, ltot, pad_lo, pad_hi, n_used, x1, route)


def _expert_kernel(tm, bstart_ref, nb_ref, nu_ref, w1_ref, w3_ref, w2_ref, xbuf_ref, ybuf_ref,
                   w1b, w3b, w2b, xb, yb, semx, semy):
    e = pl.program_id(0)
    nb = nb_ref[e]
    b0 = bstart_ref[e]
    nblk = ybuf_ref.shape[0] // tm
    rows = lambda blk: pl.ds(pl.multiple_of(blk * tm, tm), tm)
    x_copy = lambda j, slot: pltpu.make_async_copy(xbuf_ref.at[rows(b0 + j), :], xb.at[slot], semx.at[slot])
    y_copy = lambda blk, slot: pltpu.make_async_copy(yb.at[slot], ybuf_ref.at[rows(blk), :], semy.at[slot])

    @pl.when(nb > 0)
    def _():
        x_copy(0, 0).start(priority=ROW_DMA_PRIORITY)

    w1b[...] = w1_ref[0].astype(BF16)
    w3b[...] = w3_ref[0].astype(BF16)
    w2b[...] = w2_ref[0].astype(BF16)

    def body(j, carry):
        slot = j % 2
        x_copy(j, slot).wait()

        @pl.when(j + 1 < nb)
        def _():
            x_copy(j + 1, 1 - slot).start(priority=ROW_DMA_PRIORITY)

        @pl.when(j >= 2)
        def _():
            y_copy(b0 + j - 2, slot).wait()

        x = xb[slot]
        h1 = _dg(x, w1b[...])
        h3 = _dg(x, w3b[...])
        h = (h1 * _sigmoid(h1)) * h3
        yb[slot] = _dot(h, w2b[...]).astype(BF16)
        y_copy(b0 + j, slot).start(priority=ROW_DMA_PRIORITY)
        return carry

    lax.fori_loop(0, nb, body, 0)

    @pl.when(nb >= 2)
    def _():
        y_copy(b0 + nb - 2, nb % 2).wait()

    @pl.when(nb >= 1)
    def _():
        y_copy(b0 + nb - 1, (nb - 1) % 2).wait()

    @pl.when(e == pl.num_programs(0) - 1)
    def _():
        yb[0] = jnp.zeros(yb.shape[1:], yb.dtype)

        def fill(fn):
            def per_blk(blk, carry):
                fn(y_copy(blk, 0))
                return carry
            lax.fori_loop(nu_ref[0], nblk, per_blk, 0)

        fill(lambda cp: cp.start())
        fill(lambda cp: cp.wait())


def _experts(bstart, nb, n_used, xbuf, w1, w3, w2):
    rows, d = xbuf.shape
    ne, _, de = w1.shape
    tm = TM_EXPERT
    return pl.pallas_call(
        functools.partial(_expert_kernel, tm),
        grid_spec=pltpu.PrefetchScalarGridSpec(
            num_scalar_prefetch=3,
            grid=(ne,),
            in_specs=[pl.BlockSpec((1, d, de), lambda e, *_: (e, 0, 0)),
                      pl.BlockSpec((1, d, de), lambda e, *_: (e, 0, 0)),
                      pl.BlockSpec((1, de, d), lambda e, *_: (e, 0, 0)),
                      pl.BlockSpec(memory_space=pl.ANY)],
            out_specs=pl.BlockSpec(memory_space=pl.ANY),
            scratch_shapes=[pltpu.VMEM((d, de), BF16), pltpu.VMEM((d, de), BF16), pltpu.VMEM((de, d), BF16),
                            pltpu.VMEM((2, tm, d), BF16), pltpu.VMEM((2, tm, d), BF16),
                            pltpu.SemaphoreType.DMA((2,)), pltpu.SemaphoreType.DMA((2,))],
        ),
        out_shape=jax.ShapeDtypeStruct((rows, d), BF16),
        compiler_params=_params(("arbitrary",)),
    )(bstart, nb, n_used, w1, w3, w2, xbuf)


def _combine_kernel(alpha, seglen_ref, lstart_ref, gstart_ref, ltot_ref, x1_ref, rt_ref, p_ref, wpe_ref, wpg_ref,
                    g_ref, b_ref, ybuf_ref, out_ref, sorted_ref, sem):
    i = pl.program_id(0)
    nsteps = pl.num_programs(0)
    tc = x1_ref.shape[0]
    ne = N_GROUPS * EXPERTS_PER_GROUP
    nrow = sorted_ref.shape[1]

    def fetch(step, buf):
        for e in range(ne):
            idx = step * ne + e
            ls = pl.multiple_of(lstart_ref[idx], SEG_ALIGN)
            gs = pl.multiple_of(gstart_ref[idx], SEG_ALIGN)
            _for_each_piece(seglen_ref[idx], tc, lambda off, size: pltpu.make_async_copy(
                ybuf_ref.at[pl.ds(gs + off, size), :], sorted_ref.at[buf, pl.ds(ls + off, size), :],
                sem.at[buf]).start())

    @pl.when(i == 0)
    def _():
        sorted_ref[...] = jnp.zeros_like(sorted_ref)
        fetch(0, 0)

    @pl.when(i + 1 < nsteps)
    def _():
        fetch(i + 1, (i + 1) % 2)

    cur = i % 2
    _for_each_piece(ltot_ref[i], nrow, lambda off, size: pltpu.make_async_copy(
        ybuf_ref.at[pl.ds(0, size), :], sorted_ref.at[cur, pl.ds(0, size), :], sem.at[cur]).wait())

    rt = rt_ref[...]
    r = lax.broadcasted_iota(I32, (tc, nrow), 1).astype(F32)
    unsort = jnp.where(r == rt[:, 0:1], rt[:, 2:3], 0.0) + jnp.where(r == rt[:, 1:2], rt[:, 3:4], 0.0)
    ffn = _dg(unsort.astype(BF16), sorted_ref[cur])
    x2 = _layer_norm(alpha * x1_ref[...] + ffn, g_ref[...], b_ref[...])
    gate = _sigmoid(_dot(x2, wpg_ref[...]))
    out_ref[...] = x2 + gate * _dot(p_ref[...], wpe_ref[...])


def _combine(seglen, lstart, gstart, ltot, x1, route_t, pf, wpe, wpg, g, bta, ybuf, alpha):
    n, d = x1.shape
    tc = TD_DISPATCH
    tile = lambda a_: pl.BlockSpec((tc, a_.shape[1]), lambda i, *_: (i, 0))
    const = lambda a_: pl.BlockSpec(a_.shape, lambda i, *_: (0, 0))
    return pl.pallas_call(
        functools.partial(_combine_kernel, alpha),
        grid_spec=pltpu.PrefetchScalarGridSpec(
            num_scalar_prefetch=4,
            grid=(n // tc,),
            in_specs=[tile(x1), tile(route_t), tile(pf), const(wpe), const(wpg), const(g), const(bta),
                      pl.BlockSpec(memory_space=pl.ANY)],
            out_specs=pl.BlockSpec((tc, d), lambda i, *_: (i, 0)),
            scratch_shapes=[pltpu.VMEM((2, _sorted_rows(tc), d), BF16), pltpu.SemaphoreType.DMA((2,))],
        ),
        out_shape=jax.ShapeDtypeStruct((n, d), F32),
        compiler_params=_params(("arbitrary",)),
    )(seglen, lstart, gstart, ltot, x1, route_t, pf, wpe, wpg, g, bta, ybuf)


def _layer(x, p_i, w_in, rw_mu, rw_w0, rw_w_up, rw_a0, rw_a_up, rw_g_up, rw_k_k, rw_k_a, rw_r_k, rw_gn_w,
           rw_gn_b, w_a_out, hg_lb_logits, hg_norm_w, w_b_out, w_o, ln1_g, ln1_b, router_g_w, router_g_b,
           router_e_w, router_e_b, w1, w3, w2, ln2_g, ln2_b, w_pe, w_pg, alpha, layer):
    b, t, d = x.shape
    n = b * t
    rw_dim = rw_w0.shape[0]
    rw_cols = 3 * rw_dim + RW_DECAY_LORA + RW_A_LORA + RW_GATE_LORA
    hg_cols = 4 * hg_norm_w.shape[0]
    ne = N_GROUPS * EXPERTS_PER_GROUP
    row2 = lambda a_: a_.reshape(1, -1)
    xf = x.reshape(n, d)

    wb = w_in.astype(BF16)
    proj_rw, proj_hg, proj_gt = _project(xf, wb[:, :rw_cols], wb[:, rw_cols:rw_cols + hg_cols],
                                         wb[:, rw_cols + hg_cols:])
    ya = _rwkv_branch(proj_rw.reshape(b, t, rw_cols), rw_mu, row2(rw_w0), rw_w_up, rw_a0, rw_a_up, rw_g_up,
                      rw_k_k, rw_k_a, rw_r_k, rw_gn_w, rw_gn_b)
    yb = _hgrn_branch(proj_hg.reshape(b, t, hg_cols), hg_lb_logits, hg_norm_w, layer)

    wr = jnp.zeros((LANES, d), F32)
    wr = wr.at[:N_GROUPS].set(router_g_w.T).at[ROUTER_EXPERT_ROW:ROUTER_EXPERT_ROW + ne].set(router_e_w.T)
    bias = jnp.zeros((LANES,), F32)
    bias = bias.at[:N_GROUPS].set(router_g_b).at[ROUTER_EXPERT_ROW:ROUTER_EXPERT_ROW + ne].set(router_e_b)
    x1, lt = _merge(xf, ya.reshape(n, -1), yb.reshape(n, -1), proj_gt, w_a_out.astype(BF16),
                    w_b_out.astype(BF16), w_o.astype(BF16), row2(ln1_g), row2(ln1_b), wr, alpha)
    route, seg, cnt = _route(lt, bias.reshape(LANES, 1))

    tm = TM_EXPERT
    ntile = n // TD_DISPATCH
    nblk = -(-(2 * n + (SEG_ALIGN - 1) * ne * ntile) // tm) + ne
    assert TB_ROUTE == TD_DISPATCH
    segtab, tab = _finalize(seg, cnt, tm)
    per_seg = lambda row: segtab[:, row, :ne].reshape(-1)
    seglen, lstart, gstart, ltot = per_seg(0), per_seg(1), per_seg(2), segtab[:, 3, 0]
    n_used = tab[1, :1]

    xbuf = _dispatch(seglen, lstart, gstart, ltot, tab[2, :ne], tab[3, :ne], n_used, x1, route, nblk * tm, tm)
    ybuf = _experts(tab[0, :ne], tab[4, :ne], n_used, xbuf, w1, w3, w2)
    out = _combine(seglen, lstart, gstart, ltot, x1, route[2:6].T, p_i.reshape(n, -1), w_pe.astype(BF16),
                   w_pg.astype(BF16), row2(ln2_g), row2(ln2_b), ybuf, alpha)
    return out.reshape(b, t, d)


def kernel(x, p, w_in, rw_mu, rw_w0, rw_w_up, rw_a0, rw_a_up, rw_g_up, rw_k_k, rw_k_a, rw_r_k, rw_gn_w, rw_gn_b,
           w_a_out, hg_lb_logits, hg_norm_w, w_b_out, w_o, ln1_g, ln1_b, router_g_w, router_g_b, router_e_w,
           router_e_b, w1, w3, w2, ln2_g, ln2_b, w_pe, w_pg):
    depth = w_in.shape[0]
    alpha = (2 * depth) ** 0.25
    for i in range(depth):
        x = _layer(x, p[i], w_in[i], rw_mu[i], rw_w0[i], rw_w_up[i], rw_a0[i], rw_a_up[i], rw_g_up[i], rw_k_k[i],
                   rw_k_a[i], rw_r_k[i].reshape(-1), rw_gn_w[i], rw_gn_b[i], w_a_out[i], hg_lb_logits,
                   hg_norm_w[i], w_b_out[i], w_o[i], ln1_g[i], ln1_b[i],
                   router_g_w[i], router_g_b[i], router_e_w[i], router_e_b[i], w1[i], w3[i], w2[i], ln2_g[i],
                   ln2_b[i], w_pe[i], w_pg[i], alpha, i)
    return x
```

```python
import functools

import jax
import jax.numpy as jnp
from jax import lax
from jax.experimental import pallas as pl
from jax.experimental.pallas import tpu as pltpu

F32 = jnp.float32
BF16 = jnp.bfloat16
I32 = jnp.int32

NN = (((1,), (0,)), ((), ()))
NT = (((1,), (1,)), ((), ()))

RW_HEAD = 64
RW_DECAY_LORA = 64
RW_A_LORA = 64
RW_GATE_LORA = 128
RW_GN_EPS = 64e-5
RW_DECAY_SCALE = 0.6065306597126334
HG_HEADS = 4
N_GROUPS = 4
EXPERTS_PER_GROUP = 8
LN_EPS = 1e-5
RMS_EPS = 1e-6

CHUNK = 64
SUB = 16
LANES = 128
VMEM_LIMIT = 56 * 1024 * 1024

TM_PROJ = 256
TB_RWKV = 256
TB_HGRN = 512
TM_MERGE = 512
TB_ROUTE = 512
TD_DISPATCH = 512
SEG_ALIGN = 16
TM_EXPERT = 512
ROW_DMA_PRIORITY = 1
RW_WIDE = 4
HG_WIDE = 4


def _dg(a, b, dn=NN):
    return lax.dot_general(a, b, dn, preferred_element_type=F32)


def _dot(a, b, dn=NN):
    return _dg(a.astype(BF16), b.astype(BF16), dn)


def _split(a):
    hi = a.astype(BF16)
    lo = (a - hi.astype(F32)).astype(BF16)
    return hi, lo


def _dot_hl(a, b_exact, dn=NN):
    hi, lo = _split(a)
    return _dg(hi, b_exact, dn) + _dg(lo, b_exact, dn)


def _dot3(a, b, dn=NN):
    ah, al = _split(a)
    bh, bl = _split(b)
    return _dg(ah, bh, dn) + (_dg(ah, bl, dn) + _dg(al, bh, dn))


def _cumsum_chunks(x, tri):
    h = x.astype(BF16)
    r1 = x - h.astype(F32)
    m = r1.astype(BF16)
    l = (r1 - m.astype(F32)).astype(BF16)
    return _dg(tri, h) + (_dg(tri, m) + _dg(tri, l))


def _sigmoid(x):
    return 0.5 * jnp.tanh(0.5 * x) + 0.5


def _layer_norm(h, g, b):
    mu = jnp.mean(h, axis=-1, keepdims=True)
    d = h - mu
    var = jnp.mean(d * d, axis=-1, keepdims=True)
    return d * lax.rsqrt(var + LN_EPS) * g + b


def _params(sem):
    return pltpu.CompilerParams(dimension_semantics=sem, vmem_limit_bytes=VMEM_LIMIT)


def _proj_kernel(x_ref, wr_ref, wh_ref, wg_ref, pr_ref, ph_ref, pg_ref):
    xb = x_ref[...].astype(BF16)
    pr_ref[...] = _dg(xb, wr_ref[...])
    ph_ref[...] = _dg(xb, wh_ref[...])
    pg_ref[...] = _dg(xb, wg_ref[...])


def _project(xf, w_rw, w_hg, w_gt):
    n, d = xf.shape
    tm = TM_PROJ
    full = lambda w: pl.BlockSpec(w.shape, lambda i: (0, 0))
    tile = lambda c: pl.BlockSpec((tm, c), lambda i: (i, 0))
    return pl.pallas_call(
        _proj_kernel,
        grid=(n // tm,),
        in_specs=[tile(d), full(w_rw), full(w_hg), full(w_gt)],
        out_specs=[tile(w_rw.shape[1]), tile(w_hg.shape[1]), tile(w_gt.shape[1])],
        out_shape=[jax.ShapeDtypeStruct((n, w.shape[1]), F32) for w in (w_rw, w_hg, w_gt)],
        compiler_params=_params(("parallel",)),
    )(xf, w_rw, w_hg, w_gt)


def _each(f, *ls):
    return [f(*xs) for xs in zip(*ls)]


def _two(x):
    m1 = lax.broadcasted_iota(I32, x.shape, 1) < RW_HEAD
    return jnp.concatenate([jnp.where(m1, x, 0.0), jnp.where(m1, 0.0, x)], axis=0)


def _rwkv_chunk_prepare(ins, lvl_ref):
    c = CHUNK
    lane = lax.broadcasted_iota(I32, (c, LANES), 1)
    row = lax.broadcasted_iota(I32, (c, LANES), 0)
    scol = jnp.bitwise_and(lane, RW_HEAD - 1)
    strict = row > scol
    incl = row >= scol
    r2 = lax.broadcasted_iota(I32, (LANES, LANES), 0)
    c2 = lax.broadcasted_iota(I32, (LANES, LANES), 1)
    eye = jnp.where(r2 == c2, 1.0, 0.0).astype(F32)

    def prep(r, k, v, av, bv, lw, lc):
        l_end = lc[c - 1:c]
        e_r = jnp.exp(l_end - lc)
        e_n = jnp.exp(-lc)
        return dict(at=av * jnp.exp(lc - lw), rt=r * jnp.exp(lc), bt=bv * e_n, kt=k * e_n,
                    bk=jnp.concatenate([bv * e_r, k * e_r], axis=0), pc=jnp.exp(l_end), v=v)

    q = [prep(*xs) for xs in ins]
    p = [_dot(jnp.concatenate([d["at"], d["rt"]], axis=0),
              jnp.concatenate([_two(d["bt"]), _two(d["kt"])], axis=0), NT) for d in q]
    sab = [jnp.where(strict, x[:c, :LANES], 0.0) for x in p]
    sak = [jnp.where(strict, x[:c, LANES:], 0.0) for x in p]
    srb = [jnp.where(incl, x[c:, :LANES], 0.0) for x in p]
    srk = [jnp.where(incl, x[c:, LANES:], 0.0) for x in p]
    sv = _each(lambda ak, rk, d: _dot(jnp.concatenate([ak, rk], axis=0), _two(d["v"])), sak, srk, q)

    a_bd = [_two(x) for x in sab]
    t = [eye + a * lvl_ref[0] for a in a_bd]
    for lv in range(1, lvl_ref.shape[0]):
        ta = _each(lambda t_, a: _dot(t_, a * lvl_ref[lv]), t, a_bd)
        t = _each(lambda t_, ta_: t_ + _dot(ta_, t_), t, ta)

    x = _each(lambda t_, d, sv_: _dot(t_, jnp.concatenate([_two(d["at"]), _two(sv_[:c])], axis=1)), t, q, sv)
    return [dict(uk=x_[:c, :LANES] + x_[c:, :LANES],
                 w=x_[:c, LANES:] + x_[c:, LANES:],
                 rt=d["rt"], rkv=sv_[c:], srb=srb_, bk=d["bk"], v=d["v"], pc=d["pc"])
            for x_, d, sv_, srb_ in zip(x, q, sv, srb)]


def _rwkv_chunk_apply(prep, states, bd):
    c = CHUNK
    g1 = _each(lambda d, s: _dot(jnp.concatenate([d["uk"], d["rt"]], axis=0), s, NT), prep, states)
    u = _each(lambda g, d: g[:c] + d["w"], g1, prep)
    y = _each(lambda g, d, u_: g[c:] + d["rkv"] + _dot(d["srb"], _two(u_)), g1, prep, u)
    upd = _each(lambda u_, d: _dot(jnp.concatenate([u_, d["v"]], axis=0).T, d["bk"]), u, prep)
    s_new = _each(lambda d, s, up: s * d["pc"] + bd * up, prep, states, upd)
    return list(zip(y, s_new))


def _rwkv_kernel(u_ref, mu_ref, w0_ref, wup_ref, a0_ref, aup_ref, gup_ref, kk_ref, ka_ref, rk_ref,
                 gnw_ref, gnb_ref, tri_ref, gsum_ref, lvl_ref, bd_ref, ya_ref,
                 s_ref, prev_ref, r_s, k_s, v_s, a_s, b_s, lw_s, lc_s, g_s, y_s):
    tb = pl.program_id(1)

    @pl.when(tb == 0)
    def _():
        s_ref[...] = jnp.zeros_like(s_ref)
        prev_ref[...] = jnp.zeros_like(prev_ref)

    u = u_ref[0]
    nt = u.shape[0]
    dim = r_s.shape[1]
    rowid = lax.broadcasted_iota(I32, u.shape, 0)
    shifted = jnp.where(rowid == 0, prev_ref[...], pltpu.roll(u, 1, axis=0))
    prev_ref[...] = u[nt - 1:nt, :]
    um = u + (shifted - u) * mu_ref[...]

    r = um[:, 0:dim]
    k = um[:, dim:2 * dim]
    v = um[:, 2 * dim:3 * dim]
    xwa = um[:, 3 * dim:3 * dim + LANES]
    xg = um[:, 3 * dim + LANES:3 * dim + 2 * LANES]

    wpre = w0_ref[...] + _dot(jnp.tanh(xwa), wup_ref[...])
    lw = -RW_DECAY_SCALE * _sigmoid(wpre)
    a = _sigmoid(a0_ref[...] + _dot(xwa, aup_ref[...]))
    g_s[...] = _dot(_sigmoid(xg), gup_ref[...])
    kk = k * kk_ref[...]
    ss = _dot(kk * kk, gsum_ref[...])
    kk = kk * lax.rsqrt(jnp.maximum(ss, 1e-24))
    r_s[...] = r
    k_s[...] = k * (1.0 + (a - 1.0) * ka_ref[...])
    v_s[...] = v
    a_s[...] = -kk
    b_s[...] = kk * a
    lw_s[...] = lw
    lc_s[...] = _cumsum_chunks(lw, tri_ref[...])
    bd = bd_ref[...]

    npair = dim // LANES
    lanes = [slice(p * LANES, (p + 1) * LANES) for p in range(npair)]

    def group_body(gi, carry):
        rows = [pl.ds(pl.multiple_of((gi * RW_WIDE + ci) * CHUNK, CHUNK), CHUNK) for ci in range(RW_WIDE)]
        ins = [(r_s[rw, ls], k_s[rw, ls], v_s[rw, ls], a_s[rw, ls], b_s[rw, ls], lw_s[rw, ls], lc_s[rw, ls])
               for rw in rows for ls in lanes]
        prep = _rwkv_chunk_prepare(ins, lvl_ref)
        states = [s_ref[p] for p in range(npair)]
        for ci, rw in enumerate(rows):
            outs = _rwkv_chunk_apply(prep[ci * npair:(ci + 1) * npair], states, bd)
            states = [s_new for _, s_new in outs]
            for (y, _), ls in zip(outs, lanes):
                y_s[rw, ls] = y
        for p in range(npair):
            s_ref[p] = states[p]
        return carry

    lax.fori_loop(0, nt // (CHUNK * RW_WIDE), group_body, 0)

    y = y_s[...]
    gsum = gsum_ref[...]
    inv_n = 1.0 / RW_HEAD
    m = _dot_hl(y, gsum) * inv_n
    d = y - m
    var = _dot(d * d, gsum) * inv_n
    yn = d * lax.rsqrt(var + RW_GN_EPS) * gnw_ref[...] + gnb_ref[...]
    bonus = _dot(r_s[...] * k_s[...] * rk_ref[...], gsum) * v_s[...]
    ya_ref[0] = ((yn + bonus) * g_s[...]).astype(BF16)


def _rwkv_branch(proj_rw, mu, w0, wup, a0, aup, gup, k_k, k_a, r_k, gn_w, gn_b):
    b, t, cols = proj_rw.shape
    dim = w0.shape[1]
    tb = TB_RWKV
    ii = jnp.arange(tb)
    tri = ((ii[:, None] // CHUNK == ii[None, :] // CHUNK) & (ii[:, None] >= ii[None, :])).astype(BF16)
    jj = jnp.arange(dim)
    gsum = (jj[:, None] // RW_HEAD == jj[None, :] // RW_HEAD).astype(BF16)
    rr = jnp.arange(LANES)[:, None]
    cc = jnp.arange(LANES)[None, :]
    lvls = []
    s = 1
    while s < CHUNK:
        lvls.append(((rr // (2 * s) == cc // (2 * s)) & ((rr // s) % 2 == 1) & ((cc // s) % 2 == 0)).astype(F32))
        s *= 2
    lvl = jnp.stack(lvls)
    bd = (rr // RW_HEAD == cc // RW_HEAD).astype(F32)
    zpad = lambda rows: jnp.zeros((rows, dim), F32)
    wup_p = jnp.concatenate([wup, zpad(LANES - wup.shape[0])], axis=0).astype(BF16)
    aup_p = jnp.concatenate([zpad(LANES - aup.shape[0]), aup], axis=0).astype(BF16)
    row2 = lambda a_: a_.reshape(1, -1)
    const = lambda a_: pl.BlockSpec(a_.shape, lambda bi, ti: (0,) * a_.ndim)
    args = [row2(mu), row2(w0), wup_p, row2(a0), aup_p, gup.astype(BF16), row2(k_k), row2(k_a), row2(r_k),
            row2(gn_w), row2(gn_b), tri, gsum, lvl, bd]
    sc = lambda: pltpu.VMEM((tb, dim), F32)
    return pl.pallas_call(
        _rwkv_kernel,
        grid=(b, t // tb),
        in_specs=[pl.BlockSpec((1, tb, cols), lambda bi, ti: (bi, ti, 0))] + [const(a_) for a_ in args],
        out_specs=pl.BlockSpec((1, tb, dim), lambda bi, ti: (bi, ti, 0)),
        out_shape=jax.ShapeDtypeStruct((b, t, dim), BF16),
        scratch_shapes=[pltpu.VMEM((dim // LANES, LANES, LANES), F32), pltpu.VMEM((1, cols), F32)]
                       + [sc() for _ in range(9)],
        compiler_params=_params(("arbitrary", "arbitrary")),
    )(proj_rw, *args)


def _hgrn_chunk_prepare(ins):
    c = CHUNK
    subs = [(SUB * i, SUB * (i + 1)) for i in range(c // SUB)]

    def scores(q, k, lf, bc, lo, hi):
        m = bc[lo:lo + 1] - lf[lo:lo + 1]
        att = _dot(q[lo:hi] * jnp.exp(bc[lo:hi] - m), k[:hi] * jnp.exp(m - bc[:hi]), NT)
        tt = lax.broadcasted_iota(I32, (SUB, hi), 0) + lo
        s_ = lax.broadcasted_iota(I32, (SUB, hi), 1)
        return jnp.where(s_ <= tt, att, 0.0)

    att = [[scores(q, k, lf, bc, lo, hi) for lo, hi in subs] for q, k, v, lf, bc in ins]
    upd = [_dot(v.T, k * jnp.exp(bc[c - 1:c] - bc)) for q, k, v, lf, bc in ins]
    intra = [[_dot(a, x[2][:hi]) for a, (lo, hi) in zip(arow, subs)] for arow, x in zip(att, ins)]
    return [dict(intra=jnp.concatenate(rows, axis=0), qe=x[0] * jnp.exp(x[4]), upd=up, pc=jnp.exp(x[4][c - 1:c]))
            for rows, x, up in zip(intra, ins, upd)]


def _hgrn_chunk_apply(prep, states):
    o = _each(lambda d, st: d["intra"] + _dot(d["qe"], st, NT), prep, states)
    st_new = _each(lambda d, st: st * d["pc"] + d["upd"], prep, states)
    return list(zip(o, st_new))


def _hgrn_kernel(layer, u_ref, lbl_ref, nw_ref, tri_ref, gsum_ref, yb_ref,
                 st_ref, q_s, k_s, v_s, lf_s, bc_s, o_s):
    tb = pl.program_id(1)

    @pl.when(tb == 0)
    def _():
        st_ref[...] = jnp.zeros_like(st_ref)

    u = u_ref[0]
    nt = u.shape[0]
    dim = q_s.shape[1]
    lbl = lbl_ref[...]
    e = jnp.exp(lbl - jnp.max(lbl, axis=0, keepdims=True))
    lb = jnp.sum(e[0:layer + 1], axis=0, keepdims=True) / jnp.sum(e, axis=0, keepdims=True)
    zf = u[:, dim:2 * dim]
    sig = _sigmoid(zf)
    f = lb + (1.0 - lb) * sig
    qin = u[:, 0:dim]
    q_s[...] = qin * _sigmoid(qin)
    k_s[...] = (1.0 - lb) * (1.0 - sig)
    v_s[...] = u[:, 2 * dim:3 * dim]
    lf = jnp.log(f)
    lf_s[...] = lf
    bc_s[...] = _cumsum_chunks(lf, tri_ref[...])

    nhead = dim // LANES
    lanes = [slice(h * LANES, (h + 1) * LANES) for h in range(nhead)]

    def group_body(gi, carry):
        rows = [pl.ds(pl.multiple_of((gi * HG_WIDE + ci) * CHUNK, CHUNK), CHUNK) for ci in range(HG_WIDE)]
        ins = [(q_s[rw, ls], k_s[rw, ls], v_s[rw, ls], lf_s[rw, ls], bc_s[rw, ls]) for rw in rows for ls in lanes]
        prep = _hgrn_chunk_prepare(ins)
        states = [st_ref[h] for h in range(nhead)]
        for ci, rw in enumerate(rows):
            outs = _hgrn_chunk_apply(prep[ci * nhead:(ci + 1) * nhead], states)
            states = [st_new for _, st_new in outs]
            for (o, _), ls in zip(outs, lanes):
                o_s[rw, ls] = o
        for h in range(nhead):
            st_ref[h] = states[h]
        return carry

    lax.fori_loop(0, nt // (CHUNK * HG_WIDE), group_body, 0)

    o = o_s[...]
    ms = _dot(o * o, gsum_ref[...]) * (1.0 / LANES)
    og = u[:, 3 * dim:4 * dim]
    yb_ref[0] = (o * lax.rsqrt(ms + RMS_EPS) * nw_ref[...] * _sigmoid(og)).astype(BF16)


def _hgrn_branch(proj_hg, lb_logits, norm_w, layer):
    b, t, cols = proj_hg.shape
    dim = cols // 4
    tb = TB_HGRN
    ii = jnp.arange(tb)
    tri = ((ii[:, None] // CHUNK == ii[None, :] // CHUNK) & (ii[:, None] >= ii[None, :])).astype(BF16)
    jj = jnp.arange(dim)
    gsum = (jj[:, None] // LANES == jj[None, :] // LANES).astype(BF16)
    const = lambda a_: pl.BlockSpec(a_.shape, lambda bi, ti: (0,) * a_.ndim)
    args = [lb_logits, norm_w.reshape(1, -1), tri, gsum]
    sc = lambda: pltpu.VMEM((tb, dim), F32)
    return pl.pallas_call(
        functools.partial(_hgrn_kernel, layer),
        grid=(b, t // tb),
        in_specs=[pl.BlockSpec((1, tb, cols), lambda bi, ti: (bi, ti, 0))] + [const(a_) for a_ in args],
        out_specs=pl.BlockSpec((1, tb, dim), lambda bi, ti: (bi, ti, 0)),
        out_shape=jax.ShapeDtypeStruct((b, t, dim), BF16),
        scratch_shapes=[pltpu.VMEM((dim // LANES, LANES, LANES), F32)] + [sc() for _ in range(6)],
        compiler_params=_params(("arbitrary", "arbitrary")),
    )(proj_hg, *args)


def _merge_kernel(alpha, x_ref, ya_ref, yb_ref, pg_ref, wa_ref, wb_ref, wo_ref, g_ref, b_ref, wr_ref,
                  x1_ref, lt_ref):
    d = x_ref.shape[1]
    gates = _sigmoid(pg_ref[...])
    merged = gates[:, :d] * _dg(ya_ref[...], wa_ref[...]) + gates[:, d:] * _dg(yb_ref[...], wb_ref[...])
    h = alpha * x_ref[...] + _dot(merged, wo_ref[...])
    x1 = _layer_norm(h, g_ref[...], b_ref[...])
    x1_ref[...] = x1
    lt_ref[...] = _dot3(wr_ref[...], x1, NT)


def _merge(xf, ya, yb, pgate, wa, wb, wo, g, bta, wr, alpha):
    n, d = xf.shape
    tm = TM_MERGE
    tile = lambda a_: pl.BlockSpec((tm, a_.shape[1]), lambda i: (i, 0))
    const = lambda a_: pl.BlockSpec(a_.shape, lambda i: (0, 0))
    return pl.pallas_call(
        functools.partial(_merge_kernel, alpha),
        grid=(n // tm,),
        in_specs=[tile(xf), tile(ya), tile(yb), tile(pgate), const(wa), const(wb), const(wo), const(g),
                  const(bta), const(wr)],
        out_specs=[pl.BlockSpec((tm, d), lambda i: (i, 0)), pl.BlockSpec((LANES, tm), lambda i: (0, i))],
        out_shape=[jax.ShapeDtypeStruct((n, d), F32), jax.ShapeDtypeStruct((LANES, n), F32)],
        compiler_params=_params(("parallel",)),
    )(xf, ya, yb, pgate, wa, wb, wo, g, bta, wr)


ROUTER_EXPERT_ROW = 8


def _to_lanes(col, nl):
    ne = col.shape[0]
    diag = lax.broadcasted_iota(I32, (ne, nl), 0) == lax.broadcasted_iota(I32, (ne, nl), 1)
    return jnp.sum(jnp.where(diag, col, 0.0), axis=0, keepdims=True)


def _route_kernel(lt_ref, bias_ref, upper_ref, lower_ref, route_ref, seg_ref, cnt_ref, carry_ref):
    @pl.when(pl.program_id(0) == 0)
    def _():
        carry_ref[...] = jnp.zeros_like(carry_ref)

    ne = N_GROUPS * EXPERTS_PER_GROUP
    lt = lt_ref[...] + bias_ref[...]
    nb = lt.shape[1]
    neg = -jnp.inf
    lg = lt[0:8]
    rg = lax.broadcasted_iota(I32, (8, nb), 0).astype(F32)
    lg = jnp.where(rg < N_GROUPS, lg, neg)
    mg = jnp.max(lg, axis=0, keepdims=True)
    gidx = jnp.min(jnp.where(lg == mg, rg, 1e9), axis=0, keepdims=True)
    pg_sel = 1.0 / jnp.sum(jnp.exp(lg - mg), axis=0, keepdims=True)

    le = lt[ROUTER_EXPERT_ROW:ROUTER_EXPERT_ROW + ne]
    re = lax.broadcasted_iota(I32, (ne, nb), 0).astype(F32)
    in_group = jnp.floor(re * (1.0 / EXPERTS_PER_GROUP)) == gidx
    l1 = jnp.where(in_group, le, neg)
    m1 = jnp.max(l1, axis=0, keepdims=True)
    i1 = jnp.min(jnp.where(l1 == m1, re, 1e9), axis=0, keepdims=True)
    l2 = jnp.where(re == i1, neg, l1)
    m2 = jnp.max(l2, axis=0, keepdims=True)
    i2 = jnp.min(jnp.where(l2 == m2, re, 1e9), axis=0, keepdims=True)
    e2 = jnp.exp(m2 - m1)
    w1 = pg_sel / (1.0 + e2)
    w2 = pg_sel * e2 / (1.0 + e2)

    sel1 = re == i1
    sel2 = re == i2
    onehot = jnp.where(sel1 | sel2, 1.0, 0.0)
    before = _dg(onehot.astype(BF16), upper_ref[...])
    cnt_t = jnp.sum(onehot, axis=1, keepdims=True)
    seg = jnp.floor((cnt_t + (SEG_ALIGN - 1)) * (1.0 / SEG_ALIGN)) * SEG_ALIGN
    lstart = _dg(lower_ref[...], jnp.broadcast_to(seg, (ne, LANES)).astype(BF16))[:, 0:1]
    tot = lstart + before
    lpos1 = jnp.sum(jnp.where(sel1, tot, 0.0), axis=0, keepdims=True)
    lpos2 = jnp.sum(jnp.where(sel2, tot, 0.0), axis=0, keepdims=True)
    grel = carry_ref[...]
    carry = grel + seg
    carry_ref[...] = carry
    cnt_ref[...] = jnp.broadcast_to(carry, cnt_ref.shape)
    zero = jnp.zeros_like(w1)
    route_ref[...] = jnp.concatenate([i1, i2, lpos1, lpos2, w1, w2, zero, zero], axis=0)
    nl = seg_ref.shape[2]
    zl = jnp.zeros((1, nl), F32)
    ltot = jnp.broadcast_to(jnp.sum(seg, axis=0, keepdims=True), (1, nl))
    seg_ref[0] = jnp.concatenate([_to_lanes(seg, nl), _to_lanes(lstart, nl), _to_lanes(grel, nl), ltot,
                                  zl, zl, zl, zl], axis=0)


def _route(lt, bias_col):
    n = lt.shape[1]
    tb = TB_ROUTE
    ne = N_GROUPS * EXPERTS_PER_GROUP
    ii = jnp.arange(tb)
    upper = (ii[:, None] < ii[None, :]).astype(BF16)
    ee = jnp.arange(ne)
    lower = (ee[:, None] > ee[None, :]).astype(BF16)
    return pl.pallas_call(
        _route_kernel,
        grid=(n // tb,),
        in_specs=[pl.BlockSpec((LANES, tb), lambda i: (0, i)),
                  pl.BlockSpec((LANES, 1), lambda i: (0, 0)),
                  pl.BlockSpec((tb, tb), lambda i: (0, 0)),
                  pl.BlockSpec((ne, ne), lambda i: (0, 0))],
        out_specs=[pl.BlockSpec((8, tb), lambda i: (0, i)), pl.BlockSpec((1, 8, LANES), lambda i: (i, 0, 0)),
                   pl.BlockSpec((ne, LANES), lambda i: (0, 0))],
        out_shape=[jax.ShapeDtypeStruct((8, n), F32), jax.ShapeDtypeStruct((n // tb, 8, LANES), F32),
                   jax.ShapeDtypeStruct((ne, LANES), F32)],
        scratch_shapes=[pltpu.VMEM((ne, 1), F32)],
        compiler_params=_params(("arbitrary",)),
    )(lt, bias_col, upper, lower)


TAB_LANES = LANES


def _finalize_kernel(tm, seg_ref, cnt_ref, lower_ref, segtab_ref, tab_ref):
    ne = cnt_ref.shape[0]
    cnt = cnt_ref[...]
    nb = jnp.floor((cnt + (tm - 1)) * (1.0 / tm))
    bstart = _dg(lower_ref[...], nb.astype(BF16))
    bend = bstart + nb
    pad_start = bstart[:, 0:1] * tm
    seg = seg_ref[0]
    gstart = seg[2:3] + _to_lanes(pad_start, seg.shape[1])
    segtab_ref[0] = jnp.concatenate([seg[0:2], gstart, seg[3:8]], axis=0).astype(I32)

    nl = tab_ref.shape[1]
    n_used = jnp.max(bend[:, 0:1], axis=0, keepdims=True)
    pad_lo = _to_lanes(pad_start + cnt[:, 0:1], nl)
    pad_hi = _to_lanes(bend[:, 0:1] * tm, nl)
    zero = jnp.zeros((1, nl), F32)
    tab_ref[...] = jnp.concatenate([_to_lanes(bstart[:, 0:1], nl), jnp.broadcast_to(n_used, (1, nl)), pad_lo,
                                    pad_hi, _to_lanes(nb[:, 0:1], nl), zero, zero, zero], axis=0).astype(I32)


def _finalize(seg, cnt, tm):
    ntile = seg.shape[0]
    ne = cnt.shape[0]
    ii = jnp.arange(ne)
    lower = (ii[:, None] > ii[None, :]).astype(BF16)
    return pl.pallas_call(
        functools.partial(_finalize_kernel, tm),
        grid=(ntile,),
        in_specs=[pl.BlockSpec((1, 8, LANES), lambda i: (i, 0, 0)), pl.BlockSpec(cnt.shape, lambda i: (0, 0)),
                  pl.BlockSpec((ne, ne), lambda i: (0, 0))],
        out_specs=[pl.BlockSpec((1, 8, LANES), lambda i: (i, 0, 0)), pl.BlockSpec((8, TAB_LANES), lambda i: (0, 0))],
        out_shape=[jax.ShapeDtypeStruct((ntile, 8, LANES), I32), jax.ShapeDtypeStruct((8, TAB_LANES), I32)],
        compiler_params=_params(("arbitrary",)),
    )(seg, cnt, lower)


def _for_each_piece(length, max_len, fn):
    size = SEG_ALIGN
    sizes = []
    while size <= max_len:
        sizes.append(size)
        size *= 2
    for size in reversed(sizes):
        @pl.when(jnp.bitwise_and(length, size) != 0)
        def _(size=size):
            fn(pl.multiple_of(jnp.bitwise_and(length, -2 * size), SEG_ALIGN), size)


def _sorted_rows(td):
    return 2 * td + N_GROUPS * EXPERTS_PER_GROUP * SEG_ALIGN


def _dispatch_kernel(tm, seglen_ref, lstart_ref, gstart_ref, ltot_ref, plo_ref, phi_ref, nu_ref,
                     x_ref, route_ref, xbuf_ref, sorted_ref, zblk, sem, zsem):
    i = pl.program_id(0)
    nsteps = pl.num_programs(0)
    td = x_ref.shape[0]
    ne = plo_ref.shape[0]
    nblk = xbuf_ref.shape[0] // tm
    nrow = sorted_ref.shape[1]
    buf = i % 2

    lpos = route_ref[2:4, :]
    r = lax.broadcasted_iota(I32, (nrow, td), 0).astype(F32)
    onehot = jnp.where((r == lpos[0:1]) | (r == lpos[1:2]), 1.0, 0.0).astype(BF16)
    sorted_ref[buf] = _dg(onehot, x_ref[...].astype(BF16)).astype(BF16)

    def wait_tile(step, b):
        _for_each_piece(ltot_ref[step], nrow, lambda off, size: pltpu.make_async_copy(
            sorted_ref.at[b, pl.ds(0, size), :], xbuf_ref.at[pl.ds(0, size), :], sem.at[b]).wait())

    for e in range(ne):
        idx = i * ne + e
        ls = pl.multiple_of(lstart_ref[idx], SEG_ALIGN)
        gs = pl.multiple_of(gstart_ref[idx], SEG_ALIGN)
        _for_each_piece(seglen_ref[idx], td, lambda off, size: pltpu.make_async_copy(
            sorted_ref.at[buf, pl.ds(ls + off, size), :], xbuf_ref.at[pl.ds(gs + off, size), :],
            sem.at[buf]).start())

    @pl.when(i > 0)
    def _():
        wait_tile(i - 1, 1 - buf)

    @pl.when(i == nsteps - 1)
    def _():
        wait_tile(i, buf)

    def pad_fill(fn):
        for e in range(ne):
            lo = pl.multiple_of(plo_ref[e], SEG_ALIGN)
            _for_each_piece(phi_ref[e] - lo, tm // 2, lambda off, size: fn(pltpu.make_async_copy(
                zblk.at[pl.ds(0, size), :], xbuf_ref.at[pl.ds(lo + off, size), :], zsem)))

        def per_blk(b, carry):
            fn(pltpu.make_async_copy(zblk, xbuf_ref.at[pl.ds(pl.multiple_of(b * tm, tm), tm), :], zsem))
            return carry
        lax.fori_loop(nu_ref[0], nblk, per_blk, 0)

    @pl.when(i == 0)
    def _():
        zblk[...] = jnp.zeros_like(zblk)
        pad_fill(lambda cp: cp.start())
        pad_fill(lambda cp: cp.wait())


def _dispatch(seglen, lstart, gstart, ltot, pad_lo, pad_hi, n_used, x1, route, rows, tm):
    n, d = x1.shape
    td = TD_DISPATCH
    return pl.pallas_call(
        functools.partial(_dispatch_kernel, tm),
        grid_spec=pltpu.PrefetchScalarGridSpec(
            num_scalar_prefetch=7,
            grid=(n // td,),
            in_specs=[pl.BlockSpec((td, d), lambda i, *_: (i, 0)), pl.BlockSpec((8, td), lambda i, *_: (0, i))],
            out_specs=pl.BlockSpec(memory_space=pl.ANY),
            scratch_shapes=[pltpu.VMEM((2, _sorted_rows(td), d), BF16), pltpu.VMEM((tm, d), BF16),
                            pltpu.SemaphoreType.DMA((2,)), pltpu.SemaphoreType.DMA(())],
        ),
        out_shape=jax.ShapeDtypeStruct((rows, d), BF16),
        compiler_params=_params(("arbitrary",)),
    )(seglen, lstart, gstart, ltot, pad_lo, pad_hi, n_used, x1, route)


def _expert_kernel(tm, bstart_ref, nb_ref, nu_ref, w1_ref, w3_ref, w2_ref, xbuf_ref, ybuf_ref,
                   w13b, w2b, xb, yb, semx, semy):
    e = pl.program_id(0)
    nb = nb_ref[e]
    b0 = bstart_ref[e]
    nblk = ybuf_ref.shape[0] // tm
    rows = lambda blk: pl.ds(pl.multiple_of(blk * tm, tm), tm)
    x_copy = lambda j, slot: pltpu.make_async_copy(xbuf_ref.at[rows(b0 + j), :], xb.at[slot], semx.at[slot])
    y_copy = lambda blk, slot: pltpu.make_async_copy(yb.at[slot], ybuf_ref.at[rows(blk), :], semy.at[slot])

    @pl.when(nb > 0)
    def _():
        x_copy(0, 0).start(priority=ROW_DMA_PRIORITY)

    de = w2b.shape[0]
    w13b[:, :de] = w1_ref[0].astype(BF16)
    w13b[:, de:] = w3_ref[0].astype(BF16)
    w2b[...] = w2_ref[0].astype(BF16)

    def body(j, carry):
        slot = j % 2
        x_copy(j, slot).wait()

        @pl.when(j + 1 < nb)
        def _():
            x_copy(j + 1, 1 - slot).start(priority=ROW_DMA_PRIORITY)

        @pl.when(j >= 2)
        def _():
            y_copy(b0 + j - 2, slot).wait()

        x = xb[slot]
        h13 = _dg(x, w13b[...])
        h1 = h13[:, :de]
        h = (h1 * _sigmoid(h1)) * h13[:, de:]
        yb[slot] = _dot(h, w2b[...]).astype(BF16)
        y_copy(b0 + j, slot).start(priority=ROW_DMA_PRIORITY)
        return carry

    lax.fori_loop(0, nb, body, 0)

    @pl.when(nb >= 2)
    def _():
        y_copy(b0 + nb - 2, nb % 2).wait()

    @pl.when(nb >= 1)
    def _():
        y_copy(b0 + nb - 1, (nb - 1) % 2).wait()

    @pl.when(e == pl.num_programs(0) - 1)
    def _():
        yb[0] = jnp.zeros(yb.shape[1:], yb.dtype)

        def fill(fn):
            def per_blk(blk, carry):
                fn(y_copy(blk, 0))
                return carry
            lax.fori_loop(nu_ref[0], nblk, per_blk, 0)

        fill(lambda cp: cp.start())
        fill(lambda cp: cp.wait())


def _experts(bstart, nb, n_used, xbuf, w1, w3, w2):
    rows, d = xbuf.shape
    ne, _, de = w1.shape
    tm = TM_EXPERT
    return pl.pallas_call(
        functools.partial(_expert_kernel, tm),
        grid_spec=pltpu.PrefetchScalarGridSpec(
            num_scalar_prefetch=3,
            grid=(ne,),
            in_specs=[pl.BlockSpec((1, d, de), lambda e, *_: (e, 0, 0)),
                      pl.BlockSpec((1, d, de), lambda e, *_: (e, 0, 0)),
                      pl.BlockSpec((1, de, d), lambda e, *_: (e, 0, 0)),
                      pl.BlockSpec(memory_space=pl.ANY)],
            out_specs=pl.BlockSpec(memory_space=pl.ANY),
            scratch_shapes=[pltpu.VMEM((d, 2 * de), BF16), pltpu.VMEM((de, d), BF16),
                            pltpu.VMEM((2, tm, d), BF16), pltpu.VMEM((2, tm, d), BF16),
                            pltpu.SemaphoreType.DMA((2,)), pltpu.SemaphoreType.DMA((2,))],
        ),
        out_shape=jax.ShapeDtypeStruct((rows, d), BF16),
        compiler_params=_params(("arbitrary",)),
    )(bstart, nb, n_used, w1, w3, w2, xbuf)


def _combine_kernel(alpha, seglen_ref, lstart_ref, gstart_ref, ltot_ref, x1_ref, rt_ref, p_ref, wpe_ref, wpg_ref,
                    g_ref, b_ref, ybuf_ref, out_ref, sorted_ref, sem):
    i = pl.program_id(0)
    nsteps = pl.num_programs(0)
    tc = x1_ref.shape[0]
    ne = N_GROUPS * EXPERTS_PER_GROUP
    nrow = sorted_ref.shape[1]

    def fetch(step, buf):
        for e in range(ne):
            idx = step * ne + e
            ls = pl.multiple_of(lstart_ref[idx], SEG_ALIGN)
            gs = pl.multiple_of(gstart_ref[idx], SEG_ALIGN)
            _for_each_piece(seglen_ref[idx], tc, lambda off, size: pltpu.make_async_copy(
                ybuf_ref.at[pl.ds(gs + off, size), :], sorted_ref.at[buf, pl.ds(ls + off, size), :],
                sem.at[buf]).start())

    @pl.when(i == 0)
    def _():
        sorted_ref[...] = jnp.zeros_like(sorted_ref)
        fetch(0, 0)

    @pl.when(i + 1 < nsteps)
    def _():
        fetch(i + 1, (i + 1) % 2)

    cur = i % 2
    _for_each_piece(ltot_ref[i], nrow, lambda off, size: pltpu.make_async_copy(
        ybuf_ref.at[pl.ds(0, size), :], sorted_ref.at[cur, pl.ds(0, size), :], sem.at[cur]).wait())

    rt = rt_ref[...]
    r = lax.broadcasted_iota(I32, (tc, nrow), 1).astype(F32)
    unsort = jnp.where(r == rt[:, 0:1], rt[:, 2:3], 0.0) + jnp.where(r == rt[:, 1:2], rt[:, 3:4], 0.0)
    ffn = _dg(unsort.astype(BF16), sorted_ref[cur])
    x2 = _layer_norm(alpha * x1_ref[...] + ffn, g_ref[...], b_ref[...])
    gate = _sigmoid(_dot(x2, wpg_ref[...]))
    out_ref[...] = x2 + gate * _dot(p_ref[...], wpe_ref[...])


def _combine(seglen, lstart, gstart, ltot, x1, route_t, pf, wpe, wpg, g, bta, ybuf, alpha):
    n, d = x1.shape
    tc = TD_DISPATCH
    tile = lambda a_: pl.BlockSpec((tc, a_.shape[1]), lambda i, *_: (i, 0))
    const = lambda a_: pl.BlockSpec(a_.shape, lambda i, *_: (0, 0))
    return pl.pallas_call(
        functools.partial(_combine_kernel, alpha),
        grid_spec=pltpu.PrefetchScalarGridSpec(
            num_scalar_prefetch=4,
            grid=(n // tc,),
            in_specs=[tile(x1), tile(route_t), tile(pf), const(wpe), const(wpg), const(g), const(bta),
                      pl.BlockSpec(memory_space=pl.ANY)],
            out_specs=pl.BlockSpec((tc, d), lambda i, *_: (i, 0)),
            scratch_shapes=[pltpu.VMEM((2, _sorted_rows(tc), d), BF16), pltpu.SemaphoreType.DMA((2,))],
        ),
        out_shape=jax.ShapeDtypeStruct((n, d), F32),
        compiler_params=_params(("arbitrary",)),
    )(seglen, lstart, gstart, ltot, x1, route_t, pf, wpe, wpg, g, bta, ybuf)


def _layer(x, p_i, w_in, rw_mu, rw_w0, rw_w_up, rw_a0, rw_a_up, rw_g_up, rw_k_k, rw_k_a, rw_r_k, rw_gn_w,
           rw_gn_b, w_a_out, hg_lb_logits, hg_norm_w, w_b_out, w_o, ln1_g, ln1_b, router_g_w, router_g_b,
           router_e_w, router_e_b, w1, w3, w2, ln2_g, ln2_b, w_pe, w_pg, alpha, layer):
    b, t, d = x.shape
    n = b * t
    rw_dim = rw_w0.shape[0]
    rw_cols = 3 * rw_dim + RW_DECAY_LORA + RW_A_LORA + RW_GATE_LORA
    hg_cols = 4 * hg_norm_w.shape[0]
    ne = N_GROUPS * EXPERTS_PER_GROUP
    row2 = lambda a_: a_.reshape(1, -1)
    xf = x.reshape(n, d)

    wb = w_in.astype(BF16)
    proj_rw, proj_hg, proj_gt = _project(xf, wb[:, :rw_cols], wb[:, rw_cols:rw_cols + hg_cols],
                                         wb[:, rw_cols + hg_cols:])
    ya = _rwkv_branch(proj_rw.reshape(b, t, rw_cols), rw_mu, row2(rw_w0), rw_w_up, rw_a0, rw_a_up, rw_g_up,
                      rw_k_k, rw_k_a, rw_r_k, rw_gn_w, rw_gn_b)
    yb = _hgrn_branch(proj_hg.reshape(b, t, hg_cols), hg_lb_logits, hg_norm_w, layer)

    wr = jnp.zeros((LANES, d), F32)
    wr = wr.at[:N_GROUPS].set(router_g_w.T).at[ROUTER_EXPERT_ROW:ROUTER_EXPERT_ROW + ne].set(router_e_w.T)
    bias = jnp.zeros((LANES,), F32)
    bias = bias.at[:N_GROUPS].set(router_g_b).at[ROUTER_EXPERT_ROW:ROUTER_EXPERT_ROW + ne].set(router_e_b)
    x1, lt = _merge(xf, ya.reshape(n, -1), yb.reshape(n, -1), proj_gt, w_a_out.astype(BF16),
                    w_b_out.astype(BF16), w_o.astype(BF16), row2(ln1_g), row2(ln1_b), wr, alpha)
    route, seg, cnt = _route(lt, bias.reshape(LANES, 1))

    tm = TM_EXPERT
    ntile = n // TD_DISPATCH
    nblk = -(-(2 * n + (SEG_ALIGN - 1) * ne * ntile) // tm) + ne
    assert TB_ROUTE == TD_DISPATCH
    segtab, tab = _finalize(seg, cnt, tm)
    per_seg = lambda row: segtab[:, row, :ne].reshape(-1)
    seglen, lstart, gstart, ltot = per_seg(0), per_seg(1), per_seg(2), segtab[:, 3, 0]
    n_used = tab[1, :1]

    xbuf = _dispatch(seglen, lstart, gstart, ltot, tab[2, :ne], tab[3, :ne], n_used, x1, route, nblk * tm, tm)
    ybuf = _experts(tab[0, :ne], tab[4, :ne], n_used, xbuf, w1, w3, w2)
    out = _combine(seglen, lstart, gstart, ltot, x1, route[2:6].T, p_i.reshape(n, -1), w_pe.astype(BF16),
                   w_pg.astype(BF16), row2(ln2_g), row2(ln2_b), ybuf, alpha)
    return out.reshape(b, t, d)


def kernel(x, p, w_in, rw_mu, rw_w0, rw_w_up, rw_a0, rw_a_up, rw_g_up, rw_k_k, rw_k_a, rw_r_k, rw_gn_w, rw_gn_b,
           w_a_out, hg_lb_logits, hg_norm_w, w_b_out, w_o, ln1_g, ln1_b, router_g_w, router_g_b, router_e_w,
           router_e_b, w1, w3, w2, ln2_g, ln2_b, w_pe, w_pg):
    depth = w_in.shape[0]
    alpha = (2 * depth) ** 0.25
    for i in range(depth):
        x = _layer(x, p[i], w_in[i], rw_mu[i], rw_w0[i], rw_w_up[i], rw_a0[i], rw_a_up[i], rw_g_up[i], rw_k_k[i],
                   rw_k_a[i], rw_r_k[i].reshape(-1), rw_gn_w[i], rw_gn_b[i], w_a_out[i], hg_lb_logits,
                   hg_norm_w[i], w_b_out[i], w_o[i], ln1_g[i], ln1_b[i],
                   router_g_w[i], router_g_b[i], router_e_w[i], router_e_b[i], w1[i], w3[i], w2[i], ln2_g[i],
                   ln2_b[i], w_pe[i], w_pg[i], alpha, i)
    return x
```

```python
import functools

import jax
import jax.numpy as jnp
from jax import lax
from jax.experimental import pallas as pl
from jax.experimental.pallas import tpu as pltpu

F32 = jnp.float32
BF16 = jnp.bfloat16
I32 = jnp.int32

NN = (((1,), (0,)), ((), ()))
NT = (((1,), (1,)), ((), ()))

RW_HEAD = 64
RW_DECAY_LORA = 64
RW_A_LORA = 64
RW_GATE_LORA = 128
RW_GN_EPS = 64e-5
RW_DECAY_SCALE = 0.6065306597126334
HG_HEADS = 4
N_GROUPS = 4
EXPERTS_PER_GROUP = 8
LN_EPS = 1e-5
RMS_EPS = 1e-6

CHUNK = 64
SUB = 16
LANES = 128
VMEM_LIMIT = 56 * 1024 * 1024

TM_PROJ = 256
TB_RWKV = 1024
TB_HGRN = 256
TM_MERGE = 512
TB_ROUTE = 512
TD_DISPATCH = 512
SEG_ALIGN = 16
TM_EXPERT = 512
ROW_DMA_PRIORITY = 1
RW_WIDE = 4
HG_WIDE = 4


def _dg(a, b, dn=NN):
    return lax.dot_general(a, b, dn, preferred_element_type=F32)


def _dot(a, b, dn=NN):
    return _dg(a.astype(BF16), b.astype(BF16), dn)


def _split(a):
    hi = a.astype(BF16)
    lo = (a - hi.astype(F32)).astype(BF16)
    return hi, lo


def _dot_hl(a, b_exact, dn=NN):
    hi, lo = _split(a)
    return _dg(hi, b_exact, dn) + _dg(lo, b_exact, dn)


def _dot3(a, b, dn=NN):
    ah, al = _split(a)
    bh, bl = _split(b)
    return _dg(ah, bh, dn) + (_dg(ah, bl, dn) + _dg(al, bh, dn))


def _cumsum_chunks(x, tri):
    h = x.astype(BF16)
    r1 = x - h.astype(F32)
    m = r1.astype(BF16)
    l = (r1 - m.astype(F32)).astype(BF16)
    return _dg(tri, h) + (_dg(tri, m) + _dg(tri, l))


def _sigmoid(x):
    return 0.5 * jnp.tanh(0.5 * x) + 0.5


def _layer_norm(h, g, b):
    mu = jnp.mean(h, axis=-1, keepdims=True)
    d = h - mu
    var = jnp.mean(d * d, axis=-1, keepdims=True)
    return d * lax.rsqrt(var + LN_EPS) * g + b


def _params(sem):
    return pltpu.CompilerParams(dimension_semantics=sem, vmem_limit_bytes=VMEM_LIMIT)


def _proj_kernel(x_ref, wr_ref, wh_ref, wg_ref, pr_ref, ph_ref, pg_ref):
    xb = x_ref[...].astype(BF16)
    pr_ref[...] = _dg(xb, wr_ref[...])
    ph_ref[...] = _dg(xb, wh_ref[...])
    pg_ref[...] = _dg(xb, wg_ref[...])


def _project(xf, w_rw, w_hg, w_gt):
    n, d = xf.shape
    tm = TM_PROJ
    full = lambda w: pl.BlockSpec(w.shape, lambda i: (0, 0))
    tile = lambda c: pl.BlockSpec((tm, c), lambda i: (i, 0))
    return pl.pallas_call(
        _proj_kernel,
        grid=(n // tm,),
        in_specs=[tile(d), full(w_rw), full(w_hg), full(w_gt)],
        out_specs=[tile(w_rw.shape[1]), tile(w_hg.shape[1]), tile(w_gt.shape[1])],
        out_shape=[jax.ShapeDtypeStruct((n, w.shape[1]), F32) for w in (w_rw, w_hg, w_gt)],
        compiler_params=_params(("parallel",)),
    )(xf, w_rw, w_hg, w_gt)


def _each(f, *ls):
    return [f(*xs) for xs in zip(*ls)]


def _two(x):
    m1 = lax.broadcasted_iota(I32, x.shape, 1) < RW_HEAD
    return jnp.concatenate([jnp.where(m1, x, 0.0), jnp.where(m1, 0.0, x)], axis=0)


def _rwkv_chunk_prepare(ins, lvl_ref, out):
    c = CHUNK
    lane = lax.broadcasted_iota(I32, (c, LANES), 1)
    row = lax.broadcasted_iota(I32, (c, LANES), 0)
    scol = jnp.bitwise_and(lane, RW_HEAD - 1)
    strict = row > scol
    incl = row >= scol
    r2 = lax.broadcasted_iota(I32, (LANES, LANES), 0)
    c2 = lax.broadcasted_iota(I32, (LANES, LANES), 1)
    eye = jnp.where(r2 == c2, 1.0, 0.0).astype(F32)

    def prep(r, k, v, av, bv, lw, lc):
        l_end = lc[c - 1:c]
        e_r = jnp.exp(l_end - lc)
        e_n = jnp.exp(-lc)
        return dict(at=av * jnp.exp(lc - lw), rt=r * jnp.exp(lc), bt=bv * e_n, kt=k * e_n,
                    bk=jnp.concatenate([bv * e_r, k * e_r], axis=0), pc=jnp.exp(l_end), v=v)

    q = [prep(*xs) for xs in ins]
    yield
    p = [_dot(jnp.concatenate([d["at"], d["rt"]], axis=0),
              jnp.concatenate([_two(d["bt"]), _two(d["kt"])], axis=0), NT) for d in q]
    sab = [jnp.where(strict, x[:c, :LANES], 0.0) for x in p]
    sak = [jnp.where(strict, x[:c, LANES:], 0.0) for x in p]
    srb = [jnp.where(incl, x[c:, :LANES], 0.0) for x in p]
    srk = [jnp.where(incl, x[c:, LANES:], 0.0) for x in p]
    yield
    sv = _each(lambda ak, rk, d: _dot(jnp.concatenate([ak, rk], axis=0), _two(d["v"])), sak, srk, q)
    yield

    a_bd = [_two(x) for x in sab]
    t = [eye + a * lvl_ref[0] for a in a_bd]
    for lv in range(1, lvl_ref.shape[0]):
        ta = _each(lambda t_, a: _dot(t_, a * lvl_ref[lv]), t, a_bd)
        yield
        t = _each(lambda t_, ta_: t_ + _dot(ta_, t_), t, ta)
        yield

    x = _each(lambda t_, d, sv_: _dot(t_, jnp.concatenate([_two(d["at"]), _two(sv_[:c])], axis=1)), t, q, sv)
    out.extend(dict(uk=x_[:c, :LANES] + x_[c:, :LANES],
                    w=x_[:c, LANES:] + x_[c:, LANES:],
                    rt=d["rt"], rkv=sv_[c:], srb=srb_, bk=d["bk"], v=d["v"], pc=d["pc"])
               for x_, d, sv_, srb_ in zip(x, q, sv, srb))


def _rwkv_chunk_apply(prep, states, bd, out):
    c = CHUNK
    g1 = _each(lambda d, s: _dot(jnp.concatenate([d["uk"], d["rt"]], axis=0), s, NT), prep, states)
    yield
    u = _each(lambda g, d: g[:c] + d["w"], g1, prep)
    y = _each(lambda g, d, u_: g[c:] + d["rkv"] + _dot(d["srb"], _two(u_)), g1, prep, u)
    yield
    upd = _each(lambda u_, d: _dot(jnp.concatenate([u_, d["v"]], axis=0).T, d["bk"]), u, prep)
    s_new = _each(lambda d, s, up: s * d["pc"] + bd * up, prep, states, upd)
    out.extend(zip(y, s_new))
    yield


def _interleave(*gens):
    live = [g for g in gens if g is not None]
    while live:
        for g in list(live):
            try:
                next(g)
            except StopIteration:
                live.remove(g)


def _rwkv_kernel(u_ref, mu_ref, w0_ref, wup_ref, a0_ref, aup_ref, gup_ref, kk_ref, ka_ref, rk_ref,
                 gnw_ref, gnb_ref, tri_ref, gsum_ref, lvl_ref, bd_ref, ya_ref,
                 s_ref, prev_ref, r_s, k_s, v_s, a_s, b_s, lw_s, lc_s, g_s, y_s):
    tb = pl.program_id(1)

    @pl.when(tb == 0)
    def _():
        s_ref[...] = jnp.zeros_like(s_ref)
        prev_ref[...] = jnp.zeros_like(prev_ref)

    nt = u_ref.shape[1]
    dim = r_s.shape[1]
    part = CHUNK * RW_WIDE
    npair = dim // LANES
    lanes = [slice(p * LANES, (p + 1) * LANES) for p in range(npair)]
    bd = bd_ref[...]
    gsum = gsum_ref[...]

    def prologue(h):
        rows = slice(h * part, (h + 1) * part)
        u = u_ref[0, rows, :]
        before = prev_ref[...] if h == 0 else u_ref[0, h * part - 1:h * part, :]
        rowid = lax.broadcasted_iota(I32, u.shape, 0)
        shifted = jnp.where(rowid == 0, before, pltpu.roll(u, 1, axis=0))
        um = u + (shifted - u) * mu_ref[...]
        r = um[:, 0:dim]
        k = um[:, dim:2 * dim]
        v = um[:, 2 * dim:3 * dim]
        xwa = um[:, 3 * dim:3 * dim + LANES]
        xg = um[:, 3 * dim + LANES:3 * dim + 2 * LANES]
        yield
        wpre = w0_ref[...] + _dot(jnp.tanh(xwa), wup_ref[...])
        lw = -RW_DECAY_SCALE * _sigmoid(wpre)
        a = _sigmoid(a0_ref[...] + _dot(xwa, aup_ref[...]))
        g_s[rows, :] = _dot(_sigmoid(xg), gup_ref[...])
        yield
        kk = k * kk_ref[...]
        ss = _dot(kk * kk, gsum)
        kk = kk * lax.rsqrt(jnp.maximum(ss, 1e-24))
        r_s[rows, :] = r
        k_s[rows, :] = k * (1.0 + (a - 1.0) * ka_ref[...])
        v_s[rows, :] = v
        a_s[rows, :] = -kk
        b_s[rows, :] = kk * a
        yield
        lw_s[rows, :] = lw
        lc_s[rows, :] = _cumsum_chunks(lw, tri_ref[...])

    def chunk_rows(h):
        return [slice(h * part + ci * CHUNK, h * part + (ci + 1) * CHUNK) for ci in range(RW_WIDE)]

    def prepare(h, out):
        ins = [(r_s[rw, ls], k_s[rw, ls], v_s[rw, ls], a_s[rw, ls], b_s[rw, ls], lw_s[rw, ls], lc_s[rw, ls])
               for rw in chunk_rows(h) for ls in lanes]
        yield from _rwkv_chunk_prepare(ins, lvl_ref, out)

    def apply(h, prep):
        states = [s_ref[p] for p in range(npair)]
        for ci, rw in enumerate(chunk_rows(h)):
            outs = []
            yield from _rwkv_chunk_apply(prep[ci * npair:(ci + 1) * npair], states, bd, outs)
            states = [s_new for _, s_new in outs]
            for (y, _), ls in zip(outs, lanes):
                y_s[rw, ls] = y
        for p in range(npair):
            s_ref[p] = states[p]

    def epilogue(h):
        rows = slice(h * part, (h + 1) * part)
        y = y_s[rows, :]
        inv_n = 1.0 / RW_HEAD
        m = _dot_hl(y, gsum) * inv_n
        d = y - m
        yield
        var = _dot(d * d, gsum) * inv_n
        yn = d * lax.rsqrt(var + RW_GN_EPS) * gnw_ref[...] + gnb_ref[...]
        yield
        bonus = _dot(r_s[rows, :] * k_s[rows, :] * rk_ref[...], gsum) * v_s[rows, :]
        ya_ref[0, rows, :] = ((yn + bonus) * g_s[rows, :]).astype(BF16)

    nparts = nt // part
    preps = [[] for _ in range(nparts)]
    _interleave(prologue(0))
    prev_ref[...] = u_ref[0, nt - 1:nt, :]
    _interleave(prepare(0, preps[0]), prologue(1) if nparts > 1 else None)
    for h in range(nparts):
        _interleave(apply(h, preps[h]),
                    prepare(h + 1, preps[h + 1]) if h + 1 < nparts else None,
                    prologue(h + 2) if h + 2 < nparts else None,
                    epilogue(h - 1) if h > 0 else None)
    _interleave(epilogue(nparts - 1))


def _rwkv_branch(proj_rw, mu, w0, wup, a0, aup, gup, k_k, k_a, r_k, gn_w, gn_b):
    b, t, cols = proj_rw.shape
    dim = w0.shape[1]
    tb = TB_RWKV
    ii = jnp.arange(CHUNK * RW_WIDE)
    tri = ((ii[:, None] // CHUNK == ii[None, :] // CHUNK) & (ii[:, None] >= ii[None, :])).astype(BF16)
    jj = jnp.arange(dim)
    gsum = (jj[:, None] // RW_HEAD == jj[None, :] // RW_HEAD).astype(BF16)
    rr = jnp.arange(LANES)[:, None]
    cc = jnp.arange(LANES)[None, :]
    lvls = []
    s = 1
    while s < CHUNK:
        lvls.append(((rr // (2 * s) == cc // (2 * s)) & ((rr // s) % 2 == 1) & ((cc // s) % 2 == 0)).astype(F32))
        s *= 2
    lvl = jnp.stack(lvls)
    bd = (rr // RW_HEAD == cc // RW_HEAD).astype(F32)
    zpad = lambda rows: jnp.zeros((rows, dim), F32)
    wup_p = jnp.concatenate([wup, zpad(LANES - wup.shape[0])], axis=0).astype(BF16)
    aup_p = jnp.concatenate([zpad(LANES - aup.shape[0]), aup], axis=0).astype(BF16)
    row2 = lambda a_: a_.reshape(1, -1)
    const = lambda a_: pl.BlockSpec(a_.shape, lambda bi, ti: (0,) * a_.ndim)
    args = [row2(mu), row2(w0), wup_p, row2(a0), aup_p, gup.astype(BF16), row2(k_k), row2(k_a), row2(r_k),
            row2(gn_w), row2(gn_b), tri, gsum, lvl, bd]
    sc = lambda: pltpu.VMEM((tb, dim), F32)
    return pl.pallas_call(
        _rwkv_kernel,
        grid=(b, t // tb),
        in_specs=[pl.BlockSpec((1, tb, cols), lambda bi, ti: (bi, ti, 0))] + [const(a_) for a_ in args],
        out_specs=pl.BlockSpec((1, tb, dim), lambda bi, ti: (bi, ti, 0)),
        out_shape=jax.ShapeDtypeStruct((b, t, dim), BF16),
        scratch_shapes=[pltpu.VMEM((dim // LANES, LANES, LANES), F32), pltpu.VMEM((1, cols), F32)]
                       + [sc() for _ in range(9)],
        compiler_params=_params(("arbitrary", "arbitrary")),
    )(proj_rw, *args)


def _hgrn_chunk_prepare(ins):
    c = CHUNK
    subs = [(SUB * i, SUB * (i + 1)) for i in range(c // SUB)]

    def scores(q, k, lf, bc, lo, hi):
        m = bc[lo:lo + 1] - lf[lo:lo + 1]
        att = _dot(q[lo:hi] * jnp.exp(bc[lo:hi] - m), k[:hi] * jnp.exp(m - bc[:hi]), NT)
        tt = lax.broadcasted_iota(I32, (SUB, hi), 0) + lo
        s_ = lax.broadcasted_iota(I32, (SUB, hi), 1)
        return jnp.where(s_ <= tt, att, 0.0)

    att = [[scores(q, k, lf, bc, lo, hi) for lo, hi in subs] for q, k, v, lf, bc in ins]
    upd = [_dot(v.T, k * jnp.exp(bc[c - 1:c] - bc)) for q, k, v, lf, bc in ins]
    intra = [[_dot(a, x[2][:hi]) for a, (lo, hi) in zip(arow, subs)] for arow, x in zip(att, ins)]
    return [dict(intra=jnp.concatenate(rows, axis=0), qe=x[0] * jnp.exp(x[4]), upd=up, pc=jnp.exp(x[4][c - 1:c]))
            for rows, x, up in zip(intra, ins, upd)]


def _hgrn_chunk_apply(prep, states):
    o = _each(lambda d, st: d["intra"] + _dot(d["qe"], st, NT), prep, states)
    st_new = _each(lambda d, st: st * d["pc"] + d["upd"], prep, states)
    return list(zip(o, st_new))


def _hgrn_kernel(layer, u_ref, lbl_ref, nw_ref, tri_ref, gsum_ref, yb_ref,
                 st_ref, q_s, k_s, v_s, lf_s, bc_s, o_s):
    tb = pl.program_id(1)

    @pl.when(tb == 0)
    def _():
        st_ref[...] = jnp.zeros_like(st_ref)

    u = u_ref[0]
    nt = u.shape[0]
    dim = q_s.shape[1]
    lbl = lbl_ref[...]
    e = jnp.exp(lbl - jnp.max(lbl, axis=0, keepdims=True))
    lb = jnp.sum(e[0:layer + 1], axis=0, keepdims=True) / jnp.sum(e, axis=0, keepdims=True)
    zf = u[:, dim:2 * dim]
    sig = _sigmoid(zf)
    f = lb + (1.0 - lb) * sig
    qin = u[:, 0:dim]
    q_s[...] = qin * _sigmoid(qin)
    k_s[...] = (1.0 - lb) * (1.0 - sig)
    v_s[...] = u[:, 2 * dim:3 * dim]
    lf = jnp.log(f)
    lf_s[...] = lf
    bc_s[...] = _cumsum_chunks(lf, tri_ref[...])

    nhead = dim // LANES
    lanes = [slice(h * LANES, (h + 1) * LANES) for h in range(nhead)]

    def group_body(gi, carry):
        rows = [pl.ds(pl.multiple_of((gi * HG_WIDE + ci) * CHUNK, CHUNK), CHUNK) for ci in range(HG_WIDE)]
        ins = [(q_s[rw, ls], k_s[rw, ls], v_s[rw, ls], lf_s[rw, ls], bc_s[rw, ls]) for rw in rows for ls in lanes]
        prep = _hgrn_chunk_prepare(ins)
        states = [st_ref[h] for h in range(nhead)]
        for ci, rw in enumerate(rows):
            outs = _hgrn_chunk_apply(prep[ci * nhead:(ci + 1) * nhead], states)
            states = [st_new for _, st_new in outs]
            for (o, _), ls in zip(outs, lanes):
                o_s[rw, ls] = o
        for h in range(nhead):
            st_ref[h] = states[h]
        return carry

    lax.fori_loop(0, nt // (CHUNK * HG_WIDE), group_body, 0)

    o = o_s[...]
    ms = _dot(o * o, gsum_ref[...]) * (1.0 / LANES)
    og = u[:, 3 * dim:4 * dim]
    yb_ref[0] = (o * lax.rsqrt(ms + RMS_EPS) * nw_ref[...] * _sigmoid(og)).astype(BF16)


def _hgrn_branch(proj_hg, lb_logits, norm_w, layer):
    b, t, cols = proj_hg.shape
    dim = cols // 4
    tb = TB_HGRN
    ii = jnp.arange(tb)
    tri = ((ii[:, None] // CHUNK == ii[None, :] // CHUNK) & (ii[:, None] >= ii[None, :])).astype(BF16)
    jj = jnp.arange(dim)
    gsum = (jj[:, None] // LANES == jj[None, :] // LANES).astype(BF16)
    const = lambda a_: pl.BlockSpec(a_.shape, lambda bi, ti: (0,) * a_.ndim)
    args = [lb_logits, norm_w.reshape(1, -1), tri, gsum]
    sc = lambda: pltpu.VMEM((tb, dim), F32)
    return pl.pallas_call(
        functools.partial(_hgrn_kernel, layer),
        grid=(b, t // tb),
        in_specs=[pl.BlockSpec((1, tb, cols), lambda bi, ti: (bi, ti, 0))] + [const(a_) for a_ in args],
        out_specs=pl.BlockSpec((1, tb, dim), lambda bi, ti: (bi, ti, 0)),
        out_shape=jax.ShapeDtypeStruct((b, t, dim), BF16),
        scratch_shapes=[pltpu.VMEM((dim // LANES, LANES, LANES), F32)] + [sc() for _ in range(6)],
        compiler_params=_params(("arbitrary", "arbitrary")),
    )(proj_hg, *args)


def _merge_kernel(alpha, x_ref, ya_ref, yb_ref, pg_ref, wa_ref, wb_ref, wo_ref, g_ref, b_ref, wr_ref,
                  x1_ref, lt_ref):
    d = x_ref.shape[1]
    gates = _sigmoid(pg_ref[...])
    merged = gates[:, :d] * _dg(ya_ref[...], wa_ref[...]) + gates[:, d:] * _dg(yb_ref[...], wb_ref[...])
    h = alpha * x_ref[...] + _dot(merged, wo_ref[...])
    x1 = _layer_norm(h, g_ref[...], b_ref[...])
    x1_ref[...] = x1
    lt_ref[...] = _dot3(wr_ref[...], x1, NT)


def _merge(xf, ya, yb, pgate, wa, wb, wo, g, bta, wr, alpha):
    n, d = xf.shape
    tm = TM_MERGE
    tile = lambda a_: pl.BlockSpec((tm, a_.shape[1]), lambda i: (i, 0))
    const = lambda a_: pl.BlockSpec(a_.shape, lambda i: (0, 0))
    return pl.pallas_call(
        functools.partial(_merge_kernel, alpha),
        grid=(n // tm,),
        in_specs=[tile(xf), tile(ya), tile(yb), tile(pgate), const(wa), const(wb), const(wo), const(g),
                  const(bta), const(wr)],
        out_specs=[pl.BlockSpec((tm, d), lambda i: (i, 0)), pl.BlockSpec((LANES, tm), lambda i: (0, i))],
        out_shape=[jax.ShapeDtypeStruct((n, d), F32), jax.ShapeDtypeStruct((LANES, n), F32)],
        compiler_params=_params(("parallel",)),
    )(xf, ya, yb, pgate, wa, wb, wo, g, bta, wr)


ROUTER_EXPERT_ROW = 8


def _to_lanes(col, nl):
    ne = col.shape[0]
    diag = lax.broadcasted_iota(I32, (ne, nl), 0) == lax.broadcasted_iota(I32, (ne, nl), 1)
    return jnp.sum(jnp.where(diag, col, 0.0), axis=0, keepdims=True)


def _route_kernel(lt_ref, bias_ref, upper_ref, lower_ref, route_ref, seg_ref, cnt_ref, carry_ref):
    @pl.when(pl.program_id(0) == 0)
    def _():
        carry_ref[...] = jnp.zeros_like(carry_ref)

    ne = N_GROUPS * EXPERTS_PER_GROUP
    lt = lt_ref[...] + bias_ref[...]
    nb = lt.shape[1]
    neg = -jnp.inf
    lg = lt[0:8]
    rg = lax.broadcasted_iota(I32, (8, nb), 0).astype(F32)
    lg = jnp.where(rg < N_GROUPS, lg, neg)
    mg = jnp.max(lg, axis=0, keepdims=True)
    gidx = jnp.min(jnp.where(lg == mg, rg, 1e9), axis=0, keepdims=True)
    pg_sel = 1.0 / jnp.sum(jnp.exp(lg - mg), axis=0, keepdims=True)

    le = lt[ROUTER_EXPERT_ROW:ROUTER_EXPERT_ROW + ne]
    re = lax.broadcasted_iota(I32, (ne, nb), 0).astype(F32)
    in_group = jnp.floor(re * (1.0 / EXPERTS_PER_GROUP)) == gidx
    l1 = jnp.where(in_group, le, neg)
    m1 = jnp.max(l1, axis=0, keepdims=True)
    i1 = jnp.min(jnp.where(l1 == m1, re, 1e9), axis=0, keepdims=True)
    l2 = jnp.where(re == i1, neg, l1)
    m2 = jnp.max(l2, axis=0, keepdims=True)
    i2 = jnp.min(jnp.where(l2 == m2, re, 1e9), axis=0, keepdims=True)
    e2 = jnp.exp(m2 - m1)
    w1 = pg_sel / (1.0 + e2)
    w2 = pg_sel * e2 / (1.0 + e2)

    sel1 = re == i1
    sel2 = re == i2
    onehot = jnp.where(sel1 | sel2, 1.0, 0.0)
    before = _dg(onehot.astype(BF16), upper_ref[...])
    cnt_t = jnp.sum(onehot, axis=1, keepdims=True)
    seg = jnp.floor((cnt_t + (SEG_ALIGN - 1)) * (1.0 / SEG_ALIGN)) * SEG_ALIGN
    lstart = _dg(lower_ref[...], jnp.broadcast_to(seg, (ne, LANES)).astype(BF16))[:, 0:1]
    tot = lstart + before
    lpos1 = jnp.sum(jnp.where(sel1, tot, 0.0), axis=0, keepdims=True)
    lpos2 = jnp.sum(jnp.where(sel2, tot, 0.0), axis=0, keepdims=True)
    grel = carry_ref[...]
    carry = grel + seg
    carry_ref[...] = carry
    cnt_ref[...] = jnp.broadcast_to(carry, cnt_ref.shape)
    zero = jnp.zeros_like(w1)
    route_ref[...] = jnp.concatenate([i1, i2, lpos1, lpos2, w1, w2, zero, zero], axis=0)
    nl = seg_ref.shape[2]
    zl = jnp.zeros((1, nl), F32)
    ltot = jnp.broadcast_to(jnp.sum(seg, axis=0, keepdims=True), (1, nl))
    seg_ref[0] = jnp.concatenate([_to_lanes(seg, nl), _to_lanes(lstart, nl), _to_lanes(grel, nl), ltot,
                                  zl, zl, zl, zl], axis=0)


def _route(lt, bias_col):
    n = lt.shape[1]
    tb = TB_ROUTE
    ne = N_GROUPS * EXPERTS_PER_GROUP
    ii = jnp.arange(tb)
    upper = (ii[:, None] < ii[None, :]).astype(BF16)
    ee = jnp.arange(ne)
    lower = (ee[:, None] > ee[None, :]).astype(BF16)
    return pl.pallas_call(
        _route_kernel,
        grid=(n // tb,),
        in_specs=[pl.BlockSpec((LANES, tb), lambda i: (0, i)),
                  pl.BlockSpec((LANES, 1), lambda i: (0, 0)),
                  pl.BlockSpec((tb, tb), lambda i: (0, 0)),
                  pl.BlockSpec((ne, ne), lambda i: (0, 0))],
        out_specs=[pl.BlockSpec((8, tb), lambda i: (0, i)), pl.BlockSpec((1, 8, LANES), lambda i: (i, 0, 0)),
                   pl.BlockSpec((ne, LANES), lambda i: (0, 0))],
        out_shape=[jax.ShapeDtypeStruct((8, n), F32), jax.ShapeDtypeStruct((n // tb, 8, LANES), F32),
                   jax.ShapeDtypeStruct((ne, LANES), F32)],
        scratch_shapes=[pltpu.VMEM((ne, 1), F32)],
        compiler_params=_params(("arbitrary",)),
    )(lt, bias_col, upper, lower)


TAB_LANES = LANES


def _finalize_kernel(tm, seg_ref, cnt_ref, lower_ref, segtab_ref, tab_ref):
    ne = cnt_ref.shape[0]
    cnt = cnt_ref[...]
    nb = jnp.floor((cnt + (tm - 1)) * (1.0 / tm))
    bstart = _dg(lower_ref[...], nb.astype(BF16))
    bend = bstart + nb
    pad_start = bstart[:, 0:1] * tm
    seg = seg_ref[0]
    gstart = seg[2:3] + _to_lanes(pad_start, seg.shape[1])
    segtab_ref[0] = jnp.concatenate([seg[0:2], gstart, seg[3:8]], axis=0).astype(I32)

    nl = tab_ref.shape[1]
    n_used = jnp.max(bend[:, 0:1], axis=0, keepdims=True)
    pad_lo = _to_lanes(pad_start + cnt[:, 0:1], nl)
    pad_hi = _to_lanes(bend[:, 0:1] * tm, nl)
    zero = jnp.zeros((1, nl), F32)
    tab_ref[...] = jnp.concatenate([_to_lanes(bstart[:, 0:1], nl), jnp.broadcast_to(n_used, (1, nl)), pad_lo,
                                    pad_hi, _to_lanes(nb[:, 0:1], nl), zero, zero, zero], axis=0).astype(I32)


def _finalize(seg, cnt, tm):
    ntile = seg.shape[0]
    ne = cnt.shape[0]
    ii = jnp.arange(ne)
    lower = (ii[:, None] > ii[None, :]).astype(BF16)
    return pl.pallas_call(
        functools.partial(_finalize_kernel, tm),
        grid=(ntile,),
        in_specs=[pl.BlockSpec((1, 8, LANES), lambda i: (i, 0, 0)), pl.BlockSpec(cnt.shape, lambda i: (0, 0)),
                  pl.BlockSpec((ne, ne), lambda i: (0, 0))],
        out_specs=[pl.BlockSpec((1, 8, LANES), lambda i: (i, 0, 0)), pl.BlockSpec((8, TAB_LANES), lambda i: (0, 0))],
        out_shape=[jax.ShapeDtypeStruct((ntile, 8, LANES), I32), jax.ShapeDtypeStruct((8, TAB_LANES), I32)],
        compiler_params=_params(("arbitrary",)),
    )(seg, cnt, lower)


def _for_each_piece(length, max_len, fn):
    size = SEG_ALIGN
    sizes = []
    while size <= max_len:
        sizes.append(size)
        size *= 2
    for size in reversed(sizes):
        @pl.when(jnp.bitwise_and(length, size) != 0)
        def _(size=size):
            fn(pl.multiple_of(jnp.bitwise_and(length, -2 * size), SEG_ALIGN), size)


def _sorted_rows(td):
    return 2 * td + N_GROUPS * EXPERTS_PER_GROUP * SEG_ALIGN


def _dispatch_kernel(tm, seglen_ref, lstart_ref, gstart_ref, ltot_ref, plo_ref, phi_ref, nu_ref,
                     x_ref, route_ref, xbuf_ref, sorted_ref, zblk, sem, zsem):
    i = pl.program_id(0)
    nsteps = pl.num_programs(0)
    td = x_ref.shape[0]
    ne = plo_ref.shape[0]
    nblk = xbuf_ref.shape[0] // tm
    nrow = sorted_ref.shape[1]
    buf = i % 2

    lpos = route_ref[2:4, :]
    r = lax.broadcasted_iota(I32, (nrow, td), 0).astype(F32)
    onehot = jnp.where((r == lpos[0:1]) | (r == lpos[1:2]), 1.0, 0.0).astype(BF16)
    sorted_ref[buf] = _dg(onehot, x_ref[...].astype(BF16)).astype(BF16)

    def wait_tile(step, b):
        _for_each_piece(ltot_ref[step], nrow, lambda off, size: pltpu.make_async_copy(
            sorted_ref.at[b, pl.ds(0, size), :], xbuf_ref.at[pl.ds(0, size), :], sem.at[b]).wait())

    for e in range(ne):
        idx = i * ne + e
        ls = pl.multiple_of(lstart_ref[idx], SEG_ALIGN)
        gs = pl.multiple_of(gstart_ref[idx], SEG_ALIGN)
        _for_each_piece(seglen_ref[idx], td, lambda off, size: pltpu.make_async_copy(
            sorted_ref.at[buf, pl.ds(ls + off, size), :], xbuf_ref.at[pl.ds(gs + off, size), :],
            sem.at[buf]).start())

    @pl.when(i > 0)
    def _():
        wait_tile(i - 1, 1 - buf)

    @pl.when(i == nsteps - 1)
    def _():
        wait_tile(i, buf)

    def pad_fill(fn):
        for e in range(ne):
            lo = pl.multiple_of(plo_ref[e], SEG_ALIGN)
            _for_each_piece(phi_ref[e] - lo, tm // 2, lambda off, size: fn(pltpu.make_async_copy(
                zblk.at[pl.ds(0, size), :], xbuf_ref.at[pl.ds(lo + off, size), :], zsem)))

        def per_blk(b, carry):
            fn(pltpu.make_async_copy(zblk, xbuf_ref.at[pl.ds(pl.multiple_of(b * tm, tm), tm), :], zsem))
            return carry
        lax.fori_loop(nu_ref[0], nblk, per_blk, 0)

    @pl.when(i == 0)
    def _():
        zblk[...] = jnp.zeros_like(zblk)
        pad_fill(lambda cp: cp.start())
        pad_fill(lambda cp: cp.wait())


def _dispatch(seglen, lstart, gstart, ltot, pad_lo, pad_hi, n_used, x1, route, rows, tm):
    n, d = x1.shape
    td = TD_DISPATCH
    return pl.pallas_call(
        functools.partial(_dispatch_kernel, tm),
        grid_spec=pltpu.PrefetchScalarGridSpec(
            num_scalar_prefetch=7,
            grid=(n // td,),
            in_specs=[pl.BlockSpec((td, d), lambda i, *_: (i, 0)), pl.BlockSpec((8, td), lambda i, *_: (0, i))],
            out_specs=pl.BlockSpec(memory_space=pl.ANY),
            scratch_shapes=[pltpu.VMEM((2, _sorted_rows(td), d), BF16), pltpu.VMEM((tm, d), BF16),
                            pltpu.SemaphoreType.DMA((2,)), pltpu.SemaphoreType.DMA(())],
        ),
        out_shape=jax.ShapeDtypeStruct((rows, d), BF16),
        compiler_params=_params(("arbitrary",)),
    )(seglen, lstart, gstart, ltot, pad_lo, pad_hi, n_used, x1, route)


def _expert_kernel(tm, bstart_ref, nb_ref, nu_ref, w1_ref, w3_ref, w2_ref, xbuf_ref, ybuf_ref,
                   w13b, w2b, xb, yb, semx, semy):
    e = pl.program_id(0)
    nb = nb_ref[e]
    b0 = bstart_ref[e]
    nblk = ybuf_ref.shape[0] // tm
    rows = lambda blk: pl.ds(pl.multiple_of(blk * tm, tm), tm)
    x_copy = lambda j, slot: pltpu.make_async_copy(xbuf_ref.at[rows(b0 + j), :], xb.at[slot], semx.at[slot])
    y_copy = lambda blk, slot: pltpu.make_async_copy(yb.at[slot], ybuf_ref.at[rows(blk), :], semy.at[slot])

    @pl.when(nb > 0)
    def _():
        x_copy(0, 0).start(priority=ROW_DMA_PRIORITY)

    de = w2b.shape[0]
    w13b[:, :de] = w1_ref[0].astype(BF16)
    w13b[:, de:] = w3_ref[0].astype(BF16)
    w2b[...] = w2_ref[0].astype(BF16)

    def body(j, carry):
        slot = j % 2
        x_copy(j, slot).wait()

        @pl.when(j + 1 < nb)
        def _():
            x_copy(j + 1, 1 - slot).start(priority=ROW_DMA_PRIORITY)

        @pl.when(j >= 2)
        def _():
            y_copy(b0 + j - 2, slot).wait()

        x = xb[slot]
        h13 = _dg(x, w13b[...])
        h1 = h13[:, :de]
        h = (h1 * _sigmoid(h1)) * h13[:, de:]
        yb[slot] = _dot(h, w2b[...]).astype(BF16)
        y_copy(b0 + j, slot).start(priority=ROW_DMA_PRIORITY)
        return carry

    lax.fori_loop(0, nb, body, 0)

    @pl.when(nb >= 2)
    def _():
        y_copy(b0 + nb - 2, nb % 2).wait()

    @pl.when(nb >= 1)
    def _():
        y_copy(b0 + nb - 1, (nb - 1) % 2).wait()

    @pl.when(e == pl.num_programs(0) - 1)
    def _():
        yb[0] = jnp.zeros(yb.shape[1:], yb.dtype)

        def fill(fn):
            def per_blk(blk, carry):
                fn(y_copy(blk, 0))
                return carry
            lax.fori_loop(nu_ref[0], nblk, per_blk, 0)

        fill(lambda cp: cp.start())
        fill(lambda cp: cp.wait())


def _experts(bstart, nb, n_used, xbuf, w1, w3, w2):
    rows, d = xbuf.shape
    ne, _, de = w1.shape
    tm = TM_EXPERT
    return pl.pallas_call(
        functools.partial(_expert_kernel, tm),
        grid_spec=pltpu.PrefetchScalarGridSpec(
            num_scalar_prefetch=3,
            grid=(ne,),
            in_specs=[pl.BlockSpec((1, d, de), lambda e, *_: (e, 0, 0)),
                      pl.BlockSpec((1, d, de), lambda e, *_: (e, 0, 0)),
                      pl.BlockSpec((1, de, d), lambda e, *_: (e, 0, 0)),
                      pl.BlockSpec(memory_space=pl.ANY)],
            out_specs=pl.BlockSpec(memory_space=pl.ANY),
            scratch_shapes=[pltpu.VMEM((d, 2 * de), BF16), pltpu.VMEM((de, d), BF16),
                            pltpu.VMEM((2, tm, d), BF16), pltpu.VMEM((2, tm, d), BF16),
                            pltpu.SemaphoreType.DMA((2,)), pltpu.SemaphoreType.DMA((2,))],
        ),
        out_shape=jax.ShapeDtypeStruct((rows, d), BF16),
        compiler_params=_params(("arbitrary",)),
    )(bstart, nb, n_used, w1, w3, w2, xbuf)


def _combine_kernel(alpha, seglen_ref, lstart_ref, gstart_ref, ltot_ref, x1_ref, rt_ref, p_ref, wpe_ref, wpg_ref,
                    g_ref, b_ref, ybuf_ref, out_ref, sorted_ref, sem):
    i = pl.program_id(0)
    nsteps = pl.num_programs(0)
    tc = x1_ref.shape[0]
    ne = N_GROUPS * EXPERTS_PER_GROUP
    nrow = sorted_ref.shape[1]

    def fetch(step, buf):
        for e in range(ne):
            idx = step * ne + e
            ls = pl.multiple_of(lstart_ref[idx], SEG_ALIGN)
            gs = pl.multiple_of(gstart_ref[idx], SEG_ALIGN)
            _for_each_piece(seglen_ref[idx], tc, lambda off, size: pltpu.make_async_copy(
                ybuf_ref.at[pl.ds(gs + off, size), :], sorted_ref.at[buf, pl.ds(ls + off, size), :],
                sem.at[buf]).start())

    @pl.when(i == 0)
    def _():
        sorted_ref[...] = jnp.zeros_like(sorted_ref)
        fetch(0, 0)

    @pl.when(i + 1 < nsteps)
    def _():
        fetch(i + 1, (i + 1) % 2)

    cur = i % 2
    _for_each_piece(ltot_ref[i], nrow, lambda off, size: pltpu.make_async_copy(
        ybuf_ref.at[pl.ds(0, size), :], sorted_ref.at[cur, pl.ds(0, size), :], sem.at[cur]).wait())

    rt = rt_ref[...]
    r = lax.broadcasted_iota(I32, (tc, nrow), 1).astype(F32)
    unsort = jnp.where(r == rt[:, 0:1], rt[:, 2:3], 0.0) + jnp.where(r == rt[:, 1:2], rt[:, 3:4], 0.0)
    ffn = _dg(unsort.astype(BF16), sorted_ref[cur])
    x2 = _layer_norm(alpha * x1_ref[...] + ffn, g_ref[...], b_ref[...])
    gate = _sigmoid(_dot(x2, wpg_ref[...]))
    out_ref[...] = x2 + gate * _dot(p_ref[...], wpe_ref[...])


def _combine(seglen, lstart, gstart, ltot, x1, route_t, pf, wpe, wpg, g, bta, ybuf, alpha):
    n, d = x1.shape
    tc = TD_DISPATCH
    tile = lambda a_: pl.BlockSpec((tc, a_.shape[1]), lambda i, *_: (i, 0))
    const = lambda a_: pl.BlockSpec(a_.shape, lambda i, *_: (0, 0))
    return pl.pallas_call(
        functools.partial(_combine_kernel, alpha),
        grid_spec=pltpu.PrefetchScalarGridSpec(
            num_scalar_prefetch=4,
            grid=(n // tc,),
            in_specs=[tile(x1), tile(route_t), tile(pf), const(wpe), const(wpg), const(g), const(bta),
                      pl.BlockSpec(memory_space=pl.ANY)],
            out_specs=pl.BlockSpec((tc, d), lambda i, *_: (i, 0)),
            scratch_shapes=[pltpu.VMEM((2, _sorted_rows(tc), d), BF16), pltpu.SemaphoreType.DMA((2,))],
        ),
        out_shape=jax.ShapeDtypeStruct((n, d), F32),
        compiler_params=_params(("arbitrary",)),
    )(seglen, lstart, gstart, ltot, x1, route_t, pf, wpe, wpg, g, bta, ybuf)


def _layer(x, p_i, w_in, rw_mu, rw_w0, rw_w_up, rw_a0, rw_a_up, rw_g_up, rw_k_k, rw_k_a, rw_r_k, rw_gn_w,
           rw_gn_b, w_a_out, hg_lb_logits, hg_norm_w, w_b_out, w_o, ln1_g, ln1_b, router_g_w, router_g_b,
           router_e_w, router_e_b, w1, w3, w2, ln2_g, ln2_b, w_pe, w_pg, alpha, layer):
    b, t, d = x.shape
    n = b * t
    rw_dim = rw_w0.shape[0]
    rw_cols = 3 * rw_dim + RW_DECAY_LORA + RW_A_LORA + RW_GATE_LORA
    hg_cols = 4 * hg_norm_w.shape[0]
    ne = N_GROUPS * EXPERTS_PER_GROUP
    row2 = lambda a_: a_.reshape(1, -1)
    xf = x.reshape(n, d)

    wb = w_in.astype(BF16)
    proj_rw, proj_hg, proj_gt = _project(xf, wb[:, :rw_cols], wb[:, rw_cols:rw_cols + hg_cols],
                                         wb[:, rw_cols + hg_cols:])
    ya = _rwkv_branch(proj_rw.reshape(b, t, rw_cols), rw_mu, row2(rw_w0), rw_w_up, rw_a0, rw_a_up, rw_g_up,
                      rw_k_k, rw_k_a, rw_r_k, rw_gn_w, rw_gn_b)
    yb = _hgrn_branch(proj_hg.reshape(b, t, hg_cols), hg_lb_logits, hg_norm_w, layer)

    wr = jnp.zeros((LANES, d), F32)
    wr = wr.at[:N_GROUPS].set(router_g_w.T).at[ROUTER_EXPERT_ROW:ROUTER_EXPERT_ROW + ne].set(router_e_w.T)
    bias = jnp.zeros((LANES,), F32)
    bias = bias.at[:N_GROUPS].set(router_g_b).at[ROUTER_EXPERT_ROW:ROUTER_EXPERT_ROW + ne].set(router_e_b)
    x1, lt = _merge(xf, ya.reshape(n, -1), yb.reshape(n, -1), proj_gt, w_a_out.astype(BF16),
                    w_b_out.astype(BF16), w_o.astype(BF16), row2(ln1_g), row2(ln1_b), wr, alpha)
    route, seg, cnt = _route(lt, bias.reshape(LANES, 1))

    tm = TM_EXPERT
    ntile = n // TD_DISPATCH
    nblk = -(-(2 * n + (SEG_ALIGN - 1) * ne * ntile) // tm) + ne
    assert TB_ROUTE == TD_DISPATCH
    segtab, tab = _finalize(seg, cnt, tm)
    per_seg = lambda row: segtab[:, row, :ne].reshape(-1)
    seglen, lstart, gstart, ltot = per_seg(0), per_seg(1), per_seg(2), segtab[:, 3, 0]
    n_used = tab[1, :1]

    xbuf = _dispatch(seglen, lstart, gstart, ltot, tab[2, :ne], tab[3, :ne], n_used, x1, route, nblk * tm, tm)
    ybuf = _experts(tab[0, :ne], tab[4, :ne], n_used, xbuf, w1, w3, w2)
    out = _combine(seglen, lstart, gstart, ltot, x1, route[2:6].T, p_i.reshape(n, -1), w_pe.astype(BF16),
                   w_pg.astype(BF16), row2(ln2_g), row2(ln2_b), ybuf, alpha)
    return out.reshape(b, t, d)


def kernel(x, p, w_in, rw_mu, rw_w0, rw_w_up, rw_a0, rw_a_up, rw_g_up, rw_k_k, rw_k_a, rw_r_k, rw_gn_w, rw_gn_b,
           w_a_out, hg_lb_logits, hg_norm_w, w_b_out, w_o, ln1_g, ln1_b, router_g_w, router_g_b, router_e_w,
           router_e_b, w1, w3, w2, ln2_g, ln2_b, w_pe, w_pg):
    depth = w_in.shape[0]
    alpha = (2 * depth) ** 0.25
    for i in range(depth):
        x = _layer(x, p[i], w_in[i], rw_mu[i], rw_w0[i], rw_w_up[i], rw_a0[i], rw_a_up[i], rw_g_up[i], rw_k_k[i],
                   rw_k_a[i], rw_r_k[i].reshape(-1), rw_gn_w[i], rw_gn_b[i], w_a_out[i], hg_lb_logits,
                   hg_norm_w[i], w_b_out[i], w_o[i], ln1_g[i], ln1_b[i],
                   router_g_w[i], router_g_b[i], router_e_w[i], router_e_b[i], w1[i], w3[i], w2[i], ln2_g[i],
                   ln2_b[i], w_pe[i], w_pg[i], alpha, i)
    return x
```

```python
import functools

import jax
import jax.numpy as jnp
from jax import lax
from jax.experimental import pallas as pl
from jax.experimental.pallas import tpu as pltpu

F32 = jnp.float32
BF16 = jnp.bfloat16
I32 = jnp.int32

NN = (((1,), (0,)), ((), ()))
NT = (((1,), (1,)), ((), ()))

RW_HEAD = 64
RW_DECAY_LORA = 64
RW_A_LORA = 64
RW_GATE_LORA = 128
RW_GN_EPS = 64e-5
RW_DECAY_SCALE = 0.6065306597126334
HG_HEADS = 4
N_GROUPS = 4
EXPERTS_PER_GROUP = 8
LN_EPS = 1e-5
RMS_EPS = 1e-6

CHUNK = 64
SUB = 16
LANES = 128
VMEM_LIMIT = 56 * 1024 * 1024

TM_PROJ = 256
TB_RWKV = 512
TB_HGRN = 256
TM_MERGE = 512
TB_ROUTE = 512
TD_DISPATCH = 512
SEG_ALIGN = 16
TM_EXPERT = 512
ROW_DMA_PRIORITY = 1
RW_WIDE = 4
HG_WIDE = 4


def _dg(a, b, dn=NN):
    return lax.dot_general(a, b, dn, preferred_element_type=F32)


def _dot(a, b, dn=NN):
    return _dg(a.astype(BF16), b.astype(BF16), dn)


def _split(a):
    hi = a.astype(BF16)
    lo = (a - hi.astype(F32)).astype(BF16)
    return hi, lo


def _dot_hl(a, b_exact, dn=NN):
    hi, lo = _split(a)
    return _dg(hi, b_exact, dn) + _dg(lo, b_exact, dn)


def _dot3(a, b, dn=NN):
    ah, al = _split(a)
    bh, bl = _split(b)
    return _dg(ah, bh, dn) + (_dg(ah, bl, dn) + _dg(al, bh, dn))


def _cumsum_chunks(x, tri):
    h = x.astype(BF16)
    r1 = x - h.astype(F32)
    m = r1.astype(BF16)
    l = (r1 - m.astype(F32)).astype(BF16)
    return _dg(tri, h) + (_dg(tri, m) + _dg(tri, l))


def _sigmoid(x):
    return 0.5 * jnp.tanh(0.5 * x) + 0.5


def _layer_norm(h, g, b):
    mu = jnp.mean(h, axis=-1, keepdims=True)
    d = h - mu
    var = jnp.mean(d * d, axis=-1, keepdims=True)
    return d * lax.rsqrt(var + LN_EPS) * g + b


def _params(sem):
    return pltpu.CompilerParams(dimension_semantics=sem, vmem_limit_bytes=VMEM_LIMIT)


def _proj_kernel(x_ref, wr_ref, wh_ref, wg_ref, pr_ref, ph_ref, pg_ref):
    xb = x_ref[...].astype(BF16)
    pr_ref[...] = _dg(xb, wr_ref[...])
    ph_ref[...] = _dg(xb, wh_ref[...])
    pg_ref[...] = _dg(xb, wg_ref[...])


def _project(xf, w_rw, w_hg, w_gt):
    n, d = xf.shape
    tm = TM_PROJ
    full = lambda w: pl.BlockSpec(w.shape, lambda i: (0, 0))
    tile = lambda c: pl.BlockSpec((tm, c), lambda i: (i, 0))
    return pl.pallas_call(
        _proj_kernel,
        grid=(n // tm,),
        in_specs=[tile(d), full(w_rw), full(w_hg), full(w_gt)],
        out_specs=[tile(w_rw.shape[1]), tile(w_hg.shape[1]), tile(w_gt.shape[1])],
        out_shape=[jax.ShapeDtypeStruct((n, w.shape[1]), F32) for w in (w_rw, w_hg, w_gt)],
        compiler_params=_params(("parallel",)),
    )(xf, w_rw, w_hg, w_gt)


def _each(f, *ls):
    return [f(*xs) for xs in zip(*ls)]


def _two(x):
    m1 = lax.broadcasted_iota(I32, x.shape, 1) < RW_HEAD
    return jnp.concatenate([jnp.where(m1, x, 0.0), jnp.where(m1, 0.0, x)], axis=0)


def _rwkv_chunk_prepare(ins, lvl_ref, out):
    c = CHUNK
    lane = lax.broadcasted_iota(I32, (c, LANES), 1)
    row = lax.broadcasted_iota(I32, (c, LANES), 0)
    scol = jnp.bitwise_and(lane, RW_HEAD - 1)
    strict = row > scol
    incl = row >= scol
    r2 = lax.broadcasted_iota(I32, (LANES, LANES), 0)
    c2 = lax.broadcasted_iota(I32, (LANES, LANES), 1)
    eye = jnp.where(r2 == c2, 1.0, 0.0).astype(F32)

    def prep(r, k, v, av, bv, lw, lc):
        l_end = lc[c - 1:c]
        e_r = jnp.exp(l_end - lc)
        e_n = jnp.exp(-lc)
        return dict(at=av * jnp.exp(lc - lw), rt=r * jnp.exp(lc), bt=bv * e_n, kt=k * e_n,
                    bk=jnp.concatenate([bv * e_r, k * e_r], axis=0), pc=jnp.exp(l_end), v=v)

    q = [prep(*xs) for xs in ins]
    yield
    p = [_dot(jnp.concatenate([d["at"], d["rt"]], axis=0),
              jnp.concatenate([_two(d["bt"]), _two(d["kt"])], axis=0), NT) for d in q]
    sab = [jnp.where(strict, x[:c, :LANES], 0.0) for x in p]
    sak = [jnp.where(strict, x[:c, LANES:], 0.0) for x in p]
    srb = [jnp.where(incl, x[c:, :LANES], 0.0) for x in p]
    srk = [jnp.where(incl, x[c:, LANES:], 0.0) for x in p]
    yield
    sv = _each(lambda ak, rk, d: _dot(jnp.concatenate([ak, rk], axis=0), _two(d["v"])), sak, srk, q)
    yield

    a_bd = [_two(x) for x in sab]
    t = [eye + a * lvl_ref[0] for a in a_bd]
    for lv in range(1, lvl_ref.shape[0]):
        ta = _each(lambda t_, a: _dot(t_, a * lvl_ref[lv]), t, a_bd)
        yield
        t = _each(lambda t_, ta_: t_ + _dot(ta_, t_), t, ta)
        yield

    x = _each(lambda t_, d, sv_: _dot(t_, jnp.concatenate([_two(d["at"]), _two(sv_[:c])], axis=1)), t, q, sv)
    out.extend(dict(uk=x_[:c, :LANES] + x_[c:, :LANES],
                    w=x_[:c, LANES:] + x_[c:, LANES:],
                    rt=d["rt"], rkv=sv_[c:], srb=srb_, bk=d["bk"], v=d["v"], pc=d["pc"])
               for x_, d, sv_, srb_ in zip(x, q, sv, srb))


def _rwkv_chunk_apply(prep, states, bd, out):
    c = CHUNK
    g1 = _each(lambda d, s: _dot(jnp.concatenate([d["uk"], d["rt"]], axis=0), s, NT), prep, states)
    yield
    u = _each(lambda g, d: g[:c] + d["w"], g1, prep)
    y = _each(lambda g, d, u_: g[c:] + d["rkv"] + _dot(d["srb"], _two(u_)), g1, prep, u)
    yield
    upd = _each(lambda u_, d: _dot(jnp.concatenate([u_, d["v"]], axis=0).T, d["bk"]), u, prep)
    s_new = _each(lambda d, s, up: s * d["pc"] + bd * up, prep, states, upd)
    out.extend(zip(y, s_new))
    yield


def _interleave(*gens):
    live = [g for g in gens if g is not None]
    while live:
        for g in list(live):
            try:
                next(g)
            except StopIteration:
                live.remove(g)


def _rwkv_kernel(u_ref, mu_ref, w0_ref, wup_ref, a0_ref, aup_ref, gup_ref, kk_ref, ka_ref, rk_ref,
                 gnw_ref, gnb_ref, tri_ref, gsum_ref, lvl_ref, bd_ref, ya_ref,
                 s_ref, prev_ref, r_s, k_s, v_s, a_s, b_s, lw_s, lc_s, g_s, y_s):
    tb = pl.program_id(1)

    @pl.when(tb == 0)
    def _():
        s_ref[...] = jnp.zeros_like(s_ref)
        prev_ref[...] = jnp.zeros_like(prev_ref)

    nt = u_ref.shape[1]
    dim = r_s.shape[1]
    part = CHUNK * RW_WIDE
    npair = dim // LANES
    lanes = [slice(p * LANES, (p + 1) * LANES) for p in range(npair)]
    bd = bd_ref[...]
    gsum = gsum_ref[...]

    def prologue(h):
        rows = slice(h * part, (h + 1) * part)
        u = u_ref[0, rows, :]
        before = prev_ref[...] if h == 0 else u_ref[0, h * part - 1:h * part, :]
        rowid = lax.broadcasted_iota(I32, u.shape, 0)
        shifted = jnp.where(rowid == 0, before, pltpu.roll(u, 1, axis=0))
        um = u + (shifted - u) * mu_ref[...]
        r = um[:, 0:dim]
        k = um[:, dim:2 * dim]
        v = um[:, 2 * dim:3 * dim]
        xwa = um[:, 3 * dim:3 * dim + LANES]
        xg = um[:, 3 * dim + LANES:3 * dim + 2 * LANES]
        yield
        wpre = w0_ref[...] + _dot(jnp.tanh(xwa), wup_ref[...])
        lw = -RW_DECAY_SCALE * _sigmoid(wpre)
        a = _sigmoid(a0_ref[...] + _dot(xwa, aup_ref[...]))
        g_s[rows, :] = _dot(_sigmoid(xg), gup_ref[...])
        yield
        kk = k * kk_ref[...]
        ss = _dot(kk * kk, gsum)
        kk = kk * lax.rsqrt(jnp.maximum(ss, 1e-24))
        r_s[rows, :] = r
        k_s[rows, :] = k * (1.0 + (a - 1.0) * ka_ref[...])
        v_s[rows, :] = v
        a_s[rows, :] = -kk
        b_s[rows, :] = kk * a
        yield
        lw_s[rows, :] = lw
        lc_s[rows, :] = _cumsum_chunks(lw, tri_ref[...])

    def chunk_rows(h):
        return [slice(h * part + ci * CHUNK, h * part + (ci + 1) * CHUNK) for ci in range(RW_WIDE)]

    def prepare(h, out):
        ins = [(r_s[rw, ls], k_s[rw, ls], v_s[rw, ls], a_s[rw, ls], b_s[rw, ls], lw_s[rw, ls], lc_s[rw, ls])
               for rw in chunk_rows(h) for ls in lanes]
        yield from _rwkv_chunk_prepare(ins, lvl_ref, out)

    def apply(h, prep):
        states = [s_ref[p] for p in range(npair)]
        for ci, rw in enumerate(chunk_rows(h)):
            outs = []
            yield from _rwkv_chunk_apply(prep[ci * npair:(ci + 1) * npair], states, bd, outs)
            states = [s_new for _, s_new in outs]
            for (y, _), ls in zip(outs, lanes):
                y_s[rw, ls] = y
        for p in range(npair):
            s_ref[p] = states[p]

    def epilogue(h):
        rows = slice(h * part, (h + 1) * part)
        y = y_s[rows, :]
        inv_n = 1.0 / RW_HEAD
        m = _dot_hl(y, gsum) * inv_n
        d = y - m
        yield
        var = _dot(d * d, gsum) * inv_n
        yn = d * lax.rsqrt(var + RW_GN_EPS) * gnw_ref[...] + gnb_ref[...]
        yield
        bonus = _dot(r_s[rows, :] * k_s[rows, :] * rk_ref[...], gsum) * v_s[rows, :]
        ya_ref[0, rows, :] = ((yn + bonus) * g_s[rows, :]).astype(BF16)

    nparts = nt // part
    preps = [[] for _ in range(nparts)]
    _interleave(prologue(0))
    prev_ref[...] = u_ref[0, nt - 1:nt, :]
    _interleave(prepare(0, preps[0]), prologue(1) if nparts > 1 else None)
    for h in range(nparts):
        _interleave(apply(h, preps[h]),
                    prepare(h + 1, preps[h + 1]) if h + 1 < nparts else None,
                    prologue(h + 2) if h + 2 < nparts else None,
                    epilogue(h - 1) if h > 0 else None)
    _interleave(epilogue(nparts - 1))


def _rwkv_branch(proj_rw, mu, w0, wup, a0, aup, gup, k_k, k_a, r_k, gn_w, gn_b):
    b, t, cols = proj_rw.shape
    dim = w0.shape[1]
    tb = TB_RWKV
    ii = jnp.arange(CHUNK * RW_WIDE)
    tri = ((ii[:, None] // CHUNK == ii[None, :] // CHUNK) & (ii[:, None] >= ii[None, :])).astype(BF16)
    jj = jnp.arange(dim)
    gsum = (jj[:, None] // RW_HEAD == jj[None, :] // RW_HEAD).astype(BF16)
    rr = jnp.arange(LANES)[:, None]
    cc = jnp.arange(LANES)[None, :]
    lvls = []
    s = 1
    while s < CHUNK:
        lvls.append(((rr // (2 * s) == cc // (2 * s)) & ((rr // s) % 2 == 1) & ((cc // s) % 2 == 0)).astype(F32))
        s *= 2
    lvl = jnp.stack(lvls)
    bd = (rr // RW_HEAD == cc // RW_HEAD).astype(F32)
    zpad = lambda rows: jnp.zeros((rows, dim), F32)
    wup_p = jnp.concatenate([wup, zpad(LANES - wup.shape[0])], axis=0).astype(BF16)
    aup_p = jnp.concatenate([zpad(LANES - aup.shape[0]), aup], axis=0).astype(BF16)
    row2 = lambda a_: a_.reshape(1, -1)
    const = lambda a_: pl.BlockSpec(a_.shape, lambda bi, ti: (0,) * a_.ndim)
    args = [row2(mu), row2(w0), wup_p, row2(a0), aup_p, gup.astype(BF16), row2(k_k), row2(k_a), row2(r_k),
            row2(gn_w), row2(gn_b), tri, gsum, lvl, bd]
    sc = lambda: pltpu.VMEM((tb, dim), F32)
    return pl.pallas_call(
        _rwkv_kernel,
        grid=(b, t // tb),
        in_specs=[pl.BlockSpec((1, tb, cols), lambda bi, ti: (bi, ti, 0))] + [const(a_) for a_ in args],
        out_specs=pl.BlockSpec((1, tb, dim), lambda bi, ti: (bi, ti, 0)),
        out_shape=jax.ShapeDtypeStruct((b, t, dim), BF16),
        scratch_shapes=[pltpu.VMEM((dim // LANES, LANES, LANES), F32), pltpu.VMEM((1, cols), F32)]
                       + [sc() for _ in range(9)],
        compiler_params=_params(("arbitrary", "arbitrary")),
    )(proj_rw, *args)


def _hgrn_chunk_prepare(ins):
    c = CHUNK
    subs = [(SUB * i, SUB * (i + 1)) for i in range(c // SUB)]

    def scores(q, k, lf, bc, lo, hi):
        m = bc[lo:lo + 1] - lf[lo:lo + 1]
        att = _dot(q[lo:hi] * jnp.exp(bc[lo:hi] - m), k[:hi] * jnp.exp(m - bc[:hi]), NT)
        tt = lax.broadcasted_iota(I32, (SUB, hi), 0) + lo
        s_ = lax.broadcasted_iota(I32, (SUB, hi), 1)
        return jnp.where(s_ <= tt, att, 0.0)

    att = [[scores(q, k, lf, bc, lo, hi) for lo, hi in subs] for q, k, v, lf, bc in ins]
    upd = [_dot(v.T, k * jnp.exp(bc[c - 1:c] - bc)) for q, k, v, lf, bc in ins]
    intra = [[_dot(a, x[2][:hi]) for a, (lo, hi) in zip(arow, subs)] for arow, x in zip(att, ins)]
    return [dict(intra=jnp.concatenate(rows, axis=0), qe=x[0] * jnp.exp(x[4]), upd=up, pc=jnp.exp(x[4][c - 1:c]))
            for rows, x, up in zip(intra, ins, upd)]


def _hgrn_chunk_apply(prep, states):
    o = _each(lambda d, st: d["intra"] + _dot(d["qe"], st, NT), prep, states)
    st_new = _each(lambda d, st: st * d["pc"] + d["upd"], prep, states)
    return list(zip(o, st_new))


def _hgrn_kernel(layer, u_ref, lbl_ref, nw_ref, tri_ref, gsum_ref, yb_ref,
                 st_ref, q_s, k_s, v_s, lf_s, bc_s, o_s):
    tb = pl.program_id(1)

    @pl.when(tb == 0)
    def _():
        st_ref[...] = jnp.zeros_like(st_ref)

    u = u_ref[0]
    nt = u.shape[0]
    dim = q_s.shape[1]
    lbl = lbl_ref[...]
    e = jnp.exp(lbl - jnp.max(lbl, axis=0, keepdims=True))
    lb = jnp.sum(e[0:layer + 1], axis=0, keepdims=True) / jnp.sum(e, axis=0, keepdims=True)
    zf = u[:, dim:2 * dim]
    sig = _sigmoid(zf)
    f = lb + (1.0 - lb) * sig
    qin = u[:, 0:dim]
    q_s[...] = qin * _sigmoid(qin)
    k_s[...] = (1.0 - lb) * (1.0 - sig)
    v_s[...] = u[:, 2 * dim:3 * dim]
    lf = jnp.log(f)
    lf_s[...] = lf
    bc_s[...] = _cumsum_chunks(lf, tri_ref[...])

    nhead = dim // LANES
    lanes = [slice(h * LANES, (h + 1) * LANES) for h in range(nhead)]

    def group_body(gi, carry):
        rows = [pl.ds(pl.multiple_of((gi * HG_WIDE + ci) * CHUNK, CHUNK), CHUNK) for ci in range(HG_WIDE)]
        ins = [(q_s[rw, ls], k_s[rw, ls], v_s[rw, ls], lf_s[rw, ls], bc_s[rw, ls]) for rw in rows for ls in lanes]
        prep = _hgrn_chunk_prepare(ins)
        states = [st_ref[h] for h in range(nhead)]
        for ci, rw in enumerate(rows):
            outs = _hgrn_chunk_apply(prep[ci * nhead:(ci + 1) * nhead], states)
            states = [st_new for _, st_new in outs]
            for (o, _), ls in zip(outs, lanes):
                o_s[rw, ls] = o
        for h in range(nhead):
            st_ref[h] = states[h]
        return carry

    lax.fori_loop(0, nt // (CHUNK * HG_WIDE), group_body, 0)

    o = o_s[...]
    ms = _dot(o * o, gsum_ref[...]) * (1.0 / LANES)
    og = u[:, 3 * dim:4 * dim]
    yb_ref[0] = (o * lax.rsqrt(ms + RMS_EPS) * nw_ref[...] * _sigmoid(og)).astype(BF16)


def _hgrn_branch(proj_hg, lb_logits, norm_w, layer):
    b, t, cols = proj_hg.shape
    dim = cols // 4
    tb = TB_HGRN
    ii = jnp.arange(tb)
    tri = ((ii[:, None] // CHUNK == ii[None, :] // CHUNK) & (ii[:, None] >= ii[None, :])).astype(BF16)
    jj = jnp.arange(dim)
    gsum = (jj[:, None] // LANES == jj[None, :] // LANES).astype(BF16)
    const = lambda a_: pl.BlockSpec(a_.shape, lambda bi, ti: (0,) * a_.ndim)
    args = [lb_logits, norm_w.reshape(1, -1), tri, gsum]
    sc = lambda: pltpu.VMEM((tb, dim), F32)
    return pl.pallas_call(
        functools.partial(_hgrn_kernel, layer),
        grid=(b, t // tb),
        in_specs=[pl.BlockSpec((1, tb, cols), lambda bi, ti: (bi, ti, 0))] + [const(a_) for a_ in args],
        out_specs=pl.BlockSpec((1, tb, dim), lambda bi, ti: (bi, ti, 0)),
        out_shape=jax.ShapeDtypeStruct((b, t, dim), BF16),
        scratch_shapes=[pltpu.VMEM((dim // LANES, LANES, LANES), F32)] + [sc() for _ in range(6)],
        compiler_params=_params(("arbitrary", "arbitrary")),
    )(proj_hg, *args)


def _merge_kernel(alpha, x_ref, ya_ref, yb_ref, pg_ref, wa_ref, wb_ref, wo_ref, g_ref, b_ref, wr_ref,
                  x1_ref, lt_ref):
    d = x_ref.shape[1]
    gates = _sigmoid(pg_ref[...])
    merged = gates[:, :d] * _dg(ya_ref[...], wa_ref[...]) + gates[:, d:] * _dg(yb_ref[...], wb_ref[...])
    h = alpha * x_ref[...] + _dot(merged, wo_ref[...])
    x1 = _layer_norm(h, g_ref[...], b_ref[...])
    x1_ref[...] = x1
    lt_ref[...] = _dot3(wr_ref[...], x1, NT)


def _merge(xf, ya, yb, pgate, wa, wb, wo, g, bta, wr, alpha):
    n, d = xf.shape
    tm = TM_MERGE
    tile = lambda a_: pl.BlockSpec((tm, a_.shape[1]), lambda i: (i, 0))
    const = lambda a_: pl.BlockSpec(a_.shape, lambda i: (0, 0))
    return pl.pallas_call(
        functools.partial(_merge_kernel, alpha),
        grid=(n // tm,),
        in_specs=[tile(xf), tile(ya), tile(yb), tile(pgate), const(wa), const(wb), const(wo), const(g),
                  const(bta), const(wr)],
        out_specs=[pl.BlockSpec((tm, d), lambda i: (i, 0)), pl.BlockSpec((LANES, tm), lambda i: (0, i))],
        out_shape=[jax.ShapeDtypeStruct((n, d), F32), jax.ShapeDtypeStruct((LANES, n), F32)],
        compiler_params=_params(("parallel",)),
    )(xf, ya, yb, pgate, wa, wb, wo, g, bta, wr)


ROUTER_EXPERT_ROW = 8


def _to_lanes(col, nl):
    ne = col.shape[0]
    diag = lax.broadcasted_iota(I32, (ne, nl), 0) == lax.broadcasted_iota(I32, (ne, nl), 1)
    return jnp.sum(jnp.where(diag, col, 0.0), axis=0, keepdims=True)


def _route_kernel(lt_ref, bias_ref, upper_ref, lower_ref, route_ref, seg_ref, cnt_ref, carry_ref):
    @pl.when(pl.program_id(0) == 0)
    def _():
        carry_ref[...] = jnp.zeros_like(carry_ref)

    ne = N_GROUPS * EXPERTS_PER_GROUP
    lt = lt_ref[...] + bias_ref[...]
    nb = lt.shape[1]
    neg = -jnp.inf
    lg = lt[0:8]
    rg = lax.broadcasted_iota(I32, (8, nb), 0).astype(F32)
    lg = jnp.where(rg < N_GROUPS, lg, neg)
    mg = jnp.max(lg, axis=0, keepdims=True)
    gidx = jnp.min(jnp.where(lg == mg, rg, 1e9), axis=0, keepdims=True)
    pg_sel = 1.0 / jnp.sum(jnp.exp(lg - mg), axis=0, keepdims=True)

    le = lt[ROUTER_EXPERT_ROW:ROUTER_EXPERT_ROW + ne]
    re = lax.broadcasted_iota(I32, (ne, nb), 0).astype(F32)
    in_group = jnp.floor(re * (1.0 / EXPERTS_PER_GROUP)) == gidx
    l1 = jnp.where(in_group, le, neg)
    m1 = jnp.max(l1, axis=0, keepdims=True)
    i1 = jnp.min(jnp.where(l1 == m1, re, 1e9), axis=0, keepdims=True)
    l2 = jnp.where(re == i1, neg, l1)
    m2 = jnp.max(l2, axis=0, keepdims=True)
    i2 = jnp.min(jnp.where(l2 == m2, re, 1e9), axis=0, keepdims=True)
    e2 = jnp.exp(m2 - m1)
    w1 = pg_sel / (1.0 + e2)
    w2 = pg_sel * e2 / (1.0 + e2)

    sel1 = re == i1
    sel2 = re == i2
    onehot = jnp.where(sel1 | sel2, 1.0, 0.0)
    before = _dg(onehot.astype(BF16), upper_ref[...])
    cnt_t = jnp.sum(onehot, axis=1, keepdims=True)
    seg = jnp.floor((cnt_t + (SEG_ALIGN - 1)) * (1.0 / SEG_ALIGN)) * SEG_ALIGN
    lstart = _dg(lower_ref[...], jnp.broadcast_to(seg, (ne, LANES)).astype(BF16))[:, 0:1]
    tot = lstart + before
    lpos1 = jnp.sum(jnp.where(sel1, tot, 0.0), axis=0, keepdims=True)
    lpos2 = jnp.sum(jnp.where(sel2, tot, 0.0), axis=0, keepdims=True)
    grel = carry_ref[...]
    carry = grel + seg
    carry_ref[...] = carry
    cnt_ref[...] = jnp.broadcast_to(carry, cnt_ref.shape)
    zero = jnp.zeros_like(w1)
    route_ref[...] = jnp.concatenate([i1, i2, lpos1, lpos2, w1, w2, zero, zero], axis=0)
    nl = seg_ref.shape[2]
    zl = jnp.zeros((1, nl), F32)
    ltot = jnp.broadcast_to(jnp.sum(seg, axis=0, keepdims=True), (1, nl))
    seg_ref[0] = jnp.concatenate([_to_lanes(seg, nl), _to_lanes(lstart, nl), _to_lanes(grel, nl), ltot,
                                  zl, zl, zl, zl], axis=0)


def _route(lt, bias_col):
    n = lt.shape[1]
    tb = TB_ROUTE
    ne = N_GROUPS * EXPERTS_PER_GROUP
    ii = jnp.arange(tb)
    upper = (ii[:, None] < ii[None, :]).astype(BF16)
    ee = jnp.arange(ne)
    lower = (ee[:, None] > ee[None, :]).astype(BF16)
    return pl.pallas_call(
        _route_kernel,
        grid=(n // tb,),
        in_specs=[pl.BlockSpec((LANES, tb), lambda i: (0, i)),
                  pl.BlockSpec((LANES, 1), lambda i: (0, 0)),
                  pl.BlockSpec((tb, tb), lambda i: (0, 0)),
                  pl.BlockSpec((ne, ne), lambda i: (0, 0))],
        out_specs=[pl.BlockSpec((8, tb), lambda i: (0, i)), pl.BlockSpec((1, 8, LANES), lambda i: (i, 0, 0)),
                   pl.BlockSpec((ne, LANES), lambda i: (0, 0))],
        out_shape=[jax.ShapeDtypeStruct((8, n), F32), jax.ShapeDtypeStruct((n // tb, 8, LANES), F32),
                   jax.ShapeDtypeStruct((ne, LANES), F32)],
        scratch_shapes=[pltpu.VMEM((ne, 1), F32)],
        compiler_params=_params(("arbitrary",)),
    )(lt, bias_col, upper, lower)


TAB_LANES = LANES


def _finalize_kernel(tm, seg_ref, cnt_ref, lower_ref, segtab_ref, tab_ref):
    ne = cnt_ref.shape[0]
    cnt = cnt_ref[...]
    nb = jnp.floor((cnt + (tm - 1)) * (1.0 / tm))
    bstart = _dg(lower_ref[...], nb.astype(BF16))
    bend = bstart + nb
    pad_start = bstart[:, 0:1] * tm
    seg = seg_ref[0]
    gstart = seg[2:3] + _to_lanes(pad_start, seg.shape[1])
    segtab_ref[0] = jnp.concatenate([seg[0:2], gstart, seg[3:8]], axis=0).astype(I32)

    nl = tab_ref.shape[1]
    n_used = jnp.max(bend[:, 0:1], axis=0, keepdims=True)
    pad_lo = _to_lanes(pad_start + cnt[:, 0:1], nl)
    pad_hi = _to_lanes(bend[:, 0:1] * tm, nl)
    zero = jnp.zeros((1, nl), F32)
    tab_ref[...] = jnp.concatenate([_to_lanes(bstart[:, 0:1], nl), jnp.broadcast_to(n_used, (1, nl)), pad_lo,
                                    pad_hi, _to_lanes(nb[:, 0:1], nl), zero, zero, zero], axis=0).astype(I32)


def _finalize(seg, cnt, tm):
    ntile = seg.shape[0]
    ne = cnt.shape[0]
    ii = jnp.arange(ne)
    lower = (ii[:, None] > ii[None, :]).astype(BF16)
    return pl.pallas_call(
        functools.partial(_finalize_kernel, tm),
        grid=(ntile,),
        in_specs=[pl.BlockSpec((1, 8, LANES), lambda i: (i, 0, 0)), pl.BlockSpec(cnt.shape, lambda i: (0, 0)),
                  pl.BlockSpec((ne, ne), lambda i: (0, 0))],
        out_specs=[pl.BlockSpec((1, 8, LANES), lambda i: (i, 0, 0)), pl.BlockSpec((8, TAB_LANES), lambda i: (0, 0))],
        out_shape=[jax.ShapeDtypeStruct((ntile, 8, LANES), I32), jax.ShapeDtypeStruct((8, TAB_LANES), I32)],
        compiler_params=_params(("arbitrary",)),
    )(seg, cnt, lower)


def _for_each_piece(length, max_len, fn):
    size = SEG_ALIGN
    sizes = []
    while size <= max_len:
        sizes.append(size)
        size *= 2
    for size in reversed(sizes):
        @pl.when(jnp.bitwise_and(length, size) != 0)
        def _(size=size):
            fn(pl.multiple_of(jnp.bitwise_and(length, -2 * size), SEG_ALIGN), size)


def _sorted_rows(td):
    return 2 * td + N_GROUPS * EXPERTS_PER_GROUP * SEG_ALIGN


def _dispatch_kernel(tm, seglen_ref, lstart_ref, gstart_ref, ltot_ref, plo_ref, phi_ref, nu_ref,
                     x_ref, route_ref, xbuf_ref, sorted_ref, zblk, sem, zsem):
    i = pl.program_id(0)
    nsteps = pl.num_programs(0)
    td = x_ref.shape[0]
    ne = plo_ref.shape[0]
    nblk = xbuf_ref.shape[0] // tm
    nrow = sorted_ref.shape[1]
    buf = i % 2

    lpos = route_ref[2:4, :]
    r = lax.broadcasted_iota(I32, (nrow, td), 0).astype(F32)
    onehot = jnp.where((r == lpos[0:1]) | (r == lpos[1:2]), 1.0, 0.0).astype(BF16)
    sorted_ref[buf] = _dg(onehot, x_ref[...].astype(BF16)).astype(BF16)

    def wait_tile(step, b):
        _for_each_piece(ltot_ref[step], nrow, lambda off, size: pltpu.make_async_copy(
            sorted_ref.at[b, pl.ds(0, size), :], xbuf_ref.at[pl.ds(0, size), :], sem.at[b]).wait())

    for e in range(ne):
        idx = i * ne + e
        ls = pl.multiple_of(lstart_ref[idx], SEG_ALIGN)
        gs = pl.multiple_of(gstart_ref[idx], SEG_ALIGN)
        _for_each_piece(seglen_ref[idx], td, lambda off, size: pltpu.make_async_copy(
            sorted_ref.at[buf, pl.ds(ls + off, size), :], xbuf_ref.at[pl.ds(gs + off, size), :],
            sem.at[buf]).start())

    @pl.when(i > 0)
    def _():
        wait_tile(i - 1, 1 - buf)

    @pl.when(i == nsteps - 1)
    def _():
        wait_tile(i, buf)

    def pad_fill(fn):
        for e in range(ne):
            lo = pl.multiple_of(plo_ref[e], SEG_ALIGN)
            _for_each_piece(phi_ref[e] - lo, tm // 2, lambda off, size: fn(pltpu.make_async_copy(
                zblk.at[pl.ds(0, size), :], xbuf_ref.at[pl.ds(lo + off, size), :], zsem)))

        def per_blk(b, carry):
            fn(pltpu.make_async_copy(zblk, xbuf_ref.at[pl.ds(pl.multiple_of(b * tm, tm), tm), :], zsem))
            return carry
        lax.fori_loop(nu_ref[0], nblk, per_blk, 0)

    @pl.when(i == 0)
    def _():
        zblk[...] = jnp.zeros_like(zblk)
        pad_fill(lambda cp: cp.start())
        pad_fill(lambda cp: cp.wait())


def _dispatch(seglen, lstart, gstart, ltot, pad_lo, pad_hi, n_used, x1, route, rows, tm):
    n, d = x1.shape
    td = TD_DISPATCH
    return pl.pallas_call(
        functools.partial(_dispatch_kernel, tm),
        grid_spec=pltpu.PrefetchScalarGridSpec(
            num_scalar_prefetch=7,
            grid=(n // td,),
            in_specs=[pl.BlockSpec((td, d), lambda i, *_: (i, 0)), pl.BlockSpec((8, td), lambda i, *_: (0, i))],
            out_specs=pl.BlockSpec(memory_space=pl.ANY),
            scratch_shapes=[pltpu.VMEM((2, _sorted_rows(td), d), BF16), pltpu.VMEM((tm, d), BF16),
                            pltpu.SemaphoreType.DMA((2,)), pltpu.SemaphoreType.DMA(())],
        ),
        out_shape=jax.ShapeDtypeStruct((rows, d), BF16),
        compiler_params=_params(("arbitrary",)),
    )(seglen, lstart, gstart, ltot, pad_lo, pad_hi, n_used, x1, route)


def _expert_kernel(tm, bstart_ref, nb_ref, nu_ref, w1_ref, w3_ref, w2_ref, xbuf_ref, ybuf_ref,
                   w13b, w2b, xb, yb, semx, semy):
    e = pl.program_id(0)
    nb = nb_ref[e]
    b0 = bstart_ref[e]
    nblk = ybuf_ref.shape[0] // tm
    rows = lambda blk: pl.ds(pl.multiple_of(blk * tm, tm), tm)
    x_copy = lambda j, slot: pltpu.make_async_copy(xbuf_ref.at[rows(b0 + j), :], xb.at[slot], semx.at[slot])
    y_copy = lambda blk, slot: pltpu.make_async_copy(yb.at[slot], ybuf_ref.at[rows(blk), :], semy.at[slot])

    @pl.when(nb > 0)
    def _():
        x_copy(0, 0).start(priority=ROW_DMA_PRIORITY)

    de = w2b.shape[0]
    w13b[:, :de] = w1_ref[0].astype(BF16)
    w13b[:, de:] = w3_ref[0].astype(BF16)
    w2b[...] = w2_ref[0].astype(BF16)

    def body(j, carry):
        slot = j % 2
        x_copy(j, slot).wait()

        @pl.when(j + 1 < nb)
        def _():
            x_copy(j + 1, 1 - slot).start(priority=ROW_DMA_PRIORITY)

        @pl.when(j >= 2)
        def _():
            y_copy(b0 + j - 2, slot).wait()

        x = xb[slot]
        h13 = _dg(x, w13b[...])
        h1 = h13[:, :de]
        h = (h1 * _sigmoid(h1)) * h13[:, de:]
        yb[slot] = _dot(h, w2b[...]).astype(BF16)
        y_copy(b0 + j, slot).start(priority=ROW_DMA_PRIORITY)
        return carry

    lax.fori_loop(0, nb, body, 0)

    @pl.when(nb >= 2)
    def _():
        y_copy(b0 + nb - 2, nb % 2).wait()

    @pl.when(nb >= 1)
    def _():
        y_copy(b0 + nb - 1, (nb - 1) % 2).wait()

    @pl.when(e == pl.num_programs(0) - 1)
    def _():
        yb[0] = jnp.zeros(yb.shape[1:], yb.dtype)

        def fill(fn):
            def per_blk(blk, carry):
                fn(y_copy(blk, 0))
                return carry
            lax.fori_loop(nu_ref[0], nblk, per_blk, 0)

        fill(lambda cp: cp.start())
        fill(lambda cp: cp.wait())


def _experts(bstart, nb, n_used, xbuf, w1, w3, w2):
    rows, d = xbuf.shape
    ne, _, de = w1.shape
    tm = TM_EXPERT
    return pl.pallas_call(
        functools.partial(_expert_kernel, tm),
        grid_spec=pltpu.PrefetchScalarGridSpec(
            num_scalar_prefetch=3,
            grid=(ne,),
            in_specs=[pl.BlockSpec((1, d, de), lambda e, *_: (e, 0, 0)),
                      pl.BlockSpec((1, d, de), lambda e, *_: (e, 0, 0)),
                      pl.BlockSpec((1, de, d), lambda e, *_: (e, 0, 0)),
                      pl.BlockSpec(memory_space=pl.ANY)],
            out_specs=pl.BlockSpec(memory_space=pl.ANY),
            scratch_shapes=[pltpu.VMEM((d, 2 * de), BF16), pltpu.VMEM((de, d), BF16),
                            pltpu.VMEM((2, tm, d), BF16), pltpu.VMEM((2, tm, d), BF16),
                            pltpu.SemaphoreType.DMA((2,)), pltpu.SemaphoreType.DMA((2,))],
        ),
        out_shape=jax.ShapeDtypeStruct((rows, d), BF16),
        compiler_params=_params(("arbitrary",)),
    )(bstart, nb, n_used, w1, w3, w2, xbuf)


def _combine_kernel(alpha, seglen_ref, lstart_ref, gstart_ref, ltot_ref, x1_ref, rt_ref, p_ref, wpe_ref, wpg_ref,
                    g_ref, b_ref, ybuf_ref, out_ref, sorted_ref, sem):
    i = pl.program_id(0)
    nsteps = pl.num_programs(0)
    tc = x1_ref.shape[0]
    ne = N_GROUPS * EXPERTS_PER_GROUP
    nrow = sorted_ref.shape[1]

    def fetch(step, buf):
        for e in range(ne):
            idx = step * ne + e
            ls = pl.multiple_of(lstart_ref[idx], SEG_ALIGN)
            gs = pl.multiple_of(gstart_ref[idx], SEG_ALIGN)
            _for_each_piece(seglen_ref[idx], tc, lambda off, size: pltpu.make_async_copy(
                ybuf_ref.at[pl.ds(gs + off, size), :], sorted_ref.at[buf, pl.ds(ls + off, size), :],
                sem.at[buf]).start())

    @pl.when(i == 0)
    def _():
        sorted_ref[...] = jnp.zeros_like(sorted_ref)
        fetch(0, 0)

    @pl.when(i + 1 < nsteps)
    def _():
        fetch(i + 1, (i + 1) % 2)

    cur = i % 2
    _for_each_piece(ltot_ref[i], nrow, lambda off, size: pltpu.make_async_copy(
        ybuf_ref.at[pl.ds(0, size), :], sorted_ref.at[cur, pl.ds(0, size), :], sem.at[cur]).wait())

    rt = rt_ref[...]
    r = lax.broadcasted_iota(I32, (tc, nrow), 1).astype(F32)
    unsort = jnp.where(r == rt[:, 0:1], rt[:, 2:3], 0.0) + jnp.where(r == rt[:, 1:2], rt[:, 3:4], 0.0)
    ffn = _dg(unsort.astype(BF16), sorted_ref[cur])
    x2 = _layer_norm(alpha * x1_ref[...] + ffn, g_ref[...], b_ref[...])
    gate = _sigmoid(_dot(x2, wpg_ref[...]))
    out_ref[...] = x2 + gate * _dot(p_ref[...], wpe_ref[...])


def _combine(seglen, lstart, gstart, ltot, x1, route_t, pf, wpe, wpg, g, bta, ybuf, alpha):
    n, d = x1.shape
    tc = TD_DISPATCH
    tile = lambda a_: pl.BlockSpec((tc, a_.shape[1]), lambda i, *_: (i, 0))
    const = lambda a_: pl.BlockSpec(a_.shape, lambda i, *_: (0, 0))
    return pl.pallas_call(
        functools.partial(_combine_kernel, alpha),
        grid_spec=pltpu.PrefetchScalarGridSpec(
            num_scalar_prefetch=4,
            grid=(n // tc,),
            in_specs=[tile(x1), tile(route_t), tile(pf), const(wpe), const(wpg), const(g), const(bta),
                      pl.BlockSpec(memory_space=pl.ANY)],
            out_specs=pl.BlockSpec((tc, d), lambda i, *_: (i, 0)),
            scratch_shapes=[pltpu.VMEM((2, _sorted_rows(tc), d), BF16), pltpu.SemaphoreType.DMA((2,))],
        ),
        out_shape=jax.ShapeDtypeStruct((n, d), F32),
        compiler_params=_params(("arbitrary",)),
    )(seglen, lstart, gstart, ltot, x1, route_t, pf, wpe, wpg, g, bta, ybuf)


def _layer(x, p_i, w_in, rw_mu, rw_w0, rw_w_up, rw_a0, rw_a_up, rw_g_up, rw_k_k, rw_k_a, rw_r_k, rw_gn_w,
           rw_gn_b, w_a_out, hg_lb_logits, hg_norm_w, w_b_out, w_o, ln1_g, ln1_b, router_g_w, router_g_b,
           router_e_w, router_e_b, w1, w3, w2, ln2_g, ln2_b, w_pe, w_pg, alpha, layer):
    b, t, d = x.shape
    n = b * t
    rw_dim = rw_w0.shape[0]
    rw_cols = 3 * rw_dim + RW_DECAY_LORA + RW_A_LORA + RW_GATE_LORA
    hg_cols = 4 * hg_norm_w.shape[0]
    ne = N_GROUPS * EXPERTS_PER_GROUP
    row2 = lambda a_: a_.reshape(1, -1)
    xf = x.reshape(n, d)

    wb = w_in.astype(BF16)
    proj_rw, proj_hg, proj_gt = _project(xf, wb[:, :rw_cols], wb[:, rw_cols:rw_cols + hg_cols],
                                         wb[:, rw_cols + hg_cols:])
    ya = _rwkv_branch(proj_rw.reshape(b, t, rw_cols), rw_mu, row2(rw_w0), rw_w_up, rw_a0, rw_a_up, rw_g_up,
                      rw_k_k, rw_k_a, rw_r_k, rw_gn_w, rw_gn_b)
    yb = _hgrn_branch(proj_hg.reshape(b, t, hg_cols), hg_lb_logits, hg_norm_w, layer)

    wr = jnp.zeros((LANES, d), F32)
    wr = wr.at[:N_GROUPS].set(router_g_w.T).at[ROUTER_EXPERT_ROW:ROUTER_EXPERT_ROW + ne].set(router_e_w.T)
    bias = jnp.zeros((LANES,), F32)
    bias = bias.at[:N_GROUPS].set(router_g_b).at[ROUTER_EXPERT_ROW:ROUTER_EXPERT_ROW + ne].set(router_e_b)
    x1, lt = _merge(xf, ya.reshape(n, -1), yb.reshape(n, -1), proj_gt, w_a_out.astype(BF16),
                    w_b_out.astype(BF16), w_o.astype(BF16), row2(ln1_g), row2(ln1_b), wr, alpha)
    route, seg, cnt = _route(lt, bias.reshape(LANES, 1))

    tm = TM_EXPERT
    ntile = n // TD_DISPATCH
    nblk = -(-(2 * n + (SEG_ALIGN - 1) * ne * ntile) // tm) + ne
    assert TB_ROUTE == TD_DISPATCH
    segtab, tab = _finalize(seg, cnt, tm)
    per_seg = lambda row: segtab[:, row, :ne].reshape(-1)
    seglen, lstart, gstart, ltot = per_seg(0), per_seg(1), per_seg(2), segtab[:, 3, 0]
    n_used = tab[1, :1]

    xbuf = _dispatch(seglen, lstart, gstart, ltot, tab[2, :ne], tab[3, :ne], n_used, x1, route, nblk * tm, tm)
    ybuf = _experts(tab[0, :ne], tab[4, :ne], n_used, xbuf, w1, w3, w2)
    out = _combine(seglen, lstart, gstart, ltot, x1, route[2:6].T, p_i.reshape(n, -1), w_pe.astype(BF16),
                   w_pg.astype(BF16), row2(ln2_g), row2(ln2_b), ybuf, alpha)
    return out.reshape(b, t, d)


def kernel(x, p, w_in, rw_mu, rw_w0, rw_w_up, rw_a0, rw_a_up, rw_g_up, rw_k_k, rw_k_a, rw_r_k, rw_gn_w, rw_gn_b,
           w_a_out, hg_lb_logits, hg_norm_w, w_b_out, w_o, ln1_g, ln1_b, router_g_w, router_g_b, router_e_w,
           router_e_b, w1, w3, w2, ln2_g, ln2_b, w_pe, w_pg):
    depth = w_in.shape[0]
    alpha = (2 * depth) ** 0.25
    for i in range(depth):
        x = _layer(x, p[i], w_in[i], rw_mu[i], rw_w0[i], rw_w_up[i], rw_a0[i], rw_a_up[i], rw_g_up[i], rw_k_k[i],
                   rw_k_a[i], rw_r_k[i].reshape(-1), rw_gn_w[i], rw_gn_b[i], w_a_out[i], hg_lb_logits,
                   hg_norm_w[i], w_b_out[i], w_o[i], ln1_g[i], ln1_b[i],
                   router_g_w[i], router_g_b[i], router_e_w[i], router_e_b[i], w1[i], w3[i], w2[i], ln2_g[i],
                   ln2_b[i], w_pe[i], w_pg[i], alpha, i)
    return x
```

```python
import functools

import jax
import jax.numpy as jnp
from jax import lax
from jax.experimental import pallas as pl
from jax.experimental.pallas import tpu as pltpu

F32 = jnp.float32
BF16 = jnp.bfloat16
I32 = jnp.int32

NN = (((1,), (0,)), ((), ()))
NT = (((1,), (1,)), ((), ()))

RW_HEAD = 64
RW_DECAY_LORA = 64
RW_A_LORA = 64
RW_GATE_LORA = 128
RW_GN_EPS = 64e-5
RW_DECAY_SCALE = 0.6065306597126334
HG_HEADS = 4
N_GROUPS = 4
EXPERTS_PER_GROUP = 8
LN_EPS = 1e-5
RMS_EPS = 1e-6

CHUNK = 64
SUB = 16
LANES = 128
VMEM_LIMIT = 56 * 1024 * 1024

TM_PROJ = 256
TB_RWKV = 512
TB_HGRN = 256
TM_MERGE = 512
TB_ROUTE = 512
TD_DISPATCH = 512
SEG_ALIGN = 16
TM_EXPERT = 512
RW_WIDE = 4
HG_WIDE = 4


def _dg(a, b, dn=NN):
    return lax.dot_general(a, b, dn, preferred_element_type=F32)


def _dot(a, b, dn=NN):
    return _dg(a.astype(BF16), b.astype(BF16), dn)


def _split(a):
    hi = a.astype(BF16)
    lo = (a - hi.astype(F32)).astype(BF16)
    return hi, lo


def _dot_hl(a, b_exact, dn=NN):
    hi, lo = _split(a)
    return _dg(hi, b_exact, dn) + _dg(lo, b_exact, dn)


def _dot3(a, b, dn=NN):
    ah, al = _split(a)
    bh, bl = _split(b)
    return _dg(ah, bh, dn) + (_dg(ah, bl, dn) + _dg(al, bh, dn))


def _cumsum_chunks(x, tri):
    h = x.astype(BF16)
    r1 = x - h.astype(F32)
    m = r1.astype(BF16)
    l = (r1 - m.astype(F32)).astype(BF16)
    return _dg(tri, h) + (_dg(tri, m) + _dg(tri, l))


def _sigmoid(x):
    return 0.5 * jnp.tanh(0.5 * x) + 0.5


def _layer_norm(h, g, b):
    mu = jnp.mean(h, axis=-1, keepdims=True)
    d = h - mu
    var = jnp.mean(d * d, axis=-1, keepdims=True)
    return d * lax.rsqrt(var + LN_EPS) * g + b


def _params(sem):
    return pltpu.CompilerParams(dimension_semantics=sem, vmem_limit_bytes=VMEM_LIMIT)


def _proj_kernel(x_ref, wr_ref, wh_ref, wg_ref, pr_ref, ph_ref, pg_ref):
    xb = x_ref[...].astype(BF16)
    pr_ref[...] = _dg(xb, wr_ref[...])
    ph_ref[...] = _dg(xb, wh_ref[...])
    pg_ref[...] = _dg(xb, wg_ref[...])


def _project(xf, w_rw, w_hg, w_gt):
    n, d = xf.shape
    tm = TM_PROJ
    full = lambda w: pl.BlockSpec(w.shape, lambda i: (0, 0))
    tile = lambda c: pl.BlockSpec((tm, c), lambda i: (i, 0))
    return pl.pallas_call(
        _proj_kernel,
        grid=(n // tm,),
        in_specs=[tile(d), full(w_rw), full(w_hg), full(w_gt)],
        out_specs=[tile(w_rw.shape[1]), tile(w_hg.shape[1]), tile(w_gt.shape[1])],
        out_shape=[jax.ShapeDtypeStruct((n, w.shape[1]), F32) for w in (w_rw, w_hg, w_gt)],
        compiler_params=_params(("parallel",)),
    )(xf, w_rw, w_hg, w_gt)


def _each(f, *ls):
    return [f(*xs) for xs in zip(*ls)]


def _two(x):
    m1 = lax.broadcasted_iota(I32, x.shape, 1) < RW_HEAD
    return jnp.concatenate([jnp.where(m1, x, 0.0), jnp.where(m1, 0.0, x)], axis=0)


def _rwkv_chunk_prepare(ins, lvl_ref, out):
    c = CHUNK
    lane = lax.broadcasted_iota(I32, (c, LANES), 1)
    row = lax.broadcasted_iota(I32, (c, LANES), 0)
    scol = jnp.bitwise_and(lane, RW_HEAD - 1)
    strict = row > scol
    incl = row >= scol
    r2 = lax.broadcasted_iota(I32, (LANES, LANES), 0)
    c2 = lax.broadcasted_iota(I32, (LANES, LANES), 1)
    eye = jnp.where(r2 == c2, 1.0, 0.0).astype(F32)

    def prep(r, k, v, av, bv, lw, lc):
        l_end = lc[c - 1:c]
        e_r = jnp.exp(l_end - lc)
        e_n = jnp.exp(-lc)
        return dict(at=av * jnp.exp(lc - lw), rt=r * jnp.exp(lc), bt=bv * e_n, kt=k * e_n,
                    bk=jnp.concatenate([bv * e_r, k * e_r], axis=0), pc=jnp.exp(l_end), v=v)

    q = [prep(*xs) for xs in ins]
    yield
    p = [_dot(jnp.concatenate([d["at"], d["rt"]], axis=0),
              jnp.concatenate([_two(d["bt"]), _two(d["kt"])], axis=0), NT) for d in q]
    sab = [jnp.where(strict, x[:c, :LANES], 0.0) for x in p]
    sak = [jnp.where(strict, x[:c, LANES:], 0.0) for x in p]
    srb = [jnp.where(incl, x[c:, :LANES], 0.0) for x in p]
    srk = [jnp.where(incl, x[c:, LANES:], 0.0) for x in p]
    yield
    sv = _each(lambda ak, rk, d: _dot(jnp.concatenate([ak, rk], axis=0), _two(d["v"])), sak, srk, q)
    yield

    a_bd = [_two(x) for x in sab]
    t = [eye + a * lvl_ref[0] for a in a_bd]
    for lv in range(1, lvl_ref.shape[0]):
        ta = _each(lambda t_, a: _dot(t_, a * lvl_ref[lv]), t, a_bd)
        yield
        t = _each(lambda t_, ta_: t_ + _dot(ta_, t_), t, ta)
        yield

    x = _each(lambda t_, d, sv_: _dot(t_, jnp.concatenate([_two(d["at"]), _two(sv_[:c])], axis=1)), t, q, sv)
    out.extend(dict(uk=x_[:c, :LANES] + x_[c:, :LANES],
                    w=x_[:c, LANES:] + x_[c:, LANES:],
                    rt=d["rt"], rkv=sv_[c:], srb=srb_, bk=d["bk"], v=d["v"], pc=d["pc"])
               for x_, d, sv_, srb_ in zip(x, q, sv, srb))


def _rwkv_chunk_apply(prep, states, bd, out):
    c = CHUNK
    g1 = _each(lambda d, s: _dot(jnp.concatenate([d["uk"], d["rt"]], axis=0), s, NT), prep, states)
    yield
    u = _each(lambda g, d: g[:c] + d["w"], g1, prep)
    y = _each(lambda g, d, u_: g[c:] + d["rkv"] + _dot(d["srb"], _two(u_)), g1, prep, u)
    yield
    upd = _each(lambda u_, d: _dot(jnp.concatenate([u_, d["v"]], axis=0).T, d["bk"]), u, prep)
    s_new = _each(lambda d, s, up: s * d["pc"] + bd * up, prep, states, upd)
    out.extend(zip(y, s_new))
    yield


def _interleave(*gens):
    live = [g for g in gens if g is not None]
    while live:
        for g in list(live):
            try:
                next(g)
            except StopIteration:
                live.remove(g)


def _rwkv_kernel(u_ref, mu_ref, w0_ref, wup_ref, a0_ref, aup_ref, gup_ref, kk_ref, ka_ref, rk_ref,
                 gnw_ref, gnb_ref, tri_ref, gsum_ref, lvl_ref, bd_ref, ya_ref,
                 s_ref, prev_ref, r_s, k_s, v_s, a_s, b_s, lw_s, lc_s, g_s, y_s):
    tb = pl.program_id(1)

    @pl.when(tb == 0)
    def _():
        s_ref[...] = jnp.zeros_like(s_ref)
        prev_ref[...] = jnp.zeros_like(prev_ref)

    nt = u_ref.shape[1]
    dim = r_s.shape[1]
    part = CHUNK * RW_WIDE
    npair = dim // LANES
    lanes = [slice(p * LANES, (p + 1) * LANES) for p in range(npair)]
    bd = bd_ref[...]
    gsum = gsum_ref[...]

    def prologue(h):
        rows = slice(h * part, (h + 1) * part)
        u = u_ref[0, rows, :]
        before = prev_ref[...] if h == 0 else u_ref[0, h * part - 1:h * part, :]
        rowid = lax.broadcasted_iota(I32, u.shape, 0)
        shifted = jnp.where(rowid == 0, before, pltpu.roll(u, 1, axis=0))
        um = u + (shifted - u) * mu_ref[...]
        r = um[:, 0:dim]
        k = um[:, dim:2 * dim]
        v = um[:, 2 * dim:3 * dim]
        xwa = um[:, 3 * dim:3 * dim + LANES]
        xg = um[:, 3 * dim + LANES:3 * dim + 2 * LANES]
        yield
        wpre = w0_ref[...] + _dot(jnp.tanh(xwa), wup_ref[...])
        lw = -RW_DECAY_SCALE * _sigmoid(wpre)
        a = _sigmoid(a0_ref[...] + _dot(xwa, aup_ref[...]))
        g_s[rows, :] = _dot(_sigmoid(xg), gup_ref[...])
        yield
        kk = k * kk_ref[...]
        ss = _dot(kk * kk, gsum)
        kk = kk * lax.rsqrt(jnp.maximum(ss, 1e-24))
        r_s[rows, :] = r
        k_s[rows, :] = k * (1.0 + (a - 1.0) * ka_ref[...])
        v_s[rows, :] = v
        a_s[rows, :] = -kk
        b_s[rows, :] = kk * a
        yield
        lw_s[rows, :] = lw
        lc_s[rows, :] = _cumsum_chunks(lw, tri_ref[...])

    def chunk_rows(h):
        return [slice(h * part + ci * CHUNK, h * part + (ci + 1) * CHUNK) for ci in range(RW_WIDE)]

    def prepare(h, out):
        ins = [(r_s[rw, ls], k_s[rw, ls], v_s[rw, ls], a_s[rw, ls], b_s[rw, ls], lw_s[rw, ls], lc_s[rw, ls])
               for rw in chunk_rows(h) for ls in lanes]
        yield from _rwkv_chunk_prepare(ins, lvl_ref, out)

    def apply(h, prep):
        states = [s_ref[p] for p in range(npair)]
        for ci, rw in enumerate(chunk_rows(h)):
            outs = []
            yield from _rwkv_chunk_apply(prep[ci * npair:(ci + 1) * npair], states, bd, outs)
            states = [s_new for _, s_new in outs]
            for (y, _), ls in zip(outs, lanes):
                y_s[rw, ls] = y
        for p in range(npair):
            s_ref[p] = states[p]

    def epilogue(h):
        rows = slice(h * part, (h + 1) * part)
        y = y_s[rows, :]
        inv_n = 1.0 / RW_HEAD
        m = _dot_hl(y, gsum) * inv_n
        d = y - m
        yield
        var = _dot(d * d, gsum) * inv_n
        yn = d * lax.rsqrt(var + RW_GN_EPS) * gnw_ref[...] + gnb_ref[...]
        yield
        bonus = _dot(r_s[rows, :] * k_s[rows, :] * rk_ref[...], gsum) * v_s[rows, :]
        ya_ref[0, rows, :] = ((yn + bonus) * g_s[rows, :]).astype(BF16)

    nparts = nt // part
    preps = [[] for _ in range(nparts)]
    _interleave(prologue(0))
    prev_ref[...] = u_ref[0, nt - 1:nt, :]
    _interleave(prepare(0, preps[0]), prologue(1) if nparts > 1 else None)
    for h in range(nparts):
        _interleave(apply(h, preps[h]),
                    prepare(h + 1, preps[h + 1]) if h + 1 < nparts else None,
                    prologue(h + 2) if h + 2 < nparts else None,
                    epilogue(h - 1) if h > 0 else None)
    _interleave(epilogue(nparts - 1))


def _rwkv_branch(proj_rw, mu, w0, wup, a0, aup, gup, k_k, k_a, r_k, gn_w, gn_b):
    b, t, cols = proj_rw.shape
    dim = w0.shape[1]
    tb = TB_RWKV
    ii = jnp.arange(CHUNK * RW_WIDE)
    tri = ((ii[:, None] // CHUNK == ii[None, :] // CHUNK) & (ii[:, None] >= ii[None, :])).astype(BF16)
    jj = jnp.arange(dim)
    gsum = (jj[:, None] // RW_HEAD == jj[None, :] // RW_HEAD).astype(BF16)
    rr = jnp.arange(LANES)[:, None]
    cc = jnp.arange(LANES)[None, :]
    lvls = []
    s = 1
    while s < CHUNK:
        lvls.append(((rr // (2 * s) == cc // (2 * s)) & ((rr // s) % 2 == 1) & ((cc // s) % 2 == 0)).astype(F32))
        s *= 2
    lvl = jnp.stack(lvls)
    bd = (rr // RW_HEAD == cc // RW_HEAD).astype(F32)
    zpad = lambda rows: jnp.zeros((rows, dim), F32)
    wup_p = jnp.concatenate([wup, zpad(LANES - wup.shape[0])], axis=0).astype(BF16)
    aup_p = jnp.concatenate([zpad(LANES - aup.shape[0]), aup], axis=0).astype(BF16)
    row2 = lambda a_: a_.reshape(1, -1)
    const = lambda a_: pl.BlockSpec(a_.shape, lambda bi, ti: (0,) * a_.ndim)
    args = [row2(mu), row2(w0), wup_p, row2(a0), aup_p, gup.astype(BF16), row2(k_k), row2(k_a), row2(r_k),
            row2(gn_w), row2(gn_b), tri, gsum, lvl, bd]
    sc = lambda: pltpu.VMEM((tb, dim), F32)
    return pl.pallas_call(
        _rwkv_kernel,
        grid=(b, t // tb),
        in_specs=[pl.BlockSpec((1, tb, cols), lambda bi, ti: (bi, ti, 0))] + [const(a_) for a_ in args],
        out_specs=pl.BlockSpec((1, tb, dim), lambda bi, ti: (bi, ti, 0)),
        out_shape=jax.ShapeDtypeStruct((b, t, dim), BF16),
        scratch_shapes=[pltpu.VMEM((dim // LANES, LANES, LANES), F32), pltpu.VMEM((1, cols), F32)]
                       + [sc() for _ in range(9)],
        compiler_params=_params(("arbitrary", "arbitrary")),
    )(proj_rw, *args)


def _hgrn_chunk_prepare(ins):
    c = CHUNK
    subs = [(SUB * i, SUB * (i + 1)) for i in range(c // SUB)]

    def scores(q, k, lf, bc, lo, hi):
        m = bc[lo:lo + 1] - lf[lo:lo + 1]
        att = _dot(q[lo:hi] * jnp.exp(bc[lo:hi] - m), k[:hi] * jnp.exp(m - bc[:hi]), NT)
        tt = lax.broadcasted_iota(I32, (SUB, hi), 0) + lo
        s_ = lax.broadcasted_iota(I32, (SUB, hi), 1)
        return jnp.where(s_ <= tt, att, 0.0)

    att = [[scores(q, k, lf, bc, lo, hi) for lo, hi in subs] for q, k, v, lf, bc in ins]
    upd = [_dot(v.T, k * jnp.exp(bc[c - 1:c] - bc)) for q, k, v, lf, bc in ins]
    intra = [[_dot(a, x[2][:hi]) for a, (lo, hi) in zip(arow, subs)] for arow, x in zip(att, ins)]
    return [dict(intra=jnp.concatenate(rows, axis=0), qe=x[0] * jnp.exp(x[4]), upd=up, pc=jnp.exp(x[4][c - 1:c]))
            for rows, x, up in zip(intra, ins, upd)]


def _hgrn_chunk_apply(prep, states):
    o = _each(lambda d, st: d["intra"] + _dot(d["qe"], st, NT), prep, states)
    st_new = _each(lambda d, st: st * d["pc"] + d["upd"], prep, states)
    return list(zip(o, st_new))


def _hgrn_kernel(layer, u_ref, lbl_ref, nw_ref, tri_ref, gsum_ref, yb_ref,
                 st_ref, q_s, k_s, v_s, lf_s, bc_s, o_s):
    tb = pl.program_id(1)

    @pl.when(tb == 0)
    def _():
        st_ref[...] = jnp.zeros_like(st_ref)

    u = u_ref[0]
    nt = u.shape[0]
    dim = q_s.shape[1]
    lbl = lbl_ref[...]
    e = jnp.exp(lbl - jnp.max(lbl, axis=0, keepdims=True))
    lb = jnp.sum(e[0:layer + 1], axis=0, keepdims=True) / jnp.sum(e, axis=0, keepdims=True)
    zf = u[:, dim:2 * dim]
    sig = _sigmoid(zf)
    f = lb + (1.0 - lb) * sig
    qin = u[:, 0:dim]
    q_s[...] = qin * _sigmoid(qin)
    k_s[...] = (1.0 - lb) * (1.0 - sig)
    v_s[...] = u[:, 2 * dim:3 * dim]
    lf = jnp.log(f)
    lf_s[...] = lf
    bc_s[...] = _cumsum_chunks(lf, tri_ref[...])

    nhead = dim // LANES
    lanes = [slice(h * LANES, (h + 1) * LANES) for h in range(nhead)]

    def group_body(gi, carry):
        rows = [pl.ds(pl.multiple_of((gi * HG_WIDE + ci) * CHUNK, CHUNK), CHUNK) for ci in range(HG_WIDE)]
        ins = [(q_s[rw, ls], k_s[rw, ls], v_s[rw, ls], lf_s[rw, ls], bc_s[rw, ls]) for rw in rows for ls in lanes]
        prep = _hgrn_chunk_prepare(ins)
        states = [st_ref[h] for h in range(nhead)]
        for ci, rw in enumerate(rows):
            outs = _hgrn_chunk_apply(prep[ci * nhead:(ci + 1) * nhead], states)
            states = [st_new for _, st_new in outs]
            for (o, _), ls in zip(outs, lanes):
                o_s[rw, ls] = o
        for h in range(nhead):
            st_ref[h] = states[h]
        return carry

    lax.fori_loop(0, nt // (CHUNK * HG_WIDE), group_body, 0)

    o = o_s[...]
    ms = _dot(o * o, gsum_ref[...]) * (1.0 / LANES)
    og = u[:, 3 * dim:4 * dim]
    yb_ref[0] = (o * lax.rsqrt(ms + RMS_EPS) * nw_ref[...] * _sigmoid(og)).astype(BF16)


def _hgrn_branch(proj_hg, lb_logits, norm_w, layer):
    b, t, cols = proj_hg.shape
    dim = cols // 4
    tb = TB_HGRN
    ii = jnp.arange(tb)
    tri = ((ii[:, None] // CHUNK == ii[None, :] // CHUNK) & (ii[:, None] >= ii[None, :])).astype(BF16)
    jj = jnp.arange(dim)
    gsum = (jj[:, None] // LANES == jj[None, :] // LANES).astype(BF16)
    const = lambda a_: pl.BlockSpec(a_.shape, lambda bi, ti: (0,) * a_.ndim)
    args = [lb_logits, norm_w.reshape(1, -1), tri, gsum]
    sc = lambda: pltpu.VMEM((tb, dim), F32)
    return pl.pallas_call(
        functools.partial(_hgrn_kernel, layer),
        grid=(b, t // tb),
        in_specs=[pl.BlockSpec((1, tb, cols), lambda bi, ti: (bi, ti, 0))] + [const(a_) for a_ in args],
        out_specs=pl.BlockSpec((1, tb, dim), lambda bi, ti: (bi, ti, 0)),
        out_shape=jax.ShapeDtypeStruct((b, t, dim), BF16),
        scratch_shapes=[pltpu.VMEM((dim // LANES, LANES, LANES), F32)] + [sc() for _ in range(6)],
        compiler_params=_params(("arbitrary", "arbitrary")),
    )(proj_hg, *args)


def _merge_kernel(alpha, x_ref, ya_ref, yb_ref, pg_ref, wa_ref, wb_ref, wo_ref, g_ref, b_ref, wr_ref,
                  x1_ref, lt_ref):
    d = x_ref.shape[1]
    gates = _sigmoid(pg_ref[...])
    merged = gates[:, :d] * _dg(ya_ref[...], wa_ref[...]) + gates[:, d:] * _dg(yb_ref[...], wb_ref[...])
    h = alpha * x_ref[...] + _dot(merged, wo_ref[...])
    x1 = _layer_norm(h, g_ref[...], b_ref[...])
    x1_ref[...] = x1
    lt_ref[...] = _dot3(wr_ref[...], x1, NT)


def _merge(xf, ya, yb, pgate, wa, wb, wo, g, bta, wr, alpha):
    n, d = xf.shape
    tm = TM_MERGE
    tile = lambda a_: pl.BlockSpec((tm, a_.shape[1]), lambda i: (i, 0))
    const = lambda a_: pl.BlockSpec(a_.shape, lambda i: (0, 0))
    return pl.pallas_call(
        functools.partial(_merge_kernel, alpha),
        grid=(n // tm,),
        in_specs=[tile(xf), tile(ya), tile(yb), tile(pgate), const(wa), const(wb), const(wo), const(g),
                  const(bta), const(wr)],
        out_specs=[pl.BlockSpec((tm, d), lambda i: (i, 0)), pl.BlockSpec((LANES, tm), lambda i: (0, i))],
        out_shape=[jax.ShapeDtypeStruct((n, d), F32), jax.ShapeDtypeStruct((LANES, n), F32)],
        compiler_params=_params(("parallel",)),
    )(xf, ya, yb, pgate, wa, wb, wo, g, bta, wr)


ROUTER_EXPERT_ROW = 8


def _to_lanes(col, nl):
    ne = col.shape[0]
    diag = lax.broadcasted_iota(I32, (ne, nl), 0) == lax.broadcasted_iota(I32, (ne, nl), 1)
    return jnp.sum(jnp.where(diag, col, 0.0), axis=0, keepdims=True)


def _route_kernel(lt_ref, bias_ref, upper_ref, lower_ref, route_ref, seg_ref, cnt_ref, carry_ref):
    @pl.when(pl.program_id(0) == 0)
    def _():
        carry_ref[...] = jnp.zeros_like(carry_ref)

    ne = N_GROUPS * EXPERTS_PER_GROUP
    lt = lt_ref[...] + bias_ref[...]
    nb = lt.shape[1]
    neg = -jnp.inf
    lg = lt[0:8]
    rg = lax.broadcasted_iota(I32, (8, nb), 0).astype(F32)
    lg = jnp.where(rg < N_GROUPS, lg, neg)
    mg = jnp.max(lg, axis=0, keepdims=True)
    gidx = jnp.min(jnp.where(lg == mg, rg, 1e9), axis=0, keepdims=True)
    pg_sel = 1.0 / jnp.sum(jnp.exp(lg - mg), axis=0, keepdims=True)

    le = lt[ROUTER_EXPERT_ROW:ROUTER_EXPERT_ROW + ne]
    re = lax.broadcasted_iota(I32, (ne, nb), 0).astype(F32)
    in_group = jnp.floor(re * (1.0 / EXPERTS_PER_GROUP)) == gidx
    l1 = jnp.where(in_group, le, neg)
    m1 = jnp.max(l1, axis=0, keepdims=True)
    i1 = jnp.min(jnp.where(l1 == m1, re, 1e9), axis=0, keepdims=True)
    l2 = jnp.where(re == i1, neg, l1)
    m2 = jnp.max(l2, axis=0, keepdims=True)
    i2 = jnp.min(jnp.where(l2 == m2, re, 1e9), axis=0, keepdims=True)
    e2 = jnp.exp(m2 - m1)
    w1 = pg_sel / (1.0 + e2)
    w2 = pg_sel * e2 / (1.0 + e2)

    sel1 = re == i1
    sel2 = re == i2
    onehot = jnp.where(sel1 | sel2, 1.0, 0.0)
    before = _dg(onehot.astype(BF16), upper_ref[...])
    cnt_t = jnp.sum(onehot, axis=1, keepdims=True)
    seg = jnp.floor((cnt_t + (SEG_ALIGN - 1)) * (1.0 / SEG_ALIGN)) * SEG_ALIGN
    lstart = _dg(lower_ref[...], jnp.broadcast_to(seg, (ne, LANES)).astype(BF16))[:, 0:1]
    tot = lstart + before
    lpos1 = jnp.sum(jnp.where(sel1, tot, 0.0), axis=0, keepdims=True)
    lpos2 = jnp.sum(jnp.where(sel2, tot, 0.0), axis=0, keepdims=True)
    grel = carry_ref[...]
    carry = grel + seg
    carry_ref[...] = carry
    cnt_ref[...] = jnp.broadcast_to(carry, cnt_ref.shape)
    zero = jnp.zeros_like(w1)
    route_ref[...] = jnp.concatenate([i1, i2, lpos1, lpos2, w1, w2, zero, zero], axis=0)
    nl = seg_ref.shape[2]
    zl = jnp.zeros((1, nl), F32)
    ltot = jnp.broadcast_to(jnp.sum(seg, axis=0, keepdims=True), (1, nl))
    seg_ref[0] = jnp.concatenate([_to_lanes(seg, nl), _to_lanes(lstart, nl), _to_lanes(grel, nl), ltot,
                                  zl, zl, zl, zl], axis=0)


def _route(lt, bias_col):
    n = lt.shape[1]
    tb = TB_ROUTE
    ne = N_GROUPS * EXPERTS_PER_GROUP
    ii = jnp.arange(tb)
    upper = (ii[:, None] < ii[None, :]).astype(BF16)
    ee = jnp.arange(ne)
    lower = (ee[:, None] > ee[None, :]).astype(BF16)
    return pl.pallas_call(
        _route_kernel,
        grid=(n // tb,),
        in_specs=[pl.BlockSpec((LANES, tb), lambda i: (0, i)),
                  pl.BlockSpec((LANES, 1), lambda i: (0, 0)),
                  pl.BlockSpec((tb, tb), lambda i: (0, 0)),
                  pl.BlockSpec((ne, ne), lambda i: (0, 0))],
        out_specs=[pl.BlockSpec((8, tb), lambda i: (0, i)), pl.BlockSpec((1, 8, LANES), lambda i: (i, 0, 0)),
                   pl.BlockSpec((ne, LANES), lambda i: (0, 0))],
        out_shape=[jax.ShapeDtypeStruct((8, n), F32), jax.ShapeDtypeStruct((n // tb, 8, LANES), F32),
                   jax.ShapeDtypeStruct((ne, LANES), F32)],
        scratch_shapes=[pltpu.VMEM((ne, 1), F32)],
        compiler_params=_params(("arbitrary",)),
    )(lt, bias_col, upper, lower)


TAB_LANES = LANES


def _finalize_kernel(tm, seg_ref, cnt_ref, lower_ref, segtab_ref, tab_ref):
    ne = cnt_ref.shape[0]
    cnt = cnt_ref[...]
    nb = jnp.floor((cnt + (tm - 1)) * (1.0 / tm))
    bstart = _dg(lower_ref[...], nb.astype(BF16))
    bend = bstart + nb
    pad_start = bstart[:, 0:1] * tm
    seg = seg_ref[0]
    gstart = seg[2:3] + _to_lanes(pad_start, seg.shape[1])
    segtab_ref[0] = jnp.concatenate([seg[0:2], gstart, seg[3:8]], axis=0).astype(I32)

    nl = tab_ref.shape[1]
    n_used = jnp.max(bend[:, 0:1], axis=0, keepdims=True)
    pad_lo = _to_lanes(pad_start + cnt[:, 0:1], nl)
    pad_hi = _to_lanes(bend[:, 0:1] * tm, nl)
    zero = jnp.zeros((1, nl), F32)
    tab_ref[...] = jnp.concatenate([_to_lanes(bstart[:, 0:1], nl), jnp.broadcast_to(n_used, (1, nl)), pad_lo,
                                    pad_hi, _to_lanes(nb[:, 0:1], nl), zero, zero, zero], axis=0).astype(I32)


def _finalize(seg, cnt, tm):
    ntile = seg.shape[0]
    ne = cnt.shape[0]
    ii = jnp.arange(ne)
    lower = (ii[:, None] > ii[None, :]).astype(BF16)
    return pl.pallas_call(
        functools.partial(_finalize_kernel, tm),
        grid=(ntile,),
        in_specs=[pl.BlockSpec((1, 8, LANES), lambda i: (i, 0, 0)), pl.BlockSpec(cnt.shape, lambda i: (0, 0)),
                  pl.BlockSpec((ne, ne), lambda i: (0, 0))],
        out_specs=[pl.BlockSpec((1, 8, LANES), lambda i: (i, 0, 0)), pl.BlockSpec((8, TAB_LANES), lambda i: (0, 0))],
        out_shape=[jax.ShapeDtypeStruct((ntile, 8, LANES), I32), jax.ShapeDtypeStruct((8, TAB_LANES), I32)],
        compiler_params=_params(("arbitrary",)),
    )(seg, cnt, lower)


def _for_each_piece(length, max_len, fn):
    size = SEG_ALIGN
    sizes = []
    while size <= max_len:
        sizes.append(size)
        size *= 2
    for size in reversed(sizes):
        @pl.when(jnp.bitwise_and(length, size) != 0)
        def _(size=size):
            fn(pl.multiple_of(jnp.bitwise_and(length, -2 * size), SEG_ALIGN), size)


def _sorted_rows(td):
    return 2 * td + N_GROUPS * EXPERTS_PER_GROUP * SEG_ALIGN


def _dispatch_kernel(tm, seglen_ref, lstart_ref, gstart_ref, ltot_ref, plo_ref, phi_ref, nu_ref,
                     x_ref, route_ref, xbuf_ref, sorted_ref, zblk, sem, zsem):
    i = pl.program_id(0)
    nsteps = pl.num_programs(0)
    td = x_ref.shape[0]
    ne = plo_ref.shape[0]
    nblk = xbuf_ref.shape[0] // tm
    nrow = sorted_ref.shape[1]
    buf = i % 2

    lpos = route_ref[2:4, :]
    r = lax.broadcasted_iota(I32, (nrow, td), 0).astype(F32)
    onehot = jnp.where((r == lpos[0:1]) | (r == lpos[1:2]), 1.0, 0.0).astype(BF16)
    sorted_ref[buf] = _dg(onehot, x_ref[...].astype(BF16)).astype(BF16)

    def wait_tile(step, b):
        _for_each_piece(ltot_ref[step], nrow, lambda off, size: pltpu.make_async_copy(
            sorted_ref.at[b, pl.ds(0, size), :], xbuf_ref.at[pl.ds(0, size), :], sem.at[b]).wait())

    for e in range(ne):
        idx = i * ne + e
        ls = pl.multiple_of(lstart_ref[idx], SEG_ALIGN)
        gs = pl.multiple_of(gstart_ref[idx], SEG_ALIGN)
        _for_each_piece(seglen_ref[idx], td, lambda off, size: pltpu.make_async_copy(
            sorted_ref.at[buf, pl.ds(ls + off, size), :], xbuf_ref.at[pl.ds(gs + off, size), :],
            sem.at[buf]).start())

    @pl.when(i > 0)
    def _():
        wait_tile(i - 1, 1 - buf)

    @pl.when(i == nsteps - 1)
    def _():
        wait_tile(i, buf)

    def pad_fill(fn):
        for e in range(ne):
            lo = pl.multiple_of(plo_ref[e], SEG_ALIGN)
            _for_each_piece(phi_ref[e] - lo, tm // 2, lambda off, size: fn(pltpu.make_async_copy(
                zblk.at[pl.ds(0, size), :], xbuf_ref.at[pl.ds(lo + off, size), :], zsem)))

        def per_blk(b, carry):
            fn(pltpu.make_async_copy(zblk, xbuf_ref.at[pl.ds(pl.multiple_of(b * tm, tm), tm), :], zsem))
            return carry
        lax.fori_loop(nu_ref[0], nblk, per_blk, 0)

    @pl.when(i == 0)
    def _():
        zblk[...] = jnp.zeros_like(zblk)
        pad_fill(lambda cp: cp.start())
        pad_fill(lambda cp: cp.wait())


def _dispatch(seglen, lstart, gstart, ltot, pad_lo, pad_hi, n_used, x1, route, rows, tm):
    n, d = x1.shape
    td = TD_DISPATCH
    return pl.pallas_call(
        functools.partial(_dispatch_kernel, tm),
        grid_spec=pltpu.PrefetchScalarGridSpec(
            num_scalar_prefetch=7,
            grid=(n // td,),
            in_specs=[pl.BlockSpec((td, d), lambda i, *_: (i, 0)), pl.BlockSpec((8, td), lambda i, *_: (0, i))],
            out_specs=pl.BlockSpec(memory_space=pl.ANY),
            scratch_shapes=[pltpu.VMEM((2, _sorted_rows(td), d), BF16), pltpu.VMEM((tm, d), BF16),
                            pltpu.SemaphoreType.DMA((2,)), pltpu.SemaphoreType.DMA(())],
        ),
        out_shape=jax.ShapeDtypeStruct((rows, d), BF16),
        compiler_params=_params(("arbitrary",)),
    )(seglen, lstart, gstart, ltot, pad_lo, pad_hi, n_used, x1, route)


X_SLOTS = 3


def _expert_kernel(tm, bstart_ref, nb_ref, nu_ref, w1_ref, w3_ref, w2_ref, xbuf_ref, ybuf_ref,
                   wf1, wf3, wf2, w13b, w2b, xb, yb, semw, semx, semy):
    e = pl.program_id(0)
    ne = pl.num_programs(0)
    nb = nb_ref[e]
    b0 = bstart_ref[e]
    nblk = ybuf_ref.shape[0] // tm
    rows = lambda blk: pl.ds(pl.multiple_of(blk * tm, tm), tm)
    x_copy = lambda j, slot: pltpu.make_async_copy(xbuf_ref.at[rows(b0 + j), :], xb.at[slot], semx.at[slot])
    y_copy = lambda blk, slot: pltpu.make_async_copy(yb.at[slot], ybuf_ref.at[rows(blk), :], semy.at[slot])

    def w_copies(ex, slot):
        return [pltpu.make_async_copy(src.at[ex], dst.at[slot], semw.at[slot])
                for src, dst in ((w1_ref, wf1), (w3_ref, wf3), (w2_ref, wf2))]

    ws = e % 2

    @pl.when(e == 0)
    def _():
        for cp in w_copies(0, 0):
            cp.start()

    for j0 in range(X_SLOTS - 1):
        @pl.when(j0 < nb)
        def _(j0=j0):
            x_copy(j0, j0).start()

    for cp in w_copies(e, ws):
        cp.wait()
    de = w2b.shape[0]
    w13b[:, :de] = wf1[ws].astype(BF16)
    w13b[:, de:] = wf3[ws].astype(BF16)
    w2b[...] = wf2[ws].astype(BF16)

    @pl.when(e + 1 < ne)
    def _():
        for cp in w_copies(e + 1, 1 - ws):
            cp.start()

    def body(j, carry):
        slot = j % X_SLOTS
        yslot = j % 2
        x_copy(j, slot).wait()

        @pl.when(j + X_SLOTS - 1 < nb)
        def _():
            x_copy(j + X_SLOTS - 1, (j + X_SLOTS - 1) % X_SLOTS).start()

        @pl.when(j >= 2)
        def _():
            y_copy(b0 + j - 2, yslot).wait()

        x = xb[slot]
        h13 = _dg(x, w13b[...])
        h1 = h13[:, :de]
        h = (h1 * _sigmoid(h1)) * h13[:, de:]
        yb[yslot] = _dot(h, w2b[...]).astype(BF16)
        y_copy(b0 + j, yslot).start()
        return carry

    lax.fori_loop(0, nb, body, 0)

    @pl.when(nb >= 2)
    def _():
        y_copy(b0 + nb - 2, nb % 2).wait()

    @pl.when(nb >= 1)
    def _():
        y_copy(b0 + nb - 1, (nb - 1) % 2).wait()

    @pl.when(e == pl.num_programs(0) - 1)
    def _():
        yb[0] = jnp.zeros(yb.shape[1:], yb.dtype)

        def fill(fn):
            def per_blk(blk, carry):
                fn(y_copy(blk, 0))
                return carry
            lax.fori_loop(nu_ref[0], nblk, per_blk, 0)

        fill(lambda cp: cp.start())
        fill(lambda cp: cp.wait())


def _experts(bstart, nb, n_used, xbuf, w1, w3, w2):
    rows, d = xbuf.shape
    ne, _, de = w1.shape
    tm = TM_EXPERT
    return pl.pallas_call(
        functools.partial(_expert_kernel, tm),
        grid_spec=pltpu.PrefetchScalarGridSpec(
            num_scalar_prefetch=3,
            grid=(ne,),
            in_specs=[pl.BlockSpec(memory_space=pl.ANY)] * 4,
            out_specs=pl.BlockSpec(memory_space=pl.ANY),
            scratch_shapes=[pltpu.VMEM((2, d, de), F32), pltpu.VMEM((2, d, de), F32), pltpu.VMEM((2, de, d), F32),
                            pltpu.VMEM((d, 2 * de), BF16), pltpu.VMEM((de, d), BF16),
                            pltpu.VMEM((X_SLOTS, tm, d), BF16), pltpu.VMEM((2, tm, d), BF16),
                            pltpu.SemaphoreType.DMA((2,)), pltpu.SemaphoreType.DMA((X_SLOTS,)),
                            pltpu.SemaphoreType.DMA((2,))],
        ),
        out_shape=jax.ShapeDtypeStruct((rows, d), BF16),
        compiler_params=_params(("arbitrary",)),
    )(bstart, nb, n_used, w1, w3, w2, xbuf)


def _combine_kernel(alpha, seglen_ref, lstart_ref, gstart_ref, ltot_ref, x1_ref, rt_ref, p_ref, wpe_ref, wpg_ref,
                    g_ref, b_ref, ybuf_ref, out_ref, sorted_ref, sem):
    i = pl.program_id(0)
    nsteps = pl.num_programs(0)
    tc = x1_ref.shape[0]
    ne = N_GROUPS * EXPERTS_PER_GROUP
    nrow = sorted_ref.shape[1]

    def fetch(step, buf):
        for e in range(ne):
            idx = step * ne + e
            ls = pl.multiple_of(lstart_ref[idx], SEG_ALIGN)
            gs = pl.multiple_of(gstart_ref[idx], SEG_ALIGN)
            _for_each_piece(seglen_ref[idx], tc, lambda off, size: pltpu.make_async_copy(
                ybuf_ref.at[pl.ds(gs + off, size), :], sorted_ref.at[buf, pl.ds(ls + off, size), :],
                sem.at[buf]).start())

    @pl.when(i == 0)
    def _():
        sorted_ref[...] = jnp.zeros_like(sorted_ref)
        fetch(0, 0)

    @pl.when(i + 1 < nsteps)
    def _():
        fetch(i + 1, (i + 1) % 2)

    cur = i % 2
    _for_each_piece(ltot_ref[i], nrow, lambda off, size: pltpu.make_async_copy(
        ybuf_ref.at[pl.ds(0, size), :], sorted_ref.at[cur, pl.ds(0, size), :], sem.at[cur]).wait())

    rt = rt_ref[...]
    r = lax.broadcasted_iota(I32, (tc, nrow), 1).astype(F32)
    unsort = jnp.where(r == rt[:, 0:1], rt[:, 2:3], 0.0) + jnp.where(r == rt[:, 1:2], rt[:, 3:4], 0.0)
    ffn = _dg(unsort.astype(BF16), sorted_ref[cur])
    x2 = _layer_norm(alpha * x1_ref[...] + ffn, g_ref[...], b_ref[...])
    gate = _sigmoid(_dot(x2, wpg_ref[...]))
    out_ref[...] = x2 + gate * _dot(p_ref[...], wpe_ref[...])


def _combine(seglen, lstart, gstart, ltot, x1, route_t, pf, wpe, wpg, g, bta, ybuf, alpha):
    n, d = x1.shape
    tc = TD_DISPATCH
    tile = lambda a_: pl.BlockSpec((tc, a_.shape[1]), lambda i, *_: (i, 0))
    const = lambda a_: pl.BlockSpec(a_.shape, lambda i, *_: (0, 0))
    return pl.pallas_call(
        functools.partial(_combine_kernel, alpha),
        grid_spec=pltpu.PrefetchScalarGridSpec(
            num_scalar_prefetch=4,
            grid=(n // tc,),
            in_specs=[tile(x1), tile(route_t), tile(pf), const(wpe), const(wpg), const(g), const(bta),
                      pl.BlockSpec(memory_space=pl.ANY)],
            out_specs=pl.BlockSpec((tc, d), lambda i, *_: (i, 0)),
            scratch_shapes=[pltpu.VMEM((2, _sorted_rows(tc), d), BF16), pltpu.SemaphoreType.DMA((2,))],
        ),
        out_shape=jax.ShapeDtypeStruct((n, d), F32),
        compiler_params=_params(("arbitrary",)),
    )(seglen, lstart, gstart, ltot, x1, route_t, pf, wpe, wpg, g, bta, ybuf)


def _layer(x, p_i, w_in, rw_mu, rw_w0, rw_w_up, rw_a0, rw_a_up, rw_g_up, rw_k_k, rw_k_a, rw_r_k, rw_gn_w,
           rw_gn_b, w_a_out, hg_lb_logits, hg_norm_w, w_b_out, w_o, ln1_g, ln1_b, router_g_w, router_g_b,
           router_e_w, router_e_b, w1, w3, w2, ln2_g, ln2_b, w_pe, w_pg, alpha, layer):
    b, t, d = x.shape
    n = b * t
    rw_dim = rw_w0.shape[0]
    rw_cols = 3 * rw_dim + RW_DECAY_LORA + RW_A_LORA + RW_GATE_LORA
    hg_cols = 4 * hg_norm_w.shape[0]
    ne = N_GROUPS * EXPERTS_PER_GROUP
    row2 = lambda a_: a_.reshape(1, -1)
    xf = x.reshape(n, d)

    wb = w_in.astype(BF16)
    proj_rw, proj_hg, proj_gt = _project(xf, wb[:, :rw_cols], wb[:, rw_cols:rw_cols + hg_cols],
                                         wb[:, rw_cols + hg_cols:])
    ya = _rwkv_branch(proj_rw.reshape(b, t, rw_cols), rw_mu, row2(rw_w0), rw_w_up, rw_a0, rw_a_up, rw_g_up,
                      rw_k_k, rw_k_a, rw_r_k, rw_gn_w, rw_gn_b)
    yb = _hgrn_branch(proj_hg.reshape(b, t, hg_cols), hg_lb_logits, hg_norm_w, layer)

    wr = jnp.zeros((LANES, d), F32)
    wr = wr.at[:N_GROUPS].set(router_g_w.T).at[ROUTER_EXPERT_ROW:ROUTER_EXPERT_ROW + ne].set(router_e_w.T)
    bias = jnp.zeros((LANES,), F32)
    bias = bias.at[:N_GROUPS].set(router_g_b).at[ROUTER_EXPERT_ROW:ROUTER_EXPERT_ROW + ne].set(router_e_b)
    x1, lt = _merge(xf, ya.reshape(n, -1), yb.reshape(n, -1), proj_gt, w_a_out.astype(BF16),
                    w_b_out.astype(BF16), w_o.astype(BF16), row2(ln1_g), row2(ln1_b), wr, alpha)
    route, seg, cnt = _route(lt, bias.reshape(LANES, 1))

    tm = TM_EXPERT
    ntile = n // TD_DISPATCH
    nblk = -(-(2 * n + (SEG_ALIGN - 1) * ne * ntile) // tm) + ne
    assert TB_ROUTE == TD_DISPATCH
    segtab, tab = _finalize(seg, cnt, tm)
    per_seg = lambda row: segtab[:, row, :ne].reshape(-1)
    seglen, lstart, gstart, ltot = per_seg(0), per_seg(1), per_seg(2), segtab[:, 3, 0]
    n_used = tab[1, :1]

    xbuf = _dispatch(seglen, lstart, gstart, ltot, tab[2, :ne], tab[3, :ne], n_used, x1, route, nblk * tm, tm)
    ybuf = _experts(tab[0, :ne], tab[4, :ne], n_used, xbuf, w1, w3, w2)
    out = _combine(seglen, lstart, gstart, ltot, x1, route[2:6].T, p_i.reshape(n, -1), w_pe.astype(BF16),
                   w_pg.astype(BF16), row2(ln2_g), row2(ln2_b), ybuf, alpha)
    return out.reshape(b, t, d)


def kernel(x, p, w_in, rw_mu, rw_w0, rw_w_up, rw_a0, rw_a_up, rw_g_up, rw_k_k, rw_k_a, rw_r_k, rw_gn_w, rw_gn_b,
           w_a_out, hg_lb_logits, hg_norm_w, w_b_out, w_o, ln1_g, ln1_b, router_g_w, router_g_b, router_e_w,
           router_e_b, w1, w3, w2, ln2_g, ln2_b, w_pe, w_pg):
    depth = w_in.shape[0]
    alpha = (2 * depth) ** 0.25
    for i in range(depth):
        x = _layer(x, p[i], w_in[i], rw_mu[i], rw_w0[i], rw_w_up[i], rw_a0[i], rw_a_up[i], rw_g_up[i], rw_k_k[i],
                   rw_k_a[i], rw_r_k[i].reshape(-1), rw_gn_w[i], rw_gn_b[i], w_a_out[i], hg_lb_logits,
                   hg_norm_w[i], w_b_out[i], w_o[i], ln1_g[i], ln1_b[i],
                   router_g_w[i], router_g_b[i], router_e_w[i], router_e_b[i], w1[i], w3[i], w2[i], ln2_g[i],
                   ln2_b[i], w_pe[i], w_pg[i], alpha, i)
    return x
```

```python
import functools

import jax
import jax.numpy as jnp
from jax import lax
from jax.experimental import pallas as pl
from jax.experimental.pallas import tpu as pltpu

F32 = jnp.float32
BF16 = jnp.bfloat16
I32 = jnp.int32

NN = (((1,), (0,)), ((), ()))
NT = (((1,), (1,)), ((), ()))

RW_HEAD = 64
RW_DECAY_LORA = 64
RW_A_LORA = 64
RW_GATE_LORA = 128
RW_GN_EPS = 64e-5
RW_DECAY_SCALE = 0.6065306597126334
HG_HEADS = 4
N_GROUPS = 4
EXPERTS_PER_GROUP = 8
LN_EPS = 1e-5
RMS_EPS = 1e-6

CHUNK = 64
SUB = 16
LANES = 128
VMEM_LIMIT = 56 * 1024 * 1024

TM_PROJ = 256
TB_RWKV = 512
TB_HGRN = 256
TM_MERGE = 512
TB_ROUTE = 512
TD_DISPATCH = 512
SEG_ALIGN = 16
TM_EXPERT = 256
RW_WIDE = 4
HG_WIDE = 4


def _dg(a, b, dn=NN):
    return lax.dot_general(a, b, dn, preferred_element_type=F32)


def _dot(a, b, dn=NN):
    return _dg(a.astype(BF16), b.astype(BF16), dn)


def _split(a):
    hi = a.astype(BF16)
    lo = (a - hi.astype(F32)).astype(BF16)
    return hi, lo


def _dot_hl(a, b_exact, dn=NN):
    hi, lo = _split(a)
    return _dg(hi, b_exact, dn) + _dg(lo, b_exact, dn)


def _dot3(a, b, dn=NN):
    ah, al = _split(a)
    bh, bl = _split(b)
    return _dg(ah, bh, dn) + (_dg(ah, bl, dn) + _dg(al, bh, dn))


def _cumsum_chunks(x, tri):
    h = x.astype(BF16)
    r1 = x - h.astype(F32)
    m = r1.astype(BF16)
    l = (r1 - m.astype(F32)).astype(BF16)
    return _dg(tri, h) + (_dg(tri, m) + _dg(tri, l))


def _sigmoid(x):
    return 0.5 * jnp.tanh(0.5 * x) + 0.5


def _layer_norm(h, g, b):
    mu = jnp.mean(h, axis=-1, keepdims=True)
    d = h - mu
    var = jnp.mean(d * d, axis=-1, keepdims=True)
    return d * lax.rsqrt(var + LN_EPS) * g + b


def _params(sem):
    return pltpu.CompilerParams(dimension_semantics=sem, vmem_limit_bytes=VMEM_LIMIT)


def _proj_kernel(x_ref, wr_ref, wh_ref, wg_ref, pr_ref, ph_ref, pg_ref):
    xb = x_ref[...].astype(BF16)
    pr_ref[...] = _dg(xb, wr_ref[...])
    ph_ref[...] = _dg(xb, wh_ref[...])
    pg_ref[...] = _dg(xb, wg_ref[...])


def _project(xf, w_rw, w_hg, w_gt):
    n, d = xf.shape
    tm = TM_PROJ
    full = lambda w: pl.BlockSpec(w.shape, lambda i: (0, 0))
    tile = lambda c: pl.BlockSpec((tm, c), lambda i: (i, 0))
    return pl.pallas_call(
        _proj_kernel,
        grid=(n // tm,),
        in_specs=[tile(d), full(w_rw), full(w_hg), full(w_gt)],
        out_specs=[tile(w_rw.shape[1]), tile(w_hg.shape[1]), tile(w_gt.shape[1])],
        out_shape=[jax.ShapeDtypeStruct((n, w.shape[1]), F32) for w in (w_rw, w_hg, w_gt)],
        compiler_params=_params(("parallel",)),
    )(xf, w_rw, w_hg, w_gt)


def _each(f, *ls):
    return [f(*xs) for xs in zip(*ls)]


def _two(x):
    m1 = lax.broadcasted_iota(I32, x.shape, 1) < RW_HEAD
    return jnp.concatenate([jnp.where(m1, x, 0.0), jnp.where(m1, 0.0, x)], axis=0)


def _rwkv_chunk_prepare(ins, lvl_ref, out):
    c = CHUNK
    lane = lax.broadcasted_iota(I32, (c, LANES), 1)
    row = lax.broadcasted_iota(I32, (c, LANES), 0)
    scol = jnp.bitwise_and(lane, RW_HEAD - 1)
    strict = row > scol
    incl = row >= scol
    r2 = lax.broadcasted_iota(I32, (LANES, LANES), 0)
    c2 = lax.broadcasted_iota(I32, (LANES, LANES), 1)
    eye = jnp.where(r2 == c2, 1.0, 0.0).astype(F32)

    def prep(r, k, v, av, bv, lw, lc):
        l_end = lc[c - 1:c]
        e_r = jnp.exp(l_end - lc)
        e_n = jnp.exp(-lc)
        return dict(at=av * jnp.exp(lc - lw), rt=r * jnp.exp(lc), bt=bv * e_n, kt=k * e_n,
                    bk=jnp.concatenate([bv * e_r, k * e_r], axis=0), pc=jnp.exp(l_end), v=v)

    q = [prep(*xs) for xs in ins]
    yield
    p = [_dot(jnp.concatenate([d["at"], d["rt"]], axis=0),
              jnp.concatenate([_two(d["bt"]), _two(d["kt"])], axis=0), NT) for d in q]
    sab = [jnp.where(strict, x[:c, :LANES], 0.0) for x in p]
    sak = [jnp.where(strict, x[:c, LANES:], 0.0) for x in p]
    srb = [jnp.where(incl, x[c:, :LANES], 0.0) for x in p]
    srk = [jnp.where(incl, x[c:, LANES:], 0.0) for x in p]
    yield
    sv = _each(lambda ak, rk, d: _dot(jnp.concatenate([ak, rk], axis=0), _two(d["v"])), sak, srk, q)
    yield

    a_bd = [_two(x) for x in sab]
    t = [eye + a * lvl_ref[0] for a in a_bd]
    for lv in range(1, lvl_ref.shape[0]):
        ta = _each(lambda t_, a: _dot(t_, a * lvl_ref[lv]), t, a_bd)
        yield
        t = _each(lambda t_, ta_: t_ + _dot(ta_, t_), t, ta)
        yield

    x = _each(lambda t_, d, sv_: _dot(t_, jnp.concatenate([_two(d["at"]), _two(sv_[:c])], axis=1)), t, q, sv)
    out.extend(dict(uk=x_[:c, :LANES] + x_[c:, :LANES],
                    w=x_[:c, LANES:] + x_[c:, LANES:],
                    rt=d["rt"], rkv=sv_[c:], srb=srb_, bk=d["bk"], v=d["v"], pc=d["pc"])
               for x_, d, sv_, srb_ in zip(x, q, sv, srb))


def _rwkv_chunk_apply(prep, states, bd, out):
    c = CHUNK
    g1 = _each(lambda d, s: _dot(jnp.concatenate([d["uk"], d["rt"]], axis=0), s, NT), prep, states)
    yield
    u = _each(lambda g, d: g[:c] + d["w"], g1, prep)
    y = _each(lambda g, d, u_: g[c:] + d["rkv"] + _dot(d["srb"], _two(u_)), g1, prep, u)
    yield
    upd = _each(lambda u_, d: _dot(jnp.concatenate([u_, d["v"]], axis=0).T, d["bk"]), u, prep)
    s_new = _each(lambda d, s, up: s * d["pc"] + bd * up, prep, states, upd)
    out.extend(zip(y, s_new))
    yield


def _interleave(*gens):
    live = [g for g in gens if g is not None]
    while live:
        for g in list(live):
            try:
                next(g)
            except StopIteration:
                live.remove(g)


def _rwkv_kernel(u_ref, mu_ref, w0_ref, wup_ref, a0_ref, aup_ref, gup_ref, kk_ref, ka_ref, rk_ref,
                 gnw_ref, gnb_ref, tri_ref, gsum_ref, lvl_ref, bd_ref, ya_ref,
                 s_ref, prev_ref, r_s, k_s, v_s, a_s, b_s, lw_s, lc_s, g_s, y_s):
    tb = pl.program_id(1)

    @pl.when(tb == 0)
    def _():
        s_ref[...] = jnp.zeros_like(s_ref)
        prev_ref[...] = jnp.zeros_like(prev_ref)

    nt = u_ref.shape[1]
    dim = r_s.shape[1]
    part = CHUNK * RW_WIDE
    npair = dim // LANES
    lanes = [slice(p * LANES, (p + 1) * LANES) for p in range(npair)]
    bd = bd_ref[...]
    gsum = gsum_ref[...]

    def prologue(h):
        rows = slice(h * part, (h + 1) * part)
        u = u_ref[0, rows, :]
        before = prev_ref[...] if h == 0 else u_ref[0, h * part - 1:h * part, :]
        rowid = lax.broadcasted_iota(I32, u.shape, 0)
        shifted = jnp.where(rowid == 0, before, pltpu.roll(u, 1, axis=0))
        um = u + (shifted - u) * mu_ref[...]
        r = um[:, 0:dim]
        k = um[:, dim:2 * dim]
        v = um[:, 2 * dim:3 * dim]
        xwa = um[:, 3 * dim:3 * dim + LANES]
        xg = um[:, 3 * dim + LANES:3 * dim + 2 * LANES]
        yield
        wpre = w0_ref[...] + _dot(jnp.tanh(xwa), wup_ref[...])
        lw = -RW_DECAY_SCALE * _sigmoid(wpre)
        a = _sigmoid(a0_ref[...] + _dot(xwa, aup_ref[...]))
        g_s[rows, :] = _dot(_sigmoid(xg), gup_ref[...])
        yield
        kk = k * kk_ref[...]
        ss = _dot(kk * kk, gsum)
        kk = kk * lax.rsqrt(jnp.maximum(ss, 1e-24))
        r_s[rows, :] = r
        k_s[rows, :] = k * (1.0 + (a - 1.0) * ka_ref[...])
        v_s[rows, :] = v
        a_s[rows, :] = -kk
        b_s[rows, :] = kk * a
        yield
        lw_s[rows, :] = lw
        lc_s[rows, :] = _cumsum_chunks(lw, tri_ref[...])

    def chunk_rows(h):
        return [slice(h * part + ci * CHUNK, h * part + (ci + 1) * CHUNK) for ci in range(RW_WIDE)]

    def prepare(h, out):
        ins = [(r_s[rw, ls], k_s[rw, ls], v_s[rw, ls], a_s[rw, ls], b_s[rw, ls], lw_s[rw, ls], lc_s[rw, ls])
               for rw in chunk_rows(h) for ls in lanes]
        yield from _rwkv_chunk_prepare(ins, lvl_ref, out)

    def apply(h, prep):
        states = [s_ref[p] for p in range(npair)]
        for ci, rw in enumerate(chunk_rows(h)):
            outs = []
            yield from _rwkv_chunk_apply(prep[ci * npair:(ci + 1) * npair], states, bd, outs)
            states = [s_new for _, s_new in outs]
            for (y, _), ls in zip(outs, lanes):
                y_s[rw, ls] = y
        for p in range(npair):
            s_ref[p] = states[p]

    def epilogue(h):
        rows = slice(h * part, (h + 1) * part)
        y = y_s[rows, :]
        inv_n = 1.0 / RW_HEAD
        m = _dot_hl(y, gsum) * inv_n
        d = y - m
        yield
        var = _dot(d * d, gsum) * inv_n
        yn = d * lax.rsqrt(var + RW_GN_EPS) * gnw_ref[...] + gnb_ref[...]
        yield
        bonus = _dot(r_s[rows, :] * k_s[rows, :] * rk_ref[...], gsum) * v_s[rows, :]
        ya_ref[0, rows, :] = ((yn + bonus) * g_s[rows, :]).astype(BF16)

    nparts = nt // part
    preps = [[] for _ in range(nparts)]
    _interleave(prologue(0))
    prev_ref[...] = u_ref[0, nt - 1:nt, :]
    _interleave(prepare(0, preps[0]), prologue(1) if nparts > 1 else None)
    for h in range(nparts):
        _interleave(apply(h, preps[h]),
                    prepare(h + 1, preps[h + 1]) if h + 1 < nparts else None,
                    prologue(h + 2) if h + 2 < nparts else None,
                    epilogue(h - 1) if h > 0 else None)
    _interleave(epilogue(nparts - 1))


def _rwkv_branch(proj_rw, mu, w0, wup, a0, aup, gup, k_k, k_a, r_k, gn_w, gn_b):
    b, t, cols = proj_rw.shape
    dim = w0.shape[1]
    tb = TB_RWKV
    ii = jnp.arange(CHUNK * RW_WIDE)
    tri = ((ii[:, None] // CHUNK == ii[None, :] // CHUNK) & (ii[:, None] >= ii[None, :])).astype(BF16)
    jj = jnp.arange(dim)
    gsum = (jj[:, None] // RW_HEAD == jj[None, :] // RW_HEAD).astype(BF16)
    rr = jnp.arange(LANES)[:, None]
    cc = jnp.arange(LANES)[None, :]
    lvls = []
    s = 1
    while s < CHUNK:
        lvls.append(((rr // (2 * s) == cc // (2 * s)) & ((rr // s) % 2 == 1) & ((cc // s) % 2 == 0)).astype(F32))
        s *= 2
    lvl = jnp.stack(lvls)
    bd = (rr // RW_HEAD == cc // RW_HEAD).astype(F32)
    zpad = lambda rows: jnp.zeros((rows, dim), F32)
    wup_p = jnp.concatenate([wup, zpad(LANES - wup.shape[0])], axis=0).astype(BF16)
    aup_p = jnp.concatenate([zpad(LANES - aup.shape[0]), aup], axis=0).astype(BF16)
    row2 = lambda a_: a_.reshape(1, -1)
    const = lambda a_: pl.BlockSpec(a_.shape, lambda bi, ti: (0,) * a_.ndim)
    args = [row2(mu), row2(w0), wup_p, row2(a0), aup_p, gup.astype(BF16), row2(k_k), row2(k_a), row2(r_k),
            row2(gn_w), row2(gn_b), tri, gsum, lvl, bd]
    sc = lambda: pltpu.VMEM((tb, dim), F32)
    return pl.pallas_call(
        _rwkv_kernel,
        grid=(b, t // tb),
        in_specs=[pl.BlockSpec((1, tb, cols), lambda bi, ti: (bi, ti, 0))] + [const(a_) for a_ in args],
        out_specs=pl.BlockSpec((1, tb, dim), lambda bi, ti: (bi, ti, 0)),
        out_shape=jax.ShapeDtypeStruct((b, t, dim), BF16),
        scratch_shapes=[pltpu.VMEM((dim // LANES, LANES, LANES), F32), pltpu.VMEM((1, cols), F32)]
                       + [sc() for _ in range(9)],
        compiler_params=_params(("arbitrary", "arbitrary")),
    )(proj_rw, *args)


def _hgrn_chunk_prepare(ins):
    c = CHUNK
    subs = [(SUB * i, SUB * (i + 1)) for i in range(c // SUB)]

    def scores(q, k, lf, bc, lo, hi):
        m = bc[lo:lo + 1] - lf[lo:lo + 1]
        att = _dot(q[lo:hi] * jnp.exp(bc[lo:hi] - m), k[:hi] * jnp.exp(m - bc[:hi]), NT)
        tt = lax.broadcasted_iota(I32, (SUB, hi), 0) + lo
        s_ = lax.broadcasted_iota(I32, (SUB, hi), 1)
        return jnp.where(s_ <= tt, att, 0.0)

    att = [[scores(q, k, lf, bc, lo, hi) for lo, hi in subs] for q, k, v, lf, bc in ins]
    upd = [_dot(v.T, k * jnp.exp(bc[c - 1:c] - bc)) for q, k, v, lf, bc in ins]
    intra = [[_dot(a, x[2][:hi]) for a, (lo, hi) in zip(arow, subs)] for arow, x in zip(att, ins)]
    return [dict(intra=jnp.concatenate(rows, axis=0), qe=x[0] * jnp.exp(x[4]), upd=up, pc=jnp.exp(x[4][c - 1:c]))
            for rows, x, up in zip(intra, ins, upd)]


def _hgrn_chunk_apply(prep, states):
    o = _each(lambda d, st: d["intra"] + _dot(d["qe"], st, NT), prep, states)
    st_new = _each(lambda d, st: st * d["pc"] + d["upd"], prep, states)
    return list(zip(o, st_new))


def _hgrn_kernel(layer, u_ref, lbl_ref, nw_ref, tri_ref, gsum_ref, yb_ref,
                 st_ref, q_s, k_s, v_s, lf_s, bc_s, o_s):
    tb = pl.program_id(1)

    @pl.when(tb == 0)
    def _():
        st_ref[...] = jnp.zeros_like(st_ref)

    u = u_ref[0]
    nt = u.shape[0]
    dim = q_s.shape[1]
    lbl = lbl_ref[...]
    e = jnp.exp(lbl - jnp.max(lbl, axis=0, keepdims=True))
    lb = jnp.sum(e[0:layer + 1], axis=0, keepdims=True) / jnp.sum(e, axis=0, keepdims=True)
    zf = u[:, dim:2 * dim]
    sig = _sigmoid(zf)
    f = lb + (1.0 - lb) * sig
    qin = u[:, 0:dim]
    q_s[...] = qin * _sigmoid(qin)
    k_s[...] = (1.0 - lb) * (1.0 - sig)
    v_s[...] = u[:, 2 * dim:3 * dim]
    lf = jnp.log(f)
    lf_s[...] = lf
    bc_s[...] = _cumsum_chunks(lf, tri_ref[...])

    nhead = dim // LANES
    lanes = [slice(h * LANES, (h + 1) * LANES) for h in range(nhead)]

    def group_body(gi, carry):
        rows = [pl.ds(pl.multiple_of((gi * HG_WIDE + ci) * CHUNK, CHUNK), CHUNK) for ci in range(HG_WIDE)]
        ins = [(q_s[rw, ls], k_s[rw, ls], v_s[rw, ls], lf_s[rw, ls], bc_s[rw, ls]) for rw in rows for ls in lanes]
        prep = _hgrn_chunk_prepare(ins)
        states = [st_ref[h] for h in range(nhead)]
        for ci, rw in enumerate(rows):
            outs = _hgrn_chunk_apply(prep[ci * nhead:(ci + 1) * nhead], states)
            states = [st_new for _, st_new in outs]
            for (o, _), ls in zip(outs, lanes):
                o_s[rw, ls] = o
        for h in range(nhead):
            st_ref[h] = states[h]
        return carry

    lax.fori_loop(0, nt // (CHUNK * HG_WIDE), group_body, 0)

    o = o_s[...]
    ms = _dot(o * o, gsum_ref[...]) * (1.0 / LANES)
    og = u[:, 3 * dim:4 * dim]
    yb_ref[0] = (o * lax.rsqrt(ms + RMS_EPS) * nw_ref[...] * _sigmoid(og)).astype(BF16)


def _hgrn_branch(proj_hg, lb_logits, norm_w, layer):
    b, t, cols = proj_hg.shape
    dim = cols // 4
    tb = TB_HGRN
    ii = jnp.arange(tb)
    tri = ((ii[:, None] // CHUNK == ii[None, :] // CHUNK) & (ii[:, None] >= ii[None, :])).astype(BF16)
    jj = jnp.arange(dim)
    gsum = (jj[:, None] // LANES == jj[None, :] // LANES).astype(BF16)
    const = lambda a_: pl.BlockSpec(a_.shape, lambda bi, ti: (0,) * a_.ndim)
    args = [lb_logits, norm_w.reshape(1, -1), tri, gsum]
    sc = lambda: pltpu.VMEM((tb, dim), F32)
    return pl.pallas_call(
        functools.partial(_hgrn_kernel, layer),
        grid=(b, t // tb),
        in_specs=[pl.BlockSpec((1, tb, cols), lambda bi, ti: (bi, ti, 0))] + [const(a_) for a_ in args],
        out_specs=pl.BlockSpec((1, tb, dim), lambda bi, ti: (bi, ti, 0)),
        out_shape=jax.ShapeDtypeStruct((b, t, dim), BF16),
        scratch_shapes=[pltpu.VMEM((dim // LANES, LANES, LANES), F32)] + [sc() for _ in range(6)],
        compiler_params=_params(("arbitrary", "arbitrary")),
    )(proj_hg, *args)


def _merge_kernel(alpha, x_ref, ya_ref, yb_ref, pg_ref, wa_ref, wb_ref, wo_ref, g_ref, b_ref, wr_ref,
                  x1_ref, lt_ref):
    d = x_ref.shape[1]
    gates = _sigmoid(pg_ref[...])
    merged = gates[:, :d] * _dg(ya_ref[...], wa_ref[...]) + gates[:, d:] * _dg(yb_ref[...], wb_ref[...])
    h = alpha * x_ref[...] + _dot(merged, wo_ref[...])
    x1 = _layer_norm(h, g_ref[...], b_ref[...])
    x1_ref[...] = x1
    lt_ref[...] = _dot3(wr_ref[...], x1, NT)


def _merge(xf, ya, yb, pgate, wa, wb, wo, g, bta, wr, alpha):
    n, d = xf.shape
    tm = TM_MERGE
    tile = lambda a_: pl.BlockSpec((tm, a_.shape[1]), lambda i: (i, 0))
    const = lambda a_: pl.BlockSpec(a_.shape, lambda i: (0, 0))
    return pl.pallas_call(
        functools.partial(_merge_kernel, alpha),
        grid=(n // tm,),
        in_specs=[tile(xf), tile(ya), tile(yb), tile(pgate), const(wa), const(wb), const(wo), const(g),
                  const(bta), const(wr)],
        out_specs=[pl.BlockSpec((tm, d), lambda i: (i, 0)), pl.BlockSpec((LANES, tm), lambda i: (0, i))],
        out_shape=[jax.ShapeDtypeStruct((n, d), F32), jax.ShapeDtypeStruct((LANES, n), F32)],
        compiler_params=_params(("parallel",)),
    )(xf, ya, yb, pgate, wa, wb, wo, g, bta, wr)


ROUTER_EXPERT_ROW = 8


def _to_lanes(col, nl):
    ne = col.shape[0]
    diag = lax.broadcasted_iota(I32, (ne, nl), 0) == lax.broadcasted_iota(I32, (ne, nl), 1)
    return jnp.sum(jnp.where(diag, col, 0.0), axis=0, keepdims=True)


def _route_kernel(lt_ref, bias_ref, upper_ref, lower_ref, route_ref, seg_ref, cnt_ref, carry_ref):
    @pl.when(pl.program_id(0) == 0)
    def _():
        carry_ref[...] = jnp.zeros_like(carry_ref)

    ne = N_GROUPS * EXPERTS_PER_GROUP
    lt = lt_ref[...] + bias_ref[...]
    nb = lt.shape[1]
    neg = -jnp.inf
    lg = lt[0:8]
    rg = lax.broadcasted_iota(I32, (8, nb), 0).astype(F32)
    lg = jnp.where(rg < N_GROUPS, lg, neg)
    mg = jnp.max(lg, axis=0, keepdims=True)
    gidx = jnp.min(jnp.where(lg == mg, rg, 1e9), axis=0, keepdims=True)
    pg_sel = 1.0 / jnp.sum(jnp.exp(lg - mg), axis=0, keepdims=True)

    le = lt[ROUTER_EXPERT_ROW:ROUTER_EXPERT_ROW + ne]
    re = lax.broadcasted_iota(I32, (ne, nb), 0).astype(F32)
    in_group = jnp.floor(re * (1.0 / EXPERTS_PER_GROUP)) == gidx
    l1 = jnp.where(in_group, le, neg)
    m1 = jnp.max(l1, axis=0, keepdims=True)
    i1 = jnp.min(jnp.where(l1 == m1, re, 1e9), axis=0, keepdims=True)
    l2 = jnp.where(re == i1, neg, l1)
    m2 = jnp.max(l2, axis=0, keepdims=True)
    i2 = jnp.min(jnp.where(l2 == m2, re, 1e9), axis=0, keepdims=True)
    e2 = jnp.exp(m2 - m1)
    w1 = pg_sel / (1.0 + e2)
    w2 = pg_sel * e2 / (1.0 + e2)

    sel1 = re == i1
    sel2 = re == i2
    onehot = jnp.where(sel1 | sel2, 1.0, 0.0)
    before = _dg(onehot.astype(BF16), upper_ref[...])
    cnt_t = jnp.sum(onehot, axis=1, keepdims=True)
    seg = jnp.floor((cnt_t + (SEG_ALIGN - 1)) * (1.0 / SEG_ALIGN)) * SEG_ALIGN
    lstart = _dg(lower_ref[...], jnp.broadcast_to(seg, (ne, LANES)).astype(BF16))[:, 0:1]
    tot = lstart + before
    lpos1 = jnp.sum(jnp.where(sel1, tot, 0.0), axis=0, keepdims=True)
    lpos2 = jnp.sum(jnp.where(sel2, tot, 0.0), axis=0, keepdims=True)
    grel = carry_ref[...]
    carry = grel + seg
    carry_ref[...] = carry
    cnt_ref[...] = jnp.broadcast_to(carry, cnt_ref.shape)
    zero = jnp.zeros_like(w1)
    route_ref[...] = jnp.concatenate([i1, i2, lpos1, lpos2, w1, w2, zero, zero], axis=0)
    nl = seg_ref.shape[2]
    zl = jnp.zeros((1, nl), F32)
    ltot = jnp.broadcast_to(jnp.sum(seg, axis=0, keepdims=True), (1, nl))
    seg_ref[0] = jnp.concatenate([_to_lanes(seg, nl), _to_lanes(lstart, nl), _to_lanes(grel, nl), ltot,
                                  zl, zl, zl, zl], axis=0)


def _route(lt, bias_col):
    n = lt.shape[1]
    tb = TB_ROUTE
    ne = N_GROUPS * EXPERTS_PER_GROUP
    ii = jnp.arange(tb)
    upper = (ii[:, None] < ii[None, :]).astype(BF16)
    ee = jnp.arange(ne)
    lower = (ee[:, None] > ee[None, :]).astype(BF16)
    return pl.pallas_call(
        _route_kernel,
        grid=(n // tb,),
        in_specs=[pl.BlockSpec((LANES, tb), lambda i: (0, i)),
                  pl.BlockSpec((LANES, 1), lambda i: (0, 0)),
                  pl.BlockSpec((tb, tb), lambda i: (0, 0)),
                  pl.BlockSpec((ne, ne), lambda i: (0, 0))],
        out_specs=[pl.BlockSpec((8, tb), lambda i: (0, i)), pl.BlockSpec((1, 8, LANES), lambda i: (i, 0, 0)),
                   pl.BlockSpec((ne, LANES), lambda i: (0, 0))],
        out_shape=[jax.ShapeDtypeStruct((8, n), F32), jax.ShapeDtypeStruct((n // tb, 8, LANES), F32),
                   jax.ShapeDtypeStruct((ne, LANES), F32)],
        scratch_shapes=[pltpu.VMEM((ne, 1), F32)],
        compiler_params=_params(("arbitrary",)),
    )(lt, bias_col, upper, lower)


TAB_LANES = LANES


def _finalize_kernel(tm, seg_ref, cnt_ref, lower_ref, segtab_ref, tab_ref):
    ne = cnt_ref.shape[0]
    cnt = cnt_ref[...]
    nb = jnp.floor((cnt + (tm - 1)) * (1.0 / tm))
    bstart = _dg(lower_ref[...], nb.astype(BF16))
    bend = bstart + nb
    pad_start = bstart[:, 0:1] * tm
    seg = seg_ref[0]
    gstart = seg[2:3] + _to_lanes(pad_start, seg.shape[1])
    segtab_ref[0] = jnp.concatenate([seg[0:2], gstart, seg[3:8]], axis=0).astype(I32)

    nl = tab_ref.shape[1]
    n_used = jnp.max(bend[:, 0:1], axis=0, keepdims=True)
    pad_lo = _to_lanes(pad_start + cnt[:, 0:1], nl)
    pad_hi = _to_lanes(bend[:, 0:1] * tm, nl)
    zero = jnp.zeros((1, nl), F32)
    tab_ref[...] = jnp.concatenate([_to_lanes(bstart[:, 0:1], nl), jnp.broadcast_to(n_used, (1, nl)), pad_lo,
                                    pad_hi, _to_lanes(nb[:, 0:1], nl), zero, zero, zero], axis=0).astype(I32)


def _finalize(seg, cnt, tm):
    ntile = seg.shape[0]
    ne = cnt.shape[0]
    ii = jnp.arange(ne)
    lower = (ii[:, None] > ii[None, :]).astype(BF16)
    return pl.pallas_call(
        functools.partial(_finalize_kernel, tm),
        grid=(ntile,),
        in_specs=[pl.BlockSpec((1, 8, LANES), lambda i: (i, 0, 0)), pl.BlockSpec(cnt.shape, lambda i: (0, 0)),
                  pl.BlockSpec((ne, ne), lambda i: (0, 0))],
        out_specs=[pl.BlockSpec((1, 8, LANES), lambda i: (i, 0, 0)), pl.BlockSpec((8, TAB_LANES), lambda i: (0, 0))],
        out_shape=[jax.ShapeDtypeStruct((ntile, 8, LANES), I32), jax.ShapeDtypeStruct((8, TAB_LANES), I32)],
        compiler_params=_params(("arbitrary",)),
    )(seg, cnt, lower)


def _for_each_piece(length, max_len, fn):
    size = SEG_ALIGN
    sizes = []
    while size <= max_len:
        sizes.append(size)
        size *= 2
    for size in reversed(sizes):
        @pl.when(jnp.bitwise_and(length, size) != 0)
        def _(size=size):
            fn(pl.multiple_of(jnp.bitwise_and(length, -2 * size), SEG_ALIGN), size)


def _sorted_rows(td):
    return 2 * td + N_GROUPS * EXPERTS_PER_GROUP * SEG_ALIGN


def _dispatch_kernel(tm, seglen_ref, lstart_ref, gstart_ref, ltot_ref, plo_ref, phi_ref, nu_ref,
                     x_ref, route_ref, xbuf_ref, sorted_ref, zblk, sem, zsem):
    i = pl.program_id(0)
    nsteps = pl.num_programs(0)
    td = x_ref.shape[0]
    ne = plo_ref.shape[0]
    nblk = xbuf_ref.shape[0] // tm
    nrow = sorted_ref.shape[1]
    buf = i % 2

    lpos = route_ref[2:4, :]
    r = lax.broadcasted_iota(I32, (nrow, td), 0).astype(F32)
    onehot = jnp.where((r == lpos[0:1]) | (r == lpos[1:2]), 1.0, 0.0).astype(BF16)
    sorted_ref[buf] = _dg(onehot, x_ref[...].astype(BF16)).astype(BF16)

    def wait_tile(step, b):
        _for_each_piece(ltot_ref[step], nrow, lambda off, size: pltpu.make_async_copy(
            sorted_ref.at[b, pl.ds(0, size), :], xbuf_ref.at[pl.ds(0, size), :], sem.at[b]).wait())

    for e in range(ne):
        idx = i * ne + e
        ls = pl.multiple_of(lstart_ref[idx], SEG_ALIGN)
        gs = pl.multiple_of(gstart_ref[idx], SEG_ALIGN)
        _for_each_piece(seglen_ref[idx], td, lambda off, size: pltpu.make_async_copy(
            sorted_ref.at[buf, pl.ds(ls + off, size), :], xbuf_ref.at[pl.ds(gs + off, size), :],
            sem.at[buf]).start())

    @pl.when(i > 0)
    def _():
        wait_tile(i - 1, 1 - buf)

    @pl.when(i == nsteps - 1)
    def _():
        wait_tile(i, buf)

    def pad_fill(fn):
        for e in range(ne):
            lo = pl.multiple_of(plo_ref[e], SEG_ALIGN)
            _for_each_piece(phi_ref[e] - lo, tm // 2, lambda off, size: fn(pltpu.make_async_copy(
                zblk.at[pl.ds(0, size), :], xbuf_ref.at[pl.ds(lo + off, size), :], zsem)))

        def per_blk(b, carry):
            fn(pltpu.make_async_copy(zblk, xbuf_ref.at[pl.ds(pl.multiple_of(b * tm, tm), tm), :], zsem))
            return carry
        lax.fori_loop(nu_ref[0], nblk, per_blk, 0)

    @pl.when(i == 0)
    def _():
        zblk[...] = jnp.zeros_like(zblk)
        pad_fill(lambda cp: cp.start())
        pad_fill(lambda cp: cp.wait())


def _dispatch(seglen, lstart, gstart, ltot, pad_lo, pad_hi, n_used, x1, route, rows, tm):
    n, d = x1.shape
    td = TD_DISPATCH
    return pl.pallas_call(
        functools.partial(_dispatch_kernel, tm),
        grid_spec=pltpu.PrefetchScalarGridSpec(
            num_scalar_prefetch=7,
            grid=(n // td,),
            in_specs=[pl.BlockSpec((td, d), lambda i, *_: (i, 0)), pl.BlockSpec((8, td), lambda i, *_: (0, i))],
            out_specs=pl.BlockSpec(memory_space=pl.ANY),
            scratch_shapes=[pltpu.VMEM((2, _sorted_rows(td), d), BF16), pltpu.VMEM((tm, d), BF16),
                            pltpu.SemaphoreType.DMA((2,)), pltpu.SemaphoreType.DMA(())],
        ),
        out_shape=jax.ShapeDtypeStruct((rows, d), BF16),
        compiler_params=_params(("arbitrary",)),
    )(seglen, lstart, gstart, ltot, pad_lo, pad_hi, n_used, x1, route)


X_SLOTS = 3


def _expert_kernel(tm, bstart_ref, nb_ref, nu_ref, w1_ref, w3_ref, w2_ref, xbuf_ref, ybuf_ref,
                   wf1, wf3, wf2, w13b, w2b, xb, yb, semw, semx, semy):
    e = pl.program_id(0)
    ne = pl.num_programs(0)
    nb = nb_ref[e]
    b0 = bstart_ref[e]
    nblk = ybuf_ref.shape[0] // tm
    rows = lambda blk: pl.ds(pl.multiple_of(blk * tm, tm), tm)
    x_copy = lambda j, slot: pltpu.make_async_copy(xbuf_ref.at[rows(b0 + j), :], xb.at[slot], semx.at[slot])
    y_copy = lambda blk, slot: pltpu.make_async_copy(yb.at[slot], ybuf_ref.at[rows(blk), :], semy.at[slot])

    def w_copies(ex, slot):
        return [pltpu.make_async_copy(src.at[ex], dst.at[slot], semw.at[slot])
                for src, dst in ((w1_ref, wf1), (w3_ref, wf3), (w2_ref, wf2))]

    ws = e % 2

    @pl.when(e == 0)
    def _():
        for cp in w_copies(0, 0):
            cp.start()

    for j0 in range(X_SLOTS - 1):
        @pl.when(j0 < nb)
        def _(j0=j0):
            x_copy(j0, j0).start()

    for cp in w_copies(e, ws):
        cp.wait()
    de = w2b.shape[0]
    w13b[:, :de] = wf1[ws].astype(BF16)
    w13b[:, de:] = wf3[ws].astype(BF16)
    w2b[...] = wf2[ws].astype(BF16)

    @pl.when(e + 1 < ne)
    def _():
        for cp in w_copies(e + 1, 1 - ws):
            cp.start()

    def body(j, carry):
        slot = j % X_SLOTS
        yslot = j % 2
        x_copy(j, slot).wait()

        @pl.when(j + X_SLOTS - 1 < nb)
        def _():
            x_copy(j + X_SLOTS - 1, (j + X_SLOTS - 1) % X_SLOTS).start()

        @pl.when(j >= 2)
        def _():
            y_copy(b0 + j - 2, yslot).wait()

        x = xb[slot]
        h13 = _dg(x, w13b[...])
        h1 = h13[:, :de]
        h = (h1 * _sigmoid(h1)) * h13[:, de:]
        yb[yslot] = _dot(h, w2b[...]).astype(BF16)
        y_copy(b0 + j, yslot).start()
        return carry

    lax.fori_loop(0, nb, body, 0)

    @pl.when(nb >= 2)
    def _():
        y_copy(b0 + nb - 2, nb % 2).wait()

    @pl.when(nb >= 1)
    def _():
        y_copy(b0 + nb - 1, (nb - 1) % 2).wait()

    @pl.when(e == pl.num_programs(0) - 1)
    def _():
        yb[0] = jnp.zeros(yb.shape[1:], yb.dtype)

        def fill(fn):
            def per_blk(blk, carry):
                fn(y_copy(blk, 0))
                return carry
            lax.fori_loop(nu_ref[0], nblk, per_blk, 0)

        fill(lambda cp: cp.start())
        fill(lambda cp: cp.wait())


def _experts(bstart, nb, n_used, xbuf, w1, w3, w2):
    rows, d = xbuf.shape
    ne, _, de = w1.shape
    tm = TM_EXPERT
    return pl.pallas_call(
        functools.partial(_expert_kernel, tm),
        grid_spec=pltpu.PrefetchScalarGridSpec(
            num_scalar_prefetch=3,
            grid=(ne,),
            in_specs=[pl.BlockSpec(memory_space=pl.ANY)] * 4,
            out_specs=pl.BlockSpec(memory_space=pl.ANY),
            scratch_shapes=[pltpu.VMEM((2, d, de), F32), pltpu.VMEM((2, d, de), F32), pltpu.VMEM((2, de, d), F32),
                            pltpu.VMEM((d, 2 * de), BF16), pltpu.VMEM((de, d), BF16),
                            pltpu.VMEM((X_SLOTS, tm, d), BF16), pltpu.VMEM((2, tm, d), BF16),
                            pltpu.SemaphoreType.DMA((2,)), pltpu.SemaphoreType.DMA((X_SLOTS,)),
                            pltpu.SemaphoreType.DMA((2,))],
        ),
        out_shape=jax.ShapeDtypeStruct((rows, d), BF16),
        compiler_params=_params(("arbitrary",)),
    )(bstart, nb, n_used, w1, w3, w2, xbuf)


def _combine_kernel(alpha, seglen_ref, lstart_ref, gstart_ref, ltot_ref, x1_ref, rt_ref, p_ref, wpe_ref, wpg_ref,
                    g_ref, b_ref, ybuf_ref, out_ref, sorted_ref, sem):
    i = pl.program_id(0)
    nsteps = pl.num_programs(0)
    tc = x1_ref.shape[0]
    ne = N_GROUPS * EXPERTS_PER_GROUP
    nrow = sorted_ref.shape[1]

    def fetch(step, buf):
        for e in range(ne):
            idx = step * ne + e
            ls = pl.multiple_of(lstart_ref[idx], SEG_ALIGN)
            gs = pl.multiple_of(gstart_ref[idx], SEG_ALIGN)
            _for_each_piece(seglen_ref[idx], tc, lambda off, size: pltpu.make_async_copy(
                ybuf_ref.at[pl.ds(gs + off, size), :], sorted_ref.at[buf, pl.ds(ls + off, size), :],
                sem.at[buf]).start())

    @pl.when(i == 0)
    def _():
        sorted_ref[...] = jnp.zeros_like(sorted_ref)
        fetch(0, 0)

    @pl.when(i + 1 < nsteps)
    def _():
        fetch(i + 1, (i + 1) % 2)

    cur = i % 2
    _for_each_piece(ltot_ref[i], nrow, lambda off, size: pltpu.make_async_copy(
        ybuf_ref.at[pl.ds(0, size), :], sorted_ref.at[cur, pl.ds(0, size), :], sem.at[cur]).wait())

    rt = rt_ref[...]
    r = lax.broadcasted_iota(I32, (tc, nrow), 1).astype(F32)
    unsort = jnp.where(r == rt[:, 0:1], rt[:, 2:3], 0.0) + jnp.where(r == rt[:, 1:2], rt[:, 3:4], 0.0)
    ffn = _dg(unsort.astype(BF16), sorted_ref[cur])
    x2 = _layer_norm(alpha * x1_ref[...] + ffn, g_ref[...], b_ref[...])
    gate = _sigmoid(_dot(x2, wpg_ref[...]))
    out_ref[...] = x2 + gate * _dot(p_ref[...], wpe_ref[...])


def _combine(seglen, lstart, gstart, ltot, x1, route_t, pf, wpe, wpg, g, bta, ybuf, alpha):
    n, d = x1.shape
    tc = TD_DISPATCH
    tile = lambda a_: pl.BlockSpec((tc, a_.shape[1]), lambda i, *_: (i, 0))
    const = lambda a_: pl.BlockSpec(a_.shape, lambda i, *_: (0, 0))
    return pl.pallas_call(
        functools.partial(_combine_kernel, alpha),
        grid_spec=pltpu.PrefetchScalarGridSpec(
            num_scalar_prefetch=4,
            grid=(n // tc,),
            in_specs=[tile(x1), tile(route_t), tile(pf), const(wpe), const(wpg), const(g), const(bta),
                      pl.BlockSpec(memory_space=pl.ANY)],
            out_specs=pl.BlockSpec((tc, d), lambda i, *_: (i, 0)),
            scratch_shapes=[pltpu.VMEM((2, _sorted_rows(tc), d), BF16), pltpu.SemaphoreType.DMA((2,))],
        ),
        out_shape=jax.ShapeDtypeStruct((n, d), F32),
        compiler_params=_params(("arbitrary",)),
    )(seglen, lstart, gstart, ltot, x1, route_t, pf, wpe, wpg, g, bta, ybuf)


def _layer(x, p_i, w_in, rw_mu, rw_w0, rw_w_up, rw_a0, rw_a_up, rw_g_up, rw_k_k, rw_k_a, rw_r_k, rw_gn_w,
           rw_gn_b, w_a_out, hg_lb_logits, hg_norm_w, w_b_out, w_o, ln1_g, ln1_b, router_g_w, router_g_b,
           router_e_w, router_e_b, w1, w3, w2, ln2_g, ln2_b, w_pe, w_pg, alpha, layer):
    b, t, d = x.shape
    n = b * t
    rw_dim = rw_w0.shape[0]
    rw_cols = 3 * rw_dim + RW_DECAY_LORA + RW_A_LORA + RW_GATE_LORA
    hg_cols = 4 * hg_norm_w.shape[0]
    ne = N_GROUPS * EXPERTS_PER_GROUP
    row2 = lambda a_: a_.reshape(1, -1)
    xf = x.reshape(n, d)

    wb = w_in.astype(BF16)
    proj_rw, proj_hg, proj_gt = _project(xf, wb[:, :rw_cols], wb[:, rw_cols:rw_cols + hg_cols],
                                         wb[:, rw_cols + hg_cols:])
    ya = _rwkv_branch(proj_rw.reshape(b, t, rw_cols), rw_mu, row2(rw_w0), rw_w_up, rw_a0, rw_a_up, rw_g_up,
                      rw_k_k, rw_k_a, rw_r_k, rw_gn_w, rw_gn_b)
    yb = _hgrn_branch(proj_hg.reshape(b, t, hg_cols), hg_lb_logits, hg_norm_w, layer)

    wr = jnp.zeros((LANES, d), F32)
    wr = wr.at[:N_GROUPS].set(router_g_w.T).at[ROUTER_EXPERT_ROW:ROUTER_EXPERT_ROW + ne].set(router_e_w.T)
    bias = jnp.zeros((LANES,), F32)
    bias = bias.at[:N_GROUPS].set(router_g_b).at[ROUTER_EXPERT_ROW:ROUTER_EXPERT_ROW + ne].set(router_e_b)
    x1, lt = _merge(xf, ya.reshape(n, -1), yb.reshape(n, -1), proj_gt, w_a_out.astype(BF16),
                    w_b_out.astype(BF16), w_o.astype(BF16), row2(ln1_g), row2(ln1_b), wr, alpha)
    route, seg, cnt = _route(lt, bias.reshape(LANES, 1))

    tm = TM_EXPERT
    ntile = n // TD_DISPATCH
    nblk = -(-(2 * n + (SEG_ALIGN - 1) * ne * ntile) // tm) + ne
    assert TB_ROUTE == TD_DISPATCH
    segtab, tab = _finalize(seg, cnt, tm)
    per_seg = lambda row: segtab[:, row, :ne].reshape(-1)
    seglen, lstart, gstart, ltot = per_seg(0), per_seg(1), per_seg(2), segtab[:, 3, 0]
    n_used = tab[1, :1]

    xbuf = _dispatch(seglen, lstart, gstart, ltot, tab[2, :ne], tab[3, :ne], n_used, x1, route, nblk * tm, tm)
    ybuf = _experts(tab[0, :ne], tab[4, :ne], n_used, xbuf, w1, w3, w2)
    out = _combine(seglen, lstart, gstart, ltot, x1, route[2:6].T, p_i.reshape(n, -1), w_pe.astype(BF16),
                   w_pg.astype(BF16), row2(ln2_g), row2(ln2_b), ybuf, alpha)
    return out.reshape(b, t, d)


def kernel(x, p, w_in, rw_mu, rw_w0, rw_w_up, rw_a0, rw_a_up, rw_g_up, rw_k_k, rw_k_a, rw_r_k, rw_gn_w, rw_gn_b,
           w_a_out, hg_lb_logits, hg_norm_w, w_b_out, w_o, ln1_g, ln1_b, router_g_w, router_g_b, router_e_w,
           router_e_b, w1, w3, w2, ln2_g, ln2_b, w_pe, w_pg):
    depth = w_in.shape[0]
    alpha = (2 * depth) ** 0.25
    for i in range(depth):
        x = _layer(x, p[i], w_in[i], rw_mu[i], rw_w0[i], rw_w_up[i], rw_a0[i], rw_a_up[i], rw_g_up[i], rw_k_k[i],
                   rw_k_a[i], rw_r_k[i].reshape(-1), rw_gn_w[i], rw_gn_b[i], w_a_out[i], hg_lb_logits,
                   hg_norm_w[i], w_b_out[i], w_o[i], ln1_g[i], ln1_b[i],
                   router_g_w[i], router_g_b[i], router_e_w[i], router_e_b[i], w1[i], w3[i], w2[i], ln2_g[i],
                   ln2_b[i], w_pe[i], w_pg[i], alpha, i)
    return x
```

```python
import functools

import jax
import jax.numpy as jnp
from jax import lax
from jax.experimental import pallas as pl
from jax.experimental.pallas import tpu as pltpu

F32 = jnp.float32
BF16 = jnp.bfloat16
I32 = jnp.int32

NN = (((1,), (0,)), ((), ()))
NT = (((1,), (1,)), ((), ()))

RW_HEAD = 64
RW_DECAY_LORA = 64
RW_A_LORA = 64
RW_GATE_LORA = 128
RW_GN_EPS = 64e-5
RW_DECAY_SCALE = 0.6065306597126334
HG_HEADS = 4
N_GROUPS = 4
EXPERTS_PER_GROUP = 8
LN_EPS = 1e-5
RMS_EPS = 1e-6

CHUNK = 64
SUB = 16
LANES = 128
VMEM_LIMIT = 56 * 1024 * 1024

TM_PROJ = 256
TB_RWKV = 512
TB_HGRN = 256
TM_MERGE = 512
MERGE_ROWS = 256
TB_ROUTE = 512
TD_DISPATCH = 512
SEG_ALIGN = 16
TM_EXPERT = 256
RW_WIDE = 4
HG_WIDE = 4


def _dg(a, b, dn=NN):
    return lax.dot_general(a, b, dn, preferred_element_type=F32)


def _dot(a, b, dn=NN):
    return _dg(a.astype(BF16), b.astype(BF16), dn)


def _split(a):
    hi = a.astype(BF16)
    lo = (a - hi.astype(F32)).astype(BF16)
    return hi, lo


def _dot_hl(a, b_exact, dn=NN):
    hi, lo = _split(a)
    return _dg(hi, b_exact, dn) + _dg(lo, b_exact, dn)


def _dot3(a, b, dn=NN):
    ah, al = _split(a)
    bh, bl = _split(b)
    return _dg(ah, bh, dn) + (_dg(ah, bl, dn) + _dg(al, bh, dn))


def _cumsum_chunks(x, tri):
    h = x.astype(BF16)
    r1 = x - h.astype(F32)
    m = r1.astype(BF16)
    l = (r1 - m.astype(F32)).astype(BF16)
    return _dg(tri, h) + (_dg(tri, m) + _dg(tri, l))


def _sigmoid(x):
    return 0.5 * jnp.tanh(0.5 * x) + 0.5


def _layer_norm(h, g, b):
    mu = jnp.mean(h, axis=-1, keepdims=True)
    d = h - mu
    var = jnp.mean(d * d, axis=-1, keepdims=True)
    return d * lax.rsqrt(var + LN_EPS) * g + b


def _params(sem):
    return pltpu.CompilerParams(dimension_semantics=sem, vmem_limit_bytes=VMEM_LIMIT)


def _proj_kernel(x_ref, wr_ref, wh_ref, wg_ref, pr_ref, ph_ref, pg_ref):
    xb = x_ref[...].astype(BF16)
    pr_ref[...] = _dg(xb, wr_ref[...])
    ph_ref[...] = _dg(xb, wh_ref[...])
    pg_ref[...] = _dg(xb, wg_ref[...])


def _project(xf, w_rw, w_hg, w_gt):
    n, d = xf.shape
    tm = TM_PROJ
    full = lambda w: pl.BlockSpec(w.shape, lambda i: (0, 0))
    tile = lambda c: pl.BlockSpec((tm, c), lambda i: (i, 0))
    return pl.pallas_call(
        _proj_kernel,
        grid=(n // tm,),
        in_specs=[tile(d), full(w_rw), full(w_hg), full(w_gt)],
        out_specs=[tile(w_rw.shape[1]), tile(w_hg.shape[1]), tile(w_gt.shape[1])],
        out_shape=[jax.ShapeDtypeStruct((n, w.shape[1]), F32) for w in (w_rw, w_hg, w_gt)],
        compiler_params=_params(("parallel",)),
    )(xf, w_rw, w_hg, w_gt)


def _each(f, *ls):
    return [f(*xs) for xs in zip(*ls)]


def _two(x):
    m1 = lax.broadcasted_iota(I32, x.shape, 1) < RW_HEAD
    return jnp.concatenate([jnp.where(m1, x, 0.0), jnp.where(m1, 0.0, x)], axis=0)


def _rwkv_chunk_prepare(ins, lvl_ref, out):
    c = CHUNK
    lane = lax.broadcasted_iota(I32, (c, LANES), 1)
    row = lax.broadcasted_iota(I32, (c, LANES), 0)
    scol = jnp.bitwise_and(lane, RW_HEAD - 1)
    strict = row > scol
    incl = row >= scol
    r2 = lax.broadcasted_iota(I32, (LANES, LANES), 0)
    c2 = lax.broadcasted_iota(I32, (LANES, LANES), 1)
    eye = jnp.where(r2 == c2, 1.0, 0.0).astype(F32)

    def prep(r, k, v, av, bv, lw, lc):
        l_end = lc[c - 1:c]
        e_r = jnp.exp(l_end - lc)
        e_n = jnp.exp(-lc)
        return dict(at=av * jnp.exp(lc - lw), rt=r * jnp.exp(lc), bt=bv * e_n, kt=k * e_n,
                    bk=jnp.concatenate([bv * e_r, k * e_r], axis=0), pc=jnp.exp(l_end), v=v)

    q = [prep(*xs) for xs in ins]
    yield
    p = [_dot(jnp.concatenate([d["at"], d["rt"]], axis=0),
              jnp.concatenate([_two(d["bt"]), _two(d["kt"])], axis=0), NT) for d in q]
    sab = [jnp.where(strict, x[:c, :LANES], 0.0) for x in p]
    sak = [jnp.where(strict, x[:c, LANES:], 0.0) for x in p]
    srb = [jnp.where(incl, x[c:, :LANES], 0.0) for x in p]
    srk = [jnp.where(incl, x[c:, LANES:], 0.0) for x in p]
    yield
    sv = _each(lambda ak, rk, d: _dot(jnp.concatenate([ak, rk], axis=0), _two(d["v"])), sak, srk, q)
    yield

    a_bd = [_two(x) for x in sab]
    t = [eye + a * lvl_ref[0] for a in a_bd]
    for lv in range(1, lvl_ref.shape[0]):
        ta = _each(lambda t_, a: _dot(t_, a * lvl_ref[lv]), t, a_bd)
        yield
        t = _each(lambda t_, ta_: t_ + _dot(ta_, t_), t, ta)
        yield

    x = _each(lambda t_, d, sv_: _dot(t_, jnp.concatenate([_two(d["at"]), _two(sv_[:c])], axis=1)), t, q, sv)
    out.extend(dict(uk=x_[:c, :LANES] + x_[c:, :LANES],
                    w=x_[:c, LANES:] + x_[c:, LANES:],
                    rt=d["rt"], rkv=sv_[c:], srb=srb_, bk=d["bk"], v=d["v"], pc=d["pc"])
               for x_, d, sv_, srb_ in zip(x, q, sv, srb))


def _rwkv_chunk_apply(prep, states, bd, out):
    c = CHUNK
    g1 = _each(lambda d, s: _dot(jnp.concatenate([d["uk"], d["rt"]], axis=0), s, NT), prep, states)
    yield
    u = _each(lambda g, d: g[:c] + d["w"], g1, prep)
    y = _each(lambda g, d, u_: g[c:] + d["rkv"] + _dot(d["srb"], _two(u_)), g1, prep, u)
    yield
    upd = _each(lambda u_, d: _dot(jnp.concatenate([u_, d["v"]], axis=0).T, d["bk"]), u, prep)
    s_new = _each(lambda d, s, up: s * d["pc"] + bd * up, prep, states, upd)
    out.extend(zip(y, s_new))
    yield


def _skewed(gens):
    live = set(range(len(gens)))
    tick = 0
    while live:
        for q in sorted(live):
            if q <= tick:
                try:
                    next(gens[q])
                except StopIteration:
                    live.discard(q)
        tick += 1


def _interleave(*gens):
    live = [g for g in gens if g is not None]
    while live:
        for g in list(live):
            try:
                next(g)
            except StopIteration:
                live.remove(g)


def _rwkv_kernel(u_ref, mu_ref, w0_ref, wup_ref, a0_ref, aup_ref, gup_ref, kk_ref, ka_ref, rk_ref,
                 gnw_ref, gnb_ref, tri_ref, gsum_ref, lvl_ref, bd_ref, ya_ref,
                 s_ref, prev_ref, r_s, k_s, v_s, a_s, b_s, lw_s, lc_s, g_s, y_s):
    tb = pl.program_id(1)

    @pl.when(tb == 0)
    def _():
        s_ref[...] = jnp.zeros_like(s_ref)
        prev_ref[...] = jnp.zeros_like(prev_ref)

    nt = u_ref.shape[1]
    dim = r_s.shape[1]
    part = CHUNK * RW_WIDE
    npair = dim // LANES
    lanes = [slice(p * LANES, (p + 1) * LANES) for p in range(npair)]
    bd = bd_ref[...]
    gsum = gsum_ref[...]

    def prologue(h):
        rows = slice(h * part, (h + 1) * part)
        u = u_ref[0, rows, :]
        before = prev_ref[...] if h == 0 else u_ref[0, h * part - 1:h * part, :]
        rowid = lax.broadcasted_iota(I32, u.shape, 0)
        shifted = jnp.where(rowid == 0, before, pltpu.roll(u, 1, axis=0))
        um = u + (shifted - u) * mu_ref[...]
        r = um[:, 0:dim]
        k = um[:, dim:2 * dim]
        v = um[:, 2 * dim:3 * dim]
        xwa = um[:, 3 * dim:3 * dim + LANES]
        xg = um[:, 3 * dim + LANES:3 * dim + 2 * LANES]
        yield
        wpre = w0_ref[...] + _dot(jnp.tanh(xwa), wup_ref[...])
        lw = -RW_DECAY_SCALE * _sigmoid(wpre)
        a = _sigmoid(a0_ref[...] + _dot(xwa, aup_ref[...]))
        g_s[rows, :] = _dot(_sigmoid(xg), gup_ref[...])
        yield
        kk = k * kk_ref[...]
        ss = _dot(kk * kk, gsum)
        kk = kk * lax.rsqrt(jnp.maximum(ss, 1e-24))
        r_s[rows, :] = r
        k_s[rows, :] = k * (1.0 + (a - 1.0) * ka_ref[...])
        v_s[rows, :] = v
        a_s[rows, :] = -kk
        b_s[rows, :] = kk * a
        yield
        lw_s[rows, :] = lw
        lc_s[rows, :] = _cumsum_chunks(lw, tri_ref[...])

    def chunk_rows(h):
        return [slice(h * part + ci * CHUNK, h * part + (ci + 1) * CHUNK) for ci in range(RW_WIDE)]

    def prepare(h, out):
        ins = [(r_s[rw, ls], k_s[rw, ls], v_s[rw, ls], a_s[rw, ls], b_s[rw, ls], lw_s[rw, ls], lc_s[rw, ls])
               for rw in chunk_rows(h) for ls in lanes]
        yield from _rwkv_chunk_prepare(ins, lvl_ref, out)

    def apply(h, prep):
        states = [s_ref[p] for p in range(npair)]
        for ci, rw in enumerate(chunk_rows(h)):
            outs = []
            yield from _rwkv_chunk_apply(prep[ci * npair:(ci + 1) * npair], states, bd, outs)
            states = [s_new for _, s_new in outs]
            for (y, _), ls in zip(outs, lanes):
                y_s[rw, ls] = y
        for p in range(npair):
            s_ref[p] = states[p]

    def epilogue(h):
        rows = slice(h * part, (h + 1) * part)
        y = y_s[rows, :]
        inv_n = 1.0 / RW_HEAD
        m = _dot_hl(y, gsum) * inv_n
        d = y - m
        yield
        var = _dot(d * d, gsum) * inv_n
        yn = d * lax.rsqrt(var + RW_GN_EPS) * gnw_ref[...] + gnb_ref[...]
        yield
        bonus = _dot(r_s[rows, :] * k_s[rows, :] * rk_ref[...], gsum) * v_s[rows, :]
        ya_ref[0, rows, :] = ((yn + bonus) * g_s[rows, :]).astype(BF16)

    nparts = nt // part
    preps = [[] for _ in range(nparts)]
    _interleave(prologue(0))
    prev_ref[...] = u_ref[0, nt - 1:nt, :]
    _interleave(prepare(0, preps[0]), prologue(1) if nparts > 1 else None)
    for h in range(nparts):
        _interleave(apply(h, preps[h]),
                    prepare(h + 1, preps[h + 1]) if h + 1 < nparts else None,
                    prologue(h + 2) if h + 2 < nparts else None,
                    epilogue(h - 1) if h > 0 else None)
    _interleave(epilogue(nparts - 1))


def _rwkv_branch(proj_rw, mu, w0, wup, a0, aup, gup, k_k, k_a, r_k, gn_w, gn_b):
    b, t, cols = proj_rw.shape
    dim = w0.shape[1]
    tb = TB_RWKV
    ii = jnp.arange(CHUNK * RW_WIDE)
    tri = ((ii[:, None] // CHUNK == ii[None, :] // CHUNK) & (ii[:, None] >= ii[None, :])).astype(BF16)
    jj = jnp.arange(dim)
    gsum = (jj[:, None] // RW_HEAD == jj[None, :] // RW_HEAD).astype(BF16)
    rr = jnp.arange(LANES)[:, None]
    cc = jnp.arange(LANES)[None, :]
    lvls = []
    s = 1
    while s < CHUNK:
        lvls.append(((rr // (2 * s) == cc // (2 * s)) & ((rr // s) % 2 == 1) & ((cc // s) % 2 == 0)).astype(F32))
        s *= 2
    lvl = jnp.stack(lvls)
    bd = (rr // RW_HEAD == cc // RW_HEAD).astype(F32)
    zpad = lambda rows: jnp.zeros((rows, dim), F32)
    wup_p = jnp.concatenate([wup, zpad(LANES - wup.shape[0])], axis=0).astype(BF16)
    aup_p = jnp.concatenate([zpad(LANES - aup.shape[0]), aup], axis=0).astype(BF16)
    row2 = lambda a_: a_.reshape(1, -1)
    const = lambda a_: pl.BlockSpec(a_.shape, lambda bi, ti: (0,) * a_.ndim)
    args = [row2(mu), row2(w0), wup_p, row2(a0), aup_p, gup.astype(BF16), row2(k_k), row2(k_a), row2(r_k),
            row2(gn_w), row2(gn_b), tri, gsum, lvl, bd]
    sc = lambda: pltpu.VMEM((tb, dim), F32)
    return pl.pallas_call(
        _rwkv_kernel,
        grid=(b, t // tb),
        in_specs=[pl.BlockSpec((1, tb, cols), lambda bi, ti: (bi, ti, 0))] + [const(a_) for a_ in args],
        out_specs=pl.BlockSpec((1, tb, dim), lambda bi, ti: (bi, ti, 0)),
        out_shape=jax.ShapeDtypeStruct((b, t, dim), BF16),
        scratch_shapes=[pltpu.VMEM((dim // LANES, LANES, LANES), F32), pltpu.VMEM((1, cols), F32)]
                       + [sc() for _ in range(9)],
        compiler_params=_params(("arbitrary", "arbitrary")),
    )(proj_rw, *args)


def _hgrn_chunk_prepare(ins):
    c = CHUNK
    subs = [(SUB * i, SUB * (i + 1)) for i in range(c // SUB)]

    def scores(q, k, lf, bc, lo, hi):
        m = bc[lo:lo + 1] - lf[lo:lo + 1]
        att = _dot(q[lo:hi] * jnp.exp(bc[lo:hi] - m), k[:hi] * jnp.exp(m - bc[:hi]), NT)
        tt = lax.broadcasted_iota(I32, (SUB, hi), 0) + lo
        s_ = lax.broadcasted_iota(I32, (SUB, hi), 1)
        return jnp.where(s_ <= tt, att, 0.0)

    att = [[scores(q, k, lf, bc, lo, hi) for lo, hi in subs] for q, k, v, lf, bc in ins]
    upd = [_dot(v.T, k * jnp.exp(bc[c - 1:c] - bc)) for q, k, v, lf, bc in ins]
    intra = [[_dot(a, x[2][:hi]) for a, (lo, hi) in zip(arow, subs)] for arow, x in zip(att, ins)]
    return [dict(intra=jnp.concatenate(rows, axis=0), qe=x[0] * jnp.exp(x[4]), upd=up, pc=jnp.exp(x[4][c - 1:c]))
            for rows, x, up in zip(intra, ins, upd)]


def _hgrn_chunk_apply(prep, states):
    o = _each(lambda d, st: d["intra"] + _dot(d["qe"], st, NT), prep, states)
    st_new = _each(lambda d, st: st * d["pc"] + d["upd"], prep, states)
    return list(zip(o, st_new))


def _hgrn_kernel(layer, u_ref, lbl_ref, nw_ref, tri_ref, gsum_ref, yb_ref,
                 st_ref, q_s, k_s, v_s, lf_s, bc_s, o_s):
    tb = pl.program_id(1)

    @pl.when(tb == 0)
    def _():
        st_ref[...] = jnp.zeros_like(st_ref)

    u = u_ref[0]
    nt = u.shape[0]
    dim = q_s.shape[1]
    lbl = lbl_ref[...]
    e = jnp.exp(lbl - jnp.max(lbl, axis=0, keepdims=True))
    lb = jnp.sum(e[0:layer + 1], axis=0, keepdims=True) / jnp.sum(e, axis=0, keepdims=True)
    zf = u[:, dim:2 * dim]
    sig = _sigmoid(zf)
    f = lb + (1.0 - lb) * sig
    qin = u[:, 0:dim]
    q_s[...] = qin * _sigmoid(qin)
    k_s[...] = (1.0 - lb) * (1.0 - sig)
    v_s[...] = u[:, 2 * dim:3 * dim]
    lf = jnp.log(f)
    lf_s[...] = lf
    bc_s[...] = _cumsum_chunks(lf, tri_ref[...])

    nhead = dim // LANES
    lanes = [slice(h * LANES, (h + 1) * LANES) for h in range(nhead)]

    def group_body(gi, carry):
        rows = [pl.ds(pl.multiple_of((gi * HG_WIDE + ci) * CHUNK, CHUNK), CHUNK) for ci in range(HG_WIDE)]
        ins = [(q_s[rw, ls], k_s[rw, ls], v_s[rw, ls], lf_s[rw, ls], bc_s[rw, ls]) for rw in rows for ls in lanes]
        prep = _hgrn_chunk_prepare(ins)
        states = [st_ref[h] for h in range(nhead)]
        for ci, rw in enumerate(rows):
            outs = _hgrn_chunk_apply(prep[ci * nhead:(ci + 1) * nhead], states)
            states = [st_new for _, st_new in outs]
            for (o, _), ls in zip(outs, lanes):
                o_s[rw, ls] = o
        for h in range(nhead):
            st_ref[h] = states[h]
        return carry

    lax.fori_loop(0, nt // (CHUNK * HG_WIDE), group_body, 0)

    o = o_s[...]
    ms = _dot(o * o, gsum_ref[...]) * (1.0 / LANES)
    og = u[:, 3 * dim:4 * dim]
    yb_ref[0] = (o * lax.rsqrt(ms + RMS_EPS) * nw_ref[...] * _sigmoid(og)).astype(BF16)


def _hgrn_branch(proj_hg, lb_logits, norm_w, layer):
    b, t, cols = proj_hg.shape
    dim = cols // 4
    tb = TB_HGRN
    ii = jnp.arange(tb)
    tri = ((ii[:, None] // CHUNK == ii[None, :] // CHUNK) & (ii[:, None] >= ii[None, :])).astype(BF16)
    jj = jnp.arange(dim)
    gsum = (jj[:, None] // LANES == jj[None, :] // LANES).astype(BF16)
    const = lambda a_: pl.BlockSpec(a_.shape, lambda bi, ti: (0,) * a_.ndim)
    args = [lb_logits, norm_w.reshape(1, -1), tri, gsum]
    sc = lambda: pltpu.VMEM((tb, dim), F32)
    return pl.pallas_call(
        functools.partial(_hgrn_kernel, layer),
        grid=(b, t // tb),
        in_specs=[pl.BlockSpec((1, tb, cols), lambda bi, ti: (bi, ti, 0))] + [const(a_) for a_ in args],
        out_specs=pl.BlockSpec((1, tb, dim), lambda bi, ti: (bi, ti, 0)),
        out_shape=jax.ShapeDtypeStruct((b, t, dim), BF16),
        scratch_shapes=[pltpu.VMEM((dim // LANES, LANES, LANES), F32)] + [sc() for _ in range(6)],
        compiler_params=_params(("arbitrary", "arbitrary")),
    )(proj_hg, *args)


def _merge_kernel(alpha, x_ref, ya_ref, yb_ref, pg_ref, wa_ref, wb_ref, wo_ref, g_ref, b_ref, wr_ref,
                  x1_ref, lt_ref):
    tm, d = x_ref.shape

    def rows_stage(rs):
        ma = _dg(ya_ref[rs, :], wa_ref[...])
        mb = _dg(yb_ref[rs, :], wb_ref[...])
        yield
        merged = _sigmoid(pg_ref[rs, :d]) * ma + _sigmoid(pg_ref[rs, d:]) * mb
        yield
        h = alpha * x_ref[rs, :] + _dot(merged, wo_ref[...])
        yield
        x1 = _layer_norm(h, g_ref[...], b_ref[...])
        x1_ref[rs, :] = x1
        yield
        lt_ref[:, rs] = _dot3(wr_ref[...], x1, NT)
        yield

    groups = [rows_stage(slice(q * MERGE_ROWS, (q + 1) * MERGE_ROWS)) for q in range(tm // MERGE_ROWS)]
    _skewed(groups)


def _merge(xf, ya, yb, pgate, wa, wb, wo, g, bta, wr, alpha):
    n, d = xf.shape
    tm = TM_MERGE
    tile = lambda a_: pl.BlockSpec((tm, a_.shape[1]), lambda i: (i, 0))
    const = lambda a_: pl.BlockSpec(a_.shape, lambda i: (0, 0))
    return pl.pallas_call(
        functools.partial(_merge_kernel, alpha),
        grid=(n // tm,),
        in_specs=[tile(xf), tile(ya), tile(yb), tile(pgate), const(wa), const(wb), const(wo), const(g),
                  const(bta), const(wr)],
        out_specs=[pl.BlockSpec((tm, d), lambda i: (i, 0)), pl.BlockSpec((LANES, tm), lambda i: (0, i))],
        out_shape=[jax.ShapeDtypeStruct((n, d), F32), jax.ShapeDtypeStruct((LANES, n), F32)],
        compiler_params=_params(("parallel",)),
    )(xf, ya, yb, pgate, wa, wb, wo, g, bta, wr)


ROUTER_EXPERT_ROW = 8


def _to_lanes(col, nl):
    ne = col.shape[0]
    diag = lax.broadcasted_iota(I32, (ne, nl), 0) == lax.broadcasted_iota(I32, (ne, nl), 1)
    return jnp.sum(jnp.where(diag, col, 0.0), axis=0, keepdims=True)


def _route_kernel(lt_ref, bias_ref, upper_ref, lower_ref, route_ref, seg_ref, cnt_ref, carry_ref):
    @pl.when(pl.program_id(0) == 0)
    def _():
        carry_ref[...] = jnp.zeros_like(carry_ref)

    ne = N_GROUPS * EXPERTS_PER_GROUP
    lt = lt_ref[...] + bias_ref[...]
    nb = lt.shape[1]
    neg = -jnp.inf
    lg = lt[0:8]
    rg = lax.broadcasted_iota(I32, (8, nb), 0).astype(F32)
    lg = jnp.where(rg < N_GROUPS, lg, neg)
    mg = jnp.max(lg, axis=0, keepdims=True)
    gidx = jnp.min(jnp.where(lg == mg, rg, 1e9), axis=0, keepdims=True)
    pg_sel = 1.0 / jnp.sum(jnp.exp(lg - mg), axis=0, keepdims=True)

    le = lt[ROUTER_EXPERT_ROW:ROUTER_EXPERT_ROW + ne]
    re = lax.broadcasted_iota(I32, (ne, nb), 0).astype(F32)
    in_group = jnp.floor(re * (1.0 / EXPERTS_PER_GROUP)) == gidx
    l1 = jnp.where(in_group, le, neg)
    m1 = jnp.max(l1, axis=0, keepdims=True)
    i1 = jnp.min(jnp.where(l1 == m1, re, 1e9), axis=0, keepdims=True)
    l2 = jnp.where(re == i1, neg, l1)
    m2 = jnp.max(l2, axis=0, keepdims=True)
    i2 = jnp.min(jnp.where(l2 == m2, re, 1e9), axis=0, keepdims=True)
    e2 = jnp.exp(m2 - m1)
    w1 = pg_sel / (1.0 + e2)
    w2 = pg_sel * e2 / (1.0 + e2)

    sel1 = re == i1
    sel2 = re == i2
    onehot = jnp.where(sel1 | sel2, 1.0, 0.0)
    before = _dg(onehot.astype(BF16), upper_ref[...])
    cnt_t = jnp.sum(onehot, axis=1, keepdims=True)
    seg = jnp.floor((cnt_t + (SEG_ALIGN - 1)) * (1.0 / SEG_ALIGN)) * SEG_ALIGN
    lstart = _dg(lower_ref[...], jnp.broadcast_to(seg, (ne, LANES)).astype(BF16))[:, 0:1]
    tot = lstart + before
    lpos1 = jnp.sum(jnp.where(sel1, tot, 0.0), axis=0, keepdims=True)
    lpos2 = jnp.sum(jnp.where(sel2, tot, 0.0), axis=0, keepdims=True)
    grel = carry_ref[...]
    carry = grel + seg
    carry_ref[...] = carry
    cnt_ref[...] = jnp.broadcast_to(carry, cnt_ref.shape)
    zero = jnp.zeros_like(w1)
    route_ref[...] = jnp.concatenate([i1, i2, lpos1, lpos2, w1, w2, zero, zero], axis=0)
    nl = seg_ref.shape[2]
    zl = jnp.zeros((1, nl), F32)
    ltot = jnp.broadcast_to(jnp.sum(seg, axis=0, keepdims=True), (1, nl))
    seg_ref[0] = jnp.concatenate([_to_lanes(seg, nl), _to_lanes(lstart, nl), _to_lanes(grel, nl), ltot,
                                  zl, zl, zl, zl], axis=0)


def _route(lt, bias_col):
    n = lt.shape[1]
    tb = TB_ROUTE
    ne = N_GROUPS * EXPERTS_PER_GROUP
    ii = jnp.arange(tb)
    upper = (ii[:, None] < ii[None, :]).astype(BF16)
    ee = jnp.arange(ne)
    lower = (ee[:, None] > ee[None, :]).astype(BF16)
    return pl.pallas_call(
        _route_kernel,
        grid=(n // tb,),
        in_specs=[pl.BlockSpec((LANES, tb), lambda i: (0, i)),
                  pl.BlockSpec((LANES, 1), lambda i: (0, 0)),
                  pl.BlockSpec((tb, tb), lambda i: (0, 0)),
                  pl.BlockSpec((ne, ne), lambda i: (0, 0))],
        out_specs=[pl.BlockSpec((8, tb), lambda i: (0, i)), pl.BlockSpec((1, 8, LANES), lambda i: (i, 0, 0)),
                   pl.BlockSpec((ne, LANES), lambda i: (0, 0))],
        out_shape=[jax.ShapeDtypeStruct((8, n), F32), jax.ShapeDtypeStruct((n // tb, 8, LANES), F32),
                   jax.ShapeDtypeStruct((ne, LANES), F32)],
        scratch_shapes=[pltpu.VMEM((ne, 1), F32)],
        compiler_params=_params(("arbitrary",)),
    )(lt, bias_col, upper, lower)


TAB_LANES = LANES


def _finalize_kernel(tm, seg_ref, cnt_ref, lower_ref, segtab_ref, tab_ref):
    ne = cnt_ref.shape[0]
    cnt = cnt_ref[...]
    nb = jnp.floor((cnt + (tm - 1)) * (1.0 / tm))
    bstart = _dg(lower_ref[...], nb.astype(BF16))
    bend = bstart + nb
    pad_start = bstart[:, 0:1] * tm
    pad_start_row = _to_lanes(pad_start, seg_ref.shape[2])
    for t in range(seg_ref.shape[0]):
        seg = seg_ref[t]
        segtab_ref[t] = jnp.concatenate([seg[0:2], seg[2:3] + pad_start_row, seg[3:8]], axis=0).astype(I32)

    nl = tab_ref.shape[1]
    n_used = jnp.max(bend[:, 0:1], axis=0, keepdims=True)
    pad_lo = _to_lanes(pad_start + cnt[:, 0:1], nl)
    pad_hi = _to_lanes(bend[:, 0:1] * tm, nl)
    zero = jnp.zeros((1, nl), F32)
    tab_ref[...] = jnp.concatenate([_to_lanes(bstart[:, 0:1], nl), jnp.broadcast_to(n_used, (1, nl)), pad_lo,
                                    pad_hi, _to_lanes(nb[:, 0:1], nl), zero, zero, zero], axis=0).astype(I32)


def _finalize(seg, cnt, tm):
    ntile = seg.shape[0]
    ne = cnt.shape[0]
    ii = jnp.arange(ne)
    lower = (ii[:, None] > ii[None, :]).astype(BF16)
    return pl.pallas_call(
        functools.partial(_finalize_kernel, tm),
        grid=(1,),
        in_specs=[pl.BlockSpec(seg.shape, lambda i: (0, 0, 0)), pl.BlockSpec(cnt.shape, lambda i: (0, 0)),
                  pl.BlockSpec((ne, ne), lambda i: (0, 0))],
        out_specs=[pl.BlockSpec(seg.shape, lambda i: (0, 0, 0)), pl.BlockSpec((8, TAB_LANES), lambda i: (0, 0))],
        out_shape=[jax.ShapeDtypeStruct((ntile, 8, LANES), I32), jax.ShapeDtypeStruct((8, TAB_LANES), I32)],
        compiler_params=_params(("arbitrary",)),
    )(seg, cnt, lower)


def _for_each_piece(length, max_len, fn):
    size = SEG_ALIGN
    sizes = []
    while size <= max_len:
        sizes.append(size)
        size *= 2
    for size in reversed(sizes):
        @pl.when(jnp.bitwise_and(length, size) != 0)
        def _(size=size):
            fn(pl.multiple_of(jnp.bitwise_and(length, -2 * size), SEG_ALIGN), size)


def _sorted_rows(td):
    return 2 * td + N_GROUPS * EXPERTS_PER_GROUP * SEG_ALIGN


def _dispatch_kernel(tm, seglen_ref, lstart_ref, gstart_ref, ltot_ref, plo_ref, phi_ref, nu_ref,
                     x_ref, route_ref, xbuf_ref, sorted_ref, zblk, sem, zsem):
    i = pl.program_id(0)
    nsteps = pl.num_programs(0)
    td = x_ref.shape[0]
    ne = plo_ref.shape[0]
    nblk = xbuf_ref.shape[0] // tm
    nrow = sorted_ref.shape[1]
    buf = i % 2

    lpos = route_ref[2:4, :]
    r = lax.broadcasted_iota(I32, (nrow, td), 0).astype(F32)
    onehot = jnp.where((r == lpos[0:1]) | (r == lpos[1:2]), 1.0, 0.0).astype(BF16)
    sorted_ref[buf] = _dg(onehot, x_ref[...].astype(BF16)).astype(BF16)

    def wait_tile(step, b):
        _for_each_piece(ltot_ref[step], nrow, lambda off, size: pltpu.make_async_copy(
            sorted_ref.at[b, pl.ds(0, size), :], xbuf_ref.at[pl.ds(0, size), :], sem.at[b]).wait())

    for e in range(ne):
        idx = i * ne + e
        ls = pl.multiple_of(lstart_ref[idx], SEG_ALIGN)
        gs = pl.multiple_of(gstart_ref[idx], SEG_ALIGN)
        _for_each_piece(seglen_ref[idx], td, lambda off, size: pltpu.make_async_copy(
            sorted_ref.at[buf, pl.ds(ls + off, size), :], xbuf_ref.at[pl.ds(gs + off, size), :],
            sem.at[buf]).start())

    @pl.when(i > 0)
    def _():
        wait_tile(i - 1, 1 - buf)

    @pl.when(i == nsteps - 1)
    def _():
        wait_tile(i, buf)

    def pad_fill(fn):
        for e in range(ne):
            lo = pl.multiple_of(plo_ref[e], SEG_ALIGN)
            _for_each_piece(phi_ref[e] - lo, tm // 2, lambda off, size: fn(pltpu.make_async_copy(
                zblk.at[pl.ds(0, size), :], xbuf_ref.at[pl.ds(lo + off, size), :], zsem)))

        def per_blk(b, carry):
            fn(pltpu.make_async_copy(zblk, xbuf_ref.at[pl.ds(pl.multiple_of(b * tm, tm), tm), :], zsem))
            return carry
        lax.fori_loop(nu_ref[0], nblk, per_blk, 0)

    @pl.when(i == 0)
    def _():
        zblk[...] = jnp.zeros_like(zblk)
        pad_fill(lambda cp: cp.start())
        pad_fill(lambda cp: cp.wait())


def _dispatch(seglen, lstart, gstart, ltot, pad_lo, pad_hi, n_used, x1, route, rows, tm):
    n, d = x1.shape
    td = TD_DISPATCH
    return pl.pallas_call(
        functools.partial(_dispatch_kernel, tm),
        grid_spec=pltpu.PrefetchScalarGridSpec(
            num_scalar_prefetch=7,
            grid=(n // td,),
            in_specs=[pl.BlockSpec((td, d), lambda i, *_: (i, 0)), pl.BlockSpec((8, td), lambda i, *_: (0, i))],
            out_specs=pl.BlockSpec(memory_space=pl.ANY),
            scratch_shapes=[pltpu.VMEM((2, _sorted_rows(td), d), BF16), pltpu.VMEM((tm, d), BF16),
                            pltpu.SemaphoreType.DMA((2,)), pltpu.SemaphoreType.DMA(())],
        ),
        out_shape=jax.ShapeDtypeStruct((rows, d), BF16),
        compiler_params=_params(("arbitrary",)),
    )(seglen, lstart, gstart, ltot, pad_lo, pad_hi, n_used, x1, route)


X_SLOTS = 3


def _expert_kernel(tm, bstart_ref, nb_ref, nu_ref, w1_ref, w3_ref, w2_ref, xbuf_ref, ybuf_ref,
                   wf1, wf3, wf2, w13b, w2b, xb, yb, semw, semx, semy):
    e = pl.program_id(0)
    ne = pl.num_programs(0)
    nb = nb_ref[e]
    b0 = bstart_ref[e]
    nblk = ybuf_ref.shape[0] // tm
    rows = lambda blk: pl.ds(pl.multiple_of(blk * tm, tm), tm)
    x_copy = lambda j, slot: pltpu.make_async_copy(xbuf_ref.at[rows(b0 + j), :], xb.at[slot], semx.at[slot])
    y_copy = lambda blk, slot: pltpu.make_async_copy(yb.at[slot], ybuf_ref.at[rows(blk), :], semy.at[slot])

    def w_copies(ex, slot):
        return [pltpu.make_async_copy(src.at[ex], dst.at[slot], semw.at[slot])
                for src, dst in ((w1_ref, wf1), (w3_ref, wf3), (w2_ref, wf2))]

    ws = e % 2

    @pl.when(e == 0)
    def _():
        for cp in w_copies(0, 0):
            cp.start()

    for j0 in range(X_SLOTS - 1):
        @pl.when(j0 < nb)
        def _(j0=j0):
            x_copy(j0, j0).start()

    for cp in w_copies(e, ws):
        cp.wait()
    de = w2b.shape[0]
    w13b[:, :de] = wf1[ws].astype(BF16)
    w13b[:, de:] = wf3[ws].astype(BF16)
    w2b[...] = wf2[ws].astype(BF16)

    @pl.when(e + 1 < ne)
    def _():
        for cp in w_copies(e + 1, 1 - ws):
            cp.start()

    def body(j, carry):
        slot = j % X_SLOTS
        yslot = j % 2
        x_copy(j, slot).wait()

        @pl.when(j + X_SLOTS - 1 < nb)
        def _():
            x_copy(j + X_SLOTS - 1, (j + X_SLOTS - 1) % X_SLOTS).start()

        @pl.when(j >= 2)
        def _():
            y_copy(b0 + j - 2, yslot).wait()

        x = xb[slot]
        h13 = _dg(x, w13b[...])
        h1 = h13[:, :de]
        h = (h1 * _sigmoid(h1)) * h13[:, de:]
        yb[yslot] = _dot(h, w2b[...]).astype(BF16)
        y_copy(b0 + j, yslot).start()
        return carry

    lax.fori_loop(0, nb, body, 0)

    @pl.when(nb >= 2)
    def _():
        y_copy(b0 + nb - 2, nb % 2).wait()

    @pl.when(nb >= 1)
    def _():
        y_copy(b0 + nb - 1, (nb - 1) % 2).wait()

    @pl.when(e == pl.num_programs(0) - 1)
    def _():
        yb[0] = jnp.zeros(yb.shape[1:], yb.dtype)

        def fill(fn):
            def per_blk(blk, carry):
                fn(y_copy(blk, 0))
                return carry
            lax.fori_loop(nu_ref[0], nblk, per_blk, 0)

        fill(lambda cp: cp.start())
        fill(lambda cp: cp.wait())


def _experts(bstart, nb, n_used, xbuf, w1, w3, w2):
    rows, d = xbuf.shape
    ne, _, de = w1.shape
    tm = TM_EXPERT
    return pl.pallas_call(
        functools.partial(_expert_kernel, tm),
        grid_spec=pltpu.PrefetchScalarGridSpec(
            num_scalar_prefetch=3,
            grid=(ne,),
            in_specs=[pl.BlockSpec(memory_space=pl.ANY)] * 4,
            out_specs=pl.BlockSpec(memory_space=pl.ANY),
            scratch_shapes=[pltpu.VMEM((2, d, de), F32), pltpu.VMEM((2, d, de), F32), pltpu.VMEM((2, de, d), F32),
                            pltpu.VMEM((d, 2 * de), BF16), pltpu.VMEM((de, d), BF16),
                            pltpu.VMEM((X_SLOTS, tm, d), BF16), pltpu.VMEM((2, tm, d), BF16),
                            pltpu.SemaphoreType.DMA((2,)), pltpu.SemaphoreType.DMA((X_SLOTS,)),
                            pltpu.SemaphoreType.DMA((2,))],
        ),
        out_shape=jax.ShapeDtypeStruct((rows, d), BF16),
        compiler_params=_params(("arbitrary",)),
    )(bstart, nb, n_used, w1, w3, w2, xbuf)


def _combine_kernel(alpha, seglen_ref, lstart_ref, gstart_ref, ltot_ref, x1_ref, rt_ref, p_ref, wpe_ref, wpg_ref,
                    g_ref, b_ref, ybuf_ref, out_ref, sorted_ref, sem):
    i = pl.program_id(0)
    nsteps = pl.num_programs(0)
    tc = x1_ref.shape[0]
    ne = N_GROUPS * EXPERTS_PER_GROUP
    nrow = sorted_ref.shape[1]

    def fetch(step, buf):
        for e in range(ne):
            idx = step * ne + e
            ls = pl.multiple_of(lstart_ref[idx], SEG_ALIGN)
            gs = pl.multiple_of(gstart_ref[idx], SEG_ALIGN)
            _for_each_piece(seglen_ref[idx], tc, lambda off, size: pltpu.make_async_copy(
                ybuf_ref.at[pl.ds(gs + off, size), :], sorted_ref.at[buf, pl.ds(ls + off, size), :],
                sem.at[buf]).start())

    @pl.when(i == 0)
    def _():
        sorted_ref[...] = jnp.zeros_like(sorted_ref)
        fetch(0, 0)

    @pl.when(i + 1 < nsteps)
    def _():
        fetch(i + 1, (i + 1) % 2)

    cur = i % 2
    _for_each_piece(ltot_ref[i], nrow, lambda off, size: pltpu.make_async_copy(
        ybuf_ref.at[pl.ds(0, size), :], sorted_ref.at[cur, pl.ds(0, size), :], sem.at[cur]).wait())

    rt = rt_ref[...]
    r = lax.broadcasted_iota(I32, (tc, nrow), 1).astype(F32)
    unsort = jnp.where(r == rt[:, 0:1], rt[:, 2:3], 0.0) + jnp.where(r == rt[:, 1:2], rt[:, 3:4], 0.0)
    ffn = _dg(unsort.astype(BF16), sorted_ref[cur])
    x2 = _layer_norm(alpha * x1_ref[...] + ffn, g_ref[...], b_ref[...])
    gate = _sigmoid(_dot(x2, wpg_ref[...]))
    out_ref[...] = x2 + gate * _dot(p_ref[...], wpe_ref[...])


def _combine(seglen, lstart, gstart, ltot, x1, route_t, pf, wpe, wpg, g, bta, ybuf, alpha):
    n, d = x1.shape
    tc = TD_DISPATCH
    tile = lambda a_: pl.BlockSpec((tc, a_.shape[1]), lambda i, *_: (i, 0))
    const = lambda a_: pl.BlockSpec(a_.shape, lambda i, *_: (0, 0))
    return pl.pallas_call(
        functools.partial(_combine_kernel, alpha),
        grid_spec=pltpu.PrefetchScalarGridSpec(
            num_scalar_prefetch=4,
            grid=(n // tc,),
            in_specs=[tile(x1), tile(route_t), tile(pf), const(wpe), const(wpg), const(g), const(bta),
                      pl.BlockSpec(memory_space=pl.ANY)],
            out_specs=pl.BlockSpec((tc, d), lambda i, *_: (i, 0)),
            scratch_shapes=[pltpu.VMEM((2, _sorted_rows(tc), d), BF16), pltpu.SemaphoreType.DMA((2,))],
        ),
        out_shape=jax.ShapeDtypeStruct((n, d), F32),
        compiler_params=_params(("arbitrary",)),
    )(seglen, lstart, gstart, ltot, x1, route_t, pf, wpe, wpg, g, bta, ybuf)


def _layer(x, p_i, w_in, rw_mu, rw_w0, rw_w_up, rw_a0, rw_a_up, rw_g_up, rw_k_k, rw_k_a, rw_r_k, rw_gn_w,
           rw_gn_b, w_a_out, hg_lb_logits, hg_norm_w, w_b_out, w_o, ln1_g, ln1_b, router_g_w, router_g_b,
           router_e_w, router_e_b, w1, w3, w2, ln2_g, ln2_b, w_pe, w_pg, alpha, layer):
    b, t, d = x.shape
    n = b * t
    rw_dim = rw_w0.shape[0]
    rw_cols = 3 * rw_dim + RW_DECAY_LORA + RW_A_LORA + RW_GATE_LORA
    hg_cols = 4 * hg_norm_w.shape[0]
    ne = N_GROUPS * EXPERTS_PER_GROUP
    row2 = lambda a_: a_.reshape(1, -1)
    xf = x.reshape(n, d)

    wb = w_in.astype(BF16)
    proj_rw, proj_hg, proj_gt = _project(xf, wb[:, :rw_cols], wb[:, rw_cols:rw_cols + hg_cols],
                                         wb[:, rw_cols + hg_cols:])
    ya = _rwkv_branch(proj_rw.reshape(b, t, rw_cols), rw_mu, row2(rw_w0), rw_w_up, rw_a0, rw_a_up, rw_g_up,
                      rw_k_k, rw_k_a, rw_r_k, rw_gn_w, rw_gn_b)
    yb = _hgrn_branch(proj_hg.reshape(b, t, hg_cols), hg_lb_logits, hg_norm_w, layer)

    wr = jnp.zeros((LANES, d), F32)
    wr = wr.at[:N_GROUPS].set(router_g_w.T).at[ROUTER_EXPERT_ROW:ROUTER_EXPERT_ROW + ne].set(router_e_w.T)
    bias = jnp.zeros((LANES,), F32)
    bias = bias.at[:N_GROUPS].set(router_g_b).at[ROUTER_EXPERT_ROW:ROUTER_EXPERT_ROW + ne].set(router_e_b)
    x1, lt = _merge(xf, ya.reshape(n, -1), yb.reshape(n, -1), proj_gt, w_a_out.astype(BF16),
                    w_b_out.astype(BF16), w_o.astype(BF16), row2(ln1_g), row2(ln1_b), wr, alpha)
    route, seg, cnt = _route(lt, bias.reshape(LANES, 1))

    tm = TM_EXPERT
    ntile = n // TD_DISPATCH
    nblk = -(-(2 * n + (SEG_ALIGN - 1) * ne * ntile) // tm) + ne
    assert TB_ROUTE == TD_DISPATCH
    segtab, tab = _finalize(seg, cnt, tm)
    per_seg = lambda row: segtab[:, row, :ne].reshape(-1)
    seglen, lstart, gstart, ltot = per_seg(0), per_seg(1), per_seg(2), segtab[:, 3, 0]
    n_used = tab[1, :1]

    xbuf = _dispatch(seglen, lstart, gstart, ltot, tab[2, :ne], tab[3, :ne], n_used, x1, route, nblk * tm, tm)
    ybuf = _experts(tab[0, :ne], tab[4, :ne], n_used, xbuf, w1, w3, w2)
    out = _combine(seglen, lstart, gstart, ltot, x1, route[2:6].T, p_i.reshape(n, -1), w_pe.astype(BF16),
                   w_pg.astype(BF16), row2(ln2_g), row2(ln2_b), ybuf, alpha)
    return out.reshape(b, t, d)


def kernel(x, p, w_in, rw_mu, rw_w0, rw_w_up, rw_a0, rw_a_up, rw_g_up, rw_k_k, rw_k_a, rw_r_k, rw_gn_w, rw_gn_b,
           w_a_out, hg_lb_logits, hg_norm_w, w_b_out, w_o, ln1_g, ln1_b, router_g_w, router_g_b, router_e_w,
           router_e_b, w1, w3, w2, ln2_g, ln2_b, w_pe, w_pg):
    depth = w_in.shape[0]
    alpha = (2 * depth) ** 0.25
    for i in range(depth):
        x = _layer(x, p[i], w_in[i], rw_mu[i], rw_w0[i], rw_w_up[i], rw_a0[i], rw_a_up[i], rw_g_up[i], rw_k_k[i],
                   rw_k_a[i], rw_r_k[i].reshape(-1), rw_gn_w[i], rw_gn_b[i], w_a_out[i], hg_lb_logits,
                   hg_norm_w[i], w_b_out[i], w_o[i], ln1_g[i], ln1_b[i],
                   router_g_w[i], router_g_b[i], router_e_w[i], router_e_b[i], w1[i], w3[i], w2[i], ln2_g[i],
                   ln2_b[i], w_pe[i], w_pg[i], alpha, i)
    return x
```

```python
import functools

import jax
import jax.numpy as jnp
from jax import lax
from jax.experimental import pallas as pl
from jax.experimental.pallas import tpu as pltpu

F32 = jnp.float32
BF16 = jnp.bfloat16
I32 = jnp.int32

NN = (((1,), (0,)), ((), ()))
NT = (((1,), (1,)), ((), ()))

RW_HEAD = 64
RW_DECAY_LORA = 64
RW_A_LORA = 64
RW_GATE_LORA = 128
RW_GN_EPS = 64e-5
RW_DECAY_SCALE = 0.6065306597126334
HG_HEADS = 4
N_GROUPS = 4
EXPERTS_PER_GROUP = 8
LN_EPS = 1e-5
RMS_EPS = 1e-6

CHUNK = 64
SUB = 16
LANES = 128
VMEM_LIMIT = 56 * 1024 * 1024

TM_PROJ = 256
TB_RWKV = 512
TB_HGRN = 256
TM_MERGE = 512
MERGE_ROWS = 256
TB_ROUTE = 512
TD_DISPATCH = 512
SEG_ALIGN = 16
RARE_PIECE = 128
TM_EXPERT = 256
RW_WIDE = 4
HG_WIDE = 4


def _dg(a, b, dn=NN):
    return lax.dot_general(a, b, dn, preferred_element_type=F32)


def _dot(a, b, dn=NN):
    return _dg(a.astype(BF16), b.astype(BF16), dn)


def _split(a):
    hi = a.astype(BF16)
    lo = (a - hi.astype(F32)).astype(BF16)
    return hi, lo


def _dot_hl(a, b_exact, dn=NN):
    hi, lo = _split(a)
    return _dg(hi, b_exact, dn) + _dg(lo, b_exact, dn)


def _dot3(a, b, dn=NN):
    ah, al = _split(a)
    bh, bl = _split(b)
    return _dg(ah, bh, dn) + (_dg(ah, bl, dn) + _dg(al, bh, dn))


def _cumsum_chunks(x, tri):
    h = x.astype(BF16)
    r1 = x - h.astype(F32)
    m = r1.astype(BF16)
    l = (r1 - m.astype(F32)).astype(BF16)
    return _dg(tri, h) + (_dg(tri, m) + _dg(tri, l))


def _sigmoid(x):
    return 0.5 * jnp.tanh(0.5 * x) + 0.5


def _layer_norm(h, g, b):
    mu = jnp.mean(h, axis=-1, keepdims=True)
    d = h - mu
    var = jnp.mean(d * d, axis=-1, keepdims=True)
    return d * lax.rsqrt(var + LN_EPS) * g + b


def _params(sem):
    return pltpu.CompilerParams(dimension_semantics=sem, vmem_limit_bytes=VMEM_LIMIT)


def _proj_kernel(x_ref, wr_ref, wh_ref, wg_ref, pr_ref, ph_ref, pg_ref):
    xb = x_ref[...].astype(BF16)
    pr_ref[...] = _dg(xb, wr_ref[...])
    ph_ref[...] = _dg(xb, wh_ref[...])
    pg_ref[...] = _dg(xb, wg_ref[...])


def _project(xf, w_rw, w_hg, w_gt):
    n, d = xf.shape
    tm = TM_PROJ
    full = lambda w: pl.BlockSpec(w.shape, lambda i: (0, 0))
    tile = lambda c: pl.BlockSpec((tm, c), lambda i: (i, 0))
    return pl.pallas_call(
        _proj_kernel,
        grid=(n // tm,),
        in_specs=[tile(d), full(w_rw), full(w_hg), full(w_gt)],
        out_specs=[tile(w_rw.shape[1]), tile(w_hg.shape[1]), tile(w_gt.shape[1])],
        out_shape=[jax.ShapeDtypeStruct((n, w.shape[1]), F32) for w in (w_rw, w_hg, w_gt)],
        compiler_params=_params(("parallel",)),
    )(xf, w_rw, w_hg, w_gt)


def _each(f, *ls):
    return [f(*xs) for xs in zip(*ls)]


def _two(x):
    m1 = lax.broadcasted_iota(I32, x.shape, 1) < RW_HEAD
    return jnp.concatenate([jnp.where(m1, x, 0.0), jnp.where(m1, 0.0, x)], axis=0)


def _rwkv_chunk_prepare(ins, lvl_ref, out):
    c = CHUNK
    lane = lax.broadcasted_iota(I32, (c, LANES), 1)
    row = lax.broadcasted_iota(I32, (c, LANES), 0)
    scol = jnp.bitwise_and(lane, RW_HEAD - 1)
    strict = row > scol
    incl = row >= scol
    r2 = lax.broadcasted_iota(I32, (LANES, LANES), 0)
    c2 = lax.broadcasted_iota(I32, (LANES, LANES), 1)
    eye = jnp.where(r2 == c2, 1.0, 0.0).astype(F32)

    def prep(r, k, v, av, bv, lw, lc):
        l_end = lc[c - 1:c]
        e_r = jnp.exp(l_end - lc)
        e_n = jnp.exp(-lc)
        return dict(at=av * jnp.exp(lc - lw), rt=r * jnp.exp(lc), bt=bv * e_n, kt=k * e_n,
                    bk=jnp.concatenate([bv * e_r, k * e_r], axis=0), pc=jnp.exp(l_end), v=v)

    q = [prep(*xs) for xs in ins]
    yield
    p = [_dot(jnp.concatenate([d["at"], d["rt"]], axis=0),
              jnp.concatenate([_two(d["bt"]), _two(d["kt"])], axis=0), NT) for d in q]
    sab = [jnp.where(strict, x[:c, :LANES], 0.0) for x in p]
    sak = [jnp.where(strict, x[:c, LANES:], 0.0) for x in p]
    srb = [jnp.where(incl, x[c:, :LANES], 0.0) for x in p]
    srk = [jnp.where(incl, x[c:, LANES:], 0.0) for x in p]
    yield
    sv = _each(lambda ak, rk, d: _dot(jnp.concatenate([ak, rk], axis=0), _two(d["v"])), sak, srk, q)
    yield

    a_bd = [_two(x) for x in sab]
    t = [eye + a * lvl_ref[0] for a in a_bd]
    for lv in range(1, lvl_ref.shape[0]):
        ta = _each(lambda t_, a: _dot(t_, a * lvl_ref[lv]), t, a_bd)
        yield
        t = _each(lambda t_, ta_: t_ + _dot(ta_, t_), t, ta)
        yield

    x = _each(lambda t_, d, sv_: _dot(t_, jnp.concatenate([_two(d["at"]), _two(sv_[:c])], axis=1)), t, q, sv)
    out.extend(dict(uk=x_[:c, :LANES] + x_[c:, :LANES],
                    w=x_[:c, LANES:] + x_[c:, LANES:],
                    rt=d["rt"], rkv=sv_[c:], srb=srb_, bk=d["bk"], v=d["v"], pc=d["pc"])
               for x_, d, sv_, srb_ in zip(x, q, sv, srb))


def _rwkv_chunk_apply(prep, states, bd, out):
    c = CHUNK
    g1 = _each(lambda d, s: _dot(jnp.concatenate([d["uk"], d["rt"]], axis=0), s, NT), prep, states)
    yield
    u = _each(lambda g, d: g[:c] + d["w"], g1, prep)
    y = _each(lambda g, d, u_: g[c:] + d["rkv"] + _dot(d["srb"], _two(u_)), g1, prep, u)
    yield
    upd = _each(lambda u_, d: _dot(jnp.concatenate([u_, d["v"]], axis=0).T, d["bk"]), u, prep)
    s_new = _each(lambda d, s, up: s * d["pc"] + bd * up, prep, states, upd)
    out.extend(zip(y, s_new))
    yield


def _skewed(gens):
    live = set(range(len(gens)))
    tick = 0
    while live:
        for q in sorted(live):
            if q <= tick:
                try:
                    next(gens[q])
                except StopIteration:
                    live.discard(q)
        tick += 1


def _interleave(*gens):
    live = [g for g in gens if g is not None]
    while live:
        for g in list(live):
            try:
                next(g)
            except StopIteration:
                live.remove(g)


def _rwkv_kernel(u_ref, mu_ref, w0_ref, wup_ref, a0_ref, aup_ref, gup_ref, kk_ref, ka_ref, rk_ref,
                 gnw_ref, gnb_ref, tri_ref, gsum_ref, lvl_ref, bd_ref, ya_ref,
                 s_ref, prev_ref, r_s, k_s, v_s, a_s, b_s, lw_s, lc_s, g_s, y_s):
    tb = pl.program_id(1)

    @pl.when(tb == 0)
    def _():
        s_ref[...] = jnp.zeros_like(s_ref)
        prev_ref[...] = jnp.zeros_like(prev_ref)

    nt = u_ref.shape[1]
    dim = r_s.shape[1]
    part = CHUNK * RW_WIDE
    npair = dim // LANES
    lanes = [slice(p * LANES, (p + 1) * LANES) for p in range(npair)]
    bd = bd_ref[...]
    gsum = gsum_ref[...]

    def prologue(h):
        rows = slice(h * part, (h + 1) * part)
        u = u_ref[0, rows, :]
        before = prev_ref[...] if h == 0 else u_ref[0, h * part - 1:h * part, :]
        rowid = lax.broadcasted_iota(I32, u.shape, 0)
        shifted = jnp.where(rowid == 0, before, pltpu.roll(u, 1, axis=0))
        um = u + (shifted - u) * mu_ref[...]
        r = um[:, 0:dim]
        k = um[:, dim:2 * dim]
        v = um[:, 2 * dim:3 * dim]
        xwa = um[:, 3 * dim:3 * dim + LANES]
        xg = um[:, 3 * dim + LANES:3 * dim + 2 * LANES]
        yield
        wpre = w0_ref[...] + _dot(jnp.tanh(xwa), wup_ref[...])
        lw = -RW_DECAY_SCALE * _sigmoid(wpre)
        a = _sigmoid(a0_ref[...] + _dot(xwa, aup_ref[...]))
        g_s[rows, :] = _dot(_sigmoid(xg), gup_ref[...])
        yield
        kk = k * kk_ref[...]
        ss = _dot(kk * kk, gsum)
        kk = kk * lax.rsqrt(jnp.maximum(ss, 1e-24))
        r_s[rows, :] = r
        k_s[rows, :] = k * (1.0 + (a - 1.0) * ka_ref[...])
        v_s[rows, :] = v
        a_s[rows, :] = -kk
        b_s[rows, :] = kk * a
        yield
        lw_s[rows, :] = lw
        lc_s[rows, :] = _cumsum_chunks(lw, tri_ref[...])

    def chunk_rows(h):
        return [slice(h * part + ci * CHUNK, h * part + (ci + 1) * CHUNK) for ci in range(RW_WIDE)]

    def prepare(h, out):
        ins = [(r_s[rw, ls], k_s[rw, ls], v_s[rw, ls], a_s[rw, ls], b_s[rw, ls], lw_s[rw, ls], lc_s[rw, ls])
               for rw in chunk_rows(h) for ls in lanes]
        yield from _rwkv_chunk_prepare(ins, lvl_ref, out)

    def apply(h, prep):
        states = [s_ref[p] for p in range(npair)]
        for ci, rw in enumerate(chunk_rows(h)):
            outs = []
            yield from _rwkv_chunk_apply(prep[ci * npair:(ci + 1) * npair], states, bd, outs)
            states = [s_new for _, s_new in outs]
            for (y, _), ls in zip(outs, lanes):
                y_s[rw, ls] = y
        for p in range(npair):
            s_ref[p] = states[p]

    def epilogue(h):
        rows = slice(h * part, (h + 1) * part)
        y = y_s[rows, :]
        inv_n = 1.0 / RW_HEAD
        m = _dot_hl(y, gsum) * inv_n
        d = y - m
        yield
        var = _dot(d * d, gsum) * inv_n
        yn = d * lax.rsqrt(var + RW_GN_EPS) * gnw_ref[...] + gnb_ref[...]
        yield
        bonus = _dot(r_s[rows, :] * k_s[rows, :] * rk_ref[...], gsum) * v_s[rows, :]
        ya_ref[0, rows, :] = ((yn + bonus) * g_s[rows, :]).astype(BF16)

    nparts = nt // part
    preps = [[] for _ in range(nparts)]
    _interleave(prologue(0))
    prev_ref[...] = u_ref[0, nt - 1:nt, :]
    _interleave(prepare(0, preps[0]), prologue(1) if nparts > 1 else None)
    for h in range(nparts):
        _interleave(apply(h, preps[h]),
                    prepare(h + 1, preps[h + 1]) if h + 1 < nparts else None,
                    prologue(h + 2) if h + 2 < nparts else None,
                    epilogue(h - 1) if h > 0 else None)
    _interleave(epilogue(nparts - 1))


def _rwkv_branch(proj_rw, mu, w0, wup, a0, aup, gup, k_k, k_a, r_k, gn_w, gn_b):
    b, t, cols = proj_rw.shape
    dim = w0.shape[1]
    tb = TB_RWKV
    ii = jnp.arange(CHUNK * RW_WIDE)
    tri = ((ii[:, None] // CHUNK == ii[None, :] // CHUNK) & (ii[:, None] >= ii[None, :])).astype(BF16)
    jj = jnp.arange(dim)
    gsum = (jj[:, None] // RW_HEAD == jj[None, :] // RW_HEAD).astype(BF16)
    rr = jnp.arange(LANES)[:, None]
    cc = jnp.arange(LANES)[None, :]
    lvls = []
    s = 1
    while s < CHUNK:
        lvls.append(((rr // (2 * s) == cc // (2 * s)) & ((rr // s) % 2 == 1) & ((cc // s) % 2 == 0)).astype(F32))
        s *= 2
    lvl = jnp.stack(lvls)
    bd = (rr // RW_HEAD == cc // RW_HEAD).astype(F32)
    zpad = lambda rows: jnp.zeros((rows, dim), F32)
    wup_p = jnp.concatenate([wup, zpad(LANES - wup.shape[0])], axis=0).astype(BF16)
    aup_p = jnp.concatenate([zpad(LANES - aup.shape[0]), aup], axis=0).astype(BF16)
    row2 = lambda a_: a_.reshape(1, -1)
    const = lambda a_: pl.BlockSpec(a_.shape, lambda bi, ti: (0,) * a_.ndim)
    args = [row2(mu), row2(w0), wup_p, row2(a0), aup_p, gup.astype(BF16), row2(k_k), row2(k_a), row2(r_k),
            row2(gn_w), row2(gn_b), tri, gsum, lvl, bd]
    sc = lambda: pltpu.VMEM((tb, dim), F32)
    return pl.pallas_call(
        _rwkv_kernel,
        grid=(b, t // tb),
        in_specs=[pl.BlockSpec((1, tb, cols), lambda bi, ti: (bi, ti, 0))] + [const(a_) for a_ in args],
        out_specs=pl.BlockSpec((1, tb, dim), lambda bi, ti: (bi, ti, 0)),
        out_shape=jax.ShapeDtypeStruct((b, t, dim), BF16),
        scratch_shapes=[pltpu.VMEM((dim // LANES, LANES, LANES), F32), pltpu.VMEM((1, cols), F32)]
                       + [sc() for _ in range(9)],
        compiler_params=_params(("arbitrary", "arbitrary")),
    )(proj_rw, *args)


def _hgrn_chunk_prepare(ins):
    c = CHUNK
    subs = [(SUB * i, SUB * (i + 1)) for i in range(c // SUB)]

    def scores(q, k, lf, bc, lo, hi):
        m = bc[lo:lo + 1] - lf[lo:lo + 1]
        att = _dot(q[lo:hi] * jnp.exp(bc[lo:hi] - m), k[:hi] * jnp.exp(m - bc[:hi]), NT)
        tt = lax.broadcasted_iota(I32, (SUB, hi), 0) + lo
        s_ = lax.broadcasted_iota(I32, (SUB, hi), 1)
        return jnp.where(s_ <= tt, att, 0.0)

    att = [[scores(q, k, lf, bc, lo, hi) for lo, hi in subs] for q, k, v, lf, bc in ins]
    upd = [_dot(v.T, k * jnp.exp(bc[c - 1:c] - bc)) for q, k, v, lf, bc in ins]
    intra = [[_dot(a, x[2][:hi]) for a, (lo, hi) in zip(arow, subs)] for arow, x in zip(att, ins)]
    return [dict(intra=jnp.concatenate(rows, axis=0), qe=x[0] * jnp.exp(x[4]), upd=up, pc=jnp.exp(x[4][c - 1:c]))
            for rows, x, up in zip(intra, ins, upd)]


def _hgrn_chunk_apply(prep, states):
    o = _each(lambda d, st: d["intra"] + _dot(d["qe"], st, NT), prep, states)
    st_new = _each(lambda d, st: st * d["pc"] + d["upd"], prep, states)
    return list(zip(o, st_new))


def _hgrn_kernel(layer, u_ref, lbl_ref, nw_ref, tri_ref, gsum_ref, yb_ref,
                 st_ref, q_s, k_s, v_s, lf_s, bc_s, o_s):
    tb = pl.program_id(1)

    @pl.when(tb == 0)
    def _():
        st_ref[...] = jnp.zeros_like(st_ref)

    u = u_ref[0]
    nt = u.shape[0]
    dim = q_s.shape[1]
    lbl = lbl_ref[...]
    e = jnp.exp(lbl - jnp.max(lbl, axis=0, keepdims=True))
    lb = jnp.sum(e[0:layer + 1], axis=0, keepdims=True) / jnp.sum(e, axis=0, keepdims=True)
    zf = u[:, dim:2 * dim]
    sig = _sigmoid(zf)
    f = lb + (1.0 - lb) * sig
    qin = u[:, 0:dim]
    q_s[...] = qin * _sigmoid(qin)
    k_s[...] = (1.0 - lb) * (1.0 - sig)
    v_s[...] = u[:, 2 * dim:3 * dim]
    lf = jnp.log(f)
    lf_s[...] = lf
    bc_s[...] = _cumsum_chunks(lf, tri_ref[...])

    nhead = dim // LANES
    lanes = [slice(h * LANES, (h + 1) * LANES) for h in range(nhead)]

    def group_body(gi, carry):
        rows = [pl.ds(pl.multiple_of((gi * HG_WIDE + ci) * CHUNK, CHUNK), CHUNK) for ci in range(HG_WIDE)]
        ins = [(q_s[rw, ls], k_s[rw, ls], v_s[rw, ls], lf_s[rw, ls], bc_s[rw, ls]) for rw in rows for ls in lanes]
        prep = _hgrn_chunk_prepare(ins)
        states = [st_ref[h] for h in range(nhead)]
        for ci, rw in enumerate(rows):
            outs = _hgrn_chunk_apply(prep[ci * nhead:(ci + 1) * nhead], states)
            states = [st_new for _, st_new in outs]
            for (o, _), ls in zip(outs, lanes):
                o_s[rw, ls] = o
        for h in range(nhead):
            st_ref[h] = states[h]
        return carry

    lax.fori_loop(0, nt // (CHUNK * HG_WIDE), group_body, 0)

    o = o_s[...]
    ms = _dot(o * o, gsum_ref[...]) * (1.0 / LANES)
    og = u[:, 3 * dim:4 * dim]
    yb_ref[0] = (o * lax.rsqrt(ms + RMS_EPS) * nw_ref[...] * _sigmoid(og)).astype(BF16)


def _hgrn_branch(proj_hg, lb_logits, norm_w, layer):
    b, t, cols = proj_hg.shape
    dim = cols // 4
    tb = TB_HGRN
    ii = jnp.arange(tb)
    tri = ((ii[:, None] // CHUNK == ii[None, :] // CHUNK) & (ii[:, None] >= ii[None, :])).astype(BF16)
    jj = jnp.arange(dim)
    gsum = (jj[:, None] // LANES == jj[None, :] // LANES).astype(BF16)
    const = lambda a_: pl.BlockSpec(a_.shape, lambda bi, ti: (0,) * a_.ndim)
    args = [lb_logits, norm_w.reshape(1, -1), tri, gsum]
    sc = lambda: pltpu.VMEM((tb, dim), F32)
    return pl.pallas_call(
        functools.partial(_hgrn_kernel, layer),
        grid=(b, t // tb),
        in_specs=[pl.BlockSpec((1, tb, cols), lambda bi, ti: (bi, ti, 0))] + [const(a_) for a_ in args],
        out_specs=pl.BlockSpec((1, tb, dim), lambda bi, ti: (bi, ti, 0)),
        out_shape=jax.ShapeDtypeStruct((b, t, dim), BF16),
        scratch_shapes=[pltpu.VMEM((dim // LANES, LANES, LANES), F32)] + [sc() for _ in range(6)],
        compiler_params=_params(("arbitrary", "arbitrary")),
    )(proj_hg, *args)


def _merge_kernel(alpha, x_ref, ya_ref, yb_ref, pg_ref, wa_ref, wb_ref, wo_ref, g_ref, b_ref, wr_ref,
                  x1_ref, lt_ref):
    tm, d = x_ref.shape

    def rows_stage(rs):
        ma = _dg(ya_ref[rs, :], wa_ref[...])
        mb = _dg(yb_ref[rs, :], wb_ref[...])
        yield
        merged = _sigmoid(pg_ref[rs, :d]) * ma + _sigmoid(pg_ref[rs, d:]) * mb
        yield
        h = alpha * x_ref[rs, :] + _dot(merged, wo_ref[...])
        yield
        x1 = _layer_norm(h, g_ref[...], b_ref[...])
        x1_ref[rs, :] = x1
        yield
        lt_ref[:, rs] = _dot3(wr_ref[...], x1, NT)
        yield

    groups = [rows_stage(slice(q * MERGE_ROWS, (q + 1) * MERGE_ROWS)) for q in range(tm // MERGE_ROWS)]
    _skewed(groups)


def _merge(xf, ya, yb, pgate, wa, wb, wo, g, bta, wr, alpha):
    n, d = xf.shape
    tm = TM_MERGE
    tile = lambda a_: pl.BlockSpec((tm, a_.shape[1]), lambda i: (i, 0))
    const = lambda a_: pl.BlockSpec(a_.shape, lambda i: (0, 0))
    return pl.pallas_call(
        functools.partial(_merge_kernel, alpha),
        grid=(n // tm,),
        in_specs=[tile(xf), tile(ya), tile(yb), tile(pgate), const(wa), const(wb), const(wo), const(g),
                  const(bta), const(wr)],
        out_specs=[pl.BlockSpec((tm, d), lambda i: (i, 0)), pl.BlockSpec((LANES, tm), lambda i: (0, i))],
        out_shape=[jax.ShapeDtypeStruct((n, d), F32), jax.ShapeDtypeStruct((LANES, n), F32)],
        compiler_params=_params(("parallel",)),
    )(xf, ya, yb, pgate, wa, wb, wo, g, bta, wr)


ROUTER_EXPERT_ROW = 8


def _to_lanes(col, nl):
    ne = col.shape[0]
    diag = lax.broadcasted_iota(I32, (ne, nl), 0) == lax.broadcasted_iota(I32, (ne, nl), 1)
    return jnp.sum(jnp.where(diag, col, 0.0), axis=0, keepdims=True)


def _route_kernel(lt_ref, bias_ref, upper_ref, lower_ref, route_ref, seg_ref, cnt_ref, carry_ref):
    @pl.when(pl.program_id(0) == 0)
    def _():
        carry_ref[...] = jnp.zeros_like(carry_ref)

    ne = N_GROUPS * EXPERTS_PER_GROUP
    lt = lt_ref[...] + bias_ref[...]
    nb = lt.shape[1]
    neg = -jnp.inf
    lg = lt[0:8]
    rg = lax.broadcasted_iota(I32, (8, nb), 0).astype(F32)
    lg = jnp.where(rg < N_GROUPS, lg, neg)
    mg = jnp.max(lg, axis=0, keepdims=True)
    gidx = jnp.min(jnp.where(lg == mg, rg, 1e9), axis=0, keepdims=True)
    pg_sel = 1.0 / jnp.sum(jnp.exp(lg - mg), axis=0, keepdims=True)

    le = lt[ROUTER_EXPERT_ROW:ROUTER_EXPERT_ROW + ne]
    re = lax.broadcasted_iota(I32, (ne, nb), 0).astype(F32)
    in_group = jnp.floor(re * (1.0 / EXPERTS_PER_GROUP)) == gidx
    l1 = jnp.where(in_group, le, neg)
    m1 = jnp.max(l1, axis=0, keepdims=True)
    i1 = jnp.min(jnp.where(l1 == m1, re, 1e9), axis=0, keepdims=True)
    l2 = jnp.where(re == i1, neg, l1)
    m2 = jnp.max(l2, axis=0, keepdims=True)
    i2 = jnp.min(jnp.where(l2 == m2, re, 1e9), axis=0, keepdims=True)
    e2 = jnp.exp(m2 - m1)
    w1 = pg_sel / (1.0 + e2)
    w2 = pg_sel * e2 / (1.0 + e2)

    sel1 = re == i1
    sel2 = re == i2
    onehot = jnp.where(sel1 | sel2, 1.0, 0.0)
    before = _dg(onehot.astype(BF16), upper_ref[...])
    cnt_t = jnp.sum(onehot, axis=1, keepdims=True)
    seg = jnp.floor((cnt_t + (SEG_ALIGN - 1)) * (1.0 / SEG_ALIGN)) * SEG_ALIGN
    lstart = _dg(lower_ref[...], jnp.broadcast_to(seg, (ne, LANES)).astype(BF16))[:, 0:1]
    tot = lstart + before
    lpos1 = jnp.sum(jnp.where(sel1, tot, 0.0), axis=0, keepdims=True)
    lpos2 = jnp.sum(jnp.where(sel2, tot, 0.0), axis=0, keepdims=True)
    grel = carry_ref[...]
    carry = grel + seg
    carry_ref[...] = carry
    cnt_ref[...] = jnp.broadcast_to(carry, cnt_ref.shape)
    zero = jnp.zeros_like(w1)
    route_ref[...] = jnp.concatenate([i1, i2, lpos1, lpos2, w1, w2, zero, zero], axis=0)
    nl = seg_ref.shape[2]
    zl = jnp.zeros((1, nl), F32)
    ltot = jnp.broadcast_to(jnp.sum(seg, axis=0, keepdims=True), (1, nl))
    seg_ref[0] = jnp.concatenate([_to_lanes(seg, nl), _to_lanes(lstart, nl), _to_lanes(grel, nl), ltot,
                                  zl, zl, zl, zl], axis=0)


def _route(lt, bias_col):
    n = lt.shape[1]
    tb = TB_ROUTE
    ne = N_GROUPS * EXPERTS_PER_GROUP
    ii = jnp.arange(tb)
    upper = (ii[:, None] < ii[None, :]).astype(BF16)
    ee = jnp.arange(ne)
    lower = (ee[:, None] > ee[None, :]).astype(BF16)
    return pl.pallas_call(
        _route_kernel,
        grid=(n // tb,),
        in_specs=[pl.BlockSpec((LANES, tb), lambda i: (0, i)),
                  pl.BlockSpec((LANES, 1), lambda i: (0, 0)),
                  pl.BlockSpec((tb, tb), lambda i: (0, 0)),
                  pl.BlockSpec((ne, ne), lambda i: (0, 0))],
        out_specs=[pl.BlockSpec((8, tb), lambda i: (0, i)), pl.BlockSpec((1, 8, LANES), lambda i: (i, 0, 0)),
                   pl.BlockSpec((ne, LANES), lambda i: (0, 0))],
        out_shape=[jax.ShapeDtypeStruct((8, n), F32), jax.ShapeDtypeStruct((n // tb, 8, LANES), F32),
                   jax.ShapeDtypeStruct((ne, LANES), F32)],
        scratch_shapes=[pltpu.VMEM((ne, 1), F32)],
        compiler_params=_params(("arbitrary",)),
    )(lt, bias_col, upper, lower)


TAB_LANES = LANES


def _finalize_kernel(tm, seg_ref, cnt_ref, lower_ref, segtab_ref, tab_ref):
    ne = cnt_ref.shape[0]
    cnt = cnt_ref[...]
    nb = jnp.floor((cnt + (tm - 1)) * (1.0 / tm))
    bstart = _dg(lower_ref[...], nb.astype(BF16))
    bend = bstart + nb
    pad_start = bstart[:, 0:1] * tm
    pad_start_row = _to_lanes(pad_start, seg_ref.shape[2])
    for t in range(seg_ref.shape[0]):
        seg = seg_ref[t]
        segtab_ref[t] = jnp.concatenate([seg[0:2], seg[2:3] + pad_start_row, seg[3:8]], axis=0).astype(I32)

    nl = tab_ref.shape[1]
    n_used = jnp.max(bend[:, 0:1], axis=0, keepdims=True)
    pad_lo = _to_lanes(pad_start + cnt[:, 0:1], nl)
    pad_hi = _to_lanes(bend[:, 0:1] * tm, nl)
    zero = jnp.zeros((1, nl), F32)
    tab_ref[...] = jnp.concatenate([_to_lanes(bstart[:, 0:1], nl), jnp.broadcast_to(n_used, (1, nl)), pad_lo,
                                    pad_hi, _to_lanes(nb[:, 0:1], nl), zero, zero, zero], axis=0).astype(I32)


def _finalize(seg, cnt, tm):
    ntile = seg.shape[0]
    ne = cnt.shape[0]
    ii = jnp.arange(ne)
    lower = (ii[:, None] > ii[None, :]).astype(BF16)
    return pl.pallas_call(
        functools.partial(_finalize_kernel, tm),
        grid=(1,),
        in_specs=[pl.BlockSpec(seg.shape, lambda i: (0, 0, 0)), pl.BlockSpec(cnt.shape, lambda i: (0, 0)),
                  pl.BlockSpec((ne, ne), lambda i: (0, 0))],
        out_specs=[pl.BlockSpec(seg.shape, lambda i: (0, 0, 0)), pl.BlockSpec((8, TAB_LANES), lambda i: (0, 0))],
        out_shape=[jax.ShapeDtypeStruct((ntile, 8, LANES), I32), jax.ShapeDtypeStruct((8, TAB_LANES), I32)],
        compiler_params=_params(("arbitrary",)),
    )(seg, cnt, lower)


def _for_each_piece(length, max_len, fn):
    size = SEG_ALIGN
    sizes = []
    while size <= max_len:
        sizes.append(size)
        size *= 2

    def pieces(group):
        for size in reversed(group):
            @pl.when(jnp.bitwise_and(length, size) != 0)
            def _(size=size):
                fn(pl.multiple_of(jnp.bitwise_and(length, -2 * size), SEG_ALIGN), size)

    big = [s for s in sizes if s >= RARE_PIECE]
    if len(big) > 1:
        pl.when(length >= RARE_PIECE)(lambda: pieces(big))
    else:
        pieces(big)
    pieces([s for s in sizes if s < RARE_PIECE])


def _sorted_rows(td):
    return 2 * td + N_GROUPS * EXPERTS_PER_GROUP * SEG_ALIGN


def _dispatch_kernel(tm, seglen_ref, lstart_ref, gstart_ref, ltot_ref, plo_ref, phi_ref, nu_ref,
                     x_ref, route_ref, xbuf_ref, sorted_ref, zblk, sem, zsem):
    i = pl.program_id(0)
    nsteps = pl.num_programs(0)
    td = x_ref.shape[0]
    ne = plo_ref.shape[0]
    nblk = xbuf_ref.shape[0] // tm
    nrow = sorted_ref.shape[1]
    buf = i % 2

    lpos = route_ref[2:4, :]
    r = lax.broadcasted_iota(I32, (nrow, td), 0).astype(F32)
    onehot = jnp.where((r == lpos[0:1]) | (r == lpos[1:2]), 1.0, 0.0).astype(BF16)
    sorted_ref[buf] = _dg(onehot, x_ref[...].astype(BF16)).astype(BF16)

    def wait_tile(step, b):
        _for_each_piece(ltot_ref[step], nrow, lambda off, size: pltpu.make_async_copy(
            sorted_ref.at[b, pl.ds(0, size), :], xbuf_ref.at[pl.ds(0, size), :], sem.at[b]).wait())

    for e in range(ne):
        idx = i * ne + e
        ls = pl.multiple_of(lstart_ref[idx], SEG_ALIGN)
        gs = pl.multiple_of(gstart_ref[idx], SEG_ALIGN)
        _for_each_piece(seglen_ref[idx], td, lambda off, size: pltpu.make_async_copy(
            sorted_ref.at[buf, pl.ds(ls + off, size), :], xbuf_ref.at[pl.ds(gs + off, size), :],
            sem.at[buf]).start())

    @pl.when(i > 0)
    def _():
        wait_tile(i - 1, 1 - buf)

    @pl.when(i == nsteps - 1)
    def _():
        wait_tile(i, buf)

    def pad_fill(fn):
        for e in range(ne):
            lo = pl.multiple_of(plo_ref[e], SEG_ALIGN)
            _for_each_piece(phi_ref[e] - lo, tm // 2, lambda off, size: fn(pltpu.make_async_copy(
                zblk.at[pl.ds(0, size), :], xbuf_ref.at[pl.ds(lo + off, size), :], zsem)))

        def per_blk(b, carry):
            fn(pltpu.make_async_copy(zblk, xbuf_ref.at[pl.ds(pl.multiple_of(b * tm, tm), tm), :], zsem))
            return carry
        lax.fori_loop(nu_ref[0], nblk, per_blk, 0)

    @pl.when(i == 0)
    def _():
        zblk[...] = jnp.zeros_like(zblk)
        pad_fill(lambda cp: cp.start())
        pad_fill(lambda cp: cp.wait())


def _dispatch(seglen, lstart, gstart, ltot, pad_lo, pad_hi, n_used, x1, route, rows, tm):
    n, d = x1.shape
    td = TD_DISPATCH
    return pl.pallas_call(
        functools.partial(_dispatch_kernel, tm),
        grid_spec=pltpu.PrefetchScalarGridSpec(
            num_scalar_prefetch=7,
            grid=(n // td,),
            in_specs=[pl.BlockSpec((td, d), lambda i, *_: (i, 0)), pl.BlockSpec((8, td), lambda i, *_: (0, i))],
            out_specs=pl.BlockSpec(memory_space=pl.ANY),
            scratch_shapes=[pltpu.VMEM((2, _sorted_rows(td), d), BF16), pltpu.VMEM((tm, d), BF16),
                            pltpu.SemaphoreType.DMA((2,)), pltpu.SemaphoreType.DMA(())],
        ),
        out_shape=jax.ShapeDtypeStruct((rows, d), BF16),
        compiler_params=_params(("arbitrary",)),
    )(seglen, lstart, gstart, ltot, pad_lo, pad_hi, n_used, x1, route)


X_SLOTS = 3


def _expert_kernel(tm, bstart_ref, nb_ref, nu_ref, w1_ref, w3_ref, w2_ref, xbuf_ref, ybuf_ref,
                   wf1, wf3, wf2, w13b, w2b, xb, yb, semw, semx, semy):
    e = pl.program_id(0)
    ne = pl.num_programs(0)
    nb = nb_ref[e]
    b0 = bstart_ref[e]
    nblk = ybuf_ref.shape[0] // tm
    rows = lambda blk: pl.ds(pl.multiple_of(blk * tm, tm), tm)
    x_copy = lambda j, slot: pltpu.make_async_copy(xbuf_ref.at[rows(b0 + j), :], xb.at[slot], semx.at[slot])
    y_copy = lambda blk, slot: pltpu.make_async_copy(yb.at[slot], ybuf_ref.at[rows(blk), :], semy.at[slot])

    def w_copies(ex, slot):
        return [pltpu.make_async_copy(src.at[ex], dst.at[slot], semw.at[slot])
                for src, dst in ((w1_ref, wf1), (w3_ref, wf3), (w2_ref, wf2))]

    ws = e % 2

    @pl.when(e == 0)
    def _():
        for cp in w_copies(0, 0):
            cp.start()

    for j0 in range(X_SLOTS - 1):
        @pl.when(j0 < nb)
        def _(j0=j0):
            x_copy(j0, j0).start()

    for cp in w_copies(e, ws):
        cp.wait()
    de = w2b.shape[0]
    w13b[:, :de] = wf1[ws].astype(BF16)
    w13b[:, de:] = wf3[ws].astype(BF16)
    w2b[...] = wf2[ws].astype(BF16)

    @pl.when(e + 1 < ne)
    def _():
        for cp in w_copies(e + 1, 1 - ws):
            cp.start()

    def body(j, carry):
        slot = j % X_SLOTS
        yslot = j % 2
        x_copy(j, slot).wait()

        @pl.when(j + X_SLOTS - 1 < nb)
        def _():
            x_copy(j + X_SLOTS - 1, (j + X_SLOTS - 1) % X_SLOTS).start()

        @pl.when(j >= 2)
        def _():
            y_copy(b0 + j - 2, yslot).wait()

        x = xb[slot]
        h13 = _dg(x, w13b[...])
        h1 = h13[:, :de]
        h = (h1 * _sigmoid(h1)) * h13[:, de:]
        yb[yslot] = _dot(h, w2b[...]).astype(BF16)
        y_copy(b0 + j, yslot).start()
        return carry

    lax.fori_loop(0, nb, body, 0)

    @pl.when(nb >= 2)
    def _():
        y_copy(b0 + nb - 2, nb % 2).wait()

    @pl.when(nb >= 1)
    def _():
        y_copy(b0 + nb - 1, (nb - 1) % 2).wait()

    @pl.when(e == pl.num_programs(0) - 1)
    def _():
        yb[0] = jnp.zeros(yb.shape[1:], yb.dtype)

        def fill(fn):
            def per_blk(blk, carry):
                fn(y_copy(blk, 0))
                return carry
            lax.fori_loop(nu_ref[0], nblk, per_blk, 0)

        fill(lambda cp: cp.start())
        fill(lambda cp: cp.wait())


def _experts(bstart, nb, n_used, xbuf, w1, w3, w2):
    rows, d = xbuf.shape
    ne, _, de = w1.shape
    tm = TM_EXPERT
    return pl.pallas_call(
        functools.partial(_expert_kernel, tm),
        grid_spec=pltpu.PrefetchScalarGridSpec(
            num_scalar_prefetch=3,
            grid=(ne,),
            in_specs=[pl.BlockSpec(memory_space=pl.ANY)] * 4,
            out_specs=pl.BlockSpec(memory_space=pl.ANY),
            scratch_shapes=[pltpu.VMEM((2, d, de), F32), pltpu.VMEM((2, d, de), F32), pltpu.VMEM((2, de, d), F32),
                            pltpu.VMEM((d, 2 * de), BF16), pltpu.VMEM((de, d), BF16),
                            pltpu.VMEM((X_SLOTS, tm, d), BF16), pltpu.VMEM((2, tm, d), BF16),
                            pltpu.SemaphoreType.DMA((2,)), pltpu.SemaphoreType.DMA((X_SLOTS,)),
                            pltpu.SemaphoreType.DMA((2,))],
        ),
        out_shape=jax.ShapeDtypeStruct((rows, d), BF16),
        compiler_params=_params(("arbitrary",)),
    )(bstart, nb, n_used, w1, w3, w2, xbuf)


def _combine_kernel(alpha, seglen_ref, lstart_ref, gstart_ref, ltot_ref, x1_ref, rt_ref, p_ref, wpe_ref, wpg_ref,
                    g_ref, b_ref, ybuf_ref, out_ref, sorted_ref, sem):
    i = pl.program_id(0)
    nsteps = pl.num_programs(0)
    tc = x1_ref.shape[0]
    ne = N_GROUPS * EXPERTS_PER_GROUP
    nrow = sorted_ref.shape[1]

    def fetch(step, buf):
        for e in range(ne):
            idx = step * ne + e
            ls = pl.multiple_of(lstart_ref[idx], SEG_ALIGN)
            gs = pl.multiple_of(gstart_ref[idx], SEG_ALIGN)
            _for_each_piece(seglen_ref[idx], tc, lambda off, size: pltpu.make_async_copy(
                ybuf_ref.at[pl.ds(gs + off, size), :], sorted_ref.at[buf, pl.ds(ls + off, size), :],
                sem.at[buf]).start())

    @pl.when(i == 0)
    def _():
        sorted_ref[...] = jnp.zeros_like(sorted_ref)
        fetch(0, 0)

    @pl.when(i + 1 < nsteps)
    def _():
        fetch(i + 1, (i + 1) % 2)

    cur = i % 2
    _for_each_piece(ltot_ref[i], nrow, lambda off, size: pltpu.make_async_copy(
        ybuf_ref.at[pl.ds(0, size), :], sorted_ref.at[cur, pl.ds(0, size), :], sem.at[cur]).wait())

    rt = rt_ref[...]
    r = lax.broadcasted_iota(I32, (tc, nrow), 1).astype(F32)
    unsort = jnp.where(r == rt[:, 0:1], rt[:, 2:3], 0.0) + jnp.where(r == rt[:, 1:2], rt[:, 3:4], 0.0)
    ffn = _dg(unsort.astype(BF16), sorted_ref[cur])
    x2 = _layer_norm(alpha * x1_ref[...] + ffn, g_ref[...], b_ref[...])
    gate = _sigmoid(_dot(x2, wpg_ref[...]))
    out_ref[...] = x2 + gate * _dot(p_ref[...], wpe_ref[...])


def _combine(seglen, lstart, gstart, ltot, x1, route_t, pf, wpe, wpg, g, bta, ybuf, alpha):
    n, d = x1.shape
    tc = TD_DISPATCH
    tile = lambda a_: pl.BlockSpec((tc, a_.shape[1]), lambda i, *_: (i, 0))
    const = lambda a_: pl.BlockSpec(a_.shape, lambda i, *_: (0, 0))
    return pl.pallas_call(
        functools.partial(_combine_kernel, alpha),
        grid_spec=pltpu.PrefetchScalarGridSpec(
            num_scalar_prefetch=4,
            grid=(n // tc,),
            in_specs=[tile(x1), tile(route_t), tile(pf), const(wpe), const(wpg), const(g), const(bta),
                      pl.BlockSpec(memory_space=pl.ANY)],
            out_specs=pl.BlockSpec((tc, d), lambda i, *_: (i, 0)),
            scratch_shapes=[pltpu.VMEM((2, _sorted_rows(tc), d), BF16), pltpu.SemaphoreType.DMA((2,))],
        ),
        out_shape=jax.ShapeDtypeStruct((n, d), F32),
        compiler_params=_params(("arbitrary",)),
    )(seglen, lstart, gstart, ltot, x1, route_t, pf, wpe, wpg, g, bta, ybuf)


def _layer(x, p_i, w_in, rw_mu, rw_w0, rw_w_up, rw_a0, rw_a_up, rw_g_up, rw_k_k, rw_k_a, rw_r_k, rw_gn_w,
           rw_gn_b, w_a_out, hg_lb_logits, hg_norm_w, w_b_out, w_o, ln1_g, ln1_b, router_g_w, router_g_b,
           router_e_w, router_e_b, w1, w3, w2, ln2_g, ln2_b, w_pe, w_pg, alpha, layer):
    b, t, d = x.shape
    n = b * t
    rw_dim = rw_w0.shape[0]
    rw_cols = 3 * rw_dim + RW_DECAY_LORA + RW_A_LORA + RW_GATE_LORA
    hg_cols = 4 * hg_norm_w.shape[0]
    ne = N_GROUPS * EXPERTS_PER_GROUP
    row2 = lambda a_: a_.reshape(1, -1)
    xf = x.reshape(n, d)

    wb = w_in.astype(BF16)
    proj_rw, proj_hg, proj_gt = _project(xf, wb[:, :rw_cols], wb[:, rw_cols:rw_cols + hg_cols],
                                         wb[:, rw_cols + hg_cols:])
    ya = _rwkv_branch(proj_rw.reshape(b, t, rw_cols), rw_mu, row2(rw_w0), rw_w_up, rw_a0, rw_a_up, rw_g_up,
                      rw_k_k, rw_k_a, rw_r_k, rw_gn_w, rw_gn_b)
    yb = _hgrn_branch(proj_hg.reshape(b, t, hg_cols), hg_lb_logits, hg_norm_w, layer)

    wr = jnp.zeros((LANES, d), F32)
    wr = wr.at[:N_GROUPS].set(router_g_w.T).at[ROUTER_EXPERT_ROW:ROUTER_EXPERT_ROW + ne].set(router_e_w.T)
    bias = jnp.zeros((LANES,), F32)
    bias = bias.at[:N_GROUPS].set(router_g_b).at[ROUTER_EXPERT_ROW:ROUTER_EXPERT_ROW + ne].set(router_e_b)
    x1, lt = _merge(xf, ya.reshape(n, -1), yb.reshape(n, -1), proj_gt, w_a_out.astype(BF16),
                    w_b_out.astype(BF16), w_o.astype(BF16), row2(ln1_g), row2(ln1_b), wr, alpha)
    route, seg, cnt = _route(lt, bias.reshape(LANES, 1))

    tm = TM_EXPERT
    ntile = n // TD_DISPATCH
    nblk = -(-(2 * n + (SEG_ALIGN - 1) * ne * ntile) // tm) + ne
    assert TB_ROUTE == TD_DISPATCH
    segtab, tab = _finalize(seg, cnt, tm)
    per_seg = lambda row: segtab[:, row, :ne].reshape(-1)
    seglen, lstart, gstart, ltot = per_seg(0), per_seg(1), per_seg(2), segtab[:, 3, 0]
    n_used = tab[1, :1]

    xbuf = _dispatch(seglen, lstart, gstart, ltot, tab[2, :ne], tab[3, :ne], n_used, x1, route, nblk * tm, tm)
    ybuf = _experts(tab[0, :ne], tab[4, :ne], n_used, xbuf, w1, w3, w2)
    out = _combine(seglen, lstart, gstart, ltot, x1, route[2:6].T, p_i.reshape(n, -1), w_pe.astype(BF16),
                   w_pg.astype(BF16), row2(ln2_g), row2(ln2_b), ybuf, alpha)
    return out.reshape(b, t, d)


def kernel(x, p, w_in, rw_mu, rw_w0, rw_w_up, rw_a0, rw_a_up, rw_g_up, rw_k_k, rw_k_a, rw_r_k, rw_gn_w, rw_gn_b,
           w_a_out, hg_lb_logits, hg_norm_w, w_b_out, w_o, ln1_g, ln1_b, router_g_w, router_g_b, router_e_w,
           router_e_b, w1, w3, w2, ln2_g, ln2_b, w_pe, w_pg):
    depth = w_in.shape[0]
    alpha = (2 * depth) ** 0.25
    for i in range(depth):
        x = _layer(x, p[i], w_in[i], rw_mu[i], rw_w0[i], rw_w_up[i], rw_a0[i], rw_a_up[i], rw_g_up[i], rw_k_k[i],
                   rw_k_a[i], rw_r_k[i].reshape(-1), rw_gn_w[i], rw_gn_b[i], w_a_out[i], hg_lb_logits,
                   hg_norm_w[i], w_b_out[i], w_o[i], ln1_g[i], ln1_b[i],
                   router_g_w[i], router_g_b[i], router_e_w[i], router_e_b[i], w1[i], w3[i], w2[i], ln2_g[i],
                   ln2_b[i], w_pe[i], w_pg[i], alpha, i)
    return x
```

```python
import functools

import jax
import jax.numpy as jnp
from jax import lax
from jax.experimental import pallas as pl
from jax.experimental.pallas import tpu as pltpu

F32 = jnp.float32
BF16 = jnp.bfloat16
I32 = jnp.int32

NN = (((1,), (0,)), ((), ()))
NT = (((1,), (1,)), ((), ()))

RW_HEAD = 64
RW_DECAY_LORA = 64
RW_A_LORA = 64
RW_GATE_LORA = 128
RW_GN_EPS = 64e-5
RW_DECAY_SCALE = 0.6065306597126334
HG_HEADS = 4
N_GROUPS = 4
EXPERTS_PER_GROUP = 8
LN_EPS = 1e-5
RMS_EPS = 1e-6

CHUNK = 64
SUB = 16
LANES = 128
VMEM_LIMIT = 56 * 1024 * 1024

TM_PROJ = 256
TB_RWKV = 512
TB_HGRN = 256
TM_MERGE = 512
MERGE_ROWS = 256
TD_DISPATCH = 512
SEG_ALIGN = 16
TM_EXPERT = 256
RW_WIDE = 4
HG_WIDE = 4


def _dg(a, b, dn=NN):
    return lax.dot_general(a, b, dn, preferred_element_type=F32)


def _dot(a, b, dn=NN):
    return _dg(a.astype(BF16), b.astype(BF16), dn)


def _split(a):
    hi = a.astype(BF16)
    lo = (a - hi.astype(F32)).astype(BF16)
    return hi, lo


def _dot_hl(a, b_exact, dn=NN):
    hi, lo = _split(a)
    return _dg(hi, b_exact, dn) + _dg(lo, b_exact, dn)


def _dot3(a, b, dn=NN):
    ah, al = _split(a)
    bh, bl = _split(b)
    return _dg(ah, bh, dn) + (_dg(ah, bl, dn) + _dg(al, bh, dn))


def _cumsum_chunks(x, tri):
    h = x.astype(BF16)
    r1 = x - h.astype(F32)
    m = r1.astype(BF16)
    l = (r1 - m.astype(F32)).astype(BF16)
    return _dg(tri, h) + (_dg(tri, m) + _dg(tri, l))


def _sigmoid(x):
    return 0.5 * jnp.tanh(0.5 * x) + 0.5


def _layer_norm(h, g, b):
    mu = jnp.mean(h, axis=-1, keepdims=True)
    d = h - mu
    var = jnp.mean(d * d, axis=-1, keepdims=True)
    return d * lax.rsqrt(var + LN_EPS) * g + b


def _params(sem):
    return pltpu.CompilerParams(dimension_semantics=sem, vmem_limit_bytes=VMEM_LIMIT)


def _proj_kernel(x_ref, wr_ref, wh_ref, wg_ref, pr_ref, ph_ref, pg_ref):
    xb = x_ref[...].astype(BF16)
    pr_ref[...] = _dg(xb, wr_ref[...])
    ph_ref[...] = _dg(xb, wh_ref[...])
    pg_ref[...] = _dg(xb, wg_ref[...])


def _project(xf, w_rw, w_hg, w_gt):
    n, d = xf.shape
    tm = TM_PROJ
    full = lambda w: pl.BlockSpec(w.shape, lambda i: (0, 0))
    tile = lambda c: pl.BlockSpec((tm, c), lambda i: (i, 0))
    return pl.pallas_call(
        _proj_kernel,
        grid=(n // tm,),
        in_specs=[tile(d), full(w_rw), full(w_hg), full(w_gt)],
        out_specs=[tile(w_rw.shape[1]), tile(w_hg.shape[1]), tile(w_gt.shape[1])],
        out_shape=[jax.ShapeDtypeStruct((n, w.shape[1]), F32) for w in (w_rw, w_hg, w_gt)],
        compiler_params=_params(("parallel",)),
    )(xf, w_rw, w_hg, w_gt)


def _each(f, *ls):
    return [f(*xs) for xs in zip(*ls)]


def _two(x):
    m1 = lax.broadcasted_iota(I32, x.shape, 1) < RW_HEAD
    return jnp.concatenate([jnp.where(m1, x, 0.0), jnp.where(m1, 0.0, x)], axis=0)


def _rwkv_chunk_prepare(ins, lvl_ref, out):
    c = CHUNK
    lane = lax.broadcasted_iota(I32, (c, LANES), 1)
    row = lax.broadcasted_iota(I32, (c, LANES), 0)
    scol = jnp.bitwise_and(lane, RW_HEAD - 1)
    strict = row > scol
    incl = row >= scol
    r2 = lax.broadcasted_iota(I32, (LANES, LANES), 0)
    c2 = lax.broadcasted_iota(I32, (LANES, LANES), 1)
    eye = jnp.where(r2 == c2, 1.0, 0.0).astype(F32)

    def prep(r, k, v, av, bv, lw, lc):
        l_end = lc[c - 1:c]
        e_r = jnp.exp(l_end - lc)
        e_n = jnp.exp(-lc)
        return dict(at=av * jnp.exp(lc - lw), rt=r * jnp.exp(lc), bt=bv * e_n, kt=k * e_n,
                    bk=jnp.concatenate([bv * e_r, k * e_r], axis=0), pc=jnp.exp(l_end), v=v)

    q = [prep(*xs) for xs in ins]
    yield
    p = [_dot(jnp.concatenate([d["at"], d["rt"]], axis=0),
              jnp.concatenate([_two(d["bt"]), _two(d["kt"])], axis=0), NT) for d in q]
    sab = [jnp.where(strict, x[:c, :LANES], 0.0) for x in p]
    sak = [jnp.where(strict, x[:c, LANES:], 0.0) for x in p]
    srb = [jnp.where(incl, x[c:, :LANES], 0.0) for x in p]
    srk = [jnp.where(incl, x[c:, LANES:], 0.0) for x in p]
    yield
    sv = _each(lambda ak, rk, d: _dot(jnp.concatenate([ak, rk], axis=0), _two(d["v"])), sak, srk, q)
    yield

    a_bd = [_two(x) for x in sab]
    t = [eye + a * lvl_ref[0] for a in a_bd]
    for lv in range(1, lvl_ref.shape[0]):
        ta = _each(lambda t_, a: _dot(t_, a * lvl_ref[lv]), t, a_bd)
        yield
        t = _each(lambda t_, ta_: t_ + _dot(ta_, t_), t, ta)
        yield

    x = _each(lambda t_, d, sv_: _dot(t_, jnp.concatenate([_two(d["at"]), _two(sv_[:c])], axis=1)), t, q, sv)
    out.extend(dict(uk=x_[:c, :LANES] + x_[c:, :LANES],
                    w=x_[:c, LANES:] + x_[c:, LANES:],
                    rt=d["rt"], rkv=sv_[c:], srb=srb_, bk=d["bk"], v=d["v"], pc=d["pc"])
               for x_, d, sv_, srb_ in zip(x, q, sv, srb))


def _rwkv_chunk_apply(prep, states, bd, out):
    c = CHUNK
    g1 = _each(lambda d, s: _dot(jnp.concatenate([d["uk"], d["rt"]], axis=0), s, NT), prep, states)
    yield
    u = _each(lambda g, d: g[:c] + d["w"], g1, prep)
    y = _each(lambda g, d, u_: g[c:] + d["rkv"] + _dot(d["srb"], _two(u_)), g1, prep, u)
    yield
    upd = _each(lambda u_, d: _dot(jnp.concatenate([u_, d["v"]], axis=0).T, d["bk"]), u, prep)
    s_new = _each(lambda d, s, up: s * d["pc"] + bd * up, prep, states, upd)
    out.extend(zip(y, s_new))
    yield


def _skewed(gens):
    live = set(range(len(gens)))
    tick = 0
    while live:
        for q in sorted(live):
            if q <= tick:
                try:
                    next(gens[q])
                except StopIteration:
                    live.discard(q)
        tick += 1


def _interleave(*gens):
    live = [g for g in gens if g is not None]
    while live:
        for g in list(live):
            try:
                next(g)
            except StopIteration:
                live.remove(g)


def _rwkv_kernel(u_ref, mu_ref, w0_ref, wup_ref, a0_ref, aup_ref, gup_ref, kk_ref, ka_ref, rk_ref,
                 gnw_ref, gnb_ref, tri_ref, gsum_ref, lvl_ref, bd_ref, ya_ref,
                 s_ref, prev_ref, r_s, k_s, v_s, a_s, b_s, lw_s, lc_s, g_s, y_s):
    tb = pl.program_id(1)

    @pl.when(tb == 0)
    def _():
        s_ref[...] = jnp.zeros_like(s_ref)
        prev_ref[...] = jnp.zeros_like(prev_ref)

    nt = u_ref.shape[1]
    dim = r_s.shape[1]
    part = CHUNK * RW_WIDE
    npair = dim // LANES
    lanes = [slice(p * LANES, (p + 1) * LANES) for p in range(npair)]
    bd = bd_ref[...]
    gsum = gsum_ref[...]

    def prologue(h):
        rows = slice(h * part, (h + 1) * part)
        u = u_ref[0, rows, :]
        before = prev_ref[...] if h == 0 else u_ref[0, h * part - 1:h * part, :]
        rowid = lax.broadcasted_iota(I32, u.shape, 0)
        shifted = jnp.where(rowid == 0, before, pltpu.roll(u, 1, axis=0))
        um = u + (shifted - u) * mu_ref[...]
        r = um[:, 0:dim]
        k = um[:, dim:2 * dim]
        v = um[:, 2 * dim:3 * dim]
        xwa = um[:, 3 * dim:3 * dim + LANES]
        xg = um[:, 3 * dim + LANES:3 * dim + 2 * LANES]
        yield
        wpre = w0_ref[...] + _dot(jnp.tanh(xwa), wup_ref[...])
        lw = -RW_DECAY_SCALE * _sigmoid(wpre)
        a = _sigmoid(a0_ref[...] + _dot(xwa, aup_ref[...]))
        g_s[rows, :] = _dot(_sigmoid(xg), gup_ref[...])
        yield
        kk = k * kk_ref[...]
        ss = _dot(kk * kk, gsum)
        kk = kk * lax.rsqrt(jnp.maximum(ss, 1e-24))
        r_s[rows, :] = r
        k_s[rows, :] = k * (1.0 + (a - 1.0) * ka_ref[...])
        v_s[rows, :] = v
        a_s[rows, :] = -kk
        b_s[rows, :] = kk * a
        yield
        lw_s[rows, :] = lw
        lc_s[rows, :] = _cumsum_chunks(lw, tri_ref[...])

    def chunk_rows(h):
        return [slice(h * part + ci * CHUNK, h * part + (ci + 1) * CHUNK) for ci in range(RW_WIDE)]

    def prepare(h, out):
        ins = [(r_s[rw, ls], k_s[rw, ls], v_s[rw, ls], a_s[rw, ls], b_s[rw, ls], lw_s[rw, ls], lc_s[rw, ls])
               for rw in chunk_rows(h) for ls in lanes]
        yield from _rwkv_chunk_prepare(ins, lvl_ref, out)

    def apply(h, prep):
        states = [s_ref[p] for p in range(npair)]
        for ci, rw in enumerate(chunk_rows(h)):
            outs = []
            yield from _rwkv_chunk_apply(prep[ci * npair:(ci + 1) * npair], states, bd, outs)
            states = [s_new for _, s_new in outs]
            for (y, _), ls in zip(outs, lanes):
                y_s[rw, ls] = y
        for p in range(npair):
            s_ref[p] = states[p]

    def epilogue(h):
        rows = slice(h * part, (h + 1) * part)
        y = y_s[rows, :]
        inv_n = 1.0 / RW_HEAD
        m = _dot_hl(y, gsum) * inv_n
        d = y - m
        yield
        var = _dot(d * d, gsum) * inv_n
        yn = d * lax.rsqrt(var + RW_GN_EPS) * gnw_ref[...] + gnb_ref[...]
        yield
        bonus = _dot(r_s[rows, :] * k_s[rows, :] * rk_ref[...], gsum) * v_s[rows, :]
        ya_ref[0, rows, :] = ((yn + bonus) * g_s[rows, :]).astype(BF16)

    nparts = nt // part
    preps = [[] for _ in range(nparts)]
    _interleave(prologue(0))
    prev_ref[...] = u_ref[0, nt - 1:nt, :]
    _interleave(prepare(0, preps[0]), prologue(1) if nparts > 1 else None)
    for h in range(nparts):
        _interleave(apply(h, preps[h]),
                    prepare(h + 1, preps[h + 1]) if h + 1 < nparts else None,
                    prologue(h + 2) if h + 2 < nparts else None,
                    epilogue(h - 1) if h > 0 else None)
    _interleave(epilogue(nparts - 1))


def _rwkv_branch(proj_rw, mu, w0, wup, a0, aup, gup, k_k, k_a, r_k, gn_w, gn_b):
    b, t, cols = proj_rw.shape
    dim = w0.shape[1]
    tb = TB_RWKV
    ii = jnp.arange(CHUNK * RW_WIDE)
    tri = ((ii[:, None] // CHUNK == ii[None, :] // CHUNK) & (ii[:, None] >= ii[None, :])).astype(BF16)
    jj = jnp.arange(dim)
    gsum = (jj[:, None] // RW_HEAD == jj[None, :] // RW_HEAD).astype(BF16)
    rr = jnp.arange(LANES)[:, None]
    cc = jnp.arange(LANES)[None, :]
    lvls = []
    s = 1
    while s < CHUNK:
        lvls.append(((rr // (2 * s) == cc // (2 * s)) & ((rr // s) % 2 == 1) & ((cc // s) % 2 == 0)).astype(F32))
        s *= 2
    lvl = jnp.stack(lvls)
    bd = (rr // RW_HEAD == cc // RW_HEAD).astype(F32)
    zpad = lambda rows: jnp.zeros((rows, dim), F32)
    wup_p = jnp.concatenate([wup, zpad(LANES - wup.shape[0])], axis=0).astype(BF16)
    aup_p = jnp.concatenate([zpad(LANES - aup.shape[0]), aup], axis=0).astype(BF16)
    row2 = lambda a_: a_.reshape(1, -1)
    const = lambda a_: pl.BlockSpec(a_.shape, lambda bi, ti: (0,) * a_.ndim)
    args = [row2(mu), row2(w0), wup_p, row2(a0), aup_p, gup.astype(BF16), row2(k_k), row2(k_a), row2(r_k),
            row2(gn_w), row2(gn_b), tri, gsum, lvl, bd]
    sc = lambda: pltpu.VMEM((tb, dim), F32)
    return pl.pallas_call(
        _rwkv_kernel,
        grid=(b, t // tb),
        in_specs=[pl.BlockSpec((1, tb, cols), lambda bi, ti: (bi, ti, 0))] + [const(a_) for a_ in args],
        out_specs=pl.BlockSpec((1, tb, dim), lambda bi, ti: (bi, ti, 0)),
        out_shape=jax.ShapeDtypeStruct((b, t, dim), BF16),
        scratch_shapes=[pltpu.VMEM((dim // LANES, LANES, LANES), F32), pltpu.VMEM((1, cols), F32)]
                       + [sc() for _ in range(9)],
        compiler_params=_params(("arbitrary", "arbitrary")),
    )(proj_rw, *args)


def _hgrn_chunk_prepare(ins):
    c = CHUNK
    subs = [(SUB * i, SUB * (i + 1)) for i in range(c // SUB)]

    def scores(q, k, lf, bc, lo, hi):
        m = bc[lo:lo + 1] - lf[lo:lo + 1]
        att = _dot(q[lo:hi] * jnp.exp(bc[lo:hi] - m), k[:hi] * jnp.exp(m - bc[:hi]), NT)
        tt = lax.broadcasted_iota(I32, (SUB, hi), 0) + lo
        s_ = lax.broadcasted_iota(I32, (SUB, hi), 1)
        return jnp.where(s_ <= tt, att, 0.0)

    att = [[scores(q, k, lf, bc, lo, hi) for lo, hi in subs] for q, k, v, lf, bc in ins]
    upd = [_dot(v.T, k * jnp.exp(bc[c - 1:c] - bc)) for q, k, v, lf, bc in ins]
    intra = [[_dot(a, x[2][:hi]) for a, (lo, hi) in zip(arow, subs)] for arow, x in zip(att, ins)]
    return [dict(intra=jnp.concatenate(rows, axis=0), qe=x[0] * jnp.exp(x[4]), upd=up, pc=jnp.exp(x[4][c - 1:c]))
            for rows, x, up in zip(intra, ins, upd)]


def _hgrn_chunk_apply(prep, states):
    o = _each(lambda d, st: d["intra"] + _dot(d["qe"], st, NT), prep, states)
    st_new = _each(lambda d, st: st * d["pc"] + d["upd"], prep, states)
    return list(zip(o, st_new))


def _hgrn_kernel(layer, u_ref, lbl_ref, nw_ref, tri_ref, gsum_ref, yb_ref,
                 st_ref, q_s, k_s, v_s, lf_s, bc_s, o_s):
    tb = pl.program_id(1)

    @pl.when(tb == 0)
    def _():
        st_ref[...] = jnp.zeros_like(st_ref)

    u = u_ref[0]
    nt = u.shape[0]
    dim = q_s.shape[1]
    lbl = lbl_ref[...]
    e = jnp.exp(lbl - jnp.max(lbl, axis=0, keepdims=True))
    lb = jnp.sum(e[0:layer + 1], axis=0, keepdims=True) / jnp.sum(e, axis=0, keepdims=True)
    zf = u[:, dim:2 * dim]
    sig = _sigmoid(zf)
    f = lb + (1.0 - lb) * sig
    qin = u[:, 0:dim]
    q_s[...] = qin * _sigmoid(qin)
    k_s[...] = (1.0 - lb) * (1.0 - sig)
    v_s[...] = u[:, 2 * dim:3 * dim]
    lf = jnp.log(f)
    lf_s[...] = lf
    bc_s[...] = _cumsum_chunks(lf, tri_ref[...])

    nhead = dim // LANES
    lanes = [slice(h * LANES, (h + 1) * LANES) for h in range(nhead)]

    def group_body(gi, carry):
        rows = [pl.ds(pl.multiple_of((gi * HG_WIDE + ci) * CHUNK, CHUNK), CHUNK) for ci in range(HG_WIDE)]
        ins = [(q_s[rw, ls], k_s[rw, ls], v_s[rw, ls], lf_s[rw, ls], bc_s[rw, ls]) for rw in rows for ls in lanes]
        prep = _hgrn_chunk_prepare(ins)
        states = [st_ref[h] for h in range(nhead)]
        for ci, rw in enumerate(rows):
            outs = _hgrn_chunk_apply(prep[ci * nhead:(ci + 1) * nhead], states)
            states = [st_new for _, st_new in outs]
            for (o, _), ls in zip(outs, lanes):
                o_s[rw, ls] = o
        for h in range(nhead):
            st_ref[h] = states[h]
        return carry

    lax.fori_loop(0, nt // (CHUNK * HG_WIDE), group_body, 0)

    o = o_s[...]
    ms = _dot(o * o, gsum_ref[...]) * (1.0 / LANES)
    og = u[:, 3 * dim:4 * dim]
    yb_ref[0] = (o * lax.rsqrt(ms + RMS_EPS) * nw_ref[...] * _sigmoid(og)).astype(BF16)


def _hgrn_branch(proj_hg, lb_logits, norm_w, layer):
    b, t, cols = proj_hg.shape
    dim = cols // 4
    tb = TB_HGRN
    ii = jnp.arange(tb)
    tri = ((ii[:, None] // CHUNK == ii[None, :] // CHUNK) & (ii[:, None] >= ii[None, :])).astype(BF16)
    jj = jnp.arange(dim)
    gsum = (jj[:, None] // LANES == jj[None, :] // LANES).astype(BF16)
    const = lambda a_: pl.BlockSpec(a_.shape, lambda bi, ti: (0,) * a_.ndim)
    args = [lb_logits, norm_w.reshape(1, -1), tri, gsum]
    sc = lambda: pltpu.VMEM((tb, dim), F32)
    return pl.pallas_call(
        functools.partial(_hgrn_kernel, layer),
        grid=(b, t // tb),
        in_specs=[pl.BlockSpec((1, tb, cols), lambda bi, ti: (bi, ti, 0))] + [const(a_) for a_ in args],
        out_specs=pl.BlockSpec((1, tb, dim), lambda bi, ti: (bi, ti, 0)),
        out_shape=jax.ShapeDtypeStruct((b, t, dim), BF16),
        scratch_shapes=[pltpu.VMEM((dim // LANES, LANES, LANES), F32)] + [sc() for _ in range(6)],
        compiler_params=_params(("arbitrary", "arbitrary")),
    )(proj_hg, *args)


def _merge_kernel(alpha, x_ref, ya_ref, yb_ref, pg_ref, wa_ref, wb_ref, wo_ref, g_ref, b_ref, wr_ref,
                  bias_ref, upper_ref, lower_ref, x1_ref, route_ref, seg_ref, cnt_ref, lt_ref, carry_ref):
    tm, d = x_ref.shape

    def rows_stage(rs):
        ma = _dg(ya_ref[rs, :], wa_ref[...])
        mb = _dg(yb_ref[rs, :], wb_ref[...])
        yield
        merged = _sigmoid(pg_ref[rs, :d]) * ma + _sigmoid(pg_ref[rs, d:]) * mb
        yield
        h = alpha * x_ref[rs, :] + _dot(merged, wo_ref[...])
        yield
        x1 = _layer_norm(h, g_ref[...], b_ref[...])
        x1_ref[rs, :] = x1
        yield
        lt_ref[:, rs] = _dot3(wr_ref[...], x1, NT)
        yield

    groups = [rows_stage(slice(q * MERGE_ROWS, (q + 1) * MERGE_ROWS)) for q in range(tm // MERGE_ROWS)]
    _skewed(groups)
    _route_tile(lt_ref, bias_ref, upper_ref, lower_ref, route_ref, seg_ref, cnt_ref, carry_ref)


def _merge(xf, ya, yb, pgate, wa, wb, wo, g, bta, wr, bias_col, alpha):
    n, d = xf.shape
    tm = TM_MERGE
    ne = N_GROUPS * EXPERTS_PER_GROUP
    ii = jnp.arange(tm)
    upper = (ii[:, None] < ii[None, :]).astype(BF16)
    ee = jnp.arange(ne)
    lower = (ee[:, None] > ee[None, :]).astype(BF16)
    tile = lambda a_: pl.BlockSpec((tm, a_.shape[1]), lambda i: (i, 0))
    const = lambda a_: pl.BlockSpec(a_.shape, lambda i: (0, 0))
    return pl.pallas_call(
        functools.partial(_merge_kernel, alpha),
        grid=(n // tm,),
        in_specs=[tile(xf), tile(ya), tile(yb), tile(pgate), const(wa), const(wb), const(wo), const(g),
                  const(bta), const(wr), const(bias_col), const(upper), const(lower)],
        out_specs=[pl.BlockSpec((tm, d), lambda i: (i, 0)), pl.BlockSpec((8, tm), lambda i: (0, i)),
                   pl.BlockSpec((1, 8, LANES), lambda i: (i, 0, 0)), pl.BlockSpec((ne, LANES), lambda i: (0, 0))],
        out_shape=[jax.ShapeDtypeStruct((n, d), F32), jax.ShapeDtypeStruct((8, n), F32),
                   jax.ShapeDtypeStruct((n // tm, 8, LANES), F32), jax.ShapeDtypeStruct((ne, LANES), F32)],
        scratch_shapes=[pltpu.VMEM((LANES, tm), F32), pltpu.VMEM((ne, 1), F32)],
        compiler_params=_params(("arbitrary",)),
    )(xf, ya, yb, pgate, wa, wb, wo, g, bta, wr, bias_col, upper, lower)


ROUTER_EXPERT_ROW = 8


def _to_lanes(col, nl):
    ne = col.shape[0]
    diag = lax.broadcasted_iota(I32, (ne, nl), 0) == lax.broadcasted_iota(I32, (ne, nl), 1)
    return jnp.sum(jnp.where(diag, col, 0.0), axis=0, keepdims=True)


def _route_tile(lt_ref, bias_ref, upper_ref, lower_ref, route_ref, seg_ref, cnt_ref, carry_ref):
    @pl.when(pl.program_id(0) == 0)
    def _():
        carry_ref[...] = jnp.zeros_like(carry_ref)

    ne = N_GROUPS * EXPERTS_PER_GROUP
    lt = lt_ref[...] + bias_ref[...]
    nb = lt.shape[1]
    neg = -jnp.inf
    lg = lt[0:8]
    rg = lax.broadcasted_iota(I32, (8, nb), 0).astype(F32)
    lg = jnp.where(rg < N_GROUPS, lg, neg)
    mg = jnp.max(lg, axis=0, keepdims=True)
    gidx = jnp.min(jnp.where(lg == mg, rg, 1e9), axis=0, keepdims=True)
    pg_sel = 1.0 / jnp.sum(jnp.exp(lg - mg), axis=0, keepdims=True)

    le = lt[ROUTER_EXPERT_ROW:ROUTER_EXPERT_ROW + ne]
    re = lax.broadcasted_iota(I32, (ne, nb), 0).astype(F32)
    in_group = jnp.floor(re * (1.0 / EXPERTS_PER_GROUP)) == gidx
    l1 = jnp.where(in_group, le, neg)
    m1 = jnp.max(l1, axis=0, keepdims=True)
    i1 = jnp.min(jnp.where(l1 == m1, re, 1e9), axis=0, keepdims=True)
    l2 = jnp.where(re == i1, neg, l1)
    m2 = jnp.max(l2, axis=0, keepdims=True)
    i2 = jnp.min(jnp.where(l2 == m2, re, 1e9), axis=0, keepdims=True)
    e2 = jnp.exp(m2 - m1)
    w1 = pg_sel / (1.0 + e2)
    w2 = pg_sel * e2 / (1.0 + e2)

    sel1 = re == i1
    sel2 = re == i2
    onehot = jnp.where(sel1 | sel2, 1.0, 0.0)
    before = _dg(onehot.astype(BF16), upper_ref[...])
    cnt_t = jnp.sum(onehot, axis=1, keepdims=True)
    seg = jnp.floor((cnt_t + (SEG_ALIGN - 1)) * (1.0 / SEG_ALIGN)) * SEG_ALIGN
    lstart = _dg(lower_ref[...], jnp.broadcast_to(seg, (ne, LANES)).astype(BF16))[:, 0:1]
    tot = lstart + before
    lpos1 = jnp.sum(jnp.where(sel1, tot, 0.0), axis=0, keepdims=True)
    lpos2 = jnp.sum(jnp.where(sel2, tot, 0.0), axis=0, keepdims=True)
    grel = carry_ref[...]
    carry = grel + seg
    carry_ref[...] = carry
    cnt_ref[...] = jnp.broadcast_to(carry, cnt_ref.shape)
    zero = jnp.zeros_like(w1)
    route_ref[...] = jnp.concatenate([i1, i2, lpos1, lpos2, w1, w2, zero, zero], axis=0)
    nl = seg_ref.shape[2]
    zl = jnp.zeros((1, nl), F32)
    ltot = jnp.broadcast_to(jnp.sum(seg, axis=0, keepdims=True), (1, nl))
    seg_ref[0] = jnp.concatenate([_to_lanes(seg, nl), _to_lanes(lstart, nl), _to_lanes(grel, nl), ltot,
                                  zl, zl, zl, zl], axis=0)


TAB_LANES = LANES


def _finalize_kernel(tm, seg_ref, cnt_ref, lower_ref, segtab_ref, tab_ref):
    ne = cnt_ref.shape[0]
    cnt = cnt_ref[...]
    nb = jnp.floor((cnt + (tm - 1)) * (1.0 / tm))
    bstart = _dg(lower_ref[...], nb.astype(BF16))
    bend = bstart + nb
    pad_start = bstart[:, 0:1] * tm
    pad_start_row = _to_lanes(pad_start, seg_ref.shape[2])
    for t in range(seg_ref.shape[0]):
        seg = seg_ref[t]
        segtab_ref[t] = jnp.concatenate([seg[0:2], seg[2:3] + pad_start_row, seg[3:8]], axis=0).astype(I32)

    nl = tab_ref.shape[1]
    n_used = jnp.max(bend[:, 0:1], axis=0, keepdims=True)
    pad_lo = _to_lanes(pad_start + cnt[:, 0:1], nl)
    pad_hi = _to_lanes(bend[:, 0:1] * tm, nl)
    zero = jnp.zeros((1, nl), F32)
    tab_ref[...] = jnp.concatenate([_to_lanes(bstart[:, 0:1], nl), jnp.broadcast_to(n_used, (1, nl)), pad_lo,
                                    pad_hi, _to_lanes(nb[:, 0:1], nl), zero, zero, zero], axis=0).astype(I32)


def _finalize(seg, cnt, tm):
    ntile = seg.shape[0]
    ne = cnt.shape[0]
    ii = jnp.arange(ne)
    lower = (ii[:, None] > ii[None, :]).astype(BF16)
    return pl.pallas_call(
        functools.partial(_finalize_kernel, tm),
        grid=(1,),
        in_specs=[pl.BlockSpec(seg.shape, lambda i: (0, 0, 0)), pl.BlockSpec(cnt.shape, lambda i: (0, 0)),
                  pl.BlockSpec((ne, ne), lambda i: (0, 0))],
        out_specs=[pl.BlockSpec(seg.shape, lambda i: (0, 0, 0)), pl.BlockSpec((8, TAB_LANES), lambda i: (0, 0))],
        out_shape=[jax.ShapeDtypeStruct((ntile, 8, LANES), I32), jax.ShapeDtypeStruct((8, TAB_LANES), I32)],
        compiler_params=_params(("arbitrary",)),
    )(seg, cnt, lower)


def _for_each_piece(length, max_len, fn):
    size = SEG_ALIGN
    sizes = []
    while size <= max_len:
        sizes.append(size)
        size *= 2

    for size in reversed(sizes):
        @pl.when(jnp.bitwise_and(length, size) != 0)
        def _(size=size):
            fn(pl.multiple_of(jnp.bitwise_and(length, -2 * size), SEG_ALIGN), size)


def _sorted_rows(td):
    return 2 * td + N_GROUPS * EXPERTS_PER_GROUP * SEG_ALIGN


def _dispatch_kernel(tm, seglen_ref, lstart_ref, gstart_ref, ltot_ref, plo_ref, phi_ref, nu_ref,
                     x_ref, route_ref, xbuf_ref, sorted_ref, zblk, sem, zsem):
    i = pl.program_id(0)
    nsteps = pl.num_programs(0)
    td = x_ref.shape[0]
    ne = plo_ref.shape[0]
    nblk = xbuf_ref.shape[0] // tm
    nrow = sorted_ref.shape[1]
    buf = i % 2

    lpos = route_ref[2:4, :]
    r = lax.broadcasted_iota(I32, (nrow, td), 0).astype(F32)
    onehot = jnp.where((r == lpos[0:1]) | (r == lpos[1:2]), 1.0, 0.0).astype(BF16)
    sorted_ref[buf] = _dg(onehot, x_ref[...].astype(BF16)).astype(BF16)

    def wait_tile(step, b):
        _for_each_piece(ltot_ref[step], nrow, lambda off, size: pltpu.make_async_copy(
            sorted_ref.at[b, pl.ds(0, size), :], xbuf_ref.at[pl.ds(0, size), :], sem.at[b]).wait())

    for e in range(ne):
        idx = i * ne + e
        ls = pl.multiple_of(lstart_ref[idx], SEG_ALIGN)
        gs = pl.multiple_of(gstart_ref[idx], SEG_ALIGN)
        _for_each_piece(seglen_ref[idx], td, lambda off, size: pltpu.make_async_copy(
            sorted_ref.at[buf, pl.ds(ls + off, size), :], xbuf_ref.at[pl.ds(gs + off, size), :],
            sem.at[buf]).start())

    @pl.when(i > 0)
    def _():
        wait_tile(i - 1, 1 - buf)

    @pl.when(i == nsteps - 1)
    def _():
        wait_tile(i, buf)

    def pad_fill(fn):
        for e in range(ne):
            lo = pl.multiple_of(plo_ref[e], SEG_ALIGN)
            _for_each_piece(phi_ref[e] - lo, tm // 2, lambda off, size: fn(pltpu.make_async_copy(
                zblk.at[pl.ds(0, size), :], xbuf_ref.at[pl.ds(lo + off, size), :], zsem)))

        def per_blk(b, carry):
            fn(pltpu.make_async_copy(zblk, xbuf_ref.at[pl.ds(pl.multiple_of(b * tm, tm), tm), :], zsem))
            return carry
        lax.fori_loop(nu_ref[0], nblk, per_blk, 0)

    @pl.when(i == 0)
    def _():
        zblk[...] = jnp.zeros_like(zblk)
        pad_fill(lambda cp: cp.start())
        pad_fill(lambda cp: cp.wait())


def _dispatch(seglen, lstart, gstart, ltot, pad_lo, pad_hi, n_used, x1, route, rows, tm):
    n, d = x1.shape
    td = TD_DISPATCH
    return pl.pallas_call(
        functools.partial(_dispatch_kernel, tm),
        grid_spec=pltpu.PrefetchScalarGridSpec(
            num_scalar_prefetch=7,
            grid=(n // td,),
            in_specs=[pl.BlockSpec((td, d), lambda i, *_: (i, 0)), pl.BlockSpec((8, td), lambda i, *_: (0, i))],
            out_specs=pl.BlockSpec(memory_space=pl.ANY),
            scratch_shapes=[pltpu.VMEM((2, _sorted_rows(td), d), BF16), pltpu.VMEM((tm, d), BF16),
                            pltpu.SemaphoreType.DMA((2,)), pltpu.SemaphoreType.DMA(())],
        ),
        out_shape=jax.ShapeDtypeStruct((rows, d), BF16),
        compiler_params=_params(("arbitrary",)),
    )(seglen, lstart, gstart, ltot, pad_lo, pad_hi, n_used, x1, route)


X_SLOTS = 3


def _expert_kernel(tm, bstart_ref, nb_ref, nu_ref, w1_ref, w3_ref, w2_ref, xbuf_ref, ybuf_ref,
                   wf1, wf3, wf2, w13b, w2b, xb, yb, semw, semx, semy):
    e = pl.program_id(0)
    ne = pl.num_programs(0)
    nb = nb_ref[e]
    b0 = bstart_ref[e]
    nblk = ybuf_ref.shape[0] // tm
    rows = lambda blk: pl.ds(pl.multiple_of(blk * tm, tm), tm)
    x_copy = lambda j, slot: pltpu.make_async_copy(xbuf_ref.at[rows(b0 + j), :], xb.at[slot], semx.at[slot])
    y_copy = lambda blk, slot: pltpu.make_async_copy(yb.at[slot], ybuf_ref.at[rows(blk), :], semy.at[slot])

    def w_copies(ex, slot):
        return [pltpu.make_async_copy(src.at[ex], dst.at[slot], semw.at[slot])
                for src, dst in ((w1_ref, wf1), (w3_ref, wf3), (w2_ref, wf2))]

    ws = e % 2

    @pl.when(e == 0)
    def _():
        for cp in w_copies(0, 0):
            cp.start()

    for j0 in range(X_SLOTS - 1):
        @pl.when(j0 < nb)
        def _(j0=j0):
            x_copy(j0, j0).start()

    for cp in w_copies(e, ws):
        cp.wait()
    de = w2b.shape[0]
    w13b[:, :de] = wf1[ws].astype(BF16)
    w13b[:, de:] = wf3[ws].astype(BF16)
    w2b[...] = wf2[ws].astype(BF16)

    @pl.when(e + 1 < ne)
    def _():
        for cp in w_copies(e + 1, 1 - ws):
            cp.start()

    def body(j, carry):
        slot = j % X_SLOTS
        yslot = j % 2
        x_copy(j, slot).wait()

        @pl.when(j + X_SLOTS - 1 < nb)
        def _():
            x_copy(j + X_SLOTS - 1, (j + X_SLOTS - 1) % X_SLOTS).start()

        @pl.when(j >= 2)
        def _():
            y_copy(b0 + j - 2, yslot).wait()

        x = xb[slot]
        h13 = _dg(x, w13b[...])
        h1 = h13[:, :de]
        h = (h1 * _sigmoid(h1)) * h13[:, de:]
        yb[yslot] = _dot(h, w2b[...]).astype(BF16)
        y_copy(b0 + j, yslot).start()
        return carry

    lax.fori_loop(0, nb, body, 0)

    @pl.when(nb >= 2)
    def _():
        y_copy(b0 + nb - 2, nb % 2).wait()

    @pl.when(nb >= 1)
    def _():
        y_copy(b0 + nb - 1, (nb - 1) % 2).wait()

    @pl.when(e == pl.num_programs(0) - 1)
    def _():
        yb[0] = jnp.zeros(yb.shape[1:], yb.dtype)

        def fill(fn):
            def per_blk(blk, carry):
                fn(y_copy(blk, 0))
                return carry
            lax.fori_loop(nu_ref[0], nblk, per_blk, 0)

        fill(lambda cp: cp.start())
        fill(lambda cp: cp.wait())


def _experts(bstart, nb, n_used, xbuf, w1, w3, w2):
    rows, d = xbuf.shape
    ne, _, de = w1.shape
    tm = TM_EXPERT
    return pl.pallas_call(
        functools.partial(_expert_kernel, tm),
        grid_spec=pltpu.PrefetchScalarGridSpec(
            num_scalar_prefetch=3,
            grid=(ne,),
            in_specs=[pl.BlockSpec(memory_space=pl.ANY)] * 4,
            out_specs=pl.BlockSpec(memory_space=pl.ANY),
            scratch_shapes=[pltpu.VMEM((2, d, de), F32), pltpu.VMEM((2, d, de), F32), pltpu.VMEM((2, de, d), F32),
                            pltpu.VMEM((d, 2 * de), BF16), pltpu.VMEM((de, d), BF16),
                            pltpu.VMEM((X_SLOTS, tm, d), BF16), pltpu.VMEM((2, tm, d), BF16),
                            pltpu.SemaphoreType.DMA((2,)), pltpu.SemaphoreType.DMA((X_SLOTS,)),
                            pltpu.SemaphoreType.DMA((2,))],
        ),
        out_shape=jax.ShapeDtypeStruct((rows, d), BF16),
        compiler_params=_params(("arbitrary",)),
    )(bstart, nb, n_used, w1, w3, w2, xbuf)


def _combine_kernel(alpha, seglen_ref, lstart_ref, gstart_ref, ltot_ref, x1_ref, rt_ref, p_ref, wpe_ref, wpg_ref,
                    g_ref, b_ref, ybuf_ref, out_ref, sorted_ref, sem):
    i = pl.program_id(0)
    nsteps = pl.num_programs(0)
    tc = x1_ref.shape[0]
    ne = N_GROUPS * EXPERTS_PER_GROUP
    nrow = sorted_ref.shape[1]

    def fetch(step, buf):
        for e in range(ne):
            idx = step * ne + e
            ls = pl.multiple_of(lstart_ref[idx], SEG_ALIGN)
            gs = pl.multiple_of(gstart_ref[idx], SEG_ALIGN)
            _for_each_piece(seglen_ref[idx], tc, lambda off, size: pltpu.make_async_copy(
                ybuf_ref.at[pl.ds(gs + off, size), :], sorted_ref.at[buf, pl.ds(ls + off, size), :],
                sem.at[buf]).start())

    @pl.when(i == 0)
    def _():
        sorted_ref[...] = jnp.zeros_like(sorted_ref)
        fetch(0, 0)

    @pl.when(i + 1 < nsteps)
    def _():
        fetch(i + 1, (i + 1) % 2)

    cur = i % 2
    _for_each_piece(ltot_ref[i], nrow, lambda off, size: pltpu.make_async_copy(
        ybuf_ref.at[pl.ds(0, size), :], sorted_ref.at[cur, pl.ds(0, size), :], sem.at[cur]).wait())

    rt = rt_ref[...]
    r = lax.broadcasted_iota(I32, (tc, nrow), 1).astype(F32)
    unsort = jnp.where(r == rt[:, 0:1], rt[:, 2:3], 0.0) + jnp.where(r == rt[:, 1:2], rt[:, 3:4], 0.0)
    ffn = _dg(unsort.astype(BF16), sorted_ref[cur])
    x2 = _layer_norm(alpha * x1_ref[...] + ffn, g_ref[...], b_ref[...])
    gate = _sigmoid(_dot(x2, wpg_ref[...]))
    out_ref[...] = x2 + gate * _dot(p_ref[...], wpe_ref[...])


def _combine(seglen, lstart, gstart, ltot, x1, route_t, pf, wpe, wpg, g, bta, ybuf, alpha):
    n, d = x1.shape
    tc = TD_DISPATCH
    tile = lambda a_: pl.BlockSpec((tc, a_.shape[1]), lambda i, *_: (i, 0))
    const = lambda a_: pl.BlockSpec(a_.shape, lambda i, *_: (0, 0))
    return pl.pallas_call(
        functools.partial(_combine_kernel, alpha),
        grid_spec=pltpu.PrefetchScalarGridSpec(
            num_scalar_prefetch=4,
            grid=(n // tc,),
            in_specs=[tile(x1), tile(route_t), tile(pf), const(wpe), const(wpg), const(g), const(bta),
                      pl.BlockSpec(memory_space=pl.ANY)],
            out_specs=pl.BlockSpec((tc, d), lambda i, *_: (i, 0)),
            scratch_shapes=[pltpu.VMEM((2, _sorted_rows(tc), d), BF16), pltpu.SemaphoreType.DMA((2,))],
        ),
        out_shape=jax.ShapeDtypeStruct((n, d), F32),
        compiler_params=_params(("arbitrary",)),
    )(seglen, lstart, gstart, ltot, x1, route_t, pf, wpe, wpg, g, bta, ybuf)


def _layer(x, p_i, w_in, rw_mu, rw_w0, rw_w_up, rw_a0, rw_a_up, rw_g_up, rw_k_k, rw_k_a, rw_r_k, rw_gn_w,
           rw_gn_b, w_a_out, hg_lb_logits, hg_norm_w, w_b_out, w_o, ln1_g, ln1_b, router_g_w, router_g_b,
           router_e_w, router_e_b, w1, w3, w2, ln2_g, ln2_b, w_pe, w_pg, alpha, layer):
    b, t, d = x.shape
    n = b * t
    rw_dim = rw_w0.shape[0]
    rw_cols = 3 * rw_dim + RW_DECAY_LORA + RW_A_LORA + RW_GATE_LORA
    hg_cols = 4 * hg_norm_w.shape[0]
    ne = N_GROUPS * EXPERTS_PER_GROUP
    row2 = lambda a_: a_.reshape(1, -1)
    xf = x.reshape(n, d)

    wb = w_in.astype(BF16)
    proj_rw, proj_hg, proj_gt = _project(xf, wb[:, :rw_cols], wb[:, rw_cols:rw_cols + hg_cols],
                                         wb[:, rw_cols + hg_cols:])
    ya = _rwkv_branch(proj_rw.reshape(b, t, rw_cols), rw_mu, row2(rw_w0), rw_w_up, rw_a0, rw_a_up, rw_g_up,
                      rw_k_k, rw_k_a, rw_r_k, rw_gn_w, rw_gn_b)
    yb = _hgrn_branch(proj_hg.reshape(b, t, hg_cols), hg_lb_logits, hg_norm_w, layer)

    wr = jnp.zeros((LANES, d), F32)
    wr = wr.at[:N_GROUPS].set(router_g_w.T).at[ROUTER_EXPERT_ROW:ROUTER_EXPERT_ROW + ne].set(router_e_w.T)
    bias = jnp.zeros((LANES,), F32)
    bias = bias.at[:N_GROUPS].set(router_g_b).at[ROUTER_EXPERT_ROW:ROUTER_EXPERT_ROW + ne].set(router_e_b)
    x1, route, seg, cnt = _merge(xf, ya.reshape(n, -1), yb.reshape(n, -1), proj_gt, w_a_out.astype(BF16),
                                 w_b_out.astype(BF16), w_o.astype(BF16), row2(ln1_g), row2(ln1_b), wr,
                                 bias.reshape(LANES, 1), alpha)

    tm = TM_EXPERT
    ntile = n // TD_DISPATCH
    nblk = -(-(2 * n + (SEG_ALIGN - 1) * ne * ntile) // tm) + ne
    assert TM_MERGE == TD_DISPATCH
    segtab, tab = _finalize(seg, cnt, tm)
    per_seg = lambda row: segtab[:, row, :ne].reshape(-1)
    seglen, lstart, gstart, ltot = per_seg(0), per_seg(1), per_seg(2), segtab[:, 3, 0]
    n_used = tab[1, :1]

    xbuf = _dispatch(seglen, lstart, gstart, ltot, tab[2, :ne], tab[3, :ne], n_used, x1, route, nblk * tm, tm)
    ybuf = _experts(tab[0, :ne], tab[4, :ne], n_used, xbuf, w1, w3, w2)
    out = _combine(seglen, lstart, gstart, ltot, x1, route[2:6].T, p_i.reshape(n, -1), w_pe.astype(BF16),
                   w_pg.astype(BF16), row2(ln2_g), row2(ln2_b), ybuf, alpha)
    return out.reshape(b, t, d)


def kernel(x, p, w_in, rw_mu, rw_w0, rw_w_up, rw_a0, rw_a_up, rw_g_up, rw_k_k, rw_k_a, rw_r_k, rw_gn_w, rw_gn_b,
           w_a_out, hg_lb_logits, hg_norm_w, w_b_out, w_o, ln1_g, ln1_b, router_g_w, router_g_b, router_e_w,
           router_e_b, w1, w3, w2, ln2_g, ln2_b, w_pe, w_pg):
    depth = w_in.shape[0]
    alpha = (2 * depth) ** 0.25
    for i in range(depth):
        x = _layer(x, p[i], w_in[i], rw_mu[i], rw_w0[i], rw_w_up[i], rw_a0[i], rw_a_up[i], rw_g_up[i], rw_k_k[i],
                   rw_k_a[i], rw_r_k[i].reshape(-1), rw_gn_w[i], rw_gn_b[i], w_a_out[i], hg_lb_logits,
                   hg_norm_w[i], w_b_out[i], w_o[i], ln1_g[i], ln1_b[i],
                   router_g_w[i], router_g_b[i], router_e_w[i], router_e_b[i], w1[i], w3[i], w2[i], ln2_g[i],
                   ln2_b[i], w_pe[i], w_pg[i], alpha, i)
    return x
```

```python
import functools

import jax
import jax.numpy as jnp
from jax import lax
from jax.experimental import pallas as pl
from jax.experimental.pallas import tpu as pltpu

F32 = jnp.float32
BF16 = jnp.bfloat16
I32 = jnp.int32

NN = (((1,), (0,)), ((), ()))
NT = (((1,), (1,)), ((), ()))

RW_HEAD = 64
RW_DECAY_LORA = 64
RW_A_LORA = 64
RW_GATE_LORA = 128
RW_GN_EPS = 64e-5
RW_DECAY_SCALE = 0.6065306597126334
N_GROUPS = 4
EXPERTS_PER_GROUP = 8
LN_EPS = 1e-5
RMS_EPS = 1e-6

CHUNK = 64
SUB = 16
LANES = 128
VMEM_LIMIT = 56 * 1024 * 1024

TM_PROJ = 256
TB_RWKV = 512
TB_HGRN = 256
TM_MERGE = 512
MERGE_ROWS = 256
TD_DISPATCH = 512
SEG_ALIGN = 16
TM_EXPERT = 256
RW_WIDE = 4
HG_WIDE = 4


def _dg(a, b, dn=NN):
    return lax.dot_general(a, b, dn, preferred_element_type=F32)


def _dot(a, b, dn=NN):
    return _dg(a.astype(BF16), b.astype(BF16), dn)


def _split(a):
    hi = a.astype(BF16)
    lo = (a - hi.astype(F32)).astype(BF16)
    return hi, lo


def _dot_hl(a, b_exact, dn=NN):
    hi, lo = _split(a)
    return _dg(hi, b_exact, dn) + _dg(lo, b_exact, dn)


def _dot3(a, b, dn=NN):
    ah, al = _split(a)
    bh, bl = _split(b)
    return _dg(ah, bh, dn) + (_dg(ah, bl, dn) + _dg(al, bh, dn))


def _cumsum_chunks(x, tri):
    h = x.astype(BF16)
    r1 = x - h.astype(F32)
    m = r1.astype(BF16)
    l = (r1 - m.astype(F32)).astype(BF16)
    return _dg(tri, h) + (_dg(tri, m) + _dg(tri, l))


def _sigmoid(x):
    return 0.5 * jnp.tanh(0.5 * x) + 0.5


def _layer_norm(h, g, b):
    mu = jnp.mean(h, axis=-1, keepdims=True)
    d = h - mu
    var = jnp.mean(d * d, axis=-1, keepdims=True)
    return d * lax.rsqrt(var + LN_EPS) * g + b


def _params(sem):
    return pltpu.CompilerParams(dimension_semantics=sem, vmem_limit_bytes=VMEM_LIMIT)


def _proj_kernel(x_ref, wr_ref, wh_ref, wg_ref, pr_ref, ph_ref, pg_ref):
    xb = x_ref[...].astype(BF16)
    pr_ref[...] = _dg(xb, wr_ref[...])
    ph_ref[...] = _dg(xb, wh_ref[...])
    pg_ref[...] = _dg(xb, wg_ref[...])


def _project(xf, w_rw, w_hg, w_gt):
    n, d = xf.shape
    tm = TM_PROJ
    full = lambda w: pl.BlockSpec(w.shape, lambda i: (0, 0))
    tile = lambda c: pl.BlockSpec((tm, c), lambda i: (i, 0))
    return pl.pallas_call(
        _proj_kernel,
        grid=(n // tm,),
        in_specs=[tile(d), full(w_rw), full(w_hg), full(w_gt)],
        out_specs=[tile(w_rw.shape[1]), tile(w_hg.shape[1]), tile(w_gt.shape[1])],
        out_shape=[jax.ShapeDtypeStruct((n, w.shape[1]), F32) for w in (w_rw, w_hg, w_gt)],
        compiler_params=_params(("parallel",)),
    )(xf, w_rw, w_hg, w_gt)


def _each(f, *ls):
    return [f(*xs) for xs in zip(*ls)]


def _two(x):
    m1 = lax.broadcasted_iota(I32, x.shape, 1) < RW_HEAD
    return jnp.concatenate([jnp.where(m1, x, 0.0), jnp.where(m1, 0.0, x)], axis=0)


def _rwkv_chunk_prepare(ins, lvl_ref, out):
    c = CHUNK
    lane = lax.broadcasted_iota(I32, (c, LANES), 1)
    row = lax.broadcasted_iota(I32, (c, LANES), 0)
    scol = jnp.bitwise_and(lane, RW_HEAD - 1)
    strict = row > scol
    incl = row >= scol
    r2 = lax.broadcasted_iota(I32, (LANES, LANES), 0)
    c2 = lax.broadcasted_iota(I32, (LANES, LANES), 1)
    eye = jnp.where(r2 == c2, 1.0, 0.0).astype(F32)

    def prep(r, k, v, av, bv, lw, lc):
        l_end = lc[c - 1:c]
        e_r = jnp.exp(l_end - lc)
        e_n = jnp.exp(-lc)
        return dict(at=av * jnp.exp(lc - lw), rt=r * jnp.exp(lc), bt=bv * e_n, kt=k * e_n,
                    bk=jnp.concatenate([bv * e_r, k * e_r], axis=0), pc=jnp.exp(l_end), v=v)

    q = [prep(*xs) for xs in ins]
    yield
    p = [_dot(jnp.concatenate([d["at"], d["rt"]], axis=0),
              jnp.concatenate([_two(d["bt"]), _two(d["kt"])], axis=0), NT) for d in q]
    sab = [jnp.where(strict, x[:c, :LANES], 0.0) for x in p]
    sak = [jnp.where(strict, x[:c, LANES:], 0.0) for x in p]
    srb = [jnp.where(incl, x[c:, :LANES], 0.0) for x in p]
    srk = [jnp.where(incl, x[c:, LANES:], 0.0) for x in p]
    yield
    sv = _each(lambda ak, rk, d: _dot(jnp.concatenate([ak, rk], axis=0), _two(d["v"])), sak, srk, q)
    yield

    a_bd = [_two(x) for x in sab]
    t = [eye + a * lvl_ref[0] for a in a_bd]
    for lv in range(1, lvl_ref.shape[0]):
        ta = _each(lambda t_, a: _dot(t_, a * lvl_ref[lv]), t, a_bd)
        yield
        t = _each(lambda t_, ta_: t_ + _dot(ta_, t_), t, ta)
        yield

    x = _each(lambda t_, d, sv_: _dot(t_, jnp.concatenate([_two(d["at"]), _two(sv_[:c])], axis=1)), t, q, sv)
    out.extend(dict(uk=x_[:c, :LANES] + x_[c:, :LANES],
                    w=x_[:c, LANES:] + x_[c:, LANES:],
                    rt=d["rt"], rkv=sv_[c:], srb=srb_, bk=d["bk"], v=d["v"], pc=d["pc"])
               for x_, d, sv_, srb_ in zip(x, q, sv, srb))


def _rwkv_chunk_apply(prep, states, bd, out):
    c = CHUNK
    g1 = _each(lambda d, s: _dot(jnp.concatenate([d["uk"], d["rt"]], axis=0), s, NT), prep, states)
    yield
    u = _each(lambda g, d: g[:c] + d["w"], g1, prep)
    y = _each(lambda g, d, u_: g[c:] + d["rkv"] + _dot(d["srb"], _two(u_)), g1, prep, u)
    yield
    upd = _each(lambda u_, d: _dot(jnp.concatenate([u_, d["v"]], axis=0).T, d["bk"]), u, prep)
    s_new = _each(lambda d, s, up: s * d["pc"] + bd * up, prep, states, upd)
    out.extend(zip(y, s_new))
    yield


def _skewed(gens):
    live = set(range(len(gens)))
    tick = 0
    while live:
        for q in sorted(live):
            if q <= tick:
                try:
                    next(gens[q])
                except StopIteration:
                    live.discard(q)
        tick += 1


def _interleave(*gens):
    live = [g for g in gens if g is not None]
    while live:
        for g in list(live):
            try:
                next(g)
            except StopIteration:
                live.remove(g)


def _rwkv_kernel(u_ref, mu_ref, w0_ref, wup_ref, a0_ref, aup_ref, gup_ref, kk_ref, ka_ref, rk_ref,
                 gnw_ref, gnb_ref, tri_ref, gsum_ref, lvl_ref, bd_ref, ya_ref,
                 s_ref, prev_ref, r_s, k_s, v_s, a_s, b_s, lw_s, lc_s, g_s, y_s):
    tb = pl.program_id(1)

    @pl.when(tb == 0)
    def _():
        s_ref[...] = jnp.zeros_like(s_ref)
        prev_ref[...] = jnp.zeros_like(prev_ref)

    nt = u_ref.shape[1]
    dim = r_s.shape[1]
    part = CHUNK * RW_WIDE
    npair = dim // LANES
    lanes = [slice(p * LANES, (p + 1) * LANES) for p in range(npair)]
    bd = bd_ref[...]
    gsum = gsum_ref[...]

    def prologue(h):
        rows = slice(h * part, (h + 1) * part)
        u = u_ref[0, rows, :]
        before = prev_ref[...] if h == 0 else u_ref[0, h * part - 1:h * part, :]
        rowid = lax.broadcasted_iota(I32, u.shape, 0)
        shifted = jnp.where(rowid == 0, before, pltpu.roll(u, 1, axis=0))
        um = u + (shifted - u) * mu_ref[...]
        r = um[:, 0:dim]
        k = um[:, dim:2 * dim]
        v = um[:, 2 * dim:3 * dim]
        xwa = um[:, 3 * dim:3 * dim + LANES]
        xg = um[:, 3 * dim + LANES:3 * dim + 2 * LANES]
        yield
        wpre = w0_ref[...] + _dot(jnp.tanh(xwa), wup_ref[...])
        lw = -RW_DECAY_SCALE * _sigmoid(wpre)
        a = _sigmoid(a0_ref[...] + _dot(xwa, aup_ref[...]))
        g_s[rows, :] = _dot(_sigmoid(xg), gup_ref[...])
        yield
        kk = k * kk_ref[...]
        ss = _dot(kk * kk, gsum)
        kk = kk * lax.rsqrt(jnp.maximum(ss, 1e-24))
        r_s[rows, :] = r
        k_s[rows, :] = k * (1.0 + (a - 1.0) * ka_ref[...])
        v_s[rows, :] = v
        a_s[rows, :] = -kk
        b_s[rows, :] = kk * a
        yield
        lw_s[rows, :] = lw
        lc_s[rows, :] = _cumsum_chunks(lw, tri_ref[...])

    def chunk_rows(h):
        return [slice(h * part + ci * CHUNK, h * part + (ci + 1) * CHUNK) for ci in range(RW_WIDE)]

    def prepare(h, out):
        ins = [(r_s[rw, ls], k_s[rw, ls], v_s[rw, ls], a_s[rw, ls], b_s[rw, ls], lw_s[rw, ls], lc_s[rw, ls])
               for rw in chunk_rows(h) for ls in lanes]
        yield from _rwkv_chunk_prepare(ins, lvl_ref, out)

    def apply(h, prep):
        states = [s_ref[p] for p in range(npair)]
        for ci, rw in enumerate(chunk_rows(h)):
            outs = []
            yield from _rwkv_chunk_apply(prep[ci * npair:(ci + 1) * npair], states, bd, outs)
            states = [s_new for _, s_new in outs]
            for (y, _), ls in zip(outs, lanes):
                y_s[rw, ls] = y
        for p in range(npair):
            s_ref[p] = states[p]

    def epilogue(h):
        rows = slice(h * part, (h + 1) * part)
        y = y_s[rows, :]
        inv_n = 1.0 / RW_HEAD
        m = _dot_hl(y, gsum) * inv_n
        d = y - m
        yield
        var = _dot(d * d, gsum) * inv_n
        yn = d * lax.rsqrt(var + RW_GN_EPS) * gnw_ref[...] + gnb_ref[...]
        yield
        bonus = _dot(r_s[rows, :] * k_s[rows, :] * rk_ref[...], gsum) * v_s[rows, :]
        ya_ref[0, rows, :] = ((yn + bonus) * g_s[rows, :]).astype(BF16)

    nparts = nt // part
    preps = [[] for _ in range(nparts)]
    _interleave(prologue(0))
    prev_ref[...] = u_ref[0, nt - 1:nt, :]
    _interleave(prepare(0, preps[0]), prologue(1) if nparts > 1 else None)
    for h in range(nparts):
        _interleave(apply(h, preps[h]),
                    prepare(h + 1, preps[h + 1]) if h + 1 < nparts else None,
                    prologue(h + 2) if h + 2 < nparts else None,
                    epilogue(h - 1) if h > 0 else None)
    _interleave(epilogue(nparts - 1))


def _rwkv_branch(proj_rw, mu, w0, wup, a0, aup, gup, k_k, k_a, r_k, gn_w, gn_b):
    b, t, cols = proj_rw.shape
    dim = w0.shape[1]
    tb = TB_RWKV
    ii = jnp.arange(CHUNK * RW_WIDE)
    tri = ((ii[:, None] // CHUNK == ii[None, :] // CHUNK) & (ii[:, None] >= ii[None, :])).astype(BF16)
    jj = jnp.arange(dim)
    gsum = (jj[:, None] // RW_HEAD == jj[None, :] // RW_HEAD).astype(BF16)
    rr = jnp.arange(LANES)[:, None]
    cc = jnp.arange(LANES)[None, :]
    lvls = []
    s = 1
    while s < CHUNK:
        lvls.append(((rr // (2 * s) == cc // (2 * s)) & ((rr // s) % 2 == 1) & ((cc // s) % 2 == 0)).astype(F32))
        s *= 2
    lvl = jnp.stack(lvls)
    bd = (rr // RW_HEAD == cc // RW_HEAD).astype(F32)
    zpad = lambda rows: jnp.zeros((rows, dim), F32)
    wup_p = jnp.concatenate([wup, zpad(LANES - wup.shape[0])], axis=0).astype(BF16)
    aup_p = jnp.concatenate([zpad(LANES - aup.shape[0]), aup], axis=0).astype(BF16)
    row2 = lambda a_: a_.reshape(1, -1)
    const = lambda a_: pl.BlockSpec(a_.shape, lambda bi, ti: (0,) * a_.ndim)
    args = [row2(mu), row2(w0), wup_p, row2(a0), aup_p, gup.astype(BF16), row2(k_k), row2(k_a), row2(r_k),
            row2(gn_w), row2(gn_b), tri, gsum, lvl, bd]
    sc = lambda: pltpu.VMEM((tb, dim), F32)
    return pl.pallas_call(
        _rwkv_kernel,
        grid=(b, t // tb),
        in_specs=[pl.BlockSpec((1, tb, cols), lambda bi, ti: (bi, ti, 0))] + [const(a_) for a_ in args],
        out_specs=pl.BlockSpec((1, tb, dim), lambda bi, ti: (bi, ti, 0)),
        out_shape=jax.ShapeDtypeStruct((b, t, dim), BF16),
        scratch_shapes=[pltpu.VMEM((dim // LANES, LANES, LANES), F32), pltpu.VMEM((1, cols), F32)]
                       + [sc() for _ in range(9)],
        compiler_params=_params(("arbitrary", "arbitrary")),
    )(proj_rw, *args)


def _hgrn_chunk_prepare(ins):
    c = CHUNK
    subs = [(SUB * i, SUB * (i + 1)) for i in range(c // SUB)]

    def scores(q, k, lf, bc, lo, hi):
        m = bc[lo:lo + 1] - lf[lo:lo + 1]
        att = _dot(q[lo:hi] * jnp.exp(bc[lo:hi] - m), k[:hi] * jnp.exp(m - bc[:hi]), NT)
        tt = lax.broadcasted_iota(I32, (SUB, hi), 0) + lo
        s_ = lax.broadcasted_iota(I32, (SUB, hi), 1)
        return jnp.where(s_ <= tt, att, 0.0)

    att = [[scores(q, k, lf, bc, lo, hi) for lo, hi in subs] for q, k, v, lf, bc in ins]
    upd = [_dot(v.T, k * jnp.exp(bc[c - 1:c] - bc)) for q, k, v, lf, bc in ins]
    intra = [[_dot(a, x[2][:hi]) for a, (lo, hi) in zip(arow, subs)] for arow, x in zip(att, ins)]
    return [dict(intra=jnp.concatenate(rows, axis=0), qe=x[0] * jnp.exp(x[4]), upd=up, pc=jnp.exp(x[4][c - 1:c]))
            for rows, x, up in zip(intra, ins, upd)]


def _hgrn_chunk_apply(prep, states):
    o = _each(lambda d, st: d["intra"] + _dot(d["qe"], st, NT), prep, states)
    st_new = _each(lambda d, st: st * d["pc"] + d["upd"], prep, states)
    return list(zip(o, st_new))


def _hgrn_kernel(layer, u_ref, lbl_ref, nw_ref, tri_ref, gsum_ref, yb_ref,
                 st_ref, q_s, k_s, v_s, lf_s, bc_s, o_s):
    tb = pl.program_id(1)

    @pl.when(tb == 0)
    def _():
        st_ref[...] = jnp.zeros_like(st_ref)

    u = u_ref[0]
    nt = u.shape[0]
    dim = q_s.shape[1]
    lbl = lbl_ref[...]
    e = jnp.exp(lbl - jnp.max(lbl, axis=0, keepdims=True))
    lb = jnp.sum(e[0:layer + 1], axis=0, keepdims=True) / jnp.sum(e, axis=0, keepdims=True)
    zf = u[:, dim:2 * dim]
    sig = _sigmoid(zf)
    f = lb + (1.0 - lb) * sig
    qin = u[:, 0:dim]
    q_s[...] = qin * _sigmoid(qin)
    k_s[...] = (1.0 - lb) * (1.0 - sig)
    v_s[...] = u[:, 2 * dim:3 * dim]
    lf = jnp.log(f)
    lf_s[...] = lf
    bc_s[...] = _cumsum_chunks(lf, tri_ref[...])

    nhead = dim // LANES
    lanes = [slice(h * LANES, (h + 1) * LANES) for h in range(nhead)]

    def group_body(gi, carry):
        rows = [pl.ds(pl.multiple_of((gi * HG_WIDE + ci) * CHUNK, CHUNK), CHUNK) for ci in range(HG_WIDE)]
        ins = [(q_s[rw, ls], k_s[rw, ls], v_s[rw, ls], lf_s[rw, ls], bc_s[rw, ls]) for rw in rows for ls in lanes]
        prep = _hgrn_chunk_prepare(ins)
        states = [st_ref[h] for h in range(nhead)]
        for ci, rw in enumerate(rows):
            outs = _hgrn_chunk_apply(prep[ci * nhead:(ci + 1) * nhead], states)
            states = [st_new for _, st_new in outs]
            for (o, _), ls in zip(outs, lanes):
                o_s[rw, ls] = o
        for h in range(nhead):
            st_ref[h] = states[h]
        return carry

    lax.fori_loop(0, nt // (CHUNK * HG_WIDE), group_body, 0)

    o = o_s[...]
    ms = _dot(o * o, gsum_ref[...]) * (1.0 / LANES)
    og = u[:, 3 * dim:4 * dim]
    yb_ref[0] = (o * lax.rsqrt(ms + RMS_EPS) * nw_ref[...] * _sigmoid(og)).astype(BF16)


def _hgrn_branch(proj_hg, lb_logits, norm_w, layer):
    b, t, cols = proj_hg.shape
    dim = cols // 4
    tb = TB_HGRN
    ii = jnp.arange(tb)
    tri = ((ii[:, None] // CHUNK == ii[None, :] // CHUNK) & (ii[:, None] >= ii[None, :])).astype(BF16)
    jj = jnp.arange(dim)
    gsum = (jj[:, None] // LANES == jj[None, :] // LANES).astype(BF16)
    const = lambda a_: pl.BlockSpec(a_.shape, lambda bi, ti: (0,) * a_.ndim)
    args = [lb_logits, norm_w.reshape(1, -1), tri, gsum]
    sc = lambda: pltpu.VMEM((tb, dim), F32)
    return pl.pallas_call(
        functools.partial(_hgrn_kernel, layer),
        grid=(b, t // tb),
        in_specs=[pl.BlockSpec((1, tb, cols), lambda bi, ti: (bi, ti, 0))] + [const(a_) for a_ in args],
        out_specs=pl.BlockSpec((1, tb, dim), lambda bi, ti: (bi, ti, 0)),
        out_shape=jax.ShapeDtypeStruct((b, t, dim), BF16),
        scratch_shapes=[pltpu.VMEM((dim // LANES, LANES, LANES), F32)] + [sc() for _ in range(6)],
        compiler_params=_params(("arbitrary", "arbitrary")),
    )(proj_hg, *args)


def _merge_kernel(alpha, x_ref, ya_ref, yb_ref, pg_ref, wa_ref, wb_ref, wo_ref, g_ref, b_ref, wr_ref,
                  bias_ref, upper_ref, lower_ref, x1_ref, route_ref, seg_ref, cnt_ref, lt_ref, carry_ref):
    tm, d = x_ref.shape

    def rows_stage(rs):
        ma = _dg(ya_ref[rs, :], wa_ref[...])
        mb = _dg(yb_ref[rs, :], wb_ref[...])
        yield
        merged = _sigmoid(pg_ref[rs, :d]) * ma + _sigmoid(pg_ref[rs, d:]) * mb
        yield
        h = alpha * x_ref[rs, :] + _dot(merged, wo_ref[...])
        yield
        x1 = _layer_norm(h, g_ref[...], b_ref[...])
        x1_ref[rs, :] = x1
        yield
        lt_ref[:, rs] = _dot3(wr_ref[...], x1, NT)
        yield

    groups = [rows_stage(slice(q * MERGE_ROWS, (q + 1) * MERGE_ROWS)) for q in range(tm // MERGE_ROWS)]
    _skewed(groups)
    _route_tile(lt_ref, bias_ref, upper_ref, lower_ref, route_ref, seg_ref, cnt_ref, carry_ref)


def _merge(xf, ya, yb, pgate, wa, wb, wo, g, bta, wr, bias_col, alpha):
    n, d = xf.shape
    tm = TM_MERGE
    ne = N_GROUPS * EXPERTS_PER_GROUP
    ii = jnp.arange(tm)
    upper = (ii[:, None] < ii[None, :]).astype(BF16)
    ee = jnp.arange(ne)
    lower = (ee[:, None] > ee[None, :]).astype(BF16)
    tile = lambda a_: pl.BlockSpec((tm, a_.shape[1]), lambda i: (i, 0))
    const = lambda a_: pl.BlockSpec(a_.shape, lambda i: (0, 0))
    return pl.pallas_call(
        functools.partial(_merge_kernel, alpha),
        grid=(n // tm,),
        in_specs=[tile(xf), tile(ya), tile(yb), tile(pgate), const(wa), const(wb), const(wo), const(g),
                  const(bta), const(wr), const(bias_col), const(upper), const(lower)],
        out_specs=[pl.BlockSpec((tm, d), lambda i: (i, 0)), pl.BlockSpec((8, tm), lambda i: (0, i)),
                   pl.BlockSpec((1, 8, LANES), lambda i: (i, 0, 0)), pl.BlockSpec((ne, LANES), lambda i: (0, 0))],
        out_shape=[jax.ShapeDtypeStruct((n, d), F32), jax.ShapeDtypeStruct((8, n), F32),
                   jax.ShapeDtypeStruct((n // tm, 8, LANES), F32), jax.ShapeDtypeStruct((ne, LANES), F32)],
        scratch_shapes=[pltpu.VMEM((LANES, tm), F32), pltpu.VMEM((ne, 1), F32)],
        compiler_params=_params(("arbitrary",)),
    )(xf, ya, yb, pgate, wa, wb, wo, g, bta, wr, bias_col, upper, lower)


ROUTER_EXPERT_ROW = 8


def _to_lanes(col, nl):
    ne = col.shape[0]
    diag = lax.broadcasted_iota(I32, (ne, nl), 0) == lax.broadcasted_iota(I32, (ne, nl), 1)
    return jnp.sum(jnp.where(diag, col, 0.0), axis=0, keepdims=True)


def _route_tile(lt_ref, bias_ref, upper_ref, lower_ref, route_ref, seg_ref, cnt_ref, carry_ref):
    @pl.when(pl.program_id(0) == 0)
    def _():
        carry_ref[...] = jnp.zeros_like(carry_ref)

    ne = N_GROUPS * EXPERTS_PER_GROUP
    lt = lt_ref[...] + bias_ref[...]
    nb = lt.shape[1]
    neg = -jnp.inf
    lg = lt[0:8]
    rg = lax.broadcasted_iota(I32, (8, nb), 0).astype(F32)
    lg = jnp.where(rg < N_GROUPS, lg, neg)
    mg = jnp.max(lg, axis=0, keepdims=True)
    gidx = jnp.min(jnp.where(lg == mg, rg, 1e9), axis=0, keepdims=True)
    pg_sel = 1.0 / jnp.sum(jnp.exp(lg - mg), axis=0, keepdims=True)

    le = lt[ROUTER_EXPERT_ROW:ROUTER_EXPERT_ROW + ne]
    re = lax.broadcasted_iota(I32, (ne, nb), 0).astype(F32)
    in_group = jnp.floor(re * (1.0 / EXPERTS_PER_GROUP)) == gidx
    l1 = jnp.where(in_group, le, neg)
    m1 = jnp.max(l1, axis=0, keepdims=True)
    i1 = jnp.min(jnp.where(l1 == m1, re, 1e9), axis=0, keepdims=True)
    l2 = jnp.where(re == i1, neg, l1)
    m2 = jnp.max(l2, axis=0, keepdims=True)
    i2 = jnp.min(jnp.where(l2 == m2, re, 1e9), axis=0, keepdims=True)
    e2 = jnp.exp(m2 - m1)
    w1 = pg_sel / (1.0 + e2)
    w2 = pg_sel * e2 / (1.0 + e2)

    sel1 = re == i1
    sel2 = re == i2
    onehot = jnp.where(sel1 | sel2, 1.0, 0.0)
    before = _dg(onehot.astype(BF16), upper_ref[...])
    cnt_t = jnp.sum(onehot, axis=1, keepdims=True)
    seg = jnp.floor((cnt_t + (SEG_ALIGN - 1)) * (1.0 / SEG_ALIGN)) * SEG_ALIGN
    lstart = _dg(lower_ref[...], jnp.broadcast_to(seg, (ne, LANES)).astype(BF16))[:, 0:1]
    tot = lstart + before
    lpos1 = jnp.sum(jnp.where(sel1, tot, 0.0), axis=0, keepdims=True)
    lpos2 = jnp.sum(jnp.where(sel2, tot, 0.0), axis=0, keepdims=True)
    grel = carry_ref[...]
    carry = grel + seg
    carry_ref[...] = carry
    cnt_ref[...] = jnp.broadcast_to(carry, cnt_ref.shape)
    zero = jnp.zeros_like(w1)
    route_ref[...] = jnp.concatenate([i1, i2, lpos1, lpos2, w1, w2, zero, zero], axis=0)
    nl = seg_ref.shape[2]
    zl = jnp.zeros((1, nl), F32)
    ltot = jnp.broadcast_to(jnp.sum(seg, axis=0, keepdims=True), (1, nl))
    seg_ref[0] = jnp.concatenate([_to_lanes(seg, nl), _to_lanes(lstart, nl), _to_lanes(grel, nl), ltot,
                                  zl, zl, zl, zl], axis=0)


TAB_LANES = LANES


def _finalize_kernel(tm, seg_ref, cnt_ref, lower_ref, segtab_ref, tab_ref):
    ne = cnt_ref.shape[0]
    cnt = cnt_ref[...]
    nb = jnp.floor((cnt + (tm - 1)) * (1.0 / tm))
    bstart = _dg(lower_ref[...], nb.astype(BF16))
    bend = bstart + nb
    pad_start = bstart[:, 0:1] * tm
    pad_start_row = _to_lanes(pad_start, seg_ref.shape[2])
    for t in range(seg_ref.shape[0]):
        seg = seg_ref[t]
        segtab_ref[t] = jnp.concatenate([seg[0:2], seg[2:3] + pad_start_row, seg[3:8]], axis=0).astype(I32)

    nl = tab_ref.shape[1]
    n_used = jnp.max(bend[:, 0:1], axis=0, keepdims=True)
    pad_lo = _to_lanes(pad_start + cnt[:, 0:1], nl)
    pad_hi = _to_lanes(bend[:, 0:1] * tm, nl)
    zero = jnp.zeros((1, nl), F32)
    tab_ref[...] = jnp.concatenate([_to_lanes(bstart[:, 0:1], nl), jnp.broadcast_to(n_used, (1, nl)), pad_lo,
                                    pad_hi, _to_lanes(nb[:, 0:1], nl), zero, zero, zero], axis=0).astype(I32)


def _finalize(seg, cnt, tm):
    ntile = seg.shape[0]
    ne = cnt.shape[0]
    ii = jnp.arange(ne)
    lower = (ii[:, None] > ii[None, :]).astype(BF16)
    return pl.pallas_call(
        functools.partial(_finalize_kernel, tm),
        grid=(1,),
        in_specs=[pl.BlockSpec(seg.shape, lambda i: (0, 0, 0)), pl.BlockSpec(cnt.shape, lambda i: (0, 0)),
                  pl.BlockSpec((ne, ne), lambda i: (0, 0))],
        out_specs=[pl.BlockSpec(seg.shape, lambda i: (0, 0, 0)), pl.BlockSpec((8, TAB_LANES), lambda i: (0, 0))],
        out_shape=[jax.ShapeDtypeStruct((ntile, 8, LANES), I32), jax.ShapeDtypeStruct((8, TAB_LANES), I32)],
        compiler_params=_params(("arbitrary",)),
    )(seg, cnt, lower)


def _for_each_piece(length, max_len, fn):
    size = SEG_ALIGN
    sizes = []
    while size <= max_len:
        sizes.append(size)
        size *= 2

    for size in reversed(sizes):
        @pl.when(jnp.bitwise_and(length, size) != 0)
        def _(size=size):
            fn(pl.multiple_of(jnp.bitwise_and(length, -2 * size), SEG_ALIGN), size)


def _sorted_rows(td):
    return 2 * td + N_GROUPS * EXPERTS_PER_GROUP * SEG_ALIGN


def _dispatch_kernel(tm, seglen_ref, lstart_ref, gstart_ref, ltot_ref, plo_ref, phi_ref, nu_ref,
                     x_ref, route_ref, xbuf_ref, sorted_ref, zblk, sem, zsem):
    i = pl.program_id(0)
    nsteps = pl.num_programs(0)
    td = x_ref.shape[0]
    ne = plo_ref.shape[0]
    nblk = xbuf_ref.shape[0] // tm
    nrow = sorted_ref.shape[1]
    buf = i % 2

    lpos = route_ref[2:4, :]
    xb = x_ref[...].astype(BF16)

    def sort_rows(lo, hi):
        r = (lax.broadcasted_iota(I32, (hi - lo, td), 0) + lo).astype(F32)
        onehot = jnp.where((r == lpos[0:1]) | (r == lpos[1:2]), 1.0, 0.0).astype(BF16)
        sorted_ref[buf, lo:hi, :] = _dg(onehot, xb).astype(BF16)

    main = nrow - ne * SEG_ALIGN // 2
    sort_rows(0, main)

    @pl.when(ltot_ref[i] > main)
    def _():
        sort_rows(main, nrow)

    def wait_tile(step, b):
        _for_each_piece(ltot_ref[step], nrow, lambda off, size: pltpu.make_async_copy(
            sorted_ref.at[b, pl.ds(0, size), :], xbuf_ref.at[pl.ds(0, size), :], sem.at[b]).wait())

    for e in range(ne):
        idx = i * ne + e
        ls = pl.multiple_of(lstart_ref[idx], SEG_ALIGN)
        gs = pl.multiple_of(gstart_ref[idx], SEG_ALIGN)
        _for_each_piece(seglen_ref[idx], td, lambda off, size: pltpu.make_async_copy(
            sorted_ref.at[buf, pl.ds(ls + off, size), :], xbuf_ref.at[pl.ds(gs + off, size), :],
            sem.at[buf]).start())

    @pl.when(i > 0)
    def _():
        wait_tile(i - 1, 1 - buf)

    @pl.when(i == nsteps - 1)
    def _():
        wait_tile(i, buf)

    def pad_fill(fn):
        for e in range(ne):
            lo = pl.multiple_of(plo_ref[e], SEG_ALIGN)
            _for_each_piece(phi_ref[e] - lo, tm // 2, lambda off, size: fn(pltpu.make_async_copy(
                zblk.at[pl.ds(0, size), :], xbuf_ref.at[pl.ds(lo + off, size), :], zsem)))

        def per_blk(b, carry):
            fn(pltpu.make_async_copy(zblk, xbuf_ref.at[pl.ds(pl.multiple_of(b * tm, tm), tm), :], zsem))
            return carry
        lax.fori_loop(nu_ref[0], nblk, per_blk, 0)

    @pl.when(i == 0)
    def _():
        zblk[...] = jnp.zeros_like(zblk)
        pad_fill(lambda cp: cp.start())
        pad_fill(lambda cp: cp.wait())


def _dispatch(seglen, lstart, gstart, ltot, pad_lo, pad_hi, n_used, x1, route, rows, tm):
    n, d = x1.shape
    td = TD_DISPATCH
    return pl.pallas_call(
        functools.partial(_dispatch_kernel, tm),
        grid_spec=pltpu.PrefetchScalarGridSpec(
            num_scalar_prefetch=7,
            grid=(n // td,),
            in_specs=[pl.BlockSpec((td, d), lambda i, *_: (i, 0)), pl.BlockSpec((8, td), lambda i, *_: (0, i))],
            out_specs=pl.BlockSpec(memory_space=pl.ANY),
            scratch_shapes=[pltpu.VMEM((2, _sorted_rows(td), d), BF16), pltpu.VMEM((tm, d), BF16),
                            pltpu.SemaphoreType.DMA((2,)), pltpu.SemaphoreType.DMA(())],
        ),
        out_shape=jax.ShapeDtypeStruct((rows, d), BF16),
        compiler_params=_params(("arbitrary",)),
    )(seglen, lstart, gstart, ltot, pad_lo, pad_hi, n_used, x1, route)


X_SLOTS = 3


def _expert_kernel(tm, bstart_ref, nb_ref, nu_ref, w1_ref, w3_ref, w2_ref, xbuf_ref, ybuf_ref,
                   wf1, wf3, wf2, w13b, w2b, xb, yb, semw, semx, semy):
    e = pl.program_id(0)
    ne = pl.num_programs(0)
    nb = nb_ref[e]
    b0 = bstart_ref[e]
    nblk = ybuf_ref.shape[0] // tm
    rows = lambda blk: pl.ds(pl.multiple_of(blk * tm, tm), tm)
    x_copy = lambda j, slot: pltpu.make_async_copy(xbuf_ref.at[rows(b0 + j), :], xb.at[slot], semx.at[slot])
    y_copy = lambda blk, slot: pltpu.make_async_copy(yb.at[slot], ybuf_ref.at[rows(blk), :], semy.at[slot])

    def w_copies(ex, slot):
        return [pltpu.make_async_copy(src.at[ex], dst.at[slot], semw.at[slot])
                for src, dst in ((w1_ref, wf1), (w3_ref, wf3), (w2_ref, wf2))]

    ws = e % 2

    @pl.when(e == 0)
    def _():
        for cp in w_copies(0, 0):
            cp.start()

    for j0 in range(X_SLOTS - 1):
        @pl.when(j0 < nb)
        def _(j0=j0):
            x_copy(j0, j0).start()

    for cp in w_copies(e, ws):
        cp.wait()
    de = w2b.shape[0]
    w13b[:, :de] = wf1[ws].astype(BF16)
    w13b[:, de:] = wf3[ws].astype(BF16)
    w2b[...] = wf2[ws].astype(BF16)

    @pl.when(e + 1 < ne)
    def _():
        for cp in w_copies(e + 1, 1 - ws):
            cp.start()

    def body(j, carry):
        slot = j % X_SLOTS
        yslot = j % 2
        x_copy(j, slot).wait()

        @pl.when(j + X_SLOTS - 1 < nb)
        def _():
            x_copy(j + X_SLOTS - 1, (j + X_SLOTS - 1) % X_SLOTS).start()

        @pl.when(j >= 2)
        def _():
            y_copy(b0 + j - 2, yslot).wait()

        x = xb[slot]
        h13 = _dg(x, w13b[...])
        h1 = h13[:, :de]
        h = (h1 * _sigmoid(h1)) * h13[:, de:]
        yb[yslot] = _dot(h, w2b[...]).astype(BF16)
        y_copy(b0 + j, yslot).start()
        return carry

    lax.fori_loop(0, nb, body, 0)

    @pl.when(nb >= 2)
    def _():
        y_copy(b0 + nb - 2, nb % 2).wait()

    @pl.when(nb >= 1)
    def _():
        y_copy(b0 + nb - 1, (nb - 1) % 2).wait()

    @pl.when(e == pl.num_programs(0) - 1)
    def _():
        yb[0] = jnp.zeros(yb.shape[1:], yb.dtype)

        def fill(fn):
            def per_blk(blk, carry):
                fn(y_copy(blk, 0))
                return carry
            lax.fori_loop(nu_ref[0], nblk, per_blk, 0)

        fill(lambda cp: cp.start())
        fill(lambda cp: cp.wait())


def _experts(bstart, nb, n_used, xbuf, w1, w3, w2):
    rows, d = xbuf.shape
    ne, _, de = w1.shape
    tm = TM_EXPERT
    return pl.pallas_call(
        functools.partial(_expert_kernel, tm),
        grid_spec=pltpu.PrefetchScalarGridSpec(
            num_scalar_prefetch=3,
            grid=(ne,),
            in_specs=[pl.BlockSpec(memory_space=pl.ANY)] * 4,
            out_specs=pl.BlockSpec(memory_space=pl.ANY),
            scratch_shapes=[pltpu.VMEM((2, d, de), F32), pltpu.VMEM((2, d, de), F32), pltpu.VMEM((2, de, d), F32),
                            pltpu.VMEM((d, 2 * de), BF16), pltpu.VMEM((de, d), BF16),
                            pltpu.VMEM((X_SLOTS, tm, d), BF16), pltpu.VMEM((2, tm, d), BF16),
                            pltpu.SemaphoreType.DMA((2,)), pltpu.SemaphoreType.DMA((X_SLOTS,)),
                            pltpu.SemaphoreType.DMA((2,))],
        ),
        out_shape=jax.ShapeDtypeStruct((rows, d), BF16),
        compiler_params=_params(("arbitrary",)),
    )(bstart, nb, n_used, w1, w3, w2, xbuf)


def _combine_kernel(alpha, seglen_ref, lstart_ref, gstart_ref, ltot_ref, x1_ref, rt_ref, p_ref, wpe_ref, wpg_ref,
                    g_ref, b_ref, ybuf_ref, out_ref, sorted_ref, sem):
    i = pl.program_id(0)
    nsteps = pl.num_programs(0)
    tc = x1_ref.shape[0]
    ne = N_GROUPS * EXPERTS_PER_GROUP
    nrow = sorted_ref.shape[1]

    def fetch(step, buf):
        for e in range(ne):
            idx = step * ne + e
            ls = pl.multiple_of(lstart_ref[idx], SEG_ALIGN)
            gs = pl.multiple_of(gstart_ref[idx], SEG_ALIGN)
            _for_each_piece(seglen_ref[idx], tc, lambda off, size: pltpu.make_async_copy(
                ybuf_ref.at[pl.ds(gs + off, size), :], sorted_ref.at[buf, pl.ds(ls + off, size), :],
                sem.at[buf]).start())

    @pl.when(i == 0)
    def _():
        sorted_ref[...] = jnp.zeros_like(sorted_ref)
        fetch(0, 0)

    @pl.when(i + 1 < nsteps)
    def _():
        fetch(i + 1, (i + 1) % 2)

    cur = i % 2
    _for_each_piece(ltot_ref[i], nrow, lambda off, size: pltpu.make_async_copy(
        ybuf_ref.at[pl.ds(0, size), :], sorted_ref.at[cur, pl.ds(0, size), :], sem.at[cur]).wait())

    rt = rt_ref[...]
    r = lax.broadcasted_iota(I32, (tc, nrow), 1).astype(F32)
    unsort = jnp.where(r == rt[:, 0:1], rt[:, 2:3], 0.0) + jnp.where(r == rt[:, 1:2], rt[:, 3:4], 0.0)
    ffn = _dg(unsort.astype(BF16), sorted_ref[cur])
    x2 = _layer_norm(alpha * x1_ref[...] + ffn, g_ref[...], b_ref[...])
    gate = _sigmoid(_dot(x2, wpg_ref[...]))
    out_ref[...] = x2 + gate * _dot(p_ref[...], wpe_ref[...])


def _combine(seglen, lstart, gstart, ltot, x1, route_t, pf, wpe, wpg, g, bta, ybuf, alpha):
    n, d = x1.shape
    tc = TD_DISPATCH
    tile = lambda a_: pl.BlockSpec((tc, a_.shape[1]), lambda i, *_: (i, 0))
    const = lambda a_: pl.BlockSpec(a_.shape, lambda i, *_: (0, 0))
    return pl.pallas_call(
        functools.partial(_combine_kernel, alpha),
        grid_spec=pltpu.PrefetchScalarGridSpec(
            num_scalar_prefetch=4,
            grid=(n // tc,),
            in_specs=[tile(x1), tile(route_t), tile(pf), const(wpe), const(wpg), const(g), const(bta),
                      pl.BlockSpec(memory_space=pl.ANY)],
            out_specs=pl.BlockSpec((tc, d), lambda i, *_: (i, 0)),
            scratch_shapes=[pltpu.VMEM((2, _sorted_rows(tc), d), BF16), pltpu.SemaphoreType.DMA((2,))],
        ),
        out_shape=jax.ShapeDtypeStruct((n, d), F32),
        compiler_params=_params(("arbitrary",)),
    )(seglen, lstart, gstart, ltot, x1, route_t, pf, wpe, wpg, g, bta, ybuf)


def _layer(x, p_i, w_in, rw_mu, rw_w0, rw_w_up, rw_a0, rw_a_up, rw_g_up, rw_k_k, rw_k_a, rw_r_k, rw_gn_w,
           rw_gn_b, w_a_out, hg_lb_logits, hg_norm_w, w_b_out, w_o, ln1_g, ln1_b, router_g_w, router_g_b,
           router_e_w, router_e_b, w1, w3, w2, ln2_g, ln2_b, w_pe, w_pg, alpha, layer):
    b, t, d = x.shape
    n = b * t
    rw_dim = rw_w0.shape[0]
    rw_cols = 3 * rw_dim + RW_DECAY_LORA + RW_A_LORA + RW_GATE_LORA
    hg_cols = 4 * hg_norm_w.shape[0]
    ne = N_GROUPS * EXPERTS_PER_GROUP
    row2 = lambda a_: a_.reshape(1, -1)
    xf = x.reshape(n, d)

    wb = w_in.astype(BF16)
    proj_rw, proj_hg, proj_gt = _project(xf, wb[:, :rw_cols], wb[:, rw_cols:rw_cols + hg_cols],
                                         wb[:, rw_cols + hg_cols:])
    ya = _rwkv_branch(proj_rw.reshape(b, t, rw_cols), rw_mu, row2(rw_w0), rw_w_up, rw_a0, rw_a_up, rw_g_up,
                      rw_k_k, rw_k_a, rw_r_k, rw_gn_w, rw_gn_b)
    yb = _hgrn_branch(proj_hg.reshape(b, t, hg_cols), hg_lb_logits, hg_norm_w, layer)

    wr = jnp.zeros((LANES, d), F32)
    wr = wr.at[:N_GROUPS].set(router_g_w.T).at[ROUTER_EXPERT_ROW:ROUTER_EXPERT_ROW + ne].set(router_e_w.T)
    bias = jnp.zeros((LANES,), F32)
    bias = bias.at[:N_GROUPS].set(router_g_b).at[ROUTER_EXPERT_ROW:ROUTER_EXPERT_ROW + ne].set(router_e_b)
    x1, route, seg, cnt = _merge(xf, ya.reshape(n, -1), yb.reshape(n, -1), proj_gt, w_a_out.astype(BF16),
                                 w_b_out.astype(BF16), w_o.astype(BF16), row2(ln1_g), row2(ln1_b), wr,
                                 bias.reshape(LANES, 1), alpha)

    tm = TM_EXPERT
    ntile = n // TD_DISPATCH
    nblk = -(-(2 * n + (SEG_ALIGN - 1) * ne * ntile) // tm) + ne
    assert TM_MERGE == TD_DISPATCH
    segtab, tab = _finalize(seg, cnt, tm)
    per_seg = lambda row: segtab[:, row, :ne].reshape(-1)
    seglen, lstart, gstart, ltot = per_seg(0), per_seg(1), per_seg(2), segtab[:, 3, 0]
    n_used = tab[1, :1]

    xbuf = _dispatch(seglen, lstart, gstart, ltot, tab[2, :ne], tab[3, :ne], n_used, x1, route, nblk * tm, tm)
    ybuf = _experts(tab[0, :ne], tab[4, :ne], n_used, xbuf, w1, w3, w2)
    out = _combine(seglen, lstart, gstart, ltot, x1, route[2:6].T, p_i.reshape(n, -1), w_pe.astype(BF16),
                   w_pg.astype(BF16), row2(ln2_g), row2(ln2_b), ybuf, alpha)
    return out.reshape(b, t, d)


def kernel(x, p, w_in, rw_mu, rw_w0, rw_w_up, rw_a0, rw_a_up, rw_g_up, rw_k_k, rw_k_a, rw_r_k, rw_gn_w, rw_gn_b,
           w_a_out, hg_lb_logits, hg_norm_w, w_b_out, w_o, ln1_g, ln1_b, router_g_w, router_g_b, router_e_w,
           router_e_b, w1, w3, w2, ln2_g, ln2_b, w_pe, w_pg):
    depth = w_in.shape[0]
    alpha = (2 * depth) ** 0.25
    for i in range(depth):
        x = _layer(x, p[i], w_in[i], rw_mu[i], rw_w0[i], rw_w_up[i], rw_a0[i], rw_a_up[i], rw_g_up[i], rw_k_k[i],
                   rw_k_a[i], rw_r_k[i].reshape(-1), rw_gn_w[i], rw_gn_b[i], w_a_out[i], hg_lb_logits,
                   hg_norm_w[i], w_b_out[i], w_o[i], ln1_g[i], ln1_b[i],
                   router_g_w[i], router_g_b[i], router_e_w[i], router_e_b[i], w1[i], w3[i], w2[i], ln2_g[i],
                   ln2_b[i], w_pe[i], w_pg[i], alpha, i)
    return x
```

```python
import functools

import jax
import jax.numpy as jnp
from jax import lax
from jax.experimental import pallas as pl
from jax.experimental.pallas import tpu as pltpu

F32 = jnp.float32
BF16 = jnp.bfloat16
I32 = jnp.int32

NN = (((1,), (0,)), ((), ()))
NT = (((1,), (1,)), ((), ()))

RW_HEAD = 64
RW_DECAY_LORA = 64
RW_A_LORA = 64
RW_GATE_LORA = 128
RW_GN_EPS = 64e-5
RW_DECAY_SCALE = 0.6065306597126334
N_GROUPS = 4
EXPERTS_PER_GROUP = 8
LN_EPS = 1e-5
RMS_EPS = 1e-6

CHUNK = 64
SUB = 16
LANES = 128
VMEM_LIMIT = 56 * 1024 * 1024

TM_PROJ = 512
TB_RWKV = 512
TB_HGRN = 256
TM_MERGE = 512
MERGE_ROWS = 256
TD_DISPATCH = 512
SEG_ALIGN = 16
TM_EXPERT = 256
RW_WIDE = 4
HG_WIDE = 4


def _dg(a, b, dn=NN):
    return lax.dot_general(a, b, dn, preferred_element_type=F32)


def _dot(a, b, dn=NN):
    return _dg(a.astype(BF16), b.astype(BF16), dn)


def _split(a):
    hi = a.astype(BF16)
    lo = (a - hi.astype(F32)).astype(BF16)
    return hi, lo


def _dot_hl(a, b_exact, dn=NN):
    hi, lo = _split(a)
    return _dg(hi, b_exact, dn) + _dg(lo, b_exact, dn)


def _dot3(a, b, dn=NN):
    ah, al = _split(a)
    bh, bl = _split(b)
    return _dg(ah, bh, dn) + (_dg(ah, bl, dn) + _dg(al, bh, dn))


def _cumsum_chunks(x, tri):
    h = x.astype(BF16)
    r1 = x - h.astype(F32)
    m = r1.astype(BF16)
    l = (r1 - m.astype(F32)).astype(BF16)
    return _dg(tri, h) + (_dg(tri, m) + _dg(tri, l))


def _sigmoid(x):
    return 0.5 * jnp.tanh(0.5 * x) + 0.5


def _layer_norm(h, g, b):
    mu = jnp.mean(h, axis=-1, keepdims=True)
    d = h - mu
    var = jnp.mean(d * d, axis=-1, keepdims=True)
    return d * lax.rsqrt(var + LN_EPS) * g + b


def _params(sem):
    return pltpu.CompilerParams(dimension_semantics=sem, vmem_limit_bytes=VMEM_LIMIT)


def _proj_kernel(x_ref, wr_ref, wh_ref, wg_ref, pr_ref, ph_ref, pg_ref):
    xb = x_ref[...].astype(BF16)
    pr_ref[...] = _dg(xb, wr_ref[...])
    ph_ref[...] = _dg(xb, wh_ref[...])
    pg_ref[...] = _dg(xb, wg_ref[...])


def _project(xf, w_rw, w_hg, w_gt):
    n, d = xf.shape
    tm = TM_PROJ
    full = lambda w: pl.BlockSpec(w.shape, lambda i: (0, 0))
    tile = lambda c: pl.BlockSpec((tm, c), lambda i: (i, 0))
    return pl.pallas_call(
        _proj_kernel,
        grid=(n // tm,),
        in_specs=[tile(d), full(w_rw), full(w_hg), full(w_gt)],
        out_specs=[tile(w_rw.shape[1]), tile(w_hg.shape[1]), tile(w_gt.shape[1])],
        out_shape=[jax.ShapeDtypeStruct((n, w.shape[1]), F32) for w in (w_rw, w_hg, w_gt)],
        compiler_params=_params(("parallel",)),
    )(xf, w_rw, w_hg, w_gt)


def _each(f, *ls):
    return [f(*xs) for xs in zip(*ls)]


def _two(x):
    m1 = lax.broadcasted_iota(I32, x.shape, 1) < RW_HEAD
    return jnp.concatenate([jnp.where(m1, x, 0.0), jnp.where(m1, 0.0, x)], axis=0)


def _rwkv_chunk_prepare(ins, lvl_ref, out):
    c = CHUNK
    lane = lax.broadcasted_iota(I32, (c, LANES), 1)
    row = lax.broadcasted_iota(I32, (c, LANES), 0)
    scol = jnp.bitwise_and(lane, RW_HEAD - 1)
    strict = row > scol
    incl = row >= scol
    r2 = lax.broadcasted_iota(I32, (LANES, LANES), 0)
    c2 = lax.broadcasted_iota(I32, (LANES, LANES), 1)
    eye = jnp.where(r2 == c2, 1.0, 0.0).astype(F32)

    def prep(r, k, v, av, bv, lw, lc):
        l_end = lc[c - 1:c]
        e_r = jnp.exp(l_end - lc)
        e_n = jnp.exp(-lc)
        return dict(at=av * jnp.exp(lc - lw), rt=r * jnp.exp(lc), bt=bv * e_n, kt=k * e_n,
                    bk=jnp.concatenate([bv * e_r, k * e_r], axis=0), pc=jnp.exp(l_end), v=v)

    q = [prep(*xs) for xs in ins]
    yield
    p = [_dot(jnp.concatenate([d["at"], d["rt"]], axis=0),
              jnp.concatenate([_two(d["bt"]), _two(d["kt"])], axis=0), NT) for d in q]
    sab = [jnp.where(strict, x[:c, :LANES], 0.0) for x in p]
    sak = [jnp.where(strict, x[:c, LANES:], 0.0) for x in p]
    srb = [jnp.where(incl, x[c:, :LANES], 0.0) for x in p]
    srk = [jnp.where(incl, x[c:, LANES:], 0.0) for x in p]
    yield
    sv = _each(lambda ak, rk, d: _dot(jnp.concatenate([ak, rk], axis=0), _two(d["v"])), sak, srk, q)
    yield

    a_bd = [_two(x) for x in sab]
    t = [eye + a * lvl_ref[0] for a in a_bd]
    for lv in range(1, lvl_ref.shape[0]):
        ta = _each(lambda t_, a: _dot(t_, a * lvl_ref[lv]), t, a_bd)
        yield
        t = _each(lambda t_, ta_: t_ + _dot(ta_, t_), t, ta)
        yield

    x = _each(lambda t_, d, sv_: _dot(t_, jnp.concatenate([_two(d["at"]), _two(sv_[:c])], axis=1)), t, q, sv)
    out.extend(dict(uk=x_[:c, :LANES] + x_[c:, :LANES],
                    w=x_[:c, LANES:] + x_[c:, LANES:],
                    rt=d["rt"], rkv=sv_[c:], srb=srb_, bk=d["bk"], v=d["v"], pc=d["pc"])
               for x_, d, sv_, srb_ in zip(x, q, sv, srb))


def _rwkv_chunk_apply(prep, states, bd, out):
    c = CHUNK
    g1 = _each(lambda d, s: _dot(jnp.concatenate([d["uk"], d["rt"]], axis=0), s, NT), prep, states)
    yield
    u = _each(lambda g, d: g[:c] + d["w"], g1, prep)
    y = _each(lambda g, d, u_: g[c:] + d["rkv"] + _dot(d["srb"], _two(u_)), g1, prep, u)
    yield
    upd = _each(lambda u_, d: _dot(jnp.concatenate([u_, d["v"]], axis=0).T, d["bk"]), u, prep)
    s_new = _each(lambda d, s, up: s * d["pc"] + bd * up, prep, states, upd)
    out.extend(zip(y, s_new))
    yield


def _skewed(gens):
    live = set(range(len(gens)))
    tick = 0
    while live:
        for q in sorted(live):
            if q <= tick:
                try:
                    next(gens[q])
                except StopIteration:
                    live.discard(q)
        tick += 1


def _interleave(*gens):
    live = [g for g in gens if g is not None]
    while live:
        for g in list(live):
            try:
                next(g)
            except StopIteration:
                live.remove(g)


def _rwkv_kernel(u_ref, mu_ref, w0_ref, wup_ref, a0_ref, aup_ref, gup_ref, kk_ref, ka_ref, rk_ref,
                 gnw_ref, gnb_ref, tri_ref, gsum_ref, lvl_ref, bd_ref, ya_ref,
                 s_ref, prev_ref, r_s, k_s, v_s, a_s, b_s, lw_s, lc_s, g_s, y_s):
    tb = pl.program_id(1)

    @pl.when(tb == 0)
    def _():
        s_ref[...] = jnp.zeros_like(s_ref)
        prev_ref[...] = jnp.zeros_like(prev_ref)

    nt = u_ref.shape[1]
    dim = r_s.shape[1]
    part = CHUNK * RW_WIDE
    npair = dim // LANES
    lanes = [slice(p * LANES, (p + 1) * LANES) for p in range(npair)]
    bd = bd_ref[...]
    gsum = gsum_ref[...]

    def prologue(h):
        rows = slice(h * part, (h + 1) * part)
        u = u_ref[0, rows, :]
        before = prev_ref[...] if h == 0 else u_ref[0, h * part - 1:h * part, :]
        rowid = lax.broadcasted_iota(I32, u.shape, 0)
        shifted = jnp.where(rowid == 0, before, pltpu.roll(u, 1, axis=0))
        um = u + (shifted - u) * mu_ref[...]
        r = um[:, 0:dim]
        k = um[:, dim:2 * dim]
        v = um[:, 2 * dim:3 * dim]
        xwa = um[:, 3 * dim:3 * dim + LANES]
        xg = um[:, 3 * dim + LANES:3 * dim + 2 * LANES]
        yield
        wpre = w0_ref[...] + _dot(jnp.tanh(xwa), wup_ref[...])
        lw = -RW_DECAY_SCALE * _sigmoid(wpre)
        a = _sigmoid(a0_ref[...] + _dot(xwa, aup_ref[...]))
        g_s[rows, :] = _dot(_sigmoid(xg), gup_ref[...])
        yield
        kk = k * kk_ref[...]
        ss = _dot(kk * kk, gsum)
        kk = kk * lax.rsqrt(jnp.maximum(ss, 1e-24))
        r_s[rows, :] = r
        k_s[rows, :] = k * (1.0 + (a - 1.0) * ka_ref[...])
        v_s[rows, :] = v
        a_s[rows, :] = -kk
        b_s[rows, :] = kk * a
        yield
        lw_s[rows, :] = lw
        lc_s[rows, :] = _cumsum_chunks(lw, tri_ref[...])

    def chunk_rows(h):
        return [slice(h * part + ci * CHUNK, h * part + (ci + 1) * CHUNK) for ci in range(RW_WIDE)]

    def prepare(h, out):
        ins = [(r_s[rw, ls], k_s[rw, ls], v_s[rw, ls], a_s[rw, ls], b_s[rw, ls], lw_s[rw, ls], lc_s[rw, ls])
               for rw in chunk_rows(h) for ls in lanes]
        yield from _rwkv_chunk_prepare(ins, lvl_ref, out)

    def apply(h, prep):
        states = [s_ref[p] for p in range(npair)]
        for ci, rw in enumerate(chunk_rows(h)):
            outs = []
            yield from _rwkv_chunk_apply(prep[ci * npair:(ci + 1) * npair], states, bd, outs)
            states = [s_new for _, s_new in outs]
            for (y, _), ls in zip(outs, lanes):
                y_s[rw, ls] = y
        for p in range(npair):
            s_ref[p] = states[p]

    def epilogue(h):
        rows = slice(h * part, (h + 1) * part)
        y = y_s[rows, :]
        inv_n = 1.0 / RW_HEAD
        m = _dot_hl(y, gsum) * inv_n
        d = y - m
        yield
        var = _dot(d * d, gsum) * inv_n
        yn = d * lax.rsqrt(var + RW_GN_EPS) * gnw_ref[...] + gnb_ref[...]
        yield
        bonus = _dot(r_s[rows, :] * k_s[rows, :] * rk_ref[...], gsum) * v_s[rows, :]
        ya_ref[0, rows, :] = ((yn + bonus) * g_s[rows, :]).astype(BF16)

    nparts = nt // part
    preps = [[] for _ in range(nparts)]
    _interleave(prologue(0))
    prev_ref[...] = u_ref[0, nt - 1:nt, :]
    _interleave(prepare(0, preps[0]), prologue(1) if nparts > 1 else None)
    for h in range(nparts):
        _interleave(apply(h, preps[h]),
                    prepare(h + 1, preps[h + 1]) if h + 1 < nparts else None,
                    prologue(h + 2) if h + 2 < nparts else None,
                    epilogue(h - 1) if h > 0 else None)
    _interleave(epilogue(nparts - 1))


def _rwkv_branch(proj_rw, mu, w0, wup, a0, aup, gup, k_k, k_a, r_k, gn_w, gn_b):
    b, t, cols = proj_rw.shape
    dim = w0.shape[1]
    tb = TB_RWKV
    ii = jnp.arange(CHUNK * RW_WIDE)
    tri = ((ii[:, None] // CHUNK == ii[None, :] // CHUNK) & (ii[:, None] >= ii[None, :])).astype(BF16)
    jj = jnp.arange(dim)
    gsum = (jj[:, None] // RW_HEAD == jj[None, :] // RW_HEAD).astype(BF16)
    rr = jnp.arange(LANES)[:, None]
    cc = jnp.arange(LANES)[None, :]
    lvls = []
    s = 1
    while s < CHUNK:
        lvls.append(((rr // (2 * s) == cc // (2 * s)) & ((rr // s) % 2 == 1) & ((cc // s) % 2 == 0)).astype(F32))
        s *= 2
    lvl = jnp.stack(lvls)
    bd = (rr // RW_HEAD == cc // RW_HEAD).astype(F32)
    zpad = lambda rows: jnp.zeros((rows, dim), F32)
    wup_p = jnp.concatenate([wup, zpad(LANES - wup.shape[0])], axis=0).astype(BF16)
    aup_p = jnp.concatenate([zpad(LANES - aup.shape[0]), aup], axis=0).astype(BF16)
    row2 = lambda a_: a_.reshape(1, -1)
    const = lambda a_: pl.BlockSpec(a_.shape, lambda bi, ti: (0,) * a_.ndim)
    args = [row2(mu), row2(w0), wup_p, row2(a0), aup_p, gup.astype(BF16), row2(k_k), row2(k_a), row2(r_k),
            row2(gn_w), row2(gn_b), tri, gsum, lvl, bd]
    sc = lambda: pltpu.VMEM((tb, dim), F32)
    return pl.pallas_call(
        _rwkv_kernel,
        grid=(b, t // tb),
        in_specs=[pl.BlockSpec((1, tb, cols), lambda bi, ti: (bi, ti, 0))] + [const(a_) for a_ in args],
        out_specs=pl.BlockSpec((1, tb, dim), lambda bi, ti: (bi, ti, 0)),
        out_shape=jax.ShapeDtypeStruct((b, t, dim), BF16),
        scratch_shapes=[pltpu.VMEM((dim // LANES, LANES, LANES), F32), pltpu.VMEM((1, cols), F32)]
                       + [sc() for _ in range(9)],
        compiler_params=_params(("arbitrary", "arbitrary")),
    )(proj_rw, *args)


def _hgrn_chunk_prepare(ins):
    c = CHUNK
    subs = [(SUB * i, SUB * (i + 1)) for i in range(c // SUB)]

    def scores(q, k, lf, bc, lo, hi):
        m = bc[lo:lo + 1] - lf[lo:lo + 1]
        att = _dot(q[lo:hi] * jnp.exp(bc[lo:hi] - m), k[:hi] * jnp.exp(m - bc[:hi]), NT)
        tt = lax.broadcasted_iota(I32, (SUB, hi), 0) + lo
        s_ = lax.broadcasted_iota(I32, (SUB, hi), 1)
        return jnp.where(s_ <= tt, att, 0.0)

    att = [[scores(q, k, lf, bc, lo, hi) for lo, hi in subs] for q, k, v, lf, bc in ins]
    upd = [_dot(v.T, k * jnp.exp(bc[c - 1:c] - bc)) for q, k, v, lf, bc in ins]
    intra = [[_dot(a, x[2][:hi]) for a, (lo, hi) in zip(arow, subs)] for arow, x in zip(att, ins)]
    return [dict(intra=jnp.concatenate(rows, axis=0), qe=x[0] * jnp.exp(x[4]), upd=up, pc=jnp.exp(x[4][c - 1:c]))
            for rows, x, up in zip(intra, ins, upd)]


def _hgrn_chunk_apply(prep, states):
    o = _each(lambda d, st: d["intra"] + _dot(d["qe"], st, NT), prep, states)
    st_new = _each(lambda d, st: st * d["pc"] + d["upd"], prep, states)
    return list(zip(o, st_new))


def _hgrn_kernel(layer, u_ref, lbl_ref, nw_ref, tri_ref, gsum_ref, yb_ref,
                 st_ref, q_s, k_s, v_s, lf_s, bc_s, o_s):
    tb = pl.program_id(1)

    @pl.when(tb == 0)
    def _():
        st_ref[...] = jnp.zeros_like(st_ref)

    u = u_ref[0]
    nt = u.shape[0]
    dim = q_s.shape[1]
    lbl = lbl_ref[...]
    e = jnp.exp(lbl - jnp.max(lbl, axis=0, keepdims=True))
    lb = jnp.sum(e[0:layer + 1], axis=0, keepdims=True) / jnp.sum(e, axis=0, keepdims=True)
    zf = u[:, dim:2 * dim]
    sig = _sigmoid(zf)
    f = lb + (1.0 - lb) * sig
    qin = u[:, 0:dim]
    q_s[...] = qin * _sigmoid(qin)
    k_s[...] = (1.0 - lb) * (1.0 - sig)
    v_s[...] = u[:, 2 * dim:3 * dim]
    lf = jnp.log(f)
    lf_s[...] = lf
    bc_s[...] = _cumsum_chunks(lf, tri_ref[...])

    nhead = dim // LANES
    lanes = [slice(h * LANES, (h + 1) * LANES) for h in range(nhead)]

    def group_body(gi, carry):
        rows = [pl.ds(pl.multiple_of((gi * HG_WIDE + ci) * CHUNK, CHUNK), CHUNK) for ci in range(HG_WIDE)]
        ins = [(q_s[rw, ls], k_s[rw, ls], v_s[rw, ls], lf_s[rw, ls], bc_s[rw, ls]) for rw in rows for ls in lanes]
        prep = _hgrn_chunk_prepare(ins)
        states = [st_ref[h] for h in range(nhead)]
        for ci, rw in enumerate(rows):
            outs = _hgrn_chunk_apply(prep[ci * nhead:(ci + 1) * nhead], states)
            states = [st_new for _, st_new in outs]
            for (o, _), ls in zip(outs, lanes):
                o_s[rw, ls] = o
        for h in range(nhead):
            st_ref[h] = states[h]
        return carry

    lax.fori_loop(0, nt // (CHUNK * HG_WIDE), group_body, 0)

    o = o_s[...]
    ms = _dot(o * o, gsum_ref[...]) * (1.0 / LANES)
    og = u[:, 3 * dim:4 * dim]
    yb_ref[0] = (o * lax.rsqrt(ms + RMS_EPS) * nw_ref[...] * _sigmoid(og)).astype(BF16)


def _hgrn_branch(proj_hg, lb_logits, norm_w, layer):
    b, t, cols = proj_hg.shape
    dim = cols // 4
    tb = TB_HGRN
    ii = jnp.arange(tb)
    tri = ((ii[:, None] // CHUNK == ii[None, :] // CHUNK) & (ii[:, None] >= ii[None, :])).astype(BF16)
    jj = jnp.arange(dim)
    gsum = (jj[:, None] // LANES == jj[None, :] // LANES).astype(BF16)
    const = lambda a_: pl.BlockSpec(a_.shape, lambda bi, ti: (0,) * a_.ndim)
    args = [lb_logits, norm_w.reshape(1, -1), tri, gsum]
    sc = lambda: pltpu.VMEM((tb, dim), F32)
    return pl.pallas_call(
        functools.partial(_hgrn_kernel, layer),
        grid=(b, t // tb),
        in_specs=[pl.BlockSpec((1, tb, cols), lambda bi, ti: (bi, ti, 0))] + [const(a_) for a_ in args],
        out_specs=pl.BlockSpec((1, tb, dim), lambda bi, ti: (bi, ti, 0)),
        out_shape=jax.ShapeDtypeStruct((b, t, dim), BF16),
        scratch_shapes=[pltpu.VMEM((dim // LANES, LANES, LANES), F32)] + [sc() for _ in range(6)],
        compiler_params=_params(("arbitrary", "arbitrary")),
    )(proj_hg, *args)


def _merge_kernel(alpha, x_ref, ya_ref, yb_ref, pg_ref, wa_ref, wb_ref, wo_ref, g_ref, b_ref, wr_ref,
                  bias_ref, upper_ref, lower_ref, x1_ref, route_ref, seg_ref, cnt_ref, lt_ref, carry_ref):
    tm, d = x_ref.shape

    def rows_stage(rs):
        ma = _dg(ya_ref[rs, :], wa_ref[...])
        mb = _dg(yb_ref[rs, :], wb_ref[...])
        yield
        merged = _sigmoid(pg_ref[rs, :d]) * ma + _sigmoid(pg_ref[rs, d:]) * mb
        yield
        h = alpha * x_ref[rs, :] + _dot(merged, wo_ref[...])
        yield
        x1 = _layer_norm(h, g_ref[...], b_ref[...])
        x1_ref[rs, :] = x1
        yield
        lt_ref[:, rs] = _dot3(wr_ref[...], x1, NT)
        yield

    groups = [rows_stage(slice(q * MERGE_ROWS, (q + 1) * MERGE_ROWS)) for q in range(tm // MERGE_ROWS)]
    _skewed(groups)
    _route_tile(lt_ref, bias_ref, upper_ref, lower_ref, route_ref, seg_ref, cnt_ref, carry_ref)


def _merge(xf, ya, yb, pgate, wa, wb, wo, g, bta, wr, bias_col, alpha):
    n, d = xf.shape
    tm = TM_MERGE
    ne = N_GROUPS * EXPERTS_PER_GROUP
    ii = jnp.arange(tm)
    upper = (ii[:, None] < ii[None, :]).astype(BF16)
    ee = jnp.arange(ne)
    lower = (ee[:, None] > ee[None, :]).astype(BF16)
    tile = lambda a_: pl.BlockSpec((tm, a_.shape[1]), lambda i: (i, 0))
    const = lambda a_: pl.BlockSpec(a_.shape, lambda i: (0, 0))
    return pl.pallas_call(
        functools.partial(_merge_kernel, alpha),
        grid=(n // tm,),
        in_specs=[tile(xf), tile(ya), tile(yb), tile(pgate), const(wa), const(wb), const(wo), const(g),
                  const(bta), const(wr), const(bias_col), const(upper), const(lower)],
        out_specs=[pl.BlockSpec((tm, d), lambda i: (i, 0)), pl.BlockSpec((8, tm), lambda i: (0, i)),
                   pl.BlockSpec((1, 8, LANES), lambda i: (i, 0, 0)), pl.BlockSpec((ne, LANES), lambda i: (0, 0))],
        out_shape=[jax.ShapeDtypeStruct((n, d), F32), jax.ShapeDtypeStruct((8, n), F32),
                   jax.ShapeDtypeStruct((n // tm, 8, LANES), F32), jax.ShapeDtypeStruct((ne, LANES), F32)],
        scratch_shapes=[pltpu.VMEM((LANES, tm), F32), pltpu.VMEM((ne, 1), F32)],
        compiler_params=_params(("arbitrary",)),
    )(xf, ya, yb, pgate, wa, wb, wo, g, bta, wr, bias_col, upper, lower)


ROUTER_EXPERT_ROW = 8


def _to_lanes(col, nl):
    ne = col.shape[0]
    diag = lax.broadcasted_iota(I32, (ne, nl), 0) == lax.broadcasted_iota(I32, (ne, nl), 1)
    return jnp.sum(jnp.where(diag, col, 0.0), axis=0, keepdims=True)


def _route_tile(lt_ref, bias_ref, upper_ref, lower_ref, route_ref, seg_ref, cnt_ref, carry_ref):
    @pl.when(pl.program_id(0) == 0)
    def _():
        carry_ref[...] = jnp.zeros_like(carry_ref)

    ne = N_GROUPS * EXPERTS_PER_GROUP
    lt = lt_ref[...] + bias_ref[...]
    nb = lt.shape[1]
    neg = -jnp.inf
    lg = lt[0:8]
    rg = lax.broadcasted_iota(I32, (8, nb), 0).astype(F32)
    lg = jnp.where(rg < N_GROUPS, lg, neg)
    mg = jnp.max(lg, axis=0, keepdims=True)
    gidx = jnp.min(jnp.where(lg == mg, rg, 1e9), axis=0, keepdims=True)
    pg_sel = 1.0 / jnp.sum(jnp.exp(lg - mg), axis=0, keepdims=True)

    le = lt[ROUTER_EXPERT_ROW:ROUTER_EXPERT_ROW + ne]
    re = lax.broadcasted_iota(I32, (ne, nb), 0).astype(F32)
    in_group = jnp.floor(re * (1.0 / EXPERTS_PER_GROUP)) == gidx
    l1 = jnp.where(in_group, le, neg)
    m1 = jnp.max(l1, axis=0, keepdims=True)
    i1 = jnp.min(jnp.where(l1 == m1, re, 1e9), axis=0, keepdims=True)
    l2 = jnp.where(re == i1, neg, l1)
    m2 = jnp.max(l2, axis=0, keepdims=True)
    i2 = jnp.min(jnp.where(l2 == m2, re, 1e9), axis=0, keepdims=True)
    e2 = jnp.exp(m2 - m1)
    w1 = pg_sel / (1.0 + e2)
    w2 = pg_sel * e2 / (1.0 + e2)

    sel1 = re == i1
    sel2 = re == i2
    onehot = jnp.where(sel1 | sel2, 1.0, 0.0)
    before = _dg(onehot.astype(BF16), upper_ref[...])
    cnt_t = jnp.sum(onehot, axis=1, keepdims=True)
    seg = jnp.floor((cnt_t + (SEG_ALIGN - 1)) * (1.0 / SEG_ALIGN)) * SEG_ALIGN
    lstart = _dg(lower_ref[...], jnp.broadcast_to(seg, (ne, LANES)).astype(BF16))[:, 0:1]
    tot = lstart + before
    lpos1 = jnp.sum(jnp.where(sel1, tot, 0.0), axis=0, keepdims=True)
    lpos2 = jnp.sum(jnp.where(sel2, tot, 0.0), axis=0, keepdims=True)
    grel = carry_ref[...]
    carry = grel + seg
    carry_ref[...] = carry
    cnt_ref[...] = jnp.broadcast_to(carry, cnt_ref.shape)
    zero = jnp.zeros_like(w1)
    route_ref[...] = jnp.concatenate([i1, i2, lpos1, lpos2, w1, w2, zero, zero], axis=0)
    nl = seg_ref.shape[2]
    zl = jnp.zeros((1, nl), F32)
    ltot = jnp.broadcast_to(jnp.sum(seg, axis=0, keepdims=True), (1, nl))
    seg_ref[0] = jnp.concatenate([_to_lanes(seg, nl), _to_lanes(lstart, nl), _to_lanes(grel, nl), ltot,
                                  zl, zl, zl, zl], axis=0)


TAB_LANES = LANES


def _finalize_kernel(tm, seg_ref, cnt_ref, lower_ref, segtab_ref, tab_ref):
    ne = cnt_ref.shape[0]
    cnt = cnt_ref[...]
    nb = jnp.floor((cnt + (tm - 1)) * (1.0 / tm))
    bstart = _dg(lower_ref[...], nb.astype(BF16))
    bend = bstart + nb
    pad_start = bstart[:, 0:1] * tm
    pad_start_row = _to_lanes(pad_start, seg_ref.shape[2])
    for t in range(seg_ref.shape[0]):
        seg = seg_ref[t]
        segtab_ref[t] = jnp.concatenate([seg[0:2], seg[2:3] + pad_start_row, seg[3:8]], axis=0).astype(I32)

    nl = tab_ref.shape[1]
    n_used = jnp.max(bend[:, 0:1], axis=0, keepdims=True)
    pad_lo = _to_lanes(pad_start + cnt[:, 0:1], nl)
    pad_hi = _to_lanes(bend[:, 0:1] * tm, nl)
    zero = jnp.zeros((1, nl), F32)
    tab_ref[...] = jnp.concatenate([_to_lanes(bstart[:, 0:1], nl), jnp.broadcast_to(n_used, (1, nl)), pad_lo,
                                    pad_hi, _to_lanes(nb[:, 0:1], nl), zero, zero, zero], axis=0).astype(I32)


def _finalize(seg, cnt, tm):
    ntile = seg.shape[0]
    ne = cnt.shape[0]
    ii = jnp.arange(ne)
    lower = (ii[:, None] > ii[None, :]).astype(BF16)
    return pl.pallas_call(
        functools.partial(_finalize_kernel, tm),
        grid=(1,),
        in_specs=[pl.BlockSpec(seg.shape, lambda i: (0, 0, 0)), pl.BlockSpec(cnt.shape, lambda i: (0, 0)),
                  pl.BlockSpec((ne, ne), lambda i: (0, 0))],
        out_specs=[pl.BlockSpec(seg.shape, lambda i: (0, 0, 0)), pl.BlockSpec((8, TAB_LANES), lambda i: (0, 0))],
        out_shape=[jax.ShapeDtypeStruct((ntile, 8, LANES), I32), jax.ShapeDtypeStruct((8, TAB_LANES), I32)],
        compiler_params=_params(("arbitrary",)),
    )(seg, cnt, lower)


def _for_each_piece(length, max_len, fn):
    size = SEG_ALIGN
    sizes = []
    while size <= max_len:
        sizes.append(size)
        size *= 2

    for size in reversed(sizes):
        @pl.when(jnp.bitwise_and(length, size) != 0)
        def _(size=size):
            fn(pl.multiple_of(jnp.bitwise_and(length, -2 * size), SEG_ALIGN), size)


def _sorted_rows(td):
    return 2 * td + N_GROUPS * EXPERTS_PER_GROUP * SEG_ALIGN


def _dispatch_kernel(tm, seglen_ref, lstart_ref, gstart_ref, ltot_ref, plo_ref, phi_ref, nu_ref,
                     x_ref, route_ref, xbuf_ref, sorted_ref, zblk, sem, zsem):
    i = pl.program_id(0)
    nsteps = pl.num_programs(0)
    td = x_ref.shape[0]
    ne = plo_ref.shape[0]
    nblk = xbuf_ref.shape[0] // tm
    nrow = sorted_ref.shape[1]
    buf = i % 2

    lpos = route_ref[2:4, :]
    xb = x_ref[...].astype(BF16)

    def sort_rows(lo, hi):
        r = (lax.broadcasted_iota(I32, (hi - lo, td), 0) + lo).astype(F32)
        onehot = jnp.where((r == lpos[0:1]) | (r == lpos[1:2]), 1.0, 0.0).astype(BF16)
        sorted_ref[buf, lo:hi, :] = _dg(onehot, xb).astype(BF16)

    main = nrow - ne * SEG_ALIGN // 2
    sort_rows(0, main)

    @pl.when(ltot_ref[i] > main)
    def _():
        sort_rows(main, nrow)

    def wait_tile(step, b):
        _for_each_piece(ltot_ref[step], nrow, lambda off, size: pltpu.make_async_copy(
            sorted_ref.at[b, pl.ds(0, size), :], xbuf_ref.at[pl.ds(0, size), :], sem.at[b]).wait())

    for e in range(ne):
        idx = i * ne + e
        ls = pl.multiple_of(lstart_ref[idx], SEG_ALIGN)
        gs = pl.multiple_of(gstart_ref[idx], SEG_ALIGN)
        _for_each_piece(seglen_ref[idx], td, lambda off, size: pltpu.make_async_copy(
            sorted_ref.at[buf, pl.ds(ls + off, size), :], xbuf_ref.at[pl.ds(gs + off, size), :],
            sem.at[buf]).start())

    @pl.when(i > 0)
    def _():
        wait_tile(i - 1, 1 - buf)

    @pl.when(i == nsteps - 1)
    def _():
        wait_tile(i, buf)

    def pad_fill(fn):
        for e in range(ne):
            lo = pl.multiple_of(plo_ref[e], SEG_ALIGN)
            _for_each_piece(phi_ref[e] - lo, tm // 2, lambda off, size: fn(pltpu.make_async_copy(
                zblk.at[pl.ds(0, size), :], xbuf_ref.at[pl.ds(lo + off, size), :], zsem)))

        def per_blk(b, carry):
            fn(pltpu.make_async_copy(zblk, xbuf_ref.at[pl.ds(pl.multiple_of(b * tm, tm), tm), :], zsem))
            return carry
        lax.fori_loop(nu_ref[0], nblk, per_blk, 0)

    @pl.when(i == 0)
    def _():
        zblk[...] = jnp.zeros_like(zblk)
        pad_fill(lambda cp: cp.start())
        pad_fill(lambda cp: cp.wait())


def _dispatch(seglen, lstart, gstart, ltot, pad_lo, pad_hi, n_used, x1, route, rows, tm):
    n, d = x1.shape
    td = TD_DISPATCH
    return pl.pallas_call(
        functools.partial(_dispatch_kernel, tm),
        grid_spec=pltpu.PrefetchScalarGridSpec(
            num_scalar_prefetch=7,
            grid=(n // td,),
            in_specs=[pl.BlockSpec((td, d), lambda i, *_: (i, 0)), pl.BlockSpec((8, td), lambda i, *_: (0, i))],
            out_specs=pl.BlockSpec(memory_space=pl.ANY),
            scratch_shapes=[pltpu.VMEM((2, _sorted_rows(td), d), BF16), pltpu.VMEM((tm, d), BF16),
                            pltpu.SemaphoreType.DMA((2,)), pltpu.SemaphoreType.DMA(())],
        ),
        out_shape=jax.ShapeDtypeStruct((rows, d), BF16),
        compiler_params=_params(("arbitrary",)),
    )(seglen, lstart, gstart, ltot, pad_lo, pad_hi, n_used, x1, route)


X_SLOTS = 3


def _expert_kernel(tm, bstart_ref, nb_ref, nu_ref, w1_ref, w3_ref, w2_ref, xbuf_ref, ybuf_ref,
                   wf1, wf3, wf2, w13b, w2b, xb, yb, semw, semx, semy):
    e = pl.program_id(0)
    ne = pl.num_programs(0)
    nb = nb_ref[e]
    b0 = bstart_ref[e]
    nblk = ybuf_ref.shape[0] // tm
    rows = lambda blk: pl.ds(pl.multiple_of(blk * tm, tm), tm)
    x_copy = lambda j, slot: pltpu.make_async_copy(xbuf_ref.at[rows(b0 + j), :], xb.at[slot], semx.at[slot])
    y_copy = lambda blk, slot: pltpu.make_async_copy(yb.at[slot], ybuf_ref.at[rows(blk), :], semy.at[slot])

    def w_copies(ex, slot):
        return [pltpu.make_async_copy(src.at[ex], dst.at[slot], semw.at[slot])
                for src, dst in ((w1_ref, wf1), (w3_ref, wf3), (w2_ref, wf2))]

    ws = e % 2

    @pl.when(e == 0)
    def _():
        for cp in w_copies(0, 0):
            cp.start()

    for j0 in range(X_SLOTS - 1):
        @pl.when(j0 < nb)
        def _(j0=j0):
            x_copy(j0, j0).start()

    for cp in w_copies(e, ws):
        cp.wait()
    de = w2b.shape[0]
    w13b[:, :de] = wf1[ws].astype(BF16)
    w13b[:, de:] = wf3[ws].astype(BF16)
    w2b[...] = wf2[ws].astype(BF16)

    @pl.when(e + 1 < ne)
    def _():
        for cp in w_copies(e + 1, 1 - ws):
            cp.start()

    def body(j, carry):
        slot = j % X_SLOTS
        yslot = j % 2
        x_copy(j, slot).wait()

        @pl.when(j + X_SLOTS - 1 < nb)
        def _():
            x_copy(j + X_SLOTS - 1, (j + X_SLOTS - 1) % X_SLOTS).start()

        @pl.when(j >= 2)
        def _():
            y_copy(b0 + j - 2, yslot).wait()

        x = xb[slot]
        h13 = _dg(x, w13b[...])
        h1 = h13[:, :de]
        h = (h1 * _sigmoid(h1)) * h13[:, de:]
        yb[yslot] = _dot(h, w2b[...]).astype(BF16)
        y_copy(b0 + j, yslot).start()
        return carry

    lax.fori_loop(0, nb, body, 0)

    @pl.when(nb >= 2)
    def _():
        y_copy(b0 + nb - 2, nb % 2).wait()

    @pl.when(nb >= 1)
    def _():
        y_copy(b0 + nb - 1, (nb - 1) % 2).wait()

    @pl.when(e == pl.num_programs(0) - 1)
    def _():
        yb[0] = jnp.zeros(yb.shape[1:], yb.dtype)

        def fill(fn):
            def per_blk(blk, carry):
                fn(y_copy(blk, 0))
                return carry
            lax.fori_loop(nu_ref[0], nblk, per_blk, 0)

        fill(lambda cp: cp.start())
        fill(lambda cp: cp.wait())


def _experts(bstart, nb, n_used, xbuf, w1, w3, w2):
    rows, d = xbuf.shape
    ne, _, de = w1.shape
    tm = TM_EXPERT
    return pl.pallas_call(
        functools.partial(_expert_kernel, tm),
        grid_spec=pltpu.PrefetchScalarGridSpec(
            num_scalar_prefetch=3,
            grid=(ne,),
            in_specs=[pl.BlockSpec(memory_space=pl.ANY)] * 4,
            out_specs=pl.BlockSpec(memory_space=pl.ANY),
            scratch_shapes=[pltpu.VMEM((2, d, de), F32), pltpu.VMEM((2, d, de), F32), pltpu.VMEM((2, de, d), F32),
                            pltpu.VMEM((d, 2 * de), BF16), pltpu.VMEM((de, d), BF16),
                            pltpu.VMEM((X_SLOTS, tm, d), BF16), pltpu.VMEM((2, tm, d), BF16),
                            pltpu.SemaphoreType.DMA((2,)), pltpu.SemaphoreType.DMA((X_SLOTS,)),
                            pltpu.SemaphoreType.DMA((2,))],
        ),
        out_shape=jax.ShapeDtypeStruct((rows, d), BF16),
        compiler_params=_params(("arbitrary",)),
    )(bstart, nb, n_used, w1, w3, w2, xbuf)


def _combine_kernel(alpha, seglen_ref, lstart_ref, gstart_ref, ltot_ref, x1_ref, rt_ref, p_ref, wpe_ref, wpg_ref,
                    g_ref, b_ref, ybuf_ref, out_ref, sorted_ref, sem):
    i = pl.program_id(0)
    nsteps = pl.num_programs(0)
    tc = x1_ref.shape[0]
    ne = N_GROUPS * EXPERTS_PER_GROUP
    nrow = sorted_ref.shape[1]

    def fetch(step, buf):
        for e in range(ne):
            idx = step * ne + e
            ls = pl.multiple_of(lstart_ref[idx], SEG_ALIGN)
            gs = pl.multiple_of(gstart_ref[idx], SEG_ALIGN)
            _for_each_piece(seglen_ref[idx], tc, lambda off, size: pltpu.make_async_copy(
                ybuf_ref.at[pl.ds(gs + off, size), :], sorted_ref.at[buf, pl.ds(ls + off, size), :],
                sem.at[buf]).start())

    @pl.when(i == 0)
    def _():
        sorted_ref[...] = jnp.zeros_like(sorted_ref)
        fetch(0, 0)

    @pl.when(i + 1 < nsteps)
    def _():
        fetch(i + 1, (i + 1) % 2)

    cur = i % 2
    _for_each_piece(ltot_ref[i], nrow, lambda off, size: pltpu.make_async_copy(
        ybuf_ref.at[pl.ds(0, size), :], sorted_ref.at[cur, pl.ds(0, size), :], sem.at[cur]).wait())

    rt = rt_ref[...]
    r = lax.broadcasted_iota(I32, (tc, nrow), 1).astype(F32)
    unsort = jnp.where(r == rt[:, 0:1], rt[:, 2:3], 0.0) + jnp.where(r == rt[:, 1:2], rt[:, 3:4], 0.0)
    ffn = _dg(unsort.astype(BF16), sorted_ref[cur])
    x2 = _layer_norm(alpha * x1_ref[...] + ffn, g_ref[...], b_ref[...])
    gate = _sigmoid(_dot(x2, wpg_ref[...]))
    out_ref[...] = x2 + gate * _dot(p_ref[...], wpe_ref[...])


def _combine(seglen, lstart, gstart, ltot, x1, route_t, pf, wpe, wpg, g, bta, ybuf, alpha):
    n, d = x1.shape
    tc = TD_DISPATCH
    tile = lambda a_: pl.BlockSpec((tc, a_.shape[1]), lambda i, *_: (i, 0))
    const = lambda a_: pl.BlockSpec(a_.shape, lambda i, *_: (0, 0))
    return pl.pallas_call(
        functools.partial(_combine_kernel, alpha),
        grid_spec=pltpu.PrefetchScalarGridSpec(
            num_scalar_prefetch=4,
            grid=(n // tc,),
            in_specs=[tile(x1), tile(route_t), tile(pf), const(wpe), const(wpg), const(g), const(bta),
                      pl.BlockSpec(memory_space=pl.ANY)],
            out_specs=pl.BlockSpec((tc, d), lambda i, *_: (i, 0)),
            scratch_shapes=[pltpu.VMEM((2, _sorted_rows(tc), d), BF16), pltpu.SemaphoreType.DMA((2,))],
        ),
        out_shape=jax.ShapeDtypeStruct((n, d), F32),
        compiler_params=_params(("arbitrary",)),
    )(seglen, lstart, gstart, ltot, x1, route_t, pf, wpe, wpg, g, bta, ybuf)


def _layer(x, p_i, w_in, rw_mu, rw_w0, rw_w_up, rw_a0, rw_a_up, rw_g_up, rw_k_k, rw_k_a, rw_r_k, rw_gn_w,
           rw_gn_b, w_a_out, hg_lb_logits, hg_norm_w, w_b_out, w_o, ln1_g, ln1_b, router_g_w, router_g_b,
           router_e_w, router_e_b, w1, w3, w2, ln2_g, ln2_b, w_pe, w_pg, alpha, layer):
    b, t, d = x.shape
    n = b * t
    rw_dim = rw_w0.shape[0]
    rw_cols = 3 * rw_dim + RW_DECAY_LORA + RW_A_LORA + RW_GATE_LORA
    hg_cols = 4 * hg_norm_w.shape[0]
    ne = N_GROUPS * EXPERTS_PER_GROUP
    row2 = lambda a_: a_.reshape(1, -1)
    xf = x.reshape(n, d)

    wb = w_in.astype(BF16)
    proj_rw, proj_hg, proj_gt = _project(xf, wb[:, :rw_cols], wb[:, rw_cols:rw_cols + hg_cols],
                                         wb[:, rw_cols + hg_cols:])
    ya = _rwkv_branch(proj_rw.reshape(b, t, rw_cols), rw_mu, row2(rw_w0), rw_w_up, rw_a0, rw_a_up, rw_g_up,
                      rw_k_k, rw_k_a, rw_r_k, rw_gn_w, rw_gn_b)
    yb = _hgrn_branch(proj_hg.reshape(b, t, hg_cols), hg_lb_logits, hg_norm_w, layer)

    wr = jnp.zeros((LANES, d), F32)
    wr = wr.at[:N_GROUPS].set(router_g_w.T).at[ROUTER_EXPERT_ROW:ROUTER_EXPERT_ROW + ne].set(router_e_w.T)
    bias = jnp.zeros((LANES,), F32)
    bias = bias.at[:N_GROUPS].set(router_g_b).at[ROUTER_EXPERT_ROW:ROUTER_EXPERT_ROW + ne].set(router_e_b)
    x1, route, seg, cnt = _merge(xf, ya.reshape(n, -1), yb.reshape(n, -1), proj_gt, w_a_out.astype(BF16),
                                 w_b_out.astype(BF16), w_o.astype(BF16), row2(ln1_g), row2(ln1_b), wr,
                                 bias.reshape(LANES, 1), alpha)

    tm = TM_EXPERT
    ntile = n // TD_DISPATCH
    nblk = -(-(2 * n + (SEG_ALIGN - 1) * ne * ntile) // tm) + ne
    assert TM_MERGE == TD_DISPATCH
    segtab, tab = _finalize(seg, cnt, tm)
    per_seg = lambda row: segtab[:, row, :ne].reshape(-1)
    seglen, lstart, gstart, ltot = per_seg(0), per_seg(1), per_seg(2), segtab[:, 3, 0]
    n_used = tab[1, :1]

    xbuf = _dispatch(seglen, lstart, gstart, ltot, tab[2, :ne], tab[3, :ne], n_used, x1, route, nblk * tm, tm)
    ybuf = _experts(tab[0, :ne], tab[4, :ne], n_used, xbuf, w1, w3, w2)
    out = _combine(seglen, lstart, gstart, ltot, x1, route[2:6].T, p_i.reshape(n, -1), w_pe.astype(BF16),
                   w_pg.astype(BF16), row2(ln2_g), row2(ln2_b), ybuf, alpha)
    return out.reshape(b, t, d)


def kernel(x, p, w_in, rw_mu, rw_w0, rw_w_up, rw_a0, rw_a_up, rw_g_up, rw_k_k, rw_k_a, rw_r_k, rw_gn_w, rw_gn_b,
           w_a_out, hg_lb_logits, hg_norm_w, w_b_out, w_o, ln1_g, ln1_b, router_g_w, router_g_b, router_e_w,
           router_e_b, w1, w3, w2, ln2_g, ln2_b, w_pe, w_pg):
    depth = w_in.shape[0]
    alpha = (2 * depth) ** 0.25
    for i in range(depth):
        x = _layer(x, p[i], w_in[i], rw_mu[i], rw_w0[i], rw_w_up[i], rw_a0[i], rw_a_up[i], rw_g_up[i], rw_k_k[i],
                   rw_k_a[i], rw_r_k[i].reshape(-1), rw_gn_w[i], rw_gn_b[i], w_a_out[i], hg_lb_logits,
                   hg_norm_w[i], w_b_out[i], w_o[i], ln1_g[i], ln1_b[i],
                   router_g_w[i], router_g_b[i], router_e_w[i], router_e_b[i], w1[i], w3[i], w2[i], ln2_g[i],
                   ln2_b[i], w_pe[i], w_pg[i], alpha, i)
    return x
```

```python
import functools

import jax
import jax.numpy as jnp
from jax import lax
from jax.experimental import pallas as pl
from jax.experimental.pallas import tpu as pltpu

F32 = jnp.float32
BF16 = jnp.bfloat16
I32 = jnp.int32

NN = (((1,), (0,)), ((), ()))
NT = (((1,), (1,)), ((), ()))

RW_HEAD = 64
RW_DECAY_LORA = 64
RW_A_LORA = 64
RW_GATE_LORA = 128
RW_GN_EPS = 64e-5
RW_DECAY_SCALE = 0.6065306597126334
N_GROUPS = 4
EXPERTS_PER_GROUP = 8
LN_EPS = 1e-5
RMS_EPS = 1e-6

CHUNK = 64
SUB = 16
LANES = 128
VMEM_LIMIT = 56 * 1024 * 1024

TM_PROJ = 512
TB_RWKV = 512
TM_MERGE = 512
MERGE_ROWS = 256
TD_DISPATCH = 512
SEG_ALIGN = 16
TM_EXPERT = 256
RW_WIDE = 4
HG_WIDE = 4


def _dg(a, b, dn=NN):
    return lax.dot_general(a, b, dn, preferred_element_type=F32)


def _dot(a, b, dn=NN):
    return _dg(a.astype(BF16), b.astype(BF16), dn)


def _split(a):
    hi = a.astype(BF16)
    lo = (a - hi.astype(F32)).astype(BF16)
    return hi, lo


def _dot_hl(a, b_exact, dn=NN):
    hi, lo = _split(a)
    return _dg(hi, b_exact, dn) + _dg(lo, b_exact, dn)


def _dot3(a, b, dn=NN):
    ah, al = _split(a)
    bh, bl = _split(b)
    return _dg(ah, bh, dn) + (_dg(ah, bl, dn) + _dg(al, bh, dn))


def _cumsum_chunks(x, tri):
    h = x.astype(BF16)
    r1 = x - h.astype(F32)
    m = r1.astype(BF16)
    l = (r1 - m.astype(F32)).astype(BF16)
    return _dg(tri, h) + (_dg(tri, m) + _dg(tri, l))


def _sigmoid(x):
    return 0.5 * jnp.tanh(0.5 * x) + 0.5


def _layer_norm(h, g, b):
    mu = jnp.mean(h, axis=-1, keepdims=True)
    d = h - mu
    var = jnp.mean(d * d, axis=-1, keepdims=True)
    return d * lax.rsqrt(var + LN_EPS) * g + b


def _params(sem):
    return pltpu.CompilerParams(dimension_semantics=sem, vmem_limit_bytes=VMEM_LIMIT)


def _proj_kernel(x_ref, wr_ref, wh_ref, wg_ref, pr_ref, ph_ref, pg_ref):
    xb = x_ref[...].astype(BF16)
    pr_ref[...] = _dg(xb, wr_ref[...])
    ph_ref[...] = _dg(xb, wh_ref[...])
    pg_ref[...] = _dg(xb, wg_ref[...])


def _project(xf, w_rw, w_hg, w_gt):
    n, d = xf.shape
    tm = TM_PROJ
    full = lambda w: pl.BlockSpec(w.shape, lambda i: (0, 0))
    tile = lambda c: pl.BlockSpec((tm, c), lambda i: (i, 0))
    return pl.pallas_call(
        _proj_kernel,
        grid=(n // tm,),
        in_specs=[tile(d), full(w_rw), full(w_hg), full(w_gt)],
        out_specs=[tile(w_rw.shape[1]), tile(w_hg.shape[1]), tile(w_gt.shape[1])],
        out_shape=[jax.ShapeDtypeStruct((n, w.shape[1]), F32) for w in (w_rw, w_hg, w_gt)],
        compiler_params=_params(("parallel",)),
    )(xf, w_rw, w_hg, w_gt)


def _each(f, *ls):
    return [f(*xs) for xs in zip(*ls)]


def _two(x):
    m1 = lax.broadcasted_iota(I32, x.shape, 1) < RW_HEAD
    return jnp.concatenate([jnp.where(m1, x, 0.0), jnp.where(m1, 0.0, x)], axis=0)


def _rwkv_chunk_prepare(ins, lvl_ref, out):
    c = CHUNK
    lane = lax.broadcasted_iota(I32, (c, LANES), 1)
    row = lax.broadcasted_iota(I32, (c, LANES), 0)
    scol = jnp.bitwise_and(lane, RW_HEAD - 1)
    strict = row > scol
    incl = row >= scol
    r2 = lax.broadcasted_iota(I32, (LANES, LANES), 0)
    c2 = lax.broadcasted_iota(I32, (LANES, LANES), 1)
    eye = jnp.where(r2 == c2, 1.0, 0.0).astype(F32)

    def prep(r, k, v, av, bv, lw, lc):
        l_end = lc[c - 1:c]
        e_r = jnp.exp(l_end - lc)
        e_n = jnp.exp(-lc)
        return dict(at=av * jnp.exp(lc - lw), rt=r * jnp.exp(lc), bt=bv * e_n, kt=k * e_n,
                    bk=jnp.concatenate([bv * e_r, k * e_r], axis=0), pc=jnp.exp(l_end), v=v)

    q = [prep(*xs) for xs in ins]
    yield
    p = [_dot(jnp.concatenate([d["at"], d["rt"]], axis=0),
              jnp.concatenate([_two(d["bt"]), _two(d["kt"])], axis=0), NT) for d in q]
    sab = [jnp.where(strict, x[:c, :LANES], 0.0) for x in p]
    sak = [jnp.where(strict, x[:c, LANES:], 0.0) for x in p]
    srb = [jnp.where(incl, x[c:, :LANES], 0.0) for x in p]
    srk = [jnp.where(incl, x[c:, LANES:], 0.0) for x in p]
    yield
    sv = _each(lambda ak, rk, d: _dot(jnp.concatenate([ak, rk], axis=0), _two(d["v"])), sak, srk, q)
    yield

    a_bd = [_two(x) for x in sab]
    t = [eye + a * lvl_ref[0] for a in a_bd]
    for lv in range(1, lvl_ref.shape[0]):
        ta = _each(lambda t_, a: _dot(t_, a * lvl_ref[lv]), t, a_bd)
        yield
        t = _each(lambda t_, ta_: t_ + _dot(ta_, t_), t, ta)
        yield

    x = _each(lambda t_, d, sv_: _dot(t_, jnp.concatenate([_two(d["at"]), _two(sv_[:c])], axis=1)), t, q, sv)
    out.extend(dict(uk=x_[:c, :LANES] + x_[c:, :LANES],
                    w=x_[:c, LANES:] + x_[c:, LANES:],
                    rt=d["rt"], rkv=sv_[c:], srb=srb_, bk=d["bk"], v=d["v"], pc=d["pc"])
               for x_, d, sv_, srb_ in zip(x, q, sv, srb))


def _rwkv_chunk_apply(prep, states, bd, out):
    c = CHUNK
    g1 = _each(lambda d, s: _dot(jnp.concatenate([d["uk"], d["rt"]], axis=0), s, NT), prep, states)
    yield
    u = _each(lambda g, d: g[:c] + d["w"], g1, prep)
    y = _each(lambda g, d, u_: g[c:] + d["rkv"] + _dot(d["srb"], _two(u_)), g1, prep, u)
    yield
    upd = _each(lambda u_, d: _dot(jnp.concatenate([u_, d["v"]], axis=0).T, d["bk"]), u, prep)
    s_new = _each(lambda d, s, up: s * d["pc"] + bd * up, prep, states, upd)
    out.extend(zip(y, s_new))
    yield


def _skewed(gens):
    live = set(range(len(gens)))
    tick = 0
    while live:
        for q in sorted(live):
            if q <= tick:
                try:
                    next(gens[q])
                except StopIteration:
                    live.discard(q)
        tick += 1


def _interleave(*gens, background=None):
    live = [g for g in gens if g is not None]
    while live:
        for g in list(live):
            try:
                next(g)
            except StopIteration:
                live.remove(g)
        if background is not None:
            next(background, None)


def _rwkv_kernel(u_ref, mu_ref, w0_ref, wup_ref, a0_ref, aup_ref, gup_ref, kk_ref, ka_ref, rk_ref,
                 gnw_ref, gnb_ref, tri_ref, gsum_ref, lvl_ref, bd_ref, ya_ref,
                 s_ref, prev_ref, r_s, k_s, v_s, a_s, b_s, lw_s, lc_s, g_s, y_s, background=None):
    tb = pl.program_id(1)

    @pl.when(tb == 0)
    def _():
        s_ref[...] = jnp.zeros_like(s_ref)
        prev_ref[...] = jnp.zeros_like(prev_ref)

    nt = u_ref.shape[1]
    dim = r_s.shape[1]
    part = CHUNK * RW_WIDE
    npair = dim // LANES
    lanes = [slice(p * LANES, (p + 1) * LANES) for p in range(npair)]
    bd = bd_ref[...]
    gsum = gsum_ref[...]

    def prologue(h):
        rows = slice(h * part, (h + 1) * part)
        u = u_ref[0, rows, :]
        before = prev_ref[...] if h == 0 else u_ref[0, h * part - 1:h * part, :]
        rowid = lax.broadcasted_iota(I32, u.shape, 0)
        shifted = jnp.where(rowid == 0, before, pltpu.roll(u, 1, axis=0))
        um = u + (shifted - u) * mu_ref[...]
        r = um[:, 0:dim]
        k = um[:, dim:2 * dim]
        v = um[:, 2 * dim:3 * dim]
        xwa = um[:, 3 * dim:3 * dim + LANES]
        xg = um[:, 3 * dim + LANES:3 * dim + 2 * LANES]
        yield
        wpre = w0_ref[...] + _dot(jnp.tanh(xwa), wup_ref[...])
        lw = -RW_DECAY_SCALE * _sigmoid(wpre)
        a = _sigmoid(a0_ref[...] + _dot(xwa, aup_ref[...]))
        g_s[rows, :] = _dot(_sigmoid(xg), gup_ref[...])
        yield
        kk = k * kk_ref[...]
        ss = _dot(kk * kk, gsum)
        kk = kk * lax.rsqrt(jnp.maximum(ss, 1e-24))
        r_s[rows, :] = r
        k_s[rows, :] = k * (1.0 + (a - 1.0) * ka_ref[...])
        v_s[rows, :] = v
        a_s[rows, :] = -kk
        b_s[rows, :] = kk * a
        yield
        lw_s[rows, :] = lw
        lc_s[rows, :] = _cumsum_chunks(lw, tri_ref[...])

    def chunk_rows(h):
        return [slice(h * part + ci * CHUNK, h * part + (ci + 1) * CHUNK) for ci in range(RW_WIDE)]

    def prepare(h, out):
        ins = [(r_s[rw, ls], k_s[rw, ls], v_s[rw, ls], a_s[rw, ls], b_s[rw, ls], lw_s[rw, ls], lc_s[rw, ls])
               for rw in chunk_rows(h) for ls in lanes]
        yield from _rwkv_chunk_prepare(ins, lvl_ref, out)

    def apply(h, prep):
        states = [s_ref[p] for p in range(npair)]
        for ci, rw in enumerate(chunk_rows(h)):
            outs = []
            yield from _rwkv_chunk_apply(prep[ci * npair:(ci + 1) * npair], states, bd, outs)
            states = [s_new for _, s_new in outs]
            for (y, _), ls in zip(outs, lanes):
                y_s[rw, ls] = y
        for p in range(npair):
            s_ref[p] = states[p]

    def epilogue(h):
        rows = slice(h * part, (h + 1) * part)
        y = y_s[rows, :]
        inv_n = 1.0 / RW_HEAD
        m = _dot_hl(y, gsum) * inv_n
        d = y - m
        yield
        var = _dot(d * d, gsum) * inv_n
        yn = d * lax.rsqrt(var + RW_GN_EPS) * gnw_ref[...] + gnb_ref[...]
        yield
        bonus = _dot(r_s[rows, :] * k_s[rows, :] * rk_ref[...], gsum) * v_s[rows, :]
        ya_ref[0, rows, :] = ((yn + bonus) * g_s[rows, :]).astype(BF16)

    nparts = nt // part
    preps = [[] for _ in range(nparts)]
    _interleave(prologue(0))
    prev_ref[...] = u_ref[0, nt - 1:nt, :]
    _interleave(prepare(0, preps[0]), prologue(1) if nparts > 1 else None, background=background)
    for h in range(nparts):
        _interleave(apply(h, preps[h]),
                    prepare(h + 1, preps[h + 1]) if h + 1 < nparts else None,
                    prologue(h + 2) if h + 2 < nparts else None,
                    epilogue(h - 1) if h > 0 else None, background=background)
    _interleave(epilogue(nparts - 1), background=background)
    if background is not None:
        for _ in background:
            pass


def _recurrent_branches(proj_rw, mu, w0, wup, a0, aup, gup, k_k, k_a, r_k, gn_w, gn_b,
                        proj_hg, lb_logits, norm_w, layer):
    b, t, cols = proj_rw.shape
    dim = w0.shape[1]
    tb = TB_RWKV
    ii = jnp.arange(CHUNK * RW_WIDE)
    tri = ((ii[:, None] // CHUNK == ii[None, :] // CHUNK) & (ii[:, None] >= ii[None, :])).astype(BF16)
    jj = jnp.arange(dim)
    gsum = (jj[:, None] // RW_HEAD == jj[None, :] // RW_HEAD).astype(BF16)
    rr = jnp.arange(LANES)[:, None]
    cc = jnp.arange(LANES)[None, :]
    lvls = []
    s = 1
    while s < CHUNK:
        lvls.append(((rr // (2 * s) == cc // (2 * s)) & ((rr // s) % 2 == 1) & ((cc // s) % 2 == 0)).astype(F32))
        s *= 2
    lvl = jnp.stack(lvls)
    bd = (rr // RW_HEAD == cc // RW_HEAD).astype(F32)
    zpad = lambda rows: jnp.zeros((rows, dim), F32)
    wup_p = jnp.concatenate([wup, zpad(LANES - wup.shape[0])], axis=0).astype(BF16)
    aup_p = jnp.concatenate([zpad(LANES - aup.shape[0]), aup], axis=0).astype(BF16)
    row2 = lambda a_: a_.reshape(1, -1)
    const = lambda a_: pl.BlockSpec(a_.shape, lambda bi, ti: (0,) * a_.ndim)
    args = [row2(mu), row2(w0), wup_p, row2(a0), aup_p, gup.astype(BF16), row2(k_k), row2(k_a), row2(r_k),
            row2(gn_w), row2(gn_b), tri, gsum, lvl, bd]
    sc = lambda: pltpu.VMEM((tb, dim), F32)
    hcols = proj_hg.shape[2]
    hdim = hcols // 4
    hpart = CHUNK * HG_WIDE
    hh = jnp.arange(hpart)
    htri = ((hh[:, None] // CHUNK == hh[None, :] // CHUNK) & (hh[:, None] >= hh[None, :])).astype(BF16)
    hj = jnp.arange(hdim)
    hgsum = (hj[:, None] // LANES == hj[None, :] // LANES).astype(BF16)
    hargs = [lb_logits, norm_w.reshape(1, -1), htri, hgsum]
    hsc = lambda: pltpu.VMEM((hpart, hdim), F32)
    assert 1 + len(args) == N_RWKV_IN and 1 + len(hargs) == N_HGRN_IN
    blk = lambda c: pl.BlockSpec((1, tb, c), lambda bi, ti: (bi, ti, 0))
    return pl.pallas_call(
        functools.partial(_recurrent_kernel, layer),
        grid=(b, t // tb),
        in_specs=[blk(cols)] + [const(a_) for a_ in args] + [blk(hcols)] + [const(a_) for a_ in hargs],
        out_specs=[blk(dim), blk(hdim)],
        out_shape=[jax.ShapeDtypeStruct((b, t, dim), BF16), jax.ShapeDtypeStruct((b, t, hdim), BF16)],
        scratch_shapes=[pltpu.VMEM((dim // LANES, LANES, LANES), F32), pltpu.VMEM((1, cols), F32)]
                       + [sc() for _ in range(9)]
                       + [pltpu.VMEM((hdim // LANES, LANES, LANES), F32)] + [hsc() for _ in range(6)],
        compiler_params=_params(("arbitrary", "arbitrary")),
    )(proj_rw, *args, proj_hg, *hargs)


def _hgrn_chunk_prepare(ins):
    c = CHUNK
    subs = [(SUB * i, SUB * (i + 1)) for i in range(c // SUB)]

    def scores(q, k, lf, bc, lo, hi):
        m = bc[lo:lo + 1] - lf[lo:lo + 1]
        att = _dot(q[lo:hi] * jnp.exp(bc[lo:hi] - m), k[:hi] * jnp.exp(m - bc[:hi]), NT)
        tt = lax.broadcasted_iota(I32, (SUB, hi), 0) + lo
        s_ = lax.broadcasted_iota(I32, (SUB, hi), 1)
        return jnp.where(s_ <= tt, att, 0.0)

    att = [[scores(q, k, lf, bc, lo, hi) for lo, hi in subs] for q, k, v, lf, bc in ins]
    upd = [_dot(v.T, k * jnp.exp(bc[c - 1:c] - bc)) for q, k, v, lf, bc in ins]
    intra = [[_dot(a, x[2][:hi]) for a, (lo, hi) in zip(arow, subs)] for arow, x in zip(att, ins)]
    return [dict(intra=jnp.concatenate(rows, axis=0), qe=x[0] * jnp.exp(x[4]), upd=up, pc=jnp.exp(x[4][c - 1:c]))
            for rows, x, up in zip(intra, ins, upd)]


def _hgrn_chunk_apply(prep, states):
    o = _each(lambda d, st: d["intra"] + _dot(d["qe"], st, NT), prep, states)
    st_new = _each(lambda d, st: st * d["pc"] + d["upd"], prep, states)
    return list(zip(o, st_new))


def _hgrn_steps(layer, u_ref, lbl_ref, nw_ref, tri_ref, gsum_ref, yb_ref,
                st_ref, q_s, k_s, v_s, lf_s, bc_s, o_s):
    nt = u_ref.shape[1]
    dim = q_s.shape[1]
    part = q_s.shape[0]
    nhead = dim // LANES
    lanes = [slice(h * LANES, (h + 1) * LANES) for h in range(nhead)]
    lbl = lbl_ref[...]
    e = jnp.exp(lbl - jnp.max(lbl, axis=0, keepdims=True))
    lb = jnp.sum(e[0:layer + 1], axis=0, keepdims=True) / jnp.sum(e, axis=0, keepdims=True)
    for h0 in range(nt // part):
        rows = slice(h0 * part, (h0 + 1) * part)
        zf = u_ref[0, rows, dim:2 * dim]
        sig = _sigmoid(zf)
        f = lb + (1.0 - lb) * sig
        k_s[...] = (1.0 - lb) * (1.0 - sig)
        yield
        qin = u_ref[0, rows, 0:dim]
        q_s[...] = qin * _sigmoid(qin)
        v_s[...] = u_ref[0, rows, 2 * dim:3 * dim]
        yield
        lf = jnp.log(f)
        lf_s[...] = lf
        bc_s[...] = _cumsum_chunks(lf, tri_ref[...])
        yield
        crows = [slice(ci * CHUNK, (ci + 1) * CHUNK) for ci in range(part // CHUNK)]
        ins = [(q_s[rw, ls], k_s[rw, ls], v_s[rw, ls], lf_s[rw, ls], bc_s[rw, ls]) for rw in crows for ls in lanes]
        prep = _hgrn_chunk_prepare(ins)
        yield
        states = [st_ref[h] for h in range(nhead)]
        for ci, rw in enumerate(crows):
            outs = _hgrn_chunk_apply(prep[ci * nhead:(ci + 1) * nhead], states)
            states = [st_new for _, st_new in outs]
            for (o, _), ls in zip(outs, lanes):
                o_s[rw, ls] = o
            yield
        for h in range(nhead):
            st_ref[h] = states[h]
        o = o_s[...]
        ms = _dot(o * o, gsum_ref[...]) * (1.0 / LANES)
        og = u_ref[0, rows, 3 * dim:4 * dim]
        yb_ref[0, rows, :] = (o * lax.rsqrt(ms + RMS_EPS) * nw_ref[...] * _sigmoid(og)).astype(BF16)
        yield


N_RWKV_IN = 16
N_HGRN_IN = 5
N_RWKV_SCRATCH = 11


def _recurrent_kernel(layer, *refs):
    rw_in = refs[:N_RWKV_IN]
    hg_in = refs[N_RWKV_IN:N_RWKV_IN + N_HGRN_IN]
    ya_ref, yb_ref = refs[N_RWKV_IN + N_HGRN_IN:N_RWKV_IN + N_HGRN_IN + 2]
    scratch = refs[N_RWKV_IN + N_HGRN_IN + 2:]
    rw_scratch, hg_scratch = scratch[:N_RWKV_SCRATCH], scratch[N_RWKV_SCRATCH:]

    @pl.when(pl.program_id(1) == 0)
    def _():
        hg_scratch[0][...] = jnp.zeros_like(hg_scratch[0])

    _rwkv_kernel(*rw_in, ya_ref, *rw_scratch,
                 background=_hgrn_steps(layer, *hg_in, yb_ref, *hg_scratch))


def _merge_kernel(alpha, x_ref, ya_ref, yb_ref, pg_ref, wa_ref, wb_ref, wo_ref, g_ref, b_ref, wr_ref,
                  bias_ref, upper_ref, lower_ref, x1_ref, route_ref, seg_ref, cnt_ref, lt_ref, carry_ref):
    tm, d = x_ref.shape

    def rows_stage(rs):
        ma = _dg(ya_ref[rs, :], wa_ref[...])
        mb = _dg(yb_ref[rs, :], wb_ref[...])
        yield
        merged = _sigmoid(pg_ref[rs, :d]) * ma + _sigmoid(pg_ref[rs, d:]) * mb
        yield
        h = alpha * x_ref[rs, :] + _dot(merged, wo_ref[...])
        yield
        x1 = _layer_norm(h, g_ref[...], b_ref[...])
        x1_ref[rs, :] = x1
        yield
        lt_ref[:, rs] = _dot3(wr_ref[...], x1, NT)
        yield

    groups = [rows_stage(slice(q * MERGE_ROWS, (q + 1) * MERGE_ROWS)) for q in range(tm // MERGE_ROWS)]
    _skewed(groups)
    _route_tile(lt_ref, bias_ref, upper_ref, lower_ref, route_ref, seg_ref, cnt_ref, carry_ref)


def _merge(xf, ya, yb, pgate, wa, wb, wo, g, bta, wr, bias_col, alpha):
    n, d = xf.shape
    tm = TM_MERGE
    ne = N_GROUPS * EXPERTS_PER_GROUP
    ii = jnp.arange(tm)
    upper = (ii[:, None] < ii[None, :]).astype(BF16)
    ee = jnp.arange(ne)
    lower = (ee[:, None] > ee[None, :]).astype(BF16)
    tile = lambda a_: pl.BlockSpec((tm, a_.shape[1]), lambda i: (i, 0))
    const = lambda a_: pl.BlockSpec(a_.shape, lambda i: (0, 0))
    return pl.pallas_call(
        functools.partial(_merge_kernel, alpha),
        grid=(n // tm,),
        in_specs=[tile(xf), tile(ya), tile(yb), tile(pgate), const(wa), const(wb), const(wo), const(g),
                  const(bta), const(wr), const(bias_col), const(upper), const(lower)],
        out_specs=[pl.BlockSpec((tm, d), lambda i: (i, 0)), pl.BlockSpec((8, tm), lambda i: (0, i)),
                   pl.BlockSpec((1, 8, LANES), lambda i: (i, 0, 0)), pl.BlockSpec((ne, LANES), lambda i: (0, 0))],
        out_shape=[jax.ShapeDtypeStruct((n, d), F32), jax.ShapeDtypeStruct((8, n), F32),
                   jax.ShapeDtypeStruct((n // tm, 8, LANES), F32), jax.ShapeDtypeStruct((ne, LANES), F32)],
        scratch_shapes=[pltpu.VMEM((LANES, tm), F32), pltpu.VMEM((ne, 1), F32)],
        compiler_params=_params(("arbitrary",)),
    )(xf, ya, yb, pgate, wa, wb, wo, g, bta, wr, bias_col, upper, lower)


ROUTER_EXPERT_ROW = 8


def _to_lanes(col, nl):
    ne = col.shape[0]
    diag = lax.broadcasted_iota(I32, (ne, nl), 0) == lax.broadcasted_iota(I32, (ne, nl), 1)
    return jnp.sum(jnp.where(diag, col, 0.0), axis=0, keepdims=True)


def _route_tile(lt_ref, bias_ref, upper_ref, lower_ref, route_ref, seg_ref, cnt_ref, carry_ref):
    @pl.when(pl.program_id(0) == 0)
    def _():
        carry_ref[...] = jnp.zeros_like(carry_ref)

    ne = N_GROUPS * EXPERTS_PER_GROUP
    lt = lt_ref[...] + bias_ref[...]
    nb = lt.shape[1]
    neg = -jnp.inf
    lg = lt[0:8]
    rg = lax.broadcasted_iota(I32, (8, nb), 0).astype(F32)
    lg = jnp.where(rg < N_GROUPS, lg, neg)
    mg = jnp.max(lg, axis=0, keepdims=True)
    gidx = jnp.min(jnp.where(lg == mg, rg, 1e9), axis=0, keepdims=True)
    pg_sel = 1.0 / jnp.sum(jnp.exp(lg - mg), axis=0, keepdims=True)

    le = lt[ROUTER_EXPERT_ROW:ROUTER_EXPERT_ROW + ne]
    re = lax.broadcasted_iota(I32, (ne, nb), 0).astype(F32)
    in_group = jnp.floor(re * (1.0 / EXPERTS_PER_GROUP)) == gidx
    l1 = jnp.where(in_group, le, neg)
    m1 = jnp.max(l1, axis=0, keepdims=True)
    i1 = jnp.min(jnp.where(l1 == m1, re, 1e9), axis=0, keepdims=True)
    l2 = jnp.where(re == i1, neg, l1)
    m2 = jnp.max(l2, axis=0, keepdims=True)
    i2 = jnp.min(jnp.where(l2 == m2, re, 1e9), axis=0, keepdims=True)
    e2 = jnp.exp(m2 - m1)
    w1 = pg_sel / (1.0 + e2)
    w2 = pg_sel * e2 / (1.0 + e2)

    sel1 = re == i1
    sel2 = re == i2
    onehot = jnp.where(sel1 | sel2, 1.0, 0.0)
    before = _dg(onehot.astype(BF16), upper_ref[...])
    cnt_t = jnp.sum(onehot, axis=1, keepdims=True)
    seg = jnp.floor((cnt_t + (SEG_ALIGN - 1)) * (1.0 / SEG_ALIGN)) * SEG_ALIGN
    lstart = _dg(lower_ref[...], jnp.broadcast_to(seg, (ne, LANES)).astype(BF16))[:, 0:1]
    tot = lstart + before
    lpos1 = jnp.sum(jnp.where(sel1, tot, 0.0), axis=0, keepdims=True)
    lpos2 = jnp.sum(jnp.where(sel2, tot, 0.0), axis=0, keepdims=True)
    grel = carry_ref[...]
    carry = grel + seg
    carry_ref[...] = carry
    cnt_ref[...] = jnp.broadcast_to(carry, cnt_ref.shape)
    zero = jnp.zeros_like(w1)
    route_ref[...] = jnp.concatenate([i1, i2, lpos1, lpos2, w1, w2, zero, zero], axis=0)
    nl = seg_ref.shape[2]
    zl = jnp.zeros((1, nl), F32)
    ltot = jnp.broadcast_to(jnp.sum(seg, axis=0, keepdims=True), (1, nl))
    seg_ref[0] = jnp.concatenate([_to_lanes(seg, nl), _to_lanes(lstart, nl), _to_lanes(grel, nl), ltot,
                                  zl, zl, zl, zl], axis=0)


TAB_LANES = LANES


def _finalize_kernel(tm, seg_ref, cnt_ref, lower_ref, segtab_ref, tab_ref):
    ne = cnt_ref.shape[0]
    cnt = cnt_ref[...]
    nb = jnp.floor((cnt + (tm - 1)) * (1.0 / tm))
    bstart = _dg(lower_ref[...], nb.astype(BF16))
    bend = bstart + nb
    pad_start = bstart[:, 0:1] * tm
    pad_start_row = _to_lanes(pad_start, seg_ref.shape[2])
    for t in range(seg_ref.shape[0]):
        seg = seg_ref[t]
        segtab_ref[t] = jnp.concatenate([seg[0:2], seg[2:3] + pad_start_row, seg[3:8]], axis=0).astype(I32)

    nl = tab_ref.shape[1]
    n_used = jnp.max(bend[:, 0:1], axis=0, keepdims=True)
    pad_lo = _to_lanes(pad_start + cnt[:, 0:1], nl)
    pad_hi = _to_lanes(bend[:, 0:1] * tm, nl)
    zero = jnp.zeros((1, nl), F32)
    tab_ref[...] = jnp.concatenate([_to_lanes(bstart[:, 0:1], nl), jnp.broadcast_to(n_used, (1, nl)), pad_lo,
                                    pad_hi, _to_lanes(nb[:, 0:1], nl), zero, zero, zero], axis=0).astype(I32)


def _finalize(seg, cnt, tm):
    ntile = seg.shape[0]
    ne = cnt.shape[0]
    ii = jnp.arange(ne)
    lower = (ii[:, None] > ii[None, :]).astype(BF16)
    return pl.pallas_call(
        functools.partial(_finalize_kernel, tm),
        grid=(1,),
        in_specs=[pl.BlockSpec(seg.shape, lambda i: (0, 0, 0)), pl.BlockSpec(cnt.shape, lambda i: (0, 0)),
                  pl.BlockSpec((ne, ne), lambda i: (0, 0))],
        out_specs=[pl.BlockSpec(seg.shape, lambda i: (0, 0, 0)), pl.BlockSpec((8, TAB_LANES), lambda i: (0, 0))],
        out_shape=[jax.ShapeDtypeStruct((ntile, 8, LANES), I32), jax.ShapeDtypeStruct((8, TAB_LANES), I32)],
        compiler_params=_params(("arbitrary",)),
    )(seg, cnt, lower)


def _for_each_piece(length, max_len, fn):
    size = SEG_ALIGN
    sizes = []
    while size <= max_len:
        sizes.append(size)
        size *= 2

    for size in reversed(sizes):
        @pl.when(jnp.bitwise_and(length, size) != 0)
        def _(size=size):
            fn(pl.multiple_of(jnp.bitwise_and(length, -2 * size), SEG_ALIGN), size)


def _sorted_rows(td):
    return 2 * td + N_GROUPS * EXPERTS_PER_GROUP * SEG_ALIGN


def _dispatch_kernel(tm, seglen_ref, lstart_ref, gstart_ref, ltot_ref, plo_ref, phi_ref, nu_ref,
                     x_ref, route_ref, xbuf_ref, sorted_ref, zblk, sem, zsem):
    i = pl.program_id(0)
    nsteps = pl.num_programs(0)
    td = x_ref.shape[0]
    ne = plo_ref.shape[0]
    nblk = xbuf_ref.shape[0] // tm
    nrow = sorted_ref.shape[1]
    buf = i % 2

    lpos = route_ref[2:4, :]
    xb = x_ref[...].astype(BF16)

    def sort_rows(lo, hi):
        r = (lax.broadcasted_iota(I32, (hi - lo, td), 0) + lo).astype(F32)
        onehot = jnp.where((r == lpos[0:1]) | (r == lpos[1:2]), 1.0, 0.0).astype(BF16)
        sorted_ref[buf, lo:hi, :] = _dg(onehot, xb).astype(BF16)

    main = nrow - ne * SEG_ALIGN // 2
    sort_rows(0, main)

    @pl.when(ltot_ref[i] > main)
    def _():
        sort_rows(main, nrow)

    def wait_tile(step, b):
        _for_each_piece(ltot_ref[step], nrow, lambda off, size: pltpu.make_async_copy(
            sorted_ref.at[b, pl.ds(0, size), :], xbuf_ref.at[pl.ds(0, size), :], sem.at[b]).wait())

    for e in range(ne):
        idx = i * ne + e
        ls = pl.multiple_of(lstart_ref[idx], SEG_ALIGN)
        gs = pl.multiple_of(gstart_ref[idx], SEG_ALIGN)
        _for_each_piece(seglen_ref[idx], td, lambda off, size: pltpu.make_async_copy(
            sorted_ref.at[buf, pl.ds(ls + off, size), :], xbuf_ref.at[pl.ds(gs + off, size), :],
            sem.at[buf]).start())

    @pl.when(i > 0)
    def _():
        wait_tile(i - 1, 1 - buf)

    @pl.when(i == nsteps - 1)
    def _():
        wait_tile(i, buf)

    def pad_fill(fn):
        for e in range(ne):
            lo = pl.multiple_of(plo_ref[e], SEG_ALIGN)
            _for_each_piece(phi_ref[e] - lo, tm // 2, lambda off, size: fn(pltpu.make_async_copy(
                zblk.at[pl.ds(0, size), :], xbuf_ref.at[pl.ds(lo + off, size), :], zsem)))

        def per_blk(b, carry):
            fn(pltpu.make_async_copy(zblk, xbuf_ref.at[pl.ds(pl.multiple_of(b * tm, tm), tm), :], zsem))
            return carry
        lax.fori_loop(nu_ref[0], nblk, per_blk, 0)

    @pl.when(i == 0)
    def _():
        zblk[...] = jnp.zeros_like(zblk)
        pad_fill(lambda cp: cp.start())
        pad_fill(lambda cp: cp.wait())


def _dispatch(seglen, lstart, gstart, ltot, pad_lo, pad_hi, n_used, x1, route, rows, tm):
    n, d = x1.shape
    td = TD_DISPATCH
    return pl.pallas_call(
        functools.partial(_dispatch_kernel, tm),
        grid_spec=pltpu.PrefetchScalarGridSpec(
            num_scalar_prefetch=7,
            grid=(n // td,),
            in_specs=[pl.BlockSpec((td, d), lambda i, *_: (i, 0)), pl.BlockSpec((8, td), lambda i, *_: (0, i))],
            out_specs=pl.BlockSpec(memory_space=pl.ANY),
            scratch_shapes=[pltpu.VMEM((2, _sorted_rows(td), d), BF16), pltpu.VMEM((tm, d), BF16),
                            pltpu.SemaphoreType.DMA((2,)), pltpu.SemaphoreType.DMA(())],
        ),
        out_shape=jax.ShapeDtypeStruct((rows, d), BF16),
        compiler_params=_params(("arbitrary",)),
    )(seglen, lstart, gstart, ltot, pad_lo, pad_hi, n_used, x1, route)


X_SLOTS = 3


def _expert_kernel(tm, bstart_ref, nb_ref, nu_ref, w1_ref, w3_ref, w2_ref, xbuf_ref, ybuf_ref,
                   wf1, wf3, wf2, w13b, w2b, xb, yb, semw, semx, semy):
    e = pl.program_id(0)
    ne = pl.num_programs(0)
    nb = nb_ref[e]
    b0 = bstart_ref[e]
    nblk = ybuf_ref.shape[0] // tm
    rows = lambda blk: pl.ds(pl.multiple_of(blk * tm, tm), tm)
    x_copy = lambda j, slot: pltpu.make_async_copy(xbuf_ref.at[rows(b0 + j), :], xb.at[slot], semx.at[slot])
    y_copy = lambda blk, slot: pltpu.make_async_copy(yb.at[slot], ybuf_ref.at[rows(blk), :], semy.at[slot])

    def w_copies(ex, slot):
        return [pltpu.make_async_copy(src.at[ex], dst.at[slot], semw.at[slot])
                for src, dst in ((w1_ref, wf1), (w3_ref, wf3), (w2_ref, wf2))]

    ws = e % 2

    @pl.when(e == 0)
    def _():
        for cp in w_copies(0, 0):
            cp.start()

    for j0 in range(X_SLOTS - 1):
        @pl.when(j0 < nb)
        def _(j0=j0):
            x_copy(j0, j0).start()

    for cp in w_copies(e, ws):
        cp.wait()
    de = w2b.shape[0]
    w13b[:, :de] = wf1[ws].astype(BF16)
    w13b[:, de:] = wf3[ws].astype(BF16)
    w2b[...] = wf2[ws].astype(BF16)

    @pl.when(e + 1 < ne)
    def _():
        for cp in w_copies(e + 1, 1 - ws):
            cp.start()

    def body(j, carry):
        slot = j % X_SLOTS
        yslot = j % 2
        x_copy(j, slot).wait()

        @pl.when(j + X_SLOTS - 1 < nb)
        def _():
            x_copy(j + X_SLOTS - 1, (j + X_SLOTS - 1) % X_SLOTS).start()

        @pl.when(j >= 2)
        def _():
            y_copy(b0 + j - 2, yslot).wait()

        x = xb[slot]
        h13 = _dg(x, w13b[...])
        h1 = h13[:, :de]
        h = (h1 * _sigmoid(h1)) * h13[:, de:]
        yb[yslot] = _dot(h, w2b[...]).astype(BF16)
        y_copy(b0 + j, yslot).start()
        return carry

    lax.fori_loop(0, nb, body, 0)

    @pl.when(nb >= 2)
    def _():
        y_copy(b0 + nb - 2, nb % 2).wait()

    @pl.when(nb >= 1)
    def _():
        y_copy(b0 + nb - 1, (nb - 1) % 2).wait()

    @pl.when(e == pl.num_programs(0) - 1)
    def _():
        yb[0] = jnp.zeros(yb.shape[1:], yb.dtype)

        def fill(fn):
            def per_blk(blk, carry):
                fn(y_copy(blk, 0))
                return carry
            lax.fori_loop(nu_ref[0], nblk, per_blk, 0)

        fill(lambda cp: cp.start())
        fill(lambda cp: cp.wait())


def _experts(bstart, nb, n_used, xbuf, w1, w3, w2):
    rows, d = xbuf.shape
    ne, _, de = w1.shape
    tm = TM_EXPERT
    return pl.pallas_call(
        functools.partial(_expert_kernel, tm),
        grid_spec=pltpu.PrefetchScalarGridSpec(
            num_scalar_prefetch=3,
            grid=(ne,),
            in_specs=[pl.BlockSpec(memory_space=pl.ANY)] * 4,
            out_specs=pl.BlockSpec(memory_space=pl.ANY),
            scratch_shapes=[pltpu.VMEM((2, d, de), F32), pltpu.VMEM((2, d, de), F32), pltpu.VMEM((2, de, d), F32),
                            pltpu.VMEM((d, 2 * de), BF16), pltpu.VMEM((de, d), BF16),
                            pltpu.VMEM((X_SLOTS, tm, d), BF16), pltpu.VMEM((2, tm, d), BF16),
                            pltpu.SemaphoreType.DMA((2,)), pltpu.SemaphoreType.DMA((X_SLOTS,)),
                            pltpu.SemaphoreType.DMA((2,))],
        ),
        out_shape=jax.ShapeDtypeStruct((rows, d), BF16),
        compiler_params=_params(("arbitrary",)),
    )(bstart, nb, n_used, w1, w3, w2, xbuf)


def _combine_kernel(alpha, seglen_ref, lstart_ref, gstart_ref, ltot_ref, x1_ref, rt_ref, p_ref, wpe_ref, wpg_ref,
                    g_ref, b_ref, ybuf_ref, out_ref, sorted_ref, sem):
    i = pl.program_id(0)
    nsteps = pl.num_programs(0)
    tc = x1_ref.shape[0]
    ne = N_GROUPS * EXPERTS_PER_GROUP
    nrow = sorted_ref.shape[1]

    def fetch(step, buf):
        for e in range(ne):
            idx = step * ne + e
            ls = pl.multiple_of(lstart_ref[idx], SEG_ALIGN)
            gs = pl.multiple_of(gstart_ref[idx], SEG_ALIGN)
            _for_each_piece(seglen_ref[idx], tc, lambda off, size: pltpu.make_async_copy(
                ybuf_ref.at[pl.ds(gs + off, size), :], sorted_ref.at[buf, pl.ds(ls + off, size), :],
                sem.at[buf]).start())

    @pl.when(i == 0)
    def _():
        sorted_ref[...] = jnp.zeros_like(sorted_ref)
        fetch(0, 0)

    @pl.when(i + 1 < nsteps)
    def _():
        fetch(i + 1, (i + 1) % 2)

    cur = i % 2
    _for_each_piece(ltot_ref[i], nrow, lambda off, size: pltpu.make_async_copy(
        ybuf_ref.at[pl.ds(0, size), :], sorted_ref.at[cur, pl.ds(0, size), :], sem.at[cur]).wait())

    rt = rt_ref[...]
    r = lax.broadcasted_iota(I32, (tc, nrow), 1).astype(F32)
    unsort = jnp.where(r == rt[:, 0:1], rt[:, 2:3], 0.0) + jnp.where(r == rt[:, 1:2], rt[:, 3:4], 0.0)
    ffn = _dg(unsort.astype(BF16), sorted_ref[cur])
    x2 = _layer_norm(alpha * x1_ref[...] + ffn, g_ref[...], b_ref[...])
    gate = _sigmoid(_dot(x2, wpg_ref[...]))
    out_ref[...] = x2 + gate * _dot(p_ref[...], wpe_ref[...])


def _combine(seglen, lstart, gstart, ltot, x1, route_t, pf, wpe, wpg, g, bta, ybuf, alpha):
    n, d = x1.shape
    tc = TD_DISPATCH
    tile = lambda a_: pl.BlockSpec((tc, a_.shape[1]), lambda i, *_: (i, 0))
    const = lambda a_: pl.BlockSpec(a_.shape, lambda i, *_: (0, 0))
    return pl.pallas_call(
        functools.partial(_combine_kernel, alpha),
        grid_spec=pltpu.PrefetchScalarGridSpec(
            num_scalar_prefetch=4,
            grid=(n // tc,),
            in_specs=[tile(x1), tile(route_t), tile(pf), const(wpe), const(wpg), const(g), const(bta),
                      pl.BlockSpec(memory_space=pl.ANY)],
            out_specs=pl.BlockSpec((tc, d), lambda i, *_: (i, 0)),
            scratch_shapes=[pltpu.VMEM((2, _sorted_rows(tc), d), BF16), pltpu.SemaphoreType.DMA((2,))],
        ),
        out_shape=jax.ShapeDtypeStruct((n, d), F32),
        compiler_params=_params(("arbitrary",)),
    )(seglen, lstart, gstart, ltot, x1, route_t, pf, wpe, wpg, g, bta, ybuf)


def _layer(x, p_i, w_in, rw_mu, rw_w0, rw_w_up, rw_a0, rw_a_up, rw_g_up, rw_k_k, rw_k_a, rw_r_k, rw_gn_w,
           rw_gn_b, w_a_out, hg_lb_logits, hg_norm_w, w_b_out, w_o, ln1_g, ln1_b, router_g_w, router_g_b,
           router_e_w, router_e_b, w1, w3, w2, ln2_g, ln2_b, w_pe, w_pg, alpha, layer):
    b, t, d = x.shape
    n = b * t
    rw_dim = rw_w0.shape[0]
    rw_cols = 3 * rw_dim + RW_DECAY_LORA + RW_A_LORA + RW_GATE_LORA
    hg_cols = 4 * hg_norm_w.shape[0]
    ne = N_GROUPS * EXPERTS_PER_GROUP
    row2 = lambda a_: a_.reshape(1, -1)
    xf = x.reshape(n, d)

    wb = w_in.astype(BF16)
    proj_rw, proj_hg, proj_gt = _project(xf, wb[:, :rw_cols], wb[:, rw_cols:rw_cols + hg_cols],
                                         wb[:, rw_cols + hg_cols:])
    ya, yb = _recurrent_branches(proj_rw.reshape(b, t, rw_cols), rw_mu, row2(rw_w0), rw_w_up, rw_a0, rw_a_up,
                                 rw_g_up, rw_k_k, rw_k_a, rw_r_k, rw_gn_w, rw_gn_b,
                                 proj_hg.reshape(b, t, hg_cols), hg_lb_logits, hg_norm_w, layer)

    wr = jnp.zeros((LANES, d), F32)
    wr = wr.at[:N_GROUPS].set(router_g_w.T).at[ROUTER_EXPERT_ROW:ROUTER_EXPERT_ROW + ne].set(router_e_w.T)
    bias = jnp.zeros((LANES,), F32)
    bias = bias.at[:N_GROUPS].set(router_g_b).at[ROUTER_EXPERT_ROW:ROUTER_EXPERT_ROW + ne].set(router_e_b)
    x1, route, seg, cnt = _merge(xf, ya.reshape(n, -1), yb.reshape(n, -1), proj_gt, w_a_out.astype(BF16),
                                 w_b_out.astype(BF16), w_o.astype(BF16), row2(ln1_g), row2(ln1_b), wr,
                                 bias.reshape(LANES, 1), alpha)

    tm = TM_EXPERT
    ntile = n // TD_DISPATCH
    nblk = -(-(2 * n + (SEG_ALIGN - 1) * ne * ntile) // tm) + ne
    assert TM_MERGE == TD_DISPATCH
    segtab, tab = _finalize(seg, cnt, tm)
    per_seg = lambda row: segtab[:, row, :ne].reshape(-1)
    seglen, lstart, gstart, ltot = per_seg(0), per_seg(1), per_seg(2), segtab[:, 3, 0]
    n_used = tab[1, :1]

    xbuf = _dispatch(seglen, lstart, gstart, ltot, tab[2, :ne], tab[3, :ne], n_used, x1, route, nblk * tm, tm)
    ybuf = _experts(tab[0, :ne], tab[4, :ne], n_used, xbuf, w1, w3, w2)
    out = _combine(seglen, lstart, gstart, ltot, x1, route[2:6].T, p_i.reshape(n, -1), w_pe.astype(BF16),
                   w_pg.astype(BF16), row2(ln2_g), row2(ln2_b), ybuf, alpha)
    return out.reshape(b, t, d)


def kernel(x, p, w_in, rw_mu, rw_w0, rw_w_up, rw_a0, rw_a_up, rw_g_up, rw_k_k, rw_k_a, rw_r_k, rw_gn_w, rw_gn_b,
           w_a_out, hg_lb_logits, hg_norm_w, w_b_out, w_o, ln1_g, ln1_b, router_g_w, router_g_b, router_e_w,
           router_e_b, w1, w3, w2, ln2_g, ln2_b, w_pe, w_pg):
    depth = w_in.shape[0]
    alpha = (2 * depth) ** 0.25
    for i in range(depth):
        x = _layer(x, p[i], w_in[i], rw_mu[i], rw_w0[i], rw_w_up[i], rw_a0[i], rw_a_up[i], rw_g_up[i], rw_k_k[i],
                   rw_k_a[i], rw_r_k[i].reshape(-1), rw_gn_w[i], rw_gn_b[i], w_a_out[i], hg_lb_logits,
                   hg_norm_w[i], w_b_out[i], w_o[i], ln1_g[i], ln1_b[i],
                   router_g_w[i], router_g_b[i], router_e_w[i], router_e_b[i], w1[i], w3[i], w2[i], ln2_g[i],
                   ln2_b[i], w_pe[i], w_pg[i], alpha, i)
    return x
```

```python
import functools

import jax
import jax.numpy as jnp
from jax import lax
from jax.experimental import pallas as pl
from jax.experimental.pallas import tpu as pltpu

F32 = jnp.float32
BF16 = jnp.bfloat16
I32 = jnp.int32

NN = (((1,), (0,)), ((), ()))
NT = (((1,), (1,)), ((), ()))

RW_HEAD = 64
RW_DECAY_LORA = 64
RW_A_LORA = 64
RW_GATE_LORA = 128
RW_GN_EPS = 64e-5
RW_DECAY_SCALE = 0.6065306597126334
N_GROUPS = 4
EXPERTS_PER_GROUP = 8
LN_EPS = 1e-5
RMS_EPS = 1e-6

CHUNK = 64
SUB = 16
LANES = 128
VMEM_LIMIT = 56 * 1024 * 1024

TM_PROJ = 512
TB_RWKV = 512
TM_MERGE = 512
MERGE_ROWS = 256
TD_DISPATCH = 512
SEG_ALIGN = 16
TM_EXPERT = 256
RW_WIDE = 4
HG_WIDE = 4


def _dg(a, b, dn=NN):
    return lax.dot_general(a, b, dn, preferred_element_type=F32)


def _dot(a, b, dn=NN):
    return _dg(a.astype(BF16), b.astype(BF16), dn)


def _split(a):
    hi = a.astype(BF16)
    lo = (a - hi.astype(F32)).astype(BF16)
    return hi, lo


def _dot_hl(a, b_exact, dn=NN):
    hi, lo = _split(a)
    return _dg(hi, b_exact, dn) + _dg(lo, b_exact, dn)


def _dot3(a, b, dn=NN):
    ah, al = _split(a)
    bh, bl = _split(b)
    return _dg(ah, bh, dn) + (_dg(ah, bl, dn) + _dg(al, bh, dn))


def _cumsum_chunks(x, tri):
    h = x.astype(BF16)
    r1 = x - h.astype(F32)
    m = r1.astype(BF16)
    l = (r1 - m.astype(F32)).astype(BF16)
    return _dg(tri, h) + (_dg(tri, m) + _dg(tri, l))


def _sigmoid(x):
    return 0.5 * jnp.tanh(0.5 * x) + 0.5


def _layer_norm(h, g, b):
    mu = jnp.mean(h, axis=-1, keepdims=True)
    d = h - mu
    var = jnp.mean(d * d, axis=-1, keepdims=True)
    return d * lax.rsqrt(var + LN_EPS) * g + b


def _params(sem):
    return pltpu.CompilerParams(dimension_semantics=sem, vmem_limit_bytes=VMEM_LIMIT)


def _proj_kernel(x_ref, wr_ref, wh_ref, wg_ref, pr_ref, ph_ref, pg_ref):
    xb = x_ref[...].astype(BF16)
    pr_ref[...] = _dg(xb, wr_ref[...])
    ph_ref[...] = _dg(xb, wh_ref[...])
    pg_ref[...] = _dg(xb, wg_ref[...])


def _project(xf, w_rw, w_hg, w_gt):
    n, d = xf.shape
    tm = TM_PROJ
    full = lambda w: pl.BlockSpec(w.shape, lambda i: (0, 0))
    tile = lambda c: pl.BlockSpec((tm, c), lambda i: (i, 0))
    return pl.pallas_call(
        _proj_kernel,
        grid=(n // tm,),
        in_specs=[tile(d), full(w_rw), full(w_hg), full(w_gt)],
        out_specs=[tile(w_rw.shape[1]), tile(w_hg.shape[1]), tile(w_gt.shape[1])],
        out_shape=[jax.ShapeDtypeStruct((n, w.shape[1]), F32) for w in (w_rw, w_hg, w_gt)],
        compiler_params=_params(("parallel",)),
    )(xf, w_rw, w_hg, w_gt)


def _each(f, *ls):
    return [f(*xs) for xs in zip(*ls)]


def _two(x):
    m1 = lax.broadcasted_iota(I32, x.shape, 1) < RW_HEAD
    return jnp.concatenate([jnp.where(m1, x, 0.0), jnp.where(m1, 0.0, x)], axis=0)


def _rwkv_chunk_prepare(ins, lvl_ref, out):
    c = CHUNK
    lane = lax.broadcasted_iota(I32, (c, LANES), 1)
    row = lax.broadcasted_iota(I32, (c, LANES), 0)
    scol = jnp.bitwise_and(lane, RW_HEAD - 1)
    strict = row > scol
    incl = row >= scol
    r2 = lax.broadcasted_iota(I32, (LANES, LANES), 0)
    c2 = lax.broadcasted_iota(I32, (LANES, LANES), 1)
    eye = jnp.where(r2 == c2, 1.0, 0.0).astype(F32)

    def prep(r, k, v, av, bv, lw, lc):
        l_end = lc[c - 1:c]
        e_r = jnp.exp(l_end - lc)
        e_n = jnp.exp(-lc)
        return dict(at=av * jnp.exp(lc - lw), rt=r * jnp.exp(lc), bt=bv * e_n, kt=k * e_n,
                    bk=jnp.concatenate([bv * e_r, k * e_r], axis=0), pc=jnp.exp(l_end), v=v)

    q = [prep(*xs) for xs in ins]
    yield
    p = [_dot(jnp.concatenate([d["at"], d["rt"]], axis=0),
              jnp.concatenate([_two(d["bt"]), _two(d["kt"])], axis=0), NT) for d in q]
    sab = [jnp.where(strict, x[:c, :LANES], 0.0) for x in p]
    sak = [jnp.where(strict, x[:c, LANES:], 0.0) for x in p]
    srb = [jnp.where(incl, x[c:, :LANES], 0.0) for x in p]
    srk = [jnp.where(incl, x[c:, LANES:], 0.0) for x in p]
    yield
    sv = _each(lambda ak, rk, d: _dot(jnp.concatenate([ak, rk], axis=0), _two(d["v"])), sak, srk, q)
    yield

    a_bd = [_two(x) for x in sab]
    t = [eye + a * lvl_ref[0] for a in a_bd]
    for lv in range(1, lvl_ref.shape[0]):
        ta = _each(lambda t_, a: _dot(t_, a * lvl_ref[lv]), t, a_bd)
        yield
        t = _each(lambda t_, ta_: t_ + _dot(ta_, t_), t, ta)
        yield

    x = _each(lambda t_, d, sv_: _dot(t_, jnp.concatenate([_two(d["at"]), _two(sv_[:c])], axis=1)), t, q, sv)
    out.extend(dict(uk=x_[:c, :LANES] + x_[c:, :LANES],
                    w=x_[:c, LANES:] + x_[c:, LANES:],
                    rt=d["rt"], rkv=sv_[c:], srb=srb_, bk=d["bk"], v=d["v"], pc=d["pc"])
               for x_, d, sv_, srb_ in zip(x, q, sv, srb))


def _rwkv_chunk_apply(prep, states, bd, out):
    c = CHUNK
    g1 = _each(lambda d, s: _dot(jnp.concatenate([d["uk"], d["rt"]], axis=0), s, NT), prep, states)
    yield
    u = _each(lambda g, d: g[:c] + d["w"], g1, prep)
    y = _each(lambda g, d, u_: g[c:] + d["rkv"] + _dot(d["srb"], _two(u_)), g1, prep, u)
    yield
    upd = _each(lambda u_, d: _dot(jnp.concatenate([u_, d["v"]], axis=0).T, d["bk"]), u, prep)
    s_new = _each(lambda d, s, up: s * d["pc"] + bd * up, prep, states, upd)
    out.extend(zip(y, s_new))
    yield


def _skewed(gens):
    live = set(range(len(gens)))
    tick = 0
    while live:
        for q in sorted(live):
            if q <= tick:
                try:
                    next(gens[q])
                except StopIteration:
                    live.discard(q)
        tick += 1


def _interleave(*gens, background=None):
    live = [g for g in gens if g is not None]
    while live:
        for g in list(live):
            try:
                next(g)
            except StopIteration:
                live.remove(g)
        if background is not None:
            next(background, None)


def _rwkv_kernel(u_ref, mu_ref, w0_ref, wup_ref, a0_ref, aup_ref, gup_ref, kk_ref, ka_ref, rk_ref,
                 gnw_ref, gnb_ref, tri_ref, gsum_ref, lvl_ref, bd_ref, ya_ref,
                 s_ref, prev_ref, r_s, k_s, v_s, a_s, b_s, lw_s, lc_s, g_s, y_s, background=None):
    tb = pl.program_id(1)

    @pl.when(tb == 0)
    def _():
        s_ref[...] = jnp.zeros_like(s_ref)
        prev_ref[...] = jnp.zeros_like(prev_ref)

    nt = u_ref.shape[1]
    dim = r_s.shape[1]
    part = CHUNK * RW_WIDE
    npair = dim // LANES
    lanes = [slice(p * LANES, (p + 1) * LANES) for p in range(npair)]
    bd = bd_ref[...]
    gsum = gsum_ref[...]

    def prologue(h):
        rows = slice(h * part, (h + 1) * part)
        u = u_ref[0, rows, :]
        before = prev_ref[...] if h == 0 else u_ref[0, h * part - 1:h * part, :]
        rowid = lax.broadcasted_iota(I32, u.shape, 0)
        shifted = jnp.where(rowid == 0, before, pltpu.roll(u, 1, axis=0))
        um = u + (shifted - u) * mu_ref[...]
        r = um[:, 0:dim]
        k = um[:, dim:2 * dim]
        v = um[:, 2 * dim:3 * dim]
        xwa = um[:, 3 * dim:3 * dim + LANES]
        xg = um[:, 3 * dim + LANES:3 * dim + 2 * LANES]
        yield
        wpre = w0_ref[...] + _dot(jnp.tanh(xwa), wup_ref[...])
        lw = -RW_DECAY_SCALE * _sigmoid(wpre)
        a = _sigmoid(a0_ref[...] + _dot(xwa, aup_ref[...]))
        g_s[rows, :] = _dot(_sigmoid(xg), gup_ref[...])
        yield
        kk = k * kk_ref[...]
        ss = _dot(kk * kk, gsum)
        kk = kk * lax.rsqrt(jnp.maximum(ss, 1e-24))
        r_s[rows, :] = r
        k_s[rows, :] = k * (1.0 + (a - 1.0) * ka_ref[...])
        v_s[rows, :] = v
        a_s[rows, :] = -kk
        b_s[rows, :] = kk * a
        yield
        lw_s[rows, :] = lw
        lc_s[rows, :] = _cumsum_chunks(lw, tri_ref[...])

    def chunk_rows(h):
        return [slice(h * part + ci * CHUNK, h * part + (ci + 1) * CHUNK) for ci in range(RW_WIDE)]

    def prepare(h, out):
        ins = [(r_s[rw, ls], k_s[rw, ls], v_s[rw, ls], a_s[rw, ls], b_s[rw, ls], lw_s[rw, ls], lc_s[rw, ls])
               for rw in chunk_rows(h) for ls in lanes]
        yield from _rwkv_chunk_prepare(ins, lvl_ref, out)

    def apply(h, prep):
        states = [s_ref[p] for p in range(npair)]
        for ci, rw in enumerate(chunk_rows(h)):
            outs = []
            yield from _rwkv_chunk_apply(prep[ci * npair:(ci + 1) * npair], states, bd, outs)
            states = [s_new for _, s_new in outs]
            for (y, _), ls in zip(outs, lanes):
                y_s[rw, ls] = y
        for p in range(npair):
            s_ref[p] = states[p]

    def epilogue(h):
        rows = slice(h * part, (h + 1) * part)
        y = y_s[rows, :]
        inv_n = 1.0 / RW_HEAD
        m = _dot_hl(y, gsum) * inv_n
        d = y - m
        yield
        var = _dot(d * d, gsum) * inv_n
        yn = d * lax.rsqrt(var + RW_GN_EPS) * gnw_ref[...] + gnb_ref[...]
        yield
        bonus = _dot(r_s[rows, :] * k_s[rows, :] * rk_ref[...], gsum) * v_s[rows, :]
        ya_ref[0, rows, :] = ((yn + bonus) * g_s[rows, :]).astype(BF16)

    nparts = nt // part
    preps = [[] for _ in range(nparts)]
    _interleave(prologue(0))
    prev_ref[...] = u_ref[0, nt - 1:nt, :]
    _interleave(prepare(0, preps[0]), prologue(1) if nparts > 1 else None)
    for h in range(nparts):
        _interleave(apply(h, preps[h]),
                    prepare(h + 1, preps[h + 1]) if h + 1 < nparts else None,
                    prologue(h + 2) if h + 2 < nparts else None,
                    epilogue(h - 1) if h > 0 else None, background=background)
    _interleave(epilogue(nparts - 1), background=background)
    if background is not None:
        for _ in background:
            pass


def _recurrent_branches(proj_rw, mu, w0, wup, a0, aup, gup, k_k, k_a, r_k, gn_w, gn_b,
                        proj_hg, lb_logits, norm_w, layer):
    b, t, cols = proj_rw.shape
    dim = w0.shape[1]
    tb = TB_RWKV
    ii = jnp.arange(CHUNK * RW_WIDE)
    tri = ((ii[:, None] // CHUNK == ii[None, :] // CHUNK) & (ii[:, None] >= ii[None, :])).astype(BF16)
    jj = jnp.arange(dim)
    gsum = (jj[:, None] // RW_HEAD == jj[None, :] // RW_HEAD).astype(BF16)
    rr = jnp.arange(LANES)[:, None]
    cc = jnp.arange(LANES)[None, :]
    lvls = []
    s = 1
    while s < CHUNK:
        lvls.append(((rr // (2 * s) == cc // (2 * s)) & ((rr // s) % 2 == 1) & ((cc // s) % 2 == 0)).astype(F32))
        s *= 2
    lvl = jnp.stack(lvls)
    bd = (rr // RW_HEAD == cc // RW_HEAD).astype(F32)
    zpad = lambda rows: jnp.zeros((rows, dim), F32)
    wup_p = jnp.concatenate([wup, zpad(LANES - wup.shape[0])], axis=0).astype(BF16)
    aup_p = jnp.concatenate([zpad(LANES - aup.shape[0]), aup], axis=0).astype(BF16)
    row2 = lambda a_: a_.reshape(1, -1)
    const = lambda a_: pl.BlockSpec(a_.shape, lambda bi, ti: (0,) * a_.ndim)
    args = [row2(mu), row2(w0), wup_p, row2(a0), aup_p, gup.astype(BF16), row2(k_k), row2(k_a), row2(r_k),
            row2(gn_w), row2(gn_b), tri, gsum, lvl, bd]
    sc = lambda: pltpu.VMEM((tb, dim), F32)
    hcols = proj_hg.shape[2]
    hdim = hcols // 4
    hpart = CHUNK * HG_WIDE
    hh = jnp.arange(hpart)
    htri = ((hh[:, None] // CHUNK == hh[None, :] // CHUNK) & (hh[:, None] >= hh[None, :])).astype(BF16)
    hj = jnp.arange(hdim)
    hgsum = (hj[:, None] // LANES == hj[None, :] // LANES).astype(BF16)
    hargs = [lb_logits, norm_w.reshape(1, -1), htri, hgsum]
    hsc = lambda: pltpu.VMEM((hpart, hdim), F32)
    assert 1 + len(args) == N_RWKV_IN and 1 + len(hargs) == N_HGRN_IN
    blk = lambda c: pl.BlockSpec((1, tb, c), lambda bi, ti: (bi, ti, 0))
    return pl.pallas_call(
        functools.partial(_recurrent_kernel, layer),
        grid=(b, t // tb),
        in_specs=[blk(cols)] + [const(a_) for a_ in args] + [blk(hcols)] + [const(a_) for a_ in hargs],
        out_specs=[blk(dim), blk(hdim)],
        out_shape=[jax.ShapeDtypeStruct((b, t, dim), BF16), jax.ShapeDtypeStruct((b, t, hdim), BF16)],
        scratch_shapes=[pltpu.VMEM((dim // LANES, LANES, LANES), F32), pltpu.VMEM((1, cols), F32)]
                       + [sc() for _ in range(9)]
                       + [pltpu.VMEM((hdim // LANES, LANES, LANES), F32)] + [hsc() for _ in range(6)],
        compiler_params=_params(("arbitrary", "arbitrary")),
    )(proj_rw, *args, proj_hg, *hargs)


def _hgrn_chunk_prepare(ins):
    c = CHUNK
    subs = [(SUB * i, SUB * (i + 1)) for i in range(c // SUB)]

    def scores(q, k, lf, bc, lo, hi):
        m = bc[lo:lo + 1] - lf[lo:lo + 1]
        att = _dot(q[lo:hi] * jnp.exp(bc[lo:hi] - m), k[:hi] * jnp.exp(m - bc[:hi]), NT)
        tt = lax.broadcasted_iota(I32, (SUB, hi), 0) + lo
        s_ = lax.broadcasted_iota(I32, (SUB, hi), 1)
        return jnp.where(s_ <= tt, att, 0.0)

    att = [[scores(q, k, lf, bc, lo, hi) for lo, hi in subs] for q, k, v, lf, bc in ins]
    upd = [_dot(v.T, k * jnp.exp(bc[c - 1:c] - bc)) for q, k, v, lf, bc in ins]
    intra = [[_dot(a, x[2][:hi]) for a, (lo, hi) in zip(arow, subs)] for arow, x in zip(att, ins)]
    return [dict(intra=jnp.concatenate(rows, axis=0), qe=x[0] * jnp.exp(x[4]), upd=up, pc=jnp.exp(x[4][c - 1:c]))
            for rows, x, up in zip(intra, ins, upd)]


def _hgrn_chunk_apply(prep, states):
    o = _each(lambda d, st: d["intra"] + _dot(d["qe"], st, NT), prep, states)
    st_new = _each(lambda d, st: st * d["pc"] + d["upd"], prep, states)
    return list(zip(o, st_new))


def _hgrn_steps(layer, u_ref, lbl_ref, nw_ref, tri_ref, gsum_ref, yb_ref,
                st_ref, q_s, k_s, v_s, lf_s, bc_s, o_s):
    nt = u_ref.shape[1]
    dim = q_s.shape[1]
    part = q_s.shape[0]
    nhead = dim // LANES
    lanes = [slice(h * LANES, (h + 1) * LANES) for h in range(nhead)]
    lbl = lbl_ref[...]
    e = jnp.exp(lbl - jnp.max(lbl, axis=0, keepdims=True))
    lb = jnp.sum(e[0:layer + 1], axis=0, keepdims=True) / jnp.sum(e, axis=0, keepdims=True)
    for h0 in range(nt // part):
        rows = slice(h0 * part, (h0 + 1) * part)
        zf = u_ref[0, rows, dim:2 * dim]
        sig = _sigmoid(zf)
        f = lb + (1.0 - lb) * sig
        k_s[...] = (1.0 - lb) * (1.0 - sig)
        yield
        qin = u_ref[0, rows, 0:dim]
        q_s[...] = qin * _sigmoid(qin)
        v_s[...] = u_ref[0, rows, 2 * dim:3 * dim]
        yield
        lf = jnp.log(f)
        lf_s[...] = lf
        bc_s[...] = _cumsum_chunks(lf, tri_ref[...])
        yield
        crows = [slice(ci * CHUNK, (ci + 1) * CHUNK) for ci in range(part // CHUNK)]
        ins = [(q_s[rw, ls], k_s[rw, ls], v_s[rw, ls], lf_s[rw, ls], bc_s[rw, ls]) for rw in crows for ls in lanes]
        prep = _hgrn_chunk_prepare(ins)
        yield
        states = [st_ref[h] for h in range(nhead)]
        for ci, rw in enumerate(crows):
            outs = _hgrn_chunk_apply(prep[ci * nhead:(ci + 1) * nhead], states)
            states = [st_new for _, st_new in outs]
            for (o, _), ls in zip(outs, lanes):
                o_s[rw, ls] = o
            yield
        for h in range(nhead):
            st_ref[h] = states[h]
        o = o_s[...]
        ms = _dot(o * o, gsum_ref[...]) * (1.0 / LANES)
        og = u_ref[0, rows, 3 * dim:4 * dim]
        yb_ref[0, rows, :] = (o * lax.rsqrt(ms + RMS_EPS) * nw_ref[...] * _sigmoid(og)).astype(BF16)
        yield


N_RWKV_IN = 16
N_HGRN_IN = 5
N_RWKV_SCRATCH = 11


def _recurrent_kernel(layer, *refs):
    rw_in = refs[:N_RWKV_IN]
    hg_in = refs[N_RWKV_IN:N_RWKV_IN + N_HGRN_IN]
    ya_ref, yb_ref = refs[N_RWKV_IN + N_HGRN_IN:N_RWKV_IN + N_HGRN_IN + 2]
    scratch = refs[N_RWKV_IN + N_HGRN_IN + 2:]
    rw_scratch, hg_scratch = scratch[:N_RWKV_SCRATCH], scratch[N_RWKV_SCRATCH:]

    @pl.when(pl.program_id(1) == 0)
    def _():
        hg_scratch[0][...] = jnp.zeros_like(hg_scratch[0])

    _rwkv_kernel(*rw_in, ya_ref, *rw_scratch,
                 background=_hgrn_steps(layer, *hg_in, yb_ref, *hg_scratch))


def _merge_kernel(alpha, x_ref, ya_ref, yb_ref, pg_ref, wa_ref, wb_ref, wo_ref, g_ref, b_ref, wr_ref,
                  bias_ref, upper_ref, lower_ref, x1_ref, route_ref, seg_ref, cnt_ref, lt_ref, carry_ref):
    tm, d = x_ref.shape

    def rows_stage(rs):
        ma = _dg(ya_ref[rs, :], wa_ref[...])
        mb = _dg(yb_ref[rs, :], wb_ref[...])
        yield
        merged = _sigmoid(pg_ref[rs, :d]) * ma + _sigmoid(pg_ref[rs, d:]) * mb
        yield
        h = alpha * x_ref[rs, :] + _dot(merged, wo_ref[...])
        yield
        x1 = _layer_norm(h, g_ref[...], b_ref[...])
        x1_ref[rs, :] = x1
        yield
        lt_ref[:, rs] = _dot3(wr_ref[...], x1, NT)
        yield

    groups = [rows_stage(slice(q * MERGE_ROWS, (q + 1) * MERGE_ROWS)) for q in range(tm // MERGE_ROWS)]
    _skewed(groups)
    _route_tile(lt_ref, bias_ref, upper_ref, lower_ref, route_ref, seg_ref, cnt_ref, carry_ref)


def _merge(xf, ya, yb, pgate, wa, wb, wo, g, bta, wr, bias_col, alpha):
    n, d = xf.shape
    tm = TM_MERGE
    ne = N_GROUPS * EXPERTS_PER_GROUP
    ii = jnp.arange(tm)
    upper = (ii[:, None] < ii[None, :]).astype(BF16)
    ee = jnp.arange(ne)
    lower = (ee[:, None] > ee[None, :]).astype(BF16)
    tile = lambda a_: pl.BlockSpec((tm, a_.shape[1]), lambda i: (i, 0))
    const = lambda a_: pl.BlockSpec(a_.shape, lambda i: (0, 0))
    return pl.pallas_call(
        functools.partial(_merge_kernel, alpha),
        grid=(n // tm,),
        in_specs=[tile(xf), tile(ya), tile(yb), tile(pgate), const(wa), const(wb), const(wo), const(g),
                  const(bta), const(wr), const(bias_col), const(upper), const(lower)],
        out_specs=[pl.BlockSpec((tm, d), lambda i: (i, 0)), pl.BlockSpec((8, tm), lambda i: (0, i)),
                   pl.BlockSpec((1, 8, LANES), lambda i: (i, 0, 0)), pl.BlockSpec((ne, LANES), lambda i: (0, 0))],
        out_shape=[jax.ShapeDtypeStruct((n, d), F32), jax.ShapeDtypeStruct((8, n), F32),
                   jax.ShapeDtypeStruct((n // tm, 8, LANES), F32), jax.ShapeDtypeStruct((ne, LANES), F32)],
        scratch_shapes=[pltpu.VMEM((LANES, tm), F32), pltpu.VMEM((ne, 1), F32)],
        compiler_params=_params(("arbitrary",)),
    )(xf, ya, yb, pgate, wa, wb, wo, g, bta, wr, bias_col, upper, lower)


ROUTER_EXPERT_ROW = 8


def _to_lanes(col, nl):
    ne = col.shape[0]
    diag = lax.broadcasted_iota(I32, (ne, nl), 0) == lax.broadcasted_iota(I32, (ne, nl), 1)
    return jnp.sum(jnp.where(diag, col, 0.0), axis=0, keepdims=True)


def _route_tile(lt_ref, bias_ref, upper_ref, lower_ref, route_ref, seg_ref, cnt_ref, carry_ref):
    @pl.when(pl.program_id(0) == 0)
    def _():
        carry_ref[...] = jnp.zeros_like(carry_ref)

    ne = N_GROUPS * EXPERTS_PER_GROUP
    lt = lt_ref[...] + bias_ref[...]
    nb = lt.shape[1]
    neg = -jnp.inf
    lg = lt[0:8]
    rg = lax.broadcasted_iota(I32, (8, nb), 0).astype(F32)
    lg = jnp.where(rg < N_GROUPS, lg, neg)
    mg = jnp.max(lg, axis=0, keepdims=True)
    gidx = jnp.min(jnp.where(lg == mg, rg, 1e9), axis=0, keepdims=True)
    pg_sel = 1.0 / jnp.sum(jnp.exp(lg - mg), axis=0, keepdims=True)

    le = lt[ROUTER_EXPERT_ROW:ROUTER_EXPERT_ROW + ne]
    re = lax.broadcasted_iota(I32, (ne, nb), 0).astype(F32)
    in_group = jnp.floor(re * (1.0 / EXPERTS_PER_GROUP)) == gidx
    l1 = jnp.where(in_group, le, neg)
    m1 = jnp.max(l1, axis=0, keepdims=True)
    i1 = jnp.min(jnp.where(l1 == m1, re, 1e9), axis=0, keepdims=True)
    l2 = jnp.where(re == i1, neg, l1)
    m2 = jnp.max(l2, axis=0, keepdims=True)
    i2 = jnp.min(jnp.where(l2 == m2, re, 1e9), axis=0, keepdims=True)
    e2 = jnp.exp(m2 - m1)
    w1 = pg_sel / (1.0 + e2)
    w2 = pg_sel * e2 / (1.0 + e2)

    sel1 = re == i1
    sel2 = re == i2
    onehot = jnp.where(sel1 | sel2, 1.0, 0.0)
    before = _dg(onehot.astype(BF16), upper_ref[...])
    cnt_t = jnp.sum(onehot, axis=1, keepdims=True)
    seg = jnp.floor((cnt_t + (SEG_ALIGN - 1)) * (1.0 / SEG_ALIGN)) * SEG_ALIGN
    lstart = _dg(lower_ref[...], jnp.broadcast_to(seg, (ne, LANES)).astype(BF16))[:, 0:1]
    tot = lstart + before
    lpos1 = jnp.sum(jnp.where(sel1, tot, 0.0), axis=0, keepdims=True)
    lpos2 = jnp.sum(jnp.where(sel2, tot, 0.0), axis=0, keepdims=True)
    grel = carry_ref[...]
    carry = grel + seg
    carry_ref[...] = carry
    cnt_ref[...] = jnp.broadcast_to(carry, cnt_ref.shape)
    zero = jnp.zeros_like(w1)
    route_ref[...] = jnp.concatenate([i1, i2, lpos1, lpos2, w1, w2, zero, zero], axis=0)
    nl = seg_ref.shape[2]
    zl = jnp.zeros((1, nl), F32)
    ltot = jnp.broadcast_to(jnp.sum(seg, axis=0, keepdims=True), (1, nl))
    seg_ref[0] = jnp.concatenate([_to_lanes(seg, nl), _to_lanes(lstart, nl), _to_lanes(grel, nl), ltot,
                                  zl, zl, zl, zl], axis=0)


TAB_LANES = LANES


def _finalize_kernel(tm, seg_ref, cnt_ref, lower_ref, segtab_ref, tab_ref):
    ne = cnt_ref.shape[0]
    cnt = cnt_ref[...]
    nb = jnp.floor((cnt + (tm - 1)) * (1.0 / tm))
    bstart = _dg(lower_ref[...], nb.astype(BF16))
    bend = bstart + nb
    pad_start = bstart[:, 0:1] * tm
    pad_start_row = _to_lanes(pad_start, seg_ref.shape[2])
    for t in range(seg_ref.shape[0]):
        seg = seg_ref[t]
        segtab_ref[t] = jnp.concatenate([seg[0:2], seg[2:3] + pad_start_row, seg[3:8]], axis=0).astype(I32)

    nl = tab_ref.shape[1]
    n_used = jnp.max(bend[:, 0:1], axis=0, keepdims=True)
    pad_lo = _to_lanes(pad_start + cnt[:, 0:1], nl)
    pad_hi = _to_lanes(bend[:, 0:1] * tm, nl)
    zero = jnp.zeros((1, nl), F32)
    tab_ref[...] = jnp.concatenate([_to_lanes(bstart[:, 0:1], nl), jnp.broadcast_to(n_used, (1, nl)), pad_lo,
                                    pad_hi, _to_lanes(nb[:, 0:1], nl), zero, zero, zero], axis=0).astype(I32)


def _finalize(seg, cnt, tm):
    ntile = seg.shape[0]
    ne = cnt.shape[0]
    ii = jnp.arange(ne)
    lower = (ii[:, None] > ii[None, :]).astype(BF16)
    return pl.pallas_call(
        functools.partial(_finalize_kernel, tm),
        grid=(1,),
        in_specs=[pl.BlockSpec(seg.shape, lambda i: (0, 0, 0)), pl.BlockSpec(cnt.shape, lambda i: (0, 0)),
                  pl.BlockSpec((ne, ne), lambda i: (0, 0))],
        out_specs=[pl.BlockSpec(seg.shape, lambda i: (0, 0, 0)), pl.BlockSpec((8, TAB_LANES), lambda i: (0, 0))],
        out_shape=[jax.ShapeDtypeStruct((ntile, 8, LANES), I32), jax.ShapeDtypeStruct((8, TAB_LANES), I32)],
        compiler_params=_params(("arbitrary",)),
    )(seg, cnt, lower)


def _for_each_piece(length, max_len, fn):
    size = SEG_ALIGN
    sizes = []
    while size <= max_len:
        sizes.append(size)
        size *= 2

    for size in reversed(sizes):
        @pl.when(jnp.bitwise_and(length, size) != 0)
        def _(size=size):
            fn(pl.multiple_of(jnp.bitwise_and(length, -2 * size), SEG_ALIGN), size)


def _sorted_rows(td):
    return 2 * td + N_GROUPS * EXPERTS_PER_GROUP * SEG_ALIGN


def _dispatch_kernel(tm, seglen_ref, lstart_ref, gstart_ref, ltot_ref, plo_ref, phi_ref, nu_ref,
                     x_ref, route_ref, xbuf_ref, sorted_ref, zblk, sem, zsem):
    i = pl.program_id(0)
    nsteps = pl.num_programs(0)
    td = x_ref.shape[0]
    ne = plo_ref.shape[0]
    nblk = xbuf_ref.shape[0] // tm
    nrow = sorted_ref.shape[1]
    buf = i % 2

    lpos = route_ref[2:4, :]
    xb = x_ref[...].astype(BF16)

    def sort_rows(lo, hi):
        r = (lax.broadcasted_iota(I32, (hi - lo, td), 0) + lo).astype(F32)
        onehot = jnp.where((r == lpos[0:1]) | (r == lpos[1:2]), 1.0, 0.0).astype(BF16)
        sorted_ref[buf, lo:hi, :] = _dg(onehot, xb).astype(BF16)

    main = nrow - ne * SEG_ALIGN // 2
    sort_rows(0, main)

    @pl.when(ltot_ref[i] > main)
    def _():
        sort_rows(main, nrow)

    def wait_tile(step, b):
        _for_each_piece(ltot_ref[step], nrow, lambda off, size: pltpu.make_async_copy(
            sorted_ref.at[b, pl.ds(0, size), :], xbuf_ref.at[pl.ds(0, size), :], sem.at[b]).wait())

    for e in range(ne):
        idx = i * ne + e
        ls = pl.multiple_of(lstart_ref[idx], SEG_ALIGN)
        gs = pl.multiple_of(gstart_ref[idx], SEG_ALIGN)
        _for_each_piece(seglen_ref[idx], td, lambda off, size: pltpu.make_async_copy(
            sorted_ref.at[buf, pl.ds(ls + off, size), :], xbuf_ref.at[pl.ds(gs + off, size), :],
            sem.at[buf]).start())

    @pl.when(i > 0)
    def _():
        wait_tile(i - 1, 1 - buf)

    @pl.when(i == nsteps - 1)
    def _():
        wait_tile(i, buf)

    def pad_fill(fn):
        for e in range(ne):
            lo = pl.multiple_of(plo_ref[e], SEG_ALIGN)
            _for_each_piece(phi_ref[e] - lo, tm // 2, lambda off, size: fn(pltpu.make_async_copy(
                zblk.at[pl.ds(0, size), :], xbuf_ref.at[pl.ds(lo + off, size), :], zsem)))

        def per_blk(b, carry):
            fn(pltpu.make_async_copy(zblk, xbuf_ref.at[pl.ds(pl.multiple_of(b * tm, tm), tm), :], zsem))
            return carry
        lax.fori_loop(nu_ref[0], nblk, per_blk, 0)

    @pl.when(i == 0)
    def _():
        zblk[...] = jnp.zeros_like(zblk)
        pad_fill(lambda cp: cp.start())
        pad_fill(lambda cp: cp.wait())


def _dispatch(seglen, lstart, gstart, ltot, pad_lo, pad_hi, n_used, x1, route, rows, tm):
    n, d = x1.shape
    td = TD_DISPATCH
    return pl.pallas_call(
        functools.partial(_dispatch_kernel, tm),
        grid_spec=pltpu.PrefetchScalarGridSpec(
            num_scalar_prefetch=7,
            grid=(n // td,),
            in_specs=[pl.BlockSpec((td, d), lambda i, *_: (i, 0)), pl.BlockSpec((8, td), lambda i, *_: (0, i))],
            out_specs=pl.BlockSpec(memory_space=pl.ANY),
            scratch_shapes=[pltpu.VMEM((2, _sorted_rows(td), d), BF16), pltpu.VMEM((tm, d), BF16),
                            pltpu.SemaphoreType.DMA((2,)), pltpu.SemaphoreType.DMA(())],
        ),
        out_shape=jax.ShapeDtypeStruct((rows, d), BF16),
        compiler_params=_params(("arbitrary",)),
    )(seglen, lstart, gstart, ltot, pad_lo, pad_hi, n_used, x1, route)


X_SLOTS = 3


def _expert_kernel(tm, bstart_ref, nb_ref, nu_ref, w1_ref, w3_ref, w2_ref, xbuf_ref, ybuf_ref,
                   wf1, wf3, wf2, w13b, w2b, xb, yb, semw, semx, semy):
    e = pl.program_id(0)
    ne = pl.num_programs(0)
    nb = nb_ref[e]
    b0 = bstart_ref[e]
    nblk = ybuf_ref.shape[0] // tm
    rows = lambda blk: pl.ds(pl.multiple_of(blk * tm, tm), tm)
    x_copy = lambda j, slot: pltpu.make_async_copy(xbuf_ref.at[rows(b0 + j), :], xb.at[slot], semx.at[slot])
    y_copy = lambda blk, slot: pltpu.make_async_copy(yb.at[slot], ybuf_ref.at[rows(blk), :], semy.at[slot])

    def w_copies(ex, slot):
        return [pltpu.make_async_copy(src.at[ex], dst.at[slot], semw.at[slot])
                for src, dst in ((w1_ref, wf1), (w3_ref, wf3), (w2_ref, wf2))]

    ws = e % 2

    @pl.when(e == 0)
    def _():
        for cp in w_copies(0, 0):
            cp.start()

    for j0 in range(X_SLOTS - 1):
        @pl.when(j0 < nb)
        def _(j0=j0):
            x_copy(j0, j0).start()

    for cp in w_copies(e, ws):
        cp.wait()
    de = w2b.shape[0]
    w13b[:, :de] = wf1[ws].astype(BF16)
    w13b[:, de:] = wf3[ws].astype(BF16)
    w2b[...] = wf2[ws].astype(BF16)

    @pl.when(e + 1 < ne)
    def _():
        for cp in w_copies(e + 1, 1 - ws):
            cp.start()

    def body(j, carry):
        slot = j % X_SLOTS
        yslot = j % 2
        x_copy(j, slot).wait()

        @pl.when(j + X_SLOTS - 1 < nb)
        def _():
            x_copy(j + X_SLOTS - 1, (j + X_SLOTS - 1) % X_SLOTS).start()

        @pl.when(j >= 2)
        def _():
            y_copy(b0 + j - 2, yslot).wait()

        x = xb[slot]
        h13 = _dg(x, w13b[...])
        h1 = h13[:, :de]
        h = (h1 * _sigmoid(h1)) * h13[:, de:]
        yb[yslot] = _dot(h, w2b[...]).astype(BF16)
        y_copy(b0 + j, yslot).start()
        return carry

    lax.fori_loop(0, nb, body, 0)

    @pl.when(nb >= 2)
    def _():
        y_copy(b0 + nb - 2, nb % 2).wait()

    @pl.when(nb >= 1)
    def _():
        y_copy(b0 + nb - 1, (nb - 1) % 2).wait()

    @pl.when(e == pl.num_programs(0) - 1)
    def _():
        yb[0] = jnp.zeros(yb.shape[1:], yb.dtype)

        def fill(fn):
            def per_blk(blk, carry):
                fn(y_copy(blk, 0))
                return carry
            lax.fori_loop(nu_ref[0], nblk, per_blk, 0)

        fill(lambda cp: cp.start())
        fill(lambda cp: cp.wait())


def _experts(bstart, nb, n_used, xbuf, w1, w3, w2):
    rows, d = xbuf.shape
    ne, _, de = w1.shape
    tm = TM_EXPERT
    return pl.pallas_call(
        functools.partial(_expert_kernel, tm),
        grid_spec=pltpu.PrefetchScalarGridSpec(
            num_scalar_prefetch=3,
            grid=(ne,),
            in_specs=[pl.BlockSpec(memory_space=pl.ANY)] * 4,
            out_specs=pl.BlockSpec(memory_space=pl.ANY),
            scratch_shapes=[pltpu.VMEM((2, d, de), F32), pltpu.VMEM((2, d, de), F32), pltpu.VMEM((2, de, d), F32),
                            pltpu.VMEM((d, 2 * de), BF16), pltpu.VMEM((de, d), BF16),
                            pltpu.VMEM((X_SLOTS, tm, d), BF16), pltpu.VMEM((2, tm, d), BF16),
                            pltpu.SemaphoreType.DMA((2,)), pltpu.SemaphoreType.DMA((X_SLOTS,)),
                            pltpu.SemaphoreType.DMA((2,))],
        ),
        out_shape=jax.ShapeDtypeStruct((rows, d), BF16),
        compiler_params=_params(("arbitrary",)),
    )(bstart, nb, n_used, w1, w3, w2, xbuf)


def _combine_kernel(alpha, seglen_ref, lstart_ref, gstart_ref, ltot_ref, x1_ref, rt_ref, p_ref, wpe_ref, wpg_ref,
                    g_ref, b_ref, ybuf_ref, out_ref, sorted_ref, sem):
    i = pl.program_id(0)
    nsteps = pl.num_programs(0)
    tc = x1_ref.shape[0]
    ne = N_GROUPS * EXPERTS_PER_GROUP
    nrow = sorted_ref.shape[1]

    def fetch(step, buf):
        for e in range(ne):
            idx = step * ne + e
            ls = pl.multiple_of(lstart_ref[idx], SEG_ALIGN)
            gs = pl.multiple_of(gstart_ref[idx], SEG_ALIGN)
            _for_each_piece(seglen_ref[idx], tc, lambda off, size: pltpu.make_async_copy(
                ybuf_ref.at[pl.ds(gs + off, size), :], sorted_ref.at[buf, pl.ds(ls + off, size), :],
                sem.at[buf]).start())

    @pl.when(i == 0)
    def _():
        sorted_ref[...] = jnp.zeros_like(sorted_ref)
        fetch(0, 0)

    @pl.when(i + 1 < nsteps)
    def _():
        fetch(i + 1, (i + 1) % 2)

    cur = i % 2
    _for_each_piece(ltot_ref[i], nrow, lambda off, size: pltpu.make_async_copy(
        ybuf_ref.at[pl.ds(0, size), :], sorted_ref.at[cur, pl.ds(0, size), :], sem.at[cur]).wait())

    rt = rt_ref[...]
    r = lax.broadcasted_iota(I32, (tc, nrow), 1).astype(F32)
    unsort = jnp.where(r == rt[:, 0:1], rt[:, 2:3], 0.0) + jnp.where(r == rt[:, 1:2], rt[:, 3:4], 0.0)
    ffn = _dg(unsort.astype(BF16), sorted_ref[cur])
    x2 = _layer_norm(alpha * x1_ref[...] + ffn, g_ref[...], b_ref[...])
    gate = _sigmoid(_dot(x2, wpg_ref[...]))
    out_ref[...] = x2 + gate * _dot(p_ref[...], wpe_ref[...])


def _combine(seglen, lstart, gstart, ltot, x1, route_t, pf, wpe, wpg, g, bta, ybuf, alpha):
    n, d = x1.shape
    tc = TD_DISPATCH
    tile = lambda a_: pl.BlockSpec((tc, a_.shape[1]), lambda i, *_: (i, 0))
    const = lambda a_: pl.BlockSpec(a_.shape, lambda i, *_: (0, 0))
    return pl.pallas_call(
        functools.partial(_combine_kernel, alpha),
        grid_spec=pltpu.PrefetchScalarGridSpec(
            num_scalar_prefetch=4,
            grid=(n // tc,),
            in_specs=[tile(x1), tile(route_t), tile(pf), const(wpe), const(wpg), const(g), const(bta),
                      pl.BlockSpec(memory_space=pl.ANY)],
            out_specs=pl.BlockSpec((tc, d), lambda i, *_: (i, 0)),
            scratch_shapes=[pltpu.VMEM((2, _sorted_rows(tc), d), BF16), pltpu.SemaphoreType.DMA((2,))],
        ),
        out_shape=jax.ShapeDtypeStruct((n, d), F32),
        compiler_params=_params(("arbitrary",)),
    )(seglen, lstart, gstart, ltot, x1, route_t, pf, wpe, wpg, g, bta, ybuf)


def _layer(x, p_i, w_in, rw_mu, rw_w0, rw_w_up, rw_a0, rw_a_up, rw_g_up, rw_k_k, rw_k_a, rw_r_k, rw_gn_w,
           rw_gn_b, w_a_out, hg_lb_logits, hg_norm_w, w_b_out, w_o, ln1_g, ln1_b, router_g_w, router_g_b,
           router_e_w, router_e_b, w1, w3, w2, ln2_g, ln2_b, w_pe, w_pg, alpha, layer):
    b, t, d = x.shape
    n = b * t
    rw_dim = rw_w0.shape[0]
    rw_cols = 3 * rw_dim + RW_DECAY_LORA + RW_A_LORA + RW_GATE_LORA
    hg_cols = 4 * hg_norm_w.shape[0]
    ne = N_GROUPS * EXPERTS_PER_GROUP
    row2 = lambda a_: a_.reshape(1, -1)
    xf = x.reshape(n, d)

    wb = w_in.astype(BF16)
    proj_rw, proj_hg, proj_gt = _project(xf, wb[:, :rw_cols], wb[:, rw_cols:rw_cols + hg_cols],
                                         wb[:, rw_cols + hg_cols:])
    ya, yb = _recurrent_branches(proj_rw.reshape(b, t, rw_cols), rw_mu, row2(rw_w0), rw_w_up, rw_a0, rw_a_up,
                                 rw_g_up, rw_k_k, rw_k_a, rw_r_k, rw_gn_w, rw_gn_b,
                                 proj_hg.reshape(b, t, hg_cols), hg_lb_logits, hg_norm_w, layer)

    wr = jnp.zeros((LANES, d), F32)
    wr = wr.at[:N_GROUPS].set(router_g_w.T).at[ROUTER_EXPERT_ROW:ROUTER_EXPERT_ROW + ne].set(router_e_w.T)
    bias = jnp.zeros((LANES,), F32)
    bias = bias.at[:N_GROUPS].set(router_g_b).at[ROUTER_EXPERT_ROW:ROUTER_EXPERT_ROW + ne].set(router_e_b)
    x1, route, seg, cnt = _merge(xf, ya.reshape(n, -1), yb.reshape(n, -1), proj_gt, w_a_out.astype(BF16),
                                 w_b_out.astype(BF16), w_o.astype(BF16), row2(ln1_g), row2(ln1_b), wr,
                                 bias.reshape(LANES, 1), alpha)

    tm = TM_EXPERT
    ntile = n // TD_DISPATCH
    nblk = -(-(2 * n + (SEG_ALIGN - 1) * ne * ntile) // tm) + ne
    assert TM_MERGE == TD_DISPATCH
    segtab, tab = _finalize(seg, cnt, tm)
    per_seg = lambda row: segtab[:, row, :ne].reshape(-1)
    seglen, lstart, gstart, ltot = per_seg(0), per_seg(1), per_seg(2), segtab[:, 3, 0]
    n_used = tab[1, :1]

    xbuf = _dispatch(seglen, lstart, gstart, ltot, tab[2, :ne], tab[3, :ne], n_used, x1, route, nblk * tm, tm)
    ybuf = _experts(tab[0, :ne], tab[4, :ne], n_used, xbuf, w1, w3, w2)
    out = _combine(seglen, lstart, gstart, ltot, x1, route[2:6].T, p_i.reshape(n, -1), w_pe.astype(BF16),
                   w_pg.astype(BF16), row2(ln2_g), row2(ln2_b), ybuf, alpha)
    return out.reshape(b, t, d)


def kernel(x, p, w_in, rw_mu, rw_w0, rw_w_up, rw_a0, rw_a_up, rw_g_up, rw_k_k, rw_k_a, rw_r_k, rw_gn_w, rw_gn_b,
           w_a_out, hg_lb_logits, hg_norm_w, w_b_out, w_o, ln1_g, ln1_b, router_g_w, router_g_b, router_e_w,
           router_e_b, w1, w3, w2, ln2_g, ln2_b, w_pe, w_pg):
    depth = w_in.shape[0]
    alpha = (2 * depth) ** 0.25
    for i in range(depth):
        x = _layer(x, p[i], w_in[i], rw_mu[i], rw_w0[i], rw_w_up[i], rw_a0[i], rw_a_up[i], rw_g_up[i], rw_k_k[i],
                   rw_k_a[i], rw_r_k[i].reshape(-1), rw_gn_w[i], rw_gn_b[i], w_a_out[i], hg_lb_logits,
                   hg_norm_w[i], w_b_out[i], w_o[i], ln1_g[i], ln1_b[i],
                   router_g_w[i], router_g_b[i], router_e_w[i], router_e_b[i], w1[i], w3[i], w2[i], ln2_g[i],
                   ln2_b[i], w_pe[i], w_pg[i], alpha, i)
    return x
```
